```python
import math
import jax
import jax.numpy as jnp
from jax import lax
import numpy as np

D_MODEL = 1024
BATCH = 16
SEQ = 256
DEPTH = 2
DEC_BATCH = 4
DEC_SEQ = 1024
PAST_LEN = 512

GRID_W = 64
EPS = 1e-6
GLA_HEADS = 4
GLA_DK = 64
GLA_DV = 128
GLA_WIDTH = GLA_HEADS * GLA_DV
GLA_GATE_RANK = 16
GLA_GATE_NORM = 16.0
GLA_CHUNK = 64
MLA_HEADS = 4
MLA_Q_LORA = 384
MLA_KV_LORA = 256
MLA_NOPE = 64
MLA_ROPE = 32
MLA_QK = MLA_NOPE + MLA_ROPE
MLA_DV = 128
MLA_WIDTH = MLA_HEADS * MLA_DV
ROPE_THETA = 10000.0
Q_BLOCK = 128
S5_WIDTH = 512
S5_GROUP = 16
S5_GROUPS = S5_WIDTH // S5_GROUP
S5_STATE = 64
DT_MIN = 1e-3
DT_MAX = 1e-1
N_BRANCH = 3
IN_SPLITS = (GLA_HEADS * GLA_DK, GLA_HEADS * GLA_DK, GLA_WIDTH, GLA_GATE_RANK, GLA_GATE_RANK, GLA_WIDTH,
             MLA_Q_LORA, MLA_KV_LORA, MLA_ROPE, MLA_WIDTH,
             S5_WIDTH, S5_WIDTH,
             N_BRANCH * D_MODEL)
D_IN = sum(IN_SPLITS)

kernel_name = 'hybrid_gla_mla_s5_diffusion_step'


def rms_norm(x, w):
    xf = x.astype(jnp.float32)
    y = xf * lax.rsqrt(jnp.mean(xf * xf, axis=-1, keepdims=True) + EPS)
    return (y * w.astype(jnp.float32)).astype(x.dtype)


def split_cols(z, sizes):
    return jnp.split(z, np.cumsum(np.array(sizes))[:-1].tolist(), axis=-1)


def axial_rope_tables(n_tok):
    rows = n_tok // GRID_W
    r = jnp.repeat(jnp.arange(rows, dtype=jnp.float32), GRID_W)
    col = jnp.tile(jnp.arange(GRID_W, dtype=jnp.float32), rows)
    n_freq = MLA_ROPE // 4
    inv = ROPE_THETA ** (-jnp.arange(n_freq, dtype=jnp.float32) / n_freq)
    ang = jnp.concatenate([r[:, None] * inv, col[:, None] * inv], axis=-1)
    return jnp.cos(ang), jnp.sin(ang)


def apply_rope_tail(x, rope):
    cos, sin = rope
    cos = cos[:, None, :].astype(x.dtype)
    sin = sin[:, None, :].astype(x.dtype)
    x_nope, x1, x2 = jnp.split(x, [MLA_NOPE, MLA_NOPE + MLA_ROPE // 2], axis=-1)
    return jnp.concatenate([x_nope, x1 * cos - x2 * sin, x1 * sin + x2 * cos], axis=-1)


def block_attention(q, k, v):
    bsz, heads, nq, dk = q.shape
    scale = dk ** -0.5
    qb = q.reshape(bsz, heads, nq // Q_BLOCK, Q_BLOCK, dk).transpose(2, 0, 1, 3, 4)

    def one_block(q_blk):
        s = jnp.einsum('bhqd,bhkd->bhqk', q_blk, k).astype(jnp.float32) * scale
        p = jax.nn.softmax(s, axis=-1).astype(v.dtype)
        return jnp.einsum('bhqk,bhkd->bhqd', p, v)

    o = lax.map(one_block, qb)
    return o.transpose(1, 2, 0, 3, 4).reshape(bsz, heads, nq, v.shape[-1])


def gla_chunk_scan(q, k, v, log_a, s0):
    bsz, heads, n, dk = q.shape
    nc = n // GLA_CHUNK

    def chunks(t):
        return t.reshape(bsz, heads, nc, GLA_CHUNK, t.shape[-1]).transpose(2, 0, 1, 3, 4)

    mask = jnp.tril(jnp.ones((GLA_CHUNK, GLA_CHUNK), dtype=bool))

    def step(s, inp):
        qi, ki, vi, ai = inp
        bcum = jnp.cumsum(ai, axis=-2)
        blast = bcum[..., -1:, :]
        qd = qi * jnp.exp(bcum).astype(qi.dtype)
        kd = ki * jnp.exp(-bcum).astype(ki.dtype)
        att = jnp.where(mask, jnp.einsum('bhcd,bhsd->bhcs', qd, kd), 0)
        o = jnp.einsum('bhcd,bhde->bhce', qd, s) + jnp.einsum('bhcs,bhse->bhce', att, vi)
        kr = ki * jnp.exp(blast - bcum).astype(ki.dtype)
        s_new = jnp.exp(blast[..., 0, :])[..., None].astype(s.dtype) * s + jnp.einsum('bhcd,bhce->bhde', kr, vi)
        return s_new, o

    s_fin, o = lax.scan(step, s0, (chunks(q), chunks(k), chunks(v), chunks(log_a)))
    return o.transpose(1, 2, 0, 3, 4).reshape(bsz, heads, n, v.shape[-1]), s_fin


def gla_mixer(q, k, v, a_low_f, a_low_b, w_a2, b_a, o_norm, ctx):
    bsz, n, _ = q.shape

    def heads(t, d):
        return t.reshape(bsz, n, GLA_HEADS, d).transpose(0, 2, 1, 3)

    qh = heads(q, GLA_DK) * (GLA_DK ** -0.5)
    kh = heads(k, GLA_DK)
    vh = heads(v, GLA_DV)
    log_af = heads(jax.nn.log_sigmoid((a_low_f @ w_a2[0] + b_a[0]).astype(jnp.float32)) / GLA_GATE_NORM, GLA_DK)
    log_ab = heads(jax.nn.log_sigmoid((a_low_b @ w_a2[1] + b_a[1]).astype(jnp.float32)) / GLA_GATE_NORM, GLA_DK)
    if ctx is None:
        s_f0 = jnp.zeros((bsz, GLA_HEADS, GLA_DK, GLA_DV), v.dtype)
        s_b0 = s_f0
    else:
        s_f0, s_b0 = ctx
    o_f, s_f = gla_chunk_scan(qh, kh, vh, log_af, s_f0)

    def flip(t):
        return jnp.flip(t, axis=2)

    o_b, s_b = gla_chunk_scan(flip(qh), flip(kh), flip(vh), flip(log_ab), s_b0)
    o = rms_norm((o_f + flip(o_b)).transpose(0, 2, 1, 3), o_norm)
    return o.reshape(bsz, n, GLA_WIDTH), jnp.stack([s_f, s_b], axis=1)


def mla_keys_values(ckv, k_rope, w_uk, w_uv, kh_norm, rope):
    bsz, n, _ = ckv.shape
    k_nope = (ckv @ w_uk).reshape(bsz, n, MLA_HEADS, MLA_NOPE)
    v = (ckv @ w_uv).reshape(bsz, n, MLA_HEADS, MLA_DV)
    k_pe = jnp.broadcast_to(k_rope[:, :, None, :], (bsz, n, MLA_HEADS, MLA_ROPE))
    k = rms_norm(jnp.concatenate([k_nope, k_pe], axis=-1), kh_norm)
    if rope is not None:
        k = apply_rope_tail(k, rope)
    return k.transpose(0, 2, 1, 3), v.transpose(0, 2, 1, 3)


def mla_mixer(cq_raw, ckv_raw, k_rope, q_norm, w_uq, kv_norm, w_uk, w_uv, qh_norm, kh_norm, rope, ctx):
    bsz, n, _ = cq_raw.shape
    q = (rms_norm(cq_raw, q_norm) @ w_uq).reshape(bsz, n, MLA_HEADS, MLA_QK)
    q = rms_norm(q, qh_norm)
    if rope is not None:
        q = apply_rope_tail(q, rope)
    ckv = rms_norm(ckv_raw, kv_norm)
    k, v = mla_keys_values(ckv, k_rope, w_uk, w_uv, kh_norm, rope)
    if ctx is not None:
        kc, vc = mla_keys_values(ctx[0], ctx[1], w_uk, w_uv, kh_norm, None)
        k = jnp.concatenate([kc, k], axis=2)
        v = jnp.concatenate([vc, v], axis=2)
    o = block_attention(q.transpose(0, 2, 1, 3), k, v)
    return o.transpose(0, 2, 1, 3).reshape(bsz, n, MLA_WIDTH), (ckv, k_rope)


def s5_discretise(a_re, a_im, log_dt):
    dt = jnp.exp(log_dt)[:, None]
    mag = jnp.exp(a_re * dt)
    ab_re = mag * jnp.cos(a_im * dt)
    ab_im = mag * jnp.sin(a_im * dt)
    den = a_re * a_re + a_im * a_im
    n_re = ab_re - 1.0
    coef_re = (n_re * a_re + ab_im * a_im) / den
    coef_im = (ab_im * a_re - n_re * a_im) / den
    return ab_re, ab_im, coef_re, coef_im


def complex_linear_scan(ab_re, ab_im, u_re, u_im, x0_re, x0_im):
    u_re = u_re.at[:, 0].add(ab_re * x0_re - ab_im * x0_im)
    u_im = u_im.at[:, 0].add(ab_re * x0_im + ab_im * x0_re)
    a_re = jnp.broadcast_to(ab_re, u_re.shape)
    a_im = jnp.broadcast_to(ab_im, u_im.shape)

    def combine(e1, e2):
        a1r, a1i, b1r, b1i = e1
        a2r, a2i, b2r, b2i = e2
        return (a1r * a2r - a1i * a2i, a1r * a2i + a1i * a2r,
                a2r * b1r - a2i * b1i + b2r, a2r * b1i + a2i * b1r + b2i)

    _, _, x_re, x_im = lax.associative_scan(combine, (a_re, a_im, u_re, u_im), axis=1)
    return x_re, x_im


def s5_mixer(u, a_re, a_im, log_dt, b_re, b_im, c_re, c_im, d_skip, w_glu, b_glu, ctx):
    bsz, n, _ = u.shape
    ug = u.reshape(bsz, n, S5_GROUPS, S5_GROUP)
    bu_re = jnp.einsum('blgp,gnp->blgn', ug, b_re)
    bu_im = jnp.einsum('blgp,gnp->blgn', ug, b_im)
    y = d_skip * u
    finals = []
    for d in range(2):
        ab_re, ab_im, cf_re, cf_im = s5_discretise(a_re[d], a_im[d], log_dt[d])
        ub_re = cf_re * bu_re - cf_im * bu_im
        ub_im = cf_re * bu_im + cf_im * bu_re
        if d == 1:
            ub_re = jnp.flip(ub_re, axis=1)
            ub_im = jnp.flip(ub_im, axis=1)
        if ctx is None:
            x0_re = jnp.zeros((bsz, S5_GROUPS, S5_STATE), u.dtype)
            x0_im = x0_re
        else:
            x0_re, x0_im = ctx[:, d, 0], ctx[:, d, 1]
        x_re, x_im = complex_linear_scan(ab_re, ab_im, ub_re, ub_im, x0_re, x0_im)
        finals.append(jnp.stack([x_re[:, -1], x_im[:, -1]], axis=1))
        if d == 1:
            x_re = jnp.flip(x_re, axis=1)
            x_im = jnp.flip(x_im, axis=1)
        y_ssm = jnp.einsum('blgn,gpn->blgp', x_re, c_re) - jnp.einsum('blgn,gpn->blgp', x_im, c_im)
        y = y + y_ssm.reshape(bsz, n, S5_WIDTH)
    z = jax.nn.gelu(y) @ w_glu + b_glu
    out = z[..., :S5_WIDTH] * jax.nn.sigmoid(z[..., S5_WIDTH:])
    return out, jnp.stack(finals, axis=1)


def trunk_layer(x, cond, rope, ctx, p):
    ada = jax.nn.silu(cond) @ p['w_ada'] + p['b_ada']
    shift, scale, gate = jnp.split(ada[:, None, :], 3, axis=-1)
    h = rms_norm(x, p['norm_w']) * (1.0 + scale) + shift
    (gq, gk, gv, gaf, gab, g_gate, mq, mkv, mkr, m_gate, su, s_gate, merge) = split_cols(h @ p['w_in'], IN_SPLITS)
    if ctx is None:
        ctx_mla, ctx_gla, ctx_s5 = None, None, None
    else:
        ctx_mla, ctx_gla, ctx_s5 = ctx
    o_a, st_gla = gla_mixer(gq, gk, gv, gaf, gab, p['gla_w_a2'], p['gla_b_a'], p['gla_o_norm'], ctx_gla)
    o_b, st_mla = mla_mixer(mq, mkv, mkr, p['mla_q_norm'], p['mla_w_uq'], p['mla_kv_norm'], p['mla_w_uk'],
                            p['mla_w_uv'], p['mla_qh_norm'], p['mla_kh_norm'], rope, ctx_mla)
    o_c, st_s5 = s5_mixer(su, p['s5_a_re'], p['s5_a_im'], p['s5_log_dt'], p['s5_b_re'], p['s5_b_im'],
                          p['s5_c_re'], p['s5_c_im'], p['s5_d'], p['s5_w_glu'], p['s5_b_glu'], ctx_s5)
    g_a, g_b, g_c = jnp.split(jax.nn.sigmoid(merge), 3, axis=-1)
    mixed = (g_a * ((o_a * jax.nn.silu(g_gate)) @ p['w_bo_gla'])
             + g_b * ((o_b * jax.nn.silu(m_gate)) @ p['w_bo_mla'])
             + g_c * ((o_c * jax.nn.silu(s_gate)) @ p['w_bo_s5']))
    y = x + gate * (mixed @ p['w_out'])
    return y, (st_mla, st_gla, st_s5)


def setup_inputs(seed: int = 0) -> dict:
    key = jax.random.key(seed)
    keys = jax.random.split(key, 64)
    counter = [0]

    def nxt():
        counter[0] += 1
        return keys[counter[0] - 1]

    def nrm(shape, scale):
        return scale * jax.random.normal(nxt(), shape, jnp.float32)

    L = DEPTH
    n_idx = jnp.arange(S5_STATE, dtype=jnp.float32)
    log_dt = math.log(DT_MIN) + jax.random.uniform(nxt(), (L, 2, S5_GROUPS), jnp.float32) * (math.log(DT_MAX) - math.log(DT_MIN))
    return {
        'x_prompt': nrm((BATCH, SEQ, D_MODEL), 1.0),
        'x_sample': nrm((DEC_BATCH, DEC_SEQ, D_MODEL), 1.0),
        'c': nrm((DEC_BATCH, D_MODEL), 1.0),
        'c_ctx': nrm((D_MODEL,), 1.0),
        'cache_mla_ckv': nrm((DEC_BATCH, L, PAST_LEN, MLA_KV_LORA), 1.0),
        'cache_mla_krope': nrm((DEC_BATCH, L, PAST_LEN, MLA_ROPE), 1.0),
        'state_gla': nrm((DEC_BATCH, L, 2, GLA_HEADS, GLA_DK, GLA_DV), 0.5),
        'state_s5': nrm((DEC_BATCH, L, 2, 2, S5_GROUPS, S5_STATE), 0.1),
        'norm_w': 1.0 + nrm((L, D_MODEL), 0.02),
        'w_ada': nrm((L, D_MODEL, 3 * D_MODEL), 0.5 * D_MODEL ** -0.5),
        'b_ada': nrm((L, 3 * D_MODEL), 0.02),
        'w_in': nrm((L, D_MODEL, D_IN), D_MODEL ** -0.5),
        'gla_w_a2': nrm((L, 2, GLA_GATE_RANK, GLA_HEADS * GLA_DK), GLA_GATE_RANK ** -0.5),
        'gla_b_a': nrm((L, 2, GLA_HEADS * GLA_DK), 0.1),
        'gla_o_norm': 1.0 + nrm((L, GLA_DV), 0.02),
        'mla_q_norm': 1.0 + nrm((L, MLA_Q_LORA), 0.02),
        'mla_w_uq': nrm((L, MLA_Q_LORA, MLA_HEADS * MLA_QK), MLA_Q_LORA ** -0.5),
        'mla_kv_norm': 1.0 + nrm((L, MLA_KV_LORA), 0.02),
        'mla_w_uk': nrm((L, MLA_KV_LORA, MLA_HEADS * MLA_NOPE), MLA_KV_LORA ** -0.5),
        'mla_w_uv': nrm((L, MLA_KV_LORA, MLA_HEADS * MLA_DV), MLA_KV_LORA ** -0.5),
        'mla_qh_norm': 1.0 + nrm((L, MLA_QK), 0.02),
        'mla_kh_norm': 1.0 + nrm((L, MLA_QK), 0.02),
        's5_a_re': -0.5 * jnp.exp(nrm((L, 2, S5_GROUPS, S5_STATE), 0.05)),
        's5_a_im': jnp.pi * n_idx + nrm((L, 2, S5_GROUPS, S5_STATE), 0.01),
        's5_log_dt': log_dt,
        's5_b_re': nrm((L, S5_GROUPS, S5_STATE, S5_GROUP), (2 * S5_GROUP) ** -0.5),
        's5_b_im': nrm((L, S5_GROUPS, S5_STATE, S5_GROUP), (2 * S5_GROUP) ** -0.5),
        's5_c_re': nrm((L, S5_GROUPS, S5_GROUP, S5_STATE), (2 * S5_STATE) ** -0.5),
        's5_c_im': nrm((L, S5_GROUPS, S5_GROUP, S5_STATE), (2 * S5_STATE) ** -0.5),
        's5_d': nrm((L, S5_WIDTH), 1.0),
        's5_w_glu': nrm((L, S5_WIDTH, 2 * S5_WIDTH), S5_WIDTH ** -0.5),
        's5_b_glu': nrm((L, 2 * S5_WIDTH), 0.02),
        'w_bo_gla': nrm((L, GLA_WIDTH, D_MODEL), GLA_WIDTH ** -0.5),
        'w_bo_mla': nrm((L, MLA_WIDTH, D_MODEL), MLA_WIDTH ** -0.5),
        'w_bo_s5': nrm((L, S5_WIDTH, D_MODEL), S5_WIDTH ** -0.5),
        'w_out': nrm((L, D_MODEL, D_MODEL), D_MODEL ** -0.5),
    }


def reference(x_prompt, x_sample, c, c_ctx, cache_mla_ckv, cache_mla_krope, state_gla, state_s5,
              norm_w, w_ada, b_ada, w_in, gla_w_a2, gla_b_a, gla_o_norm,
              mla_q_norm, mla_w_uq, mla_kv_norm, mla_w_uk, mla_w_uv, mla_qh_norm, mla_kh_norm,
              s5_a_re, s5_a_im, s5_log_dt, s5_b_re, s5_b_im, s5_c_re, s5_c_im, s5_d, s5_w_glu, s5_b_glu,
              w_bo_gla, w_bo_mla, w_bo_s5, w_out):
    cond_ctx = jnp.broadcast_to(c_ctx[None, :], (x_prompt.shape[0], D_MODEL))
    rope = axial_rope_tables(x_sample.shape[1])
    hp = x_prompt
    hs = x_sample
    ckv_l, krope_l, gla_l, s5_l = [], [], [], []
    for l in range(DEPTH):
        p = {
            'norm_w': norm_w[l], 'w_ada': w_ada[l], 'b_ada': b_ada[l], 'w_in': w_in[l],
            'gla_w_a2': gla_w_a2[l], 'gla_b_a': gla_b_a[l], 'gla_o_norm': gla_o_norm[l],
            'mla_q_norm': mla_q_norm[l], 'mla_w_uq': mla_w_uq[l], 'mla_kv_norm': mla_kv_norm[l],
            'mla_w_uk': mla_w_uk[l], 'mla_w_uv': mla_w_uv[l], 'mla_qh_norm': mla_qh_norm[l],
            'mla_kh_norm': mla_kh_norm[l],
            's5_a_re': s5_a_re[l], 's5_a_im': s5_a_im[l], 's5_log_dt': s5_log_dt[l],
            's5_b_re': s5_b_re[l], 's5_b_im': s5_b_im[l], 's5_c_re': s5_c_re[l], 's5_c_im': s5_c_im[l],
            's5_d': s5_d[l], 's5_w_glu': s5_w_glu[l], 's5_b_glu': s5_b_glu[l],
            'w_bo_gla': w_bo_gla[l], 'w_bo_mla': w_bo_mla[l], 'w_bo_s5': w_bo_s5[l], 'w_out': w_out[l],
        }
        hp, (st_mla, st_gla, st_s5) = trunk_layer(hp, cond_ctx, None, None, p)
        ckv_l.append(st_mla[0])
        krope_l.append(st_mla[1])
        gla_l.append(st_gla)
        s5_l.append(st_s5)
        ctx = ((cache_mla_ckv[:, l], cache_mla_krope[:, l]),
               (state_gla[:, l, 0], state_gla[:, l, 1]),
               state_s5[:, l])
        hs, _ = trunk_layer(hs, c, rope, ctx, p)
    new_mla_ckv = jnp.stack(ckv_l, axis=1)
    new_mla_krope = jnp.stack(krope_l, axis=1)
    new_state_gla = jnp.stack(gla_l, axis=1)
    new_state_s5 = jnp.stack(s5_l, axis=1)
    return (hp, hs, new_mla_ckv, new_mla_krope, new_state_gla, new_state_s5)
```

```python
import functools

import jax
import jax.numpy as jnp
import numpy as np
from jax import lax
from jax.experimental import pallas as pl
from jax.experimental.pallas import tpu as pltpu

F32 = jnp.float32
BF16 = jnp.bfloat16

EPS = 1e-6
D_MODEL = 1024
DEPTH = 2
GRID_W = 64
ROPE_THETA = 10000.0
GLA_HEADS = 4
GLA_DK = 64
GLA_DV = 128
GLA_RANK = 16
GLA_GATE_NORM = 16.0
GLA_QK = GLA_HEADS * GLA_DK
GLA_WIDTH = GLA_HEADS * GLA_DV
MLA_HEADS = 4
MLA_Q_LORA = 384
MLA_KV_LORA = 256
MLA_NOPE = 64
MLA_ROPE = 32
MLA_QK = MLA_NOPE + MLA_ROPE
MLA_DV = 128
MLA_WIDTH = MLA_HEADS * MLA_DV
S5_WIDTH = 512
S5_GROUP = 16
S5_GROUPS = 32
S5_STATE = 64
S5_NSTATE = S5_GROUPS * S5_STATE

LANE = 128
HEAD_PAD = LANE
CHUNK = 64
S5_TILES = S5_WIDTH // LANE
S5_TILE_STATE = S5_NSTATE // S5_TILES
ROW_TILE = 512
Q_TILE = 256
VMEM_LIMIT = 56 * 1024 * 1024

ZG_Q, ZG_K, ZG_V, ZG_A, ZG_GATE, ZG_W = 0, 256, 512, 1024, 1152, 1664
ZM_Q, ZM_KV, ZM_KR, ZM_GATE, ZM_W = 0, 384, 640, 768, 1280
ZS_U, ZS_GATE, ZS_W = 0, 512, 1024


def _dot(a, b):
    return jnp.dot(a.astype(BF16), b.astype(BF16), preferred_element_type=F32)


def _dot_nt(a, b):
    return lax.dot_general(a.astype(BF16), b.astype(BF16), (((1,), (1,)), ((), ())),
                           preferred_element_type=F32)


def _dot_tn(a, b):
    return lax.dot_general(a.astype(BF16), b.astype(BF16), (((0,), (0,)), ((), ())),
                           preferred_element_type=F32)


def _split_bf16(x, parts):
    out = []
    r = x
    for _ in range(parts):
        p = r.astype(BF16)
        out.append(p)
        r = r - p.astype(F32)
    return out


def _params(sem):
    return pltpu.CompilerParams(dimension_semantics=sem, vmem_limit_bytes=VMEM_LIMIT)


def _ada_kernel(c_ref, w_ref, b_ref, o_ref):
    s = jax.nn.silu(c_ref[...])
    o_ref[...] = _dot(s, w_ref[...]) + b_ref[...]


def _ada(cond8, w_ada, b_ada):
    tn = 1024
    return pl.pallas_call(
        _ada_kernel,
        grid=(DEPTH, 3 * D_MODEL // tn),
        in_specs=[
            pl.BlockSpec((8, D_MODEL), lambda l, n: (0, 0)),
            pl.BlockSpec((None, D_MODEL, tn), lambda l, n: (l, 0, n)),
            pl.BlockSpec((None, 1, tn), lambda l, n: (l, 0, n)),
        ],
        out_specs=pl.BlockSpec((None, 8, tn), lambda l, n: (l, 0, n)),
        out_shape=jax.ShapeDtypeStruct((DEPTH, 8, 3 * D_MODEL), F32),
        compiler_params=_params(("parallel", "parallel")),
        name="ada",
    )(cond8, w_ada, b_ada.reshape(DEPTH, 1, 3 * D_MODEL))


def _mod_rmsnorm(x, nw, mod):
    ms = jnp.mean(x * x, axis=-1, keepdims=True)
    y = x * lax.rsqrt(ms + EPS) * nw
    return y * (1.0 + mod[:, D_MODEL:2 * D_MODEL]) + mod[:, 0:D_MODEL]


def _in_proj_kernel(x_ref, mod_ref, nw_ref, wg_ref, wm_ref, ws_ref, zg_ref, zm_ref, zs_ref):
    h = _mod_rmsnorm(x_ref[...], nw_ref[...], mod_ref[0]).astype(BF16)
    zg_ref[...] = jnp.dot(h, wg_ref[...], preferred_element_type=F32)
    zm_ref[...] = jnp.dot(h, wm_ref[...], preferred_element_type=F32)
    zs_ref[...] = jnp.dot(h, ws_ref[...], preferred_element_type=F32)


def _in_proj(x2, mod, mod_idx, nw, wg, wm, ws):
    n = x2.shape[0]
    tm = ROW_TILE
    const = lambda i: (0, 0)
    return pl.pallas_call(
        _in_proj_kernel,
        grid=(n // tm,),
        in_specs=[
            pl.BlockSpec((tm, D_MODEL), lambda i: (i, 0)),
            pl.BlockSpec((1, 1, 3 * D_MODEL), lambda i: (mod_idx(i), 0, 0)),
            pl.BlockSpec((1, D_MODEL), const),
            pl.BlockSpec((D_MODEL, ZG_W), const),
            pl.BlockSpec((D_MODEL, ZM_W), const),
            pl.BlockSpec((D_MODEL, ZS_W), const),
        ],
        out_specs=[
            pl.BlockSpec((tm, ZG_W), lambda i: (i, 0)),
            pl.BlockSpec((tm, ZM_W), lambda i: (i, 0)),
            pl.BlockSpec((tm, ZS_W), lambda i: (i, 0)),
        ],
        out_shape=[
            jax.ShapeDtypeStruct((n, ZG_W), F32),
            jax.ShapeDtypeStruct((n, ZM_W), F32),
            jax.ShapeDtypeStruct((n, ZS_W), F32),
        ],
        compiler_params=_params(("parallel",)),
        name="in_proj",
    )(x2, mod, nw, wg, wm, ws)


def _gla_kernel(zg_ref, s0_ref, waf_ref, wab_ref, ba_ref, onorm_ref, o_ref, sfin_ref,
                la_s, of_s, ob_s, st_s, *, nc):
    a_blk = zg_ref[:, ZG_A:ZG_A + LANE]
    inv_norm = 1.0 / GLA_GATE_NORM
    la_s[0] = jax.nn.log_sigmoid(_dot(a_blk, waf_ref[...]) + ba_ref[0:1, :]) * inv_norm
    la_s[1] = jax.nn.log_sigmoid(_dot(a_blk, wab_ref[...]) + ba_ref[1:2, :]) * inv_norm
    st_s[...] = s0_ref[...]

    row = lax.broadcasted_iota(jnp.int32, (CHUNK, CHUNK), 0)
    col = lax.broadcasted_iota(jnp.int32, (CHUNK, CHUNK), 1)
    masks = (row >= col, row <= col)
    ones = jnp.ones((CHUNK, LANE), BF16)
    qscale = GLA_DK ** -0.5

    def chunk(c, carry):
        for d in (0, 1):
            cc = c if d == 0 else nc - 1 - c
            r0 = pl.multiple_of(cc * CHUNK, CHUNK)
            rows = pl.ds(r0, CHUNK)
            a = la_s[d, rows, :]
            a_hi, a_lo = _split_bf16(a, 2)
            tri = masks[d].astype(BF16)
            cum = (jnp.dot(tri, a_hi, preferred_element_type=F32)
                   + jnp.dot(tri, a_lo, preferred_element_type=F32))
            dcol = _dot_tn(a_hi, ones) + _dot_tn(a_lo, ones)
            blast = cum[CHUNK - 1:CHUNK, :] if d == 0 else cum[0:1, :]
            q = zg_ref[rows, ZG_Q:ZG_Q + GLA_QK] * qscale
            k = zg_ref[rows, ZG_K:ZG_K + GLA_QK]
            v = zg_ref[rows, ZG_V:ZG_V + GLA_WIDTH].astype(BF16)
            qd = (q * jnp.exp(cum)).astype(BF16)
            kd = (k * jnp.exp(-cum)).astype(BF16)
            kr = (k * jnp.exp(blast - cum)).astype(BF16)
            s_old = st_s[d]
            s_bf = s_old.astype(BF16)
            outs, kvs = [], []
            for h in range(GLA_HEADS):
                ks = slice(h * GLA_DK, (h + 1) * GLA_DK)
                vs = slice(h * GLA_DV, (h + 1) * GLA_DV)
                att = jnp.where(masks[d], _dot_nt(qd[:, ks], kd[:, ks]), 0.0)
                outs.append(_dot(qd[:, ks], s_bf[ks, :]) + _dot(att, v[:, vs]))
                kvs.append(_dot_tn(kr[:, ks], v[:, vs]))
            st_s[d] = jnp.exp(dcol) * s_old + jnp.concatenate(kvs, axis=0)
            o_dir = of_s if d == 0 else ob_s
            o_dir[rows, :] = jnp.concatenate(outs, axis=1)
        return carry

    lax.fori_loop(0, nc, chunk, 0)
    sfin_ref[...] = st_s[...]
    o = of_s[...] + ob_s[...]
    gate = zg_ref[:, ZG_GATE:ZG_GATE + GLA_WIDTH]
    onorm = onorm_ref[...]
    for h in range(GLA_HEADS):
        vs = slice(h * GLA_DV, (h + 1) * GLA_DV)
        oh = o[:, vs]
        ms = jnp.mean(oh * oh, axis=-1, keepdims=True)
        o_ref[:, vs] = oh * lax.rsqrt(ms + EPS) * onorm * jax.nn.silu(gate[:, vs])


def _gla(zg, s0, s0_idx, waf, wab, ba, onorm, bsz, seq):
    const = lambda b: (0, 0)
    nsd = len(s0.shape)
    s0_block = (None,) * (nsd - 3) + (2, GLA_QK, GLA_DV)
    return pl.pallas_call(
        functools.partial(_gla_kernel, nc=seq // CHUNK),
        grid=(bsz,),
        in_specs=[
            pl.BlockSpec((seq, ZG_W), lambda b: (b, 0)),
            pl.BlockSpec(s0_block, s0_idx),
            pl.BlockSpec((LANE, GLA_QK), const),
            pl.BlockSpec((LANE, GLA_QK), const),
            pl.BlockSpec((2, GLA_QK), const),
            pl.BlockSpec((1, GLA_DV), const),
        ],
        out_specs=[
            pl.BlockSpec((seq, GLA_WIDTH), lambda b: (b, 0)),
            pl.BlockSpec((None, 2, GLA_QK, GLA_DV), lambda b: (b, 0, 0, 0)),
        ],
        out_shape=[
            jax.ShapeDtypeStruct((bsz * seq, GLA_WIDTH), F32),
            jax.ShapeDtypeStruct((bsz, 2, GLA_QK, GLA_DV), F32),
        ],
        scratch_shapes=[
            pltpu.VMEM((2, seq, GLA_QK), F32),
            pltpu.VMEM((seq, GLA_WIDTH), F32),
            pltpu.VMEM((seq, GLA_WIDTH), F32),
            pltpu.VMEM((2, GLA_QK, GLA_DV), F32),
        ],
        compiler_params=_params(("parallel",)),
        name="gla",
    )(zg, s0, waf, wab, ba, onorm)


def _rms(x, w):
    ms = jnp.mean(x * x, axis=-1, keepdims=True)
    return x * lax.rsqrt(ms + EPS) * w


def _head_norm(x, w, rope):
    outs = []
    for h in range(MLA_HEADS):
        xh = x[:, h * HEAD_PAD:(h + 1) * HEAD_PAD]
        ms = jnp.sum(xh * xh, axis=-1, keepdims=True) * (1.0 / MLA_QK)
        y = xh * lax.rsqrt(ms + EPS) * w
        if rope is not None:
            c, s_up, s_dn = rope
            half = MLA_ROPE // 2
            y = y * c + pltpu.roll(y, half, 1) * s_up + pltpu.roll(y, HEAD_PAD - half, 1) * s_dn
        outs.append(y)
    return outs


def _place_rope_key(kr, e):
    return sum(jnp.dot(p, e, preferred_element_type=F32) for p in _split_bf16(kr, 3))


def _mla_kernel(*refs, seq, n_ctx, use_rope):
    it = iter(refs)
    zm_ref = next(it)
    if n_ctx:
        cckv_ref, ckr_ref = next(it), next(it)
    qn_ref, wuq_ref, kvn_ref, wuk_ref, wuv_ref, qhn_ref, khn_ref, e_ref = (next(it) for _ in range(8))
    rope_ref = next(it) if use_rope else None
    o_ref, ckv_ref = next(it), next(it)
    q_s, k_s, v_s = next(it), next(it), next(it)

    rope = None
    if use_rope:
        rope = (rope_ref[0], rope_ref[1], rope_ref[2])
    e = e_ref[...]
    wuk = wuk_ref[...]
    wuv = wuv_ref[...]
    khn = khn_ref[...]

    cq = _rms(zm_ref[:, ZM_Q:ZM_Q + MLA_Q_LORA], qn_ref[...])
    qh = _head_norm(_dot(cq, wuq_ref[...]), qhn_ref[...], rope)
    qscale = MLA_QK ** -0.5
    for h in range(MLA_HEADS):
        q_s[:, h * HEAD_PAD:(h + 1) * HEAD_PAD] = (qh[h] * qscale).astype(BF16)

    ckv = _rms(zm_ref[:, ZM_KV:ZM_KV + MLA_KV_LORA], kvn_ref[...])
    ckv_ref[...] = ckv
    kh = _head_norm(_dot(ckv, wuk) + _place_rope_key(zm_ref[:, ZM_KR:ZM_KR + LANE], e), khn, rope)
    for h in range(MLA_HEADS):
        k_s[n_ctx:n_ctx + seq, h * HEAD_PAD:(h + 1) * HEAD_PAD] = kh[h].astype(BF16)
    v_s[n_ctx:n_ctx + seq, :] = _dot(ckv, wuv).astype(BF16)
    if n_ctx:
        cc = cckv_ref[...]
        kch = _head_norm(_dot(cc, wuk) + _place_rope_key(ckr_ref[...], e), khn, None)
        for h in range(MLA_HEADS):
            k_s[0:n_ctx, h * HEAD_PAD:(h + 1) * HEAD_PAD] = kch[h].astype(BF16)
        v_s[0:n_ctx, :] = _dot(cc, wuv).astype(BF16)

    def q_block(i, carry):
        rows = pl.ds(pl.multiple_of(i * Q_TILE, Q_TILE), Q_TILE)
        gate = zm_ref[rows, ZM_GATE:ZM_GATE + MLA_WIDTH]
        for h in range(MLA_HEADS):
            hs = slice(h * HEAD_PAD, (h + 1) * HEAD_PAD)
            s = lax.dot_general(q_s[rows, hs], k_s[:, hs], (((1,), (1,)), ((), ())),
                                preferred_element_type=F32)
            m = jnp.max(s, axis=-1, keepdims=True)
            p = jnp.exp(s - m)
            l = jnp.sum(p, axis=-1, keepdims=True)
            o = jnp.dot(p.astype(BF16), v_s[:, hs], preferred_element_type=F32) / l
            o_ref[rows, hs] = o * jax.nn.silu(gate[:, hs])
        return carry

    lax.fori_loop(0, seq // Q_TILE, q_block, 0)


def _mla(zm, ctx, w, rope_tab, bsz, seq):
    const2 = lambda b: (0, 0)
    n_ctx = 0 if ctx is None else ctx[0].shape[-2]
    in_specs = [pl.BlockSpec((seq, ZM_W), lambda b: (b, 0))]
    args = [zm]
    if ctx is not None:
        cckv, ckr, layer = ctx
        in_specs += [
            pl.BlockSpec((None, None, n_ctx, MLA_KV_LORA), lambda b: (b, layer, 0, 0)),
            pl.BlockSpec((None, None, n_ctx, LANE), lambda b: (b, layer, 0, 0)),
        ]
        args += [cckv, ckr]
    in_specs += [
        pl.BlockSpec((1, MLA_Q_LORA), const2),
        pl.BlockSpec((MLA_Q_LORA, MLA_HEADS * HEAD_PAD), const2),
        pl.BlockSpec((1, MLA_KV_LORA), const2),
        pl.BlockSpec((MLA_KV_LORA, MLA_HEADS * HEAD_PAD), const2),
        pl.BlockSpec((MLA_KV_LORA, MLA_WIDTH), const2),
        pl.BlockSpec((1, HEAD_PAD), const2),
        pl.BlockSpec((1, HEAD_PAD), const2),
        pl.BlockSpec((LANE, MLA_HEADS * HEAD_PAD), const2),
    ]
    args += list(w)
    if rope_tab is not None:
        in_specs.append(pl.BlockSpec((3, seq, HEAD_PAD), lambda b: (0, 0, 0)))
        args.append(rope_tab)
    return pl.pallas_call(
        functools.partial(_mla_kernel, seq=seq, n_ctx=n_ctx, use_rope=rope_tab is not None),
        grid=(bsz,),
        in_specs=in_specs,
        out_specs=[
            pl.BlockSpec((seq, MLA_WIDTH), lambda b: (b, 0)),
            pl.BlockSpec((seq, MLA_KV_LORA), lambda b: (b, 0)),
        ],
        out_shape=[
            jax.ShapeDtypeStruct((bsz * seq, MLA_WIDTH), F32),
            jax.ShapeDtypeStruct((bsz * seq, MLA_KV_LORA), F32),
        ],
        scratch_shapes=[
            pltpu.VMEM((seq, MLA_HEADS * HEAD_PAD), BF16),
            pltpu.VMEM((n_ctx + seq, MLA_HEADS * HEAD_PAD), BF16),
            pltpu.VMEM((n_ctx + seq, MLA_WIDTH), BF16),
        ],
        compiler_params=_params(("parallel",)),
        name="mla",
    )(*args)


def _s5_prep_kernel(are_ref, aim_ref, ldt_ref, bre_ref, bim_ref, bw_ref, tabp_ref, tabs_ref):
    a_re = are_ref[...]
    a_im = aim_ref[...]
    dt = jnp.exp(ldt_ref[...])
    lam = a_re * dt
    th = a_im * dt
    mag = jnp.exp(lam)
    ab_re = mag * jnp.cos(th)
    ab_im = mag * jnp.sin(th)
    den = a_re * a_re + a_im * a_im
    n_re = ab_re - 1.0
    cf_re = (n_re * a_re + ab_im * a_im) / den
    cf_im = (ab_im * a_re - n_re * a_im) / den
    for j in range(S5_TILES):
        cs = slice(j * S5_TILE_STATE, (j + 1) * S5_TILE_STATE)
        br = bre_ref[j]
        bi = bim_ref[j]
        bw_ref[0, j] = (br * cf_re[:, cs] - bi * cf_im[:, cs]).astype(BF16)
        bw_ref[1, j] = (br * cf_im[:, cs] + bi * cf_re[:, cs]).astype(BF16)
    t = lax.broadcasted_iota(jnp.int32, (CHUNK, S5_NSTATE), 0).astype(F32)
    pm = jnp.exp(t * lam)
    qm = jnp.exp(-(t * lam))
    ang = t * th
    cs_, sn_ = jnp.cos(ang), jnp.sin(ang)
    tabp_ref[0] = pm * cs_
    tabp_ref[1] = pm * sn_
    tabp_ref[2] = qm * cs_
    tabp_ref[3] = -(qm * sn_)
    mag_c = jnp.exp(float(CHUNK) * lam)
    tabs_ref[0:1, :] = ab_re
    tabs_ref[1:2, :] = ab_im
    tabs_ref[2:3, :] = mag_c * jnp.cos(float(CHUNK) * th)
    tabs_ref[3:4, :] = mag_c * jnp.sin(float(CHUNK) * th)


def _s5_prep(a_re, a_im, ldt, blk_re, blk_im):
    vec = pl.BlockSpec((None, None, 1, S5_NSTATE), lambda l, d: (l, d, 0, 0))
    blk = pl.BlockSpec((None, S5_TILES, LANE, S5_TILE_STATE), lambda l, d: (l, 0, 0, 0))
    return pl.pallas_call(
        _s5_prep_kernel,
        grid=(DEPTH, 2),
        in_specs=[vec, vec, vec, blk, blk],
        out_specs=[
            pl.BlockSpec((None, None, 2, S5_TILES, LANE, S5_TILE_STATE), lambda l, d: (l, d, 0, 0, 0, 0)),
            pl.BlockSpec((None, None, 4, CHUNK, S5_NSTATE), lambda l, d: (l, d, 0, 0, 0)),
            pl.BlockSpec((None, None, 4, S5_NSTATE), lambda l, d: (l, d, 0, 0)),
        ],
        out_shape=[
            jax.ShapeDtypeStruct((DEPTH, 2, 2, S5_TILES, LANE, S5_TILE_STATE), BF16),
            jax.ShapeDtypeStruct((DEPTH, 2, 4, CHUNK, S5_NSTATE), F32),
            jax.ShapeDtypeStruct((DEPTH, 2, 4, S5_NSTATE), F32),
        ],
        compiler_params=_params(("parallel", "parallel")),
        name="s5_prep",
    )(a_re, a_im, ldt, blk_re, blk_im)


def _cmul(ar, ai, br, bi):
    return ar * br - ai * bi, ar * bi + ai * br


def _s5_kernel(u_ref, gate_ref, x0_ref, bw_ref, tabp_ref, tabs_ref, cw_ref, dsk_ref, wglu_ref,
               bglu_ref, o_ref, fs_ref, ub_s, y_s, *, nc):
    j = pl.program_id(1)
    u = u_ref[...]
    ub16 = u.astype(BF16)
    for d in (0, 1):
        for ri in (0, 1):
            ub_s[d, ri] = jnp.dot(ub16, bw_ref[d, ri], preferred_element_type=F32)

    row = lax.broadcasted_iota(jnp.int32, (CHUNK, CHUNK), 0)
    col = lax.broadcasted_iota(jnp.int32, (CHUNK, CHUNK), 1)
    tril = (row >= col).astype(BF16)
    triu = (row <= col).astype(BF16)

    def chunk(c, carry):
        xf_re, xf_im, xb_re, xb_im = carry
        rows = pl.ds(pl.multiple_of(c * CHUNK, CHUNK), CHUNK)
        w_re, w_im = _cmul(tabp_ref[0, 2], tabp_ref[0, 3], ub_s[0, 0, rows, :], ub_s[0, 1, rows, :])
        k_re, k_im = _cmul(tabs_ref[0, 0:1, :], tabs_ref[0, 1:2, :], xf_re, xf_im)
        c_re = _dot(tril, w_re) + k_re
        c_im = _dot(tril, w_im) + k_im
        x_re, x_im = _cmul(tabp_ref[0, 0], tabp_ref[0, 1], c_re, c_im)
        ub_s[0, 0, rows, :] = x_re
        ub_s[0, 1, rows, :] = x_im
        xf_re, xf_im = x_re[CHUNK - 1:CHUNK, :], x_im[CHUNK - 1:CHUNK, :]
        rows = pl.ds(pl.multiple_of((nc - 1 - c) * CHUNK, CHUNK), CHUNK)
        w_re, w_im = _cmul(tabp_ref[1, 0], tabp_ref[1, 1], ub_s[1, 0, rows, :], ub_s[1, 1, rows, :])
        k_re, k_im = _cmul(tabs_ref[1, 2:3, :], tabs_ref[1, 3:4, :], xb_re, xb_im)
        c_re = _dot(triu, w_re) + k_re
        c_im = _dot(triu, w_im) + k_im
        x_re, x_im = _cmul(tabp_ref[1, 2], tabp_ref[1, 3], c_re, c_im)
        ub_s[1, 0, rows, :] = x_re
        ub_s[1, 1, rows, :] = x_im
        xb_re, xb_im = x_re[0:1, :], x_im[0:1, :]
        return xf_re, xf_im, xb_re, xb_im

    init = (x0_ref[0, 0:1, :], x0_ref[0, 1:2, :], x0_ref[1, 0:1, :], x0_ref[1, 1:2, :])
    xf_re, xf_im, xb_re, xb_im = lax.fori_loop(0, nc, chunk, init)
    fs_ref[0, 0:1, :] = xf_re
    fs_ref[0, 1:2, :] = xf_im
    fs_ref[1, 0:1, :] = xb_re
    fs_ref[1, 1:2, :] = xb_im

    x_re = ub_s[0, 0] + ub_s[1, 0]
    x_im = ub_s[0, 1] + ub_s[1, 1]
    y_s[j] = dsk_ref[...] * u + _dot(x_re, cw_ref[0]) - _dot(x_im, cw_ref[1])

    @pl.when(j == S5_TILES - 1)
    def _():
        y = jnp.concatenate([y_s[t] for t in range(S5_TILES)], axis=1)
        z = _dot(jax.nn.gelu(y), wglu_ref[...]) + bglu_ref[...]
        out = z[:, :S5_WIDTH] * jax.nn.sigmoid(z[:, S5_WIDTH:])
        o_ref[...] = out * jax.nn.silu(gate_ref[...])


def _s5(zs, x0, x0_idx, layer, bw, tabp, tabs, cw, dsk, wglu, bglu, bsz, seq):
    nx = len(x0.shape)
    x0_block = (None,) * (nx - 3) + (2, 2, S5_TILE_STATE)
    return pl.pallas_call(
        functools.partial(_s5_kernel, nc=seq // CHUNK),
        grid=(bsz, S5_TILES),
        in_specs=[
            pl.BlockSpec((seq, LANE), lambda b, j: (b, j)),
            pl.BlockSpec((seq, S5_WIDTH), lambda b, j: (b, ZS_GATE // S5_WIDTH)),
            pl.BlockSpec(x0_block, x0_idx),
            pl.BlockSpec((None, 2, 2, None, LANE, S5_TILE_STATE), lambda b, j: (layer, 0, 0, j, 0, 0)),
            pl.BlockSpec((None, 2, 4, CHUNK, S5_TILE_STATE), lambda b, j: (layer, 0, 0, 0, j)),
            pl.BlockSpec((None, 2, 4, S5_TILE_STATE), lambda b, j: (layer, 0, 0, j)),
            pl.BlockSpec((None, None, 2, S5_TILE_STATE, LANE), lambda b, j: (layer, j, 0, 0, 0)),
            pl.BlockSpec((None, 1, LANE), lambda b, j: (layer, 0, j)),
            pl.BlockSpec((None, S5_WIDTH, 2 * S5_WIDTH), lambda b, j: (layer, 0, 0)),
            pl.BlockSpec((None, 1, 2 * S5_WIDTH), lambda b, j: (layer, 0, 0)),
        ],
        out_specs=[
            pl.BlockSpec((seq, S5_WIDTH), lambda b, j: (b, 0)),
            pl.BlockSpec((None, 2, 2, S5_TILE_STATE), lambda b, j: (b, 0, 0, j)),
        ],
        out_shape=[
            jax.ShapeDtypeStruct((bsz * seq, S5_WIDTH), F32),
            jax.ShapeDtypeStruct((bsz, 2, 2, S5_NSTATE), F32),
        ],
        scratch_shapes=[
            pltpu.VMEM((2, 2, seq, S5_TILE_STATE), F32),
            pltpu.VMEM((S5_TILES, seq, LANE), F32),
        ],
        compiler_params=_params(("parallel", "arbitrary")),
        name="s5",
    )(zs, zs, x0, bw, tabp, tabs, cw, dsk, wglu, bglu)


def _merge_kernel(x_ref, mod_ref, nw_ref, oa_ref, ob_ref, oc_ref, wmg_ref, wa_ref, wb_ref, wc_ref,
                  wout_ref, y_ref):
    x = x_ref[...]
    mod = mod_ref[0]
    h = _mod_rmsnorm(x, nw_ref[...], mod).astype(BF16)
    g = jax.nn.sigmoid(jnp.dot(h, wmg_ref[...], preferred_element_type=F32))
    mixed = (g[:, 0:D_MODEL] * _dot(oa_ref[...], wa_ref[...])
             + g[:, D_MODEL:2 * D_MODEL] * _dot(ob_ref[...], wb_ref[...])
             + g[:, 2 * D_MODEL:] * _dot(oc_ref[...], wc_ref[...]))
    y_ref[...] = x + mod[:, 2 * D_MODEL:] * _dot(mixed, wout_ref[...])


def _merge(x2, mod, mod_idx, nw, oa, ob, oc, wmg, wa, wb, wc, wout):
    n = x2.shape[0]
    tm = ROW_TILE
    const = lambda i: (0, 0)
    rows = lambda w: pl.BlockSpec((tm, w), lambda i: (i, 0))
    return pl.pallas_call(
        _merge_kernel,
        grid=(n // tm,),
        in_specs=[
            rows(D_MODEL),
            pl.BlockSpec((1, 1, 3 * D_MODEL), lambda i: (mod_idx(i), 0, 0)),
            pl.BlockSpec((1, D_MODEL), const),
            rows(GLA_WIDTH), rows(MLA_WIDTH), rows(S5_WIDTH),
            pl.BlockSpec((D_MODEL, 3 * D_MODEL), const),
            pl.BlockSpec((GLA_WIDTH, D_MODEL), const),
            pl.BlockSpec((MLA_WIDTH, D_MODEL), const),
            pl.BlockSpec((S5_WIDTH, D_MODEL), const),
            pl.BlockSpec((D_MODEL, D_MODEL), const),
        ],
        out_specs=rows(D_MODEL),
        out_shape=jax.ShapeDtypeStruct((n, D_MODEL), F32),
        compiler_params=_params(("parallel",)),
        name="merge",
    )(x2, mod, nw, oa, ob, oc, wmg, wa, wb, wc, wout)


def _pack_w_in(w):
    sizes = (GLA_QK, GLA_QK, GLA_WIDTH, GLA_RANK, GLA_RANK, GLA_WIDTH,
             MLA_Q_LORA, MLA_KV_LORA, MLA_ROPE, MLA_WIDTH, S5_WIDTH, S5_WIDTH, 3 * D_MODEL)
    offs = np.cumsum((0,) + sizes)
    gq, gk, gv, gaf, gab, gg, mq, mkv, mkr, mg, su, sg, mrg = (w[:, offs[i]:offs[i + 1]] for i in range(13))
    z96 = jnp.zeros((D_MODEL, LANE - 2 * GLA_RANK), w.dtype)
    wg = jnp.concatenate([gq, gk, gv, gaf, gab, z96, gg], axis=1)
    wm = jnp.concatenate([mq, mkv, mkr, z96, mg], axis=1)
    ws = jnp.concatenate([su, sg], axis=1)
    return wg.astype(BF16), wm.astype(BF16), ws.astype(BF16), mrg.astype(BF16)


def _pad_heads(w, width):
    lead = w.shape[:-1]
    w = w.reshape(lead + (MLA_HEADS, width))
    w = jnp.pad(w, [(0, 0)] * len(lead) + [(0, 0), (0, HEAD_PAD - width)])
    return w.reshape(lead + (MLA_HEADS * HEAD_PAD,))


def _rope_tables(n_tok):
    rows = n_tok // GRID_W
    r = jnp.repeat(jnp.arange(rows, dtype=F32), GRID_W)
    col = jnp.tile(jnp.arange(GRID_W, dtype=F32), rows)
    n_freq = MLA_ROPE // 4
    inv = ROPE_THETA ** (-jnp.arange(n_freq, dtype=F32) / n_freq)
    ang = jnp.concatenate([r[:, None] * inv, col[:, None] * inv], axis=-1)
    cos, sin = jnp.cos(ang), jnp.sin(ang)
    half = MLA_ROPE // 2
    z = lambda w: jnp.zeros((n_tok, w), F32)
    tail = HEAD_PAD - MLA_QK
    c = jnp.concatenate([jnp.ones((n_tok, MLA_NOPE), F32), cos, cos, z(tail)], axis=1)
    s_up = jnp.concatenate([z(MLA_NOPE + half), sin, z(tail)], axis=1)
    s_dn = jnp.concatenate([z(MLA_NOPE), -sin, z(half + tail)], axis=1)
    return jnp.stack([c, s_up, s_dn])


def _s5_block_diag_b(b):
    lyr = b.shape[0]
    b = b.reshape(lyr, S5_TILES, 8, S5_STATE, S5_GROUP).transpose(0, 1, 2, 4, 3)
    eye = jnp.eye(8, dtype=b.dtype)
    blk = b[:, :, :, :, None, :] * eye[None, None, :, None, :, None]
    return blk.reshape(lyr, S5_TILES, LANE, S5_TILE_STATE)


def _s5_block_diag_c(c):
    lyr = c.shape[0]
    c = c.reshape(lyr, S5_TILES, 8, S5_GROUP, S5_STATE).transpose(0, 1, 2, 4, 3)
    eye = jnp.eye(8, dtype=c.dtype)
    blk = c[:, :, :, :, None, :] * eye[None, None, :, None, :, None]
    return blk.reshape(lyr, S5_TILES, S5_TILE_STATE, LANE)


def kernel(x_prompt, x_sample, c, c_ctx, cache_mla_ckv, cache_mla_krope, state_gla, state_s5,
           norm_w, w_ada, b_ada, w_in, gla_w_a2, gla_b_a, gla_o_norm,
           mla_q_norm, mla_w_uq, mla_kv_norm, mla_w_uk, mla_w_uv, mla_qh_norm, mla_kh_norm,
           s5_a_re, s5_a_im, s5_log_dt, s5_b_re, s5_b_im, s5_c_re, s5_c_im, s5_d, s5_w_glu, s5_b_glu,
           w_bo_gla, w_bo_mla, w_bo_s5, w_out):
    bsz, seq, _ = x_prompt.shape
    dbsz, dseq, _ = x_sample.shape
    past = cache_mla_ckv.shape[2]
    ctx_row = 8 - 1
    assert dbsz <= ctx_row and (bsz * seq) % ROW_TILE == 0 and dseq % ROW_TILE == 0

    cond8 = jnp.zeros((8, D_MODEL), F32).at[0:dbsz].set(c).at[ctx_row].set(c_ctx)
    ada = _ada(cond8, w_ada, b_ada)

    vec = lambda a: a.reshape(DEPTH, 2, 1, S5_NSTATE)
    ldt = jnp.repeat(s5_log_dt[..., None], S5_STATE, axis=-1)
    bw, tabp, tabs = _s5_prep(vec(s5_a_re), vec(s5_a_im), vec(ldt),
                              _s5_block_diag_b(s5_b_re), _s5_block_diag_b(s5_b_im))
    cw = jnp.stack([_s5_block_diag_c(s5_c_re), _s5_block_diag_c(s5_c_im)], axis=2).astype(BF16)
    dsk = s5_d.reshape(DEPTH, 1, S5_WIDTH)
    wglu = s5_w_glu.astype(BF16)
    bglu = s5_b_glu.reshape(DEPTH, 1, 2 * S5_WIDTH)

    wuq = _pad_heads(mla_w_uq, MLA_QK).astype(BF16)
    wuk = _pad_heads(mla_w_uk, MLA_NOPE).astype(BF16)
    wuv = mla_w_uv.astype(BF16)
    qhn = jnp.pad(mla_qh_norm, ((0, 0), (0, HEAD_PAD - MLA_QK)))
    khn = jnp.pad(mla_kh_norm, ((0, 0), (0, HEAD_PAD - MLA_QK)))
    e_np = np.zeros((LANE, MLA_HEADS * HEAD_PAD), np.float32)
    for h in range(MLA_HEADS):
        for i in range(MLA_ROPE):
            e_np[i, h * HEAD_PAD + MLA_NOPE + i] = 1.0
    e_place = jnp.asarray(e_np, BF16)
    rope_tab = _rope_tables(dseq)
    ckr_pad = jnp.pad(cache_mla_krope, ((0, 0), (0, 0), (0, 0), (0, LANE - MLA_ROPE)))

    zrow = lambda n: jnp.zeros((DEPTH, n, GLA_QK), F32)
    waf = jnp.concatenate([gla_w_a2[:, 0], zrow(LANE - GLA_RANK)], axis=1).astype(BF16)
    wab = jnp.concatenate([zrow(GLA_RANK), gla_w_a2[:, 1], zrow(LANE - 2 * GLA_RANK)], axis=1).astype(BF16)
    sgla = state_gla.reshape(dbsz, DEPTH, 2, GLA_QK, GLA_DV)
    ss5 = state_s5.reshape(dbsz, DEPTH, 2, 2, S5_NSTATE)
    zero_gla = jnp.zeros((2, GLA_QK, GLA_DV), F32)
    zero_s5 = jnp.zeros((2, 2, S5_NSTATE), F32)

    hp = x_prompt.reshape(bsz * seq, D_MODEL)
    hs = x_sample.reshape(dbsz * dseq, D_MODEL)
    ckv_l, krope_l, gla_l, s5_l = [], [], [], []
    for l in range(DEPTH):
        wg, wm, ws, wmg = _pack_w_in(w_in[l])
        mod = ada[l].reshape(8, 1, 3 * D_MODEL)
        nw = norm_w[l].reshape(1, D_MODEL)
        mla_w = (mla_q_norm[l].reshape(1, -1), wuq[l], mla_kv_norm[l].reshape(1, -1), wuk[l], wuv[l],
                 qhn[l].reshape(1, -1), khn[l].reshape(1, -1), e_place)
        wbo = (w_bo_gla[l].astype(BF16), w_bo_mla[l].astype(BF16), w_bo_s5[l].astype(BF16))
        wout = w_out[l].astype(BF16)
        onorm = gla_o_norm[l].reshape(1, GLA_DV)

        def layer(x2, nb, n, mod_idx, ctx):
            zg, zm, zs = _in_proj(x2, mod, mod_idx, nw, wg, wm, ws)
            if ctx:
                s0, s0_idx = sgla, (lambda b: (b, l, 0, 0, 0))
                x0, x0_idx = ss5, (lambda b, j: (b, l, 0, 0, j))
                mctx, rt = (cache_mla_ckv, ckr_pad, l), rope_tab
            else:
                s0, s0_idx = zero_gla, (lambda b: (0, 0, 0))
                x0, x0_idx = zero_s5, (lambda b, j: (0, 0, j))
                mctx, rt = None, None
            oa, st_gla = _gla(zg, s0, s0_idx, waf[l], wab[l], gla_b_a[l], onorm, nb, n)
            ob, ckv = _mla(zm, mctx, mla_w, rt, nb, n)
            oc, st_s5 = _s5(zs, x0, x0_idx, l, bw, tabp, tabs, cw, dsk, wglu, bglu, nb, n)
            y = _merge(x2, mod, mod_idx, nw, oa, ob, oc, wmg, *wbo, wout)
            return y, zm, ckv, st_gla, st_s5

        hp, zm_p, ckv_p, st_gla_p, st_s5_p = layer(hp, bsz, seq, lambda i: ctx_row, False)
        ckv_l.append(ckv_p.reshape(bsz, seq, MLA_KV_LORA))
        krope_l.append(zm_p[:, ZM_KR:ZM_KR + MLA_ROPE].reshape(bsz, seq, MLA_ROPE))
        gla_l.append(st_gla_p.reshape(bsz, 2, GLA_HEADS, GLA_DK, GLA_DV))
        s5_l.append(st_s5_p.reshape(bsz, 2, 2, S5_GROUPS, S5_STATE))
        blocks_per_seq = dseq // ROW_TILE
        hs = layer(hs, dbsz, dseq, lambda i: i // blocks_per_seq, True)[0]

    return (hp.reshape(bsz, seq, D_MODEL), hs.reshape(dbsz, dseq, D_MODEL),
            jnp.stack(ckv_l, axis=1), jnp.stack(krope_l, axis=1),
            jnp.stack(gla_l, axis=1), jnp.stack(s5_l, axis=1))
```

```python
import functools

import jax
import jax.numpy as jnp
import numpy as np
from jax import lax
from jax.experimental import pallas as pl
from jax.experimental.pallas import tpu as pltpu

F32 = jnp.float32
BF16 = jnp.bfloat16

EPS = 1e-6
D_MODEL = 1024
DEPTH = 2
GRID_W = 64
ROPE_THETA = 10000.0
GLA_HEADS = 4
GLA_DK = 64
GLA_DV = 128
GLA_RANK = 16
GLA_GATE_NORM = 16.0
GLA_QK = GLA_HEADS * GLA_DK
GLA_WIDTH = GLA_HEADS * GLA_DV
MLA_HEADS = 4
MLA_Q_LORA = 384
MLA_KV_LORA = 256
MLA_NOPE = 64
MLA_ROPE = 32
MLA_QK = MLA_NOPE + MLA_ROPE
MLA_DV = 128
MLA_WIDTH = MLA_HEADS * MLA_DV
S5_WIDTH = 512
S5_GROUP = 16
S5_GROUPS = 32
S5_STATE = 64
S5_NSTATE = S5_GROUPS * S5_STATE

LANE = 128
HEAD_PAD = LANE
CHUNK = 64
S5_TILES = S5_WIDTH // LANE
S5_TILE_STATE = S5_NSTATE // S5_TILES
ROW_TILE = 512
Q_TILE = 256
VMEM_LIMIT = 56 * 1024 * 1024

ZG_Q, ZG_K, ZG_V, ZG_A, ZG_GATE, ZG_W = 0, 256, 512, 1024, 1152, 1664
ZM_Q, ZM_KV, ZM_KR, ZM_GATE, ZM_W = 0, 384, 640, 768, 1280
ZS_U, ZS_GATE, ZS_W = 0, 512, 1024


def _dot(a, b):
    return jnp.dot(a.astype(BF16), b.astype(BF16), preferred_element_type=F32)


def _dot_nt(a, b):
    return lax.dot_general(a.astype(BF16), b.astype(BF16), (((1,), (1,)), ((), ())),
                           preferred_element_type=F32)


def _dot_tn(a, b):
    return lax.dot_general(a.astype(BF16), b.astype(BF16), (((0,), (0,)), ((), ())),
                           preferred_element_type=F32)


def _split_bf16(x, parts):
    out = []
    r = x
    for _ in range(parts):
        p = r.astype(BF16)
        out.append(p)
        r = r - p.astype(F32)
    return out


def _params(sem):
    return pltpu.CompilerParams(dimension_semantics=sem, vmem_limit_bytes=VMEM_LIMIT)


def _ada_kernel(c_ref, w_ref, b_ref, o_ref):
    s = jax.nn.silu(c_ref[...])
    o_ref[...] = _dot(s, w_ref[...]) + b_ref[...]


def _ada(cond8, w_ada, b_ada):
    tn = 1024
    return pl.pallas_call(
        _ada_kernel,
        grid=(DEPTH, 3 * D_MODEL // tn),
        in_specs=[
            pl.BlockSpec((8, D_MODEL), lambda l, n: (0, 0)),
            pl.BlockSpec((None, D_MODEL, tn), lambda l, n: (l, 0, n)),
            pl.BlockSpec((None, 1, tn), lambda l, n: (l, 0, n)),
        ],
        out_specs=pl.BlockSpec((None, 8, tn), lambda l, n: (l, 0, n)),
        out_shape=jax.ShapeDtypeStruct((DEPTH, 8, 3 * D_MODEL), F32),
        compiler_params=_params(("parallel", "parallel")),
        name="ada",
    )(cond8, w_ada, b_ada.reshape(DEPTH, 1, 3 * D_MODEL))


def _mod_rmsnorm(x, nw, mod):
    ms = jnp.mean(x * x, axis=-1, keepdims=True)
    y = x * lax.rsqrt(ms + EPS) * nw
    return y * (1.0 + mod[:, D_MODEL:2 * D_MODEL]) + mod[:, 0:D_MODEL]


def _in_proj_kernel(x_ref, mod_ref, nw_ref, wg_ref, wm_ref, ws_ref, zg_ref, zm_ref, zs_ref):
    h = _mod_rmsnorm(x_ref[...], nw_ref[...], mod_ref[0]).astype(BF16)
    zg_ref[...] = jnp.dot(h, wg_ref[...], preferred_element_type=F32)
    zm_ref[...] = jnp.dot(h, wm_ref[...], preferred_element_type=F32)
    zs_ref[...] = jnp.dot(h, ws_ref[...], preferred_element_type=F32)


def _in_proj(x2, mod, mod_idx, nw, wg, wm, ws):
    n = x2.shape[0]
    tm = ROW_TILE
    const = lambda i: (0, 0)
    return pl.pallas_call(
        _in_proj_kernel,
        grid=(n // tm,),
        in_specs=[
            pl.BlockSpec((tm, D_MODEL), lambda i: (i, 0)),
            pl.BlockSpec((1, 1, 3 * D_MODEL), lambda i: (mod_idx(i), 0, 0)),
            pl.BlockSpec((1, D_MODEL), const),
            pl.BlockSpec((D_MODEL, ZG_W), const),
            pl.BlockSpec((D_MODEL, ZM_W), const),
            pl.BlockSpec((D_MODEL, ZS_W), const),
        ],
        out_specs=[
            pl.BlockSpec((tm, ZG_W), lambda i: (i, 0)),
            pl.BlockSpec((tm, ZM_W), lambda i: (i, 0)),
            pl.BlockSpec((tm, ZS_W), lambda i: (i, 0)),
        ],
        out_shape=[
            jax.ShapeDtypeStruct((n, ZG_W), F32),
            jax.ShapeDtypeStruct((n, ZM_W), F32),
            jax.ShapeDtypeStruct((n, ZS_W), F32),
        ],
        compiler_params=_params(("parallel",)),
        name="in_proj",
    )(x2, mod, nw, wg, wm, ws)


def _gla_kernel(zg_ref, s0_ref, waf_ref, wab_ref, ba_ref, onorm_ref, o_ref, sfin_ref,
                la_s, of_s, ob_s, st_s, *, nc):
    a_blk = zg_ref[:, ZG_A:ZG_A + LANE]
    inv_norm = 1.0 / GLA_GATE_NORM
    la_s[0] = jax.nn.log_sigmoid(_dot(a_blk, waf_ref[...]) + ba_ref[0:1, :]) * inv_norm
    la_s[1] = jax.nn.log_sigmoid(_dot(a_blk, wab_ref[...]) + ba_ref[1:2, :]) * inv_norm
    st_s[...] = s0_ref[...]

    row = lax.broadcasted_iota(jnp.int32, (CHUNK, CHUNK), 0)
    col = lax.broadcasted_iota(jnp.int32, (CHUNK, CHUNK), 1)
    masks = (row >= col, row <= col)
    ones = jnp.ones((CHUNK, LANE), BF16)
    qscale = GLA_DK ** -0.5

    def chunk(c, carry):
        for d in (0, 1):
            cc = c if d == 0 else nc - 1 - c
            r0 = pl.multiple_of(cc * CHUNK, CHUNK)
            rows = pl.ds(r0, CHUNK)
            a = la_s[d, rows, :]
            a_hi, a_lo = _split_bf16(a, 2)
            tri = masks[d].astype(BF16)
            cum = (jnp.dot(tri, a_hi, preferred_element_type=F32)
                   + jnp.dot(tri, a_lo, preferred_element_type=F32))
            dcol = _dot_tn(a_hi, ones) + _dot_tn(a_lo, ones)
            blast = cum[CHUNK - 1:CHUNK, :] if d == 0 else cum[0:1, :]
            q = zg_ref[rows, ZG_Q:ZG_Q + GLA_QK] * qscale
            k = zg_ref[rows, ZG_K:ZG_K + GLA_QK]
            v = zg_ref[rows, ZG_V:ZG_V + GLA_WIDTH].astype(BF16)
            qd = (q * jnp.exp(cum)).astype(BF16)
            kd = (k * jnp.exp(-cum)).astype(BF16)
            kr = (k * jnp.exp(blast - cum)).astype(BF16)
            s_old = st_s[d]
            s_bf = s_old.astype(BF16)
            outs, kvs = [], []
            for h in range(GLA_HEADS):
                ks = slice(h * GLA_DK, (h + 1) * GLA_DK)
                vs = slice(h * GLA_DV, (h + 1) * GLA_DV)
                att = jnp.where(masks[d], _dot_nt(qd[:, ks], kd[:, ks]), 0.0)
                outs.append(_dot(qd[:, ks], s_bf[ks, :]) + _dot(att, v[:, vs]))
                kvs.append(_dot_tn(kr[:, ks], v[:, vs]))
            st_s[d] = jnp.exp(dcol) * s_old + jnp.concatenate(kvs, axis=0)
            o_dir = of_s if d == 0 else ob_s
            o_dir[rows, :] = jnp.concatenate(outs, axis=1)
        return carry

    lax.fori_loop(0, nc, chunk, 0)
    sfin_ref[...] = st_s[...]
    o = of_s[...] + ob_s[...]
    gate = zg_ref[:, ZG_GATE:ZG_GATE + GLA_WIDTH]
    onorm = onorm_ref[...]
    for h in range(GLA_HEADS):
        vs = slice(h * GLA_DV, (h + 1) * GLA_DV)
        oh = o[:, vs]
        ms = jnp.mean(oh * oh, axis=-1, keepdims=True)
        o_ref[:, vs] = oh * lax.rsqrt(ms + EPS) * onorm * jax.nn.silu(gate[:, vs])


def _gla(zg, s0, s0_idx, waf, wab, ba, onorm, bsz, seq):
    const = lambda b: (0, 0)
    nsd = len(s0.shape)
    s0_block = (None,) * (nsd - 3) + (2, GLA_QK, GLA_DV)
    return pl.pallas_call(
        functools.partial(_gla_kernel, nc=seq // CHUNK),
        grid=(bsz,),
        in_specs=[
            pl.BlockSpec((seq, ZG_W), lambda b: (b, 0)),
            pl.BlockSpec(s0_block, s0_idx),
            pl.BlockSpec((LANE, GLA_QK), const),
            pl.BlockSpec((LANE, GLA_QK), const),
            pl.BlockSpec((2, GLA_QK), const),
            pl.BlockSpec((1, GLA_DV), const),
        ],
        out_specs=[
            pl.BlockSpec((seq, GLA_WIDTH), lambda b: (b, 0)),
            pl.BlockSpec((None, 2, GLA_QK, GLA_DV), lambda b: (b, 0, 0, 0)),
        ],
        out_shape=[
            jax.ShapeDtypeStruct((bsz * seq, GLA_WIDTH), F32),
            jax.ShapeDtypeStruct((bsz, 2, GLA_QK, GLA_DV), F32),
        ],
        scratch_shapes=[
            pltpu.VMEM((2, seq, GLA_QK), F32),
            pltpu.VMEM((seq, GLA_WIDTH), F32),
            pltpu.VMEM((seq, GLA_WIDTH), F32),
            pltpu.VMEM((2, GLA_QK, GLA_DV), F32),
        ],
        compiler_params=_params(("parallel",)),
        name="gla",
    )(zg, s0, waf, wab, ba, onorm)


def _rms(x, w):
    ms = jnp.mean(x * x, axis=-1, keepdims=True)
    return x * lax.rsqrt(ms + EPS) * w


def _head_norm(x, w, rope):
    outs = []
    for h in range(MLA_HEADS):
        xh = x[:, h * HEAD_PAD:(h + 1) * HEAD_PAD]
        ms = jnp.sum(xh * xh, axis=-1, keepdims=True) * (1.0 / MLA_QK)
        y = xh * lax.rsqrt(ms + EPS) * w
        if rope is not None:
            c, s_up, s_dn = rope
            half = MLA_ROPE // 2
            y = y * c + pltpu.roll(y, half, 1) * s_up + pltpu.roll(y, HEAD_PAD - half, 1) * s_dn
        outs.append(y)
    return outs


def _place_rope_key(kr, e):
    return sum(jnp.dot(p, e, preferred_element_type=F32) for p in _split_bf16(kr, 3))


def _mla_kernel(*refs, seq, n_ctx, use_rope):
    it = iter(refs)
    zm_ref = next(it)
    if n_ctx:
        cckv_ref, ckr_ref = next(it), next(it)
    qn_ref, wuq_ref, kvn_ref, wuk_ref, wuv_ref, qhn_ref, khn_ref, e_ref = (next(it) for _ in range(8))
    rope_ref = next(it) if use_rope else None
    o_ref, ckv_ref = next(it), next(it)
    q_s, k_s, v_s = next(it), next(it), next(it)

    rope = None
    if use_rope:
        rope = (rope_ref[0], rope_ref[1], rope_ref[2])
    e = e_ref[...]
    wuk = wuk_ref[...]
    wuv = wuv_ref[...]
    khn = khn_ref[...]

    cq = _rms(zm_ref[:, ZM_Q:ZM_Q + MLA_Q_LORA], qn_ref[...])
    qh = _head_norm(_dot(cq, wuq_ref[...]), qhn_ref[...], rope)
    qscale = MLA_QK ** -0.5
    for h in range(MLA_HEADS):
        q_s[:, h * HEAD_PAD:(h + 1) * HEAD_PAD] = (qh[h] * qscale).astype(BF16)

    ckv = _rms(zm_ref[:, ZM_KV:ZM_KV + MLA_KV_LORA], kvn_ref[...])
    ckv_ref[...] = ckv
    kh = _head_norm(_dot(ckv, wuk) + _place_rope_key(zm_ref[:, ZM_KR:ZM_KR + LANE], e), khn, rope)
    for h in range(MLA_HEADS):
        k_s[n_ctx:n_ctx + seq, h * HEAD_PAD:(h + 1) * HEAD_PAD] = kh[h].astype(BF16)
    v_s[n_ctx:n_ctx + seq, :] = _dot(ckv, wuv).astype(BF16)
    if n_ctx:
        cc = cckv_ref[...]
        kch = _head_norm(_dot(cc, wuk) + _place_rope_key(ckr_ref[...], e), khn, None)
        for h in range(MLA_HEADS):
            k_s[0:n_ctx, h * HEAD_PAD:(h + 1) * HEAD_PAD] = kch[h].astype(BF16)
        v_s[0:n_ctx, :] = _dot(cc, wuv).astype(BF16)

    def q_block(i, carry):
        rows = pl.ds(pl.multiple_of(i * Q_TILE, Q_TILE), Q_TILE)
        gate = zm_ref[rows, ZM_GATE:ZM_GATE + MLA_WIDTH]
        for h in range(MLA_HEADS):
            hs = slice(h * HEAD_PAD, (h + 1) * HEAD_PAD)
            s = lax.dot_general(q_s[rows, hs], k_s[:, hs], (((1,), (1,)), ((), ())),
                                preferred_element_type=F32)
            m = jnp.max(s, axis=-1, keepdims=True)
            p = jnp.exp(s - m)
            l = jnp.sum(p, axis=-1, keepdims=True)
            o = jnp.dot(p.astype(BF16), v_s[:, hs], preferred_element_type=F32) / l
            o_ref[rows, hs] = o * jax.nn.silu(gate[:, hs])
        return carry

    lax.fori_loop(0, seq // Q_TILE, q_block, 0)


def _mla(zm, ctx, w, rope_tab, bsz, seq):
    const2 = lambda b: (0, 0)
    n_ctx = 0 if ctx is None else ctx[0].shape[-2]
    in_specs = [pl.BlockSpec((seq, ZM_W), lambda b: (b, 0))]
    args = [zm]
    if ctx is not None:
        cckv, ckr, layer = ctx
        in_specs += [
            pl.BlockSpec((None, None, n_ctx, MLA_KV_LORA), lambda b: (b, layer, 0, 0)),
            pl.BlockSpec((None, None, n_ctx, LANE), lambda b: (b, layer, 0, 0)),
        ]
        args += [cckv, ckr]
    in_specs += [
        pl.BlockSpec((1, MLA_Q_LORA), const2),
        pl.BlockSpec((MLA_Q_LORA, MLA_HEADS * HEAD_PAD), const2),
        pl.BlockSpec((1, MLA_KV_LORA), const2),
        pl.BlockSpec((MLA_KV_LORA, MLA_HEADS * HEAD_PAD), const2),
        pl.BlockSpec((MLA_KV_LORA, MLA_WIDTH), const2),
        pl.BlockSpec((1, HEAD_PAD), const2),
        pl.BlockSpec((1, HEAD_PAD), const2),
        pl.BlockSpec((LANE, MLA_HEADS * HEAD_PAD), const2),
    ]
    args += list(w)
    if rope_tab is not None:
        in_specs.append(pl.BlockSpec((3, seq, HEAD_PAD), lambda b: (0, 0, 0)))
        args.append(rope_tab)
    return pl.pallas_call(
        functools.partial(_mla_kernel, seq=seq, n_ctx=n_ctx, use_rope=rope_tab is not None),
        grid=(bsz,),
        in_specs=in_specs,
        out_specs=[
            pl.BlockSpec((seq, MLA_WIDTH), lambda b: (b, 0)),
            pl.BlockSpec((seq, MLA_KV_LORA), lambda b: (b, 0)),
        ],
        out_shape=[
            jax.ShapeDtypeStruct((bsz * seq, MLA_WIDTH), F32),
            jax.ShapeDtypeStruct((bsz * seq, MLA_KV_LORA), F32),
        ],
        scratch_shapes=[
            pltpu.VMEM((seq, MLA_HEADS * HEAD_PAD), BF16),
            pltpu.VMEM((n_ctx + seq, MLA_HEADS * HEAD_PAD), BF16),
            pltpu.VMEM((n_ctx + seq, MLA_WIDTH), BF16),
        ],
        compiler_params=_params(("parallel",)),
        name="mla",
    )(*args)


S5_T = 8
S5_R = CHUNK // S5_T
S5_ROW = S5_T * LANE
S5_W = 2 * S5_TILE_STATE
W_M, W_SF, W_SB, W_CF, W_CB = range(5)


def _cmul(ar, ai, br, bi):
    return ar * br - ai * bi, ar * bi + ai * br


def _s5_prep_kernel(are_ref, aim_ref, ldt_ref, bre_ref, bim_ref, cre_ref, cim_ref, d_ref,
                    w_ref, tab8_ref, tab1_ref):
    b_re, b_im = bre_ref[...], bim_ref[...]
    c_re, c_im = cre_ref[...], cim_ref[...]
    c_hi, c_lo = _split_bf16(jnp.concatenate([c_re, c_im], axis=1), 2)
    kern = []
    for d in (0, 1):
        a_re, a_im = are_ref[d], aim_ref[d]
        dt = jnp.exp(ldt_ref[d])
        lam = a_re * dt
        th = a_im * dt
        mag = jnp.exp(lam)
        ab_re = mag * jnp.cos(th)
        ab_im = mag * jnp.sin(th)
        den = a_re * a_re + a_im * a_im
        n_re = ab_re - 1.0
        cf_re = (n_re * a_re + ab_im * a_im) / den
        cf_im = (ab_im * a_re - n_re * a_im) / den
        bp_re, bp_im = _cmul(b_re, b_im, cf_re, cf_im)
        k = lax.broadcasted_iota(jnp.int32, (2 * S5_T, S5_TILE_STATE), 0).astype(F32)
        pmag = jnp.exp(k * lam)
        pw_re = pmag * jnp.cos(k * th)
        pw_im = pmag * jnp.sin(k * th)
        taps = []
        for p in range(S5_T + 1):
            ar, ai = pw_re[p:p + 1, :], pw_im[p:p + 1, :]
            l_re, l_im = _cmul(bp_re, bp_im, ar, ai)
            v_re, v_im = _cmul(c_re, c_im, ar, ai)
            t_in = S5_T - 1 - p if d == 0 else p
            if 0 <= t_in < S5_T:
                w_ref[W_SF + d, t_in * LANE:(t_in + 1) * LANE, :] = (
                    jnp.concatenate([l_re, l_im], axis=1).astype(BF16))
            t_out = p - 1 if d == 0 else S5_T - p
            if 0 <= t_out < S5_T:
                w_ref[W_CF + d, t_out * LANE:(t_out + 1) * LANE, :] = (
                    jnp.concatenate([v_re, -v_im], axis=1).astype(BF16))
            if p < S5_T:
                l_hi, l_lo = _split_bf16(jnp.concatenate([l_re, -l_im], axis=1), 2)
                nt = lambda a, b: lax.dot_general(a, b, (((1,), (1,)), ((), ())),
                                                  preferred_element_type=F32)
                taps.append(nt(l_hi, c_hi) + nt(l_hi, c_lo) + nt(l_lo, c_hi))
        kern.append(taps)
        r = lax.broadcasted_iota(jnp.int32, (S5_R, S5_TILE_STATE), 0).astype(F32) * float(S5_T)
        r1 = r + float(S5_T)
        pm = jnp.exp(r * lam)
        qm = jnp.exp(-(r1 * lam))
        tab8_ref[d, 0] = pm * jnp.cos(r * th)
        tab8_ref[d, 1] = pm * jnp.sin(r * th)
        tab8_ref[d, 2] = qm * jnp.cos(r1 * th)
        tab8_ref[d, 3] = -(qm * jnp.sin(r1 * th))
        mc = jnp.exp(float(CHUNK) * lam)
        tab1_ref[d, 0:1, :] = mc * jnp.cos(float(CHUNK) * th)
        tab1_ref[d, 1:2, :] = mc * jnp.sin(float(CHUNK) * th)
    row = lax.broadcasted_iota(jnp.int32, (LANE, LANE), 0)
    col = lax.broadcasted_iota(jnp.int32, (LANE, LANE), 1)
    skip = jnp.where(row == col, d_ref[...], 0.0)
    for t in range(S5_T):
        for t2 in range(S5_T):
            if t < t2:
                blk = kern[0][t2 - t]
            elif t > t2:
                blk = kern[1][t - t2]
            else:
                blk = kern[0][0] + kern[1][0] + skip
            w_ref[W_M, t * LANE:(t + 1) * LANE, t2 * LANE:(t2 + 1) * LANE] = blk.astype(BF16)


def _s5_prep(a_re, a_im, ldt, b_re, b_im, c_re, c_im, dsk):
    vec = pl.BlockSpec((None, 2, 1, S5_TILE_STATE), lambda l, j: (l, 0, 0, j))
    blk = pl.BlockSpec((None, None, LANE, S5_TILE_STATE), lambda l, j: (l, j, 0, 0))
    return pl.pallas_call(
        _s5_prep_kernel,
        grid=(DEPTH, S5_TILES),
        in_specs=[vec, vec, vec, blk, blk, blk, blk,
                  pl.BlockSpec((None, 1, LANE), lambda l, j: (l, 0, j))],
        out_specs=[
            pl.BlockSpec((None, None, 5, S5_ROW, S5_W), lambda l, j: (l, j, 0, 0, 0)),
            pl.BlockSpec((None, None, 2, 4, S5_R, S5_TILE_STATE), lambda l, j: (l, j, 0, 0, 0, 0)),
            pl.BlockSpec((None, None, 2, 2, S5_TILE_STATE), lambda l, j: (l, j, 0, 0, 0)),
        ],
        out_shape=[
            jax.ShapeDtypeStruct((DEPTH, S5_TILES, 5, S5_ROW, S5_W), BF16),
            jax.ShapeDtypeStruct((DEPTH, S5_TILES, 2, 4, S5_R, S5_TILE_STATE), F32),
            jax.ShapeDtypeStruct((DEPTH, S5_TILES, 2, 2, S5_TILE_STATE), F32),
        ],
        compiler_params=_params(("parallel", "parallel")),
        name="s5_prep",
    )(a_re, a_im, ldt, b_re, b_im, c_re, c_im, dsk)


def _s5_scan_kernel(u_ref, x0_ref, w_ref, tab8_ref, tab1_ref, y_ref, fs_ref, *, nseq, nb):
    groups = nseq * nb
    nrow = groups * S5_R
    ts = S5_TILE_STATE
    u8 = jnp.concatenate([u_ref[pl.ds(t, nrow, stride=S5_T), :] for t in range(S5_T)],
                         axis=1).astype(BF16)
    ef = jnp.dot(u8, w_ref[W_SF], preferred_element_type=F32).reshape(groups, S5_R, S5_W)
    eb = jnp.dot(u8, w_ref[W_SB], preferred_element_type=F32).reshape(groups, S5_R, S5_W)
    rowi = lax.broadcasted_iota(jnp.int32, (groups, S5_R, ts), 1)

    def prefix(x):
        for s in (1, 2, 4):
            x = x + jnp.where(rowi >= s, pltpu.roll(x, s, 1), 0.0)
        return x

    def suffix(x):
        for s in (1, 2, 4):
            x = x + jnp.where(rowi < S5_R - s, pltpu.roll(x, S5_R - s, 1), 0.0)
        return x

    p_re, p_im, q_re, q_im = (tab8_ref[0, i] for i in range(4))
    a_re, a_im = tab1_ref[0, 0:1, :], tab1_ref[0, 1:2, :]
    w_re, w_im = _cmul(q_re, q_im, ef[:, :, :ts], ef[:, :, ts:])
    cs_re, cs_im = prefix(w_re), prefix(w_im)
    st_re, st_im = [], []
    for s in range(nseq):
        x_re, x_im = x0_ref[s, 0, 0:1, :], x0_ref[s, 0, 1:2, :]
        for b in range(nb):
            g = s * nb + b
            st_re.append(x_re)
            st_im.append(x_im)
            x_re, x_im = _cmul(a_re, a_im, x_re + cs_re[g, S5_R - 1:S5_R, :],
                               x_im + cs_im[g, S5_R - 1:S5_R, :])
        fs_ref[s, 0, 0:1, :] = x_re
        fs_ref[s, 0, 1:2, :] = x_im
    xin_re, xin_im = _cmul(p_re, p_im, cs_re - w_re + jnp.stack(st_re), cs_im - w_im + jnp.stack(st_im))
    xin = jnp.concatenate([xin_re, xin_im], axis=2).reshape(nrow, S5_W)

    p_re, p_im, q_re, q_im = (tab8_ref[1, i] for i in range(4))
    a_re, a_im = tab1_ref[1, 0:1, :], tab1_ref[1, 1:2, :]
    w_re, w_im = _cmul(p_re, p_im, eb[:, :, :ts], eb[:, :, ts:])
    sf_re, sf_im = suffix(w_re), suffix(w_im)
    z_re, z_im = [None] * groups, [None] * groups
    for s in range(nseq):
        x_re, x_im = x0_ref[s, 1, 0:1, :], x0_ref[s, 1, 1:2, :]
        for b in reversed(range(nb)):
            g = s * nb + b
            z_re[g], z_im[g] = _cmul(a_re, a_im, x_re, x_im)
            x_re = sf_re[g, 0:1, :] + z_re[g]
            x_im = sf_im[g, 0:1, :] + z_im[g]
        fs_ref[s, 1, 0:1, :] = x_re
        fs_ref[s, 1, 1:2, :] = x_im
    xnx_re, xnx_im = _cmul(q_re, q_im, sf_re - w_re + jnp.stack(z_re), sf_im - w_im + jnp.stack(z_im))
    xnx = jnp.concatenate([xnx_re, xnx_im], axis=2).reshape(nrow, S5_W)

    y8 = (jnp.dot(u8, w_ref[W_M], preferred_element_type=F32)
          + _dot_nt(xin, w_ref[W_CF]) + _dot_nt(xnx, w_ref[W_CB]))
    for t in range(S5_T):
        y_ref[pl.ds(t, nrow, stride=S5_T), :] = y8[:, t * LANE:(t + 1) * LANE]


def _s5_scan(zs, x0, x0_block, x0_idx, layer, wmat, tab8, tab1, nseq, seq):
    n = nseq * seq
    return pl.pallas_call(
        functools.partial(_s5_scan_kernel, nseq=nseq, nb=seq // CHUNK),
        grid=(S5_TILES,),
        in_specs=[
            pl.BlockSpec((n, LANE), lambda j: (0, j)),
            pl.BlockSpec(x0_block, x0_idx),
            pl.BlockSpec((None, None, 5, S5_ROW, S5_W), lambda j: (layer, j, 0, 0, 0)),
            pl.BlockSpec((None, None, 2, 4, S5_R, S5_TILE_STATE), lambda j: (layer, j, 0, 0, 0, 0)),
            pl.BlockSpec((None, None, 2, 2, S5_TILE_STATE), lambda j: (layer, j, 0, 0, 0)),
        ],
        out_specs=[
            pl.BlockSpec((n, LANE), lambda j: (0, j)),
            pl.BlockSpec((nseq, 2, 2, S5_TILE_STATE), lambda j: (0, 0, 0, j)),
        ],
        out_shape=[
            jax.ShapeDtypeStruct((n, S5_WIDTH), F32),
            jax.ShapeDtypeStruct((nseq, 2, 2, S5_NSTATE), F32),
        ],
        compiler_params=_params(("parallel",)),
        name="s5_scan",
    )(zs, x0, wmat, tab8, tab1)


def _s5_glu_kernel(y_ref, gate_ref, wglu_ref, bglu_ref, o_ref):
    z = _dot(jax.nn.gelu(y_ref[...]), wglu_ref[...]) + bglu_ref[...]
    out = z[:, :S5_WIDTH] * jax.nn.sigmoid(z[:, S5_WIDTH:])
    o_ref[...] = out * jax.nn.silu(gate_ref[...])


def _s5_glu(y, zs, layer, wglu, bglu):
    n = y.shape[0]
    tm = ROW_TILE
    return pl.pallas_call(
        _s5_glu_kernel,
        grid=(n // tm,),
        in_specs=[
            pl.BlockSpec((tm, S5_WIDTH), lambda i: (i, 0)),
            pl.BlockSpec((tm, S5_WIDTH), lambda i: (i, ZS_GATE // S5_WIDTH)),
            pl.BlockSpec((None, S5_WIDTH, 2 * S5_WIDTH), lambda i: (layer, 0, 0)),
            pl.BlockSpec((None, 1, 2 * S5_WIDTH), lambda i: (layer, 0, 0)),
        ],
        out_specs=pl.BlockSpec((tm, S5_WIDTH), lambda i: (i, 0)),
        out_shape=jax.ShapeDtypeStruct((n, S5_WIDTH), F32),
        compiler_params=_params(("parallel",)),
        name="s5_glu",
    )(y, zs, wglu, bglu)


def _merge_kernel(x_ref, mod_ref, nw_ref, oa_ref, ob_ref, oc_ref, wmg_ref, wa_ref, wb_ref, wc_ref,
                  wout_ref, y_ref):
    x = x_ref[...]
    mod = mod_ref[0]
    h = _mod_rmsnorm(x, nw_ref[...], mod).astype(BF16)
    g = jax.nn.sigmoid(jnp.dot(h, wmg_ref[...], preferred_element_type=F32))
    mixed = (g[:, 0:D_MODEL] * _dot(oa_ref[...], wa_ref[...])
             + g[:, D_MODEL:2 * D_MODEL] * _dot(ob_ref[...], wb_ref[...])
             + g[:, 2 * D_MODEL:] * _dot(oc_ref[...], wc_ref[...]))
    y_ref[...] = x + mod[:, 2 * D_MODEL:] * _dot(mixed, wout_ref[...])


def _merge(x2, mod, mod_idx, nw, oa, ob, oc, wmg, wa, wb, wc, wout):
    n = x2.shape[0]
    tm = ROW_TILE
    const = lambda i: (0, 0)
    rows = lambda w: pl.BlockSpec((tm, w), lambda i: (i, 0))
    return pl.pallas_call(
        _merge_kernel,
        grid=(n // tm,),
        in_specs=[
            rows(D_MODEL),
            pl.BlockSpec((1, 1, 3 * D_MODEL), lambda i: (mod_idx(i), 0, 0)),
            pl.BlockSpec((1, D_MODEL), const),
            rows(GLA_WIDTH), rows(MLA_WIDTH), rows(S5_WIDTH),
            pl.BlockSpec((D_MODEL, 3 * D_MODEL), const),
            pl.BlockSpec((GLA_WIDTH, D_MODEL), const),
            pl.BlockSpec((MLA_WIDTH, D_MODEL), const),
            pl.BlockSpec((S5_WIDTH, D_MODEL), const),
            pl.BlockSpec((D_MODEL, D_MODEL), const),
        ],
        out_specs=rows(D_MODEL),
        out_shape=jax.ShapeDtypeStruct((n, D_MODEL), F32),
        compiler_params=_params(("parallel",)),
        name="merge",
    )(x2, mod, nw, oa, ob, oc, wmg, wa, wb, wc, wout)


def _pack_w_in(w):
    sizes = (GLA_QK, GLA_QK, GLA_WIDTH, GLA_RANK, GLA_RANK, GLA_WIDTH,
             MLA_Q_LORA, MLA_KV_LORA, MLA_ROPE, MLA_WIDTH, S5_WIDTH, S5_WIDTH, 3 * D_MODEL)
    offs = np.cumsum((0,) + sizes)
    gq, gk, gv, gaf, gab, gg, mq, mkv, mkr, mg, su, sg, mrg = (w[:, offs[i]:offs[i + 1]] for i in range(13))
    z96 = jnp.zeros((D_MODEL, LANE - 2 * GLA_RANK), w.dtype)
    wg = jnp.concatenate([gq, gk, gv, gaf, gab, z96, gg], axis=1)
    wm = jnp.concatenate([mq, mkv, mkr, z96, mg], axis=1)
    ws = jnp.concatenate([su, sg], axis=1)
    return wg.astype(BF16), wm.astype(BF16), ws.astype(BF16), mrg.astype(BF16)


def _pad_heads(w, width):
    lead = w.shape[:-1]
    w = w.reshape(lead + (MLA_HEADS, width))
    w = jnp.pad(w, [(0, 0)] * len(lead) + [(0, 0), (0, HEAD_PAD - width)])
    return w.reshape(lead + (MLA_HEADS * HEAD_PAD,))


def _rope_tables(n_tok):
    rows = n_tok // GRID_W
    r = jnp.repeat(jnp.arange(rows, dtype=F32), GRID_W)
    col = jnp.tile(jnp.arange(GRID_W, dtype=F32), rows)
    n_freq = MLA_ROPE // 4
    inv = ROPE_THETA ** (-jnp.arange(n_freq, dtype=F32) / n_freq)
    ang = jnp.concatenate([r[:, None] * inv, col[:, None] * inv], axis=-1)
    cos, sin = jnp.cos(ang), jnp.sin(ang)
    half = MLA_ROPE // 2
    z = lambda w: jnp.zeros((n_tok, w), F32)
    tail = HEAD_PAD - MLA_QK
    c = jnp.concatenate([jnp.ones((n_tok, MLA_NOPE), F32), cos, cos, z(tail)], axis=1)
    s_up = jnp.concatenate([z(MLA_NOPE + half), sin, z(tail)], axis=1)
    s_dn = jnp.concatenate([z(MLA_NOPE), -sin, z(half + tail)], axis=1)
    return jnp.stack([c, s_up, s_dn])


def _s5_group_blocks(x):
    lyr = x.shape[0]
    x = x.reshape(lyr, S5_TILES, 8, S5_GROUP, S5_STATE)
    eye = jnp.eye(8, dtype=x.dtype)
    blk = x[:, :, :, :, None, :] * eye[None, None, :, None, :, None]
    return blk.reshape(lyr, S5_TILES, LANE, S5_TILE_STATE)


def kernel(x_prompt, x_sample, c, c_ctx, cache_mla_ckv, cache_mla_krope, state_gla, state_s5,
           norm_w, w_ada, b_ada, w_in, gla_w_a2, gla_b_a, gla_o_norm,
           mla_q_norm, mla_w_uq, mla_kv_norm, mla_w_uk, mla_w_uv, mla_qh_norm, mla_kh_norm,
           s5_a_re, s5_a_im, s5_log_dt, s5_b_re, s5_b_im, s5_c_re, s5_c_im, s5_d, s5_w_glu, s5_b_glu,
           w_bo_gla, w_bo_mla, w_bo_s5, w_out):
    bsz, seq, _ = x_prompt.shape
    dbsz, dseq, _ = x_sample.shape
    past = cache_mla_ckv.shape[2]
    ctx_row = 8 - 1
    assert dbsz <= ctx_row and (bsz * seq) % ROW_TILE == 0 and dseq % ROW_TILE == 0

    cond8 = jnp.zeros((8, D_MODEL), F32).at[0:dbsz].set(c).at[ctx_row].set(c_ctx)
    ada = _ada(cond8, w_ada, b_ada)

    vec = lambda a: a.reshape(DEPTH, 2, 1, S5_NSTATE)
    ldt = jnp.repeat(s5_log_dt[..., None], S5_STATE, axis=-1)
    bt = lambda b: _s5_group_blocks(b.transpose(0, 1, 3, 2))
    wmat, tab8, tab1 = _s5_prep(vec(s5_a_re), vec(s5_a_im), vec(ldt), bt(s5_b_re), bt(s5_b_im),
                                _s5_group_blocks(s5_c_re), _s5_group_blocks(s5_c_im),
                                s5_d.reshape(DEPTH, 1, S5_WIDTH))
    wglu = s5_w_glu.astype(BF16)
    bglu = s5_b_glu.reshape(DEPTH, 1, 2 * S5_WIDTH)

    wuq = _pad_heads(mla_w_uq, MLA_QK).astype(BF16)
    wuk = _pad_heads(mla_w_uk, MLA_NOPE).astype(BF16)
    wuv = mla_w_uv.astype(BF16)
    qhn = jnp.pad(mla_qh_norm, ((0, 0), (0, HEAD_PAD - MLA_QK)))
    khn = jnp.pad(mla_kh_norm, ((0, 0), (0, HEAD_PAD - MLA_QK)))
    e_np = np.zeros((LANE, MLA_HEADS * HEAD_PAD), np.float32)
    for h in range(MLA_HEADS):
        for i in range(MLA_ROPE):
            e_np[i, h * HEAD_PAD + MLA_NOPE + i] = 1.0
    e_place = jnp.asarray(e_np, BF16)
    rope_tab = _rope_tables(dseq)
    ckr_pad = jnp.pad(cache_mla_krope, ((0, 0), (0, 0), (0, 0), (0, LANE - MLA_ROPE)))

    zrow = lambda n: jnp.zeros((DEPTH, n, GLA_QK), F32)
    waf = jnp.concatenate([gla_w_a2[:, 0], zrow(LANE - GLA_RANK)], axis=1).astype(BF16)
    wab = jnp.concatenate([zrow(GLA_RANK), gla_w_a2[:, 1], zrow(LANE - 2 * GLA_RANK)], axis=1).astype(BF16)
    sgla = state_gla.reshape(dbsz, DEPTH, 2, GLA_QK, GLA_DV)
    ss5 = state_s5.reshape(dbsz, DEPTH, 2, 2, S5_NSTATE)
    zero_gla = jnp.zeros((2, GLA_QK, GLA_DV), F32)
    zero_s5 = jnp.zeros((bsz, 2, 2, S5_NSTATE), F32)

    hp = x_prompt.reshape(bsz * seq, D_MODEL)
    hs = x_sample.reshape(dbsz * dseq, D_MODEL)
    ckv_l, krope_l, gla_l, s5_l = [], [], [], []
    for l in range(DEPTH):
        wg, wm, ws, wmg = _pack_w_in(w_in[l])
        mod = ada[l].reshape(8, 1, 3 * D_MODEL)
        nw = norm_w[l].reshape(1, D_MODEL)
        mla_w = (mla_q_norm[l].reshape(1, -1), wuq[l], mla_kv_norm[l].reshape(1, -1), wuk[l], wuv[l],
                 qhn[l].reshape(1, -1), khn[l].reshape(1, -1), e_place)
        wbo = (w_bo_gla[l].astype(BF16), w_bo_mla[l].astype(BF16), w_bo_s5[l].astype(BF16))
        wout = w_out[l].astype(BF16)
        onorm = gla_o_norm[l].reshape(1, GLA_DV)

        def layer(x2, nb, n, mod_idx, ctx):
            zg, zm, zs = _in_proj(x2, mod, mod_idx, nw, wg, wm, ws)
            if ctx:
                s0, s0_idx = sgla, (lambda b: (b, l, 0, 0, 0))
                x0, x0_blk = ss5, (nb, None, 2, 2, S5_TILE_STATE)
                x0_idx = lambda j: (0, l, 0, 0, j)
                mctx, rt = (cache_mla_ckv, ckr_pad, l), rope_tab
            else:
                s0, s0_idx = zero_gla, (lambda b: (0, 0, 0))
                x0, x0_blk = zero_s5, (nb, 2, 2, S5_TILE_STATE)
                x0_idx = lambda j: (0, 0, 0, j)
                mctx, rt = None, None
            oa, st_gla = _gla(zg, s0, s0_idx, waf[l], wab[l], gla_b_a[l], onorm, nb, n)
            ob, ckv = _mla(zm, mctx, mla_w, rt, nb, n)
            y_ssm, st_s5 = _s5_scan(zs, x0, x0_blk, x0_idx, l, wmat, tab8, tab1, nb, n)
            oc = _s5_glu(y_ssm, zs, l, wglu, bglu)
            y = _merge(x2, mod, mod_idx, nw, oa, ob, oc, wmg, *wbo, wout)
            return y, zm, ckv, st_gla, st_s5

        hp, zm_p, ckv_p, st_gla_p, st_s5_p = layer(hp, bsz, seq, lambda i: ctx_row, False)
        ckv_l.append(ckv_p.reshape(bsz, seq, MLA_KV_LORA))
        krope_l.append(zm_p[:, ZM_KR:ZM_KR + MLA_ROPE].reshape(bsz, seq, MLA_ROPE))
        gla_l.append(st_gla_p.reshape(bsz, 2, GLA_HEADS, GLA_DK, GLA_DV))
        s5_l.append(st_s5_p.reshape(bsz, 2, 2, S5_GROUPS, S5_STATE))
        blocks_per_seq = dseq // ROW_TILE
        hs = layer(hs, dbsz, dseq, lambda i: i // blocks_per_seq, True)[0]

    return (hp.reshape(bsz, seq, D_MODEL), hs.reshape(dbsz, dseq, D_MODEL),
            jnp.stack(ckv_l, axis=1), jnp.stack(krope_l, axis=1),
            jnp.stack(gla_l, axis=1), jnp.stack(s5_l, axis=1))
```

```python
import functools

import jax
import jax.numpy as jnp
import numpy as np
from jax import lax
from jax.experimental import pallas as pl
from jax.experimental.pallas import tpu as pltpu

F32 = jnp.float32
BF16 = jnp.bfloat16

EPS = 1e-6
D_MODEL = 1024
DEPTH = 2
GRID_W = 64
ROPE_THETA = 10000.0
GLA_HEADS = 4
GLA_DK = 64
GLA_DV = 128
GLA_RANK = 16
GLA_GATE_NORM = 16.0
GLA_QK = GLA_HEADS * GLA_DK
GLA_WIDTH = GLA_HEADS * GLA_DV
MLA_HEADS = 4
MLA_Q_LORA = 384
MLA_KV_LORA = 256
MLA_NOPE = 64
MLA_ROPE = 32
MLA_QK = MLA_NOPE + MLA_ROPE
MLA_DV = 128
MLA_WIDTH = MLA_HEADS * MLA_DV
S5_WIDTH = 512
S5_GROUP = 16
S5_GROUPS = 32
S5_STATE = 64
S5_NSTATE = S5_GROUPS * S5_STATE

LANE = 128
HEAD_PAD = LANE
CHUNK = 64
GLA_STEP = 256
S5_TILES = S5_WIDTH // LANE
S5_TILE_STATE = S5_NSTATE // S5_TILES
ROW_TILE = 512
Q_TILE = 256
VMEM_LIMIT = 56 * 1024 * 1024

ZG_Q, ZG_K, ZG_V, ZG_A, ZG_GATE, ZG_W = 0, 256, 512, 1024, 1152, 1664
ZM_Q, ZM_KV, ZM_KR, ZM_GATE, ZM_W = 0, 384, 640, 768, 1280
ZS_U, ZS_GATE, ZS_W = 0, 512, 1024


def _dot(a, b):
    return jnp.dot(a.astype(BF16), b.astype(BF16), preferred_element_type=F32)


def _dot_nt(a, b):
    return lax.dot_general(a.astype(BF16), b.astype(BF16), (((1,), (1,)), ((), ())),
                           preferred_element_type=F32)


def _dot_tn(a, b):
    return lax.dot_general(a.astype(BF16), b.astype(BF16), (((0,), (0,)), ((), ())),
                           preferred_element_type=F32)


def _split_bf16(x, parts):
    out = []
    r = x
    for _ in range(parts):
        p = r.astype(BF16)
        out.append(p)
        r = r - p.astype(F32)
    return out


def _params(sem):
    return pltpu.CompilerParams(dimension_semantics=sem, vmem_limit_bytes=VMEM_LIMIT)


def _ada_kernel(c_ref, w_ref, b_ref, o_ref):
    s = jax.nn.silu(c_ref[...])
    o_ref[...] = _dot(s, w_ref[...]) + b_ref[...]


def _ada(cond8, w_ada, b_ada):
    tn = 1024
    return pl.pallas_call(
        _ada_kernel,
        grid=(DEPTH, 3 * D_MODEL // tn),
        in_specs=[
            pl.BlockSpec((8, D_MODEL), lambda l, n: (0, 0)),
            pl.BlockSpec((None, D_MODEL, tn), lambda l, n: (l, 0, n)),
            pl.BlockSpec((None, 1, tn), lambda l, n: (l, 0, n)),
        ],
        out_specs=pl.BlockSpec((None, 8, tn), lambda l, n: (l, 0, n)),
        out_shape=jax.ShapeDtypeStruct((DEPTH, 8, 3 * D_MODEL), F32),
        compiler_params=_params(("parallel", "parallel")),
        name="ada",
    )(cond8, w_ada, b_ada.reshape(DEPTH, 1, 3 * D_MODEL))


def _mod_rmsnorm(x, nw, mod):
    ms = jnp.mean(x * x, axis=-1, keepdims=True)
    y = x * lax.rsqrt(ms + EPS) * nw
    return y * (1.0 + mod[:, D_MODEL:2 * D_MODEL]) + mod[:, 0:D_MODEL]


def _in_proj_kernel(x_ref, mod_ref, nw_ref, wg_ref, wm_ref, ws_ref, zg_ref, zm_ref, zs_ref):
    h = _mod_rmsnorm(x_ref[...], nw_ref[...], mod_ref[0]).astype(BF16)
    zg_ref[...] = jnp.dot(h, wg_ref[...], preferred_element_type=F32)
    zm_ref[...] = jnp.dot(h, wm_ref[...], preferred_element_type=F32)
    zs_ref[...] = jnp.dot(h, ws_ref[...], preferred_element_type=F32)


def _in_proj(x2, mod, mod_idx, nw, wg, wm, ws):
    n = x2.shape[0]
    tm = ROW_TILE
    const = lambda i: (0, 0)
    return pl.pallas_call(
        _in_proj_kernel,
        grid=(n // tm,),
        in_specs=[
            pl.BlockSpec((tm, D_MODEL), lambda i: (i, 0)),
            pl.BlockSpec((1, 1, 3 * D_MODEL), lambda i: (mod_idx(i), 0, 0)),
            pl.BlockSpec((1, D_MODEL), const),
            pl.BlockSpec((D_MODEL, ZG_W), const),
            pl.BlockSpec((D_MODEL, ZM_W), const),
            pl.BlockSpec((D_MODEL, ZS_W), const),
        ],
        out_specs=[
            pl.BlockSpec((tm, ZG_W), lambda i: (i, 0)),
            pl.BlockSpec((tm, ZM_W), lambda i: (i, 0)),
            pl.BlockSpec((tm, ZS_W), lambda i: (i, 0)),
        ],
        out_shape=[
            jax.ShapeDtypeStruct((n, ZG_W), F32),
            jax.ShapeDtypeStruct((n, ZM_W), F32),
            jax.ShapeDtypeStruct((n, ZS_W), F32),
        ],
        compiler_params=_params(("parallel",)),
        name="in_proj",
    )(x2, mod, nw, wg, wm, ws)


def _gla_kernel(zg_ref, s0_ref, waf_ref, wab_ref, ba_ref, onorm_ref, o_ref, sfin_ref,
                la_s, o_s, st_s, *, nsteps):
    a_blk = zg_ref[:, ZG_A:ZG_A + LANE]
    inv_norm = 1.0 / GLA_GATE_NORM
    la_s[0] = jax.nn.log_sigmoid(_dot(a_blk, waf_ref[...]) + ba_ref[0:1, :]) * inv_norm
    la_s[1] = jax.nn.log_sigmoid(_dot(a_blk, wab_ref[...]) + ba_ref[1:2, :]) * inv_norm
    zero_blk = jnp.zeros((GLA_DK, GLA_DV), F32)
    for d in (0, 1):
        s0 = s0_ref[d]
        rows_bd = []
        for h in range(GLA_HEADS):
            sh = s0[h * GLA_DK:(h + 1) * GLA_DK, :]
            rows_bd.append(jnp.concatenate([sh if h2 == h else zero_blk for h2 in range(GLA_HEADS)], axis=1))
        st_s[d] = jnp.concatenate(rows_bd, axis=0).T

    def iota(shape, axis, shift):
        return lax.shift_right_logical(lax.broadcasted_iota(jnp.int32, shape, axis), shift)

    log_chunk, log_dv = CHUNK.bit_length() - 1, GLA_DV.bit_length() - 1
    row = lax.broadcasted_iota(jnp.int32, (GLA_STEP, GLA_STEP), 0)
    col = lax.broadcasted_iota(jnp.int32, (GLA_STEP, GLA_STEP), 1)
    same_chunk = iota((GLA_STEP, GLA_STEP), 0, log_chunk) == iota((GLA_STEP, GLA_STEP), 1, log_chunk)
    masks = (same_chunk & (row >= col), same_chunk & (row <= col))
    lane_head = iota((GLA_STEP, GLA_QK), 1, log_chunk)
    row_chunk = iota((GLA_STEP, GLA_QK), 0, log_chunk)
    state_blk = iota((GLA_WIDTH, GLA_QK), 0, log_dv) == iota((GLA_WIDTH, GLA_QK), 1, log_chunk)
    qscale = GLA_DK ** -0.5
    nch = GLA_STEP // CHUNK

    def step(i, carry):
        for d in (0, 1):
            sc = i if d == 0 else nsteps - 1 - i
            rows = pl.ds(pl.multiple_of(sc * GLA_STEP, GLA_STEP), GLA_STEP)
            a_hi, a_lo = _split_bf16(la_s[d, rows, :], 2)
            tri = masks[d].astype(BF16)
            cum = (jnp.dot(tri, a_hi, preferred_element_type=F32)
                   + jnp.dot(tri, a_lo, preferred_element_type=F32))
            edge = CHUNK - 1 if d == 0 else 0
            blast = [cum[c * CHUNK + edge:c * CHUNK + edge + 1, :] for c in range(nch)]
            bl = jnp.concatenate([jnp.broadcast_to(b, (CHUNK, GLA_QK)) for b in blast], axis=0)
            q = zg_ref[rows, ZG_Q:ZG_Q + GLA_QK] * qscale
            k = zg_ref[rows, ZG_K:ZG_K + GLA_QK]
            vf = zg_ref[rows, ZG_V:ZG_V + GLA_WIDTH]
            v = vf.astype(BF16)
            v_t = vf.T.astype(BF16)
            qd = q * jnp.exp(cum)
            kd = (k * jnp.exp(-cum)).astype(BF16)
            kr = k * jnp.exp(bl - cum)
            outs = []
            for h in range(GLA_HEADS):
                qh = jnp.where(lane_head == h, qd, 0.0)
                att = jnp.where(masks[d], _dot_nt(qh, kd), 0.0)
                outs.append(_dot(att, v[:, h * GLA_DV:(h + 1) * GLA_DV]))
            s = st_s[d]
            inter = [None] * nch
            for c in (range(nch) if d == 0 else reversed(range(nch))):
                inter[c] = _dot_nt(qd[c * CHUNK:(c + 1) * CHUNK, :], s)
                kv_t = jnp.where(state_blk, _dot(v_t, jnp.where(row_chunk == c, kr, 0.0)), 0.0)
                s = s * jnp.exp(blast[c]) + kv_t
            st_s[d] = s
            o_s[d, rows, :] = jnp.concatenate(outs, axis=1) + jnp.concatenate(inter, axis=0)
        return carry

    lax.fori_loop(0, nsteps, step, 0)
    for d in (0, 1):
        s_fin = st_s[d].T
        for h in range(GLA_HEADS):
            sfin_ref[d, h * GLA_DK:(h + 1) * GLA_DK, :] = (
                s_fin[h * GLA_DK:(h + 1) * GLA_DK, h * GLA_DV:(h + 1) * GLA_DV])
    o = o_s[0] + o_s[1]
    gate = zg_ref[:, ZG_GATE:ZG_GATE + GLA_WIDTH]
    onorm = onorm_ref[...]
    for h in range(GLA_HEADS):
        vs = slice(h * GLA_DV, (h + 1) * GLA_DV)
        oh = o[:, vs]
        ms = jnp.mean(oh * oh, axis=-1, keepdims=True)
        o_ref[:, vs] = oh * lax.rsqrt(ms + EPS) * onorm * jax.nn.silu(gate[:, vs])


def _gla(zg, s0, s0_idx, waf, wab, ba, onorm, bsz, seq):
    const = lambda b: (0, 0)
    nsd = len(s0.shape)
    s0_block = (None,) * (nsd - 3) + (2, GLA_QK, GLA_DV)
    return pl.pallas_call(
        functools.partial(_gla_kernel, nsteps=seq // GLA_STEP),
        grid=(bsz,),
        in_specs=[
            pl.BlockSpec((seq, ZG_W), lambda b: (b, 0)),
            pl.BlockSpec(s0_block, s0_idx),
            pl.BlockSpec((LANE, GLA_QK), const),
            pl.BlockSpec((LANE, GLA_QK), const),
            pl.BlockSpec((2, GLA_QK), const),
            pl.BlockSpec((1, GLA_DV), const),
        ],
        out_specs=[
            pl.BlockSpec((seq, GLA_WIDTH), lambda b: (b, 0)),
            pl.BlockSpec((None, 2, GLA_QK, GLA_DV), lambda b: (b, 0, 0, 0)),
        ],
        out_shape=[
            jax.ShapeDtypeStruct((bsz * seq, GLA_WIDTH), F32),
            jax.ShapeDtypeStruct((bsz, 2, GLA_QK, GLA_DV), F32),
        ],
        scratch_shapes=[
            pltpu.VMEM((2, seq, GLA_QK), F32),
            pltpu.VMEM((2, seq, GLA_WIDTH), F32),
            pltpu.VMEM((2, GLA_WIDTH, GLA_QK), F32),
        ],
        compiler_params=_params(("parallel",)),
        name="gla",
    )(zg, s0, waf, wab, ba, onorm)


def _rms(x, w):
    ms = jnp.mean(x * x, axis=-1, keepdims=True)
    return x * lax.rsqrt(ms + EPS) * w


def _head_norm(x, w, rope):
    outs = []
    for h in range(MLA_HEADS):
        xh = x[:, h * HEAD_PAD:(h + 1) * HEAD_PAD]
        ms = jnp.sum(xh * xh, axis=-1, keepdims=True) * (1.0 / MLA_QK)
        y = xh * lax.rsqrt(ms + EPS) * w
        if rope is not None:
            c, s_up, s_dn = rope
            half = MLA_ROPE // 2
            y = y * c + pltpu.roll(y, half, 1) * s_up + pltpu.roll(y, HEAD_PAD - half, 1) * s_dn
        outs.append(y)
    return outs


def _place_rope_key(kr, e):
    return sum(jnp.dot(p, e, preferred_element_type=F32) for p in _split_bf16(kr, 3))


def _mla_kernel(*refs, seq, n_ctx, use_rope):
    it = iter(refs)
    zm_ref = next(it)
    if n_ctx:
        cckv_ref, ckr_ref = next(it), next(it)
    qn_ref, wuq_ref, kvn_ref, wuk_ref, wuv_ref, qhn_ref, khn_ref, e_ref = (next(it) for _ in range(8))
    rope_ref = next(it) if use_rope else None
    o_ref, ckv_ref = next(it), next(it)
    q_s, k_s, v_s = next(it), next(it), next(it)

    rope = None
    if use_rope:
        rope = (rope_ref[0], rope_ref[1], rope_ref[2])
    e = e_ref[...]
    wuk = wuk_ref[...]
    wuv = wuv_ref[...]
    khn = khn_ref[...]

    cq = _rms(zm_ref[:, ZM_Q:ZM_Q + MLA_Q_LORA], qn_ref[...])
    qh = _head_norm(_dot(cq, wuq_ref[...]), qhn_ref[...], rope)
    qscale = MLA_QK ** -0.5
    for h in range(MLA_HEADS):
        q_s[:, h * HEAD_PAD:(h + 1) * HEAD_PAD] = (qh[h] * qscale).astype(BF16)

    ckv = _rms(zm_ref[:, ZM_KV:ZM_KV + MLA_KV_LORA], kvn_ref[...])
    ckv_ref[...] = ckv
    kh = _head_norm(_dot(ckv, wuk) + _place_rope_key(zm_ref[:, ZM_KR:ZM_KR + LANE], e), khn, rope)
    for h in range(MLA_HEADS):
        k_s[n_ctx:n_ctx + seq, h * HEAD_PAD:(h + 1) * HEAD_PAD] = kh[h].astype(BF16)
    v_s[n_ctx:n_ctx + seq, :] = _dot(ckv, wuv).astype(BF16)
    if n_ctx:
        cc = cckv_ref[...]
        kch = _head_norm(_dot(cc, wuk) + _place_rope_key(ckr_ref[...], e), khn, None)
        for h in range(MLA_HEADS):
            k_s[0:n_ctx, h * HEAD_PAD:(h + 1) * HEAD_PAD] = kch[h].astype(BF16)
        v_s[0:n_ctx, :] = _dot(cc, wuv).astype(BF16)

    def q_block(i, carry):
        rows = pl.ds(pl.multiple_of(i * Q_TILE, Q_TILE), Q_TILE)
        gate = zm_ref[rows, ZM_GATE:ZM_GATE + MLA_WIDTH]
        for h in range(MLA_HEADS):
            hs = slice(h * HEAD_PAD, (h + 1) * HEAD_PAD)
            s = lax.dot_general(q_s[rows, hs], k_s[:, hs], (((1,), (1,)), ((), ())),
                                preferred_element_type=F32)
            m = jnp.max(s, axis=-1, keepdims=True)
            p = jnp.exp(s - m)
            l = jnp.sum(p, axis=-1, keepdims=True)
            o = jnp.dot(p.astype(BF16), v_s[:, hs], preferred_element_type=F32) / l
            o_ref[rows, hs] = o * jax.nn.silu(gate[:, hs])
        return carry

    lax.fori_loop(0, seq // Q_TILE, q_block, 0)


def _mla(zm, ctx, w, rope_tab, bsz, seq):
    const2 = lambda b: (0, 0)
    n_ctx = 0 if ctx is None else ctx[0].shape[-2]
    in_specs = [pl.BlockSpec((seq, ZM_W), lambda b: (b, 0))]
    args = [zm]
    if ctx is not None:
        cckv, ckr, layer = ctx
        in_specs += [
            pl.BlockSpec((None, None, n_ctx, MLA_KV_LORA), lambda b: (b, layer, 0, 0)),
            pl.BlockSpec((None, None, n_ctx, LANE), lambda b: (b, layer, 0, 0)),
        ]
        args += [cckv, ckr]
    in_specs += [
        pl.BlockSpec((1, MLA_Q_LORA), const2),
        pl.BlockSpec((MLA_Q_LORA, MLA_HEADS * HEAD_PAD), const2),
        pl.BlockSpec((1, MLA_KV_LORA), const2),
        pl.BlockSpec((MLA_KV_LORA, MLA_HEADS * HEAD_PAD), const2),
        pl.BlockSpec((MLA_KV_LORA, MLA_WIDTH), const2),
        pl.BlockSpec((1, HEAD_PAD), const2),
        pl.BlockSpec((1, HEAD_PAD), const2),
        pl.BlockSpec((LANE, MLA_HEADS * HEAD_PAD), const2),
    ]
    args += list(w)
    if rope_tab is not None:
        in_specs.append(pl.BlockSpec((3, seq, HEAD_PAD), lambda b: (0, 0, 0)))
        args.append(rope_tab)
    return pl.pallas_call(
        functools.partial(_mla_kernel, seq=seq, n_ctx=n_ctx, use_rope=rope_tab is not None),
        grid=(bsz,),
        in_specs=in_specs,
        out_specs=[
            pl.BlockSpec((seq, MLA_WIDTH), lambda b: (b, 0)),
            pl.BlockSpec((seq, MLA_KV_LORA), lambda b: (b, 0)),
        ],
        out_shape=[
            jax.ShapeDtypeStruct((bsz * seq, MLA_WIDTH), F32),
            jax.ShapeDtypeStruct((bsz * seq, MLA_KV_LORA), F32),
        ],
        scratch_shapes=[
            pltpu.VMEM((seq, MLA_HEADS * HEAD_PAD), BF16),
            pltpu.VMEM((n_ctx + seq, MLA_HEADS * HEAD_PAD), BF16),
            pltpu.VMEM((n_ctx + seq, MLA_WIDTH), BF16),
        ],
        compiler_params=_params(("parallel",)),
        name="mla",
    )(*args)


S5_T = 8
S5_R = CHUNK // S5_T
S5_ROW = S5_T * LANE
S5_W = 2 * S5_TILE_STATE
W_M, W_SF, W_SB, W_CF, W_CB = range(5)


def _cmul(ar, ai, br, bi):
    return ar * br - ai * bi, ar * bi + ai * br


def _s5_prep_kernel(are_ref, aim_ref, ldt_ref, bre_ref, bim_ref, cre_ref, cim_ref, d_ref,
                    w_ref, tab8_ref, tab1_ref):
    b_re, b_im = bre_ref[...], bim_ref[...]
    c_re, c_im = cre_ref[...], cim_ref[...]
    c_hi, c_lo = _split_bf16(jnp.concatenate([c_re, c_im], axis=1), 2)
    kern = []
    for d in (0, 1):
        a_re, a_im = are_ref[d], aim_ref[d]
        dt = jnp.exp(ldt_ref[d])
        lam = a_re * dt
        th = a_im * dt
        mag = jnp.exp(lam)
        ab_re = mag * jnp.cos(th)
        ab_im = mag * jnp.sin(th)
        den = a_re * a_re + a_im * a_im
        n_re = ab_re - 1.0
        cf_re = (n_re * a_re + ab_im * a_im) / den
        cf_im = (ab_im * a_re - n_re * a_im) / den
        bp_re, bp_im = _cmul(b_re, b_im, cf_re, cf_im)
        k = lax.broadcasted_iota(jnp.int32, (2 * S5_T, S5_TILE_STATE), 0).astype(F32)
        pmag = jnp.exp(k * lam)
        pw_re = pmag * jnp.cos(k * th)
        pw_im = pmag * jnp.sin(k * th)
        taps = []
        for p in range(S5_T + 1):
            ar, ai = pw_re[p:p + 1, :], pw_im[p:p + 1, :]
            l_re, l_im = _cmul(bp_re, bp_im, ar, ai)
            v_re, v_im = _cmul(c_re, c_im, ar, ai)
            t_in = S5_T - 1 - p if d == 0 else p
            if 0 <= t_in < S5_T:
                w_ref[W_SF + d, t_in * LANE:(t_in + 1) * LANE, :] = (
                    jnp.concatenate([l_re, l_im], axis=1).astype(BF16))
            t_out = p - 1 if d == 0 else S5_T - p
            if 0 <= t_out < S5_T:
                w_ref[W_CF + d, t_out * LANE:(t_out + 1) * LANE, :] = (
                    jnp.concatenate([v_re, -v_im], axis=1).astype(BF16))
            if p < S5_T:
                l_hi, l_lo = _split_bf16(jnp.concatenate([l_re, -l_im], axis=1), 2)
                nt = lambda a, b: lax.dot_general(a, b, (((1,), (1,)), ((), ())),
                                                  preferred_element_type=F32)
                taps.append(nt(l_hi, c_hi) + nt(l_hi, c_lo) + nt(l_lo, c_hi))
        kern.append(taps)
        r = lax.broadcasted_iota(jnp.int32, (S5_R, S5_TILE_STATE), 0).astype(F32) * float(S5_T)
        r1 = r + float(S5_T)
        pm = jnp.exp(r * lam)
        qm = jnp.exp(-(r1 * lam))
        tab8_ref[d, 0] = pm * jnp.cos(r * th)
        tab8_ref[d, 1] = pm * jnp.sin(r * th)
        tab8_ref[d, 2] = qm * jnp.cos(r1 * th)
        tab8_ref[d, 3] = -(qm * jnp.sin(r1 * th))
        mc = jnp.exp(float(CHUNK) * lam)
        tab1_ref[d, 0:1, :] = mc * jnp.cos(float(CHUNK) * th)
        tab1_ref[d, 1:2, :] = mc * jnp.sin(float(CHUNK) * th)
    row = lax.broadcasted_iota(jnp.int32, (LANE, LANE), 0)
    col = lax.broadcasted_iota(jnp.int32, (LANE, LANE), 1)
    skip = jnp.where(row == col, d_ref[...], 0.0)
    for t in range(S5_T):
        for t2 in range(S5_T):
            if t < t2:
                blk = kern[0][t2 - t]
            elif t > t2:
                blk = kern[1][t - t2]
            else:
                blk = kern[0][0] + kern[1][0] + skip
            w_ref[W_M, t * LANE:(t + 1) * LANE, t2 * LANE:(t2 + 1) * LANE] = blk.astype(BF16)


def _s5_prep(a_re, a_im, ldt, b_re, b_im, c_re, c_im, dsk):
    vec = pl.BlockSpec((None, 2, 1, S5_TILE_STATE), lambda l, j: (l, 0, 0, j))
    blk = pl.BlockSpec((None, None, LANE, S5_TILE_STATE), lambda l, j: (l, j, 0, 0))
    return pl.pallas_call(
        _s5_prep_kernel,
        grid=(DEPTH, S5_TILES),
        in_specs=[vec, vec, vec, blk, blk, blk, blk,
                  pl.BlockSpec((None, 1, LANE), lambda l, j: (l, 0, j))],
        out_specs=[
            pl.BlockSpec((None, None, 5, S5_ROW, S5_W), lambda l, j: (l, j, 0, 0, 0)),
            pl.BlockSpec((None, None, 2, 4, S5_R, S5_TILE_STATE), lambda l, j: (l, j, 0, 0, 0, 0)),
            pl.BlockSpec((None, None, 2, 2, S5_TILE_STATE), lambda l, j: (l, j, 0, 0, 0)),
        ],
        out_shape=[
            jax.ShapeDtypeStruct((DEPTH, S5_TILES, 5, S5_ROW, S5_W), BF16),
            jax.ShapeDtypeStruct((DEPTH, S5_TILES, 2, 4, S5_R, S5_TILE_STATE), F32),
            jax.ShapeDtypeStruct((DEPTH, S5_TILES, 2, 2, S5_TILE_STATE), F32),
        ],
        compiler_params=_params(("parallel", "parallel")),
        name="s5_prep",
    )(a_re, a_im, ldt, b_re, b_im, c_re, c_im, dsk)


def _s5_scan_kernel(u_ref, x0_ref, w_ref, tab8_ref, tab1_ref, y_ref, fs_ref, *, nseq, nb):
    groups = nseq * nb
    nrow = groups * S5_R
    ts = S5_TILE_STATE
    u8 = jnp.concatenate([u_ref[pl.ds(t, nrow, stride=S5_T), :] for t in range(S5_T)],
                         axis=1).astype(BF16)
    ef = jnp.dot(u8, w_ref[W_SF], preferred_element_type=F32).reshape(groups, S5_R, S5_W)
    eb = jnp.dot(u8, w_ref[W_SB], preferred_element_type=F32).reshape(groups, S5_R, S5_W)
    rowi = lax.broadcasted_iota(jnp.int32, (groups, S5_R, ts), 1)

    def prefix(x):
        for s in (1, 2, 4):
            x = x + jnp.where(rowi >= s, pltpu.roll(x, s, 1), 0.0)
        return x

    def suffix(x):
        for s in (1, 2, 4):
            x = x + jnp.where(rowi < S5_R - s, pltpu.roll(x, S5_R - s, 1), 0.0)
        return x

    p_re, p_im, q_re, q_im = (tab8_ref[0, i] for i in range(4))
    a_re, a_im = tab1_ref[0, 0:1, :], tab1_ref[0, 1:2, :]
    w_re, w_im = _cmul(q_re, q_im, ef[:, :, :ts], ef[:, :, ts:])
    cs_re, cs_im = prefix(w_re), prefix(w_im)
    st_re, st_im = [], []
    for s in range(nseq):
        x_re, x_im = x0_ref[s, 0, 0:1, :], x0_ref[s, 0, 1:2, :]
        for b in range(nb):
            g = s * nb + b
            st_re.append(x_re)
            st_im.append(x_im)
            x_re, x_im = _cmul(a_re, a_im, x_re + cs_re[g, S5_R - 1:S5_R, :],
                               x_im + cs_im[g, S5_R - 1:S5_R, :])
        fs_ref[s, 0, 0:1, :] = x_re
        fs_ref[s, 0, 1:2, :] = x_im
    xin_re, xin_im = _cmul(p_re, p_im, cs_re - w_re + jnp.stack(st_re), cs_im - w_im + jnp.stack(st_im))
    xin = jnp.concatenate([xin_re, xin_im], axis=2).reshape(nrow, S5_W)

    p_re, p_im, q_re, q_im = (tab8_ref[1, i] for i in range(4))
    a_re, a_im = tab1_ref[1, 0:1, :], tab1_ref[1, 1:2, :]
    w_re, w_im = _cmul(p_re, p_im, eb[:, :, :ts], eb[:, :, ts:])
    sf_re, sf_im = suffix(w_re), suffix(w_im)
    z_re, z_im = [None] * groups, [None] * groups
    for s in range(nseq):
        x_re, x_im = x0_ref[s, 1, 0:1, :], x0_ref[s, 1, 1:2, :]
        for b in reversed(range(nb)):
            g = s * nb + b
            z_re[g], z_im[g] = _cmul(a_re, a_im, x_re, x_im)
            x_re = sf_re[g, 0:1, :] + z_re[g]
            x_im = sf_im[g, 0:1, :] + z_im[g]
        fs_ref[s, 1, 0:1, :] = x_re
        fs_ref[s, 1, 1:2, :] = x_im
    xnx_re, xnx_im = _cmul(q_re, q_im, sf_re - w_re + jnp.stack(z_re), sf_im - w_im + jnp.stack(z_im))
    xnx = jnp.concatenate([xnx_re, xnx_im], axis=2).reshape(nrow, S5_W)

    y8 = (jnp.dot(u8, w_ref[W_M], preferred_element_type=F32)
          + _dot_nt(xin, w_ref[W_CF]) + _dot_nt(xnx, w_ref[W_CB]))
    for t in range(S5_T):
        y_ref[pl.ds(t, nrow, stride=S5_T), :] = y8[:, t * LANE:(t + 1) * LANE]


def _s5_scan(zs, x0, x0_block, x0_idx, layer, wmat, tab8, tab1, nseq, seq):
    n = nseq * seq
    return pl.pallas_call(
        functools.partial(_s5_scan_kernel, nseq=nseq, nb=seq // CHUNK),
        grid=(S5_TILES,),
        in_specs=[
            pl.BlockSpec((n, LANE), lambda j: (0, j)),
            pl.BlockSpec(x0_block, x0_idx),
            pl.BlockSpec((None, None, 5, S5_ROW, S5_W), lambda j: (layer, j, 0, 0, 0)),
            pl.BlockSpec((None, None, 2, 4, S5_R, S5_TILE_STATE), lambda j: (layer, j, 0, 0, 0, 0)),
            pl.BlockSpec((None, None, 2, 2, S5_TILE_STATE), lambda j: (layer, j, 0, 0, 0)),
        ],
        out_specs=[
            pl.BlockSpec((n, LANE), lambda j: (0, j)),
            pl.BlockSpec((nseq, 2, 2, S5_TILE_STATE), lambda j: (0, 0, 0, j)),
        ],
        out_shape=[
            jax.ShapeDtypeStruct((n, S5_WIDTH), F32),
            jax.ShapeDtypeStruct((nseq, 2, 2, S5_NSTATE), F32),
        ],
        compiler_params=_params(("parallel",)),
        name="s5_scan",
    )(zs, x0, wmat, tab8, tab1)


def _s5_glu_kernel(y_ref, gate_ref, wglu_ref, bglu_ref, o_ref):
    z = _dot(jax.nn.gelu(y_ref[...]), wglu_ref[...]) + bglu_ref[...]
    out = z[:, :S5_WIDTH] * jax.nn.sigmoid(z[:, S5_WIDTH:])
    o_ref[...] = out * jax.nn.silu(gate_ref[...])


def _s5_glu(y, zs, layer, wglu, bglu):
    n = y.shape[0]
    tm = ROW_TILE
    return pl.pallas_call(
        _s5_glu_kernel,
        grid=(n // tm,),
        in_specs=[
            pl.BlockSpec((tm, S5_WIDTH), lambda i: (i, 0)),
            pl.BlockSpec((tm, S5_WIDTH), lambda i: (i, ZS_GATE // S5_WIDTH)),
            pl.BlockSpec((None, S5_WIDTH, 2 * S5_WIDTH), lambda i: (layer, 0, 0)),
            pl.BlockSpec((None, 1, 2 * S5_WIDTH), lambda i: (layer, 0, 0)),
        ],
        out_specs=pl.BlockSpec((tm, S5_WIDTH), lambda i: (i, 0)),
        out_shape=jax.ShapeDtypeStruct((n, S5_WIDTH), F32),
        compiler_params=_params(("parallel",)),
        name="s5_glu",
    )(y, zs, wglu, bglu)


def _merge_kernel(x_ref, mod_ref, nw_ref, oa_ref, ob_ref, oc_ref, wmg_ref, wa_ref, wb_ref, wc_ref,
                  wout_ref, y_ref):
    x = x_ref[...]
    mod = mod_ref[0]
    h = _mod_rmsnorm(x, nw_ref[...], mod).astype(BF16)
    g = jax.nn.sigmoid(jnp.dot(h, wmg_ref[...], preferred_element_type=F32))
    mixed = (g[:, 0:D_MODEL] * _dot(oa_ref[...], wa_ref[...])
             + g[:, D_MODEL:2 * D_MODEL] * _dot(ob_ref[...], wb_ref[...])
             + g[:, 2 * D_MODEL:] * _dot(oc_ref[...], wc_ref[...]))
    y_ref[...] = x + mod[:, 2 * D_MODEL:] * _dot(mixed, wout_ref[...])


def _merge(x2, mod, mod_idx, nw, oa, ob, oc, wmg, wa, wb, wc, wout):
    n = x2.shape[0]
    tm = ROW_TILE
    const = lambda i: (0, 0)
    rows = lambda w: pl.BlockSpec((tm, w), lambda i: (i, 0))
    return pl.pallas_call(
        _merge_kernel,
        grid=(n // tm,),
        in_specs=[
            rows(D_MODEL),
            pl.BlockSpec((1, 1, 3 * D_MODEL), lambda i: (mod_idx(i), 0, 0)),
            pl.BlockSpec((1, D_MODEL), const),
            rows(GLA_WIDTH), rows(MLA_WIDTH), rows(S5_WIDTH),
            pl.BlockSpec((D_MODEL, 3 * D_MODEL), const),
            pl.BlockSpec((GLA_WIDTH, D_MODEL), const),
            pl.BlockSpec((MLA_WIDTH, D_MODEL), const),
            pl.BlockSpec((S5_WIDTH, D_MODEL), const),
            pl.BlockSpec((D_MODEL, D_MODEL), const),
        ],
        out_specs=rows(D_MODEL),
        out_shape=jax.ShapeDtypeStruct((n, D_MODEL), F32),
        compiler_params=_params(("parallel",)),
        name="merge",
    )(x2, mod, nw, oa, ob, oc, wmg, wa, wb, wc, wout)


def _pack_w_in(w):
    sizes = (GLA_QK, GLA_QK, GLA_WIDTH, GLA_RANK, GLA_RANK, GLA_WIDTH,
             MLA_Q_LORA, MLA_KV_LORA, MLA_ROPE, MLA_WIDTH, S5_WIDTH, S5_WIDTH, 3 * D_MODEL)
    offs = np.cumsum((0,) + sizes)
    gq, gk, gv, gaf, gab, gg, mq, mkv, mkr, mg, su, sg, mrg = (w[:, offs[i]:offs[i + 1]] for i in range(13))
    z96 = jnp.zeros((D_MODEL, LANE - 2 * GLA_RANK), w.dtype)
    wg = jnp.concatenate([gq, gk, gv, gaf, gab, z96, gg], axis=1)
    wm = jnp.concatenate([mq, mkv, mkr, z96, mg], axis=1)
    ws = jnp.concatenate([su, sg], axis=1)
    return wg.astype(BF16), wm.astype(BF16), ws.astype(BF16), mrg.astype(BF16)


def _pad_heads(w, width):
    lead = w.shape[:-1]
    w = w.reshape(lead + (MLA_HEADS, width))
    w = jnp.pad(w, [(0, 0)] * len(lead) + [(0, 0), (0, HEAD_PAD - width)])
    return w.reshape(lead + (MLA_HEADS * HEAD_PAD,))


def _rope_tables(n_tok):
    rows = n_tok // GRID_W
    r = jnp.repeat(jnp.arange(rows, dtype=F32), GRID_W)
    col = jnp.tile(jnp.arange(GRID_W, dtype=F32), rows)
    n_freq = MLA_ROPE // 4
    inv = ROPE_THETA ** (-jnp.arange(n_freq, dtype=F32) / n_freq)
    ang = jnp.concatenate([r[:, None] * inv, col[:, None] * inv], axis=-1)
    cos, sin = jnp.cos(ang), jnp.sin(ang)
    half = MLA_ROPE // 2
    z = lambda w: jnp.zeros((n_tok, w), F32)
    tail = HEAD_PAD - MLA_QK
    c = jnp.concatenate([jnp.ones((n_tok, MLA_NOPE), F32), cos, cos, z(tail)], axis=1)
    s_up = jnp.concatenate([z(MLA_NOPE + half), sin, z(tail)], axis=1)
    s_dn = jnp.concatenate([z(MLA_NOPE), -sin, z(half + tail)], axis=1)
    return jnp.stack([c, s_up, s_dn])


def _s5_group_blocks(x):
    lyr = x.shape[0]
    x = x.reshape(lyr, S5_TILES, 8, S5_GROUP, S5_STATE)
    eye = jnp.eye(8, dtype=x.dtype)
    blk = x[:, :, :, :, None, :] * eye[None, None, :, None, :, None]
    return blk.reshape(lyr, S5_TILES, LANE, S5_TILE_STATE)


def kernel(x_prompt, x_sample, c, c_ctx, cache_mla_ckv, cache_mla_krope, state_gla, state_s5,
           norm_w, w_ada, b_ada, w_in, gla_w_a2, gla_b_a, gla_o_norm,
           mla_q_norm, mla_w_uq, mla_kv_norm, mla_w_uk, mla_w_uv, mla_qh_norm, mla_kh_norm,
           s5_a_re, s5_a_im, s5_log_dt, s5_b_re, s5_b_im, s5_c_re, s5_c_im, s5_d, s5_w_glu, s5_b_glu,
           w_bo_gla, w_bo_mla, w_bo_s5, w_out):
    bsz, seq, _ = x_prompt.shape
    dbsz, dseq, _ = x_sample.shape
    past = cache_mla_ckv.shape[2]
    ctx_row = 8 - 1
    assert dbsz <= ctx_row and (bsz * seq) % ROW_TILE == 0 and dseq % ROW_TILE == 0

    cond8 = jnp.zeros((8, D_MODEL), F32).at[0:dbsz].set(c).at[ctx_row].set(c_ctx)
    ada = _ada(cond8, w_ada, b_ada)

    vec = lambda a: a.reshape(DEPTH, 2, 1, S5_NSTATE)
    ldt = jnp.repeat(s5_log_dt[..., None], S5_STATE, axis=-1)
    bt = lambda b: _s5_group_blocks(b.transpose(0, 1, 3, 2))
    wmat, tab8, tab1 = _s5_prep(vec(s5_a_re), vec(s5_a_im), vec(ldt), bt(s5_b_re), bt(s5_b_im),
                                _s5_group_blocks(s5_c_re), _s5_group_blocks(s5_c_im),
                                s5_d.reshape(DEPTH, 1, S5_WIDTH))
    wglu = s5_w_glu.astype(BF16)
    bglu = s5_b_glu.reshape(DEPTH, 1, 2 * S5_WIDTH)

    wuq = _pad_heads(mla_w_uq, MLA_QK).astype(BF16)
    wuk = _pad_heads(mla_w_uk, MLA_NOPE).astype(BF16)
    wuv = mla_w_uv.astype(BF16)
    qhn = jnp.pad(mla_qh_norm, ((0, 0), (0, HEAD_PAD - MLA_QK)))
    khn = jnp.pad(mla_kh_norm, ((0, 0), (0, HEAD_PAD - MLA_QK)))
    e_np = np.zeros((LANE, MLA_HEADS * HEAD_PAD), np.float32)
    for h in range(MLA_HEADS):
        for i in range(MLA_ROPE):
            e_np[i, h * HEAD_PAD + MLA_NOPE + i] = 1.0
    e_place = jnp.asarray(e_np, BF16)
    rope_tab = _rope_tables(dseq)
    ckr_pad = jnp.pad(cache_mla_krope, ((0, 0), (0, 0), (0, 0), (0, LANE - MLA_ROPE)))

    zrow = lambda n: jnp.zeros((DEPTH, n, GLA_QK), F32)
    waf = jnp.concatenate([gla_w_a2[:, 0], zrow(LANE - GLA_RANK)], axis=1).astype(BF16)
    wab = jnp.concatenate([zrow(GLA_RANK), gla_w_a2[:, 1], zrow(LANE - 2 * GLA_RANK)], axis=1).astype(BF16)
    sgla = state_gla.reshape(dbsz, DEPTH, 2, GLA_QK, GLA_DV)
    ss5 = state_s5.reshape(dbsz, DEPTH, 2, 2, S5_NSTATE)
    zero_gla = jnp.zeros((2, GLA_QK, GLA_DV), F32)
    zero_s5 = jnp.zeros((bsz, 2, 2, S5_NSTATE), F32)

    hp = x_prompt.reshape(bsz * seq, D_MODEL)
    hs = x_sample.reshape(dbsz * dseq, D_MODEL)
    ckv_l, krope_l, gla_l, s5_l = [], [], [], []
    for l in range(DEPTH):
        wg, wm, ws, wmg = _pack_w_in(w_in[l])
        mod = ada[l].reshape(8, 1, 3 * D_MODEL)
        nw = norm_w[l].reshape(1, D_MODEL)
        mla_w = (mla_q_norm[l].reshape(1, -1), wuq[l], mla_kv_norm[l].reshape(1, -1), wuk[l], wuv[l],
                 qhn[l].reshape(1, -1), khn[l].reshape(1, -1), e_place)
        wbo = (w_bo_gla[l].astype(BF16), w_bo_mla[l].astype(BF16), w_bo_s5[l].astype(BF16))
        wout = w_out[l].astype(BF16)
        onorm = gla_o_norm[l].reshape(1, GLA_DV)

        def layer(x2, nb, n, mod_idx, ctx):
            zg, zm, zs = _in_proj(x2, mod, mod_idx, nw, wg, wm, ws)
            if ctx:
                s0, s0_idx = sgla, (lambda b: (b, l, 0, 0, 0))
                x0, x0_blk = ss5, (nb, None, 2, 2, S5_TILE_STATE)
                x0_idx = lambda j: (0, l, 0, 0, j)
                mctx, rt = (cache_mla_ckv, ckr_pad, l), rope_tab
            else:
                s0, s0_idx = zero_gla, (lambda b: (0, 0, 0))
                x0, x0_blk = zero_s5, (nb, 2, 2, S5_TILE_STATE)
                x0_idx = lambda j: (0, 0, 0, j)
                mctx, rt = None, None
            oa, st_gla = _gla(zg, s0, s0_idx, waf[l], wab[l], gla_b_a[l], onorm, nb, n)
            ob, ckv = _mla(zm, mctx, mla_w, rt, nb, n)
            y_ssm, st_s5 = _s5_scan(zs, x0, x0_blk, x0_idx, l, wmat, tab8, tab1, nb, n)
            oc = _s5_glu(y_ssm, zs, l, wglu, bglu)
            y = _merge(x2, mod, mod_idx, nw, oa, ob, oc, wmg, *wbo, wout)
            return y, zm, ckv, st_gla, st_s5

        hp, zm_p, ckv_p, st_gla_p, st_s5_p = layer(hp, bsz, seq, lambda i: ctx_row, False)
        ckv_l.append(ckv_p.reshape(bsz, seq, MLA_KV_LORA))
        krope_l.append(zm_p[:, ZM_KR:ZM_KR + MLA_ROPE].reshape(bsz, seq, MLA_ROPE))
        gla_l.append(st_gla_p.reshape(bsz, 2, GLA_HEADS, GLA_DK, GLA_DV))
        s5_l.append(st_s5_p.reshape(bsz, 2, 2, S5_GROUPS, S5_STATE))
        blocks_per_seq = dseq // ROW_TILE
        hs = layer(hs, dbsz, dseq, lambda i: i // blocks_per_seq, True)[0]

    return (hp.reshape(bsz, seq, D_MODEL), hs.reshape(dbsz, dseq, D_MODEL),
            jnp.stack(ckv_l, axis=1), jnp.stack(krope_l, axis=1),
            jnp.stack(gla_l, axis=1), jnp.stack(s5_l, axis=1))
```

```python
import functools

import jax
import jax.numpy as jnp
import numpy as np
from jax import lax
from jax.experimental import pallas as pl
from jax.experimental.pallas import tpu as pltpu

F32 = jnp.float32
BF16 = jnp.bfloat16

EPS = 1e-6
D_MODEL = 1024
DEPTH = 2
GRID_W = 64
ROPE_THETA = 10000.0
GLA_HEADS = 4
GLA_DK = 64
GLA_DV = 128
GLA_RANK = 16
GLA_GATE_NORM = 16.0
GLA_QK = GLA_HEADS * GLA_DK
GLA_WIDTH = GLA_HEADS * GLA_DV
MLA_HEADS = 4
MLA_Q_LORA = 384
MLA_KV_LORA = 256
MLA_NOPE = 64
MLA_ROPE = 32
MLA_QK = MLA_NOPE + MLA_ROPE
MLA_DV = 128
MLA_WIDTH = MLA_HEADS * MLA_DV
S5_WIDTH = 512
S5_GROUP = 16
S5_GROUPS = 32
S5_STATE = 64
S5_NSTATE = S5_GROUPS * S5_STATE

LANE = 128
HEAD_PAD = LANE
CHUNK = 64
GLA_STEP = 256
S5_TILES = S5_WIDTH // LANE
S5_TILE_STATE = S5_NSTATE // S5_TILES
ROW_TILE = 512
Q_TILE = 256
VMEM_LIMIT = 56 * 1024 * 1024

IN_W = 3840
ZG_BASE, ZG_W = 0, 1664
ZG_Q, ZG_K, ZG_V, ZG_A, ZG_GATE = 0, 256, 512, 1024, 1056
ZM_BASE, ZM_W = 1536, 1280
ZM_Q, ZM_KV, ZM_KR, ZM_KR_TILE, ZM_GATE = 32, 416, 672, 640, 704
ZS_BASE, ZS_W = 2688, 1152
ZS_U, ZS_GATE = 64, 576
MERGE_COL = 3776


def _dot(a, b):
    return jnp.dot(a.astype(BF16), b.astype(BF16), preferred_element_type=F32)


def _dot_nt(a, b):
    return lax.dot_general(a.astype(BF16), b.astype(BF16), (((1,), (1,)), ((), ())),
                           preferred_element_type=F32)


def _split_bf16(x, parts):
    out = []
    r = x
    for _ in range(parts):
        p = r.astype(BF16)
        out.append(p)
        r = r - p.astype(F32)
    return out


def _params(sem):
    return pltpu.CompilerParams(dimension_semantics=sem, vmem_limit_bytes=VMEM_LIMIT)


def _ada_kernel(c_ref, w_ref, b_ref, o_ref):
    s = jax.nn.silu(c_ref[...])
    o_ref[...] = _dot(s, w_ref[...]) + b_ref[...]


def _ada(cond8, w_ada, b_ada):
    tn = 1024
    return pl.pallas_call(
        _ada_kernel,
        grid=(DEPTH, 3 * D_MODEL // tn),
        in_specs=[
            pl.BlockSpec((8, D_MODEL), lambda l, n: (0, 0)),
            pl.BlockSpec((None, D_MODEL, tn), lambda l, n: (l, 0, n)),
            pl.BlockSpec((None, 1, tn), lambda l, n: (l, 0, n)),
        ],
        out_specs=pl.BlockSpec((None, 8, tn), lambda l, n: (l, 0, n)),
        out_shape=jax.ShapeDtypeStruct((DEPTH, 8, 3 * D_MODEL), F32),
        compiler_params=_params(("parallel", "parallel")),
        name="ada",
    )(cond8, w_ada, b_ada.reshape(DEPTH, 1, 3 * D_MODEL))


def _mod_rmsnorm(x, nw, mod):
    ms = jnp.mean(x * x, axis=-1, keepdims=True)
    y = x * lax.rsqrt(ms + EPS) * nw
    return y * (1.0 + mod[:, D_MODEL:2 * D_MODEL]) + mod[:, 0:D_MODEL]


def _in_proj_kernel(x_ref, mod_ref, nw_ref, w_ref, zg_ref, zm_ref, zs_ref, wb_s):
    @pl.when(pl.program_id(0) == 0)
    def _():
        wb_s[...] = w_ref[...].astype(BF16)

    h = _mod_rmsnorm(x_ref[...], nw_ref[...], mod_ref[0]).astype(BF16)
    z = jnp.dot(h, wb_s[...], preferred_element_type=F32)
    zg_ref[...] = z[:, ZG_BASE:ZG_BASE + ZG_W].astype(BF16)
    zm_ref[...] = z[:, ZM_BASE:ZM_BASE + ZM_W].astype(BF16)
    zs_ref[...] = z[:, ZS_BASE:ZS_BASE + ZS_W].astype(BF16)


def _in_proj(x2, mod, mod_idx, nw, w_in, layer):
    n = x2.shape[0]
    tm = ROW_TILE
    const = lambda i: (0, 0)
    return pl.pallas_call(
        _in_proj_kernel,
        grid=(n // tm,),
        in_specs=[
            pl.BlockSpec((tm, D_MODEL), lambda i: (i, 0)),
            pl.BlockSpec((1, 1, 3 * D_MODEL), lambda i: (mod_idx(i), 0, 0)),
            pl.BlockSpec((1, D_MODEL), const),
            pl.BlockSpec((None, D_MODEL, IN_W), lambda i: (layer, 0, 0), pipeline_mode=pl.Buffered(1)),
        ],
        out_specs=[
            pl.BlockSpec((tm, ZG_W), lambda i: (i, 0)),
            pl.BlockSpec((tm, ZM_W), lambda i: (i, 0)),
            pl.BlockSpec((tm, ZS_W), lambda i: (i, 0)),
        ],
        out_shape=[
            jax.ShapeDtypeStruct((n, ZG_W), BF16),
            jax.ShapeDtypeStruct((n, ZM_W), BF16),
            jax.ShapeDtypeStruct((n, ZS_W), BF16),
        ],
        scratch_shapes=[pltpu.VMEM((D_MODEL, IN_W), BF16)],
        compiler_params=_params(("arbitrary",)),
        name="in_proj",
    )(x2, mod, nw, w_in)


def _gla_kernel(zg_ref, s0_ref, waf_ref, wab_ref, ba_ref, onorm_ref, o_ref, sfin_ref,
                la_s, o_s, st_s, *, nsteps):
    a_blk = zg_ref[:, ZG_A:ZG_A + LANE]
    inv_norm = 1.0 / GLA_GATE_NORM
    la_s[0] = jax.nn.log_sigmoid(_dot(a_blk, waf_ref[...]) + ba_ref[0:1, :]) * inv_norm
    la_s[1] = jax.nn.log_sigmoid(_dot(a_blk, wab_ref[...]) + ba_ref[1:2, :]) * inv_norm
    zero_blk = jnp.zeros((GLA_DK, GLA_DV), F32)
    for d in (0, 1):
        s0 = s0_ref[d]
        rows_bd = []
        for h in range(GLA_HEADS):
            sh = s0[h * GLA_DK:(h + 1) * GLA_DK, :]
            rows_bd.append(jnp.concatenate([sh if h2 == h else zero_blk for h2 in range(GLA_HEADS)], axis=1))
        st_s[d] = jnp.concatenate(rows_bd, axis=0).T

    def iota(shape, axis, shift):
        return lax.shift_right_logical(lax.broadcasted_iota(jnp.int32, shape, axis), shift)

    log_chunk, log_dv = CHUNK.bit_length() - 1, GLA_DV.bit_length() - 1
    row = lax.broadcasted_iota(jnp.int32, (GLA_STEP, GLA_STEP), 0)
    col = lax.broadcasted_iota(jnp.int32, (GLA_STEP, GLA_STEP), 1)
    same_chunk = iota((GLA_STEP, GLA_STEP), 0, log_chunk) == iota((GLA_STEP, GLA_STEP), 1, log_chunk)
    masks = (same_chunk & (row >= col), same_chunk & (row <= col))
    lane_head = iota((GLA_STEP, GLA_QK), 1, log_chunk)
    row_chunk = iota((GLA_STEP, GLA_QK), 0, log_chunk)
    state_blk = iota((GLA_WIDTH, GLA_QK), 0, log_dv) == iota((GLA_WIDTH, GLA_QK), 1, log_chunk)
    qscale = GLA_DK ** -0.5
    nch = GLA_STEP // CHUNK

    def step(i, carry):
        for d in (0, 1):
            sc = i if d == 0 else nsteps - 1 - i
            rows = pl.ds(pl.multiple_of(sc * GLA_STEP, GLA_STEP), GLA_STEP)
            a_hi, a_lo = _split_bf16(la_s[d, rows, :], 2)
            tri = masks[d].astype(BF16)
            cum = (jnp.dot(tri, a_hi, preferred_element_type=F32)
                   + jnp.dot(tri, a_lo, preferred_element_type=F32))
            edge = CHUNK - 1 if d == 0 else 0
            blast = [cum[c * CHUNK + edge:c * CHUNK + edge + 1, :] for c in range(nch)]
            bl = jnp.concatenate([jnp.broadcast_to(b, (CHUNK, GLA_QK)) for b in blast], axis=0)
            q = zg_ref[rows, ZG_Q:ZG_Q + GLA_QK].astype(F32) * qscale
            k = zg_ref[rows, ZG_K:ZG_K + GLA_QK].astype(F32)
            v = zg_ref[rows, ZG_V:ZG_V + GLA_WIDTH]
            v_t = v.astype(F32).T.astype(BF16)
            qd = q * jnp.exp(cum)
            kd = (k * jnp.exp(-cum)).astype(BF16)
            kr = k * jnp.exp(bl - cum)
            outs = []
            for h in range(GLA_HEADS):
                qh = jnp.where(lane_head == h, qd, 0.0)
                att = jnp.where(masks[d], _dot_nt(qh, kd), 0.0)
                outs.append(_dot(att, v[:, h * GLA_DV:(h + 1) * GLA_DV]))
            s = st_s[d]
            inter = [None] * nch
            for c in (range(nch) if d == 0 else reversed(range(nch))):
                inter[c] = _dot_nt(qd[c * CHUNK:(c + 1) * CHUNK, :], s)
                kv_t = jnp.where(state_blk, _dot(v_t, jnp.where(row_chunk == c, kr, 0.0)), 0.0)
                s = s * jnp.exp(blast[c]) + kv_t
            st_s[d] = s
            o_s[d, rows, :] = jnp.concatenate(outs, axis=1) + jnp.concatenate(inter, axis=0)
        return carry

    lax.fori_loop(0, nsteps, step, 0)
    for d in (0, 1):
        s_fin = st_s[d].T
        for h in range(GLA_HEADS):
            sfin_ref[d, h * GLA_DK:(h + 1) * GLA_DK, :] = (
                s_fin[h * GLA_DK:(h + 1) * GLA_DK, h * GLA_DV:(h + 1) * GLA_DV])
    o = o_s[0] + o_s[1]
    gate = zg_ref[:, ZG_GATE:ZG_GATE + GLA_WIDTH].astype(F32)
    onorm = onorm_ref[...]
    for h in range(GLA_HEADS):
        vs = slice(h * GLA_DV, (h + 1) * GLA_DV)
        oh = o[:, vs]
        ms = jnp.mean(oh * oh, axis=-1, keepdims=True)
        o_ref[:, vs] = oh * lax.rsqrt(ms + EPS) * onorm * jax.nn.silu(gate[:, vs])


def _gla(zg, s0, s0_idx, waf, wab, ba, onorm, bsz, seq):
    const = lambda b: (0, 0)
    nsd = len(s0.shape)
    s0_block = (None,) * (nsd - 3) + (2, GLA_QK, GLA_DV)
    return pl.pallas_call(
        functools.partial(_gla_kernel, nsteps=seq // GLA_STEP),
        grid=(bsz,),
        in_specs=[
            pl.BlockSpec((seq, ZG_W), lambda b: (b, 0)),
            pl.BlockSpec(s0_block, s0_idx),
            pl.BlockSpec((LANE, GLA_QK), const),
            pl.BlockSpec((LANE, GLA_QK), const),
            pl.BlockSpec((2, GLA_QK), const),
            pl.BlockSpec((1, GLA_DV), const),
        ],
        out_specs=[
            pl.BlockSpec((seq, GLA_WIDTH), lambda b: (b, 0)),
            pl.BlockSpec((None, 2, GLA_QK, GLA_DV), lambda b: (b, 0, 0, 0)),
        ],
        out_shape=[
            jax.ShapeDtypeStruct((bsz * seq, GLA_WIDTH), F32),
            jax.ShapeDtypeStruct((bsz, 2, GLA_QK, GLA_DV), F32),
        ],
        scratch_shapes=[
            pltpu.VMEM((2, seq, GLA_QK), F32),
            pltpu.VMEM((2, seq, GLA_WIDTH), F32),
            pltpu.VMEM((2, GLA_WIDTH, GLA_QK), F32),
        ],
        compiler_params=_params(("parallel",)),
        name="gla",
    )(zg, s0, waf, wab, ba, onorm)


def _rms(x, w):
    ms = jnp.mean(x * x, axis=-1, keepdims=True)
    return x * lax.rsqrt(ms + EPS) * w


def _head_norm(x, w, rope):
    outs = []
    for h in range(MLA_HEADS):
        xh = x[:, h * HEAD_PAD:(h + 1) * HEAD_PAD]
        ms = jnp.sum(xh * xh, axis=-1, keepdims=True) * (1.0 / MLA_QK)
        y = xh * lax.rsqrt(ms + EPS) * w
        if rope is not None:
            c, s_up, s_dn = rope
            half = MLA_ROPE // 2
            y = y * c + pltpu.roll(y, half, 1) * s_up + pltpu.roll(y, HEAD_PAD - half, 1) * s_dn
        outs.append(y)
    return outs


def _place_rope_key(kr, e):
    return sum(jnp.dot(p, e, preferred_element_type=F32) for p in _split_bf16(kr, 3))


def _mla_kernel(*refs, seq, n_ctx, use_rope):
    it = iter(refs)
    zm_ref = next(it)
    if n_ctx:
        cckv_ref, ckr_ref = next(it), next(it)
    qn_ref, wuq_ref, kvn_ref, wuk_ref, wuv_ref, qhn_ref, khn_ref, e_ref, ec_ref = (next(it) for _ in range(9))
    rope_ref = next(it) if use_rope else None
    o_ref, ckv_ref = next(it), next(it)
    q_s, k_s, v_s = next(it), next(it), next(it)

    rope = None
    if use_rope:
        rope = (rope_ref[0], rope_ref[1], rope_ref[2])
    e = e_ref[...]
    wuk = wuk_ref[...]
    wuv = wuv_ref[...]
    khn = khn_ref[...]

    cq = _rms(zm_ref[:, ZM_Q:ZM_Q + MLA_Q_LORA].astype(F32), qn_ref[...])
    qh = _head_norm(_dot(cq, wuq_ref[...]), qhn_ref[...], rope)
    qscale = MLA_QK ** -0.5
    for h in range(MLA_HEADS):
        q_s[:, h * HEAD_PAD:(h + 1) * HEAD_PAD] = (qh[h] * qscale).astype(BF16)

    ckv = _rms(zm_ref[:, ZM_KV:ZM_KV + MLA_KV_LORA].astype(F32), kvn_ref[...])
    ckv_ref[...] = ckv
    k_pe = jnp.dot(zm_ref[:, ZM_KR_TILE:ZM_KR_TILE + LANE], e, preferred_element_type=F32)
    kh = _head_norm(_dot(ckv, wuk) + k_pe, khn, rope)
    for h in range(MLA_HEADS):
        k_s[n_ctx:n_ctx + seq, h * HEAD_PAD:(h + 1) * HEAD_PAD] = kh[h].astype(BF16)
    v_s[n_ctx:n_ctx + seq, :] = _dot(ckv, wuv).astype(BF16)
    if n_ctx:
        cc = cckv_ref[...]
        kch = _head_norm(_dot(cc, wuk) + _place_rope_key(ckr_ref[...], ec_ref[...]), khn, None)
        for h in range(MLA_HEADS):
            k_s[0:n_ctx, h * HEAD_PAD:(h + 1) * HEAD_PAD] = kch[h].astype(BF16)
        v_s[0:n_ctx, :] = _dot(cc, wuv).astype(BF16)

    def q_block(i, carry):
        rows = pl.ds(pl.multiple_of(i * Q_TILE, Q_TILE), Q_TILE)
        gate = zm_ref[rows, ZM_GATE:ZM_GATE + MLA_WIDTH].astype(F32)
        for h in range(MLA_HEADS):
            hs = slice(h * HEAD_PAD, (h + 1) * HEAD_PAD)
            s = lax.dot_general(q_s[rows, hs], k_s[:, hs], (((1,), (1,)), ((), ())),
                                preferred_element_type=F32)
            m = jnp.max(s, axis=-1, keepdims=True)
            p = jnp.exp(s - m)
            l = jnp.sum(p, axis=-1, keepdims=True)
            o = jnp.dot(p.astype(BF16), v_s[:, hs], preferred_element_type=F32) / l
            o_ref[rows, hs] = o * jax.nn.silu(gate[:, hs])
        return carry

    lax.fori_loop(0, seq // Q_TILE, q_block, 0)


def _mla(zm, ctx, w, rope_tab, bsz, seq):
    const2 = lambda b: (0, 0)
    n_ctx = 0 if ctx is None else ctx[0].shape[-2]
    in_specs = [pl.BlockSpec((seq, ZM_W), lambda b: (b, 0))]
    args = [zm]
    if ctx is not None:
        cckv, ckr, layer = ctx
        in_specs += [
            pl.BlockSpec((None, None, n_ctx, MLA_KV_LORA), lambda b: (b, layer, 0, 0)),
            pl.BlockSpec((None, None, n_ctx, LANE), lambda b: (b, layer, 0, 0)),
        ]
        args += [cckv, ckr]
    in_specs += [
        pl.BlockSpec((1, MLA_Q_LORA), const2),
        pl.BlockSpec((MLA_Q_LORA, MLA_HEADS * HEAD_PAD), const2),
        pl.BlockSpec((1, MLA_KV_LORA), const2),
        pl.BlockSpec((MLA_KV_LORA, MLA_HEADS * HEAD_PAD), const2),
        pl.BlockSpec((MLA_KV_LORA, MLA_WIDTH), const2),
        pl.BlockSpec((1, HEAD_PAD), const2),
        pl.BlockSpec((1, HEAD_PAD), const2),
        pl.BlockSpec((LANE, MLA_HEADS * HEAD_PAD), const2),
        pl.BlockSpec((LANE, MLA_HEADS * HEAD_PAD), const2),
    ]
    args += list(w)
    if rope_tab is not None:
        in_specs.append(pl.BlockSpec((3, seq, HEAD_PAD), lambda b: (0, 0, 0)))
        args.append(rope_tab)
    return pl.pallas_call(
        functools.partial(_mla_kernel, seq=seq, n_ctx=n_ctx, use_rope=rope_tab is not None),
        grid=(bsz,),
        in_specs=in_specs,
        out_specs=[
            pl.BlockSpec((seq, MLA_WIDTH), lambda b: (b, 0)),
            pl.BlockSpec((seq, MLA_KV_LORA), lambda b: (b, 0)),
        ],
        out_shape=[
            jax.ShapeDtypeStruct((bsz * seq, MLA_WIDTH), F32),
            jax.ShapeDtypeStruct((bsz * seq, MLA_KV_LORA), F32),
        ],
        scratch_shapes=[
            pltpu.VMEM((seq, MLA_HEADS * HEAD_PAD), BF16),
            pltpu.VMEM((n_ctx + seq, MLA_HEADS * HEAD_PAD), BF16),
            pltpu.VMEM((n_ctx + seq, MLA_WIDTH), BF16),
        ],
        compiler_params=_params(("parallel",)),
        name="mla",
    )(*args)


S5_T = 8
S5_R = CHUNK // S5_T
S5_ROW = S5_T * LANE
S5_W = 2 * S5_TILE_STATE
W_M, W_SF, W_SB, W_CF, W_CB = range(5)


def _cmul(ar, ai, br, bi):
    return ar * br - ai * bi, ar * bi + ai * br


def _s5_prep_kernel(are_ref, aim_ref, ldt_ref, bre_ref, bim_ref, cre_ref, cim_ref, d_ref,
                    w_ref, tab8_ref, tab1_ref):
    b_re, b_im = bre_ref[...], bim_ref[...]
    c_re, c_im = cre_ref[...], cim_ref[...]
    c_hi, c_lo = _split_bf16(jnp.concatenate([c_re, c_im], axis=1), 2)
    kern = []
    for d in (0, 1):
        a_re, a_im = are_ref[d], aim_ref[d]
        dt = jnp.exp(ldt_ref[d])
        lam = a_re * dt
        th = a_im * dt
        mag = jnp.exp(lam)
        ab_re = mag * jnp.cos(th)
        ab_im = mag * jnp.sin(th)
        den = a_re * a_re + a_im * a_im
        n_re = ab_re - 1.0
        cf_re = (n_re * a_re + ab_im * a_im) / den
        cf_im = (ab_im * a_re - n_re * a_im) / den
        bp_re, bp_im = _cmul(b_re, b_im, cf_re, cf_im)
        k = lax.broadcasted_iota(jnp.int32, (2 * S5_T, S5_TILE_STATE), 0).astype(F32)
        pmag = jnp.exp(k * lam)
        pw_re = pmag * jnp.cos(k * th)
        pw_im = pmag * jnp.sin(k * th)
        taps = []
        for p in range(S5_T + 1):
            ar, ai = pw_re[p:p + 1, :], pw_im[p:p + 1, :]
            l_re, l_im = _cmul(bp_re, bp_im, ar, ai)
            v_re, v_im = _cmul(c_re, c_im, ar, ai)
            t_in = S5_T - 1 - p if d == 0 else p
            if 0 <= t_in < S5_T:
                w_ref[W_SF + d, t_in * LANE:(t_in + 1) * LANE, :] = (
                    jnp.concatenate([l_re, l_im], axis=1).astype(BF16))
            t_out = p - 1 if d == 0 else S5_T - p
            if 0 <= t_out < S5_T:
                w_ref[W_CF + d, t_out * LANE:(t_out + 1) * LANE, :] = (
                    jnp.concatenate([v_re, -v_im], axis=1).astype(BF16))
            if p < S5_T:
                l_hi, l_lo = _split_bf16(jnp.concatenate([l_re, -l_im], axis=1), 2)
                nt = lambda a, b: lax.dot_general(a, b, (((1,), (1,)), ((), ())),
                                                  preferred_element_type=F32)
                taps.append(nt(l_hi, c_hi) + nt(l_hi, c_lo) + nt(l_lo, c_hi))
        kern.append(taps)
        r = lax.broadcasted_iota(jnp.int32, (S5_R, S5_TILE_STATE), 0).astype(F32) * float(S5_T)
        r1 = r + float(S5_T)
        pm = jnp.exp(r * lam)
        qm = jnp.exp(-(r1 * lam))
        tab8_ref[d, 0] = pm * jnp.cos(r * th)
        tab8_ref[d, 1] = pm * jnp.sin(r * th)
        tab8_ref[d, 2] = qm * jnp.cos(r1 * th)
        tab8_ref[d, 3] = -(qm * jnp.sin(r1 * th))
        mc = jnp.exp(float(CHUNK) * lam)
        tab1_ref[d, 0:1, :] = mc * jnp.cos(float(CHUNK) * th)
        tab1_ref[d, 1:2, :] = mc * jnp.sin(float(CHUNK) * th)
    row = lax.broadcasted_iota(jnp.int32, (LANE, LANE), 0)
    col = lax.broadcasted_iota(jnp.int32, (LANE, LANE), 1)
    skip = jnp.where(row == col, d_ref[...], 0.0)
    for t in range(S5_T):
        for t2 in range(S5_T):
            if t < t2:
                blk = kern[0][t2 - t]
            elif t > t2:
                blk = kern[1][t - t2]
            else:
                blk = kern[0][0] + kern[1][0] + skip
            w_ref[W_M, t * LANE:(t + 1) * LANE, t2 * LANE:(t2 + 1) * LANE] = blk.astype(BF16)


def _s5_prep(a_re, a_im, ldt, b_re, b_im, c_re, c_im, dsk):
    vec = pl.BlockSpec((None, 2, 1, S5_TILE_STATE), lambda l, j: (l, 0, 0, j))
    blk = pl.BlockSpec((None, None, LANE, S5_TILE_STATE), lambda l, j: (l, j, 0, 0))
    return pl.pallas_call(
        _s5_prep_kernel,
        grid=(DEPTH, S5_TILES),
        in_specs=[vec, vec, vec, blk, blk, blk, blk,
                  pl.BlockSpec((None, 1, LANE), lambda l, j: (l, 0, j))],
        out_specs=[
            pl.BlockSpec((None, None, 5, S5_ROW, S5_W), lambda l, j: (l, j, 0, 0, 0)),
            pl.BlockSpec((None, None, 2, 4, S5_R, S5_TILE_STATE), lambda l, j: (l, j, 0, 0, 0, 0)),
            pl.BlockSpec((None, None, 2, 2, S5_TILE_STATE), lambda l, j: (l, j, 0, 0, 0)),
        ],
        out_shape=[
            jax.ShapeDtypeStruct((DEPTH, S5_TILES, 5, S5_ROW, S5_W), BF16),
            jax.ShapeDtypeStruct((DEPTH, S5_TILES, 2, 4, S5_R, S5_TILE_STATE), F32),
            jax.ShapeDtypeStruct((DEPTH, S5_TILES, 2, 2, S5_TILE_STATE), F32),
        ],
        compiler_params=_params(("parallel", "parallel")),
        name="s5_prep",
    )(a_re, a_im, ldt, b_re, b_im, c_re, c_im, dsk)


def _s5_scan_kernel(ulo_ref, uhi_ref, x0_ref, w_ref, tab8_ref, tab1_ref, y_ref, fs_ref, u_s, *, nseq, nb):
    groups = nseq * nb
    nrow = groups * S5_R
    ts = S5_TILE_STATE
    off = ZS_U % LANE
    u_s[...] = jnp.concatenate([ulo_ref[:, off:], uhi_ref[:, :off]], axis=1).astype(F32)
    u8 = jnp.concatenate([u_s[pl.ds(t, nrow, stride=S5_T), :] for t in range(S5_T)],
                         axis=1).astype(BF16)
    ef = jnp.dot(u8, w_ref[W_SF], preferred_element_type=F32).reshape(groups, S5_R, S5_W)
    eb = jnp.dot(u8, w_ref[W_SB], preferred_element_type=F32).reshape(groups, S5_R, S5_W)
    rowi = lax.broadcasted_iota(jnp.int32, (groups, S5_R, ts), 1)

    def prefix(x):
        for s in (1, 2, 4):
            x = x + jnp.where(rowi >= s, pltpu.roll(x, s, 1), 0.0)
        return x

    def suffix(x):
        for s in (1, 2, 4):
            x = x + jnp.where(rowi < S5_R - s, pltpu.roll(x, S5_R - s, 1), 0.0)
        return x

    p_re, p_im, q_re, q_im = (tab8_ref[0, i] for i in range(4))
    a_re, a_im = tab1_ref[0, 0:1, :], tab1_ref[0, 1:2, :]
    w_re, w_im = _cmul(q_re, q_im, ef[:, :, :ts], ef[:, :, ts:])
    cs_re, cs_im = prefix(w_re), prefix(w_im)
    st_re, st_im = [], []
    for s in range(nseq):
        x_re, x_im = x0_ref[s, 0, 0:1, :], x0_ref[s, 0, 1:2, :]
        for b in range(nb):
            g = s * nb + b
            st_re.append(x_re)
            st_im.append(x_im)
            x_re, x_im = _cmul(a_re, a_im, x_re + cs_re[g, S5_R - 1:S5_R, :],
                               x_im + cs_im[g, S5_R - 1:S5_R, :])
        fs_ref[s, 0, 0:1, :] = x_re
        fs_ref[s, 0, 1:2, :] = x_im
    xin_re, xin_im = _cmul(p_re, p_im, cs_re - w_re + jnp.stack(st_re), cs_im - w_im + jnp.stack(st_im))
    xin = jnp.concatenate([xin_re, xin_im], axis=2).reshape(nrow, S5_W)

    p_re, p_im, q_re, q_im = (tab8_ref[1, i] for i in range(4))
    a_re, a_im = tab1_ref[1, 0:1, :], tab1_ref[1, 1:2, :]
    w_re, w_im = _cmul(p_re, p_im, eb[:, :, :ts], eb[:, :, ts:])
    sf_re, sf_im = suffix(w_re), suffix(w_im)
    z_re, z_im = [None] * groups, [None] * groups
    for s in range(nseq):
        x_re, x_im = x0_ref[s, 1, 0:1, :], x0_ref[s, 1, 1:2, :]
        for b in reversed(range(nb)):
            g = s * nb + b
            z_re[g], z_im[g] = _cmul(a_re, a_im, x_re, x_im)
            x_re = sf_re[g, 0:1, :] + z_re[g]
            x_im = sf_im[g, 0:1, :] + z_im[g]
        fs_ref[s, 1, 0:1, :] = x_re
        fs_ref[s, 1, 1:2, :] = x_im
    xnx_re, xnx_im = _cmul(q_re, q_im, sf_re - w_re + jnp.stack(z_re), sf_im - w_im + jnp.stack(z_im))
    xnx = jnp.concatenate([xnx_re, xnx_im], axis=2).reshape(nrow, S5_W)

    y8 = (jnp.dot(u8, w_ref[W_M], preferred_element_type=F32)
          + _dot_nt(xin, w_ref[W_CF]) + _dot_nt(xnx, w_ref[W_CB]))
    for t in range(S5_T):
        y_ref[pl.ds(t, nrow, stride=S5_T), :] = y8[:, t * LANE:(t + 1) * LANE]


def _s5_scan(zs, x0, x0_block, x0_idx, layer, wmat, tab8, tab1, nseq, seq):
    n = nseq * seq
    return pl.pallas_call(
        functools.partial(_s5_scan_kernel, nseq=nseq, nb=seq // CHUNK),
        grid=(S5_TILES,),
        in_specs=[
            pl.BlockSpec((n, LANE), lambda j: (0, ZS_U // LANE + j)),
            pl.BlockSpec((n, LANE), lambda j: (0, ZS_U // LANE + j + 1)),
            pl.BlockSpec(x0_block, x0_idx),
            pl.BlockSpec((None, None, 5, S5_ROW, S5_W), lambda j: (layer, j, 0, 0, 0)),
            pl.BlockSpec((None, None, 2, 4, S5_R, S5_TILE_STATE), lambda j: (layer, j, 0, 0, 0, 0)),
            pl.BlockSpec((None, None, 2, 2, S5_TILE_STATE), lambda j: (layer, j, 0, 0, 0)),
        ],
        out_specs=[
            pl.BlockSpec((n, LANE), lambda j: (0, j)),
            pl.BlockSpec((nseq, 2, 2, S5_TILE_STATE), lambda j: (0, 0, 0, j)),
        ],
        out_shape=[
            jax.ShapeDtypeStruct((n, S5_WIDTH), F32),
            jax.ShapeDtypeStruct((nseq, 2, 2, S5_NSTATE), F32),
        ],
        scratch_shapes=[pltpu.VMEM((n, LANE), F32)],
        compiler_params=_params(("parallel",)),
        name="s5_scan",
    )(zs, zs, x0, wmat, tab8, tab1)


def _merge_kernel(x_ref, mod_ref, nw_ref, oa_ref, ob_ref, ys_ref, zs_ref, wglu_ref, bglu_ref, wmg_ref,
                  wa_ref, wb_ref, wc_ref, wout_ref, y_ref):
    x = x_ref[...]
    mod = mod_ref[0]
    h = _mod_rmsnorm(x, nw_ref[...], mod).astype(BF16)
    g = jax.nn.sigmoid(jnp.dot(h, wmg_ref[...], preferred_element_type=F32))
    zg = _dot(jax.nn.gelu(ys_ref[...]), wglu_ref[...]) + bglu_ref[...]
    oc = (zg[:, :S5_WIDTH] * jax.nn.sigmoid(zg[:, S5_WIDTH:])
          * jax.nn.silu(zs_ref[:, ZS_GATE:ZS_GATE + S5_WIDTH].astype(F32)))
    mixed = (g[:, 0:D_MODEL] * _dot(oa_ref[...], wa_ref[...])
             + g[:, D_MODEL:2 * D_MODEL] * _dot(ob_ref[...], wb_ref[...])
             + g[:, 2 * D_MODEL:] * _dot(oc, wc_ref[...]))
    y_ref[...] = x + mod[:, 2 * D_MODEL:] * _dot(mixed, wout_ref[...])


def _merge(x2, mod, mod_idx, nw, oa, ob, ys, zs, layer, wglu, bglu, wmg, wa, wb, wc, wout):
    n = x2.shape[0]
    tm = ROW_TILE
    const = lambda i: (0, 0)
    rows = lambda w: pl.BlockSpec((tm, w), lambda i: (i, 0))
    return pl.pallas_call(
        _merge_kernel,
        grid=(n // tm,),
        in_specs=[
            rows(D_MODEL),
            pl.BlockSpec((1, 1, 3 * D_MODEL), lambda i: (mod_idx(i), 0, 0)),
            pl.BlockSpec((1, D_MODEL), const),
            rows(GLA_WIDTH), rows(MLA_WIDTH), rows(S5_WIDTH), rows(ZS_W),
            pl.BlockSpec((None, S5_WIDTH, 2 * S5_WIDTH), lambda i: (layer, 0, 0)),
            pl.BlockSpec((None, 1, 2 * S5_WIDTH), lambda i: (layer, 0, 0)),
            pl.BlockSpec((None, D_MODEL, 3 * D_MODEL), lambda i: (layer, 0, 0)),
            pl.BlockSpec((GLA_WIDTH, D_MODEL), const),
            pl.BlockSpec((MLA_WIDTH, D_MODEL), const),
            pl.BlockSpec((S5_WIDTH, D_MODEL), const),
            pl.BlockSpec((D_MODEL, D_MODEL), const),
        ],
        out_specs=rows(D_MODEL),
        out_shape=jax.ShapeDtypeStruct((n, D_MODEL), F32),
        compiler_params=_params(("parallel",)),
        name="merge",
    )(x2, mod, nw, oa, ob, ys, zs, wglu, bglu, wmg, wa, wb, wc, wout)


def _pad_heads(w, width):
    lead = w.shape[:-1]
    w = w.reshape(lead + (MLA_HEADS, width))
    w = jnp.pad(w, [(0, 0)] * len(lead) + [(0, 0), (0, HEAD_PAD - width)])
    return w.reshape(lead + (MLA_HEADS * HEAD_PAD,))


def _rope_tables(n_tok):
    rows = n_tok // GRID_W
    r = jnp.repeat(jnp.arange(rows, dtype=F32), GRID_W)
    col = jnp.tile(jnp.arange(GRID_W, dtype=F32), rows)
    n_freq = MLA_ROPE // 4
    inv = ROPE_THETA ** (-jnp.arange(n_freq, dtype=F32) / n_freq)
    ang = jnp.concatenate([r[:, None] * inv, col[:, None] * inv], axis=-1)
    cos, sin = jnp.cos(ang), jnp.sin(ang)
    half = MLA_ROPE // 2
    z = lambda w: jnp.zeros((n_tok, w), F32)
    tail = HEAD_PAD - MLA_QK
    c = jnp.concatenate([jnp.ones((n_tok, MLA_NOPE), F32), cos, cos, z(tail)], axis=1)
    s_up = jnp.concatenate([z(MLA_NOPE + half), sin, z(tail)], axis=1)
    s_dn = jnp.concatenate([z(MLA_NOPE), -sin, z(half + tail)], axis=1)
    return jnp.stack([c, s_up, s_dn])


def _s5_group_blocks(x):
    lyr = x.shape[0]
    x = x.reshape(lyr, S5_TILES, 8, S5_GROUP, S5_STATE)
    eye = jnp.eye(8, dtype=x.dtype)
    blk = x[:, :, :, :, None, :] * eye[None, None, :, None, :, None]
    return blk.reshape(lyr, S5_TILES, LANE, S5_TILE_STATE)


def kernel(x_prompt, x_sample, c, c_ctx, cache_mla_ckv, cache_mla_krope, state_gla, state_s5,
           norm_w, w_ada, b_ada, w_in, gla_w_a2, gla_b_a, gla_o_norm,
           mla_q_norm, mla_w_uq, mla_kv_norm, mla_w_uk, mla_w_uv, mla_qh_norm, mla_kh_norm,
           s5_a_re, s5_a_im, s5_log_dt, s5_b_re, s5_b_im, s5_c_re, s5_c_im, s5_d, s5_w_glu, s5_b_glu,
           w_bo_gla, w_bo_mla, w_bo_s5, w_out):
    bsz, seq, _ = x_prompt.shape
    dbsz, dseq, _ = x_sample.shape
    ctx_row = 8 - 1
    assert dbsz <= ctx_row and (bsz * seq) % ROW_TILE == 0 and dseq % ROW_TILE == 0

    cond8 = jnp.zeros((8, D_MODEL), F32).at[0:dbsz].set(c).at[ctx_row].set(c_ctx)
    ada = _ada(cond8, w_ada, b_ada)

    vec = lambda a: a.reshape(DEPTH, 2, 1, S5_NSTATE)
    ldt = jnp.repeat(s5_log_dt[..., None], S5_STATE, axis=-1)
    bt = lambda b: _s5_group_blocks(b.transpose(0, 1, 3, 2))
    wmat, tab8, tab1 = _s5_prep(vec(s5_a_re), vec(s5_a_im), vec(ldt), bt(s5_b_re), bt(s5_b_im),
                                _s5_group_blocks(s5_c_re), _s5_group_blocks(s5_c_im),
                                s5_d.reshape(DEPTH, 1, S5_WIDTH))
    wglu = s5_w_glu.astype(BF16)
    bglu = s5_b_glu.reshape(DEPTH, 1, 2 * S5_WIDTH)

    wuq = _pad_heads(mla_w_uq, MLA_QK).astype(BF16)
    wuk = _pad_heads(mla_w_uk, MLA_NOPE).astype(BF16)
    wuv = mla_w_uv.astype(BF16)
    qhn = jnp.pad(mla_qh_norm, ((0, 0), (0, HEAD_PAD - MLA_QK)))
    khn = jnp.pad(mla_kh_norm, ((0, 0), (0, HEAD_PAD - MLA_QK)))
    e_np = np.zeros((2, LANE, MLA_HEADS * HEAD_PAD), np.float32)
    for src, lane0 in enumerate((ZM_KR - ZM_KR_TILE, 0)):
        for h in range(MLA_HEADS):
            for i in range(MLA_ROPE):
                e_np[src, lane0 + i, h * HEAD_PAD + MLA_NOPE + i] = 1.0
    e_place = jnp.asarray(e_np, BF16)
    rope_tab = _rope_tables(dseq)
    ckr_pad = jnp.pad(cache_mla_krope, ((0, 0), (0, 0), (0, 0), (0, LANE - MLA_ROPE)))

    zrow = lambda n: jnp.zeros((DEPTH, n, GLA_QK), F32)
    waf = jnp.concatenate([gla_w_a2[:, 0], zrow(LANE - GLA_RANK)], axis=1).astype(BF16)
    wab = jnp.concatenate([zrow(GLA_RANK), gla_w_a2[:, 1], zrow(LANE - 2 * GLA_RANK)], axis=1).astype(BF16)
    sgla = state_gla.reshape(dbsz, DEPTH, 2, GLA_QK, GLA_DV)
    ss5 = state_s5.reshape(dbsz, DEPTH, 2, 2, S5_NSTATE)
    zero_gla = jnp.zeros((2, GLA_QK, GLA_DV), F32)
    zero_s5 = jnp.zeros((bsz, 2, 2, S5_NSTATE), F32)

    hp = x_prompt.reshape(bsz * seq, D_MODEL)
    hs = x_sample.reshape(dbsz * dseq, D_MODEL)
    ckv_l, krope_l, gla_l, s5_l = [], [], [], []
    wmg = w_in[:, :, MERGE_COL:].astype(BF16)
    for l in range(DEPTH):
        mod = ada[l].reshape(8, 1, 3 * D_MODEL)
        nw = norm_w[l].reshape(1, D_MODEL)
        mla_w = (mla_q_norm[l].reshape(1, -1), wuq[l], mla_kv_norm[l].reshape(1, -1), wuk[l], wuv[l],
                 qhn[l].reshape(1, -1), khn[l].reshape(1, -1), e_place[0], e_place[1])
        wbo = (w_bo_gla[l].astype(BF16), w_bo_mla[l].astype(BF16), w_bo_s5[l].astype(BF16))
        wout = w_out[l].astype(BF16)
        onorm = gla_o_norm[l].reshape(1, GLA_DV)

        def layer(x2, nb, n, mod_idx, ctx):
            zg, zm, zs = _in_proj(x2, mod, mod_idx, nw, w_in, l)
            if ctx:
                s0, s0_idx = sgla, (lambda b: (b, l, 0, 0, 0))
                x0, x0_blk = ss5, (nb, None, 2, 2, S5_TILE_STATE)
                x0_idx = lambda j: (0, l, 0, 0, j)
                mctx, rt = (cache_mla_ckv, ckr_pad, l), rope_tab
            else:
                s0, s0_idx = zero_gla, (lambda b: (0, 0, 0))
                x0, x0_blk = zero_s5, (nb, 2, 2, S5_TILE_STATE)
                x0_idx = lambda j: (0, 0, 0, j)
                mctx, rt = None, None
            oa, st_gla = _gla(zg, s0, s0_idx, waf[l], wab[l], gla_b_a[l], onorm, nb, n)
            ob, ckv = _mla(zm, mctx, mla_w, rt, nb, n)
            y_ssm, st_s5 = _s5_scan(zs, x0, x0_blk, x0_idx, l, wmat, tab8, tab1, nb, n)
            y = _merge(x2, mod, mod_idx, nw, oa, ob, y_ssm, zs, l, wglu, bglu, wmg, *wbo, wout)
            return y, zm, ckv, st_gla, st_s5

        hp, zm_p, ckv_p, st_gla_p, st_s5_p = layer(hp, bsz, seq, lambda i: ctx_row, False)
        ckv_l.append(ckv_p.reshape(bsz, seq, MLA_KV_LORA))
        krope_l.append(zm_p[:, ZM_KR:ZM_KR + MLA_ROPE].astype(F32).reshape(bsz, seq, MLA_ROPE))
        gla_l.append(st_gla_p.reshape(bsz, 2, GLA_HEADS, GLA_DK, GLA_DV))
        s5_l.append(st_s5_p.reshape(bsz, 2, 2, S5_GROUPS, S5_STATE))
        blocks_per_seq = dseq // ROW_TILE
        hs = layer(hs, dbsz, dseq, lambda i: i // blocks_per_seq, True)[0]

    return (hp.reshape(bsz, seq, D_MODEL), hs.reshape(dbsz, dseq, D_MODEL),
            jnp.stack(ckv_l, axis=1), jnp.stack(krope_l, axis=1),
            jnp.stack(gla_l, axis=1), jnp.stack(s5_l, axis=1))
```

```python
import functools

import jax
import jax.numpy as jnp
import numpy as np
from jax import lax
from jax.experimental import pallas as pl
from jax.experimental.pallas import tpu as pltpu

F32 = jnp.float32
BF16 = jnp.bfloat16

EPS = 1e-6
D_MODEL = 1024
DEPTH = 2
GRID_W = 64
ROPE_THETA = 10000.0
GLA_HEADS = 4
GLA_DK = 64
GLA_DV = 128
GLA_RANK = 16
GLA_GATE_NORM = 16.0
GLA_QK = GLA_HEADS * GLA_DK
GLA_WIDTH = GLA_HEADS * GLA_DV
MLA_HEADS = 4
MLA_Q_LORA = 384
MLA_KV_LORA = 256
MLA_NOPE = 64
MLA_ROPE = 32
MLA_QK = MLA_NOPE + MLA_ROPE
MLA_DV = 128
MLA_WIDTH = MLA_HEADS * MLA_DV
S5_WIDTH = 512
S5_GROUP = 16
S5_GROUPS = 32
S5_STATE = 64
S5_NSTATE = S5_GROUPS * S5_STATE

LANE = 128
HEAD_PAD = LANE
CHUNK = 64
GLA_STEP = 256
S5_TILES = S5_WIDTH // LANE
S5_TILE_STATE = S5_NSTATE // S5_TILES
ROW_TILE = 512
Q_TILE = 256
VMEM_LIMIT = 56 * 1024 * 1024

MERGE_COL = 3776
ZG_Q, ZG_K, ZG_V, ZG_A, ZG_GATE, ZG_W = 0, 256, 512, 1024, 1152, 1664
ZM_Q, ZM_KV, ZM_KR, ZM_GATE, ZM_W = 0, 384, 640, 768, 1280
ZS_U, ZS_GATE, ZS_W = 0, 512, 1024
ZG_BASE, ZM_BASE, ZS_BASE, PACK_W = 0, ZG_W, ZG_W + ZM_W, ZG_W + ZM_W + ZS_W
IN_PIECES = (
    (0, 8, ZG_BASE + ZG_Q, None),
    (1024, 1, ZG_BASE + ZG_A, 2 * GLA_RANK),
    (1056, 4, ZG_BASE + ZG_GATE, None),
    (1568, 5, ZM_BASE + ZM_Q, None),
    (2208, 1, ZM_BASE + ZM_KR, MLA_ROPE),
    (2240, 4, ZM_BASE + ZM_GATE, None),
    (2752, 8, ZS_BASE + ZS_U, None),
)


def _dot(a, b):
    return jnp.dot(a.astype(BF16), b.astype(BF16), preferred_element_type=F32)


def _dot_nt(a, b):
    return lax.dot_general(a.astype(BF16), b.astype(BF16), (((1,), (1,)), ((), ())),
                           preferred_element_type=F32)


def _split_bf16(x, parts):
    out = []
    r = x
    for _ in range(parts):
        p = r.astype(BF16)
        out.append(p)
        r = r - p.astype(F32)
    return out


def _params(sem):
    return pltpu.CompilerParams(dimension_semantics=sem, vmem_limit_bytes=VMEM_LIMIT)


def _ada_kernel(c_ref, w_ref, b_ref, o_ref):
    s = jax.nn.silu(c_ref[...])
    o_ref[...] = _dot(s, w_ref[...]) + b_ref[...]


def _ada(cond8, w_ada, b_ada):
    tn = 1024
    return pl.pallas_call(
        _ada_kernel,
        grid=(DEPTH, 3 * D_MODEL // tn),
        in_specs=[
            pl.BlockSpec((8, D_MODEL), lambda l, n: (0, 0)),
            pl.BlockSpec((None, D_MODEL, tn), lambda l, n: (l, 0, n)),
            pl.BlockSpec((None, 1, tn), lambda l, n: (l, 0, n)),
        ],
        out_specs=pl.BlockSpec((None, 8, tn), lambda l, n: (l, 0, n)),
        out_shape=jax.ShapeDtypeStruct((DEPTH, 8, 3 * D_MODEL), F32),
        compiler_params=_params(("parallel", "parallel")),
        name="ada",
    )(cond8, w_ada, b_ada.reshape(DEPTH, 1, 3 * D_MODEL))


def _mod_rmsnorm(x, nw, mod):
    ms = jnp.mean(x * x, axis=-1, keepdims=True)
    y = x * lax.rsqrt(ms + EPS) * nw
    return y * (1.0 + mod[:, D_MODEL:2 * D_MODEL]) + mod[:, 0:D_MODEL]


def _pack_transposed(w_ref, wb_s, src, tiles, dst, keep):
    lane = lax.broadcasted_iota(jnp.int32, (D_MODEL, LANE), 1)
    for t in range(tiles):
        blk = w_ref[src + t * LANE:src + (t + 1) * LANE, :].T
        if keep is not None:
            blk = jnp.where(lane < keep, blk, 0.0)
        wb_s[:, dst + t * LANE:dst + (t + 1) * LANE] = blk.astype(BF16)


def _in_proj_kernel(x_ref, mod_ref, nw_ref, w_ref, zg_ref, zm_ref, zs_ref, wb_s):
    @pl.when(pl.program_id(0) == 0)
    def _():
        for src, tiles, dst, keep in IN_PIECES:
            _pack_transposed(w_ref, wb_s, src, tiles, dst, keep)

    h = _mod_rmsnorm(x_ref[...], nw_ref[...], mod_ref[0]).astype(BF16)
    z = jnp.dot(h, wb_s[...], preferred_element_type=F32)
    zg_ref[...] = z[:, ZG_BASE:ZG_BASE + ZG_W].astype(BF16)
    zm_ref[...] = z[:, ZM_BASE:ZM_BASE + ZM_W].astype(BF16)
    zs_ref[...] = z[:, ZS_BASE:ZS_BASE + ZS_W].astype(BF16)


def _in_proj(x2, mod, mod_idx, nw, w_in_t, layer):
    n = x2.shape[0]
    tm = ROW_TILE
    const = lambda i: (0, 0)
    return pl.pallas_call(
        _in_proj_kernel,
        grid=(n // tm,),
        in_specs=[
            pl.BlockSpec((tm, D_MODEL), lambda i: (i, 0)),
            pl.BlockSpec((1, 1, 3 * D_MODEL), lambda i: (mod_idx(i), 0, 0)),
            pl.BlockSpec((1, D_MODEL), const),
            pl.BlockSpec((None, MERGE_COL, D_MODEL), lambda i: (layer, 0, 0), pipeline_mode=pl.Buffered(1)),
        ],
        out_specs=[
            pl.BlockSpec((tm, ZG_W), lambda i: (i, 0)),
            pl.BlockSpec((tm, ZM_W), lambda i: (i, 0)),
            pl.BlockSpec((tm, ZS_W), lambda i: (i, 0)),
        ],
        out_shape=[
            jax.ShapeDtypeStruct((n, ZG_W), BF16),
            jax.ShapeDtypeStruct((n, ZM_W), BF16),
            jax.ShapeDtypeStruct((n, ZS_W), BF16),
        ],
        scratch_shapes=[pltpu.VMEM((D_MODEL, PACK_W), BF16)],
        compiler_params=_params(("arbitrary",)),
        name="in_proj",
    )(x2, mod, nw, w_in_t)


def _gla_kernel(zg_ref, s0_ref, waf_ref, wab_ref, ba_ref, onorm_ref, o_ref, sfin_ref,
                la_s, o_s, st_s, *, nsteps):
    a_blk = zg_ref[:, ZG_A:ZG_A + LANE]
    inv_norm = 1.0 / GLA_GATE_NORM
    la_s[0] = jax.nn.log_sigmoid(_dot(a_blk, waf_ref[...]) + ba_ref[0:1, :]) * inv_norm
    la_s[1] = jax.nn.log_sigmoid(_dot(a_blk, wab_ref[...]) + ba_ref[1:2, :]) * inv_norm
    zero_blk = jnp.zeros((GLA_DK, GLA_DV), F32)
    for d in (0, 1):
        s0 = s0_ref[d]
        rows_bd = []
        for h in range(GLA_HEADS):
            sh = s0[h * GLA_DK:(h + 1) * GLA_DK, :]
            rows_bd.append(jnp.concatenate([sh if h2 == h else zero_blk for h2 in range(GLA_HEADS)], axis=1))
        st_s[d] = jnp.concatenate(rows_bd, axis=0).T

    def iota(shape, axis, shift):
        return lax.shift_right_logical(lax.broadcasted_iota(jnp.int32, shape, axis), shift)

    log_chunk, log_dv = CHUNK.bit_length() - 1, GLA_DV.bit_length() - 1
    row = lax.broadcasted_iota(jnp.int32, (GLA_STEP, GLA_STEP), 0)
    col = lax.broadcasted_iota(jnp.int32, (GLA_STEP, GLA_STEP), 1)
    same_chunk = iota((GLA_STEP, GLA_STEP), 0, log_chunk) == iota((GLA_STEP, GLA_STEP), 1, log_chunk)
    masks = (same_chunk & (row >= col), same_chunk & (row <= col))
    lane_head = iota((GLA_STEP, GLA_QK), 1, log_chunk)
    row_chunk = iota((GLA_STEP, GLA_QK), 0, log_chunk)
    state_blk = iota((GLA_WIDTH, GLA_QK), 0, log_dv) == iota((GLA_WIDTH, GLA_QK), 1, log_chunk)
    qscale = GLA_DK ** -0.5
    nch = GLA_STEP // CHUNK

    def step(i, carry):
        for d in (0, 1):
            sc = i if d == 0 else nsteps - 1 - i
            rows = pl.ds(pl.multiple_of(sc * GLA_STEP, GLA_STEP), GLA_STEP)
            a_hi, a_lo = _split_bf16(la_s[d, rows, :], 2)
            tri = masks[d].astype(BF16)
            cum = (jnp.dot(tri, a_hi, preferred_element_type=F32)
                   + jnp.dot(tri, a_lo, preferred_element_type=F32))
            edge = CHUNK - 1 if d == 0 else 0
            blast = [cum[c * CHUNK + edge:c * CHUNK + edge + 1, :] for c in range(nch)]
            bl = jnp.concatenate([jnp.broadcast_to(b, (CHUNK, GLA_QK)) for b in blast], axis=0)
            q = zg_ref[rows, ZG_Q:ZG_Q + GLA_QK].astype(F32) * qscale
            k = zg_ref[rows, ZG_K:ZG_K + GLA_QK].astype(F32)
            v = zg_ref[rows, ZG_V:ZG_V + GLA_WIDTH]
            v_t = v.astype(F32).T.astype(BF16)
            qd = q * jnp.exp(cum)
            kd = (k * jnp.exp(-cum)).astype(BF16)
            kr = k * jnp.exp(bl - cum)
            outs = []
            for h in range(GLA_HEADS):
                qh = jnp.where(lane_head == h, qd, 0.0)
                att = jnp.where(masks[d], _dot_nt(qh, kd), 0.0)
                outs.append(_dot(att, v[:, h * GLA_DV:(h + 1) * GLA_DV]))
            s = st_s[d]
            inter = [None] * nch
            for c in (range(nch) if d == 0 else reversed(range(nch))):
                inter[c] = _dot_nt(qd[c * CHUNK:(c + 1) * CHUNK, :], s)
                kv_t = jnp.where(state_blk, _dot(v_t, jnp.where(row_chunk == c, kr, 0.0)), 0.0)
                s = s * jnp.exp(blast[c]) + kv_t
            st_s[d] = s
            o_s[d, rows, :] = jnp.concatenate(outs, axis=1) + jnp.concatenate(inter, axis=0)
        return carry

    lax.fori_loop(0, nsteps, step, 0)
    for d in (0, 1):
        s_fin = st_s[d].T
        for h in range(GLA_HEADS):
            sfin_ref[d, h * GLA_DK:(h + 1) * GLA_DK, :] = (
                s_fin[h * GLA_DK:(h + 1) * GLA_DK, h * GLA_DV:(h + 1) * GLA_DV])
    o = o_s[0] + o_s[1]
    gate = zg_ref[:, ZG_GATE:ZG_GATE + GLA_WIDTH].astype(F32)
    onorm = onorm_ref[...]
    for h in range(GLA_HEADS):
        vs = slice(h * GLA_DV, (h + 1) * GLA_DV)
        oh = o[:, vs]
        ms = jnp.mean(oh * oh, axis=-1, keepdims=True)
        o_ref[:, vs] = oh * lax.rsqrt(ms + EPS) * onorm * jax.nn.silu(gate[:, vs])


def _gla(zg, s0, s0_idx, waf, wab, ba, onorm, bsz, seq):
    const = lambda b: (0, 0)
    nsd = len(s0.shape)
    s0_block = (None,) * (nsd - 3) + (2, GLA_QK, GLA_DV)
    return pl.pallas_call(
        functools.partial(_gla_kernel, nsteps=seq // GLA_STEP),
        grid=(bsz,),
        in_specs=[
            pl.BlockSpec((seq, ZG_W), lambda b: (b, 0)),
            pl.BlockSpec(s0_block, s0_idx),
            pl.BlockSpec((LANE, GLA_QK), const),
            pl.BlockSpec((LANE, GLA_QK), const),
            pl.BlockSpec((2, GLA_QK), const),
            pl.BlockSpec((1, GLA_DV), const),
        ],
        out_specs=[
            pl.BlockSpec((seq, GLA_WIDTH), lambda b: (b, 0)),
            pl.BlockSpec((None, 2, GLA_QK, GLA_DV), lambda b: (b, 0, 0, 0)),
        ],
        out_shape=[
            jax.ShapeDtypeStruct((bsz * seq, GLA_WIDTH), F32),
            jax.ShapeDtypeStruct((bsz, 2, GLA_QK, GLA_DV), F32),
        ],
        scratch_shapes=[
            pltpu.VMEM((2, seq, GLA_QK), F32),
            pltpu.VMEM((2, seq, GLA_WIDTH), F32),
            pltpu.VMEM((2, GLA_WIDTH, GLA_QK), F32),
        ],
        compiler_params=_params(("parallel",)),
        name="gla",
    )(zg, s0, waf, wab, ba, onorm)


def _rms(x, w):
    ms = jnp.mean(x * x, axis=-1, keepdims=True)
    return x * lax.rsqrt(ms + EPS) * w


def _head_norm(x, w, rope):
    outs = []
    for h in range(MLA_HEADS):
        xh = x[:, h * HEAD_PAD:(h + 1) * HEAD_PAD]
        ms = jnp.sum(xh * xh, axis=-1, keepdims=True) * (1.0 / MLA_QK)
        y = xh * lax.rsqrt(ms + EPS) * w
        if rope is not None:
            c, s_up, s_dn = rope
            half = MLA_ROPE // 2
            y = y * c + pltpu.roll(y, half, 1) * s_up + pltpu.roll(y, HEAD_PAD - half, 1) * s_dn
        outs.append(y)
    return outs


def _place_rope_key(kr, e):
    return sum(jnp.dot(p, e, preferred_element_type=F32) for p in _split_bf16(kr, 3))


def _mla_kernel(*refs, seq, n_ctx, use_rope):
    it = iter(refs)
    zm_ref = next(it)
    if n_ctx:
        cckv_ref, ckr_ref = next(it), next(it)
    qn_ref, wuq_ref, kvn_ref, wuk_ref, wuv_ref, qhn_ref, khn_ref, e_ref = (next(it) for _ in range(8))
    rope_ref = next(it) if use_rope else None
    o_ref, ckv_ref = next(it), next(it)
    q_s, k_s, v_s = next(it), next(it), next(it)

    rope = None
    if use_rope:
        rope = (rope_ref[0], rope_ref[1], rope_ref[2])
    e = e_ref[...]
    wuk = wuk_ref[...]
    wuv = wuv_ref[...]
    khn = khn_ref[...]

    cq = _rms(zm_ref[:, ZM_Q:ZM_Q + MLA_Q_LORA].astype(F32), qn_ref[...])
    qh = _head_norm(_dot(cq, wuq_ref[...]), qhn_ref[...], rope)
    qscale = MLA_QK ** -0.5
    for h in range(MLA_HEADS):
        q_s[:, h * HEAD_PAD:(h + 1) * HEAD_PAD] = (qh[h] * qscale).astype(BF16)

    ckv = _rms(zm_ref[:, ZM_KV:ZM_KV + MLA_KV_LORA].astype(F32), kvn_ref[...])
    ckv_ref[...] = ckv
    k_pe = jnp.dot(zm_ref[:, ZM_KR:ZM_KR + LANE], e, preferred_element_type=F32)
    kh = _head_norm(_dot(ckv, wuk) + k_pe, khn, rope)
    for h in range(MLA_HEADS):
        k_s[n_ctx:n_ctx + seq, h * HEAD_PAD:(h + 1) * HEAD_PAD] = kh[h].astype(BF16)
    v_s[n_ctx:n_ctx + seq, :] = _dot(ckv, wuv).astype(BF16)
    if n_ctx:
        cc = cckv_ref[...]
        kch = _head_norm(_dot(cc, wuk) + _place_rope_key(ckr_ref[...], e), khn, None)
        for h in range(MLA_HEADS):
            k_s[0:n_ctx, h * HEAD_PAD:(h + 1) * HEAD_PAD] = kch[h].astype(BF16)
        v_s[0:n_ctx, :] = _dot(cc, wuv).astype(BF16)

    def q_block(i, carry):
        rows = pl.ds(pl.multiple_of(i * Q_TILE, Q_TILE), Q_TILE)
        gate = zm_ref[rows, ZM_GATE:ZM_GATE + MLA_WIDTH].astype(F32)
        for h in range(MLA_HEADS):
            hs = slice(h * HEAD_PAD, (h + 1) * HEAD_PAD)
            s = lax.dot_general(q_s[rows, hs], k_s[:, hs], (((1,), (1,)), ((), ())),
                                preferred_element_type=F32)
            m = jnp.max(s, axis=-1, keepdims=True)
            p = jnp.exp(s - m)
            l = jnp.sum(p, axis=-1, keepdims=True)
            o = jnp.dot(p.astype(BF16), v_s[:, hs], preferred_element_type=F32) / l
            o_ref[rows, hs] = o * jax.nn.silu(gate[:, hs])
        return carry

    lax.fori_loop(0, seq // Q_TILE, q_block, 0)


def _mla(zm, ctx, w, rope_tab, bsz, seq):
    const2 = lambda b: (0, 0)
    n_ctx = 0 if ctx is None else ctx[0].shape[-2]
    in_specs = [pl.BlockSpec((seq, ZM_W), lambda b: (b, 0))]
    args = [zm]
    if ctx is not None:
        cckv, ckr, layer = ctx
        in_specs += [
            pl.BlockSpec((None, None, n_ctx, MLA_KV_LORA), lambda b: (b, layer, 0, 0)),
            pl.BlockSpec((None, None, n_ctx, LANE), lambda b: (b, layer, 0, 0)),
        ]
        args += [cckv, ckr]
    in_specs += [
        pl.BlockSpec((1, MLA_Q_LORA), const2),
        pl.BlockSpec((MLA_Q_LORA, MLA_HEADS * HEAD_PAD), const2),
        pl.BlockSpec((1, MLA_KV_LORA), const2),
        pl.BlockSpec((MLA_KV_LORA, MLA_HEADS * HEAD_PAD), const2),
        pl.BlockSpec((MLA_KV_LORA, MLA_WIDTH), const2),
        pl.BlockSpec((1, HEAD_PAD), const2),
        pl.BlockSpec((1, HEAD_PAD), const2),
        pl.BlockSpec((LANE, MLA_HEADS * HEAD_PAD), const2),
    ]
    args += list(w)
    if rope_tab is not None:
        in_specs.append(pl.BlockSpec((3, seq, HEAD_PAD), lambda b: (0, 0, 0)))
        args.append(rope_tab)
    return pl.pallas_call(
        functools.partial(_mla_kernel, seq=seq, n_ctx=n_ctx, use_rope=rope_tab is not None),
        grid=(bsz,),
        in_specs=in_specs,
        out_specs=[
            pl.BlockSpec((seq, MLA_WIDTH), lambda b: (b, 0)),
            pl.BlockSpec((seq, MLA_KV_LORA), lambda b: (b, 0)),
        ],
        out_shape=[
            jax.ShapeDtypeStruct((bsz * seq, MLA_WIDTH), F32),
            jax.ShapeDtypeStruct((bsz * seq, MLA_KV_LORA), F32),
        ],
        scratch_shapes=[
            pltpu.VMEM((seq, MLA_HEADS * HEAD_PAD), BF16),
            pltpu.VMEM((n_ctx + seq, MLA_HEADS * HEAD_PAD), BF16),
            pltpu.VMEM((n_ctx + seq, MLA_WIDTH), BF16),
        ],
        compiler_params=_params(("parallel",)),
        name="mla",
    )(*args)


S5_T = 8
S5_R = CHUNK // S5_T
S5_ROW = S5_T * LANE
S5_W = 2 * S5_TILE_STATE
W_M, W_SF, W_SB, W_CF, W_CB = range(5)


def _cmul(ar, ai, br, bi):
    return ar * br - ai * bi, ar * bi + ai * br


def _s5_prep_kernel(are_ref, aim_ref, ldt_ref, bre_ref, bim_ref, cre_ref, cim_ref, d_ref,
                    w_ref, tab8_ref, tab1_ref):
    b_re, b_im = bre_ref[...], bim_ref[...]
    c_re, c_im = cre_ref[...], cim_ref[...]
    c_hi, c_lo = _split_bf16(jnp.concatenate([c_re, c_im], axis=1), 2)
    kern = []
    for d in (0, 1):
        a_re, a_im = are_ref[d], aim_ref[d]
        dt = jnp.exp(ldt_ref[d])
        lam = a_re * dt
        th = a_im * dt
        mag = jnp.exp(lam)
        ab_re = mag * jnp.cos(th)
        ab_im = mag * jnp.sin(th)
        den = a_re * a_re + a_im * a_im
        n_re = ab_re - 1.0
        cf_re = (n_re * a_re + ab_im * a_im) / den
        cf_im = (ab_im * a_re - n_re * a_im) / den
        bp_re, bp_im = _cmul(b_re, b_im, cf_re, cf_im)
        k = lax.broadcasted_iota(jnp.int32, (2 * S5_T, S5_TILE_STATE), 0).astype(F32)
        pmag = jnp.exp(k * lam)
        pw_re = pmag * jnp.cos(k * th)
        pw_im = pmag * jnp.sin(k * th)
        taps = []
        for p in range(S5_T + 1):
            ar, ai = pw_re[p:p + 1, :], pw_im[p:p + 1, :]
            l_re, l_im = _cmul(bp_re, bp_im, ar, ai)
            v_re, v_im = _cmul(c_re, c_im, ar, ai)
            t_in = S5_T - 1 - p if d == 0 else p
            if 0 <= t_in < S5_T:
                w_ref[W_SF + d, t_in * LANE:(t_in + 1) * LANE, :] = (
                    jnp.concatenate([l_re, l_im], axis=1).astype(BF16))
            t_out = p - 1 if d == 0 else S5_T - p
            if 0 <= t_out < S5_T:
                w_ref[W_CF + d, t_out * LANE:(t_out + 1) * LANE, :] = (
                    jnp.concatenate([v_re, -v_im], axis=1).astype(BF16))
            if p < S5_T:
                l_hi, l_lo = _split_bf16(jnp.concatenate([l_re, -l_im], axis=1), 2)
                nt = lambda a, b: lax.dot_general(a, b, (((1,), (1,)), ((), ())),
                                                  preferred_element_type=F32)
                taps.append(nt(l_hi, c_hi) + nt(l_hi, c_lo) + nt(l_lo, c_hi))
        kern.append(taps)
        r = lax.broadcasted_iota(jnp.int32, (S5_R, S5_TILE_STATE), 0).astype(F32) * float(S5_T)
        r1 = r + float(S5_T)
        pm = jnp.exp(r * lam)
        qm = jnp.exp(-(r1 * lam))
        tab8_ref[d, 0] = pm * jnp.cos(r * th)
        tab8_ref[d, 1] = pm * jnp.sin(r * th)
        tab8_ref[d, 2] = qm * jnp.cos(r1 * th)
        tab8_ref[d, 3] = -(qm * jnp.sin(r1 * th))
        mc = jnp.exp(float(CHUNK) * lam)
        tab1_ref[d, 0:1, :] = mc * jnp.cos(float(CHUNK) * th)
        tab1_ref[d, 1:2, :] = mc * jnp.sin(float(CHUNK) * th)
    row = lax.broadcasted_iota(jnp.int32, (LANE, LANE), 0)
    col = lax.broadcasted_iota(jnp.int32, (LANE, LANE), 1)
    skip = jnp.where(row == col, d_ref[...], 0.0)
    for t in range(S5_T):
        for t2 in range(S5_T):
            if t < t2:
                blk = kern[0][t2 - t]
            elif t > t2:
                blk = kern[1][t - t2]
            else:
                blk = kern[0][0] + kern[1][0] + skip
            w_ref[W_M, t * LANE:(t + 1) * LANE, t2 * LANE:(t2 + 1) * LANE] = blk.astype(BF16)


def _s5_prep(a_re, a_im, ldt, b_re, b_im, c_re, c_im, dsk):
    vec = pl.BlockSpec((None, 2, 1, S5_TILE_STATE), lambda l, j: (l, 0, 0, j))
    blk = pl.BlockSpec((None, None, LANE, S5_TILE_STATE), lambda l, j: (l, j, 0, 0))
    return pl.pallas_call(
        _s5_prep_kernel,
        grid=(DEPTH, S5_TILES),
        in_specs=[vec, vec, vec, blk, blk, blk, blk,
                  pl.BlockSpec((None, 1, LANE), lambda l, j: (l, 0, j))],
        out_specs=[
            pl.BlockSpec((None, None, 5, S5_ROW, S5_W), lambda l, j: (l, j, 0, 0, 0)),
            pl.BlockSpec((None, None, 2, 4, S5_R, S5_TILE_STATE), lambda l, j: (l, j, 0, 0, 0, 0)),
            pl.BlockSpec((None, None, 2, 2, S5_TILE_STATE), lambda l, j: (l, j, 0, 0, 0)),
        ],
        out_shape=[
            jax.ShapeDtypeStruct((DEPTH, S5_TILES, 5, S5_ROW, S5_W), BF16),
            jax.ShapeDtypeStruct((DEPTH, S5_TILES, 2, 4, S5_R, S5_TILE_STATE), F32),
            jax.ShapeDtypeStruct((DEPTH, S5_TILES, 2, 2, S5_TILE_STATE), F32),
        ],
        compiler_params=_params(("parallel", "parallel")),
        name="s5_prep",
    )(a_re, a_im, ldt, b_re, b_im, c_re, c_im, dsk)


def _s5_scan_kernel(u_ref, x0_ref, w_ref, tab8_ref, tab1_ref, y_ref, fs_ref, u_s, *, nseq, nb):
    groups = nseq * nb
    nrow = groups * S5_R
    ts = S5_TILE_STATE
    u_s[...] = u_ref[...].astype(F32)
    u8 = jnp.concatenate([u_s[pl.ds(t, nrow, stride=S5_T), :] for t in range(S5_T)],
                         axis=1).astype(BF16)
    ef = jnp.dot(u8, w_ref[W_SF], preferred_element_type=F32).reshape(groups, S5_R, S5_W)
    eb = jnp.dot(u8, w_ref[W_SB], preferred_element_type=F32).reshape(groups, S5_R, S5_W)
    rowi = lax.broadcasted_iota(jnp.int32, (groups, S5_R, ts), 1)

    def prefix(x):
        for s in (1, 2, 4):
            x = x + jnp.where(rowi >= s, pltpu.roll(x, s, 1), 0.0)
        return x

    def suffix(x):
        for s in (1, 2, 4):
            x = x + jnp.where(rowi < S5_R - s, pltpu.roll(x, S5_R - s, 1), 0.0)
        return x

    p_re, p_im, q_re, q_im = (tab8_ref[0, i] for i in range(4))
    a_re, a_im = tab1_ref[0, 0:1, :], tab1_ref[0, 1:2, :]
    w_re, w_im = _cmul(q_re, q_im, ef[:, :, :ts], ef[:, :, ts:])
    cs_re, cs_im = prefix(w_re), prefix(w_im)
    st_re, st_im = [], []
    for s in range(nseq):
        x_re, x_im = x0_ref[s, 0, 0:1, :], x0_ref[s, 0, 1:2, :]
        for b in range(nb):
            g = s * nb + b
            st_re.append(x_re)
            st_im.append(x_im)
            x_re, x_im = _cmul(a_re, a_im, x_re + cs_re[g, S5_R - 1:S5_R, :],
                               x_im + cs_im[g, S5_R - 1:S5_R, :])
        fs_ref[s, 0, 0:1, :] = x_re
        fs_ref[s, 0, 1:2, :] = x_im
    xin_re, xin_im = _cmul(p_re, p_im, cs_re - w_re + jnp.stack(st_re), cs_im - w_im + jnp.stack(st_im))
    xin = jnp.concatenate([xin_re, xin_im], axis=2).reshape(nrow, S5_W)

    p_re, p_im, q_re, q_im = (tab8_ref[1, i] for i in range(4))
    a_re, a_im = tab1_ref[1, 0:1, :], tab1_ref[1, 1:2, :]
    w_re, w_im = _cmul(p_re, p_im, eb[:, :, :ts], eb[:, :, ts:])
    sf_re, sf_im = suffix(w_re), suffix(w_im)
    z_re, z_im = [None] * groups, [None] * groups
    for s in range(nseq):
        x_re, x_im = x0_ref[s, 1, 0:1, :], x0_ref[s, 1, 1:2, :]
        for b in reversed(range(nb)):
            g = s * nb + b
            z_re[g], z_im[g] = _cmul(a_re, a_im, x_re, x_im)
            x_re = sf_re[g, 0:1, :] + z_re[g]
            x_im = sf_im[g, 0:1, :] + z_im[g]
        fs_ref[s, 1, 0:1, :] = x_re
        fs_ref[s, 1, 1:2, :] = x_im
    xnx_re, xnx_im = _cmul(q_re, q_im, sf_re - w_re + jnp.stack(z_re), sf_im - w_im + jnp.stack(z_im))
    xnx = jnp.concatenate([xnx_re, xnx_im], axis=2).reshape(nrow, S5_W)

    y8 = (jnp.dot(u8, w_ref[W_M], preferred_element_type=F32)
          + _dot_nt(xin, w_ref[W_CF]) + _dot_nt(xnx, w_ref[W_CB]))
    for t in range(S5_T):
        y_ref[pl.ds(t, nrow, stride=S5_T), :] = y8[:, t * LANE:(t + 1) * LANE]


def _s5_scan(zs, x0, x0_block, x0_idx, layer, wmat, tab8, tab1, nseq, seq):
    n = nseq * seq
    return pl.pallas_call(
        functools.partial(_s5_scan_kernel, nseq=nseq, nb=seq // CHUNK),
        grid=(S5_TILES,),
        in_specs=[
            pl.BlockSpec((n, LANE), lambda j: (0, ZS_U // LANE + j)),
            pl.BlockSpec(x0_block, x0_idx),
            pl.BlockSpec((None, None, 5, S5_ROW, S5_W), lambda j: (layer, j, 0, 0, 0)),
            pl.BlockSpec((None, None, 2, 4, S5_R, S5_TILE_STATE), lambda j: (layer, j, 0, 0, 0, 0)),
            pl.BlockSpec((None, None, 2, 2, S5_TILE_STATE), lambda j: (layer, j, 0, 0, 0)),
        ],
        out_specs=[
            pl.BlockSpec((n, LANE), lambda j: (0, j)),
            pl.BlockSpec((nseq, 2, 2, S5_TILE_STATE), lambda j: (0, 0, 0, j)),
        ],
        out_shape=[
            jax.ShapeDtypeStruct((n, S5_WIDTH), F32),
            jax.ShapeDtypeStruct((nseq, 2, 2, S5_NSTATE), F32),
        ],
        scratch_shapes=[pltpu.VMEM((n, LANE), F32)],
        compiler_params=_params(("parallel",)),
        name="s5_scan",
    )(zs, x0, wmat, tab8, tab1)


def _merge_kernel(x_ref, mod_ref, nw_ref, oa_ref, ob_ref, ys_ref, sg_ref, wglu_ref, bglu_ref, wmg_ref,
                  wa_ref, wb_ref, wc_ref, wout_ref, y_ref, wmg_s):
    @pl.when(pl.program_id(0) == 0)
    def _():
        _pack_transposed(wmg_ref.at[0], wmg_s, 0, 3 * D_MODEL // LANE, 0, None)

    x = x_ref[...]
    mod = mod_ref[0]
    h = _mod_rmsnorm(x, nw_ref[...], mod).astype(BF16)
    g = jax.nn.sigmoid(jnp.dot(h, wmg_s[...], preferred_element_type=F32))
    zg = _dot(jax.nn.gelu(ys_ref[...]), wglu_ref[...]) + bglu_ref[...]
    oc = (zg[:, :S5_WIDTH] * jax.nn.sigmoid(zg[:, S5_WIDTH:])
          * jax.nn.silu(sg_ref[...].astype(F32)))
    mixed = (g[:, 0:D_MODEL] * _dot(oa_ref[...], wa_ref[...])
             + g[:, D_MODEL:2 * D_MODEL] * _dot(ob_ref[...], wb_ref[...])
             + g[:, 2 * D_MODEL:] * _dot(oc, wc_ref[...]))
    y_ref[...] = x + mod[:, 2 * D_MODEL:] * _dot(mixed, wout_ref[...])


def _merge(x2, mod, mod_idx, nw, oa, ob, ys, zs, layer, wglu, bglu, w_in_t, wa, wb, wc, wout):
    n = x2.shape[0]
    tm = ROW_TILE
    const = lambda i: (0, 0)
    rows = lambda w: pl.BlockSpec((tm, w), lambda i: (i, 0))
    return pl.pallas_call(
        _merge_kernel,
        grid=(n // tm,),
        in_specs=[
            rows(D_MODEL),
            pl.BlockSpec((1, 1, 3 * D_MODEL), lambda i: (mod_idx(i), 0, 0)),
            pl.BlockSpec((1, D_MODEL), const),
            rows(GLA_WIDTH), rows(MLA_WIDTH), rows(S5_WIDTH),
            pl.BlockSpec((tm, S5_WIDTH), lambda i: (i, ZS_GATE // S5_WIDTH)),
            pl.BlockSpec((None, S5_WIDTH, 2 * S5_WIDTH), lambda i: (layer, 0, 0)),
            pl.BlockSpec((None, 1, 2 * S5_WIDTH), lambda i: (layer, 0, 0)),
            pl.BlockSpec((pl.Element(1), pl.Element(3 * D_MODEL), pl.Element(D_MODEL)),
                         lambda i: (layer, MERGE_COL, 0), pipeline_mode=pl.Buffered(1)),
            pl.BlockSpec((GLA_WIDTH, D_MODEL), const),
            pl.BlockSpec((MLA_WIDTH, D_MODEL), const),
            pl.BlockSpec((S5_WIDTH, D_MODEL), const),
            pl.BlockSpec((D_MODEL, D_MODEL), const),
        ],
        out_specs=rows(D_MODEL),
        out_shape=jax.ShapeDtypeStruct((n, D_MODEL), F32),
        scratch_shapes=[pltpu.VMEM((D_MODEL, 3 * D_MODEL), BF16)],
        compiler_params=_params(("arbitrary",)),
        name="merge",
    )(x2, mod, nw, oa, ob, ys, zs, wglu, bglu, w_in_t, wa, wb, wc, wout)


def _pad_heads(w, width):
    lead = w.shape[:-1]
    w = w.reshape(lead + (MLA_HEADS, width))
    w = jnp.pad(w, [(0, 0)] * len(lead) + [(0, 0), (0, HEAD_PAD - width)])
    return w.reshape(lead + (MLA_HEADS * HEAD_PAD,))


def _rope_tables(n_tok):
    rows = n_tok // GRID_W
    r = jnp.repeat(jnp.arange(rows, dtype=F32), GRID_W)
    col = jnp.tile(jnp.arange(GRID_W, dtype=F32), rows)
    n_freq = MLA_ROPE // 4
    inv = ROPE_THETA ** (-jnp.arange(n_freq, dtype=F32) / n_freq)
    ang = jnp.concatenate([r[:, None] * inv, col[:, None] * inv], axis=-1)
    cos, sin = jnp.cos(ang), jnp.sin(ang)
    half = MLA_ROPE // 2
    z = lambda w: jnp.zeros((n_tok, w), F32)
    tail = HEAD_PAD - MLA_QK
    c = jnp.concatenate([jnp.ones((n_tok, MLA_NOPE), F32), cos, cos, z(tail)], axis=1)
    s_up = jnp.concatenate([z(MLA_NOPE + half), sin, z(tail)], axis=1)
    s_dn = jnp.concatenate([z(MLA_NOPE), -sin, z(half + tail)], axis=1)
    return jnp.stack([c, s_up, s_dn])


def _s5_group_blocks(x):
    lyr = x.shape[0]
    x = x.reshape(lyr, S5_TILES, 8, S5_GROUP, S5_STATE)
    eye = jnp.eye(8, dtype=x.dtype)
    blk = x[:, :, :, :, None, :] * eye[None, None, :, None, :, None]
    return blk.reshape(lyr, S5_TILES, LANE, S5_TILE_STATE)


def kernel(x_prompt, x_sample, c, c_ctx, cache_mla_ckv, cache_mla_krope, state_gla, state_s5,
           norm_w, w_ada, b_ada, w_in, gla_w_a2, gla_b_a, gla_o_norm,
           mla_q_norm, mla_w_uq, mla_kv_norm, mla_w_uk, mla_w_uv, mla_qh_norm, mla_kh_norm,
           s5_a_re, s5_a_im, s5_log_dt, s5_b_re, s5_b_im, s5_c_re, s5_c_im, s5_d, s5_w_glu, s5_b_glu,
           w_bo_gla, w_bo_mla, w_bo_s5, w_out):
    bsz, seq, _ = x_prompt.shape
    dbsz, dseq, _ = x_sample.shape
    ctx_row = 8 - 1
    assert dbsz <= ctx_row and (bsz * seq) % ROW_TILE == 0 and dseq % ROW_TILE == 0

    cond8 = jnp.zeros((8, D_MODEL), F32).at[0:dbsz].set(c).at[ctx_row].set(c_ctx)
    ada = _ada(cond8, w_ada, b_ada)

    vec = lambda a: a.reshape(DEPTH, 2, 1, S5_NSTATE)
    ldt = jnp.repeat(s5_log_dt[..., None], S5_STATE, axis=-1)
    bt = lambda b: _s5_group_blocks(b.transpose(0, 1, 3, 2))
    wmat, tab8, tab1 = _s5_prep(vec(s5_a_re), vec(s5_a_im), vec(ldt), bt(s5_b_re), bt(s5_b_im),
                                _s5_group_blocks(s5_c_re), _s5_group_blocks(s5_c_im),
                                s5_d.reshape(DEPTH, 1, S5_WIDTH))
    wglu = s5_w_glu.astype(BF16)
    bglu = s5_b_glu.reshape(DEPTH, 1, 2 * S5_WIDTH)

    wuq = _pad_heads(mla_w_uq, MLA_QK).astype(BF16)
    wuk = _pad_heads(mla_w_uk, MLA_NOPE).astype(BF16)
    wuv = mla_w_uv.astype(BF16)
    qhn = jnp.pad(mla_qh_norm, ((0, 0), (0, HEAD_PAD - MLA_QK)))
    khn = jnp.pad(mla_kh_norm, ((0, 0), (0, HEAD_PAD - MLA_QK)))
    e_np = np.zeros((LANE, MLA_HEADS * HEAD_PAD), np.float32)
    for h in range(MLA_HEADS):
        for i in range(MLA_ROPE):
            e_np[i, h * HEAD_PAD + MLA_NOPE + i] = 1.0
    e_place = jnp.asarray(e_np, BF16)
    rope_tab = _rope_tables(dseq)
    ckr_pad = jnp.pad(cache_mla_krope, ((0, 0), (0, 0), (0, 0), (0, LANE - MLA_ROPE)))

    zrow = lambda n: jnp.zeros((DEPTH, n, GLA_QK), F32)
    waf = jnp.concatenate([gla_w_a2[:, 0], zrow(LANE - GLA_RANK)], axis=1).astype(BF16)
    wab = jnp.concatenate([zrow(GLA_RANK), gla_w_a2[:, 1], zrow(LANE - 2 * GLA_RANK)], axis=1).astype(BF16)
    sgla = state_gla.reshape(dbsz, DEPTH, 2, GLA_QK, GLA_DV)
    ss5 = state_s5.reshape(dbsz, DEPTH, 2, 2, S5_NSTATE)
    zero_gla = jnp.zeros((2, GLA_QK, GLA_DV), F32)
    zero_s5 = jnp.zeros((bsz, 2, 2, S5_NSTATE), F32)

    hp = x_prompt.reshape(bsz * seq, D_MODEL)
    hs = x_sample.reshape(dbsz * dseq, D_MODEL)
    ckv_l, krope_l, gla_l, s5_l = [], [], [], []
    w_in_t = jnp.swapaxes(w_in, 1, 2)
    for l in range(DEPTH):
        mod = ada[l].reshape(8, 1, 3 * D_MODEL)
        nw = norm_w[l].reshape(1, D_MODEL)
        mla_w = (mla_q_norm[l].reshape(1, -1), wuq[l], mla_kv_norm[l].reshape(1, -1), wuk[l], wuv[l],
                 qhn[l].reshape(1, -1), khn[l].reshape(1, -1), e_place)
        wbo = (w_bo_gla[l].astype(BF16), w_bo_mla[l].astype(BF16), w_bo_s5[l].astype(BF16))
        wout = w_out[l].astype(BF16)
        onorm = gla_o_norm[l].reshape(1, GLA_DV)

        def layer(x2, nb, n, mod_idx, ctx):
            zg, zm, zs = _in_proj(x2, mod, mod_idx, nw, w_in_t, l)
            if ctx:
                s0, s0_idx = sgla, (lambda b: (b, l, 0, 0, 0))
                x0, x0_blk = ss5, (nb, None, 2, 2, S5_TILE_STATE)
                x0_idx = lambda j: (0, l, 0, 0, j)
                mctx, rt = (cache_mla_ckv, ckr_pad, l), rope_tab
            else:
                s0, s0_idx = zero_gla, (lambda b: (0, 0, 0))
                x0, x0_blk = zero_s5, (nb, 2, 2, S5_TILE_STATE)
                x0_idx = lambda j: (0, 0, 0, j)
                mctx, rt = None, None
            oa, st_gla = _gla(zg, s0, s0_idx, waf[l], wab[l], gla_b_a[l], onorm, nb, n)
            ob, ckv = _mla(zm, mctx, mla_w, rt, nb, n)
            y_ssm, st_s5 = _s5_scan(zs, x0, x0_blk, x0_idx, l, wmat, tab8, tab1, nb, n)
            y = _merge(x2, mod, mod_idx, nw, oa, ob, y_ssm, zs, l, wglu, bglu, w_in_t, *wbo, wout)
            return y, zm, ckv, st_gla, st_s5

        hp, zm_p, ckv_p, st_gla_p, st_s5_p = layer(hp, bsz, seq, lambda i: ctx_row, False)
        ckv_l.append(ckv_p.reshape(bsz, seq, MLA_KV_LORA))
        krope_l.append(zm_p[:, ZM_KR:ZM_KR + MLA_ROPE].astype(F32).reshape(bsz, seq, MLA_ROPE))
        gla_l.append(st_gla_p.reshape(bsz, 2, GLA_HEADS, GLA_DK, GLA_DV))
        s5_l.append(st_s5_p.reshape(bsz, 2, 2, S5_GROUPS, S5_STATE))
        blocks_per_seq = dseq // ROW_TILE
        hs = layer(hs, dbsz, dseq, lambda i: i // blocks_per_seq, True)[0]

    return (hp.reshape(bsz, seq, D_MODEL), hs.reshape(dbsz, dseq, D_MODEL),
            jnp.stack(ckv_l, axis=1), jnp.stack(krope_l, axis=1),
            jnp.stack(gla_l, axis=1), jnp.stack(s5_l, axis=1))
```

```python
import functools

import jax
import jax.numpy as jnp
import numpy as np
from jax import lax
from jax.experimental import pallas as pl
from jax.experimental.pallas import tpu as pltpu

F32 = jnp.float32
BF16 = jnp.bfloat16

EPS = 1e-6
D_MODEL = 1024
DEPTH = 2
GRID_W = 64
ROPE_THETA = 10000.0
GLA_HEADS = 4
GLA_DK = 64
GLA_DV = 128
GLA_RANK = 16
GLA_GATE_NORM = 16.0
GLA_QK = GLA_HEADS * GLA_DK
GLA_WIDTH = GLA_HEADS * GLA_DV
MLA_HEADS = 4
MLA_Q_LORA = 384
MLA_KV_LORA = 256
MLA_NOPE = 64
MLA_ROPE = 32
MLA_QK = MLA_NOPE + MLA_ROPE
MLA_DV = 128
MLA_WIDTH = MLA_HEADS * MLA_DV
S5_WIDTH = 512
S5_GROUP = 16
S5_GROUPS = 32
S5_STATE = 64
S5_NSTATE = S5_GROUPS * S5_STATE

LANE = 128
HEAD_PAD = LANE
ROPE_SHIFT = LANE // 2
CHUNK = 64
GLA_STEP = 256
S5_TILES = S5_WIDTH // LANE
S5_TILE_STATE = S5_NSTATE // S5_TILES
ROW_TILE = 512
Q_TILE = 512
PROJ_TILE = 256
VMEM_LIMIT = 56 * 1024 * 1024

MERGE_COL = 3776
ZG_Q, ZG_K, ZG_V, ZG_A, ZG_GATE, ZG_W = 0, 256, 512, 1024, 1152, 1664
ZM_Q, ZM_KV, ZM_KR, ZM_GATE, ZM_W = 0, 384, 640, 768, 1280
ZS_U, ZS_GATE, ZS_W = 0, 512, 1024
ZG_BASE, ZM_BASE, ZS_BASE, PACK_W = 0, ZG_W, ZG_W + ZM_W, ZG_W + ZM_W + ZS_W
IN_PIECES = (
    (0, 8, ZG_BASE + ZG_Q, None),
    (1024, 1, ZG_BASE + ZG_A, 2 * GLA_RANK),
    (1056, 4, ZG_BASE + ZG_GATE, None),
    (1568, 5, ZM_BASE + ZM_Q, None),
    (2208, 1, ZM_BASE + ZM_KR, MLA_ROPE),
    (2240, 4, ZM_BASE + ZM_GATE, None),
    (2752, 8, ZS_BASE + ZS_U, None),
)


def _dot(a, b):
    return jnp.dot(a.astype(BF16), b.astype(BF16), preferred_element_type=F32)


def _dot_nt(a, b):
    return lax.dot_general(a.astype(BF16), b.astype(BF16), (((1,), (1,)), ((), ())),
                           preferred_element_type=F32)


def _split_bf16(x, parts):
    out = []
    r = x
    for _ in range(parts):
        p = r.astype(BF16)
        out.append(p)
        r = r - p.astype(F32)
    return out


def _params(sem):
    return pltpu.CompilerParams(dimension_semantics=sem, vmem_limit_bytes=VMEM_LIMIT)


def _ada_kernel(c_ref, w_ref, b_ref, o_ref):
    s = jax.nn.silu(c_ref[...])
    o_ref[...] = _dot(s, w_ref[...]) + b_ref[...]


def _ada(cond8, w_ada, b_ada):
    tn = 1024
    return pl.pallas_call(
        _ada_kernel,
        grid=(DEPTH, 3 * D_MODEL // tn),
        in_specs=[
            pl.BlockSpec((8, D_MODEL), lambda l, n: (0, 0)),
            pl.BlockSpec((None, D_MODEL, tn), lambda l, n: (l, 0, n)),
            pl.BlockSpec((None, 1, tn), lambda l, n: (l, 0, n)),
        ],
        out_specs=pl.BlockSpec((None, 8, tn), lambda l, n: (l, 0, n)),
        out_shape=jax.ShapeDtypeStruct((DEPTH, 8, 3 * D_MODEL), F32),
        compiler_params=_params(("parallel", "parallel")),
        name="ada",
    )(cond8, w_ada, b_ada.reshape(DEPTH, 1, 3 * D_MODEL))


def _mod_rmsnorm(x, nw, mod):
    ms = jnp.mean(x * x, axis=-1, keepdims=True)
    y = x * lax.rsqrt(ms + EPS) * nw
    return y * (1.0 + mod[:, D_MODEL:2 * D_MODEL]) + mod[:, 0:D_MODEL]


def _pack_transposed(w_ref, wb_s, src, tiles, dst, keep):
    lane = lax.broadcasted_iota(jnp.int32, (D_MODEL, LANE), 1)
    for t in range(tiles):
        blk = w_ref[src + t * LANE:src + (t + 1) * LANE, :].T
        if keep is not None:
            blk = jnp.where(lane < keep, blk, 0.0)
        wb_s[:, dst + t * LANE:dst + (t + 1) * LANE] = blk.astype(BF16)


def _in_proj_kernel(x_ref, mod_ref, nw_ref, w_ref, zg_ref, zm_ref, zs_ref, wb_s):
    @pl.when(pl.program_id(0) == 0)
    def _():
        for src, tiles, dst, keep in IN_PIECES:
            _pack_transposed(w_ref, wb_s, src, tiles, dst, keep)

    h = _mod_rmsnorm(x_ref[...], nw_ref[...], mod_ref[0]).astype(BF16)
    z = jnp.dot(h, wb_s[...], preferred_element_type=F32)
    zg_ref[...] = z[:, ZG_BASE:ZG_BASE + ZG_W].astype(BF16)
    zm_ref[...] = z[:, ZM_BASE:ZM_BASE + ZM_W].astype(BF16)
    zs_ref[...] = z[:, ZS_BASE:ZS_BASE + ZS_W].astype(BF16)


def _in_proj(x2, mod, mod_idx, nw, w_in_t, layer):
    n = x2.shape[0]
    tm = ROW_TILE
    const = lambda i: (0, 0)
    return pl.pallas_call(
        _in_proj_kernel,
        grid=(n // tm,),
        in_specs=[
            pl.BlockSpec((tm, D_MODEL), lambda i: (i, 0)),
            pl.BlockSpec((1, 1, 3 * D_MODEL), lambda i: (mod_idx(i), 0, 0)),
            pl.BlockSpec((1, D_MODEL), const),
            pl.BlockSpec((None, MERGE_COL, D_MODEL), lambda i: (layer, 0, 0), pipeline_mode=pl.Buffered(1)),
        ],
        out_specs=[
            pl.BlockSpec((tm, ZG_W), lambda i: (i, 0)),
            pl.BlockSpec((tm, ZM_W), lambda i: (i, 0)),
            pl.BlockSpec((tm, ZS_W), lambda i: (i, 0)),
        ],
        out_shape=[
            jax.ShapeDtypeStruct((n, ZG_W), BF16),
            jax.ShapeDtypeStruct((n, ZM_W), BF16),
            jax.ShapeDtypeStruct((n, ZS_W), BF16),
        ],
        scratch_shapes=[pltpu.VMEM((D_MODEL, PACK_W), BF16)],
        compiler_params=_params(("arbitrary",)),
        name="in_proj",
    )(x2, mod, nw, w_in_t)


def _gla_kernel(zg_ref, s0_ref, waf_ref, wab_ref, ba_ref, onorm_ref, o_ref, sfin_ref,
                la_s, o_s, st_s, *, nsteps):
    a_blk = zg_ref[:, ZG_A:ZG_A + LANE]
    inv_norm = 1.0 / GLA_GATE_NORM
    la_s[0] = jax.nn.log_sigmoid(_dot(a_blk, waf_ref[...]) + ba_ref[0:1, :]) * inv_norm
    la_s[1] = jax.nn.log_sigmoid(_dot(a_blk, wab_ref[...]) + ba_ref[1:2, :]) * inv_norm
    zero_blk = jnp.zeros((GLA_DK, GLA_DV), F32)
    for d in (0, 1):
        s0 = s0_ref[d]
        rows_bd = []
        for h in range(GLA_HEADS):
            sh = s0[h * GLA_DK:(h + 1) * GLA_DK, :]
            rows_bd.append(jnp.concatenate([sh if h2 == h else zero_blk for h2 in range(GLA_HEADS)], axis=1))
        st_s[d] = jnp.concatenate(rows_bd, axis=0).T

    def iota(shape, axis, shift):
        return lax.shift_right_logical(lax.broadcasted_iota(jnp.int32, shape, axis), shift)

    log_chunk, log_dv = CHUNK.bit_length() - 1, GLA_DV.bit_length() - 1
    row = lax.broadcasted_iota(jnp.int32, (GLA_STEP, GLA_STEP), 0)
    col = lax.broadcasted_iota(jnp.int32, (GLA_STEP, GLA_STEP), 1)
    same_chunk = iota((GLA_STEP, GLA_STEP), 0, log_chunk) == iota((GLA_STEP, GLA_STEP), 1, log_chunk)
    masks = (same_chunk & (row >= col), same_chunk & (row <= col))
    lane_head = iota((GLA_STEP, GLA_QK), 1, log_chunk)
    row_chunk = iota((GLA_STEP, GLA_QK), 0, log_chunk)
    state_blk = iota((GLA_WIDTH, GLA_QK), 0, log_dv) == iota((GLA_WIDTH, GLA_QK), 1, log_chunk)
    qscale = GLA_DK ** -0.5
    nch = GLA_STEP // CHUNK

    def step(i, carry):
        for d in (0, 1):
            sc = i if d == 0 else nsteps - 1 - i
            rows = pl.ds(pl.multiple_of(sc * GLA_STEP, GLA_STEP), GLA_STEP)
            a_hi, a_lo = _split_bf16(la_s[d, rows, :], 2)
            tri = masks[d].astype(BF16)
            cum = (jnp.dot(tri, a_hi, preferred_element_type=F32)
                   + jnp.dot(tri, a_lo, preferred_element_type=F32))
            edge = CHUNK - 1 if d == 0 else 0
            blast = [cum[c * CHUNK + edge:c * CHUNK + edge + 1, :] for c in range(nch)]
            bl = jnp.concatenate([jnp.broadcast_to(b, (CHUNK, GLA_QK)) for b in blast], axis=0)
            q = zg_ref[rows, ZG_Q:ZG_Q + GLA_QK].astype(F32) * qscale
            k = zg_ref[rows, ZG_K:ZG_K + GLA_QK].astype(F32)
            v = zg_ref[rows, ZG_V:ZG_V + GLA_WIDTH]
            v_t = v.astype(F32).T.astype(BF16)
            qd = q * jnp.exp(cum)
            kd = (k * jnp.exp(-cum)).astype(BF16)
            kr = k * jnp.exp(bl - cum)
            outs = []
            for h in range(GLA_HEADS):
                qh = jnp.where(lane_head == h, qd, 0.0)
                att = jnp.where(masks[d], _dot_nt(qh, kd), 0.0)
                outs.append(_dot(att, v[:, h * GLA_DV:(h + 1) * GLA_DV]))
            s = st_s[d]
            inter = [None] * nch
            for c in (range(nch) if d == 0 else reversed(range(nch))):
                inter[c] = _dot_nt(qd[c * CHUNK:(c + 1) * CHUNK, :], s)
                kv_t = jnp.where(state_blk, _dot(v_t, jnp.where(row_chunk == c, kr, 0.0)), 0.0)
                s = s * jnp.exp(blast[c]) + kv_t
            st_s[d] = s
            o_s[d, rows, :] = jnp.concatenate(outs, axis=1) + jnp.concatenate(inter, axis=0)
        return carry

    lax.fori_loop(0, nsteps, step, 0)
    for d in (0, 1):
        s_fin = st_s[d].T
        for h in range(GLA_HEADS):
            sfin_ref[d, h * GLA_DK:(h + 1) * GLA_DK, :] = (
                s_fin[h * GLA_DK:(h + 1) * GLA_DK, h * GLA_DV:(h + 1) * GLA_DV])
    o = o_s[0] + o_s[1]
    gate = zg_ref[:, ZG_GATE:ZG_GATE + GLA_WIDTH].astype(F32)
    onorm = onorm_ref[...]
    for h in range(GLA_HEADS):
        vs = slice(h * GLA_DV, (h + 1) * GLA_DV)
        oh = o[:, vs]
        ms = jnp.mean(oh * oh, axis=-1, keepdims=True)
        o_ref[:, vs] = oh * lax.rsqrt(ms + EPS) * onorm * jax.nn.silu(gate[:, vs])


def _gla(zg, s0, s0_idx, waf, wab, ba, onorm, bsz, seq):
    const = lambda b: (0, 0)
    nsd = len(s0.shape)
    s0_block = (None,) * (nsd - 3) + (2, GLA_QK, GLA_DV)
    return pl.pallas_call(
        functools.partial(_gla_kernel, nsteps=seq // GLA_STEP),
        grid=(bsz,),
        in_specs=[
            pl.BlockSpec((seq, ZG_W), lambda b: (b, 0)),
            pl.BlockSpec(s0_block, s0_idx),
            pl.BlockSpec((LANE, GLA_QK), const),
            pl.BlockSpec((LANE, GLA_QK), const),
            pl.BlockSpec((2, GLA_QK), const),
            pl.BlockSpec((1, GLA_DV), const),
        ],
        out_specs=[
            pl.BlockSpec((seq, GLA_WIDTH), lambda b: (b, 0)),
            pl.BlockSpec((None, 2, GLA_QK, GLA_DV), lambda b: (b, 0, 0, 0)),
        ],
        out_shape=[
            jax.ShapeDtypeStruct((bsz * seq, GLA_WIDTH), F32),
            jax.ShapeDtypeStruct((bsz, 2, GLA_QK, GLA_DV), F32),
        ],
        scratch_shapes=[
            pltpu.VMEM((2, seq, GLA_QK), F32),
            pltpu.VMEM((2, seq, GLA_WIDTH), F32),
            pltpu.VMEM((2, GLA_WIDTH, GLA_QK), F32),
        ],
        compiler_params=_params(("parallel",)),
        name="gla",
    )(zg, s0, waf, wab, ba, onorm)


def _rms(x, w):
    ms = jnp.mean(x * x, axis=-1, keepdims=True)
    return x * lax.rsqrt(ms + EPS) * w


def _head_sums_mxu(x):
    width = x.shape[-1]
    shift = HEAD_PAD.bit_length() - 1
    gi = lax.shift_right_logical(lax.broadcasted_iota(jnp.int32, (width, width), 0), shift)
    gj = lax.shift_right_logical(lax.broadcasted_iota(jnp.int32, (width, width), 1), shift)
    return _dot(x * x, jnp.where(gi == gj, 1.0, 0.0))


def _head_norm(x, w, rope, on_mxu):
    sums = _head_sums_mxu(x) if on_mxu else None
    outs = []
    for h in range(MLA_HEADS):
        hs = slice(h * HEAD_PAD, (h + 1) * HEAD_PAD)
        xh = x[:, hs]
        ss = sums[:, hs] if on_mxu else jnp.sum(xh * xh, axis=-1, keepdims=True)
        yh = xh * lax.rsqrt(ss * (1.0 / MLA_QK) + EPS) * w
        if rope is not None:
            c, s = rope
            yh = yh * c + pltpu.roll(yh, ROPE_SHIFT, 1) * s
        outs.append(yh)
    return outs


def _place_rope_key(kr, e):
    return sum(jnp.dot(p, e, preferred_element_type=F32) for p in _split_bf16(kr, 3))


def _mla_kernel(*refs, seq, n_ctx, use_rope):
    it = iter(refs)
    zm_ref = next(it)
    if n_ctx:
        cckv_ref, ckr_ref = next(it), next(it)
    qn_ref, wuq_ref, kvn_ref, wuk_ref, wuv_ref, qhn_ref, khn_ref, e_ref = (next(it) for _ in range(8))
    rope_ref = next(it) if use_rope else None
    o_ref, ckv_ref = next(it), next(it)
    q_s, k_s, v_s = next(it), next(it), next(it)

    qscale = MLA_QK ** -0.5
    heads = [slice(h * HEAD_PAD, (h + 1) * HEAD_PAD) for h in range(MLA_HEADS)]

    def keys_values(ckv, k_rope_placed, rope, k_rows):
        kh = _head_norm(_dot(ckv, wuk_ref[...]) + k_rope_placed, khn_ref[...], rope, False)
        for h, hs in enumerate(heads):
            k_s[k_rows, hs] = kh[h].astype(BF16)
        v_s[k_rows, :] = _dot(ckv, wuv_ref[...]).astype(BF16)

    def latent_tile(i, carry):
        r0 = pl.multiple_of(i * PROJ_TILE, PROJ_TILE)
        rows = pl.ds(r0, PROJ_TILE)
        rope = (rope_ref[0, rows, :], rope_ref[1, rows, :]) if use_rope else None
        ckv = _rms(zm_ref[rows, ZM_KV:ZM_KV + MLA_KV_LORA].astype(F32), kvn_ref[...])
        ckv_ref[rows, :] = ckv
        k_pe = jnp.dot(zm_ref[rows, ZM_KR:ZM_KR + LANE], e_ref[...], preferred_element_type=F32)
        keys_values(ckv, k_pe, rope, pl.ds(n_ctx + r0, PROJ_TILE))
        cq = _rms(zm_ref[rows, ZM_Q:ZM_Q + MLA_Q_LORA].astype(F32), qn_ref[...])
        qh = _head_norm(_dot(cq, wuq_ref[...]), qhn_ref[...], rope, True)
        for h, hs in enumerate(heads):
            q_s[rows, hs] = (qh[h] * qscale).astype(BF16)
        return carry

    lax.fori_loop(0, seq // PROJ_TILE, latent_tile, 0)

    def context_tile(i, carry):
        rows = pl.ds(pl.multiple_of(i * PROJ_TILE, PROJ_TILE), PROJ_TILE)
        keys_values(cckv_ref[rows, :], _place_rope_key(ckr_ref[rows, :], e_ref[...]), None, rows)
        return carry

    if n_ctx:
        lax.fori_loop(0, n_ctx // PROJ_TILE, context_tile, 0)

    q_tile = min(seq, Q_TILE)

    def q_block(i, carry):
        rows = pl.ds(pl.multiple_of(i * q_tile, q_tile), q_tile)
        gate = zm_ref[rows, ZM_GATE:ZM_GATE + MLA_WIDTH].astype(F32)
        for h in range(MLA_HEADS):
            hs = slice(h * HEAD_PAD, (h + 1) * HEAD_PAD)
            s = lax.dot_general(q_s[rows, hs], k_s[:, hs], (((1,), (1,)), ((), ())),
                                preferred_element_type=F32)
            m = jnp.max(s, axis=-1, keepdims=True)
            p = jnp.exp(s - m)
            l = jnp.sum(p, axis=-1, keepdims=True)
            o = jnp.dot(p.astype(BF16), v_s[:, hs], preferred_element_type=F32) / l
            o_ref[rows, hs] = o * jax.nn.silu(gate[:, hs])
        return carry

    lax.fori_loop(0, seq // q_tile, q_block, 0)


def _mla(zm, ctx, w, rope_tab, bsz, seq):
    const2 = lambda b: (0, 0)
    n_ctx = 0 if ctx is None else ctx[0].shape[-2]
    in_specs = [pl.BlockSpec((seq, ZM_W), lambda b: (b, 0))]
    args = [zm]
    if ctx is not None:
        cckv, ckr, layer = ctx
        in_specs += [
            pl.BlockSpec((None, None, n_ctx, MLA_KV_LORA), lambda b: (b, layer, 0, 0)),
            pl.BlockSpec((None, None, n_ctx, LANE), lambda b: (b, layer, 0, 0)),
        ]
        args += [cckv, ckr]
    in_specs += [
        pl.BlockSpec((1, MLA_Q_LORA), const2),
        pl.BlockSpec((MLA_Q_LORA, MLA_HEADS * HEAD_PAD), const2),
        pl.BlockSpec((1, MLA_KV_LORA), const2),
        pl.BlockSpec((MLA_KV_LORA, MLA_HEADS * HEAD_PAD), const2),
        pl.BlockSpec((MLA_KV_LORA, MLA_WIDTH), const2),
        pl.BlockSpec((1, HEAD_PAD), const2),
        pl.BlockSpec((1, HEAD_PAD), const2),
        pl.BlockSpec((LANE, MLA_HEADS * HEAD_PAD), const2),
    ]
    args += list(w)
    if rope_tab is not None:
        in_specs.append(pl.BlockSpec((2, seq, HEAD_PAD), lambda b: (0, 0, 0)))
        args.append(rope_tab)
    return pl.pallas_call(
        functools.partial(_mla_kernel, seq=seq, n_ctx=n_ctx, use_rope=rope_tab is not None),
        grid=(bsz,),
        in_specs=in_specs,
        out_specs=[
            pl.BlockSpec((seq, MLA_WIDTH), lambda b: (b, 0)),
            pl.BlockSpec((seq, MLA_KV_LORA), lambda b: (b, 0)),
        ],
        out_shape=[
            jax.ShapeDtypeStruct((bsz * seq, MLA_WIDTH), F32),
            jax.ShapeDtypeStruct((bsz * seq, MLA_KV_LORA), F32),
        ],
        scratch_shapes=[
            pltpu.VMEM((seq, MLA_HEADS * HEAD_PAD), BF16),
            pltpu.VMEM((n_ctx + seq, MLA_HEADS * HEAD_PAD), BF16),
            pltpu.VMEM((n_ctx + seq, MLA_WIDTH), BF16),
        ],
        compiler_params=_params(("parallel",)),
        name="mla",
    )(*args)


S5_T = 8
S5_R = CHUNK // S5_T
S5_ROW = S5_T * LANE
S5_W = 2 * S5_TILE_STATE
W_M, W_SF, W_SB, W_CF, W_CB = range(5)


def _cmul(ar, ai, br, bi):
    return ar * br - ai * bi, ar * bi + ai * br


def _s5_prep_kernel(are_ref, aim_ref, ldt_ref, bre_ref, bim_ref, cre_ref, cim_ref, d_ref,
                    w_ref, tab8_ref, tab1_ref):
    b_re, b_im = bre_ref[...], bim_ref[...]
    c_re, c_im = cre_ref[...], cim_ref[...]
    c_cat = jnp.concatenate([c_re, c_im], axis=1).astype(BF16)
    kern = []
    for d in (0, 1):
        a_re, a_im = are_ref[d], aim_ref[d]
        dt = jnp.exp(ldt_ref[d])
        lam = a_re * dt
        th = a_im * dt
        mag = jnp.exp(lam)
        ab_re = mag * jnp.cos(th)
        ab_im = mag * jnp.sin(th)
        den = a_re * a_re + a_im * a_im
        n_re = ab_re - 1.0
        cf_re = (n_re * a_re + ab_im * a_im) / den
        cf_im = (ab_im * a_re - n_re * a_im) / den
        bp_re, bp_im = _cmul(b_re, b_im, cf_re, cf_im)
        k = lax.broadcasted_iota(jnp.int32, (2 * S5_T, S5_TILE_STATE), 0).astype(F32)
        pmag = jnp.exp(k * lam)
        pw_re = pmag * jnp.cos(k * th)
        pw_im = pmag * jnp.sin(k * th)
        taps = []
        for p in range(S5_T + 1):
            ar, ai = pw_re[p:p + 1, :], pw_im[p:p + 1, :]
            l_re, l_im = _cmul(bp_re, bp_im, ar, ai)
            v_re, v_im = _cmul(c_re, c_im, ar, ai)
            t_in = S5_T - 1 - p if d == 0 else p
            if 0 <= t_in < S5_T:
                w_ref[W_SF + d, t_in * LANE:(t_in + 1) * LANE, :] = (
                    jnp.concatenate([l_re, l_im], axis=1).astype(BF16))
            t_out = p - 1 if d == 0 else S5_T - p
            if 0 <= t_out < S5_T:
                w_ref[W_CF + d, t_out * LANE:(t_out + 1) * LANE, :] = (
                    jnp.concatenate([v_re, -v_im], axis=1).astype(BF16))
            if p < S5_T:
                taps.append(_dot_nt(jnp.concatenate([l_re, -l_im], axis=1), c_cat))
        kern.append(taps)
        r = lax.broadcasted_iota(jnp.int32, (S5_R, S5_TILE_STATE), 0).astype(F32) * float(S5_T)
        r1 = r + float(S5_T)
        pm = jnp.exp(r * lam)
        qm = jnp.exp(-(r1 * lam))
        tab8_ref[d, 0] = pm * jnp.cos(r * th)
        tab8_ref[d, 1] = pm * jnp.sin(r * th)
        tab8_ref[d, 2] = qm * jnp.cos(r1 * th)
        tab8_ref[d, 3] = -(qm * jnp.sin(r1 * th))
        mc = jnp.exp(float(CHUNK) * lam)
        tab1_ref[d, 0:1, :] = mc * jnp.cos(float(CHUNK) * th)
        tab1_ref[d, 1:2, :] = mc * jnp.sin(float(CHUNK) * th)
    row = lax.broadcasted_iota(jnp.int32, (LANE, LANE), 0)
    col = lax.broadcasted_iota(jnp.int32, (LANE, LANE), 1)
    skip = jnp.where(row == col, d_ref[...], 0.0)
    for t in range(S5_T):
        for t2 in range(S5_T):
            if t < t2:
                blk = kern[0][t2 - t]
            elif t > t2:
                blk = kern[1][t - t2]
            else:
                blk = kern[0][0] + kern[1][0] + skip
            w_ref[W_M, t * LANE:(t + 1) * LANE, t2 * LANE:(t2 + 1) * LANE] = blk.astype(BF16)


def _s5_prep(a_re, a_im, ldt, b_re, b_im, c_re, c_im, dsk):
    vec = pl.BlockSpec((None, 2, 1, S5_TILE_STATE), lambda l, j: (l, 0, 0, j))
    blk = pl.BlockSpec((None, None, LANE, S5_TILE_STATE), lambda l, j: (l, j, 0, 0))
    return pl.pallas_call(
        _s5_prep_kernel,
        grid=(DEPTH, S5_TILES),
        in_specs=[vec, vec, vec, blk, blk, blk, blk,
                  pl.BlockSpec((None, 1, LANE), lambda l, j: (l, 0, j))],
        out_specs=[
            pl.BlockSpec((None, None, 5, S5_ROW, S5_W), lambda l, j: (l, j, 0, 0, 0)),
            pl.BlockSpec((None, None, 2, 4, S5_R, S5_TILE_STATE), lambda l, j: (l, j, 0, 0, 0, 0)),
            pl.BlockSpec((None, None, 2, 2, S5_TILE_STATE), lambda l, j: (l, j, 0, 0, 0)),
        ],
        out_shape=[
            jax.ShapeDtypeStruct((DEPTH, S5_TILES, 5, S5_ROW, S5_W), BF16),
            jax.ShapeDtypeStruct((DEPTH, S5_TILES, 2, 4, S5_R, S5_TILE_STATE), F32),
            jax.ShapeDtypeStruct((DEPTH, S5_TILES, 2, 2, S5_TILE_STATE), F32),
        ],
        compiler_params=_params(("parallel", "parallel")),
        name="s5_prep",
    )(a_re, a_im, ldt, b_re, b_im, c_re, c_im, dsk)


def _s5_scan_kernel(u_ref, x0_ref, w_ref, tab8_ref, tab1_ref, y_ref, fs_ref, u_s, *, nseq, nb):
    groups = nseq * nb
    nrow = groups * S5_R
    ts = S5_TILE_STATE
    u_s[...] = u_ref[...].astype(F32)
    u8 = jnp.concatenate([u_s[pl.ds(t, nrow, stride=S5_T), :] for t in range(S5_T)],
                         axis=1).astype(BF16)
    ef = jnp.dot(u8, w_ref[W_SF], preferred_element_type=F32).reshape(groups, S5_R, S5_W)
    eb = jnp.dot(u8, w_ref[W_SB], preferred_element_type=F32).reshape(groups, S5_R, S5_W)
    rowi = lax.broadcasted_iota(jnp.int32, (groups, S5_R, ts), 1)

    def prefix(x):
        for s in (1, 2, 4):
            x = x + jnp.where(rowi >= s, pltpu.roll(x, s, 1), 0.0)
        return x

    def suffix(x):
        for s in (1, 2, 4):
            x = x + jnp.where(rowi < S5_R - s, pltpu.roll(x, S5_R - s, 1), 0.0)
        return x

    p_re, p_im, q_re, q_im = (tab8_ref[0, i] for i in range(4))
    a_re, a_im = tab1_ref[0, 0:1, :], tab1_ref[0, 1:2, :]
    w_re, w_im = _cmul(q_re, q_im, ef[:, :, :ts], ef[:, :, ts:])
    cs_re, cs_im = prefix(w_re), prefix(w_im)
    st_re, st_im = [], []
    for s in range(nseq):
        x_re, x_im = x0_ref[s, 0, 0:1, :], x0_ref[s, 0, 1:2, :]
        for b in range(nb):
            g = s * nb + b
            st_re.append(x_re)
            st_im.append(x_im)
            x_re, x_im = _cmul(a_re, a_im, x_re + cs_re[g, S5_R - 1:S5_R, :],
                               x_im + cs_im[g, S5_R - 1:S5_R, :])
        fs_ref[s, 0, 0:1, :] = x_re
        fs_ref[s, 0, 1:2, :] = x_im
    xin_re, xin_im = _cmul(p_re, p_im, cs_re - w_re + jnp.stack(st_re), cs_im - w_im + jnp.stack(st_im))
    xin = jnp.concatenate([xin_re, xin_im], axis=2).reshape(nrow, S5_W)

    p_re, p_im, q_re, q_im = (tab8_ref[1, i] for i in range(4))
    a_re, a_im = tab1_ref[1, 0:1, :], tab1_ref[1, 1:2, :]
    w_re, w_im = _cmul(p_re, p_im, eb[:, :, :ts], eb[:, :, ts:])
    sf_re, sf_im = suffix(w_re), suffix(w_im)
    z_re, z_im = [None] * groups, [None] * groups
    for s in range(nseq):
        x_re, x_im = x0_ref[s, 1, 0:1, :], x0_ref[s, 1, 1:2, :]
        for b in reversed(range(nb)):
            g = s * nb + b
            z_re[g], z_im[g] = _cmul(a_re, a_im, x_re, x_im)
            x_re = sf_re[g, 0:1, :] + z_re[g]
            x_im = sf_im[g, 0:1, :] + z_im[g]
        fs_ref[s, 1, 0:1, :] = x_re
        fs_ref[s, 1, 1:2, :] = x_im
    xnx_re, xnx_im = _cmul(q_re, q_im, sf_re - w_re + jnp.stack(z_re), sf_im - w_im + jnp.stack(z_im))
    xnx = jnp.concatenate([xnx_re, xnx_im], axis=2).reshape(nrow, S5_W)

    y8 = (jnp.dot(u8, w_ref[W_M], preferred_element_type=F32)
          + _dot_nt(xin, w_ref[W_CF]) + _dot_nt(xnx, w_ref[W_CB]))
    for t in range(S5_T):
        y_ref[pl.ds(t, nrow, stride=S5_T), :] = y8[:, t * LANE:(t + 1) * LANE]


def _s5_scan(zs, x0, x0_block, x0_idx, layer, wmat, tab8, tab1, nseq, seq):
    n = nseq * seq
    return pl.pallas_call(
        functools.partial(_s5_scan_kernel, nseq=nseq, nb=seq // CHUNK),
        grid=(S5_TILES,),
        in_specs=[
            pl.BlockSpec((n, LANE), lambda j: (0, ZS_U // LANE + j)),
            pl.BlockSpec(x0_block, x0_idx),
            pl.BlockSpec((None, None, 5, S5_ROW, S5_W), lambda j: (layer, j, 0, 0, 0)),
            pl.BlockSpec((None, None, 2, 4, S5_R, S5_TILE_STATE), lambda j: (layer, j, 0, 0, 0, 0)),
            pl.BlockSpec((None, None, 2, 2, S5_TILE_STATE), lambda j: (layer, j, 0, 0, 0)),
        ],
        out_specs=[
            pl.BlockSpec((n, LANE), lambda j: (0, j)),
            pl.BlockSpec((nseq, 2, 2, S5_TILE_STATE), lambda j: (0, 0, 0, j)),
        ],
        out_shape=[
            jax.ShapeDtypeStruct((n, S5_WIDTH), F32),
            jax.ShapeDtypeStruct((nseq, 2, 2, S5_NSTATE), F32),
        ],
        scratch_shapes=[pltpu.VMEM((n, LANE), F32)],
        compiler_params=_params(("parallel",)),
        name="s5_scan",
    )(zs, x0, wmat, tab8, tab1)


def _merge_kernel(x_ref, mod_ref, nw_ref, oa_ref, ob_ref, ys_ref, sg_ref, wglu_ref, bglu_ref, wmg_ref,
                  wa_ref, wb_ref, wc_ref, wout_ref, y_ref, wmg_s):
    @pl.when(pl.program_id(0) == 0)
    def _():
        _pack_transposed(wmg_ref.at[0], wmg_s, 0, 3 * D_MODEL // LANE, 0, None)

    x = x_ref[...]
    mod = mod_ref[0]
    h = _mod_rmsnorm(x, nw_ref[...], mod).astype(BF16)
    g = jax.nn.sigmoid(jnp.dot(h, wmg_s[...], preferred_element_type=F32))
    zg = _dot(jax.nn.gelu(ys_ref[...]), wglu_ref[...]) + bglu_ref[...]
    oc = (zg[:, :S5_WIDTH] * jax.nn.sigmoid(zg[:, S5_WIDTH:])
          * jax.nn.silu(sg_ref[...].astype(F32)))
    mixed = (g[:, 0:D_MODEL] * _dot(oa_ref[...], wa_ref[...])
             + g[:, D_MODEL:2 * D_MODEL] * _dot(ob_ref[...], wb_ref[...])
             + g[:, 2 * D_MODEL:] * _dot(oc, wc_ref[...]))
    y_ref[...] = x + mod[:, 2 * D_MODEL:] * _dot(mixed, wout_ref[...])


def _merge(x2, mod, mod_idx, nw, oa, ob, ys, zs, layer, wglu, bglu, w_in_t, wa, wb, wc, wout):
    n = x2.shape[0]
    tm = ROW_TILE
    const = lambda i: (0, 0)
    rows = lambda w: pl.BlockSpec((tm, w), lambda i: (i, 0))
    return pl.pallas_call(
        _merge_kernel,
        grid=(n // tm,),
        in_specs=[
            rows(D_MODEL),
            pl.BlockSpec((1, 1, 3 * D_MODEL), lambda i: (mod_idx(i), 0, 0)),
            pl.BlockSpec((1, D_MODEL), const),
            rows(GLA_WIDTH), rows(MLA_WIDTH), rows(S5_WIDTH),
            pl.BlockSpec((tm, S5_WIDTH), lambda i: (i, ZS_GATE // S5_WIDTH)),
            pl.BlockSpec((None, S5_WIDTH, 2 * S5_WIDTH), lambda i: (layer, 0, 0)),
            pl.BlockSpec((None, 1, 2 * S5_WIDTH), lambda i: (layer, 0, 0)),
            pl.BlockSpec((pl.Element(1), pl.Element(3 * D_MODEL), pl.Element(D_MODEL)),
                         lambda i: (layer, MERGE_COL, 0), pipeline_mode=pl.Buffered(1)),
            pl.BlockSpec((GLA_WIDTH, D_MODEL), const),
            pl.BlockSpec((MLA_WIDTH, D_MODEL), const),
            pl.BlockSpec((S5_WIDTH, D_MODEL), const),
            pl.BlockSpec((D_MODEL, D_MODEL), const),
        ],
        out_specs=rows(D_MODEL),
        out_shape=jax.ShapeDtypeStruct((n, D_MODEL), F32),
        scratch_shapes=[pltpu.VMEM((D_MODEL, 3 * D_MODEL), BF16)],
        compiler_params=_params(("arbitrary",)),
        name="merge",
    )(x2, mod, nw, oa, ob, ys, zs, wglu, bglu, w_in_t, wa, wb, wc, wout)


def _mla_lane_of_dim():
    half = MLA_ROPE // 2
    first_gap = ROPE_SHIFT - half
    lane = np.zeros(MLA_QK, np.int32)
    for j in range(MLA_NOPE):
        lane[j] = half + j if j < first_gap else 2 * half + j
    for r in range(half):
        lane[MLA_NOPE + r] = r
        lane[MLA_NOPE + half + r] = ROPE_SHIFT + r
    return lane


MLA_LANE_OF_DIM = _mla_lane_of_dim()


def _place_heads(w, heads, lane_of_dim):
    width = len(lane_of_dim)
    src = np.zeros(heads * HEAD_PAD, np.int32)
    used = np.zeros(heads * HEAD_PAD, bool)
    for h in range(heads):
        src[h * HEAD_PAD + lane_of_dim] = h * width + np.arange(width)
        used[h * HEAD_PAD + lane_of_dim] = True
    return jnp.where(jnp.asarray(used), jnp.take(w, jnp.asarray(src), axis=-1), 0.0)


def _rope_tables(n_tok):
    rows = n_tok // GRID_W
    r = jnp.repeat(jnp.arange(rows, dtype=F32), GRID_W)
    col = jnp.tile(jnp.arange(GRID_W, dtype=F32), rows)
    n_freq = MLA_ROPE // 4
    inv = ROPE_THETA ** (-jnp.arange(n_freq, dtype=F32) / n_freq)
    ang = jnp.concatenate([r[:, None] * inv, col[:, None] * inv], axis=-1)
    cos, sin = jnp.cos(ang), jnp.sin(ang)
    ones = jnp.ones((n_tok, MLA_NOPE), F32)
    c = _place_heads(jnp.concatenate([ones, cos, cos], axis=1), 1, MLA_LANE_OF_DIM)
    s = _place_heads(jnp.concatenate([0.0 * ones, -sin, sin], axis=1), 1, MLA_LANE_OF_DIM)
    return jnp.stack([c, s])


def _s5_group_blocks(x):
    lyr = x.shape[0]
    x = x.reshape(lyr, S5_TILES, 8, S5_GROUP, S5_STATE)
    eye = jnp.eye(8, dtype=x.dtype)
    blk = x[:, :, :, :, None, :] * eye[None, None, :, None, :, None]
    return blk.reshape(lyr, S5_TILES, LANE, S5_TILE_STATE)


def kernel(x_prompt, x_sample, c, c_ctx, cache_mla_ckv, cache_mla_krope, state_gla, state_s5,
           norm_w, w_ada, b_ada, w_in, gla_w_a2, gla_b_a, gla_o_norm,
           mla_q_norm, mla_w_uq, mla_kv_norm, mla_w_uk, mla_w_uv, mla_qh_norm, mla_kh_norm,
           s5_a_re, s5_a_im, s5_log_dt, s5_b_re, s5_b_im, s5_c_re, s5_c_im, s5_d, s5_w_glu, s5_b_glu,
           w_bo_gla, w_bo_mla, w_bo_s5, w_out):
    bsz, seq, _ = x_prompt.shape
    dbsz, dseq, _ = x_sample.shape
    ctx_row = 8 - 1
    assert dbsz <= ctx_row and (bsz * seq) % ROW_TILE == 0 and dseq % ROW_TILE == 0

    cond8 = jnp.zeros((8, D_MODEL), F32).at[0:dbsz].set(c).at[ctx_row].set(c_ctx)
    ada = _ada(cond8, w_ada, b_ada)

    vec = lambda a: a.reshape(DEPTH, 2, 1, S5_NSTATE)
    ldt = jnp.repeat(s5_log_dt[..., None], S5_STATE, axis=-1)
    bt = lambda b: _s5_group_blocks(b.transpose(0, 1, 3, 2))
    wmat, tab8, tab1 = _s5_prep(vec(s5_a_re), vec(s5_a_im), vec(ldt), bt(s5_b_re), bt(s5_b_im),
                                _s5_group_blocks(s5_c_re), _s5_group_blocks(s5_c_im),
                                s5_d.reshape(DEPTH, 1, S5_WIDTH))
    wglu = s5_w_glu.astype(BF16)
    bglu = s5_b_glu.reshape(DEPTH, 1, 2 * S5_WIDTH)

    wuq = _place_heads(mla_w_uq, MLA_HEADS, MLA_LANE_OF_DIM).astype(BF16)
    wuk = _place_heads(mla_w_uk, MLA_HEADS, MLA_LANE_OF_DIM[:MLA_NOPE]).astype(BF16)
    wuv = mla_w_uv.astype(BF16)
    qhn = _place_heads(mla_qh_norm, 1, MLA_LANE_OF_DIM)
    khn = _place_heads(mla_kh_norm, 1, MLA_LANE_OF_DIM)
    e_np = np.zeros((LANE, MLA_HEADS * HEAD_PAD), np.float32)
    for h in range(MLA_HEADS):
        for i in range(MLA_ROPE):
            e_np[i, h * HEAD_PAD + MLA_LANE_OF_DIM[MLA_NOPE + i]] = 1.0
    e_place = jnp.asarray(e_np, BF16)
    rope_tab = _rope_tables(dseq)
    ckr_pad = jnp.pad(cache_mla_krope, ((0, 0), (0, 0), (0, 0), (0, LANE - MLA_ROPE)))

    zrow = lambda n: jnp.zeros((DEPTH, n, GLA_QK), F32)
    waf = jnp.concatenate([gla_w_a2[:, 0], zrow(LANE - GLA_RANK)], axis=1).astype(BF16)
    wab = jnp.concatenate([zrow(GLA_RANK), gla_w_a2[:, 1], zrow(LANE - 2 * GLA_RANK)], axis=1).astype(BF16)
    sgla = state_gla.reshape(dbsz, DEPTH, 2, GLA_QK, GLA_DV)
    ss5 = state_s5.reshape(dbsz, DEPTH, 2, 2, S5_NSTATE)
    zero_gla = jnp.zeros((2, GLA_QK, GLA_DV), F32)
    zero_s5 = jnp.zeros((bsz, 2, 2, S5_NSTATE), F32)

    hp = x_prompt.reshape(bsz * seq, D_MODEL)
    hs = x_sample.reshape(dbsz * dseq, D_MODEL)
    ckv_l, krope_l, gla_l, s5_l = [], [], [], []
    w_in_t = jnp.swapaxes(w_in, 1, 2)
    for l in range(DEPTH):
        mod = ada[l].reshape(8, 1, 3 * D_MODEL)
        nw = norm_w[l].reshape(1, D_MODEL)
        mla_w = (mla_q_norm[l].reshape(1, -1), wuq[l], mla_kv_norm[l].reshape(1, -1), wuk[l], wuv[l],
                 qhn[l].reshape(1, -1), khn[l].reshape(1, -1), e_place)
        wbo = (w_bo_gla[l].astype(BF16), w_bo_mla[l].astype(BF16), w_bo_s5[l].astype(BF16))
        wout = w_out[l].astype(BF16)
        onorm = gla_o_norm[l].reshape(1, GLA_DV)

        def layer(x2, nb, n, mod_idx, ctx):
            zg, zm, zs = _in_proj(x2, mod, mod_idx, nw, w_in_t, l)
            if ctx:
                s0, s0_idx = sgla, (lambda b: (b, l, 0, 0, 0))
                x0, x0_blk = ss5, (nb, None, 2, 2, S5_TILE_STATE)
                x0_idx = lambda j: (0, l, 0, 0, j)
                mctx, rt = (cache_mla_ckv, ckr_pad, l), rope_tab
            else:
                s0, s0_idx = zero_gla, (lambda b: (0, 0, 0))
                x0, x0_blk = zero_s5, (nb, 2, 2, S5_TILE_STATE)
                x0_idx = lambda j: (0, 0, 0, j)
                mctx, rt = None, None
            oa, st_gla = _gla(zg, s0, s0_idx, waf[l], wab[l], gla_b_a[l], onorm, nb, n)
            ob, ckv = _mla(zm, mctx, mla_w, rt, nb, n)
            y_ssm, st_s5 = _s5_scan(zs, x0, x0_blk, x0_idx, l, wmat, tab8, tab1, nb, n)
            y = _merge(x2, mod, mod_idx, nw, oa, ob, y_ssm, zs, l, wglu, bglu, w_in_t, *wbo, wout)
            return y, zm, ckv, st_gla, st_s5

        hp, zm_p, ckv_p, st_gla_p, st_s5_p = layer(hp, bsz, seq, lambda i: ctx_row, False)
        ckv_l.append(ckv_p.reshape(bsz, seq, MLA_KV_LORA))
        krope_l.append(zm_p[:, ZM_KR:ZM_KR + MLA_ROPE].astype(F32).reshape(bsz, seq, MLA_ROPE))
        gla_l.append(st_gla_p.reshape(bsz, 2, GLA_HEADS, GLA_DK, GLA_DV))
        s5_l.append(st_s5_p.reshape(bsz, 2, 2, S5_GROUPS, S5_STATE))
        blocks_per_seq = dseq // ROW_TILE
        hs = layer(hs, dbsz, dseq, lambda i: i // blocks_per_seq, True)[0]

    return (hp.reshape(bsz, seq, D_MODEL), hs.reshape(dbsz, dseq, D_MODEL),
            jnp.stack(ckv_l, axis=1), jnp.stack(krope_l, axis=1),
            jnp.stack(gla_l, axis=1), jnp.stack(s5_l, axis=1))
```

```python
import functools

import jax
import jax.numpy as jnp
import numpy as np
from jax import lax
from jax.experimental import pallas as pl
from jax.experimental.pallas import tpu as pltpu

F32 = jnp.float32
BF16 = jnp.bfloat16

EPS = 1e-6
D_MODEL = 1024
DEPTH = 2
GRID_W = 64
ROPE_THETA = 10000.0
GLA_HEADS = 4
GLA_DK = 64
GLA_DV = 128
GLA_RANK = 16
GLA_GATE_NORM = 16.0
GLA_QK = GLA_HEADS * GLA_DK
GLA_WIDTH = GLA_HEADS * GLA_DV
MLA_HEADS = 4
MLA_Q_LORA = 384
MLA_KV_LORA = 256
MLA_NOPE = 64
MLA_ROPE = 32
MLA_QK = MLA_NOPE + MLA_ROPE
MLA_DV = 128
MLA_WIDTH = MLA_HEADS * MLA_DV
S5_WIDTH = 512
S5_GROUP = 16
S5_GROUPS = 32
S5_STATE = 64
S5_NSTATE = S5_GROUPS * S5_STATE

LANE = 128
HEAD_PAD = LANE
ROPE_SHIFT = LANE // 2
CHUNK = 64
GLA_STEP = 256
GLA_SEQS_PER_STEP = 2
GLA_ROWS_PER_STEP = 1024
S5_TILES = S5_WIDTH // LANE
S5_TILE_STATE = S5_NSTATE // S5_TILES
ROW_TILE = 512
Q_TILE = 512
PROJ_TILE = 256
VMEM_LIMIT = 56 * 1024 * 1024

MERGE_COL = 3776
ZG_Q, ZG_K, ZG_V, ZG_A, ZG_GATE, ZG_W = 0, 256, 512, 1024, 1152, 1664
ZM_Q, ZM_KV, ZM_KR, ZM_GATE, ZM_W = 0, 384, 640, 768, 1280
ZS_U, ZS_GATE, ZS_W = 0, 512, 1024
ZG_BASE, ZM_BASE, ZS_BASE, PACK_W = 0, ZG_W, ZG_W + ZM_W, ZG_W + ZM_W + ZS_W
IN_PIECES = (
    (0, 8, ZG_BASE + ZG_Q, None),
    (1024, 1, ZG_BASE + ZG_A, 2 * GLA_RANK),
    (1056, 4, ZG_BASE + ZG_GATE, None),
    (1568, 5, ZM_BASE + ZM_Q, None),
    (2208, 1, ZM_BASE + ZM_KR, MLA_ROPE),
    (2240, 4, ZM_BASE + ZM_GATE, None),
    (2752, 8, ZS_BASE + ZS_U, None),
)


def _dot(a, b):
    return jnp.dot(a.astype(BF16), b.astype(BF16), preferred_element_type=F32)


def _dot_nt(a, b):
    return lax.dot_general(a.astype(BF16), b.astype(BF16), (((1,), (1,)), ((), ())),
                           preferred_element_type=F32)


def _split_bf16(x, parts):
    out = []
    r = x
    for _ in range(parts):
        p = r.astype(BF16)
        out.append(p)
        r = r - p.astype(F32)
    return out


def _params(sem):
    return pltpu.CompilerParams(dimension_semantics=sem, vmem_limit_bytes=VMEM_LIMIT)


def _ada_kernel(c_ref, w_ref, b_ref, o_ref):
    s = jax.nn.silu(c_ref[...])
    o_ref[...] = _dot(s, w_ref[...]) + b_ref[...]


def _ada(cond8, w_ada, b_ada):
    tn = 1024
    return pl.pallas_call(
        _ada_kernel,
        grid=(DEPTH, 3 * D_MODEL // tn),
        in_specs=[
            pl.BlockSpec((8, D_MODEL), lambda l, n: (0, 0)),
            pl.BlockSpec((None, D_MODEL, tn), lambda l, n: (l, 0, n)),
            pl.BlockSpec((None, 1, tn), lambda l, n: (l, 0, n)),
        ],
        out_specs=pl.BlockSpec((None, 8, tn), lambda l, n: (l, 0, n)),
        out_shape=jax.ShapeDtypeStruct((DEPTH, 8, 3 * D_MODEL), F32),
        compiler_params=_params(("parallel", "parallel")),
        name="ada",
    )(cond8, w_ada, b_ada.reshape(DEPTH, 1, 3 * D_MODEL))


def _mod_rmsnorm(x, nw, mod):
    ms = jnp.mean(x * x, axis=-1, keepdims=True)
    y = x * lax.rsqrt(ms + EPS) * nw
    return y * (1.0 + mod[:, D_MODEL:2 * D_MODEL]) + mod[:, 0:D_MODEL]


def _pack_transposed(w_ref, wb_s, src, tiles, dst, keep):
    lane = lax.broadcasted_iota(jnp.int32, (D_MODEL, LANE), 1)
    for t in range(tiles):
        blk = w_ref[src + t * LANE:src + (t + 1) * LANE, :].T
        if keep is not None:
            blk = jnp.where(lane < keep, blk, 0.0)
        wb_s[:, dst + t * LANE:dst + (t + 1) * LANE] = blk.astype(BF16)


def _in_proj_kernel(x_ref, mod_ref, nw_ref, w_ref, zg_ref, zm_ref, zs_ref, wb_s):
    @pl.when(pl.program_id(0) == 0)
    def _():
        for src, tiles, dst, keep in IN_PIECES:
            _pack_transposed(w_ref, wb_s, src, tiles, dst, keep)

    h = _mod_rmsnorm(x_ref[...], nw_ref[...], mod_ref[0]).astype(BF16)
    z = jnp.dot(h, wb_s[...], preferred_element_type=F32)
    zg_ref[...] = z[:, ZG_BASE:ZG_BASE + ZG_W].astype(BF16)
    zm_ref[...] = z[:, ZM_BASE:ZM_BASE + ZM_W].astype(BF16)
    zs_ref[...] = z[:, ZS_BASE:ZS_BASE + ZS_W].astype(BF16)


def _in_proj(x2, mod, mod_idx, nw, w_in_t, layer):
    n = x2.shape[0]
    tm = ROW_TILE
    const = lambda i: (0, 0)
    return pl.pallas_call(
        _in_proj_kernel,
        grid=(n // tm,),
        in_specs=[
            pl.BlockSpec((tm, D_MODEL), lambda i: (i, 0)),
            pl.BlockSpec((1, 1, 3 * D_MODEL), lambda i: (mod_idx(i), 0, 0)),
            pl.BlockSpec((1, D_MODEL), const),
            pl.BlockSpec((None, MERGE_COL, D_MODEL), lambda i: (layer, 0, 0), pipeline_mode=pl.Buffered(1)),
        ],
        out_specs=[
            pl.BlockSpec((tm, ZG_W), lambda i: (i, 0)),
            pl.BlockSpec((tm, ZM_W), lambda i: (i, 0)),
            pl.BlockSpec((tm, ZS_W), lambda i: (i, 0)),
        ],
        out_shape=[
            jax.ShapeDtypeStruct((n, ZG_W), BF16),
            jax.ShapeDtypeStruct((n, ZM_W), BF16),
            jax.ShapeDtypeStruct((n, ZS_W), BF16),
        ],
        scratch_shapes=[pltpu.VMEM((D_MODEL, PACK_W), BF16)],
        compiler_params=_params(("arbitrary",)),
        name="in_proj",
    )(x2, mod, nw, w_in_t)


def _gla_kernel(*refs, nsteps, seq, nseq, has_ctx):
    it = iter(refs)
    zg_ref = next(it)
    s0_ref = next(it) if has_ctx else None
    waf_ref, wab_ref, ba_ref, onorm_ref, o_ref, sfin_ref, la_s, o_s, st_s = (next(it) for _ in range(9))
    chains = [(g, d) for g in range(nseq) for d in (0, 1)]
    inv_norm = 1.0 / GLA_GATE_NORM
    zero_blk = jnp.zeros((GLA_DK, GLA_DV), F32)
    for ch, (g, d) in enumerate(chains):
        if d == 0:
            a_blk = zg_ref[g * seq:(g + 1) * seq, ZG_A:ZG_A + LANE]
        wa_ref = waf_ref if d == 0 else wab_ref
        la_s[ch] = jax.nn.log_sigmoid(_dot(a_blk, wa_ref[...]) + ba_ref[d:d + 1, :]) * inv_norm
        if has_ctx:
            s0 = s0_ref[g, d]
            rows_bd = []
            for h in range(GLA_HEADS):
                sh = s0[h * GLA_DK:(h + 1) * GLA_DK, :]
                rows_bd.append(jnp.concatenate([sh if h2 == h else zero_blk for h2 in range(GLA_HEADS)], axis=1))
            st_s[ch] = jnp.concatenate(rows_bd, axis=0).T
        else:
            st_s[ch] = jnp.zeros((GLA_WIDTH, GLA_QK), F32)

    def iota(shape, axis, shift):
        return lax.shift_right_logical(lax.broadcasted_iota(jnp.int32, shape, axis), shift)

    log_chunk, log_dv = CHUNK.bit_length() - 1, GLA_DV.bit_length() - 1
    row = lax.broadcasted_iota(jnp.int32, (GLA_STEP, GLA_STEP), 0)
    col = lax.broadcasted_iota(jnp.int32, (GLA_STEP, GLA_STEP), 1)
    same_chunk = iota((GLA_STEP, GLA_STEP), 0, log_chunk) == iota((GLA_STEP, GLA_STEP), 1, log_chunk)
    masks = (same_chunk & (row >= col), same_chunk & (row <= col))
    lane_head = iota((GLA_STEP, GLA_QK), 1, log_chunk)
    row_chunk = iota((GLA_STEP, GLA_QK), 0, log_chunk)
    state_blk = iota((GLA_WIDTH, GLA_QK), 0, log_dv) == iota((GLA_WIDTH, GLA_QK), 1, log_chunk)
    qscale = GLA_DK ** -0.5
    nch = GLA_STEP // CHUNK
    n_chain = len(chains)

    def step(i, carry):
        rows, cum = [], []
        for ch, (g, d) in enumerate(chains):
            r0 = pl.multiple_of((i if d == 0 else nsteps - 1 - i) * GLA_STEP, GLA_STEP)
            rows.append((pl.ds(g * seq + r0, GLA_STEP), pl.ds(r0, GLA_STEP)))
            a_hi, a_lo = _split_bf16(la_s[ch, rows[ch][1], :], 2)
            tri = masks[d].astype(BF16)
            cum.append(jnp.dot(tri, a_hi, preferred_element_type=F32)
                       + jnp.dot(tri, a_lo, preferred_element_type=F32))
        blast, v, v_t, qd, kd, kr = [], [], [], [], [], []
        for ch, (g, d) in enumerate(chains):
            edge = CHUNK - 1 if d == 0 else 0
            blast.append([cum[ch][c * CHUNK + edge:c * CHUNK + edge + 1, :] for c in range(nch)])
            bl = jnp.concatenate([jnp.broadcast_to(b, (CHUNK, GLA_QK)) for b in blast[ch]], axis=0)
            zrows = rows[ch][0]
            q = zg_ref[zrows, ZG_Q:ZG_Q + GLA_QK].astype(F32) * qscale
            k = zg_ref[zrows, ZG_K:ZG_K + GLA_QK].astype(F32)
            v.append(zg_ref[zrows, ZG_V:ZG_V + GLA_WIDTH])
            v_t.append(v[ch].astype(F32).T.astype(BF16))
            qd.append(q * jnp.exp(cum[ch]))
            kd.append((k * jnp.exp(-cum[ch])).astype(BF16))
            kr.append(k * jnp.exp(bl - cum[ch]))
        outs = [[] for _ in chains]
        for h in range(GLA_HEADS):
            for ch, (g, d) in enumerate(chains):
                qh = jnp.where(lane_head == h, qd[ch], 0.0)
                att = jnp.where(masks[d], _dot_nt(qh, kd[ch]), 0.0)
                outs[ch].append(_dot(att, v[ch][:, h * GLA_DV:(h + 1) * GLA_DV]))
        s = [st_s[ch] for ch in range(n_chain)]
        inter = [[None] * nch for _ in chains]
        for j in range(nch):
            for ch, (g, d) in enumerate(chains):
                c = j if d == 0 else nch - 1 - j
                inter[ch][c] = _dot_nt(qd[ch][c * CHUNK:(c + 1) * CHUNK, :], s[ch])
                kv_t = jnp.where(state_blk, _dot(v_t[ch], jnp.where(row_chunk == c, kr[ch], 0.0)), 0.0)
                s[ch] = s[ch] * jnp.exp(blast[ch][c]) + kv_t
        for ch in range(n_chain):
            st_s[ch] = s[ch]
            o_s[ch, rows[ch][1], :] = jnp.concatenate(outs[ch], axis=1) + jnp.concatenate(inter[ch], axis=0)
        return carry

    lax.fori_loop(0, nsteps, step, 0)
    onorm = onorm_ref[...]
    for ch, (g, d) in enumerate(chains):
        s_fin = st_s[ch].T
        for h in range(GLA_HEADS):
            sfin_ref[g, d, h * GLA_DK:(h + 1) * GLA_DK, :] = (
                s_fin[h * GLA_DK:(h + 1) * GLA_DK, h * GLA_DV:(h + 1) * GLA_DV])
    for g in range(nseq):
        srows = slice(g * seq, (g + 1) * seq)
        o = o_s[2 * g] + o_s[2 * g + 1]
        gate = zg_ref[srows, ZG_GATE:ZG_GATE + GLA_WIDTH].astype(F32)
        for h in range(GLA_HEADS):
            vs = slice(h * GLA_DV, (h + 1) * GLA_DV)
            oh = o[:, vs]
            ms = jnp.mean(oh * oh, axis=-1, keepdims=True)
            o_ref[srows, vs] = oh * lax.rsqrt(ms + EPS) * onorm * jax.nn.silu(gate[:, vs])


def _gla(zg, ctx, waf, wab, ba, onorm, bsz, seq):
    const = lambda b: (0, 0)
    nseq = max(GLA_SEQS_PER_STEP, GLA_ROWS_PER_STEP // seq)
    in_specs = [pl.BlockSpec((nseq * seq, ZG_W), lambda b: (b, 0))]
    args = [zg]
    if ctx is not None:
        s0, layer = ctx
        in_specs.append(pl.BlockSpec((nseq, None, 2, GLA_QK, GLA_DV), lambda b: (b, layer, 0, 0, 0)))
        args.append(s0)
    in_specs += [
        pl.BlockSpec((LANE, GLA_QK), const),
        pl.BlockSpec((LANE, GLA_QK), const),
        pl.BlockSpec((2, GLA_QK), const),
        pl.BlockSpec((1, GLA_DV), const),
    ]
    return pl.pallas_call(
        functools.partial(_gla_kernel, nsteps=seq // GLA_STEP, seq=seq, nseq=nseq, has_ctx=ctx is not None),
        grid=(bsz // nseq,),
        in_specs=in_specs,
        out_specs=[
            pl.BlockSpec((nseq * seq, GLA_WIDTH), lambda b: (b, 0)),
            pl.BlockSpec((nseq, 2, GLA_QK, GLA_DV), lambda b: (b, 0, 0, 0)),
        ],
        out_shape=[
            jax.ShapeDtypeStruct((bsz * seq, GLA_WIDTH), F32),
            jax.ShapeDtypeStruct((bsz, 2, GLA_QK, GLA_DV), F32),
        ],
        scratch_shapes=[
            pltpu.VMEM((2 * nseq, seq, GLA_QK), F32),
            pltpu.VMEM((2 * nseq, seq, GLA_WIDTH), F32),
            pltpu.VMEM((2 * nseq, GLA_WIDTH, GLA_QK), F32),
        ],
        compiler_params=_params(("parallel",)),
        name="gla",
    )(*args, waf, wab, ba, onorm)


def _rms(x, w):
    ms = jnp.mean(x * x, axis=-1, keepdims=True)
    return x * lax.rsqrt(ms + EPS) * w


def _head_sums_mxu(x):
    width = x.shape[-1]
    shift = HEAD_PAD.bit_length() - 1
    gi = lax.shift_right_logical(lax.broadcasted_iota(jnp.int32, (width, width), 0), shift)
    gj = lax.shift_right_logical(lax.broadcasted_iota(jnp.int32, (width, width), 1), shift)
    return _dot(x * x, jnp.where(gi == gj, 1.0, 0.0))


def _head_norm(x, w, rope, on_mxu):
    sums = _head_sums_mxu(x) if on_mxu else None
    outs = []
    for h in range(MLA_HEADS):
        hs = slice(h * HEAD_PAD, (h + 1) * HEAD_PAD)
        xh = x[:, hs]
        ss = sums[:, hs] if on_mxu else jnp.sum(xh * xh, axis=-1, keepdims=True)
        yh = xh * lax.rsqrt(ss * (1.0 / MLA_QK) + EPS) * w
        if rope is not None:
            c, s = rope
            yh = yh * c + pltpu.roll(yh, ROPE_SHIFT, 1) * s
        outs.append(yh)
    return outs


def _place_rope_key(kr, e):
    return sum(jnp.dot(p, e, preferred_element_type=F32) for p in _split_bf16(kr, 3))


def _mla_kernel(*refs, seq, n_ctx, use_rope):
    it = iter(refs)
    zm_ref = next(it)
    if n_ctx:
        cckv_ref, ckr_ref = next(it), next(it)
    qn_ref, wuq_ref, kvn_ref, wuk_ref, wuv_ref, qhn_ref, khn_ref, e_ref = (next(it) for _ in range(8))
    rope_ref = next(it) if use_rope else None
    o_ref, ckv_ref = next(it), next(it)
    q_s, k_s, v_s = next(it), next(it), next(it)

    qscale = MLA_QK ** -0.5
    heads = [slice(h * HEAD_PAD, (h + 1) * HEAD_PAD) for h in range(MLA_HEADS)]

    def keys_values(ckv, k_rope_placed, rope, k_rows):
        kh = _head_norm(_dot(ckv, wuk_ref[...]) + k_rope_placed, khn_ref[...], rope, False)
        for h, hs in enumerate(heads):
            k_s[k_rows, hs] = kh[h].astype(BF16)
        v_s[k_rows, :] = _dot(ckv, wuv_ref[...]).astype(BF16)

    def latent_tile(i, carry):
        r0 = pl.multiple_of(i * PROJ_TILE, PROJ_TILE)
        rows = pl.ds(r0, PROJ_TILE)
        rope = (rope_ref[0, rows, :], rope_ref[1, rows, :]) if use_rope else None
        ckv = _rms(zm_ref[rows, ZM_KV:ZM_KV + MLA_KV_LORA].astype(F32), kvn_ref[...])
        ckv_ref[rows, :] = ckv
        k_pe = jnp.dot(zm_ref[rows, ZM_KR:ZM_KR + LANE], e_ref[...], preferred_element_type=F32)
        keys_values(ckv, k_pe, rope, pl.ds(n_ctx + r0, PROJ_TILE))
        cq = _rms(zm_ref[rows, ZM_Q:ZM_Q + MLA_Q_LORA].astype(F32), qn_ref[...])
        qh = _head_norm(_dot(cq, wuq_ref[...]), qhn_ref[...], rope, True)
        for h, hs in enumerate(heads):
            q_s[rows, hs] = (qh[h] * qscale).astype(BF16)
        return carry

    lax.fori_loop(0, seq // PROJ_TILE, latent_tile, 0)

    def context_tile(i, carry):
        rows = pl.ds(pl.multiple_of(i * PROJ_TILE, PROJ_TILE), PROJ_TILE)
        keys_values(cckv_ref[rows, :], _place_rope_key(ckr_ref[rows, :], e_ref[...]), None, rows)
        return carry

    if n_ctx:
        lax.fori_loop(0, n_ctx // PROJ_TILE, context_tile, 0)

    q_tile = min(seq, Q_TILE)

    def q_block(i, carry):
        rows = pl.ds(pl.multiple_of(i * q_tile, q_tile), q_tile)
        gate = zm_ref[rows, ZM_GATE:ZM_GATE + MLA_WIDTH].astype(F32)
        for h in range(MLA_HEADS):
            hs = slice(h * HEAD_PAD, (h + 1) * HEAD_PAD)
            s = lax.dot_general(q_s[rows, hs], k_s[:, hs], (((1,), (1,)), ((), ())),
                                preferred_element_type=F32)
            m = jnp.max(s, axis=-1, keepdims=True)
            p = jnp.exp(s - m)
            l = jnp.sum(p, axis=-1, keepdims=True)
            o = jnp.dot(p.astype(BF16), v_s[:, hs], preferred_element_type=F32) / l
            o_ref[rows, hs] = o * jax.nn.silu(gate[:, hs])
        return carry

    lax.fori_loop(0, seq // q_tile, q_block, 0)


def _mla(zm, ctx, w, rope_tab, bsz, seq):
    const2 = lambda b: (0, 0)
    n_ctx = 0 if ctx is None else ctx[0].shape[-2]
    in_specs = [pl.BlockSpec((seq, ZM_W), lambda b: (b, 0))]
    args = [zm]
    if ctx is not None:
        cckv, ckr, layer = ctx
        in_specs += [
            pl.BlockSpec((None, None, n_ctx, MLA_KV_LORA), lambda b: (b, layer, 0, 0)),
            pl.BlockSpec((None, None, n_ctx, LANE), lambda b: (b, layer, 0, 0)),
        ]
        args += [cckv, ckr]
    in_specs += [
        pl.BlockSpec((1, MLA_Q_LORA), const2),
        pl.BlockSpec((MLA_Q_LORA, MLA_HEADS * HEAD_PAD), const2),
        pl.BlockSpec((1, MLA_KV_LORA), const2),
        pl.BlockSpec((MLA_KV_LORA, MLA_HEADS * HEAD_PAD), const2),
        pl.BlockSpec((MLA_KV_LORA, MLA_WIDTH), const2),
        pl.BlockSpec((1, HEAD_PAD), const2),
        pl.BlockSpec((1, HEAD_PAD), const2),
        pl.BlockSpec((LANE, MLA_HEADS * HEAD_PAD), const2),
    ]
    args += list(w)
    if rope_tab is not None:
        in_specs.append(pl.BlockSpec((2, seq, HEAD_PAD), lambda b: (0, 0, 0)))
        args.append(rope_tab)
    return pl.pallas_call(
        functools.partial(_mla_kernel, seq=seq, n_ctx=n_ctx, use_rope=rope_tab is not None),
        grid=(bsz,),
        in_specs=in_specs,
        out_specs=[
            pl.BlockSpec((seq, MLA_WIDTH), lambda b: (b, 0)),
            pl.BlockSpec((seq, MLA_KV_LORA), lambda b: (b, 0)),
        ],
        out_shape=[
            jax.ShapeDtypeStruct((bsz * seq, MLA_WIDTH), F32),
            jax.ShapeDtypeStruct((bsz * seq, MLA_KV_LORA), F32),
        ],
        scratch_shapes=[
            pltpu.VMEM((seq, MLA_HEADS * HEAD_PAD), BF16),
            pltpu.VMEM((n_ctx + seq, MLA_HEADS * HEAD_PAD), BF16),
            pltpu.VMEM((n_ctx + seq, MLA_WIDTH), BF16),
        ],
        compiler_params=_params(("parallel",)),
        name="mla",
    )(*args)


S5_T = 8
S5_R = CHUNK // S5_T
S5_ROW = S5_T * LANE
S5_W = 2 * S5_TILE_STATE
W_M, W_SF, W_SB, W_CF, W_CB = range(5)


def _cmul(ar, ai, br, bi):
    return ar * br - ai * bi, ar * bi + ai * br


def _s5_prep_kernel(are_ref, aim_ref, ldt_ref, bre_ref, bim_ref, cre_ref, cim_ref, d_ref,
                    w_ref, tab8_ref, tab1_ref):
    b_re, b_im = bre_ref[...], bim_ref[...]
    c_re, c_im = cre_ref[...], cim_ref[...]
    c_cat = jnp.concatenate([c_re, c_im], axis=1).astype(BF16)
    kern = []
    for d in (0, 1):
        a_re, a_im = are_ref[d], aim_ref[d]
        dt = jnp.exp(ldt_ref[d])
        lam = a_re * dt
        th = a_im * dt
        mag = jnp.exp(lam)
        ab_re = mag * jnp.cos(th)
        ab_im = mag * jnp.sin(th)
        den = a_re * a_re + a_im * a_im
        n_re = ab_re - 1.0
        cf_re = (n_re * a_re + ab_im * a_im) / den
        cf_im = (ab_im * a_re - n_re * a_im) / den
        bp_re, bp_im = _cmul(b_re, b_im, cf_re, cf_im)
        k = lax.broadcasted_iota(jnp.int32, (2 * S5_T, S5_TILE_STATE), 0).astype(F32)
        pmag = jnp.exp(k * lam)
        pw_re = pmag * jnp.cos(k * th)
        pw_im = pmag * jnp.sin(k * th)
        taps = []
        for p in range(S5_T + 1):
            ar, ai = pw_re[p:p + 1, :], pw_im[p:p + 1, :]
            l_re, l_im = _cmul(bp_re, bp_im, ar, ai)
            v_re, v_im = _cmul(c_re, c_im, ar, ai)
            t_in = S5_T - 1 - p if d == 0 else p
            if 0 <= t_in < S5_T:
                w_ref[W_SF + d, t_in * LANE:(t_in + 1) * LANE, :] = (
                    jnp.concatenate([l_re, l_im], axis=1).astype(BF16))
            t_out = p - 1 if d == 0 else S5_T - p
            if 0 <= t_out < S5_T:
                w_ref[W_CF + d, t_out * LANE:(t_out + 1) * LANE, :] = (
                    jnp.concatenate([v_re, -v_im], axis=1).astype(BF16))
            if p < S5_T:
                taps.append(_dot_nt(jnp.concatenate([l_re, -l_im], axis=1), c_cat))
        kern.append(taps)
        r = lax.broadcasted_iota(jnp.int32, (S5_R, S5_TILE_STATE), 0).astype(F32) * float(S5_T)
        r1 = r + float(S5_T)
        pm = jnp.exp(r * lam)
        qm = jnp.exp(-(r1 * lam))
        tab8_ref[d, 0] = pm * jnp.cos(r * th)
        tab8_ref[d, 1] = pm * jnp.sin(r * th)
        tab8_ref[d, 2] = qm * jnp.cos(r1 * th)
        tab8_ref[d, 3] = -(qm * jnp.sin(r1 * th))
        mc = jnp.exp(float(CHUNK) * lam)
        tab1_ref[d, 0:1, :] = mc * jnp.cos(float(CHUNK) * th)
        tab1_ref[d, 1:2, :] = mc * jnp.sin(float(CHUNK) * th)
    row = lax.broadcasted_iota(jnp.int32, (LANE, LANE), 0)
    col = lax.broadcasted_iota(jnp.int32, (LANE, LANE), 1)
    skip = jnp.where(row == col, d_ref[...], 0.0)
    for t in range(S5_T):
        for t2 in range(S5_T):
            if t < t2:
                blk = kern[0][t2 - t]
            elif t > t2:
                blk = kern[1][t - t2]
            else:
                blk = kern[0][0] + kern[1][0] + skip
            w_ref[W_M, t * LANE:(t + 1) * LANE, t2 * LANE:(t2 + 1) * LANE] = blk.astype(BF16)


def _s5_prep(a_re, a_im, ldt, b_re, b_im, c_re, c_im, dsk):
    vec = pl.BlockSpec((None, 2, 1, S5_TILE_STATE), lambda l, j: (l, 0, 0, j))
    blk = pl.BlockSpec((None, None, LANE, S5_TILE_STATE), lambda l, j: (l, j, 0, 0))
    return pl.pallas_call(
        _s5_prep_kernel,
        grid=(DEPTH, S5_TILES),
        in_specs=[vec, vec, vec, blk, blk, blk, blk,
                  pl.BlockSpec((None, 1, LANE), lambda l, j: (l, 0, j))],
        out_specs=[
            pl.BlockSpec((None, None, 5, S5_ROW, S5_W), lambda l, j: (l, j, 0, 0, 0)),
            pl.BlockSpec((None, None, 2, 4, S5_R, S5_TILE_STATE), lambda l, j: (l, j, 0, 0, 0, 0)),
            pl.BlockSpec((None, None, 2, 2, S5_TILE_STATE), lambda l, j: (l, j, 0, 0, 0)),
        ],
        out_shape=[
            jax.ShapeDtypeStruct((DEPTH, S5_TILES, 5, S5_ROW, S5_W), BF16),
            jax.ShapeDtypeStruct((DEPTH, S5_TILES, 2, 4, S5_R, S5_TILE_STATE), F32),
            jax.ShapeDtypeStruct((DEPTH, S5_TILES, 2, 2, S5_TILE_STATE), F32),
        ],
        compiler_params=_params(("parallel", "parallel")),
        name="s5_prep",
    )(a_re, a_im, ldt, b_re, b_im, c_re, c_im, dsk)


def _s5_scan_kernel(u_ref, x0_ref, w_ref, tab8_ref, tab1_ref, y_ref, fs_ref, u_s, *, nseq, nb):
    groups = nseq * nb
    nrow = groups * S5_R
    ts = S5_TILE_STATE
    u_s[...] = u_ref[...].astype(F32)
    u8 = jnp.concatenate([u_s[pl.ds(t, nrow, stride=S5_T), :] for t in range(S5_T)],
                         axis=1).astype(BF16)
    ef = jnp.dot(u8, w_ref[W_SF], preferred_element_type=F32).reshape(groups, S5_R, S5_W)
    eb = jnp.dot(u8, w_ref[W_SB], preferred_element_type=F32).reshape(groups, S5_R, S5_W)
    rowi = lax.broadcasted_iota(jnp.int32, (groups, S5_R, ts), 1)

    def prefix(x):
        for s in (1, 2, 4):
            x = x + jnp.where(rowi >= s, pltpu.roll(x, s, 1), 0.0)
        return x

    def suffix(x):
        for s in (1, 2, 4):
            x = x + jnp.where(rowi < S5_R - s, pltpu.roll(x, S5_R - s, 1), 0.0)
        return x

    p_re, p_im, q_re, q_im = (tab8_ref[0, i] for i in range(4))
    a_re, a_im = tab1_ref[0, 0:1, :], tab1_ref[0, 1:2, :]
    w_re, w_im = _cmul(q_re, q_im, ef[:, :, :ts], ef[:, :, ts:])
    cs_re, cs_im = prefix(w_re), prefix(w_im)
    st_re, st_im = [], []
    for s in range(nseq):
        x_re, x_im = x0_ref[s, 0, 0:1, :], x0_ref[s, 0, 1:2, :]
        for b in range(nb):
            g = s * nb + b
            st_re.append(x_re)
            st_im.append(x_im)
            x_re, x_im = _cmul(a_re, a_im, x_re + cs_re[g, S5_R - 1:S5_R, :],
                               x_im + cs_im[g, S5_R - 1:S5_R, :])
        fs_ref[s, 0, 0:1, :] = x_re
        fs_ref[s, 0, 1:2, :] = x_im
    xin_re, xin_im = _cmul(p_re, p_im, cs_re - w_re + jnp.stack(st_re), cs_im - w_im + jnp.stack(st_im))
    xin = jnp.concatenate([xin_re, xin_im], axis=2).reshape(nrow, S5_W)

    p_re, p_im, q_re, q_im = (tab8_ref[1, i] for i in range(4))
    a_re, a_im = tab1_ref[1, 0:1, :], tab1_ref[1, 1:2, :]
    w_re, w_im = _cmul(p_re, p_im, eb[:, :, :ts], eb[:, :, ts:])
    sf_re, sf_im = suffix(w_re), suffix(w_im)
    z_re, z_im = [None] * groups, [None] * groups
    for s in range(nseq):
        x_re, x_im = x0_ref[s, 1, 0:1, :], x0_ref[s, 1, 1:2, :]
        for b in reversed(range(nb)):
            g = s * nb + b
            z_re[g], z_im[g] = _cmul(a_re, a_im, x_re, x_im)
            x_re = sf_re[g, 0:1, :] + z_re[g]
            x_im = sf_im[g, 0:1, :] + z_im[g]
        fs_ref[s, 1, 0:1, :] = x_re
        fs_ref[s, 1, 1:2, :] = x_im
    xnx_re, xnx_im = _cmul(q_re, q_im, sf_re - w_re + jnp.stack(z_re), sf_im - w_im + jnp.stack(z_im))
    xnx = jnp.concatenate([xnx_re, xnx_im], axis=2).reshape(nrow, S5_W)

    y8 = (jnp.dot(u8, w_ref[W_M], preferred_element_type=F32)
          + _dot_nt(xin, w_ref[W_CF]) + _dot_nt(xnx, w_ref[W_CB]))
    for t in range(S5_T):
        y_ref[pl.ds(t, nrow, stride=S5_T), :] = y8[:, t * LANE:(t + 1) * LANE]


def _s5_scan(zs, x0, x0_block, x0_idx, layer, wmat, tab8, tab1, nseq, seq):
    n = nseq * seq
    return pl.pallas_call(
        functools.partial(_s5_scan_kernel, nseq=nseq, nb=seq // CHUNK),
        grid=(S5_TILES,),
        in_specs=[
            pl.BlockSpec((n, LANE), lambda j: (0, ZS_U // LANE + j)),
            pl.BlockSpec(x0_block, x0_idx),
            pl.BlockSpec((None, None, 5, S5_ROW, S5_W), lambda j: (layer, j, 0, 0, 0)),
            pl.BlockSpec((None, None, 2, 4, S5_R, S5_TILE_STATE), lambda j: (layer, j, 0, 0, 0, 0)),
            pl.BlockSpec((None, None, 2, 2, S5_TILE_STATE), lambda j: (layer, j, 0, 0, 0)),
        ],
        out_specs=[
            pl.BlockSpec((n, LANE), lambda j: (0, j)),
            pl.BlockSpec((nseq, 2, 2, S5_TILE_STATE), lambda j: (0, 0, 0, j)),
        ],
        out_shape=[
            jax.ShapeDtypeStruct((n, S5_WIDTH), F32),
            jax.ShapeDtypeStruct((nseq, 2, 2, S5_NSTATE), F32),
        ],
        scratch_shapes=[pltpu.VMEM((n, LANE), F32)],
        compiler_params=_params(("parallel",)),
        name="s5_scan",
    )(zs, x0, wmat, tab8, tab1)


def _merge_kernel(x_ref, mod_ref, nw_ref, oa_ref, ob_ref, ys_ref, sg_ref, wglu_ref, bglu_ref, wmg_ref,
                  wa_ref, wb_ref, wc_ref, wout_ref, y_ref, wmg_s):
    @pl.when(pl.program_id(0) == 0)
    def _():
        _pack_transposed(wmg_ref.at[0], wmg_s, 0, 3 * D_MODEL // LANE, 0, None)

    x = x_ref[...]
    mod = mod_ref[0]
    h = _mod_rmsnorm(x, nw_ref[...], mod).astype(BF16)
    g = jax.nn.sigmoid(jnp.dot(h, wmg_s[...], preferred_element_type=F32))
    zg = _dot(jax.nn.gelu(ys_ref[...]), wglu_ref[...]) + bglu_ref[...]
    oc = (zg[:, :S5_WIDTH] * jax.nn.sigmoid(zg[:, S5_WIDTH:])
          * jax.nn.silu(sg_ref[...].astype(F32)))
    mixed = (g[:, 0:D_MODEL] * _dot(oa_ref[...], wa_ref[...])
             + g[:, D_MODEL:2 * D_MODEL] * _dot(ob_ref[...], wb_ref[...])
             + g[:, 2 * D_MODEL:] * _dot(oc, wc_ref[...]))
    y_ref[...] = x + mod[:, 2 * D_MODEL:] * _dot(mixed, wout_ref[...])


def _merge(x2, mod, mod_idx, nw, oa, ob, ys, zs, layer, wglu, bglu, w_in_t, wa, wb, wc, wout):
    n = x2.shape[0]
    tm = ROW_TILE
    const = lambda i: (0, 0)
    rows = lambda w: pl.BlockSpec((tm, w), lambda i: (i, 0))
    return pl.pallas_call(
        _merge_kernel,
        grid=(n // tm,),
        in_specs=[
            rows(D_MODEL),
            pl.BlockSpec((1, 1, 3 * D_MODEL), lambda i: (mod_idx(i), 0, 0)),
            pl.BlockSpec((1, D_MODEL), const),
            rows(GLA_WIDTH), rows(MLA_WIDTH), rows(S5_WIDTH),
            pl.BlockSpec((tm, S5_WIDTH), lambda i: (i, ZS_GATE // S5_WIDTH)),
            pl.BlockSpec((None, S5_WIDTH, 2 * S5_WIDTH), lambda i: (layer, 0, 0)),
            pl.BlockSpec((None, 1, 2 * S5_WIDTH), lambda i: (layer, 0, 0)),
            pl.BlockSpec((pl.Element(1), pl.Element(3 * D_MODEL), pl.Element(D_MODEL)),
                         lambda i: (layer, MERGE_COL, 0), pipeline_mode=pl.Buffered(1)),
            pl.BlockSpec((GLA_WIDTH, D_MODEL), const),
            pl.BlockSpec((MLA_WIDTH, D_MODEL), const),
            pl.BlockSpec((S5_WIDTH, D_MODEL), const),
            pl.BlockSpec((D_MODEL, D_MODEL), const),
        ],
        out_specs=rows(D_MODEL),
        out_shape=jax.ShapeDtypeStruct((n, D_MODEL), F32),
        scratch_shapes=[pltpu.VMEM((D_MODEL, 3 * D_MODEL), BF16)],
        compiler_params=_params(("arbitrary",)),
        name="merge",
    )(x2, mod, nw, oa, ob, ys, zs, wglu, bglu, w_in_t, wa, wb, wc, wout)


def _mla_lane_of_dim():
    half = MLA_ROPE // 2
    first_gap = ROPE_SHIFT - half
    lane = np.zeros(MLA_QK, np.int32)
    for j in range(MLA_NOPE):
        lane[j] = half + j if j < first_gap else 2 * half + j
    for r in range(half):
        lane[MLA_NOPE + r] = r
        lane[MLA_NOPE + half + r] = ROPE_SHIFT + r
    return lane


MLA_LANE_OF_DIM = _mla_lane_of_dim()


def _place_heads(w, heads, lane_of_dim):
    width = len(lane_of_dim)
    src = np.zeros(heads * HEAD_PAD, np.int32)
    used = np.zeros(heads * HEAD_PAD, bool)
    for h in range(heads):
        src[h * HEAD_PAD + lane_of_dim] = h * width + np.arange(width)
        used[h * HEAD_PAD + lane_of_dim] = True
    return jnp.where(jnp.asarray(used), jnp.take(w, jnp.asarray(src), axis=-1), 0.0)


def _rope_tables(n_tok):
    rows = n_tok // GRID_W
    r = jnp.repeat(jnp.arange(rows, dtype=F32), GRID_W)
    col = jnp.tile(jnp.arange(GRID_W, dtype=F32), rows)
    n_freq = MLA_ROPE // 4
    inv = ROPE_THETA ** (-jnp.arange(n_freq, dtype=F32) / n_freq)
    ang = jnp.concatenate([r[:, None] * inv, col[:, None] * inv], axis=-1)
    cos, sin = jnp.cos(ang), jnp.sin(ang)
    ones = jnp.ones((n_tok, MLA_NOPE), F32)
    c = _place_heads(jnp.concatenate([ones, cos, cos], axis=1), 1, MLA_LANE_OF_DIM)
    s = _place_heads(jnp.concatenate([0.0 * ones, -sin, sin], axis=1), 1, MLA_LANE_OF_DIM)
    return jnp.stack([c, s])


def _s5_group_blocks(x):
    lyr = x.shape[0]
    x = x.reshape(lyr, S5_TILES, 8, S5_GROUP, S5_STATE)
    eye = jnp.eye(8, dtype=x.dtype)
    blk = x[:, :, :, :, None, :] * eye[None, None, :, None, :, None]
    return blk.reshape(lyr, S5_TILES, LANE, S5_TILE_STATE)


def kernel(x_prompt, x_sample, c, c_ctx, cache_mla_ckv, cache_mla_krope, state_gla, state_s5,
           norm_w, w_ada, b_ada, w_in, gla_w_a2, gla_b_a, gla_o_norm,
           mla_q_norm, mla_w_uq, mla_kv_norm, mla_w_uk, mla_w_uv, mla_qh_norm, mla_kh_norm,
           s5_a_re, s5_a_im, s5_log_dt, s5_b_re, s5_b_im, s5_c_re, s5_c_im, s5_d, s5_w_glu, s5_b_glu,
           w_bo_gla, w_bo_mla, w_bo_s5, w_out):
    bsz, seq, _ = x_prompt.shape
    dbsz, dseq, _ = x_sample.shape
    ctx_row = 8 - 1
    assert dbsz <= ctx_row and (bsz * seq) % ROW_TILE == 0 and dseq % ROW_TILE == 0

    cond8 = jnp.zeros((8, D_MODEL), F32).at[0:dbsz].set(c).at[ctx_row].set(c_ctx)
    ada = _ada(cond8, w_ada, b_ada)

    vec = lambda a: a.reshape(DEPTH, 2, 1, S5_NSTATE)
    ldt = jnp.repeat(s5_log_dt[..., None], S5_STATE, axis=-1)
    bt = lambda b: _s5_group_blocks(b.transpose(0, 1, 3, 2))
    wmat, tab8, tab1 = _s5_prep(vec(s5_a_re), vec(s5_a_im), vec(ldt), bt(s5_b_re), bt(s5_b_im),
                                _s5_group_blocks(s5_c_re), _s5_group_blocks(s5_c_im),
                                s5_d.reshape(DEPTH, 1, S5_WIDTH))
    wglu = s5_w_glu.astype(BF16)
    bglu = s5_b_glu.reshape(DEPTH, 1, 2 * S5_WIDTH)

    wuq = _place_heads(mla_w_uq, MLA_HEADS, MLA_LANE_OF_DIM).astype(BF16)
    wuk = _place_heads(mla_w_uk, MLA_HEADS, MLA_LANE_OF_DIM[:MLA_NOPE]).astype(BF16)
    wuv = mla_w_uv.astype(BF16)
    qhn = _place_heads(mla_qh_norm, 1, MLA_LANE_OF_DIM)
    khn = _place_heads(mla_kh_norm, 1, MLA_LANE_OF_DIM)
    e_np = np.zeros((LANE, MLA_HEADS * HEAD_PAD), np.float32)
    for h in range(MLA_HEADS):
        for i in range(MLA_ROPE):
            e_np[i, h * HEAD_PAD + MLA_LANE_OF_DIM[MLA_NOPE + i]] = 1.0
    e_place = jnp.asarray(e_np, BF16)
    rope_tab = _rope_tables(dseq)
    ckr_pad = jnp.pad(cache_mla_krope, ((0, 0), (0, 0), (0, 0), (0, LANE - MLA_ROPE)))

    zrow = lambda n: jnp.zeros((DEPTH, n, GLA_QK), F32)
    waf = jnp.concatenate([gla_w_a2[:, 0], zrow(LANE - GLA_RANK)], axis=1).astype(BF16)
    wab = jnp.concatenate([zrow(GLA_RANK), gla_w_a2[:, 1], zrow(LANE - 2 * GLA_RANK)], axis=1).astype(BF16)
    sgla = state_gla.reshape(dbsz, DEPTH, 2, GLA_QK, GLA_DV)
    ss5 = state_s5.reshape(dbsz, DEPTH, 2, 2, S5_NSTATE)
    zero_s5 = jnp.zeros((bsz, 2, 2, S5_NSTATE), F32)

    hp = x_prompt.reshape(bsz * seq, D_MODEL)
    hs = x_sample.reshape(dbsz * dseq, D_MODEL)
    ckv_l, krope_l, gla_l, s5_l = [], [], [], []
    w_in_t = jnp.swapaxes(w_in, 1, 2)
    for l in range(DEPTH):
        mod = ada[l].reshape(8, 1, 3 * D_MODEL)
        nw = norm_w[l].reshape(1, D_MODEL)
        mla_w = (mla_q_norm[l].reshape(1, -1), wuq[l], mla_kv_norm[l].reshape(1, -1), wuk[l], wuv[l],
                 qhn[l].reshape(1, -1), khn[l].reshape(1, -1), e_place)
        wbo = (w_bo_gla[l].astype(BF16), w_bo_mla[l].astype(BF16), w_bo_s5[l].astype(BF16))
        wout = w_out[l].astype(BF16)
        onorm = gla_o_norm[l].reshape(1, GLA_DV)

        def layer(x2, nb, n, mod_idx, ctx):
            zg, zm, zs = _in_proj(x2, mod, mod_idx, nw, w_in_t, l)
            if ctx:
                gctx = (sgla, l)
                x0, x0_blk = ss5, (nb, None, 2, 2, S5_TILE_STATE)
                x0_idx = lambda j: (0, l, 0, 0, j)
                mctx, rt = (cache_mla_ckv, ckr_pad, l), rope_tab
            else:
                gctx = None
                x0, x0_blk = zero_s5, (nb, 2, 2, S5_TILE_STATE)
                x0_idx = lambda j: (0, 0, 0, j)
                mctx, rt = None, None
            oa, st_gla = _gla(zg, gctx, waf[l], wab[l], gla_b_a[l], onorm, nb, n)
            ob, ckv = _mla(zm, mctx, mla_w, rt, nb, n)
            y_ssm, st_s5 = _s5_scan(zs, x0, x0_blk, x0_idx, l, wmat, tab8, tab1, nb, n)
            y = _merge(x2, mod, mod_idx, nw, oa, ob, y_ssm, zs, l, wglu, bglu, w_in_t, *wbo, wout)
            return y, zm, ckv, st_gla, st_s5

        hp, zm_p, ckv_p, st_gla_p, st_s5_p = layer(hp, bsz, seq, lambda i: ctx_row, False)
        ckv_l.append(ckv_p.reshape(bsz, seq, MLA_KV_LORA))
        krope_l.append(zm_p[:, ZM_KR:ZM_KR + MLA_ROPE].astype(F32).reshape(bsz, seq, MLA_ROPE))
        gla_l.append(st_gla_p.reshape(bsz, 2, GLA_HEADS, GLA_DK, GLA_DV))
        s5_l.append(st_s5_p.reshape(bsz, 2, 2, S5_GROUPS, S5_STATE))
        blocks_per_seq = dseq // ROW_TILE
        hs = layer(hs, dbsz, dseq, lambda i: i // blocks_per_seq, True)[0]

    return (hp.reshape(bsz, seq, D_MODEL), hs.reshape(dbsz, dseq, D_MODEL),
            jnp.stack(ckv_l, axis=1), jnp.stack(krope_l, axis=1),
            jnp.stack(gla_l, axis=1), jnp.stack(s5_l, axis=1))
```

```python
import functools

import jax
import jax.numpy as jnp
import numpy as np
from jax import lax
from jax.experimental import pallas as pl
from jax.experimental.pallas import tpu as pltpu

F32 = jnp.float32
BF16 = jnp.bfloat16

EPS = 1e-6
D_MODEL = 1024
DEPTH = 2
GRID_W = 64
ROPE_THETA = 10000.0
GLA_HEADS = 4
GLA_DK = 64
GLA_DV = 128
GLA_RANK = 16
GLA_GATE_NORM = 16.0
GLA_QK = GLA_HEADS * GLA_DK
GLA_WIDTH = GLA_HEADS * GLA_DV
MLA_HEADS = 4
MLA_Q_LORA = 384
MLA_KV_LORA = 256
MLA_NOPE = 64
MLA_ROPE = 32
MLA_QK = MLA_NOPE + MLA_ROPE
MLA_DV = 128
MLA_WIDTH = MLA_HEADS * MLA_DV
S5_WIDTH = 512
S5_GROUP = 16
S5_GROUPS = 32
S5_STATE = 64
S5_NSTATE = S5_GROUPS * S5_STATE

LANE = 128
HEAD_PAD = LANE
ROPE_SHIFT = LANE // 2
CHUNK = 64
GLA_STEP = 256
GLA_SEQS_PER_STEP = 2
GLA_ROWS_PER_STEP = 1024
S5_TILES = S5_WIDTH // LANE
S5_TILE_STATE = S5_NSTATE // S5_TILES
ROW_TILE = 512
Q_TILE = 512
PROJ_TILE = 256
VMEM_LIMIT = 56 * 1024 * 1024

MERGE_COL = 3776
ZG_Q, ZG_K, ZG_V, ZG_A, ZG_GATE, ZG_W = 0, 256, 512, 1024, 1152, 1664
ZM_Q, ZM_KV, ZM_KR, ZM_GATE, ZM_W = 0, 384, 640, 768, 1280
ZS_U, ZS_GATE, ZS_W = 0, 512, 1024
ZG_BASE, ZM_BASE, ZS_BASE, PACK_W = 0, ZG_W, ZG_W + ZM_W, ZG_W + ZM_W + ZS_W
IN_PIECES = (
    (0, 8, ZG_BASE + ZG_Q, None),
    (1024, 1, ZG_BASE + ZG_A, 2 * GLA_RANK),
    (1056, 4, ZG_BASE + ZG_GATE, None),
    (1568, 5, ZM_BASE + ZM_Q, None),
    (2208, 1, ZM_BASE + ZM_KR, MLA_ROPE),
    (2240, 4, ZM_BASE + ZM_GATE, None),
    (2752, 8, ZS_BASE + ZS_U, None),
)


def _dot(a, b):
    return jnp.dot(a.astype(BF16), b.astype(BF16), preferred_element_type=F32)


def _dot_nt(a, b):
    return lax.dot_general(a.astype(BF16), b.astype(BF16), (((1,), (1,)), ((), ())),
                           preferred_element_type=F32)


def _split_bf16(x, parts):
    out = []
    r = x
    for _ in range(parts):
        p = r.astype(BF16)
        out.append(p)
        r = r - p.astype(F32)
    return out


def _params(sem):
    return pltpu.CompilerParams(dimension_semantics=sem, vmem_limit_bytes=VMEM_LIMIT)


def _ada_kernel(c_ref, w_ref, b_ref, o_ref):
    s = jax.nn.silu(c_ref[...])
    o_ref[...] = _dot(s, w_ref[...]) + b_ref[...]


def _ada(cond8, w_ada, b_ada):
    tn = 1024
    return pl.pallas_call(
        _ada_kernel,
        grid=(DEPTH, 3 * D_MODEL // tn),
        in_specs=[
            pl.BlockSpec((8, D_MODEL), lambda l, n: (0, 0)),
            pl.BlockSpec((None, D_MODEL, tn), lambda l, n: (l, 0, n)),
            pl.BlockSpec((None, 1, tn), lambda l, n: (l, 0, n)),
        ],
        out_specs=pl.BlockSpec((None, 8, tn), lambda l, n: (l, 0, n)),
        out_shape=jax.ShapeDtypeStruct((DEPTH, 8, 3 * D_MODEL), F32),
        compiler_params=_params(("parallel", "parallel")),
        name="ada",
    )(cond8, w_ada, b_ada.reshape(DEPTH, 1, 3 * D_MODEL))


def _mod_rmsnorm(x, nw, mod):
    ms = jnp.mean(x * x, axis=-1, keepdims=True)
    y = x * lax.rsqrt(ms + EPS) * nw
    return y * (1.0 + mod[:, D_MODEL:2 * D_MODEL]) + mod[:, 0:D_MODEL]


def _pack_transposed(w_ref, wb_s, src, tiles, dst, keep):
    lane = lax.broadcasted_iota(jnp.int32, (D_MODEL, LANE), 1)
    for t in range(tiles):
        blk = w_ref[src + t * LANE:src + (t + 1) * LANE, :].T
        if keep is not None:
            blk = jnp.where(lane < keep, blk, 0.0)
        wb_s[:, dst + t * LANE:dst + (t + 1) * LANE] = blk.astype(BF16)


def _in_proj_kernel(x_ref, mod_ref, nw_ref, w_ref, zg_ref, zm_ref, zs_ref, wb_s):
    @pl.when(pl.program_id(0) == 0)
    def _():
        for src, tiles, dst, keep in IN_PIECES:
            _pack_transposed(w_ref, wb_s, src, tiles, dst, keep)

    h = _mod_rmsnorm(x_ref[...], nw_ref[...], mod_ref[0]).astype(BF16)
    z = jnp.dot(h, wb_s[...], preferred_element_type=F32)
    zg_ref[...] = z[:, ZG_BASE:ZG_BASE + ZG_W].astype(BF16)
    zm_ref[...] = z[:, ZM_BASE:ZM_BASE + ZM_W].astype(BF16)
    zs_ref[...] = z[:, ZS_BASE:ZS_BASE + ZS_W].astype(BF16)


def _in_proj(x2, mod, mod_idx, nw, w_in_t, layer):
    n = x2.shape[0]
    tm = ROW_TILE
    const = lambda i: (0, 0)
    return pl.pallas_call(
        _in_proj_kernel,
        grid=(n // tm,),
        in_specs=[
            pl.BlockSpec((tm, D_MODEL), lambda i: (i, 0)),
            pl.BlockSpec((1, 1, 3 * D_MODEL), lambda i: (mod_idx(i), 0, 0)),
            pl.BlockSpec((1, D_MODEL), const),
            pl.BlockSpec((None, MERGE_COL, D_MODEL), lambda i: (layer, 0, 0), pipeline_mode=pl.Buffered(1)),
        ],
        out_specs=[
            pl.BlockSpec((tm, ZG_W), lambda i: (i, 0)),
            pl.BlockSpec((tm, ZM_W), lambda i: (i, 0)),
            pl.BlockSpec((tm, ZS_W), lambda i: (i, 0)),
        ],
        out_shape=[
            jax.ShapeDtypeStruct((n, ZG_W), BF16),
            jax.ShapeDtypeStruct((n, ZM_W), BF16),
            jax.ShapeDtypeStruct((n, ZS_W), BF16),
        ],
        scratch_shapes=[pltpu.VMEM((D_MODEL, PACK_W), BF16)],
        compiler_params=_params(("arbitrary",)),
        name="in_proj",
    )(x2, mod, nw, w_in_t)


def _gla_kernel(*refs, nsteps, seq, nseq, has_ctx):
    it = iter(refs)
    zg_ref = next(it)
    s0_ref = next(it) if has_ctx else None
    waf_ref, wab_ref, ba_ref, onorm_ref, o_ref, sfin_ref, la_s, o_s, st_s = (next(it) for _ in range(9))
    chains = [(g, d) for g in range(nseq) for d in (0, 1)]
    inv_norm = 1.0 / GLA_GATE_NORM
    zero_blk = jnp.zeros((GLA_DK, GLA_DV), F32)
    for ch, (g, d) in enumerate(chains):
        if d == 0:
            a_blk = zg_ref[g * seq:(g + 1) * seq, ZG_A:ZG_A + LANE]
        wa_ref = waf_ref if d == 0 else wab_ref
        la_s[ch] = jax.nn.log_sigmoid(_dot(a_blk, wa_ref[...]) + ba_ref[d:d + 1, :]) * inv_norm
        if has_ctx:
            s0 = s0_ref[g, d]
            rows_bd = []
            for h in range(GLA_HEADS):
                sh = s0[h * GLA_DK:(h + 1) * GLA_DK, :]
                rows_bd.append(jnp.concatenate([sh if h2 == h else zero_blk for h2 in range(GLA_HEADS)], axis=1))
            st_s[ch] = jnp.concatenate(rows_bd, axis=0).T
        else:
            st_s[ch] = jnp.zeros((GLA_WIDTH, GLA_QK), F32)

    def iota(shape, axis, shift):
        return lax.shift_right_logical(lax.broadcasted_iota(jnp.int32, shape, axis), shift)

    log_chunk, log_dv = CHUNK.bit_length() - 1, GLA_DV.bit_length() - 1
    row = lax.broadcasted_iota(jnp.int32, (GLA_STEP, GLA_STEP), 0)
    col = lax.broadcasted_iota(jnp.int32, (GLA_STEP, GLA_STEP), 1)
    same_chunk = iota((GLA_STEP, GLA_STEP), 0, log_chunk) == iota((GLA_STEP, GLA_STEP), 1, log_chunk)
    masks = (same_chunk & (row >= col), same_chunk & (row <= col))
    lane_head = iota((GLA_STEP, GLA_QK), 1, log_chunk)
    row_chunk = iota((GLA_STEP, GLA_QK), 0, log_chunk)
    state_blk = iota((GLA_WIDTH, GLA_QK), 0, log_dv) == iota((GLA_WIDTH, GLA_QK), 1, log_chunk)
    qscale = GLA_DK ** -0.5
    nch = GLA_STEP // CHUNK
    n_chain = len(chains)

    def step(i, carry):
        rows, cum = [], []
        for ch, (g, d) in enumerate(chains):
            r0 = pl.multiple_of((i if d == 0 else nsteps - 1 - i) * GLA_STEP, GLA_STEP)
            rows.append((pl.ds(g * seq + r0, GLA_STEP), pl.ds(r0, GLA_STEP)))
            a_hi, a_lo = _split_bf16(la_s[ch, rows[ch][1], :], 2)
            tri = masks[d].astype(BF16)
            cum.append(jnp.dot(tri, a_hi, preferred_element_type=F32)
                       + jnp.dot(tri, a_lo, preferred_element_type=F32))
        blast, v, v_t, qd, kd, kr = [], [], [], [], [], []
        for ch, (g, d) in enumerate(chains):
            edge = CHUNK - 1 if d == 0 else 0
            blast.append([cum[ch][c * CHUNK + edge:c * CHUNK + edge + 1, :] for c in range(nch)])
            bl = jnp.concatenate([jnp.broadcast_to(b, (CHUNK, GLA_QK)) for b in blast[ch]], axis=0)
            zrows = rows[ch][0]
            q = zg_ref[zrows, ZG_Q:ZG_Q + GLA_QK].astype(F32) * qscale
            k = zg_ref[zrows, ZG_K:ZG_K + GLA_QK].astype(F32)
            v.append(zg_ref[zrows, ZG_V:ZG_V + GLA_WIDTH])
            v_t.append(v[ch].astype(F32).T.astype(BF16))
            qd.append(q * jnp.exp(cum[ch]))
            kd.append((k * jnp.exp(-cum[ch])).astype(BF16))
            kr.append(k * jnp.exp(bl - cum[ch]))
        outs = [[] for _ in chains]
        for h in range(GLA_HEADS):
            for ch, (g, d) in enumerate(chains):
                qh = jnp.where(lane_head == h, qd[ch], 0.0)
                att = jnp.where(masks[d], _dot_nt(qh, kd[ch]), 0.0)
                outs[ch].append(_dot(att, v[ch][:, h * GLA_DV:(h + 1) * GLA_DV]))
        s = [st_s[ch] for ch in range(n_chain)]
        inter = [[None] * nch for _ in chains]
        for j in range(nch):
            for ch, (g, d) in enumerate(chains):
                c = j if d == 0 else nch - 1 - j
                inter[ch][c] = _dot_nt(qd[ch][c * CHUNK:(c + 1) * CHUNK, :], s[ch])
                kv_t = jnp.where(state_blk, _dot(v_t[ch], jnp.where(row_chunk == c, kr[ch], 0.0)), 0.0)
                s[ch] = s[ch] * jnp.exp(blast[ch][c]) + kv_t
        for ch in range(n_chain):
            st_s[ch] = s[ch]
            o_s[ch, rows[ch][1], :] = jnp.concatenate(outs[ch], axis=1) + jnp.concatenate(inter[ch], axis=0)
        return carry

    lax.fori_loop(0, nsteps, step, 0)
    onorm = onorm_ref[...]
    for ch, (g, d) in enumerate(chains):
        s_fin = st_s[ch].T
        for h in range(GLA_HEADS):
            sfin_ref[g, d, h * GLA_DK:(h + 1) * GLA_DK, :] = (
                s_fin[h * GLA_DK:(h + 1) * GLA_DK, h * GLA_DV:(h + 1) * GLA_DV])
    for g in range(nseq):
        srows = slice(g * seq, (g + 1) * seq)
        o = o_s[2 * g] + o_s[2 * g + 1]
        gate = zg_ref[srows, ZG_GATE:ZG_GATE + GLA_WIDTH].astype(F32)
        for h in range(GLA_HEADS):
            vs = slice(h * GLA_DV, (h + 1) * GLA_DV)
            oh = o[:, vs]
            ms = jnp.mean(oh * oh, axis=-1, keepdims=True)
            o_ref[srows, vs] = oh * lax.rsqrt(ms + EPS) * onorm * jax.nn.silu(gate[:, vs])


def _gla(zg, ctx, waf, wab, ba, onorm, bsz, seq):
    const = lambda b: (0, 0)
    nseq = max(GLA_SEQS_PER_STEP, GLA_ROWS_PER_STEP // seq)
    in_specs = [pl.BlockSpec((nseq * seq, ZG_W), lambda b: (b, 0))]
    args = [zg]
    if ctx is not None:
        s0, layer = ctx
        in_specs.append(pl.BlockSpec((nseq, None, 2, GLA_QK, GLA_DV), lambda b: (b, layer, 0, 0, 0)))
        args.append(s0)
    in_specs += [
        pl.BlockSpec((LANE, GLA_QK), const),
        pl.BlockSpec((LANE, GLA_QK), const),
        pl.BlockSpec((2, GLA_QK), const),
        pl.BlockSpec((1, GLA_DV), const),
    ]
    return pl.pallas_call(
        functools.partial(_gla_kernel, nsteps=seq // GLA_STEP, seq=seq, nseq=nseq, has_ctx=ctx is not None),
        grid=(bsz // nseq,),
        in_specs=in_specs,
        out_specs=[
            pl.BlockSpec((nseq * seq, GLA_WIDTH), lambda b: (b, 0)),
            pl.BlockSpec((nseq, 2, GLA_QK, GLA_DV), lambda b: (b, 0, 0, 0)),
        ],
        out_shape=[
            jax.ShapeDtypeStruct((bsz * seq, GLA_WIDTH), F32),
            jax.ShapeDtypeStruct((bsz, 2, GLA_QK, GLA_DV), F32),
        ],
        scratch_shapes=[
            pltpu.VMEM((2 * nseq, seq, GLA_QK), F32),
            pltpu.VMEM((2 * nseq, seq, GLA_WIDTH), F32),
            pltpu.VMEM((2 * nseq, GLA_WIDTH, GLA_QK), F32),
        ],
        compiler_params=_params(("parallel",)),
        name="gla",
    )(*args, waf, wab, ba, onorm)


def _rms(x, w):
    ms = jnp.mean(x * x, axis=-1, keepdims=True)
    return x * lax.rsqrt(ms + EPS) * w


def _head_sums_mxu(x):
    width = x.shape[-1]
    shift = HEAD_PAD.bit_length() - 1
    gi = lax.shift_right_logical(lax.broadcasted_iota(jnp.int32, (width, width), 0), shift)
    gj = lax.shift_right_logical(lax.broadcasted_iota(jnp.int32, (width, width), 1), shift)
    return _dot(x * x, jnp.where(gi == gj, 1.0, 0.0))


def _head_norm(x, w, rope, on_mxu):
    sums = _head_sums_mxu(x) if on_mxu else None
    outs = []
    for h in range(MLA_HEADS):
        hs = slice(h * HEAD_PAD, (h + 1) * HEAD_PAD)
        xh = x[:, hs]
        ss = sums[:, hs] if on_mxu else jnp.sum(xh * xh, axis=-1, keepdims=True)
        yh = xh * lax.rsqrt(ss * (1.0 / MLA_QK) + EPS) * w
        if rope is not None:
            c, s = rope
            yh = yh * c + pltpu.roll(yh, ROPE_SHIFT, 1) * s
        outs.append(yh)
    return outs


def _place_rope_key(kr, e):
    return sum(jnp.dot(p, e, preferred_element_type=F32) for p in _split_bf16(kr, 3))


def _mla_kernel(*refs, seq, n_ctx, use_rope):
    it = iter(refs)
    zm_ref = next(it)
    if n_ctx:
        cckv_ref, ckr_ref = next(it), next(it)
    qn_ref, wuq_ref, kvn_ref, wuk_ref, wuv_ref, qhn_ref, khn_ref, e_ref = (next(it) for _ in range(8))
    rope_ref = next(it) if use_rope else None
    o_ref, ckv_ref = next(it), next(it)
    q_s, k_s, v_s = next(it), next(it), next(it)

    qscale = MLA_QK ** -0.5
    heads = [slice(h * HEAD_PAD, (h + 1) * HEAD_PAD) for h in range(MLA_HEADS)]

    def keys_values(ckv, k_rope_placed, rope, k_rows):
        kh = _head_norm(_dot(ckv, wuk_ref[...]) + k_rope_placed, khn_ref[...], rope, False)
        for h, hs in enumerate(heads):
            k_s[k_rows, hs] = kh[h].astype(BF16)
        v_s[k_rows, :] = _dot(ckv, wuv_ref[...]).astype(BF16)

    def latent_tile(i, carry):
        r0 = pl.multiple_of(i * PROJ_TILE, PROJ_TILE)
        rows = pl.ds(r0, PROJ_TILE)
        rope = (rope_ref[0, rows, :], rope_ref[1, rows, :]) if use_rope else None
        ckv = _rms(zm_ref[rows, ZM_KV:ZM_KV + MLA_KV_LORA].astype(F32), kvn_ref[...])
        ckv_ref[rows, :] = ckv
        k_pe = jnp.dot(zm_ref[rows, ZM_KR:ZM_KR + LANE], e_ref[...], preferred_element_type=F32)
        keys_values(ckv, k_pe, rope, pl.ds(n_ctx + r0, PROJ_TILE))
        cq = _rms(zm_ref[rows, ZM_Q:ZM_Q + MLA_Q_LORA].astype(F32), qn_ref[...])
        qh = _head_norm(_dot(cq, wuq_ref[...]), qhn_ref[...], rope, True)
        for h, hs in enumerate(heads):
            q_s[rows, hs] = (qh[h] * qscale).astype(BF16)
        return carry

    lax.fori_loop(0, seq // PROJ_TILE, latent_tile, 0)

    def context_tile(i, carry):
        rows = pl.ds(pl.multiple_of(i * PROJ_TILE, PROJ_TILE), PROJ_TILE)
        keys_values(cckv_ref[rows, :], _place_rope_key(ckr_ref[rows, :], e_ref[...]), None, rows)
        return carry

    if n_ctx:
        lax.fori_loop(0, n_ctx // PROJ_TILE, context_tile, 0)

    q_tile = min(seq, Q_TILE)

    def q_block(i, carry):
        rows = pl.ds(pl.multiple_of(i * q_tile, q_tile), q_tile)
        gate = zm_ref[rows, ZM_GATE:ZM_GATE + MLA_WIDTH].astype(F32)
        s = [lax.dot_general(q_s[rows, hs], k_s[:, hs], (((1,), (1,)), ((), ())),
                             preferred_element_type=F32) for hs in heads]
        p, l = [], []
        for h in range(MLA_HEADS):
            e = jnp.exp(s[h] - jnp.max(s[h], axis=-1, keepdims=True))
            l.append(jnp.sum(e, axis=-1, keepdims=True))
            p.append(e.astype(BF16))
        for h, hs in enumerate(heads):
            o = jnp.dot(p[h], v_s[:, hs], preferred_element_type=F32) / l[h]
            o_ref[rows, hs] = o * jax.nn.silu(gate[:, hs])
        return carry

    lax.fori_loop(0, seq // q_tile, q_block, 0)


def _mla(zm, ctx, w, rope_tab, bsz, seq):
    const2 = lambda b: (0, 0)
    n_ctx = 0 if ctx is None else ctx[0].shape[-2]
    in_specs = [pl.BlockSpec((seq, ZM_W), lambda b: (b, 0))]
    args = [zm]
    if ctx is not None:
        cckv, ckr, layer = ctx
        in_specs += [
            pl.BlockSpec((None, None, n_ctx, MLA_KV_LORA), lambda b: (b, layer, 0, 0)),
            pl.BlockSpec((None, None, n_ctx, LANE), lambda b: (b, layer, 0, 0)),
        ]
        args += [cckv, ckr]
    in_specs += [
        pl.BlockSpec((1, MLA_Q_LORA), const2),
        pl.BlockSpec((MLA_Q_LORA, MLA_HEADS * HEAD_PAD), const2),
        pl.BlockSpec((1, MLA_KV_LORA), const2),
        pl.BlockSpec((MLA_KV_LORA, MLA_HEADS * HEAD_PAD), const2),
        pl.BlockSpec((MLA_KV_LORA, MLA_WIDTH), const2),
        pl.BlockSpec((1, HEAD_PAD), const2),
        pl.BlockSpec((1, HEAD_PAD), const2),
        pl.BlockSpec((LANE, MLA_HEADS * HEAD_PAD), const2),
    ]
    args += list(w)
    if rope_tab is not None:
        in_specs.append(pl.BlockSpec((2, seq, HEAD_PAD), lambda b: (0, 0, 0)))
        args.append(rope_tab)
    return pl.pallas_call(
        functools.partial(_mla_kernel, seq=seq, n_ctx=n_ctx, use_rope=rope_tab is not None),
        grid=(bsz,),
        in_specs=in_specs,
        out_specs=[
            pl.BlockSpec((seq, MLA_WIDTH), lambda b: (b, 0)),
            pl.BlockSpec((seq, MLA_KV_LORA), lambda b: (b, 0)),
        ],
        out_shape=[
            jax.ShapeDtypeStruct((bsz * seq, MLA_WIDTH), F32),
            jax.ShapeDtypeStruct((bsz * seq, MLA_KV_LORA), F32),
        ],
        scratch_shapes=[
            pltpu.VMEM((seq, MLA_HEADS * HEAD_PAD), BF16),
            pltpu.VMEM((n_ctx + seq, MLA_HEADS * HEAD_PAD), BF16),
            pltpu.VMEM((n_ctx + seq, MLA_WIDTH), BF16),
        ],
        compiler_params=_params(("parallel",)),
        name="mla",
    )(*args)


S5_T = 8
S5_R = CHUNK // S5_T
S5_ROW = S5_T * LANE
S5_W = 2 * S5_TILE_STATE
W_M, W_SF, W_SB, W_CF, W_CB = range(5)


def _cmul(ar, ai, br, bi):
    return ar * br - ai * bi, ar * bi + ai * br


def _s5_prep_kernel(are_ref, aim_ref, ldt_ref, bre_ref, bim_ref, cre_ref, cim_ref, d_ref,
                    w_ref, tab8_ref, tab1_ref):
    gr = lax.shift_right_logical(lax.broadcasted_iota(jnp.int32, (LANE, S5_TILE_STATE), 0),
                                 S5_GROUP.bit_length() - 1)
    gc = lax.shift_right_logical(lax.broadcasted_iota(jnp.int32, (LANE, S5_TILE_STATE), 1),
                                 S5_STATE.bit_length() - 1)
    spread = lambda ref: jnp.where(gr == gc, jnp.concatenate([ref[...]] * (LANE // S5_GROUP), axis=1), 0.0)
    b_re, b_im = spread(bre_ref), spread(bim_ref)
    c_re, c_im = spread(cre_ref), spread(cim_ref)
    c_cat = jnp.concatenate([c_re, c_im], axis=1).astype(BF16)
    kern = []
    for d in (0, 1):
        a_re, a_im = are_ref[d], aim_ref[d]
        dt = jnp.exp(ldt_ref[d])
        lam = a_re * dt
        th = a_im * dt
        mag = jnp.exp(lam)
        ab_re = mag * jnp.cos(th)
        ab_im = mag * jnp.sin(th)
        den = a_re * a_re + a_im * a_im
        n_re = ab_re - 1.0
        cf_re = (n_re * a_re + ab_im * a_im) / den
        cf_im = (ab_im * a_re - n_re * a_im) / den
        bp_re, bp_im = _cmul(b_re, b_im, cf_re, cf_im)
        k = lax.broadcasted_iota(jnp.int32, (2 * S5_T, S5_TILE_STATE), 0).astype(F32)
        pmag = jnp.exp(k * lam)
        pw_re = pmag * jnp.cos(k * th)
        pw_im = pmag * jnp.sin(k * th)
        taps = []
        for p in range(S5_T + 1):
            ar, ai = pw_re[p:p + 1, :], pw_im[p:p + 1, :]
            l_re, l_im = _cmul(bp_re, bp_im, ar, ai)
            v_re, v_im = _cmul(c_re, c_im, ar, ai)
            t_in = S5_T - 1 - p if d == 0 else p
            if 0 <= t_in < S5_T:
                w_ref[W_SF + d, t_in * LANE:(t_in + 1) * LANE, :] = (
                    jnp.concatenate([l_re, l_im], axis=1).astype(BF16))
            t_out = p - 1 if d == 0 else S5_T - p
            if 0 <= t_out < S5_T:
                w_ref[W_CF + d, t_out * LANE:(t_out + 1) * LANE, :] = (
                    jnp.concatenate([v_re, -v_im], axis=1).astype(BF16))
            if p < S5_T:
                taps.append(_dot_nt(jnp.concatenate([l_re, -l_im], axis=1), c_cat))
        kern.append(taps)
        r = lax.broadcasted_iota(jnp.int32, (S5_R, S5_TILE_STATE), 0).astype(F32) * float(S5_T)
        r1 = r + float(S5_T)
        pm = jnp.exp(r * lam)
        qm = jnp.exp(-(r1 * lam))
        tab8_ref[d, 0] = pm * jnp.cos(r * th)
        tab8_ref[d, 1] = pm * jnp.sin(r * th)
        tab8_ref[d, 2] = qm * jnp.cos(r1 * th)
        tab8_ref[d, 3] = -(qm * jnp.sin(r1 * th))
        mc = jnp.exp(float(CHUNK) * lam)
        tab1_ref[d, 0:1, :] = mc * jnp.cos(float(CHUNK) * th)
        tab1_ref[d, 1:2, :] = mc * jnp.sin(float(CHUNK) * th)
    row = lax.broadcasted_iota(jnp.int32, (LANE, LANE), 0)
    col = lax.broadcasted_iota(jnp.int32, (LANE, LANE), 1)
    skip = jnp.where(row == col, d_ref[...], 0.0)
    for t in range(S5_T):
        for t2 in range(S5_T):
            if t < t2:
                blk = kern[0][t2 - t]
            elif t > t2:
                blk = kern[1][t - t2]
            else:
                blk = kern[0][0] + kern[1][0] + skip
            w_ref[W_M, t * LANE:(t + 1) * LANE, t2 * LANE:(t2 + 1) * LANE] = blk.astype(BF16)


def _s5_prep(a_re, a_im, ldt, b_re, b_im, c_re, c_im, dsk):
    vec = pl.BlockSpec((None, 2, 1, S5_TILE_STATE), lambda l, j: (l, 0, 0, j))
    blk = pl.BlockSpec((None, None, LANE, S5_STATE), lambda l, j: (l, j, 0, 0))
    return pl.pallas_call(
        _s5_prep_kernel,
        grid=(DEPTH, S5_TILES),
        in_specs=[vec, vec, vec, blk, blk, blk, blk,
                  pl.BlockSpec((None, 1, LANE), lambda l, j: (l, 0, j))],
        out_specs=[
            pl.BlockSpec((None, None, 5, S5_ROW, S5_W), lambda l, j: (l, j, 0, 0, 0)),
            pl.BlockSpec((None, None, 2, 4, S5_R, S5_TILE_STATE), lambda l, j: (l, j, 0, 0, 0, 0)),
            pl.BlockSpec((None, None, 2, 2, S5_TILE_STATE), lambda l, j: (l, j, 0, 0, 0)),
        ],
        out_shape=[
            jax.ShapeDtypeStruct((DEPTH, S5_TILES, 5, S5_ROW, S5_W), BF16),
            jax.ShapeDtypeStruct((DEPTH, S5_TILES, 2, 4, S5_R, S5_TILE_STATE), F32),
            jax.ShapeDtypeStruct((DEPTH, S5_TILES, 2, 2, S5_TILE_STATE), F32),
        ],
        compiler_params=_params(("parallel", "parallel")),
        name="s5_prep",
    )(a_re, a_im, ldt, b_re, b_im, c_re, c_im, dsk)


def _s5_scan_kernel(u_ref, x0_ref, w_ref, tab8_ref, tab1_ref, y_ref, fs_ref, u_s, *, nseq, nb):
    groups = nseq * nb
    nrow = groups * S5_R
    ts = S5_TILE_STATE
    u_s[...] = u_ref[...].astype(F32)
    u8 = jnp.concatenate([u_s[pl.ds(t, nrow, stride=S5_T), :] for t in range(S5_T)],
                         axis=1).astype(BF16)
    ef = jnp.dot(u8, w_ref[W_SF], preferred_element_type=F32).reshape(groups, S5_R, S5_W)
    eb = jnp.dot(u8, w_ref[W_SB], preferred_element_type=F32).reshape(groups, S5_R, S5_W)
    rowi = lax.broadcasted_iota(jnp.int32, (groups, S5_R, ts), 1)

    def prefix(x):
        for s in (1, 2, 4):
            x = x + jnp.where(rowi >= s, pltpu.roll(x, s, 1), 0.0)
        return x

    def suffix(x):
        for s in (1, 2, 4):
            x = x + jnp.where(rowi < S5_R - s, pltpu.roll(x, S5_R - s, 1), 0.0)
        return x

    p_re, p_im, q_re, q_im = (tab8_ref[0, i] for i in range(4))
    a_re, a_im = tab1_ref[0, 0:1, :], tab1_ref[0, 1:2, :]
    w_re, w_im = _cmul(q_re, q_im, ef[:, :, :ts], ef[:, :, ts:])
    cs_re, cs_im = prefix(w_re), prefix(w_im)
    st_re, st_im = [], []
    for s in range(nseq):
        x_re, x_im = x0_ref[s, 0, 0:1, :], x0_ref[s, 0, 1:2, :]
        for b in range(nb):
            g = s * nb + b
            st_re.append(x_re)
            st_im.append(x_im)
            x_re, x_im = _cmul(a_re, a_im, x_re + cs_re[g, S5_R - 1:S5_R, :],
                               x_im + cs_im[g, S5_R - 1:S5_R, :])
        fs_ref[s, 0, 0:1, :] = x_re
        fs_ref[s, 0, 1:2, :] = x_im
    xin_re, xin_im = _cmul(p_re, p_im, cs_re - w_re + jnp.stack(st_re), cs_im - w_im + jnp.stack(st_im))
    xin = jnp.concatenate([xin_re, xin_im], axis=2).reshape(nrow, S5_W)

    p_re, p_im, q_re, q_im = (tab8_ref[1, i] for i in range(4))
    a_re, a_im = tab1_ref[1, 0:1, :], tab1_ref[1, 1:2, :]
    w_re, w_im = _cmul(p_re, p_im, eb[:, :, :ts], eb[:, :, ts:])
    sf_re, sf_im = suffix(w_re), suffix(w_im)
    z_re, z_im = [None] * groups, [None] * groups
    for s in range(nseq):
        x_re, x_im = x0_ref[s, 1, 0:1, :], x0_ref[s, 1, 1:2, :]
        for b in reversed(range(nb)):
            g = s * nb + b
            z_re[g], z_im[g] = _cmul(a_re, a_im, x_re, x_im)
            x_re = sf_re[g, 0:1, :] + z_re[g]
            x_im = sf_im[g, 0:1, :] + z_im[g]
        fs_ref[s, 1, 0:1, :] = x_re
        fs_ref[s, 1, 1:2, :] = x_im
    xnx_re, xnx_im = _cmul(q_re, q_im, sf_re - w_re + jnp.stack(z_re), sf_im - w_im + jnp.stack(z_im))
    xnx = jnp.concatenate([xnx_re, xnx_im], axis=2).reshape(nrow, S5_W)

    y8 = (jnp.dot(u8, w_ref[W_M], preferred_element_type=F32)
          + _dot_nt(xin, w_ref[W_CF]) + _dot_nt(xnx, w_ref[W_CB]))
    for t in range(S5_T):
        y_ref[pl.ds(t, nrow, stride=S5_T), :] = y8[:, t * LANE:(t + 1) * LANE]


def _s5_scan(zs, x0, x0_block, x0_idx, layer, wmat, tab8, tab1, nseq, seq):
    n = nseq * seq
    return pl.pallas_call(
        functools.partial(_s5_scan_kernel, nseq=nseq, nb=seq // CHUNK),
        grid=(S5_TILES,),
        in_specs=[
            pl.BlockSpec((n, LANE), lambda j: (0, ZS_U // LANE + j)),
            pl.BlockSpec(x0_block, x0_idx),
            pl.BlockSpec((None, None, 5, S5_ROW, S5_W), lambda j: (layer, j, 0, 0, 0)),
            pl.BlockSpec((None, None, 2, 4, S5_R, S5_TILE_STATE), lambda j: (layer, j, 0, 0, 0, 0)),
            pl.BlockSpec((None, None, 2, 2, S5_TILE_STATE), lambda j: (layer, j, 0, 0, 0)),
        ],
        out_specs=[
            pl.BlockSpec((n, LANE), lambda j: (0, j)),
            pl.BlockSpec((nseq, 2, 2, S5_TILE_STATE), lambda j: (0, 0, 0, j)),
        ],
        out_shape=[
            jax.ShapeDtypeStruct((n, S5_WIDTH), F32),
            jax.ShapeDtypeStruct((nseq, 2, 2, S5_NSTATE), F32),
        ],
        scratch_shapes=[pltpu.VMEM((n, LANE), F32)],
        compiler_params=_params(("parallel",)),
        name="s5_scan",
    )(zs, x0, wmat, tab8, tab1)


def _merge_kernel(x_ref, mod_ref, nw_ref, oa_ref, ob_ref, ys_ref, sg_ref, wglu_ref, bglu_ref, wmg_ref,
                  wa_ref, wb_ref, wc_ref, wout_ref, y_ref, wmg_s):
    @pl.when(pl.program_id(0) == 0)
    def _():
        _pack_transposed(wmg_ref.at[0], wmg_s, 0, 3 * D_MODEL // LANE, 0, None)

    x = x_ref[...]
    mod = mod_ref[0]
    h = _mod_rmsnorm(x, nw_ref[...], mod).astype(BF16)
    zg = _dot(jax.nn.gelu(ys_ref[...]), wglu_ref[...]) + bglu_ref[...]
    oc = (zg[:, :S5_WIDTH] * jax.nn.sigmoid(zg[:, S5_WIDTH:])
          * jax.nn.silu(sg_ref[...].astype(F32)))
    mixed = None
    for i, (o_i, w_ref) in enumerate(((oa_ref[...], wa_ref), (ob_ref[...], wb_ref), (oc, wc_ref))):
        gate = jax.nn.sigmoid(jnp.dot(h, wmg_s[:, i * D_MODEL:(i + 1) * D_MODEL], preferred_element_type=F32))
        term = gate * _dot(o_i, w_ref[...])
        mixed = term if mixed is None else mixed + term
    y_ref[...] = x + mod[:, 2 * D_MODEL:] * _dot(mixed, wout_ref[...])


def _merge(x2, mod, mod_idx, nw, oa, ob, ys, zs, layer, wglu, bglu, w_in_t, wa, wb, wc, wout):
    n = x2.shape[0]
    tm = ROW_TILE
    const = lambda i: (0, 0)
    rows = lambda w: pl.BlockSpec((tm, w), lambda i: (i, 0))
    return pl.pallas_call(
        _merge_kernel,
        grid=(n // tm,),
        in_specs=[
            rows(D_MODEL),
            pl.BlockSpec((1, 1, 3 * D_MODEL), lambda i: (mod_idx(i), 0, 0)),
            pl.BlockSpec((1, D_MODEL), const),
            rows(GLA_WIDTH), rows(MLA_WIDTH), rows(S5_WIDTH),
            pl.BlockSpec((tm, S5_WIDTH), lambda i: (i, ZS_GATE // S5_WIDTH)),
            pl.BlockSpec((None, S5_WIDTH, 2 * S5_WIDTH), lambda i: (layer, 0, 0)),
            pl.BlockSpec((None, 1, 2 * S5_WIDTH), lambda i: (layer, 0, 0)),
            pl.BlockSpec((pl.Element(1), pl.Element(3 * D_MODEL), pl.Element(D_MODEL)),
                         lambda i: (layer, MERGE_COL, 0), pipeline_mode=pl.Buffered(1)),
            pl.BlockSpec((GLA_WIDTH, D_MODEL), const),
            pl.BlockSpec((MLA_WIDTH, D_MODEL), const),
            pl.BlockSpec((S5_WIDTH, D_MODEL), const),
            pl.BlockSpec((D_MODEL, D_MODEL), const),
        ],
        out_specs=rows(D_MODEL),
        out_shape=jax.ShapeDtypeStruct((n, D_MODEL), F32),
        scratch_shapes=[pltpu.VMEM((D_MODEL, 3 * D_MODEL), BF16)],
        compiler_params=_params(("arbitrary",)),
        name="merge",
    )(x2, mod, nw, oa, ob, ys, zs, wglu, bglu, w_in_t, wa, wb, wc, wout)


def _mla_lane_of_dim():
    half = MLA_ROPE // 2
    first_gap = ROPE_SHIFT - half
    lane = np.zeros(MLA_QK, np.int32)
    for j in range(MLA_NOPE):
        lane[j] = half + j if j < first_gap else 2 * half + j
    for r in range(half):
        lane[MLA_NOPE + r] = r
        lane[MLA_NOPE + half + r] = ROPE_SHIFT + r
    return lane


MLA_LANE_OF_DIM = _mla_lane_of_dim()


def _place_heads(w, heads, lane_of_dim):
    width = len(lane_of_dim)
    src = np.zeros(heads * HEAD_PAD, np.int32)
    used = np.zeros(heads * HEAD_PAD, bool)
    for h in range(heads):
        src[h * HEAD_PAD + lane_of_dim] = h * width + np.arange(width)
        used[h * HEAD_PAD + lane_of_dim] = True
    return jnp.where(jnp.asarray(used), jnp.take(w, jnp.asarray(src), axis=-1), 0.0)


def _rope_tables(n_tok):
    rows = n_tok // GRID_W
    r = jnp.repeat(jnp.arange(rows, dtype=F32), GRID_W)
    col = jnp.tile(jnp.arange(GRID_W, dtype=F32), rows)
    n_freq = MLA_ROPE // 4
    inv = ROPE_THETA ** (-jnp.arange(n_freq, dtype=F32) / n_freq)
    ang = jnp.concatenate([r[:, None] * inv, col[:, None] * inv], axis=-1)
    cos, sin = jnp.cos(ang), jnp.sin(ang)
    ones = jnp.ones((n_tok, MLA_NOPE), F32)
    c = _place_heads(jnp.concatenate([ones, cos, cos], axis=1), 1, MLA_LANE_OF_DIM)
    s = _place_heads(jnp.concatenate([0.0 * ones, -sin, sin], axis=1), 1, MLA_LANE_OF_DIM)
    return jnp.stack([c, s])


def kernel(x_prompt, x_sample, c, c_ctx, cache_mla_ckv, cache_mla_krope, state_gla, state_s5,
           norm_w, w_ada, b_ada, w_in, gla_w_a2, gla_b_a, gla_o_norm,
           mla_q_norm, mla_w_uq, mla_kv_norm, mla_w_uk, mla_w_uv, mla_qh_norm, mla_kh_norm,
           s5_a_re, s5_a_im, s5_log_dt, s5_b_re, s5_b_im, s5_c_re, s5_c_im, s5_d, s5_w_glu, s5_b_glu,
           w_bo_gla, w_bo_mla, w_bo_s5, w_out):
    bsz, seq, _ = x_prompt.shape
    dbsz, dseq, _ = x_sample.shape
    ctx_row = 8 - 1
    assert dbsz <= ctx_row and (bsz * seq) % ROW_TILE == 0 and dseq % ROW_TILE == 0

    cond8 = jnp.zeros((8, D_MODEL), F32).at[0:dbsz].set(c).at[ctx_row].set(c_ctx)
    ada = _ada(cond8, w_ada, b_ada)

    vec = lambda a: a.reshape(DEPTH, 2, 1, S5_NSTATE)
    ldt = jnp.repeat(s5_log_dt[..., None], S5_STATE, axis=-1)
    rows_gp = lambda t: t.reshape(DEPTH, S5_TILES, LANE, S5_STATE)
    bt = lambda b: rows_gp(b.transpose(0, 1, 3, 2))
    wmat, tab8, tab1 = _s5_prep(vec(s5_a_re), vec(s5_a_im), vec(ldt), bt(s5_b_re), bt(s5_b_im),
                                rows_gp(s5_c_re), rows_gp(s5_c_im), s5_d.reshape(DEPTH, 1, S5_WIDTH))
    wglu = s5_w_glu.astype(BF16)
    bglu = s5_b_glu.reshape(DEPTH, 1, 2 * S5_WIDTH)

    wuq = _place_heads(mla_w_uq, MLA_HEADS, MLA_LANE_OF_DIM).astype(BF16)
    wuk = _place_heads(mla_w_uk, MLA_HEADS, MLA_LANE_OF_DIM[:MLA_NOPE]).astype(BF16)
    wuv = mla_w_uv.astype(BF16)
    qhn = _place_heads(mla_qh_norm, 1, MLA_LANE_OF_DIM)
    khn = _place_heads(mla_kh_norm, 1, MLA_LANE_OF_DIM)
    e_np = np.zeros((LANE, MLA_HEADS * HEAD_PAD), np.float32)
    for h in range(MLA_HEADS):
        for i in range(MLA_ROPE):
            e_np[i, h * HEAD_PAD + MLA_LANE_OF_DIM[MLA_NOPE + i]] = 1.0
    e_place = jnp.asarray(e_np, BF16)
    rope_tab = _rope_tables(dseq)
    ckr_pad = jnp.pad(cache_mla_krope, ((0, 0), (0, 0), (0, 0), (0, LANE - MLA_ROPE)))

    zrow = lambda n: jnp.zeros((DEPTH, n, GLA_QK), F32)
    waf = jnp.concatenate([gla_w_a2[:, 0], zrow(LANE - GLA_RANK)], axis=1).astype(BF16)
    wab = jnp.concatenate([zrow(GLA_RANK), gla_w_a2[:, 1], zrow(LANE - 2 * GLA_RANK)], axis=1).astype(BF16)
    sgla = state_gla.reshape(dbsz, DEPTH, 2, GLA_QK, GLA_DV)
    ss5 = state_s5.reshape(dbsz, DEPTH, 2, 2, S5_NSTATE)
    zero_s5 = jnp.zeros((bsz, 2, 2, S5_NSTATE), F32)

    hp = x_prompt.reshape(bsz * seq, D_MODEL)
    hs = x_sample.reshape(dbsz * dseq, D_MODEL)
    ckv_l, krope_l, gla_l, s5_l = [], [], [], []
    w_in_t = jnp.swapaxes(w_in, 1, 2)
    for l in range(DEPTH):
        mod = ada[l].reshape(8, 1, 3 * D_MODEL)
        nw = norm_w[l].reshape(1, D_MODEL)
        mla_w = (mla_q_norm[l].reshape(1, -1), wuq[l], mla_kv_norm[l].reshape(1, -1), wuk[l], wuv[l],
                 qhn[l].reshape(1, -1), khn[l].reshape(1, -1), e_place)
        wbo = (w_bo_gla[l].astype(BF16), w_bo_mla[l].astype(BF16), w_bo_s5[l].astype(BF16))
        wout = w_out[l].astype(BF16)
        onorm = gla_o_norm[l].reshape(1, GLA_DV)

        def layer(x2, nb, n, mod_idx, ctx):
            zg, zm, zs = _in_proj(x2, mod, mod_idx, nw, w_in_t, l)
            if ctx:
                gctx = (sgla, l)
                x0, x0_blk = ss5, (nb, None, 2, 2, S5_TILE_STATE)
                x0_idx = lambda j: (0, l, 0, 0, j)
                mctx, rt = (cache_mla_ckv, ckr_pad, l), rope_tab
            else:
                gctx = None
                x0, x0_blk = zero_s5, (nb, 2, 2, S5_TILE_STATE)
                x0_idx = lambda j: (0, 0, 0, j)
                mctx, rt = None, None
            oa, st_gla = _gla(zg, gctx, waf[l], wab[l], gla_b_a[l], onorm, nb, n)
            ob, ckv = _mla(zm, mctx, mla_w, rt, nb, n)
            y_ssm, st_s5 = _s5_scan(zs, x0, x0_blk, x0_idx, l, wmat, tab8, tab1, nb, n)
            y = _merge(x2, mod, mod_idx, nw, oa, ob, y_ssm, zs, l, wglu, bglu, w_in_t, *wbo, wout)
            return y, zm, ckv, st_gla, st_s5

        hp, zm_p, ckv_p, st_gla_p, st_s5_p = layer(hp, bsz, seq, lambda i: ctx_row, False)
        ckv_l.append(ckv_p.reshape(bsz, seq, MLA_KV_LORA))
        krope_l.append(zm_p[:, ZM_KR:ZM_KR + MLA_ROPE].astype(F32).reshape(bsz, seq, MLA_ROPE))
        gla_l.append(st_gla_p.reshape(bsz, 2, GLA_HEADS, GLA_DK, GLA_DV))
        s5_l.append(st_s5_p.reshape(bsz, 2, 2, S5_GROUPS, S5_STATE))
        blocks_per_seq = dseq // ROW_TILE
        hs = layer(hs, dbsz, dseq, lambda i: i // blocks_per_seq, True)[0]

    return (hp.reshape(bsz, seq, D_MODEL), hs.reshape(dbsz, dseq, D_MODEL),
            jnp.stack(ckv_l, axis=1), jnp.stack(krope_l, axis=1),
            jnp.stack(gla_l, axis=1), jnp.stack(s5_l, axis=1))
```

```python
import functools

import jax
import jax.numpy as jnp
import numpy as np
from jax import lax
from jax.experimental import pallas as pl
from jax.experimental.pallas import tpu as pltpu

F32 = jnp.float32
BF16 = jnp.bfloat16

EPS = 1e-6
D_MODEL = 1024
DEPTH = 2
GRID_W = 64
ROPE_THETA = 10000.0
GLA_HEADS = 4
GLA_DK = 64
GLA_DV = 128
GLA_RANK = 16
GLA_GATE_NORM = 16.0
GLA_QK = GLA_HEADS * GLA_DK
GLA_WIDTH = GLA_HEADS * GLA_DV
MLA_HEADS = 4
MLA_Q_LORA = 384
MLA_KV_LORA = 256
MLA_NOPE = 64
MLA_ROPE = 32
MLA_QK = MLA_NOPE + MLA_ROPE
MLA_DV = 128
MLA_WIDTH = MLA_HEADS * MLA_DV
S5_WIDTH = 512
S5_GROUP = 16
S5_GROUPS = 32
S5_STATE = 64
S5_NSTATE = S5_GROUPS * S5_STATE

LANE = 128
HEAD_PAD = LANE
ROPE_SHIFT = LANE // 2
CHUNK = 64
GLA_STEP = 256
GLA_SEQS_PER_STEP = 2
GLA_ROWS_PER_STEP = 1024
S5_TILES = S5_WIDTH // LANE
S5_TILE_STATE = S5_NSTATE // S5_TILES
ROW_TILE = 512
Q_TILE = 512
PROJ_TILE = 256
VMEM_LIMIT = 56 * 1024 * 1024

MERGE_COL = 3776
ZG_Q, ZG_K, ZG_V, ZG_A, ZG_GATE, ZG_W = 0, 256, 512, 1024, 1152, 1664
ZM_Q, ZM_KV, ZM_KR, ZM_GATE, ZM_W = 0, 384, 640, 768, 1280
ZS_U, ZS_GATE, ZS_W = 0, 512, 1024
ZG_BASE, ZM_BASE, ZS_BASE, PACK_W = 0, ZG_W, ZG_W + ZM_W, ZG_W + ZM_W + ZS_W
IN_PIECES = (
    (0, 8, ZG_BASE + ZG_Q, None),
    (1024, 1, ZG_BASE + ZG_A, 2 * GLA_RANK),
    (1056, 4, ZG_BASE + ZG_GATE, None),
    (1568, 5, ZM_BASE + ZM_Q, None),
    (2208, 1, ZM_BASE + ZM_KR, MLA_ROPE),
    (2240, 4, ZM_BASE + ZM_GATE, None),
    (2752, 8, ZS_BASE + ZS_U, None),
)


def _dot(a, b):
    return jnp.dot(a.astype(BF16), b.astype(BF16), preferred_element_type=F32)


def _dot_nt(a, b):
    return lax.dot_general(a.astype(BF16), b.astype(BF16), (((1,), (1,)), ((), ())),
                           preferred_element_type=F32)


def _split_bf16(x, parts):
    out = []
    r = x
    for _ in range(parts):
        p = r.astype(BF16)
        out.append(p)
        r = r - p.astype(F32)
    return out


def _params(sem):
    return pltpu.CompilerParams(dimension_semantics=sem, vmem_limit_bytes=VMEM_LIMIT)


def _ada_kernel(c_ref, w_ref, b_ref, o_ref):
    s = jax.nn.silu(c_ref[...])
    o_ref[...] = _dot(s, w_ref[...]) + b_ref[...]


def _ada(cond8, w_ada, b_ada):
    tn = 1024
    return pl.pallas_call(
        _ada_kernel,
        grid=(DEPTH, 3 * D_MODEL // tn),
        in_specs=[
            pl.BlockSpec((8, D_MODEL), lambda l, n: (0, 0)),
            pl.BlockSpec((None, D_MODEL, tn), lambda l, n: (l, 0, n)),
            pl.BlockSpec((None, 1, tn), lambda l, n: (l, 0, n)),
        ],
        out_specs=pl.BlockSpec((None, 8, tn), lambda l, n: (l, 0, n)),
        out_shape=jax.ShapeDtypeStruct((DEPTH, 8, 3 * D_MODEL), F32),
        compiler_params=_params(("parallel", "parallel")),
        name="ada",
    )(cond8, w_ada, b_ada.reshape(DEPTH, 1, 3 * D_MODEL))


def _mod_rmsnorm(x, nw, mod):
    ms = jnp.mean(x * x, axis=-1, keepdims=True)
    y = x * lax.rsqrt(ms + EPS) * nw
    return y * (1.0 + mod[:, D_MODEL:2 * D_MODEL]) + mod[:, 0:D_MODEL]


def _pack_transposed(w_ref, wb_s, src, tiles, dst, keep):
    lane = lax.broadcasted_iota(jnp.int32, (D_MODEL, LANE), 1)
    for t in range(tiles):
        blk = w_ref[src + t * LANE:src + (t + 1) * LANE, :].T
        if keep is not None:
            blk = jnp.where(lane < keep, blk, 0.0)
        wb_s[:, dst + t * LANE:dst + (t + 1) * LANE] = blk.astype(BF16)


def _in_proj_kernel(xp_ref, xs_ref, mod_ref, nw_ref, w_ref, zg_ref, zm_ref, zs_ref, wb_s, *, p_blocks):
    i = pl.program_id(0)

    @pl.when(i == 0)
    def _():
        for src, tiles, dst, keep in IN_PIECES:
            _pack_transposed(w_ref, wb_s, src, tiles, dst, keep)

    x = jnp.where(i < p_blocks, xp_ref[...], xs_ref[...])
    h = _mod_rmsnorm(x, nw_ref[...], mod_ref[0]).astype(BF16)
    z = jnp.dot(h, wb_s[...], preferred_element_type=F32)
    zg_ref[...] = z[:, ZG_BASE:ZG_BASE + ZG_W].astype(BF16)
    zm_ref[...] = z[:, ZM_BASE:ZM_BASE + ZM_W].astype(BF16)
    zs_ref[...] = z[:, ZS_BASE:ZS_BASE + ZS_W].astype(BF16)


def _two_group_rows(width, p_blocks):
    tm = ROW_TILE
    return (pl.BlockSpec((tm, width), lambda i: (jnp.minimum(i, p_blocks - 1), 0)),
            pl.BlockSpec((tm, width), lambda i: (jnp.maximum(i - p_blocks, 0), 0)))


def _in_proj(xp, xs, mod, mod_idx, nw, w_in_t, layer):
    tm = ROW_TILE
    p_blocks = xp.shape[0] // tm
    n = xp.shape[0] + xs.shape[0]
    const = lambda i: (0, 0)
    return pl.pallas_call(
        functools.partial(_in_proj_kernel, p_blocks=p_blocks),
        grid=(n // tm,),
        in_specs=[
            *_two_group_rows(D_MODEL, p_blocks),
            pl.BlockSpec((1, 1, 3 * D_MODEL), lambda i: (mod_idx(i), 0, 0)),
            pl.BlockSpec((1, D_MODEL), const),
            pl.BlockSpec((None, MERGE_COL, D_MODEL), lambda i: (layer, 0, 0), pipeline_mode=pl.Buffered(1)),
        ],
        out_specs=[
            pl.BlockSpec((tm, ZG_W), lambda i: (i, 0)),
            pl.BlockSpec((tm, ZM_W), lambda i: (i, 0)),
            pl.BlockSpec((tm, ZS_W), lambda i: (i, 0)),
        ],
        out_shape=[
            jax.ShapeDtypeStruct((n, ZG_W), BF16),
            jax.ShapeDtypeStruct((n, ZM_W), BF16),
            jax.ShapeDtypeStruct((n, ZS_W), BF16),
        ],
        scratch_shapes=[pltpu.VMEM((D_MODEL, PACK_W), BF16)],
        compiler_params=_params(("arbitrary",)),
        name="in_proj",
    )(xp, xs, mod, nw, w_in_t)


def _gla_kernel(*refs, nsteps, seq, nseq, has_ctx):
    it = iter(refs)
    zg_ref = next(it)
    s0_ref = next(it) if has_ctx else None
    waf_ref, wab_ref, ba_ref, onorm_ref, o_ref, sfin_ref, la_s, o_s, st_s = (next(it) for _ in range(9))
    chains = [(g, d) for g in range(nseq) for d in (0, 1)]
    inv_norm = 1.0 / GLA_GATE_NORM
    zero_blk = jnp.zeros((GLA_DK, GLA_DV), F32)
    for ch, (g, d) in enumerate(chains):
        if d == 0:
            a_blk = zg_ref[g * seq:(g + 1) * seq, ZG_A:ZG_A + LANE]
        wa_ref = waf_ref if d == 0 else wab_ref
        la_s[ch] = jax.nn.log_sigmoid(_dot(a_blk, wa_ref[...]) + ba_ref[d:d + 1, :]) * inv_norm
        if has_ctx:
            s0 = s0_ref[g, d]
            rows_bd = []
            for h in range(GLA_HEADS):
                sh = s0[h * GLA_DK:(h + 1) * GLA_DK, :]
                rows_bd.append(jnp.concatenate([sh if h2 == h else zero_blk for h2 in range(GLA_HEADS)], axis=1))
            st_s[ch] = jnp.concatenate(rows_bd, axis=0).T
        else:
            st_s[ch] = jnp.zeros((GLA_WIDTH, GLA_QK), F32)

    def iota(shape, axis, shift):
        return lax.shift_right_logical(lax.broadcasted_iota(jnp.int32, shape, axis), shift)

    log_chunk, log_dv = CHUNK.bit_length() - 1, GLA_DV.bit_length() - 1
    row = lax.broadcasted_iota(jnp.int32, (GLA_STEP, GLA_STEP), 0)
    col = lax.broadcasted_iota(jnp.int32, (GLA_STEP, GLA_STEP), 1)
    same_chunk = iota((GLA_STEP, GLA_STEP), 0, log_chunk) == iota((GLA_STEP, GLA_STEP), 1, log_chunk)
    masks = (same_chunk & (row >= col), same_chunk & (row <= col))
    lane_head = iota((GLA_STEP, GLA_QK), 1, log_chunk)
    row_chunk = iota((GLA_STEP, GLA_QK), 0, log_chunk)
    state_blk = iota((GLA_WIDTH, GLA_QK), 0, log_dv) == iota((GLA_WIDTH, GLA_QK), 1, log_chunk)
    qscale = GLA_DK ** -0.5
    nch = GLA_STEP // CHUNK
    n_chain = len(chains)

    def step(i, carry):
        rows, cum = [], []
        for ch, (g, d) in enumerate(chains):
            r0 = pl.multiple_of((i if d == 0 else nsteps - 1 - i) * GLA_STEP, GLA_STEP)
            rows.append((pl.ds(g * seq + r0, GLA_STEP), pl.ds(r0, GLA_STEP)))
            a_hi, a_lo = _split_bf16(la_s[ch, rows[ch][1], :], 2)
            tri = masks[d].astype(BF16)
            cum.append(jnp.dot(tri, a_hi, preferred_element_type=F32)
                       + jnp.dot(tri, a_lo, preferred_element_type=F32))
        blast, v, v_t, qd, kd, kr = [], [], [], [], [], []
        for ch, (g, d) in enumerate(chains):
            edge = CHUNK - 1 if d == 0 else 0
            blast.append([cum[ch][c * CHUNK + edge:c * CHUNK + edge + 1, :] for c in range(nch)])
            bl = jnp.concatenate([jnp.broadcast_to(b, (CHUNK, GLA_QK)) for b in blast[ch]], axis=0)
            zrows = rows[ch][0]
            q = zg_ref[zrows, ZG_Q:ZG_Q + GLA_QK].astype(F32) * qscale
            k = zg_ref[zrows, ZG_K:ZG_K + GLA_QK].astype(F32)
            v.append(zg_ref[zrows, ZG_V:ZG_V + GLA_WIDTH])
            v_t.append(v[ch].astype(F32).T.astype(BF16))
            qd.append(q * jnp.exp(cum[ch]))
            kd.append((k * jnp.exp(-cum[ch])).astype(BF16))
            kr.append(k * jnp.exp(bl - cum[ch]))
        outs = [[] for _ in chains]
        for h in range(GLA_HEADS):
            for ch, (g, d) in enumerate(chains):
                qh = jnp.where(lane_head == h, qd[ch], 0.0)
                att = jnp.where(masks[d], _dot_nt(qh, kd[ch]), 0.0)
                outs[ch].append(_dot(att, v[ch][:, h * GLA_DV:(h + 1) * GLA_DV]))
        s = [st_s[ch] for ch in range(n_chain)]
        inter = [[None] * nch for _ in chains]
        for j in range(nch):
            for ch, (g, d) in enumerate(chains):
                c = j if d == 0 else nch - 1 - j
                inter[ch][c] = _dot_nt(qd[ch][c * CHUNK:(c + 1) * CHUNK, :], s[ch])
                kv_t = jnp.where(state_blk, _dot(v_t[ch], jnp.where(row_chunk == c, kr[ch], 0.0)), 0.0)
                s[ch] = s[ch] * jnp.exp(blast[ch][c]) + kv_t
        for ch in range(n_chain):
            st_s[ch] = s[ch]
            o_s[ch, rows[ch][1], :] = jnp.concatenate(outs[ch], axis=1) + jnp.concatenate(inter[ch], axis=0)
        return carry

    lax.fori_loop(0, nsteps, step, 0)
    onorm = onorm_ref[...]
    for ch, (g, d) in enumerate(chains):
        s_fin = st_s[ch].T
        for h in range(GLA_HEADS):
            sfin_ref[g, d, h * GLA_DK:(h + 1) * GLA_DK, :] = (
                s_fin[h * GLA_DK:(h + 1) * GLA_DK, h * GLA_DV:(h + 1) * GLA_DV])
    for g in range(nseq):
        srows = slice(g * seq, (g + 1) * seq)
        o = o_s[2 * g] + o_s[2 * g + 1]
        gate = zg_ref[srows, ZG_GATE:ZG_GATE + GLA_WIDTH].astype(F32)
        for h in range(GLA_HEADS):
            vs = slice(h * GLA_DV, (h + 1) * GLA_DV)
            oh = o[:, vs]
            ms = jnp.mean(oh * oh, axis=-1, keepdims=True)
            o_ref[srows, vs] = oh * lax.rsqrt(ms + EPS) * onorm * jax.nn.silu(gate[:, vs])


def _gla(zg, row0, ctx, waf, wab, ba, onorm, bsz, seq):
    const = lambda b: (0, 0)
    nseq = max(GLA_SEQS_PER_STEP, GLA_ROWS_PER_STEP // seq)
    blk0 = row0 // (nseq * seq)
    in_specs = [pl.BlockSpec((nseq * seq, ZG_W), lambda b: (b + blk0, 0))]
    args = [zg]
    if ctx is not None:
        s0, layer = ctx
        in_specs.append(pl.BlockSpec((nseq, None, 2, GLA_QK, GLA_DV), lambda b: (b, layer, 0, 0, 0)))
        args.append(s0)
    in_specs += [
        pl.BlockSpec((LANE, GLA_QK), const),
        pl.BlockSpec((LANE, GLA_QK), const),
        pl.BlockSpec((2, GLA_QK), const),
        pl.BlockSpec((1, GLA_DV), const),
    ]
    return pl.pallas_call(
        functools.partial(_gla_kernel, nsteps=seq // GLA_STEP, seq=seq, nseq=nseq, has_ctx=ctx is not None),
        grid=(bsz // nseq,),
        in_specs=in_specs,
        out_specs=[
            pl.BlockSpec((nseq * seq, GLA_WIDTH), lambda b: (b, 0)),
            pl.BlockSpec((nseq, 2, GLA_QK, GLA_DV), lambda b: (b, 0, 0, 0)),
        ],
        out_shape=[
            jax.ShapeDtypeStruct((bsz * seq, GLA_WIDTH), F32),
            jax.ShapeDtypeStruct((bsz, 2, GLA_QK, GLA_DV), F32),
        ],
        scratch_shapes=[
            pltpu.VMEM((2 * nseq, seq, GLA_QK), F32),
            pltpu.VMEM((2 * nseq, seq, GLA_WIDTH), F32),
            pltpu.VMEM((2 * nseq, GLA_WIDTH, GLA_QK), F32),
        ],
        compiler_params=_params(("parallel",)),
        name="gla",
    )(*args, waf, wab, ba, onorm)


def _rms(x, w):
    ms = jnp.mean(x * x, axis=-1, keepdims=True)
    return x * lax.rsqrt(ms + EPS) * w


def _head_sums_mxu(x):
    width = x.shape[-1]
    shift = HEAD_PAD.bit_length() - 1
    gi = lax.shift_right_logical(lax.broadcasted_iota(jnp.int32, (width, width), 0), shift)
    gj = lax.shift_right_logical(lax.broadcasted_iota(jnp.int32, (width, width), 1), shift)
    return _dot(x * x, jnp.where(gi == gj, 1.0, 0.0))


def _head_norm(x, w, rope, on_mxu):
    sums = _head_sums_mxu(x) if on_mxu else None
    outs = []
    for h in range(MLA_HEADS):
        hs = slice(h * HEAD_PAD, (h + 1) * HEAD_PAD)
        xh = x[:, hs]
        ss = sums[:, hs] if on_mxu else jnp.sum(xh * xh, axis=-1, keepdims=True)
        yh = xh * lax.rsqrt(ss * (1.0 / MLA_QK) + EPS) * w
        if rope is not None:
            c, s = rope
            yh = yh * c + pltpu.roll(yh, ROPE_SHIFT, 1) * s
        outs.append(yh)
    return outs


def _place_rope_key(kr, e):
    return sum(jnp.dot(p, e, preferred_element_type=F32) for p in _split_bf16(kr, 3))


def _mla_kernel(*refs, seq, n_ctx, use_rope):
    it = iter(refs)
    zm_ref = next(it)
    if n_ctx:
        cckv_ref, ckr_ref = next(it), next(it)
    qn_ref, wuq_ref, kvn_ref, wuk_ref, wuv_ref, qhn_ref, khn_ref, e_ref = (next(it) for _ in range(8))
    rope_ref = next(it) if use_rope else None
    o_ref, ckv_ref = next(it), next(it)
    q_s, k_s, v_s = next(it), next(it), next(it)

    qscale = MLA_QK ** -0.5
    heads = [slice(h * HEAD_PAD, (h + 1) * HEAD_PAD) for h in range(MLA_HEADS)]

    def keys_values(ckv, k_rope_placed, rope, k_rows):
        kh = _head_norm(_dot(ckv, wuk_ref[...]) + k_rope_placed, khn_ref[...], rope, False)
        for h, hs in enumerate(heads):
            k_s[k_rows, hs] = kh[h].astype(BF16)
        v_s[k_rows, :] = _dot(ckv, wuv_ref[...]).astype(BF16)

    def latent_tile(i, carry):
        r0 = pl.multiple_of(i * PROJ_TILE, PROJ_TILE)
        rows = pl.ds(r0, PROJ_TILE)
        rope = (rope_ref[0, rows, :], rope_ref[1, rows, :]) if use_rope else None
        ckv = _rms(zm_ref[rows, ZM_KV:ZM_KV + MLA_KV_LORA].astype(F32), kvn_ref[...])
        ckv_ref[rows, :] = ckv
        k_pe = jnp.dot(zm_ref[rows, ZM_KR:ZM_KR + LANE], e_ref[...], preferred_element_type=F32)
        keys_values(ckv, k_pe, rope, pl.ds(n_ctx + r0, PROJ_TILE))
        cq = _rms(zm_ref[rows, ZM_Q:ZM_Q + MLA_Q_LORA].astype(F32), qn_ref[...])
        qh = _head_norm(_dot(cq, wuq_ref[...]), qhn_ref[...], rope, True)
        for h, hs in enumerate(heads):
            q_s[rows, hs] = (qh[h] * qscale).astype(BF16)
        return carry

    lax.fori_loop(0, seq // PROJ_TILE, latent_tile, 0)

    def context_tile(i, carry):
        rows = pl.ds(pl.multiple_of(i * PROJ_TILE, PROJ_TILE), PROJ_TILE)
        keys_values(cckv_ref[rows, :], _place_rope_key(ckr_ref[rows, :], e_ref[...]), None, rows)
        return carry

    if n_ctx:
        lax.fori_loop(0, n_ctx // PROJ_TILE, context_tile, 0)

    q_tile = min(seq, Q_TILE)

    def q_block(i, carry):
        rows = pl.ds(pl.multiple_of(i * q_tile, q_tile), q_tile)
        gate = zm_ref[rows, ZM_GATE:ZM_GATE + MLA_WIDTH].astype(F32)
        s = [lax.dot_general(q_s[rows, hs], k_s[:, hs], (((1,), (1,)), ((), ())),
                             preferred_element_type=F32) for hs in heads]
        p, l = [], []
        for h in range(MLA_HEADS):
            e = jnp.exp(s[h] - jnp.max(s[h], axis=-1, keepdims=True))
            l.append(jnp.sum(e, axis=-1, keepdims=True))
            p.append(e.astype(BF16))
        for h, hs in enumerate(heads):
            o = jnp.dot(p[h], v_s[:, hs], preferred_element_type=F32) / l[h]
            o_ref[rows, hs] = o * jax.nn.silu(gate[:, hs])
        return carry

    lax.fori_loop(0, seq // q_tile, q_block, 0)


def _mla(zm, row0, ctx, w, rope_tab, bsz, seq):
    const2 = lambda b: (0, 0)
    n_ctx = 0 if ctx is None else ctx[0].shape[-2]
    blk0 = row0 // seq
    in_specs = [pl.BlockSpec((seq, ZM_W), lambda b: (b + blk0, 0))]
    args = [zm]
    if ctx is not None:
        cckv, ckr, layer = ctx
        in_specs += [
            pl.BlockSpec((None, None, n_ctx, MLA_KV_LORA), lambda b: (b, layer, 0, 0)),
            pl.BlockSpec((None, None, n_ctx, LANE), lambda b: (b, layer, 0, 0)),
        ]
        args += [cckv, ckr]
    in_specs += [
        pl.BlockSpec((1, MLA_Q_LORA), const2),
        pl.BlockSpec((MLA_Q_LORA, MLA_HEADS * HEAD_PAD), const2),
        pl.BlockSpec((1, MLA_KV_LORA), const2),
        pl.BlockSpec((MLA_KV_LORA, MLA_HEADS * HEAD_PAD), const2),
        pl.BlockSpec((MLA_KV_LORA, MLA_WIDTH), const2),
        pl.BlockSpec((1, HEAD_PAD), const2),
        pl.BlockSpec((1, HEAD_PAD), const2),
        pl.BlockSpec((LANE, MLA_HEADS * HEAD_PAD), const2),
    ]
    args += list(w)
    if rope_tab is not None:
        in_specs.append(pl.BlockSpec((2, seq, HEAD_PAD), lambda b: (0, 0, 0)))
        args.append(rope_tab)
    return pl.pallas_call(
        functools.partial(_mla_kernel, seq=seq, n_ctx=n_ctx, use_rope=rope_tab is not None),
        grid=(bsz,),
        in_specs=in_specs,
        out_specs=[
            pl.BlockSpec((seq, MLA_WIDTH), lambda b: (b, 0)),
            pl.BlockSpec((seq, MLA_KV_LORA), lambda b: (b, 0)),
        ],
        out_shape=[
            jax.ShapeDtypeStruct((bsz * seq, MLA_WIDTH), F32),
            jax.ShapeDtypeStruct((bsz * seq, MLA_KV_LORA), F32),
        ],
        scratch_shapes=[
            pltpu.VMEM((seq, MLA_HEADS * HEAD_PAD), BF16),
            pltpu.VMEM((n_ctx + seq, MLA_HEADS * HEAD_PAD), BF16),
            pltpu.VMEM((n_ctx + seq, MLA_WIDTH), BF16),
        ],
        compiler_params=_params(("parallel",)),
        name="mla",
    )(*args)


S5_T = 8
S5_R = CHUNK // S5_T
S5_ROW = S5_T * LANE
S5_W = 2 * S5_TILE_STATE
W_M, W_SF, W_SB, W_CF, W_CB = range(5)


def _cmul(ar, ai, br, bi):
    return ar * br - ai * bi, ar * bi + ai * br


def _s5_prep_kernel(are_ref, aim_ref, ldt_ref, bre_ref, bim_ref, cre_ref, cim_ref, d_ref,
                    w_ref, tab8_ref, tab1_ref):
    gr = lax.shift_right_logical(lax.broadcasted_iota(jnp.int32, (LANE, S5_TILE_STATE), 0),
                                 S5_GROUP.bit_length() - 1)
    gc = lax.shift_right_logical(lax.broadcasted_iota(jnp.int32, (LANE, S5_TILE_STATE), 1),
                                 S5_STATE.bit_length() - 1)
    spread = lambda ref: jnp.where(gr == gc, jnp.concatenate([ref[...]] * (LANE // S5_GROUP), axis=1), 0.0)
    b_re, b_im = spread(bre_ref), spread(bim_ref)
    c_re, c_im = spread(cre_ref), spread(cim_ref)
    c_cat = jnp.concatenate([c_re, c_im], axis=1).astype(BF16)
    kern = []
    for d in (0, 1):
        a_re, a_im = are_ref[d], aim_ref[d]
        dt = jnp.exp(ldt_ref[d])
        lam = a_re * dt
        th = a_im * dt
        mag = jnp.exp(lam)
        ab_re = mag * jnp.cos(th)
        ab_im = mag * jnp.sin(th)
        den = a_re * a_re + a_im * a_im
        n_re = ab_re - 1.0
        cf_re = (n_re * a_re + ab_im * a_im) / den
        cf_im = (ab_im * a_re - n_re * a_im) / den
        bp_re, bp_im = _cmul(b_re, b_im, cf_re, cf_im)
        k = lax.broadcasted_iota(jnp.int32, (2 * S5_T, S5_TILE_STATE), 0).astype(F32)
        pmag = jnp.exp(k * lam)
        pw_re = pmag * jnp.cos(k * th)
        pw_im = pmag * jnp.sin(k * th)
        taps = []
        for p in range(S5_T + 1):
            ar, ai = pw_re[p:p + 1, :], pw_im[p:p + 1, :]
            l_re, l_im = _cmul(bp_re, bp_im, ar, ai)
            v_re, v_im = _cmul(c_re, c_im, ar, ai)
            t_in = S5_T - 1 - p if d == 0 else p
            if 0 <= t_in < S5_T:
                w_ref[W_SF + d, t_in * LANE:(t_in + 1) * LANE, :] = (
                    jnp.concatenate([l_re, l_im], axis=1).astype(BF16))
            t_out = p - 1 if d == 0 else S5_T - p
            if 0 <= t_out < S5_T:
                w_ref[W_CF + d, t_out * LANE:(t_out + 1) * LANE, :] = (
                    jnp.concatenate([v_re, -v_im], axis=1).astype(BF16))
            if p < S5_T:
                taps.append(_dot_nt(jnp.concatenate([l_re, -l_im], axis=1), c_cat))
        kern.append(taps)
        r = lax.broadcasted_iota(jnp.int32, (S5_R, S5_TILE_STATE), 0).astype(F32) * float(S5_T)
        r1 = r + float(S5_T)
        pm = jnp.exp(r * lam)
        qm = jnp.exp(-(r1 * lam))
        tab8_ref[d, 0] = pm * jnp.cos(r * th)
        tab8_ref[d, 1] = pm * jnp.sin(r * th)
        tab8_ref[d, 2] = qm * jnp.cos(r1 * th)
        tab8_ref[d, 3] = -(qm * jnp.sin(r1 * th))
        mc = jnp.exp(float(CHUNK) * lam)
        tab1_ref[d, 0:1, :] = mc * jnp.cos(float(CHUNK) * th)
        tab1_ref[d, 1:2, :] = mc * jnp.sin(float(CHUNK) * th)
    row = lax.broadcasted_iota(jnp.int32, (LANE, LANE), 0)
    col = lax.broadcasted_iota(jnp.int32, (LANE, LANE), 1)
    skip = jnp.where(row == col, d_ref[...], 0.0)
    for t in range(S5_T):
        for t2 in range(S5_T):
            if t < t2:
                blk = kern[0][t2 - t]
            elif t > t2:
                blk = kern[1][t - t2]
            else:
                blk = kern[0][0] + kern[1][0] + skip
            w_ref[W_M, t * LANE:(t + 1) * LANE, t2 * LANE:(t2 + 1) * LANE] = blk.astype(BF16)


def _s5_prep(a_re, a_im, ldt, b_re, b_im, c_re, c_im, dsk):
    vec = pl.BlockSpec((None, 2, 1, S5_TILE_STATE), lambda l, j: (l, 0, 0, j))
    blk = pl.BlockSpec((None, None, LANE, S5_STATE), lambda l, j: (l, j, 0, 0))
    return pl.pallas_call(
        _s5_prep_kernel,
        grid=(DEPTH, S5_TILES),
        in_specs=[vec, vec, vec, blk, blk, blk, blk,
                  pl.BlockSpec((None, 1, LANE), lambda l, j: (l, 0, j))],
        out_specs=[
            pl.BlockSpec((None, None, 5, S5_ROW, S5_W), lambda l, j: (l, j, 0, 0, 0)),
            pl.BlockSpec((None, None, 2, 4, S5_R, S5_TILE_STATE), lambda l, j: (l, j, 0, 0, 0, 0)),
            pl.BlockSpec((None, None, 2, 2, S5_TILE_STATE), lambda l, j: (l, j, 0, 0, 0)),
        ],
        out_shape=[
            jax.ShapeDtypeStruct((DEPTH, S5_TILES, 5, S5_ROW, S5_W), BF16),
            jax.ShapeDtypeStruct((DEPTH, S5_TILES, 2, 4, S5_R, S5_TILE_STATE), F32),
            jax.ShapeDtypeStruct((DEPTH, S5_TILES, 2, 2, S5_TILE_STATE), F32),
        ],
        compiler_params=_params(("parallel", "parallel")),
        name="s5_prep",
    )(a_re, a_im, ldt, b_re, b_im, c_re, c_im, dsk)


def _s5_scan_kernel(u_ref, x0_ref, w_ref, tab8_ref, tab1_ref, y_ref, fs_ref, u_s, *, nseq, nb):
    groups = nseq * nb
    nrow = groups * S5_R
    ts = S5_TILE_STATE
    u_s[...] = u_ref[...].astype(F32)
    u8 = jnp.concatenate([u_s[pl.ds(t, nrow, stride=S5_T), :] for t in range(S5_T)],
                         axis=1).astype(BF16)
    ef = jnp.dot(u8, w_ref[W_SF], preferred_element_type=F32).reshape(groups, S5_R, S5_W)
    eb = jnp.dot(u8, w_ref[W_SB], preferred_element_type=F32).reshape(groups, S5_R, S5_W)
    rowi = lax.broadcasted_iota(jnp.int32, (groups, S5_R, ts), 1)

    def prefix(x):
        for s in (1, 2, 4):
            x = x + jnp.where(rowi >= s, pltpu.roll(x, s, 1), 0.0)
        return x

    def suffix(x):
        for s in (1, 2, 4):
            x = x + jnp.where(rowi < S5_R - s, pltpu.roll(x, S5_R - s, 1), 0.0)
        return x

    p_re, p_im, q_re, q_im = (tab8_ref[0, i] for i in range(4))
    a_re, a_im = tab1_ref[0, 0:1, :], tab1_ref[0, 1:2, :]
    w_re, w_im = _cmul(q_re, q_im, ef[:, :, :ts], ef[:, :, ts:])
    cs_re, cs_im = prefix(w_re), prefix(w_im)
    st_re, st_im = [], []
    for s in range(nseq):
        x_re, x_im = x0_ref[s, 0, 0:1, :], x0_ref[s, 0, 1:2, :]
        for b in range(nb):
            g = s * nb + b
            st_re.append(x_re)
            st_im.append(x_im)
            x_re, x_im = _cmul(a_re, a_im, x_re + cs_re[g, S5_R - 1:S5_R, :],
                               x_im + cs_im[g, S5_R - 1:S5_R, :])
        fs_ref[s, 0, 0:1, :] = x_re
        fs_ref[s, 0, 1:2, :] = x_im
    xin_re, xin_im = _cmul(p_re, p_im, cs_re - w_re + jnp.stack(st_re), cs_im - w_im + jnp.stack(st_im))
    xin = jnp.concatenate([xin_re, xin_im], axis=2).reshape(nrow, S5_W)

    p_re, p_im, q_re, q_im = (tab8_ref[1, i] for i in range(4))
    a_re, a_im = tab1_ref[1, 0:1, :], tab1_ref[1, 1:2, :]
    w_re, w_im = _cmul(p_re, p_im, eb[:, :, :ts], eb[:, :, ts:])
    sf_re, sf_im = suffix(w_re), suffix(w_im)
    z_re, z_im = [None] * groups, [None] * groups
    for s in range(nseq):
        x_re, x_im = x0_ref[s, 1, 0:1, :], x0_ref[s, 1, 1:2, :]
        for b in reversed(range(nb)):
            g = s * nb + b
            z_re[g], z_im[g] = _cmul(a_re, a_im, x_re, x_im)
            x_re = sf_re[g, 0:1, :] + z_re[g]
            x_im = sf_im[g, 0:1, :] + z_im[g]
        fs_ref[s, 1, 0:1, :] = x_re
        fs_ref[s, 1, 1:2, :] = x_im
    xnx_re, xnx_im = _cmul(q_re, q_im, sf_re - w_re + jnp.stack(z_re), sf_im - w_im + jnp.stack(z_im))
    xnx = jnp.concatenate([xnx_re, xnx_im], axis=2).reshape(nrow, S5_W)

    y8 = (jnp.dot(u8, w_ref[W_M], preferred_element_type=F32)
          + _dot_nt(xin, w_ref[W_CF]) + _dot_nt(xnx, w_ref[W_CB]))
    for t in range(S5_T):
        y_ref[pl.ds(t, nrow, stride=S5_T), :] = y8[:, t * LANE:(t + 1) * LANE]


def _s5_scan(zs, row0, x0, x0_block, x0_idx, layer, wmat, tab8, tab1, nseq, seq):
    n = nseq * seq
    rblk = row0 // n
    return pl.pallas_call(
        functools.partial(_s5_scan_kernel, nseq=nseq, nb=seq // CHUNK),
        grid=(S5_TILES,),
        in_specs=[
            pl.BlockSpec((n, LANE), lambda j: (rblk, ZS_U // LANE + j)),
            pl.BlockSpec(x0_block, x0_idx),
            pl.BlockSpec((None, None, 5, S5_ROW, S5_W), lambda j: (layer, j, 0, 0, 0)),
            pl.BlockSpec((None, None, 2, 4, S5_R, S5_TILE_STATE), lambda j: (layer, j, 0, 0, 0, 0)),
            pl.BlockSpec((None, None, 2, 2, S5_TILE_STATE), lambda j: (layer, j, 0, 0, 0)),
        ],
        out_specs=[
            pl.BlockSpec((n, LANE), lambda j: (0, j)),
            pl.BlockSpec((nseq, 2, 2, S5_TILE_STATE), lambda j: (0, 0, 0, j)),
        ],
        out_shape=[
            jax.ShapeDtypeStruct((n, S5_WIDTH), F32),
            jax.ShapeDtypeStruct((nseq, 2, 2, S5_NSTATE), F32),
        ],
        scratch_shapes=[pltpu.VMEM((n, LANE), F32)],
        compiler_params=_params(("parallel",)),
        name="s5_scan",
    )(zs, x0, wmat, tab8, tab1)


def _merge_kernel(x_ref, mod_ref, nw_ref, oa_ref, ob_ref, ys_ref, sg_ref, wglu_ref, bglu_ref, wmg_ref,
                  wa_ref, wb_ref, wc_ref, wout_ref, y_ref, wmg_s):
    @pl.when(pl.program_id(0) == 0)
    def _():
        _pack_transposed(wmg_ref.at[0], wmg_s, 0, 3 * D_MODEL // LANE, 0, None)

    x = x_ref[...]
    mod = mod_ref[0]
    h = _mod_rmsnorm(x, nw_ref[...], mod).astype(BF16)
    zg = _dot(jax.nn.gelu(ys_ref[...]), wglu_ref[...]) + bglu_ref[...]
    oc = (zg[:, :S5_WIDTH] * jax.nn.sigmoid(zg[:, S5_WIDTH:])
          * jax.nn.silu(sg_ref[...].astype(F32)))
    mixed = None
    for br, (o_br, w_ref) in enumerate(((oa_ref[...], wa_ref), (ob_ref[...], wb_ref), (oc, wc_ref))):
        gate = jax.nn.sigmoid(jnp.dot(h, wmg_s[:, br * D_MODEL:(br + 1) * D_MODEL], preferred_element_type=F32))
        term = gate * _dot(o_br, w_ref[...])
        mixed = term if mixed is None else mixed + term
    y_ref[...] = x + mod[:, 2 * D_MODEL:] * _dot(mixed, wout_ref[...])


def _merge(x2, row0, mod, mod_idx, nw, oa, ob, ys, zs, layer, wglu, bglu, w_in_t, wa, wb, wc, wout):
    n = x2.shape[0]
    tm = ROW_TILE
    blk0 = row0 // tm
    const = lambda i: (0, 0)
    rows = lambda w: pl.BlockSpec((tm, w), lambda i: (i, 0))
    return pl.pallas_call(
        _merge_kernel,
        grid=(n // tm,),
        in_specs=[
            rows(D_MODEL),
            pl.BlockSpec((1, 1, 3 * D_MODEL), lambda i: (mod_idx(i), 0, 0)),
            pl.BlockSpec((1, D_MODEL), const),
            rows(GLA_WIDTH), rows(MLA_WIDTH), rows(S5_WIDTH),
            pl.BlockSpec((tm, S5_WIDTH), lambda i: (i + blk0, ZS_GATE // S5_WIDTH)),
            pl.BlockSpec((None, S5_WIDTH, 2 * S5_WIDTH), lambda i: (layer, 0, 0)),
            pl.BlockSpec((None, 1, 2 * S5_WIDTH), lambda i: (layer, 0, 0)),
            pl.BlockSpec((pl.Element(1), pl.Element(3 * D_MODEL), pl.Element(D_MODEL)),
                         lambda i: (layer, MERGE_COL, 0), pipeline_mode=pl.Buffered(1)),
            pl.BlockSpec((GLA_WIDTH, D_MODEL), const),
            pl.BlockSpec((MLA_WIDTH, D_MODEL), const),
            pl.BlockSpec((S5_WIDTH, D_MODEL), const),
            pl.BlockSpec((D_MODEL, D_MODEL), const),
        ],
        out_specs=rows(D_MODEL),
        out_shape=jax.ShapeDtypeStruct((n, D_MODEL), F32),
        scratch_shapes=[pltpu.VMEM((D_MODEL, 3 * D_MODEL), BF16)],
        compiler_params=_params(("arbitrary",)),
        name="merge",
    )(x2, mod, nw, oa, ob, ys, zs, wglu, bglu, w_in_t, wa, wb, wc, wout)


def _mla_lane_of_dim():
    half = MLA_ROPE // 2
    first_gap = ROPE_SHIFT - half
    lane = np.zeros(MLA_QK, np.int32)
    for j in range(MLA_NOPE):
        lane[j] = half + j if j < first_gap else 2 * half + j
    for r in range(half):
        lane[MLA_NOPE + r] = r
        lane[MLA_NOPE + half + r] = ROPE_SHIFT + r
    return lane


MLA_LANE_OF_DIM = _mla_lane_of_dim()


def _place_heads(w, heads, lane_of_dim):
    width = len(lane_of_dim)
    src = np.zeros(heads * HEAD_PAD, np.int32)
    used = np.zeros(heads * HEAD_PAD, bool)
    for h in range(heads):
        src[h * HEAD_PAD + lane_of_dim] = h * width + np.arange(width)
        used[h * HEAD_PAD + lane_of_dim] = True
    return jnp.where(jnp.asarray(used), jnp.take(w, jnp.asarray(src), axis=-1), 0.0)


def _rope_tables(n_tok):
    rows = n_tok // GRID_W
    r = jnp.repeat(jnp.arange(rows, dtype=F32), GRID_W)
    col = jnp.tile(jnp.arange(GRID_W, dtype=F32), rows)
    n_freq = MLA_ROPE // 4
    inv = ROPE_THETA ** (-jnp.arange(n_freq, dtype=F32) / n_freq)
    ang = jnp.concatenate([r[:, None] * inv, col[:, None] * inv], axis=-1)
    cos, sin = jnp.cos(ang), jnp.sin(ang)
    ones = jnp.ones((n_tok, MLA_NOPE), F32)
    c = _place_heads(jnp.concatenate([ones, cos, cos], axis=1), 1, MLA_LANE_OF_DIM)
    s = _place_heads(jnp.concatenate([0.0 * ones, -sin, sin], axis=1), 1, MLA_LANE_OF_DIM)
    return jnp.stack([c, s])


def kernel(x_prompt, x_sample, c, c_ctx, cache_mla_ckv, cache_mla_krope, state_gla, state_s5,
           norm_w, w_ada, b_ada, w_in, gla_w_a2, gla_b_a, gla_o_norm,
           mla_q_norm, mla_w_uq, mla_kv_norm, mla_w_uk, mla_w_uv, mla_qh_norm, mla_kh_norm,
           s5_a_re, s5_a_im, s5_log_dt, s5_b_re, s5_b_im, s5_c_re, s5_c_im, s5_d, s5_w_glu, s5_b_glu,
           w_bo_gla, w_bo_mla, w_bo_s5, w_out):
    bsz, seq, _ = x_prompt.shape
    dbsz, dseq, _ = x_sample.shape
    ctx_row = 8 - 1
    assert dbsz <= ctx_row and (bsz * seq) % ROW_TILE == 0 and dseq % ROW_TILE == 0

    cond8 = jnp.zeros((8, D_MODEL), F32).at[0:dbsz].set(c).at[ctx_row].set(c_ctx)
    ada = _ada(cond8, w_ada, b_ada)

    vec = lambda a: a.reshape(DEPTH, 2, 1, S5_NSTATE)
    ldt = jnp.repeat(s5_log_dt[..., None], S5_STATE, axis=-1)
    rows_gp = lambda t: t.reshape(DEPTH, S5_TILES, LANE, S5_STATE)
    bt = lambda b: rows_gp(b.transpose(0, 1, 3, 2))
    wmat, tab8, tab1 = _s5_prep(vec(s5_a_re), vec(s5_a_im), vec(ldt), bt(s5_b_re), bt(s5_b_im),
                                rows_gp(s5_c_re), rows_gp(s5_c_im), s5_d.reshape(DEPTH, 1, S5_WIDTH))
    wglu = s5_w_glu.astype(BF16)
    bglu = s5_b_glu.reshape(DEPTH, 1, 2 * S5_WIDTH)

    wuq = _place_heads(mla_w_uq, MLA_HEADS, MLA_LANE_OF_DIM).astype(BF16)
    wuk = _place_heads(mla_w_uk, MLA_HEADS, MLA_LANE_OF_DIM[:MLA_NOPE]).astype(BF16)
    wuv = mla_w_uv.astype(BF16)
    qhn = _place_heads(mla_qh_norm, 1, MLA_LANE_OF_DIM)
    khn = _place_heads(mla_kh_norm, 1, MLA_LANE_OF_DIM)
    e_np = np.zeros((LANE, MLA_HEADS * HEAD_PAD), np.float32)
    for h in range(MLA_HEADS):
        for i in range(MLA_ROPE):
            e_np[i, h * HEAD_PAD + MLA_LANE_OF_DIM[MLA_NOPE + i]] = 1.0
    e_place = jnp.asarray(e_np, BF16)
    rope_tab = _rope_tables(dseq)
    ckr_pad = jnp.pad(cache_mla_krope, ((0, 0), (0, 0), (0, 0), (0, LANE - MLA_ROPE)))

    zrow = lambda n: jnp.zeros((DEPTH, n, GLA_QK), F32)
    waf = jnp.concatenate([gla_w_a2[:, 0], zrow(LANE - GLA_RANK)], axis=1).astype(BF16)
    wab = jnp.concatenate([zrow(GLA_RANK), gla_w_a2[:, 1], zrow(LANE - 2 * GLA_RANK)], axis=1).astype(BF16)
    sgla = state_gla.reshape(dbsz, DEPTH, 2, GLA_QK, GLA_DV)
    ss5 = state_s5.reshape(dbsz, DEPTH, 2, 2, S5_NSTATE)
    zero_s5 = jnp.zeros((bsz, 2, 2, S5_NSTATE), F32)

    hp = x_prompt.reshape(bsz * seq, D_MODEL)
    hs = x_sample.reshape(dbsz * dseq, D_MODEL)
    ckv_l, krope_l, gla_l, s5_l = [], [], [], []
    w_in_t = jnp.swapaxes(w_in, 1, 2)
    for l in range(DEPTH):
        mod = ada[l].reshape(8, 1, 3 * D_MODEL)
        nw = norm_w[l].reshape(1, D_MODEL)
        mla_w = (mla_q_norm[l].reshape(1, -1), wuq[l], mla_kv_norm[l].reshape(1, -1), wuk[l], wuv[l],
                 qhn[l].reshape(1, -1), khn[l].reshape(1, -1), e_place)
        wbo = (w_bo_gla[l].astype(BF16), w_bo_mla[l].astype(BF16), w_bo_s5[l].astype(BF16))
        wout = w_out[l].astype(BF16)
        onorm = gla_o_norm[l].reshape(1, GLA_DV)

        p_rows, p_blocks, blocks_per_seq = bsz * seq, bsz * seq // ROW_TILE, dseq // ROW_TILE
        mod_idx = lambda i: jnp.where(i < p_blocks, ctx_row, (i - p_blocks) // blocks_per_seq)
        zg, zm, zs = _in_proj(hp, hs, mod, mod_idx, nw, w_in_t, l)

        def mixers(x2, row0, nb, n, ctx):
            if ctx:
                gctx = (sgla, l)
                x0, x0_blk = ss5, (nb, None, 2, 2, S5_TILE_STATE)
                x0_idx = lambda j: (0, l, 0, 0, j)
                mctx, rt = (cache_mla_ckv, ckr_pad, l), rope_tab
            else:
                gctx = None
                x0, x0_blk = zero_s5, (nb, 2, 2, S5_TILE_STATE)
                x0_idx = lambda j: (0, 0, 0, j)
                mctx, rt = None, None
            oa, st_gla = _gla(zg, row0, gctx, waf[l], wab[l], gla_b_a[l], onorm, nb, n)
            ob, ckv = _mla(zm, row0, mctx, mla_w, rt, nb, n)
            y_ssm, st_s5 = _s5_scan(zs, row0, x0, x0_blk, x0_idx, l, wmat, tab8, tab1, nb, n)
            grp_mod_idx = lambda i: mod_idx(i + row0 // ROW_TILE)
            y = _merge(x2, row0, mod, grp_mod_idx, nw, oa, ob, y_ssm, zs, l, wglu, bglu, w_in_t, *wbo, wout)
            return y, ckv, st_gla, st_s5

        hp_next, ckv_p, st_gla_p, st_s5_p = mixers(hp, 0, bsz, seq, False)
        hs = mixers(hs, p_rows, dbsz, dseq, True)[0]
        hp = hp_next
        ckv_l.append(ckv_p.reshape(bsz, seq, MLA_KV_LORA))
        krope_l.append(zm[:p_rows, ZM_KR:ZM_KR + MLA_ROPE].astype(F32).reshape(bsz, seq, MLA_ROPE))
        gla_l.append(st_gla_p.reshape(bsz, 2, GLA_HEADS, GLA_DK, GLA_DV))
        s5_l.append(st_s5_p.reshape(bsz, 2, 2, S5_GROUPS, S5_STATE))

    return (hp.reshape(bsz, seq, D_MODEL), hs.reshape(dbsz, dseq, D_MODEL),
            jnp.stack(ckv_l, axis=1), jnp.stack(krope_l, axis=1),
            jnp.stack(gla_l, axis=1), jnp.stack(s5_l, axis=1))
```

```python
import functools

import jax
import jax.numpy as jnp
import numpy as np
from jax import lax
from jax.experimental import pallas as pl
from jax.experimental.pallas import tpu as pltpu

F32 = jnp.float32
BF16 = jnp.bfloat16

EPS = 1e-6
D_MODEL = 1024
DEPTH = 2
GRID_W = 64
ROPE_THETA = 10000.0
GLA_HEADS = 4
GLA_DK = 64
GLA_DV = 128
GLA_RANK = 16
GLA_GATE_NORM = 16.0
GLA_QK = GLA_HEADS * GLA_DK
GLA_WIDTH = GLA_HEADS * GLA_DV
MLA_HEADS = 4
MLA_Q_LORA = 384
MLA_KV_LORA = 256
MLA_NOPE = 64
MLA_ROPE = 32
MLA_QK = MLA_NOPE + MLA_ROPE
MLA_DV = 128
MLA_WIDTH = MLA_HEADS * MLA_DV
S5_WIDTH = 512
S5_GROUP = 16
S5_GROUPS = 32
S5_STATE = 64
S5_NSTATE = S5_GROUPS * S5_STATE

LANE = 128
HEAD_PAD = LANE
ROPE_SHIFT = LANE // 2
CHUNK = 64
GLA_STEP = 256
GLA_SEQS_PER_STEP = 2
GLA_ROWS_PER_STEP = 1024
S5_TILES = S5_WIDTH // LANE
S5_TILE_STATE = S5_NSTATE // S5_TILES
ROW_TILE = 512
Q_TILE = 512
PROJ_TILE = 256
VMEM_LIMIT = 56 * 1024 * 1024

MERGE_COL = 3776
ZG_Q, ZG_K, ZG_V, ZG_A, ZG_GATE, ZG_W = 0, 256, 512, 1024, 1152, 1664
ZM_Q, ZM_KV, ZM_KR, ZM_GATE, ZM_W = 0, 384, 640, 768, 1280
ZS_U, ZS_GATE, ZS_W = 0, 512, 1024
ZG_BASE, ZM_BASE, ZS_BASE, PACK_W = 0, ZG_W, ZG_W + ZM_W, ZG_W + ZM_W + ZS_W
IN_PIECES = (
    (0, 8, ZG_BASE + ZG_Q, None),
    (1024, 1, ZG_BASE + ZG_A, 2 * GLA_RANK),
    (1056, 4, ZG_BASE + ZG_GATE, None),
    (1568, 5, ZM_BASE + ZM_Q, None),
    (2208, 1, ZM_BASE + ZM_KR, MLA_ROPE),
    (2240, 4, ZM_BASE + ZM_GATE, None),
    (2752, 8, ZS_BASE + ZS_U, None),
)


def _dot(a, b):
    return jnp.dot(a.astype(BF16), b.astype(BF16), preferred_element_type=F32)


def _dot_nt(a, b):
    return lax.dot_general(a.astype(BF16), b.astype(BF16), (((1,), (1,)), ((), ())),
                           preferred_element_type=F32)


def _split_bf16(x, parts):
    out = []
    r = x
    for _ in range(parts):
        p = r.astype(BF16)
        out.append(p)
        r = r - p.astype(F32)
    return out


def _layer_spec(shape, layer):
    return pl.BlockSpec((None,) + tuple(shape), lambda *_: (layer,) + (0,) * len(shape))


def _params(sem):
    return pltpu.CompilerParams(dimension_semantics=sem, vmem_limit_bytes=VMEM_LIMIT)


def _ada_kernel(c_ref, w_ref, b_ref, o_ref):
    s = jax.nn.silu(c_ref[...])
    o_ref[...] = _dot(s, w_ref[...]) + b_ref[...]


def _ada(cond8, w_ada, b_ada):
    tn = 1024
    return pl.pallas_call(
        _ada_kernel,
        grid=(DEPTH, 3 * D_MODEL // tn),
        in_specs=[
            pl.BlockSpec((8, D_MODEL), lambda l, n: (0, 0)),
            pl.BlockSpec((None, D_MODEL, tn), lambda l, n: (l, 0, n)),
            pl.BlockSpec((None, 1, tn), lambda l, n: (l, 0, n)),
        ],
        out_specs=pl.BlockSpec((None, 8, tn), lambda l, n: (l, 0, n)),
        out_shape=jax.ShapeDtypeStruct((DEPTH, 8, 3 * D_MODEL), F32),
        compiler_params=_params(("parallel", "parallel")),
        name="ada",
    )(cond8, w_ada, b_ada.reshape(DEPTH, 1, 3 * D_MODEL))


def _mod_rmsnorm(x, nw, mod):
    ms = jnp.mean(x * x, axis=-1, keepdims=True)
    y = x * lax.rsqrt(ms + EPS) * nw
    return y * (1.0 + mod[:, D_MODEL:2 * D_MODEL]) + mod[:, 0:D_MODEL]


def _pack_transposed(w_ref, wb_s, src, tiles, dst, keep):
    lane = lax.broadcasted_iota(jnp.int32, (D_MODEL, LANE), 1)
    for t in range(tiles):
        blk = w_ref[src + t * LANE:src + (t + 1) * LANE, :].T
        if keep is not None:
            blk = jnp.where(lane < keep, blk, 0.0)
        wb_s[:, dst + t * LANE:dst + (t + 1) * LANE] = blk.astype(BF16)


def _in_proj_kernel(xp_ref, xs_ref, mod_ref, nw_ref, w_ref, zg_ref, zm_ref, zs_ref, wb_s, *, p_blocks):
    i = pl.program_id(0)

    @pl.when(i == 0)
    def _():
        for src, tiles, dst, keep in IN_PIECES:
            _pack_transposed(w_ref, wb_s, src, tiles, dst, keep)

    x = jnp.where(i < p_blocks, xp_ref[...], xs_ref[...])
    h = _mod_rmsnorm(x, nw_ref[...], mod_ref[0]).astype(BF16)
    z = jnp.dot(h, wb_s[...], preferred_element_type=F32)
    zg_ref[...] = z[:, ZG_BASE:ZG_BASE + ZG_W].astype(BF16)
    zm_ref[...] = z[:, ZM_BASE:ZM_BASE + ZM_W].astype(BF16)
    zs_ref[...] = z[:, ZS_BASE:ZS_BASE + ZS_W].astype(BF16)


def _two_group_rows(width, p_blocks):
    tm = ROW_TILE
    return (pl.BlockSpec((tm, width), lambda i: (jnp.minimum(i, p_blocks - 1), 0)),
            pl.BlockSpec((tm, width), lambda i: (jnp.maximum(i - p_blocks, 0), 0)))


def _in_proj(xp, xs, mod, mod_idx, nw, w_in_t, layer):
    tm = ROW_TILE
    p_blocks = xp.shape[0] // tm
    n = xp.shape[0] + xs.shape[0]
    return pl.pallas_call(
        functools.partial(_in_proj_kernel, p_blocks=p_blocks),
        grid=(n // tm,),
        in_specs=[
            *_two_group_rows(D_MODEL, p_blocks),
            pl.BlockSpec((None, 1, 1, 3 * D_MODEL), lambda i: (layer, mod_idx(i), 0, 0)),
            _layer_spec((1, D_MODEL), layer),
            pl.BlockSpec((None, MERGE_COL, D_MODEL), lambda i: (layer, 0, 0), pipeline_mode=pl.Buffered(1)),
        ],
        out_specs=[
            pl.BlockSpec((tm, ZG_W), lambda i: (i, 0)),
            pl.BlockSpec((tm, ZM_W), lambda i: (i, 0)),
            pl.BlockSpec((tm, ZS_W), lambda i: (i, 0)),
        ],
        out_shape=[
            jax.ShapeDtypeStruct((n, ZG_W), BF16),
            jax.ShapeDtypeStruct((n, ZM_W), BF16),
            jax.ShapeDtypeStruct((n, ZS_W), BF16),
        ],
        scratch_shapes=[pltpu.VMEM((D_MODEL, PACK_W), BF16)],
        compiler_params=_params(("arbitrary",)),
        name="in_proj",
    )(xp, xs, mod, nw, w_in_t)


def _gla_kernel(*refs, nsteps, seq, nseq, has_ctx):
    it = iter(refs)
    zg_ref = next(it)
    s0_ref = next(it) if has_ctx else None
    waf_ref, wab_ref, ba_ref, onorm_ref, o_ref, sfin_ref, la_s, o_s, st_s = (next(it) for _ in range(9))
    chains = [(g, d) for g in range(nseq) for d in (0, 1)]
    inv_norm = 1.0 / GLA_GATE_NORM
    zero_blk = jnp.zeros((GLA_DK, GLA_DV), F32)
    for ch, (g, d) in enumerate(chains):
        if d == 0:
            a_blk = zg_ref[g * seq:(g + 1) * seq, ZG_A:ZG_A + LANE]
        wa_ref = waf_ref if d == 0 else wab_ref
        la_s[ch] = jax.nn.log_sigmoid(_dot(a_blk, wa_ref[...]) + ba_ref[d:d + 1, :]) * inv_norm
        if has_ctx:
            s0 = s0_ref[g, d]
            rows_bd = []
            for h in range(GLA_HEADS):
                sh = s0[h * GLA_DK:(h + 1) * GLA_DK, :]
                rows_bd.append(jnp.concatenate([sh if h2 == h else zero_blk for h2 in range(GLA_HEADS)], axis=1))
            st_s[ch] = jnp.concatenate(rows_bd, axis=0).T
        else:
            st_s[ch] = jnp.zeros((GLA_WIDTH, GLA_QK), F32)

    def iota(shape, axis, shift):
        return lax.shift_right_logical(lax.broadcasted_iota(jnp.int32, shape, axis), shift)

    log_chunk, log_dv = CHUNK.bit_length() - 1, GLA_DV.bit_length() - 1
    row = lax.broadcasted_iota(jnp.int32, (GLA_STEP, GLA_STEP), 0)
    col = lax.broadcasted_iota(jnp.int32, (GLA_STEP, GLA_STEP), 1)
    same_chunk = iota((GLA_STEP, GLA_STEP), 0, log_chunk) == iota((GLA_STEP, GLA_STEP), 1, log_chunk)
    masks = (same_chunk & (row >= col), same_chunk & (row <= col))
    lane_head = iota((GLA_STEP, GLA_QK), 1, log_chunk)
    row_chunk = iota((GLA_STEP, GLA_QK), 0, log_chunk)
    state_blk = iota((GLA_WIDTH, GLA_QK), 0, log_dv) == iota((GLA_WIDTH, GLA_QK), 1, log_chunk)
    qscale = GLA_DK ** -0.5
    nch = GLA_STEP // CHUNK
    n_chain = len(chains)

    def step(i, carry):
        rows, cum = [], []
        for ch, (g, d) in enumerate(chains):
            r0 = pl.multiple_of((i if d == 0 else nsteps - 1 - i) * GLA_STEP, GLA_STEP)
            rows.append((pl.ds(g * seq + r0, GLA_STEP), pl.ds(r0, GLA_STEP)))
            a_hi, a_lo = _split_bf16(la_s[ch, rows[ch][1], :], 2)
            tri = masks[d].astype(BF16)
            cum.append(jnp.dot(tri, a_hi, preferred_element_type=F32)
                       + jnp.dot(tri, a_lo, preferred_element_type=F32))
        blast, v, v_t, qd, kd, kr = [], [], [], [], [], []
        for ch, (g, d) in enumerate(chains):
            edge = CHUNK - 1 if d == 0 else 0
            blast.append([cum[ch][c * CHUNK + edge:c * CHUNK + edge + 1, :] for c in range(nch)])
            bl = jnp.concatenate([jnp.broadcast_to(b, (CHUNK, GLA_QK)) for b in blast[ch]], axis=0)
            zrows = rows[ch][0]
            q = zg_ref[zrows, ZG_Q:ZG_Q + GLA_QK].astype(F32) * qscale
            k = zg_ref[zrows, ZG_K:ZG_K + GLA_QK].astype(F32)
            v.append(zg_ref[zrows, ZG_V:ZG_V + GLA_WIDTH])
            v_t.append(v[ch].astype(F32).T.astype(BF16))
            qd.append(q * jnp.exp(cum[ch]))
            kd.append((k * jnp.exp(-cum[ch])).astype(BF16))
            kr.append(k * jnp.exp(bl - cum[ch]))
        outs = [[] for _ in chains]
        for h in range(GLA_HEADS):
            for ch, (g, d) in enumerate(chains):
                qh = jnp.where(lane_head == h, qd[ch], 0.0)
                att = jnp.where(masks[d], _dot_nt(qh, kd[ch]), 0.0)
                outs[ch].append(_dot(att, v[ch][:, h * GLA_DV:(h + 1) * GLA_DV]))
        s = [st_s[ch] for ch in range(n_chain)]
        inter = [[None] * nch for _ in chains]
        for j in range(nch):
            for ch, (g, d) in enumerate(chains):
                c = j if d == 0 else nch - 1 - j
                inter[ch][c] = _dot_nt(qd[ch][c * CHUNK:(c + 1) * CHUNK, :], s[ch])
                kv_t = jnp.where(state_blk, _dot(v_t[ch], jnp.where(row_chunk == c, kr[ch], 0.0)), 0.0)
                s[ch] = s[ch] * jnp.exp(blast[ch][c]) + kv_t
        for ch in range(n_chain):
            st_s[ch] = s[ch]
            o_s[ch, rows[ch][1], :] = jnp.concatenate(outs[ch], axis=1) + jnp.concatenate(inter[ch], axis=0)
        return carry

    lax.fori_loop(0, nsteps, step, 0)
    onorm = onorm_ref[...]
    for ch, (g, d) in enumerate(chains):
        s_fin = st_s[ch].T
        for h in range(GLA_HEADS):
            sfin_ref[g, d, h * GLA_DK:(h + 1) * GLA_DK, :] = (
                s_fin[h * GLA_DK:(h + 1) * GLA_DK, h * GLA_DV:(h + 1) * GLA_DV])
    for g in range(nseq):
        srows = slice(g * seq, (g + 1) * seq)
        o = o_s[2 * g] + o_s[2 * g + 1]
        gate = zg_ref[srows, ZG_GATE:ZG_GATE + GLA_WIDTH].astype(F32)
        for h in range(GLA_HEADS):
            vs = slice(h * GLA_DV, (h + 1) * GLA_DV)
            oh = o[:, vs]
            ms = jnp.mean(oh * oh, axis=-1, keepdims=True)
            o_ref[srows, vs] = oh * lax.rsqrt(ms + EPS) * onorm * jax.nn.silu(gate[:, vs])


def _gla(zg, row0, ctx, layer, waf, wab, ba, onorm, bsz, seq):
    nseq = max(GLA_SEQS_PER_STEP, GLA_ROWS_PER_STEP // seq)
    blk0 = row0 // (nseq * seq)
    in_specs = [pl.BlockSpec((nseq * seq, ZG_W), lambda b: (b + blk0, 0))]
    args = [zg]
    if ctx is not None:
        in_specs.append(pl.BlockSpec((nseq, None, 2, GLA_QK, GLA_DV), lambda b: (b, layer, 0, 0, 0)))
        args.append(ctx)
    in_specs += [
        _layer_spec((LANE, GLA_QK), layer),
        _layer_spec((LANE, GLA_QK), layer),
        _layer_spec((2, GLA_QK), layer),
        _layer_spec((1, GLA_DV), layer),
    ]
    return pl.pallas_call(
        functools.partial(_gla_kernel, nsteps=seq // GLA_STEP, seq=seq, nseq=nseq, has_ctx=ctx is not None),
        grid=(bsz // nseq,),
        in_specs=in_specs,
        out_specs=[
            pl.BlockSpec((nseq * seq, GLA_WIDTH), lambda b: (b, 0)),
            pl.BlockSpec((nseq, 2, GLA_QK, GLA_DV), lambda b: (b, 0, 0, 0)),
        ],
        out_shape=[
            jax.ShapeDtypeStruct((bsz * seq, GLA_WIDTH), F32),
            jax.ShapeDtypeStruct((bsz, 2, GLA_QK, GLA_DV), F32),
        ],
        scratch_shapes=[
            pltpu.VMEM((2 * nseq, seq, GLA_QK), F32),
            pltpu.VMEM((2 * nseq, seq, GLA_WIDTH), F32),
            pltpu.VMEM((2 * nseq, GLA_WIDTH, GLA_QK), F32),
        ],
        compiler_params=_params(("parallel",)),
        name="gla",
    )(*args, waf, wab, ba, onorm)


def _rms(x, w):
    ms = jnp.mean(x * x, axis=-1, keepdims=True)
    return x * lax.rsqrt(ms + EPS) * w


def _head_sums_mxu(x):
    width = x.shape[-1]
    shift = HEAD_PAD.bit_length() - 1
    gi = lax.shift_right_logical(lax.broadcasted_iota(jnp.int32, (width, width), 0), shift)
    gj = lax.shift_right_logical(lax.broadcasted_iota(jnp.int32, (width, width), 1), shift)
    return _dot(x * x, jnp.where(gi == gj, 1.0, 0.0))


def _head_norm(x, w, rope, on_mxu):
    sums = _head_sums_mxu(x) if on_mxu else None
    outs = []
    for h in range(MLA_HEADS):
        hs = slice(h * HEAD_PAD, (h + 1) * HEAD_PAD)
        xh = x[:, hs]
        ss = sums[:, hs] if on_mxu else jnp.sum(xh * xh, axis=-1, keepdims=True)
        yh = xh * lax.rsqrt(ss * (1.0 / MLA_QK) + EPS) * w
        if rope is not None:
            c, s = rope
            yh = yh * c + pltpu.roll(yh, ROPE_SHIFT, 1) * s
        outs.append(yh)
    return outs


def _place_rope_key(kr, e):
    return sum(jnp.dot(p, e, preferred_element_type=F32) for p in _split_bf16(kr, 3))


def _mla_kernel(*refs, seq, n_ctx, use_rope):
    it = iter(refs)
    zm_ref = next(it)
    if n_ctx:
        cckv_ref, ckr_ref = next(it), next(it)
    qn_ref, wuq_ref, kvn_ref, wuk_ref, wuv_ref, qhn_ref, khn_ref, e_ref = (next(it) for _ in range(8))
    rope_ref = next(it) if use_rope else None
    o_ref, ckv_ref = next(it), next(it)
    q_s, k_s, v_s = next(it), next(it), next(it)

    qscale = MLA_QK ** -0.5
    heads = [slice(h * HEAD_PAD, (h + 1) * HEAD_PAD) for h in range(MLA_HEADS)]

    def keys_values(ckv, k_rope_placed, rope, k_rows):
        kh = _head_norm(_dot(ckv, wuk_ref[...]) + k_rope_placed, khn_ref[...], rope, False)
        for h, hs in enumerate(heads):
            k_s[k_rows, hs] = kh[h].astype(BF16)
        v_s[k_rows, :] = _dot(ckv, wuv_ref[...]).astype(BF16)

    def latent_tile(i, carry):
        r0 = pl.multiple_of(i * PROJ_TILE, PROJ_TILE)
        rows = pl.ds(r0, PROJ_TILE)
        rope = (rope_ref[0, rows, :], rope_ref[1, rows, :]) if use_rope else None
        ckv = _rms(zm_ref[rows, ZM_KV:ZM_KV + MLA_KV_LORA].astype(F32), kvn_ref[...])
        ckv_ref[rows, :] = ckv
        k_pe = jnp.dot(zm_ref[rows, ZM_KR:ZM_KR + LANE], e_ref[...], preferred_element_type=F32)
        keys_values(ckv, k_pe, rope, pl.ds(n_ctx + r0, PROJ_TILE))
        cq = _rms(zm_ref[rows, ZM_Q:ZM_Q + MLA_Q_LORA].astype(F32), qn_ref[...])
        qh = _head_norm(_dot(cq, wuq_ref[...]), qhn_ref[...], rope, True)
        for h, hs in enumerate(heads):
            q_s[rows, hs] = (qh[h] * qscale).astype(BF16)
        return carry

    lax.fori_loop(0, seq // PROJ_TILE, latent_tile, 0)

    def context_tile(i, carry):
        rows = pl.ds(pl.multiple_of(i * PROJ_TILE, PROJ_TILE), PROJ_TILE)
        keys_values(cckv_ref[rows, :], _place_rope_key(ckr_ref[rows, :], e_ref[...]), None, rows)
        return carry

    if n_ctx:
        lax.fori_loop(0, n_ctx // PROJ_TILE, context_tile, 0)

    q_tile = min(seq, Q_TILE)

    def q_block(i, carry):
        rows = pl.ds(pl.multiple_of(i * q_tile, q_tile), q_tile)
        gate = zm_ref[rows, ZM_GATE:ZM_GATE + MLA_WIDTH].astype(F32)
        s = [lax.dot_general(q_s[rows, hs], k_s[:, hs], (((1,), (1,)), ((), ())),
                             preferred_element_type=F32) for hs in heads]
        p, l = [], []
        for h in range(MLA_HEADS):
            e = jnp.exp(s[h] - jnp.max(s[h], axis=-1, keepdims=True))
            l.append(jnp.sum(e, axis=-1, keepdims=True))
            p.append(e.astype(BF16))
        for h, hs in enumerate(heads):
            o = jnp.dot(p[h], v_s[:, hs], preferred_element_type=F32) / l[h]
            o_ref[rows, hs] = o * jax.nn.silu(gate[:, hs])
        return carry

    lax.fori_loop(0, seq // q_tile, q_block, 0)


def _mla(zm, row0, ctx, layer, w, rope_tab, bsz, seq):
    n_ctx = 0 if ctx is None else ctx[0].shape[-2]
    blk0 = row0 // seq
    in_specs = [pl.BlockSpec((seq, ZM_W), lambda b: (b + blk0, 0))]
    args = [zm]
    if ctx is not None:
        cckv, ckr = ctx
        in_specs += [
            pl.BlockSpec((None, None, n_ctx, MLA_KV_LORA), lambda b: (b, layer, 0, 0)),
            pl.BlockSpec((None, None, n_ctx, LANE), lambda b: (b, layer, 0, 0)),
        ]
        args += [cckv, ckr]
    in_specs += [
        _layer_spec((1, MLA_Q_LORA), layer),
        _layer_spec((MLA_Q_LORA, MLA_HEADS * HEAD_PAD), layer),
        _layer_spec((1, MLA_KV_LORA), layer),
        _layer_spec((MLA_KV_LORA, MLA_HEADS * HEAD_PAD), layer),
        _layer_spec((MLA_KV_LORA, MLA_WIDTH), layer),
        _layer_spec((1, HEAD_PAD), layer),
        _layer_spec((1, HEAD_PAD), layer),
        pl.BlockSpec((LANE, MLA_HEADS * HEAD_PAD), lambda b: (0, 0)),
    ]
    args += list(w)
    if rope_tab is not None:
        in_specs.append(pl.BlockSpec((2, seq, HEAD_PAD), lambda b: (0, 0, 0)))
        args.append(rope_tab)
    return pl.pallas_call(
        functools.partial(_mla_kernel, seq=seq, n_ctx=n_ctx, use_rope=rope_tab is not None),
        grid=(bsz,),
        in_specs=in_specs,
        out_specs=[
            pl.BlockSpec((seq, MLA_WIDTH), lambda b: (b, 0)),
            pl.BlockSpec((seq, MLA_KV_LORA), lambda b: (b, 0)),
        ],
        out_shape=[
            jax.ShapeDtypeStruct((bsz * seq, MLA_WIDTH), F32),
            jax.ShapeDtypeStruct((bsz * seq, MLA_KV_LORA), F32),
        ],
        scratch_shapes=[
            pltpu.VMEM((seq, MLA_HEADS * HEAD_PAD), BF16),
            pltpu.VMEM((n_ctx + seq, MLA_HEADS * HEAD_PAD), BF16),
            pltpu.VMEM((n_ctx + seq, MLA_WIDTH), BF16),
        ],
        compiler_params=_params(("parallel",)),
        name="mla",
    )(*args)


S5_T = 8
S5_R = CHUNK // S5_T
S5_ROW = S5_T * LANE
S5_W = 2 * S5_TILE_STATE
W_M, W_SF, W_SB, W_CF, W_CB = range(5)


def _cmul(ar, ai, br, bi):
    return ar * br - ai * bi, ar * bi + ai * br


def _s5_prep_kernel(are_ref, aim_ref, ldt_ref, bre_ref, bim_ref, cre_ref, cim_ref, d_ref,
                    w_ref, tab8_ref, tab1_ref):
    gr = lax.shift_right_logical(lax.broadcasted_iota(jnp.int32, (LANE, S5_TILE_STATE), 0),
                                 S5_GROUP.bit_length() - 1)
    gc = lax.shift_right_logical(lax.broadcasted_iota(jnp.int32, (LANE, S5_TILE_STATE), 1),
                                 S5_STATE.bit_length() - 1)
    spread = lambda ref: jnp.where(gr == gc, jnp.concatenate([ref[...]] * (LANE // S5_GROUP), axis=1), 0.0)
    b_re, b_im = spread(bre_ref), spread(bim_ref)
    c_re, c_im = spread(cre_ref), spread(cim_ref)
    c_cat = jnp.concatenate([c_re, c_im], axis=1).astype(BF16)
    kern = []
    for d in (0, 1):
        a_re, a_im = are_ref[d], aim_ref[d]
        dt = jnp.exp(ldt_ref[d])
        lam = a_re * dt
        th = a_im * dt
        mag = jnp.exp(lam)
        ab_re = mag * jnp.cos(th)
        ab_im = mag * jnp.sin(th)
        den = a_re * a_re + a_im * a_im
        n_re = ab_re - 1.0
        cf_re = (n_re * a_re + ab_im * a_im) / den
        cf_im = (ab_im * a_re - n_re * a_im) / den
        bp_re, bp_im = _cmul(b_re, b_im, cf_re, cf_im)
        k = lax.broadcasted_iota(jnp.int32, (2 * S5_T, S5_TILE_STATE), 0).astype(F32)
        pmag = jnp.exp(k * lam)
        pw_re = pmag * jnp.cos(k * th)
        pw_im = pmag * jnp.sin(k * th)
        taps = []
        for p in range(S5_T + 1):
            ar, ai = pw_re[p:p + 1, :], pw_im[p:p + 1, :]
            l_re, l_im = _cmul(bp_re, bp_im, ar, ai)
            v_re, v_im = _cmul(c_re, c_im, ar, ai)
            t_in = S5_T - 1 - p if d == 0 else p
            if 0 <= t_in < S5_T:
                w_ref[W_SF + d, t_in * LANE:(t_in + 1) * LANE, :] = (
                    jnp.concatenate([l_re, l_im], axis=1).astype(BF16))
            t_out = p - 1 if d == 0 else S5_T - p
            if 0 <= t_out < S5_T:
                w_ref[W_CF + d, t_out * LANE:(t_out + 1) * LANE, :] = (
                    jnp.concatenate([v_re, -v_im], axis=1).astype(BF16))
            if p < S5_T:
                taps.append(_dot_nt(jnp.concatenate([l_re, -l_im], axis=1), c_cat))
        kern.append(taps)
        r = lax.broadcasted_iota(jnp.int32, (S5_R, S5_TILE_STATE), 0).astype(F32) * float(S5_T)
        r1 = r + float(S5_T)
        pm = jnp.exp(r * lam)
        qm = jnp.exp(-(r1 * lam))
        tab8_ref[d, 0] = pm * jnp.cos(r * th)
        tab8_ref[d, 1] = pm * jnp.sin(r * th)
        tab8_ref[d, 2] = qm * jnp.cos(r1 * th)
        tab8_ref[d, 3] = -(qm * jnp.sin(r1 * th))
        mc = jnp.exp(float(CHUNK) * lam)
        tab1_ref[d, 0:1, :] = mc * jnp.cos(float(CHUNK) * th)
        tab1_ref[d, 1:2, :] = mc * jnp.sin(float(CHUNK) * th)
    row = lax.broadcasted_iota(jnp.int32, (LANE, LANE), 0)
    col = lax.broadcasted_iota(jnp.int32, (LANE, LANE), 1)
    skip = jnp.where(row == col, d_ref[...], 0.0)
    for t in range(S5_T):
        for t2 in range(S5_T):
            if t < t2:
                blk = kern[0][t2 - t]
            elif t > t2:
                blk = kern[1][t - t2]
            else:
                blk = kern[0][0] + kern[1][0] + skip
            w_ref[W_M, t * LANE:(t + 1) * LANE, t2 * LANE:(t2 + 1) * LANE] = blk.astype(BF16)


def _s5_prep(a_re, a_im, ldt, b_re, b_im, c_re, c_im, dsk):
    vec = pl.BlockSpec((None, 2, 1, S5_TILE_STATE), lambda l, j: (l, 0, 0, j))
    blk = pl.BlockSpec((None, None, LANE, S5_STATE), lambda l, j: (l, j, 0, 0))
    return pl.pallas_call(
        _s5_prep_kernel,
        grid=(DEPTH, S5_TILES),
        in_specs=[vec, vec, vec, blk, blk, blk, blk,
                  pl.BlockSpec((None, 1, LANE), lambda l, j: (l, 0, j))],
        out_specs=[
            pl.BlockSpec((None, None, 5, S5_ROW, S5_W), lambda l, j: (l, j, 0, 0, 0)),
            pl.BlockSpec((None, None, 2, 4, S5_R, S5_TILE_STATE), lambda l, j: (l, j, 0, 0, 0, 0)),
            pl.BlockSpec((None, None, 2, 2, S5_TILE_STATE), lambda l, j: (l, j, 0, 0, 0)),
        ],
        out_shape=[
            jax.ShapeDtypeStruct((DEPTH, S5_TILES, 5, S5_ROW, S5_W), BF16),
            jax.ShapeDtypeStruct((DEPTH, S5_TILES, 2, 4, S5_R, S5_TILE_STATE), F32),
            jax.ShapeDtypeStruct((DEPTH, S5_TILES, 2, 2, S5_TILE_STATE), F32),
        ],
        compiler_params=_params(("parallel", "parallel")),
        name="s5_prep",
    )(a_re, a_im, ldt, b_re, b_im, c_re, c_im, dsk)


def _s5_scan_kernel(u_ref, x0_ref, w_ref, tab8_ref, tab1_ref, y_ref, fs_ref, u_s, *, nseq, nb):
    groups = nseq * nb
    nrow = groups * S5_R
    ts = S5_TILE_STATE
    u_s[...] = u_ref[...].astype(F32)
    u8 = jnp.concatenate([u_s[pl.ds(t, nrow, stride=S5_T), :] for t in range(S5_T)],
                         axis=1).astype(BF16)
    ef = jnp.dot(u8, w_ref[W_SF], preferred_element_type=F32).reshape(groups, S5_R, S5_W)
    eb = jnp.dot(u8, w_ref[W_SB], preferred_element_type=F32).reshape(groups, S5_R, S5_W)
    rowi = lax.broadcasted_iota(jnp.int32, (groups, S5_R, ts), 1)

    def prefix(x):
        for s in (1, 2, 4):
            x = x + jnp.where(rowi >= s, pltpu.roll(x, s, 1), 0.0)
        return x

    def suffix(x):
        for s in (1, 2, 4):
            x = x + jnp.where(rowi < S5_R - s, pltpu.roll(x, S5_R - s, 1), 0.0)
        return x

    p_re, p_im, q_re, q_im = (tab8_ref[0, i] for i in range(4))
    a_re, a_im = tab1_ref[0, 0:1, :], tab1_ref[0, 1:2, :]
    w_re, w_im = _cmul(q_re, q_im, ef[:, :, :ts], ef[:, :, ts:])
    cs_re, cs_im = prefix(w_re), prefix(w_im)
    st_re, st_im = [], []
    for s in range(nseq):
        x_re, x_im = x0_ref[s, 0, 0:1, :], x0_ref[s, 0, 1:2, :]
        for b in range(nb):
            g = s * nb + b
            st_re.append(x_re)
            st_im.append(x_im)
            x_re, x_im = _cmul(a_re, a_im, x_re + cs_re[g, S5_R - 1:S5_R, :],
                               x_im + cs_im[g, S5_R - 1:S5_R, :])
        fs_ref[s, 0, 0:1, :] = x_re
        fs_ref[s, 0, 1:2, :] = x_im
    xin_re, xin_im = _cmul(p_re, p_im, cs_re - w_re + jnp.stack(st_re), cs_im - w_im + jnp.stack(st_im))
    xin = jnp.concatenate([xin_re, xin_im], axis=2).reshape(nrow, S5_W)

    p_re, p_im, q_re, q_im = (tab8_ref[1, i] for i in range(4))
    a_re, a_im = tab1_ref[1, 0:1, :], tab1_ref[1, 1:2, :]
    w_re, w_im = _cmul(p_re, p_im, eb[:, :, :ts], eb[:, :, ts:])
    sf_re, sf_im = suffix(w_re), suffix(w_im)
    z_re, z_im = [None] * groups, [None] * groups
    for s in range(nseq):
        x_re, x_im = x0_ref[s, 1, 0:1, :], x0_ref[s, 1, 1:2, :]
        for b in reversed(range(nb)):
            g = s * nb + b
            z_re[g], z_im[g] = _cmul(a_re, a_im, x_re, x_im)
            x_re = sf_re[g, 0:1, :] + z_re[g]
            x_im = sf_im[g, 0:1, :] + z_im[g]
        fs_ref[s, 1, 0:1, :] = x_re
        fs_ref[s, 1, 1:2, :] = x_im
    xnx_re, xnx_im = _cmul(q_re, q_im, sf_re - w_re + jnp.stack(z_re), sf_im - w_im + jnp.stack(z_im))
    xnx = jnp.concatenate([xnx_re, xnx_im], axis=2).reshape(nrow, S5_W)

    y8 = (jnp.dot(u8, w_ref[W_M], preferred_element_type=F32)
          + _dot_nt(xin, w_ref[W_CF]) + _dot_nt(xnx, w_ref[W_CB]))
    for t in range(S5_T):
        y_ref[pl.ds(t, nrow, stride=S5_T), :] = y8[:, t * LANE:(t + 1) * LANE]


def _s5_scan(zs, row0, x0, x0_block, x0_idx, layer, wmat, tab8, tab1, nseq, seq):
    n = nseq * seq
    rblk = row0 // n
    return pl.pallas_call(
        functools.partial(_s5_scan_kernel, nseq=nseq, nb=seq // CHUNK),
        grid=(S5_TILES,),
        in_specs=[
            pl.BlockSpec((n, LANE), lambda j: (rblk, ZS_U // LANE + j)),
            pl.BlockSpec(x0_block, x0_idx),
            pl.BlockSpec((None, None, 5, S5_ROW, S5_W), lambda j: (layer, j, 0, 0, 0)),
            pl.BlockSpec((None, None, 2, 4, S5_R, S5_TILE_STATE), lambda j: (layer, j, 0, 0, 0, 0)),
            pl.BlockSpec((None, None, 2, 2, S5_TILE_STATE), lambda j: (layer, j, 0, 0, 0)),
        ],
        out_specs=[
            pl.BlockSpec((n, LANE), lambda j: (0, j)),
            pl.BlockSpec((nseq, 2, 2, S5_TILE_STATE), lambda j: (0, 0, 0, j)),
        ],
        out_shape=[
            jax.ShapeDtypeStruct((n, S5_WIDTH), F32),
            jax.ShapeDtypeStruct((nseq, 2, 2, S5_NSTATE), F32),
        ],
        scratch_shapes=[pltpu.VMEM((n, LANE), F32)],
        compiler_params=_params(("parallel",)),
        name="s5_scan",
    )(zs, x0, wmat, tab8, tab1)


def _merge_kernel(x_ref, mod_ref, nw_ref, oa_ref, ob_ref, ys_ref, sg_ref, wglu_ref, bglu_ref, wmg_ref,
                  wa_ref, wb_ref, wc_ref, wout_ref, y_ref, wmg_s):
    @pl.when(pl.program_id(0) == 0)
    def _():
        _pack_transposed(wmg_ref.at[0], wmg_s, 0, 3 * D_MODEL // LANE, 0, None)

    x = x_ref[...]
    mod = mod_ref[0]
    h = _mod_rmsnorm(x, nw_ref[...], mod).astype(BF16)
    zg = _dot(jax.nn.gelu(ys_ref[...]), wglu_ref[...]) + bglu_ref[...]
    oc = (zg[:, :S5_WIDTH] * jax.nn.sigmoid(zg[:, S5_WIDTH:])
          * jax.nn.silu(sg_ref[...].astype(F32)))
    mixed = None
    for br, (o_br, w_ref) in enumerate(((oa_ref[...], wa_ref), (ob_ref[...], wb_ref), (oc, wc_ref))):
        gate = jax.nn.sigmoid(jnp.dot(h, wmg_s[:, br * D_MODEL:(br + 1) * D_MODEL], preferred_element_type=F32))
        term = gate * _dot(o_br, w_ref[...])
        mixed = term if mixed is None else mixed + term
    y_ref[...] = x + mod[:, 2 * D_MODEL:] * _dot(mixed, wout_ref[...])


def _merge(x2, row0, mod, mod_idx, nw, oa, ob, ys, zs, layer, wglu, bglu, w_in_t, wa, wb, wc, wout):
    n = x2.shape[0]
    tm = ROW_TILE
    blk0 = row0 // tm
    rows = lambda w: pl.BlockSpec((tm, w), lambda i: (i, 0))
    return pl.pallas_call(
        _merge_kernel,
        grid=(n // tm,),
        in_specs=[
            rows(D_MODEL),
            pl.BlockSpec((None, 1, 1, 3 * D_MODEL), lambda i: (layer, mod_idx(i), 0, 0)),
            _layer_spec((1, D_MODEL), layer),
            rows(GLA_WIDTH), rows(MLA_WIDTH), rows(S5_WIDTH),
            pl.BlockSpec((tm, S5_WIDTH), lambda i: (i + blk0, ZS_GATE // S5_WIDTH)),
            pl.BlockSpec((None, S5_WIDTH, 2 * S5_WIDTH), lambda i: (layer, 0, 0)),
            pl.BlockSpec((None, 1, 2 * S5_WIDTH), lambda i: (layer, 0, 0)),
            pl.BlockSpec((pl.Element(1), pl.Element(3 * D_MODEL), pl.Element(D_MODEL)),
                         lambda i: (layer, MERGE_COL, 0), pipeline_mode=pl.Buffered(1)),
            _layer_spec((GLA_WIDTH, D_MODEL), layer),
            _layer_spec((MLA_WIDTH, D_MODEL), layer),
            _layer_spec((S5_WIDTH, D_MODEL), layer),
            _layer_spec((D_MODEL, D_MODEL), layer),
        ],
        out_specs=rows(D_MODEL),
        out_shape=jax.ShapeDtypeStruct((n, D_MODEL), F32),
        scratch_shapes=[pltpu.VMEM((D_MODEL, 3 * D_MODEL), BF16)],
        compiler_params=_params(("arbitrary",)),
        name="merge",
    )(x2, mod, nw, oa, ob, ys, zs, wglu, bglu, w_in_t, wa, wb, wc, wout)


def _mla_lane_of_dim():
    half = MLA_ROPE // 2
    first_gap = ROPE_SHIFT - half
    lane = np.zeros(MLA_QK, np.int32)
    for j in range(MLA_NOPE):
        lane[j] = half + j if j < first_gap else 2 * half + j
    for r in range(half):
        lane[MLA_NOPE + r] = r
        lane[MLA_NOPE + half + r] = ROPE_SHIFT + r
    return lane


MLA_LANE_OF_DIM = _mla_lane_of_dim()


def _place_heads(w, heads, lane_of_dim):
    width = len(lane_of_dim)
    src = np.zeros(heads * HEAD_PAD, np.int32)
    used = np.zeros(heads * HEAD_PAD, bool)
    for h in range(heads):
        src[h * HEAD_PAD + lane_of_dim] = h * width + np.arange(width)
        used[h * HEAD_PAD + lane_of_dim] = True
    return jnp.where(jnp.asarray(used), jnp.take(w, jnp.asarray(src), axis=-1), 0.0)


def _rope_tables(n_tok):
    rows = n_tok // GRID_W
    r = jnp.repeat(jnp.arange(rows, dtype=F32), GRID_W)
    col = jnp.tile(jnp.arange(GRID_W, dtype=F32), rows)
    n_freq = MLA_ROPE // 4
    inv = ROPE_THETA ** (-jnp.arange(n_freq, dtype=F32) / n_freq)
    ang = jnp.concatenate([r[:, None] * inv, col[:, None] * inv], axis=-1)
    cos, sin = jnp.cos(ang), jnp.sin(ang)
    ones = jnp.ones((n_tok, MLA_NOPE), F32)
    c = _place_heads(jnp.concatenate([ones, cos, cos], axis=1), 1, MLA_LANE_OF_DIM)
    s = _place_heads(jnp.concatenate([0.0 * ones, -sin, sin], axis=1), 1, MLA_LANE_OF_DIM)
    return jnp.stack([c, s])


def kernel(x_prompt, x_sample, c, c_ctx, cache_mla_ckv, cache_mla_krope, state_gla, state_s5,
           norm_w, w_ada, b_ada, w_in, gla_w_a2, gla_b_a, gla_o_norm,
           mla_q_norm, mla_w_uq, mla_kv_norm, mla_w_uk, mla_w_uv, mla_qh_norm, mla_kh_norm,
           s5_a_re, s5_a_im, s5_log_dt, s5_b_re, s5_b_im, s5_c_re, s5_c_im, s5_d, s5_w_glu, s5_b_glu,
           w_bo_gla, w_bo_mla, w_bo_s5, w_out):
    bsz, seq, _ = x_prompt.shape
    dbsz, dseq, _ = x_sample.shape
    ctx_row = 8 - 1
    assert dbsz <= ctx_row and (bsz * seq) % ROW_TILE == 0 and dseq % ROW_TILE == 0

    cond8 = jnp.zeros((8, D_MODEL), F32).at[0:dbsz].set(c).at[ctx_row].set(c_ctx)
    ada = _ada(cond8, w_ada, b_ada)

    vec = lambda a: a.reshape(DEPTH, 2, 1, S5_NSTATE)
    ldt = jnp.repeat(s5_log_dt[..., None], S5_STATE, axis=-1)
    rows_gp = lambda t: t.reshape(DEPTH, S5_TILES, LANE, S5_STATE)
    bt = lambda b: rows_gp(b.transpose(0, 1, 3, 2))
    wmat, tab8, tab1 = _s5_prep(vec(s5_a_re), vec(s5_a_im), vec(ldt), bt(s5_b_re), bt(s5_b_im),
                                rows_gp(s5_c_re), rows_gp(s5_c_im), s5_d.reshape(DEPTH, 1, S5_WIDTH))
    wglu = s5_w_glu.astype(BF16)
    bglu = s5_b_glu.reshape(DEPTH, 1, 2 * S5_WIDTH)

    wuq = _place_heads(mla_w_uq, MLA_HEADS, MLA_LANE_OF_DIM).astype(BF16)
    wuk = _place_heads(mla_w_uk, MLA_HEADS, MLA_LANE_OF_DIM[:MLA_NOPE]).astype(BF16)
    wuv = mla_w_uv.astype(BF16)
    qhn = _place_heads(mla_qh_norm, 1, MLA_LANE_OF_DIM)
    khn = _place_heads(mla_kh_norm, 1, MLA_LANE_OF_DIM)
    e_np = np.zeros((LANE, MLA_HEADS * HEAD_PAD), np.float32)
    for h in range(MLA_HEADS):
        for i in range(MLA_ROPE):
            e_np[i, h * HEAD_PAD + MLA_LANE_OF_DIM[MLA_NOPE + i]] = 1.0
    e_place = jnp.asarray(e_np, BF16)
    rope_tab = _rope_tables(dseq)
    ckr_pad = jnp.pad(cache_mla_krope, ((0, 0), (0, 0), (0, 0), (0, LANE - MLA_ROPE)))

    zrow = lambda n: jnp.zeros((DEPTH, n, GLA_QK), F32)
    waf = jnp.concatenate([gla_w_a2[:, 0], zrow(LANE - GLA_RANK)], axis=1).astype(BF16)
    wab = jnp.concatenate([zrow(GLA_RANK), gla_w_a2[:, 1], zrow(LANE - 2 * GLA_RANK)], axis=1).astype(BF16)
    sgla = state_gla.reshape(dbsz, DEPTH, 2, GLA_QK, GLA_DV)
    ss5 = state_s5.reshape(dbsz, DEPTH, 2, 2, S5_NSTATE)
    zero_s5 = jnp.zeros((bsz, 2, 2, S5_NSTATE), F32)

    hp = x_prompt.reshape(bsz * seq, D_MODEL)
    hs = x_sample.reshape(dbsz * dseq, D_MODEL)
    ckv_l, krope_l, gla_l, s5_l = [], [], [], []
    w_in_t = jnp.swapaxes(w_in, 1, 2)
    mod = ada.reshape(DEPTH, 8, 1, 3 * D_MODEL)
    nw = norm_w.reshape(DEPTH, 1, D_MODEL)
    row = lambda t: t.reshape(DEPTH, 1, -1)
    mla_w = (row(mla_q_norm), wuq, row(mla_kv_norm), wuk, wuv, row(qhn), row(khn), e_place)
    wbo = (w_bo_gla.astype(BF16), w_bo_mla.astype(BF16), w_bo_s5.astype(BF16))
    wout = w_out.astype(BF16)
    onorm = gla_o_norm.reshape(DEPTH, 1, GLA_DV)
    for l in range(DEPTH):
        p_rows, p_blocks, blocks_per_seq = bsz * seq, bsz * seq // ROW_TILE, dseq // ROW_TILE
        mod_idx = lambda i: jnp.where(i < p_blocks, ctx_row, (i - p_blocks) // blocks_per_seq)
        zg, zm, zs = _in_proj(hp, hs, mod, mod_idx, nw, w_in_t, l)

        def mixers(x2, row0, nb, n, ctx):
            if ctx:
                gctx = sgla
                x0, x0_blk = ss5, (nb, None, 2, 2, S5_TILE_STATE)
                x0_idx = lambda j: (0, l, 0, 0, j)
                mctx, rt = (cache_mla_ckv, ckr_pad), rope_tab
            else:
                gctx = None
                x0, x0_blk = zero_s5, (nb, 2, 2, S5_TILE_STATE)
                x0_idx = lambda j: (0, 0, 0, j)
                mctx, rt = None, None
            oa, st_gla = _gla(zg, row0, gctx, l, waf, wab, gla_b_a, onorm, nb, n)
            ob, ckv = _mla(zm, row0, mctx, l, mla_w, rt, nb, n)
            y_ssm, st_s5 = _s5_scan(zs, row0, x0, x0_blk, x0_idx, l, wmat, tab8, tab1, nb, n)
            grp_mod_idx = lambda i: mod_idx(i + row0 // ROW_TILE)
            y = _merge(x2, row0, mod, grp_mod_idx, nw, oa, ob, y_ssm, zs, l, wglu, bglu, w_in_t, *wbo, wout)
            return y, ckv, st_gla, st_s5

        hp_next, ckv_p, st_gla_p, st_s5_p = mixers(hp, 0, bsz, seq, False)
        hs = mixers(hs, p_rows, dbsz, dseq, True)[0]
        hp = hp_next
        ckv_l.append(ckv_p.reshape(bsz, seq, MLA_KV_LORA))
        krope_l.append(zm[:p_rows, ZM_KR:ZM_KR + MLA_ROPE].astype(F32).reshape(bsz, seq, MLA_ROPE))
        gla_l.append(st_gla_p.reshape(bsz, 2, GLA_HEADS, GLA_DK, GLA_DV))
        s5_l.append(st_s5_p.reshape(bsz, 2, 2, S5_GROUPS, S5_STATE))

    return (hp.reshape(bsz, seq, D_MODEL), hs.reshape(dbsz, dseq, D_MODEL),
            jnp.stack(ckv_l, axis=1), jnp.stack(krope_l, axis=1),
            jnp.stack(gla_l, axis=1), jnp.stack(s5_l, axis=1))
```

```python
import functools

import jax
import jax.numpy as jnp
import numpy as np
from jax import lax
from jax.experimental import pallas as pl
from jax.experimental.pallas import tpu as pltpu

F32 = jnp.float32
BF16 = jnp.bfloat16

EPS = 1e-6
D_MODEL = 1024
DEPTH = 2
GRID_W = 64
ROPE_THETA = 10000.0
GLA_HEADS = 4
GLA_DK = 64
GLA_DV = 128
GLA_RANK = 16
GLA_GATE_NORM = 16.0
GLA_QK = GLA_HEADS * GLA_DK
GLA_WIDTH = GLA_HEADS * GLA_DV
MLA_HEADS = 4
MLA_Q_LORA = 384
MLA_KV_LORA = 256
MLA_NOPE = 64
MLA_ROPE = 32
MLA_QK = MLA_NOPE + MLA_ROPE
MLA_DV = 128
MLA_WIDTH = MLA_HEADS * MLA_DV
S5_WIDTH = 512
S5_GROUP = 16
S5_GROUPS = 32
S5_STATE = 64
S5_NSTATE = S5_GROUPS * S5_STATE

LANE = 128
HEAD_PAD = LANE
ROPE_SHIFT = LANE // 2
CHUNK = 64
GLA_STEP = 256
GLA_SEQS_PER_STEP = 2
GLA_ROWS_PER_STEP = 1024
S5_TILES = S5_WIDTH // LANE
S5_TILE_STATE = S5_NSTATE // S5_TILES
ROW_TILE = 512
Q_TILE = 512
PROJ_TILE = 256
VMEM_LIMIT = 56 * 1024 * 1024

MERGE_COL = 3776
ZG_Q, ZG_K, ZG_V, ZG_A, ZG_GATE, ZG_W = 0, 256, 512, 1024, 1152, 1664
ZM_Q, ZM_KV, ZM_KR, ZM_GATE, ZM_W = 0, 384, 640, 768, 1280
ZS_U, ZS_GATE, ZS_W = 0, 512, 1024
ZG_BASE, ZM_BASE, ZS_BASE, PACK_W = 0, ZG_W, ZG_W + ZM_W, ZG_W + ZM_W + ZS_W
IN_PIECES = (
    (0, 8, ZG_BASE + ZG_Q, None),
    (1024, 1, ZG_BASE + ZG_A, 2 * GLA_RANK),
    (1056, 4, ZG_BASE + ZG_GATE, None),
    (1568, 5, ZM_BASE + ZM_Q, None),
    (2208, 1, ZM_BASE + ZM_KR, MLA_ROPE),
    (2240, 4, ZM_BASE + ZM_GATE, None),
    (2752, 8, ZS_BASE + ZS_U, None),
)


def _dot(a, b):
    return jnp.dot(a.astype(BF16), b.astype(BF16), preferred_element_type=F32)


def _dot_nt(a, b):
    return lax.dot_general(a.astype(BF16), b.astype(BF16), (((1,), (1,)), ((), ())),
                           preferred_element_type=F32)


def _split_bf16(x, parts):
    out = []
    r = x
    for _ in range(parts):
        p = r.astype(BF16)
        out.append(p)
        r = r - p.astype(F32)
    return out


def _layer_spec(shape, layer):
    return pl.BlockSpec((None,) + tuple(shape), lambda *_: (layer,) + (0,) * len(shape))


def _params(sem):
    return pltpu.CompilerParams(dimension_semantics=sem, vmem_limit_bytes=VMEM_LIMIT)


def _ada_kernel(c_ref, w_ref, b_ref, o_ref):
    s = jax.nn.silu(c_ref[...])
    o_ref[...] = _dot(s, w_ref[...]) + b_ref[...]


def _ada(cond8, w_ada, b_ada):
    tn = 1024
    return pl.pallas_call(
        _ada_kernel,
        grid=(DEPTH, 3 * D_MODEL // tn),
        in_specs=[
            pl.BlockSpec((8, D_MODEL), lambda l, n: (0, 0)),
            pl.BlockSpec((None, D_MODEL, tn), lambda l, n: (l, 0, n)),
            pl.BlockSpec((None, 1, tn), lambda l, n: (l, 0, n)),
        ],
        out_specs=pl.BlockSpec((None, 8, tn), lambda l, n: (l, 0, n)),
        out_shape=jax.ShapeDtypeStruct((DEPTH, 8, 3 * D_MODEL), F32),
        compiler_params=_params(("parallel", "parallel")),
        name="ada",
    )(cond8, w_ada, b_ada.reshape(DEPTH, 1, 3 * D_MODEL))


def _mod_rmsnorm(x, nw, mod):
    ms = jnp.mean(x * x, axis=-1, keepdims=True)
    y = x * lax.rsqrt(ms + EPS) * nw
    return y * (1.0 + mod[:, D_MODEL:2 * D_MODEL]) + mod[:, 0:D_MODEL]


def _pack_transposed(w_ref, wb_s, src, tiles, dst, keep):
    lane = lax.broadcasted_iota(jnp.int32, (D_MODEL, LANE), 1)
    for t in range(tiles):
        blk = w_ref[src + t * LANE:src + (t + 1) * LANE, :].T
        if keep is not None:
            blk = jnp.where(lane < keep, blk, 0.0)
        wb_s[:, dst + t * LANE:dst + (t + 1) * LANE] = blk.astype(BF16)


def _in_proj_kernel(xp_ref, xs_ref, mod_ref, nw_ref, w_ref, zg_ref, zm_ref, zs_ref, wb_s, *, p_blocks):
    i = pl.program_id(0)

    @pl.when(i == 0)
    def _():
        for src, tiles, dst, keep in IN_PIECES:
            _pack_transposed(w_ref, wb_s, src, tiles, dst, keep)

    x = jnp.where(i < p_blocks, xp_ref[...], xs_ref[...])
    h = _mod_rmsnorm(x, nw_ref[...], mod_ref[0]).astype(BF16)
    z = jnp.dot(h, wb_s[...], preferred_element_type=F32)
    zg_ref[...] = z[:, ZG_BASE:ZG_BASE + ZG_W].astype(BF16)
    zm_ref[...] = z[:, ZM_BASE:ZM_BASE + ZM_W].astype(BF16)
    zs_ref[...] = z[:, ZS_BASE:ZS_BASE + ZS_W].astype(BF16)


def _two_group_rows(width, p_blocks):
    tm = ROW_TILE
    return (pl.BlockSpec((tm, width), lambda i: (jnp.minimum(i, p_blocks - 1), 0)),
            pl.BlockSpec((tm, width), lambda i: (jnp.maximum(i - p_blocks, 0), 0)))


def _in_proj(xp, xs, mod, mod_idx, nw, w_in_t, layer):
    tm = ROW_TILE
    p_blocks = xp.shape[0] // tm
    n = xp.shape[0] + xs.shape[0]
    return pl.pallas_call(
        functools.partial(_in_proj_kernel, p_blocks=p_blocks),
        grid=(n // tm,),
        in_specs=[
            *_two_group_rows(D_MODEL, p_blocks),
            pl.BlockSpec((None, 1, 1, 3 * D_MODEL), lambda i: (layer, mod_idx(i), 0, 0)),
            _layer_spec((1, D_MODEL), layer),
            pl.BlockSpec((None, MERGE_COL, D_MODEL), lambda i: (layer, 0, 0), pipeline_mode=pl.Buffered(1)),
        ],
        out_specs=[
            pl.BlockSpec((tm, ZG_W), lambda i: (i, 0)),
            pl.BlockSpec((tm, ZM_W), lambda i: (i, 0)),
            pl.BlockSpec((tm, ZS_W), lambda i: (i, 0)),
        ],
        out_shape=[
            jax.ShapeDtypeStruct((n, ZG_W), BF16),
            jax.ShapeDtypeStruct((n, ZM_W), BF16),
            jax.ShapeDtypeStruct((n, ZS_W), BF16),
        ],
        scratch_shapes=[pltpu.VMEM((D_MODEL, PACK_W), BF16)],
        compiler_params=_params(("arbitrary",)),
        name="in_proj",
    )(xp, xs, mod, nw, w_in_t)


def _gla_kernel(*refs, nsteps, seq, nseq, has_ctx):
    it = iter(refs)
    zg_ref = next(it)
    s0_ref = next(it) if has_ctx else None
    waf_ref, wab_ref, ba_ref, onorm_ref, o_ref, sfin_ref, la_s, o_s, st_s = (next(it) for _ in range(9))
    chains = [(g, d) for g in range(nseq) for d in (0, 1)]
    inv_norm = 1.0 / GLA_GATE_NORM
    zero_blk = jnp.zeros((GLA_DK, GLA_DV), F32)
    for ch, (g, d) in enumerate(chains):
        if d == 0:
            a_blk = zg_ref[g * seq:(g + 1) * seq, ZG_A:ZG_A + LANE]
        wa_ref = waf_ref if d == 0 else wab_ref
        la_s[ch] = jax.nn.log_sigmoid(_dot(a_blk, wa_ref[...]) + ba_ref[d:d + 1, :]) * inv_norm
        if has_ctx:
            s0 = s0_ref[g, d]
            rows_bd = []
            for h in range(GLA_HEADS):
                sh = s0[h * GLA_DK:(h + 1) * GLA_DK, :]
                rows_bd.append(jnp.concatenate([sh if h2 == h else zero_blk for h2 in range(GLA_HEADS)], axis=1))
            st_s[ch] = jnp.concatenate(rows_bd, axis=0).T
        else:
            st_s[ch] = jnp.zeros((GLA_WIDTH, GLA_QK), F32)

    def iota(shape, axis, shift):
        return lax.shift_right_logical(lax.broadcasted_iota(jnp.int32, shape, axis), shift)

    log_chunk, log_dv = CHUNK.bit_length() - 1, GLA_DV.bit_length() - 1
    row = lax.broadcasted_iota(jnp.int32, (GLA_STEP, GLA_STEP), 0)
    col = lax.broadcasted_iota(jnp.int32, (GLA_STEP, GLA_STEP), 1)
    same_chunk = iota((GLA_STEP, GLA_STEP), 0, log_chunk) == iota((GLA_STEP, GLA_STEP), 1, log_chunk)
    masks = (same_chunk & (row >= col), same_chunk & (row <= col))
    lane_head = iota((GLA_STEP, GLA_QK), 1, log_chunk)
    row_chunk = iota((GLA_STEP, GLA_QK), 0, log_chunk)
    state_blk = iota((GLA_WIDTH, GLA_QK), 0, log_dv) == iota((GLA_WIDTH, GLA_QK), 1, log_chunk)
    qscale = GLA_DK ** -0.5
    nch = GLA_STEP // CHUNK
    n_chain = len(chains)

    def step(i, carry):
        rows, cum = [], []
        for ch, (g, d) in enumerate(chains):
            r0 = pl.multiple_of((i if d == 0 else nsteps - 1 - i) * GLA_STEP, GLA_STEP)
            rows.append((pl.ds(g * seq + r0, GLA_STEP), pl.ds(r0, GLA_STEP)))
            a_hi, a_lo = _split_bf16(la_s[ch, rows[ch][1], :], 2)
            tri = masks[d].astype(BF16)
            cum.append(jnp.dot(tri, a_hi, preferred_element_type=F32)
                       + jnp.dot(tri, a_lo, preferred_element_type=F32))
        blast, v, v_t, qd, kd, kr = [], [], [], [], [], []
        for ch, (g, d) in enumerate(chains):
            edge = CHUNK - 1 if d == 0 else 0
            blast.append([cum[ch][c * CHUNK + edge:c * CHUNK + edge + 1, :] for c in range(nch)])
            bl = jnp.concatenate([jnp.broadcast_to(b, (CHUNK, GLA_QK)) for b in blast[ch]], axis=0)
            zrows = rows[ch][0]
            q = zg_ref[zrows, ZG_Q:ZG_Q + GLA_QK].astype(F32) * qscale
            k = zg_ref[zrows, ZG_K:ZG_K + GLA_QK].astype(F32)
            v.append(zg_ref[zrows, ZG_V:ZG_V + GLA_WIDTH])
            v_t.append(v[ch].astype(F32).T.astype(BF16))
            qd.append(q * jnp.exp(cum[ch]))
            kd.append((k * jnp.exp(-cum[ch])).astype(BF16))
            kr.append(k * jnp.exp(bl - cum[ch]))
        outs = [[] for _ in chains]
        for h in range(GLA_HEADS):
            for ch, (g, d) in enumerate(chains):
                qh = jnp.where(lane_head == h, qd[ch], 0.0)
                att = jnp.where(masks[d], _dot_nt(qh, kd[ch]), 0.0)
                outs[ch].append(_dot(att, v[ch][:, h * GLA_DV:(h + 1) * GLA_DV]))
        s = [st_s[ch] for ch in range(n_chain)]
        inter = [[None] * nch for _ in chains]
        for j in range(nch):
            for ch, (g, d) in enumerate(chains):
                c = j if d == 0 else nch - 1 - j
                inter[ch][c] = _dot_nt(qd[ch][c * CHUNK:(c + 1) * CHUNK, :], s[ch])
                kv_t = jnp.where(state_blk, _dot(v_t[ch], jnp.where(row_chunk == c, kr[ch], 0.0)), 0.0)
                s[ch] = s[ch] * jnp.exp(blast[ch][c]) + kv_t
        for ch in range(n_chain):
            st_s[ch] = s[ch]
            o_s[ch, rows[ch][1], :] = jnp.concatenate(outs[ch], axis=1) + jnp.concatenate(inter[ch], axis=0)
        return carry

    lax.fori_loop(0, nsteps, step, 0)
    onorm = onorm_ref[...]
    for ch, (g, d) in enumerate(chains):
        s_fin = st_s[ch].T
        for h in range(GLA_HEADS):
            sfin_ref[g, d, h * GLA_DK:(h + 1) * GLA_DK, :] = (
                s_fin[h * GLA_DK:(h + 1) * GLA_DK, h * GLA_DV:(h + 1) * GLA_DV])
    for g in range(nseq):
        srows = slice(g * seq, (g + 1) * seq)
        o = o_s[2 * g] + o_s[2 * g + 1]
        gate = zg_ref[srows, ZG_GATE:ZG_GATE + GLA_WIDTH].astype(F32)
        for h in range(GLA_HEADS):
            vs = slice(h * GLA_DV, (h + 1) * GLA_DV)
            oh = o[:, vs]
            ms = jnp.mean(oh * oh, axis=-1, keepdims=True)
            o_ref[srows, vs] = oh * lax.rsqrt(ms + EPS) * onorm * jax.nn.silu(gate[:, vs])


def _gla(zg, row0, ctx, layer, waf, wab, ba, onorm, bsz, seq):
    nseq = max(GLA_SEQS_PER_STEP, GLA_ROWS_PER_STEP // seq)
    blk0 = row0 // (nseq * seq)
    in_specs = [pl.BlockSpec((nseq * seq, ZG_W), lambda b: (b + blk0, 0))]
    args = [zg]
    if ctx is not None:
        in_specs.append(pl.BlockSpec((nseq, None, 2, GLA_QK, GLA_DV), lambda b: (b, layer, 0, 0, 0)))
        args.append(ctx)
    in_specs += [
        _layer_spec((LANE, GLA_QK), layer),
        _layer_spec((LANE, GLA_QK), layer),
        _layer_spec((2, GLA_QK), layer),
        _layer_spec((1, GLA_DV), layer),
    ]
    return pl.pallas_call(
        functools.partial(_gla_kernel, nsteps=seq // GLA_STEP, seq=seq, nseq=nseq, has_ctx=ctx is not None),
        grid=(bsz // nseq,),
        in_specs=in_specs,
        out_specs=[
            pl.BlockSpec((nseq * seq, GLA_WIDTH), lambda b: (b, 0)),
            pl.BlockSpec((nseq, 2, GLA_QK, GLA_DV), lambda b: (b, 0, 0, 0)),
        ],
        out_shape=[
            jax.ShapeDtypeStruct((bsz * seq, GLA_WIDTH), F32),
            jax.ShapeDtypeStruct((bsz, 2, GLA_QK, GLA_DV), F32),
        ],
        scratch_shapes=[
            pltpu.VMEM((2 * nseq, seq, GLA_QK), F32),
            pltpu.VMEM((2 * nseq, seq, GLA_WIDTH), F32),
            pltpu.VMEM((2 * nseq, GLA_WIDTH, GLA_QK), F32),
        ],
        compiler_params=_params(("parallel",)),
        name="gla",
    )(*args, waf, wab, ba, onorm)


def _rms(x, w):
    ms = jnp.mean(x * x, axis=-1, keepdims=True)
    return x * lax.rsqrt(ms + EPS) * w


def _head_sums_mxu(x):
    width = x.shape[-1]
    shift = HEAD_PAD.bit_length() - 1
    gi = lax.shift_right_logical(lax.broadcasted_iota(jnp.int32, (width, width), 0), shift)
    gj = lax.shift_right_logical(lax.broadcasted_iota(jnp.int32, (width, width), 1), shift)
    return _dot(x * x, jnp.where(gi == gj, 1.0, 0.0))


def _head_norm(x, w, rope, on_mxu):
    sums = _head_sums_mxu(x) if on_mxu else None
    outs = []
    for h in range(MLA_HEADS):
        hs = slice(h * HEAD_PAD, (h + 1) * HEAD_PAD)
        xh = x[:, hs]
        ss = sums[:, hs] if on_mxu else jnp.sum(xh * xh, axis=-1, keepdims=True)
        yh = xh * lax.rsqrt(ss * (1.0 / MLA_QK) + EPS) * w
        if rope is not None:
            c, s = rope
            yh = yh * c + pltpu.roll(yh, ROPE_SHIFT, 1) * s
        outs.append(yh)
    return outs


def _place_rope_key(kr, e):
    return sum(jnp.dot(p, e, preferred_element_type=F32) for p in _split_bf16(kr, 3))


def _mla_kernel(*refs, seq, n_ctx, use_rope):
    it = iter(refs)
    zm_ref = next(it)
    if n_ctx:
        cckv_ref, ckr_ref = next(it), next(it)
    qn_ref, wuq_ref, kvn_ref, wuk_ref, wuv_ref, qhn_ref, khn_ref, e_ref = (next(it) for _ in range(8))
    rope_ref = next(it) if use_rope else None
    o_ref, ckv_ref = next(it), next(it)
    q_s, k_s, v_s = next(it), next(it), next(it)

    qscale = MLA_QK ** -0.5
    heads = [slice(h * HEAD_PAD, (h + 1) * HEAD_PAD) for h in range(MLA_HEADS)]

    def keys_values(ckv, k_rope_placed, rope, k_rows):
        kh = _head_norm(_dot(ckv, wuk_ref[...]) + k_rope_placed, khn_ref[...], rope, False)
        for h, hs in enumerate(heads):
            k_s[k_rows, hs] = kh[h].astype(BF16)
        v_s[k_rows, :] = _dot(ckv, wuv_ref[...]).astype(BF16)

    def latent_tile(i, carry):
        r0 = pl.multiple_of(i * PROJ_TILE, PROJ_TILE)
        rows = pl.ds(r0, PROJ_TILE)
        rope = (rope_ref[0, rows, :], rope_ref[1, rows, :]) if use_rope else None
        ckv = _rms(zm_ref[rows, ZM_KV:ZM_KV + MLA_KV_LORA].astype(F32), kvn_ref[...])
        ckv_ref[rows, :] = ckv
        k_pe = jnp.dot(zm_ref[rows, ZM_KR:ZM_KR + LANE], e_ref[...], preferred_element_type=F32)
        keys_values(ckv, k_pe, rope, pl.ds(n_ctx + r0, PROJ_TILE))
        cq = _rms(zm_ref[rows, ZM_Q:ZM_Q + MLA_Q_LORA].astype(F32), qn_ref[...])
        qh = _head_norm(_dot(cq, wuq_ref[...]), qhn_ref[...], rope, True)
        for h, hs in enumerate(heads):
            q_s[rows, hs] = (qh[h] * qscale).astype(BF16)
        return carry

    lax.fori_loop(0, seq // PROJ_TILE, latent_tile, 0)

    def context_tile(i, carry):
        rows = pl.ds(pl.multiple_of(i * PROJ_TILE, PROJ_TILE), PROJ_TILE)
        keys_values(cckv_ref[rows, :], _place_rope_key(ckr_ref[rows, :], e_ref[...]), None, rows)
        return carry

    if n_ctx:
        lax.fori_loop(0, n_ctx // PROJ_TILE, context_tile, 0)

    q_tile = min(seq, Q_TILE)

    def q_block(i, carry):
        rows = pl.ds(pl.multiple_of(i * q_tile, q_tile), q_tile)
        gate = zm_ref[rows, ZM_GATE:ZM_GATE + MLA_WIDTH].astype(F32)
        s = [lax.dot_general(q_s[rows, hs], k_s[:, hs], (((1,), (1,)), ((), ())),
                             preferred_element_type=F32) for hs in heads]
        p, l = [], []
        for h in range(MLA_HEADS):
            e = jnp.exp(s[h] - jnp.max(s[h], axis=-1, keepdims=True))
            l.append(jnp.sum(e, axis=-1, keepdims=True))
            p.append(e.astype(BF16))
        for h, hs in enumerate(heads):
            o = jnp.dot(p[h], v_s[:, hs], preferred_element_type=F32) / l[h]
            o_ref[rows, hs] = o * jax.nn.silu(gate[:, hs])
        return carry

    lax.fori_loop(0, seq // q_tile, q_block, 0)


def _mla(zm, row0, ctx, layer, w, rope_tab, bsz, seq):
    n_ctx = 0 if ctx is None else ctx[0].shape[-2]
    blk0 = row0 // seq
    in_specs = [pl.BlockSpec((seq, ZM_W), lambda b: (b + blk0, 0))]
    args = [zm]
    if ctx is not None:
        cckv, ckr = ctx
        in_specs += [
            pl.BlockSpec((None, None, n_ctx, MLA_KV_LORA), lambda b: (b, layer, 0, 0)),
            pl.BlockSpec((None, None, n_ctx, LANE), lambda b: (b, layer, 0, 0)),
        ]
        args += [cckv, ckr]
    in_specs += [
        _layer_spec((1, MLA_Q_LORA), layer),
        _layer_spec((MLA_Q_LORA, MLA_HEADS * HEAD_PAD), layer),
        _layer_spec((1, MLA_KV_LORA), layer),
        _layer_spec((MLA_KV_LORA, MLA_HEADS * HEAD_PAD), layer),
        _layer_spec((MLA_KV_LORA, MLA_WIDTH), layer),
        _layer_spec((1, HEAD_PAD), layer),
        _layer_spec((1, HEAD_PAD), layer),
        pl.BlockSpec((LANE, MLA_HEADS * HEAD_PAD), lambda b: (0, 0)),
    ]
    args += list(w)
    if rope_tab is not None:
        in_specs.append(pl.BlockSpec((2, seq, HEAD_PAD), lambda b: (0, 0, 0)))
        args.append(rope_tab)
    return pl.pallas_call(
        functools.partial(_mla_kernel, seq=seq, n_ctx=n_ctx, use_rope=rope_tab is not None),
        grid=(bsz,),
        in_specs=in_specs,
        out_specs=[
            pl.BlockSpec((seq, MLA_WIDTH), lambda b: (b, 0)),
            pl.BlockSpec((seq, MLA_KV_LORA), lambda b: (b, 0)),
        ],
        out_shape=[
            jax.ShapeDtypeStruct((bsz * seq, MLA_WIDTH), F32),
            jax.ShapeDtypeStruct((bsz * seq, MLA_KV_LORA), F32),
        ],
        scratch_shapes=[
            pltpu.VMEM((seq, MLA_HEADS * HEAD_PAD), BF16),
            pltpu.VMEM((n_ctx + seq, MLA_HEADS * HEAD_PAD), BF16),
            pltpu.VMEM((n_ctx + seq, MLA_WIDTH), BF16),
        ],
        compiler_params=_params(("parallel",)),
        name="mla",
    )(*args)


S5_T = 8
S5_R = CHUNK // S5_T
S5_SUB_CH = 64
S5_SUBS = LANE // S5_SUB_CH
S5_SUB_STATE = S5_TILE_STATE // S5_SUBS
S5_ROW = S5_T * S5_SUB_CH
S5_W = 2 * S5_SUB_STATE
W_M, W_SF, W_SB, W_CF, W_CB = range(5)


def _cmul(ar, ai, br, bi):
    return ar * br - ai * bi, ar * bi + ai * br


def _s5_prep_kernel(are_ref, aim_ref, ldt_ref, bre_ref, bim_ref, cre_ref, cim_ref, d_ref,
                    w_ref, tab8_ref, tab1_ref):
    gr = lax.shift_right_logical(lax.broadcasted_iota(jnp.int32, (S5_SUB_CH, S5_SUB_STATE), 0),
                                 S5_GROUP.bit_length() - 1)
    gc = lax.shift_right_logical(lax.broadcasted_iota(jnp.int32, (S5_SUB_CH, S5_SUB_STATE), 1),
                                 S5_STATE.bit_length() - 1)

    def spread(ref, h):
        x = ref[h * S5_SUB_CH:(h + 1) * S5_SUB_CH, :]
        return jnp.where(gr == gc, jnp.concatenate([x] * (S5_SUB_CH // S5_GROUP), axis=1), 0.0)

    row = lax.broadcasted_iota(jnp.int32, (S5_SUB_CH, S5_SUB_CH), 0)
    col = lax.broadcasted_iota(jnp.int32, (S5_SUB_CH, S5_SUB_CH), 1)
    taps = [[[], []] for _ in range(S5_SUBS)]
    for d in (0, 1):
        a_re, a_im = are_ref[d], aim_ref[d]
        dt = jnp.exp(ldt_ref[d])
        lam = a_re * dt
        th = a_im * dt
        mag = jnp.exp(lam)
        ab_re = mag * jnp.cos(th)
        ab_im = mag * jnp.sin(th)
        den = a_re * a_re + a_im * a_im
        n_re = ab_re - 1.0
        cf_re = (n_re * a_re + ab_im * a_im) / den
        cf_im = (ab_im * a_re - n_re * a_im) / den
        k = lax.broadcasted_iota(jnp.int32, (2 * S5_T, S5_TILE_STATE), 0).astype(F32)
        pmag = jnp.exp(k * lam)
        pw_re = pmag * jnp.cos(k * th)
        pw_im = pmag * jnp.sin(k * th)
        for h in range(S5_SUBS):
            ss = slice(h * S5_SUB_STATE, (h + 1) * S5_SUB_STATE)
            c_re, c_im = spread(cre_ref, h), spread(cim_ref, h)
            c_cat = jnp.concatenate([c_re, c_im], axis=1).astype(BF16)
            bp_re, bp_im = _cmul(spread(bre_ref, h), spread(bim_ref, h), cf_re[:, ss], cf_im[:, ss])
            for p in range(S5_T + 1):
                ar, ai = pw_re[p:p + 1, ss], pw_im[p:p + 1, ss]
                t_in = S5_T - 1 - p if d == 0 else p
                t_out = p - 1 if d == 0 else S5_T - p
                if p < S5_T:
                    l_re, l_im = _cmul(bp_re, bp_im, ar, ai)
                    w_ref[h, W_SF + d, t_in * S5_SUB_CH:(t_in + 1) * S5_SUB_CH, :] = (
                        jnp.concatenate([l_re, l_im], axis=1).astype(BF16))
                    taps[h][d].append(_dot_nt(jnp.concatenate([l_re, -l_im], axis=1), c_cat))
                if p > 0:
                    v_re, v_im = _cmul(c_re, c_im, ar, ai)
                    w_ref[h, W_CF + d, t_out * S5_SUB_CH:(t_out + 1) * S5_SUB_CH, :] = (
                        jnp.concatenate([v_re, -v_im], axis=1).astype(BF16))
        r = lax.broadcasted_iota(jnp.int32, (S5_R, S5_TILE_STATE), 0).astype(F32) * float(S5_T)
        r1 = r + float(S5_T)
        pm = jnp.exp(r * lam)
        qm = jnp.exp(-(r1 * lam))
        tab8_ref[d, 0] = pm * jnp.cos(r * th)
        tab8_ref[d, 1] = pm * jnp.sin(r * th)
        tab8_ref[d, 2] = qm * jnp.cos(r1 * th)
        tab8_ref[d, 3] = -(qm * jnp.sin(r1 * th))
        mc = jnp.exp(float(CHUNK) * lam)
        tab1_ref[d, 0:1, :] = mc * jnp.cos(float(CHUNK) * th)
        tab1_ref[d, 1:2, :] = mc * jnp.sin(float(CHUNK) * th)
    for h in range(S5_SUBS):
        skip = jnp.where(row == col, d_ref[:, h * S5_SUB_CH:(h + 1) * S5_SUB_CH], 0.0)
        for t in range(S5_T):
            blocks = []
            for t2 in range(S5_T):
                if t < t2:
                    blocks.append(taps[h][0][t2 - t])
                elif t > t2:
                    blocks.append(taps[h][1][t - t2])
                else:
                    blocks.append(taps[h][0][0] + taps[h][1][0] + skip)
            w_ref[h, W_M, t * S5_SUB_CH:(t + 1) * S5_SUB_CH, :] = jnp.concatenate(blocks, axis=1).astype(BF16)


def _s5_prep(a_re, a_im, ldt, b_re, b_im, c_re, c_im, dsk):
    vec = pl.BlockSpec((None, 2, 1, S5_TILE_STATE), lambda l, j: (l, 0, 0, j))
    blk = pl.BlockSpec((None, None, LANE, S5_STATE), lambda l, j: (l, j, 0, 0))
    return pl.pallas_call(
        _s5_prep_kernel,
        grid=(DEPTH, S5_TILES),
        in_specs=[vec, vec, vec, blk, blk, blk, blk,
                  pl.BlockSpec((None, 1, LANE), lambda l, j: (l, 0, j))],
        out_specs=[
            pl.BlockSpec((None, None, S5_SUBS, 5, S5_ROW, S5_W), lambda l, j: (l, j, 0, 0, 0, 0)),
            pl.BlockSpec((None, None, 2, 4, S5_R, S5_TILE_STATE), lambda l, j: (l, j, 0, 0, 0, 0)),
            pl.BlockSpec((None, None, 2, 2, S5_TILE_STATE), lambda l, j: (l, j, 0, 0, 0)),
        ],
        out_shape=[
            jax.ShapeDtypeStruct((DEPTH, S5_TILES, S5_SUBS, 5, S5_ROW, S5_W), BF16),
            jax.ShapeDtypeStruct((DEPTH, S5_TILES, 2, 4, S5_R, S5_TILE_STATE), F32),
            jax.ShapeDtypeStruct((DEPTH, S5_TILES, 2, 2, S5_TILE_STATE), F32),
        ],
        compiler_params=_params(("parallel", "parallel")),
        name="s5_prep",
    )(a_re, a_im, ldt, b_re, b_im, c_re, c_im, dsk)


def _s5_scan_kernel(u_ref, x0_ref, w_ref, tab8_ref, tab1_ref, y_ref, fs_ref, u_s, *, nseq, nb):
    groups = nseq * nb
    nrow = groups * S5_R
    ts = S5_SUB_STATE
    u_s[...] = u_ref[...].astype(F32)
    tokens = [u_s[pl.ds(t, nrow, stride=S5_T), :] for t in range(S5_T)]
    rowi = lax.broadcasted_iota(jnp.int32, (groups, S5_R, ts), 1)

    def prefix(x):
        for s in (1, 2, 4):
            x = x + jnp.where(rowi >= s, pltpu.roll(x, s, 1), 0.0)
        return x

    def suffix(x):
        for s in (1, 2, 4):
            x = x + jnp.where(rowi < S5_R - s, pltpu.roll(x, S5_R - s, 1), 0.0)
        return x

    y_sub = []
    for h in range(S5_SUBS):
        ch = slice(h * S5_SUB_CH, (h + 1) * S5_SUB_CH)
        ss = slice(h * ts, (h + 1) * ts)
        u8 = jnp.concatenate([tok[:, ch] for tok in tokens], axis=1).astype(BF16)
        ef = jnp.dot(u8, w_ref[h, W_SF], preferred_element_type=F32).reshape(groups, S5_R, S5_W)
        eb = jnp.dot(u8, w_ref[h, W_SB], preferred_element_type=F32).reshape(groups, S5_R, S5_W)

        p_re, p_im, q_re, q_im = (tab8_ref[0, i, :, ss] for i in range(4))
        a_re, a_im = tab1_ref[0, 0:1, ss], tab1_ref[0, 1:2, ss]
        w_re, w_im = _cmul(q_re, q_im, ef[:, :, :ts], ef[:, :, ts:])
        cs_re, cs_im = prefix(w_re), prefix(w_im)
        st_re, st_im = [], []
        for s in range(nseq):
            x_re, x_im = x0_ref[s, 0, 0:1, ss], x0_ref[s, 0, 1:2, ss]
            for b in range(nb):
                g = s * nb + b
                st_re.append(x_re)
                st_im.append(x_im)
                x_re, x_im = _cmul(a_re, a_im, x_re + cs_re[g, S5_R - 1:S5_R, :],
                                   x_im + cs_im[g, S5_R - 1:S5_R, :])
            fs_ref[s, 0, 0:1, ss] = x_re
            fs_ref[s, 0, 1:2, ss] = x_im
        xin_re, xin_im = _cmul(p_re, p_im, cs_re - w_re + jnp.stack(st_re), cs_im - w_im + jnp.stack(st_im))
        xin = jnp.concatenate([xin_re, xin_im], axis=2).reshape(nrow, S5_W)

        p_re, p_im, q_re, q_im = (tab8_ref[1, i, :, ss] for i in range(4))
        a_re, a_im = tab1_ref[1, 0:1, ss], tab1_ref[1, 1:2, ss]
        w_re, w_im = _cmul(p_re, p_im, eb[:, :, :ts], eb[:, :, ts:])
        sf_re, sf_im = suffix(w_re), suffix(w_im)
        z_re, z_im = [None] * groups, [None] * groups
        for s in range(nseq):
            x_re, x_im = x0_ref[s, 1, 0:1, ss], x0_ref[s, 1, 1:2, ss]
            for b in reversed(range(nb)):
                g = s * nb + b
                z_re[g], z_im[g] = _cmul(a_re, a_im, x_re, x_im)
                x_re = sf_re[g, 0:1, :] + z_re[g]
                x_im = sf_im[g, 0:1, :] + z_im[g]
            fs_ref[s, 1, 0:1, ss] = x_re
            fs_ref[s, 1, 1:2, ss] = x_im
        xnx_re, xnx_im = _cmul(q_re, q_im, sf_re - w_re + jnp.stack(z_re), sf_im - w_im + jnp.stack(z_im))
        xnx = jnp.concatenate([xnx_re, xnx_im], axis=2).reshape(nrow, S5_W)

        y_sub.append(jnp.dot(u8, w_ref[h, W_M], preferred_element_type=F32)
                     + _dot_nt(xin, w_ref[h, W_CF]) + _dot_nt(xnx, w_ref[h, W_CB]))
    for t in range(S5_T):
        tc = slice(t * S5_SUB_CH, (t + 1) * S5_SUB_CH)
        y_ref[pl.ds(t, nrow, stride=S5_T), :] = jnp.concatenate([y[:, tc] for y in y_sub], axis=1)


def _s5_scan(zs, row0, x0, x0_block, x0_idx, layer, wmat, tab8, tab1, nseq, seq):
    n = nseq * seq
    rblk = row0 // n
    return pl.pallas_call(
        functools.partial(_s5_scan_kernel, nseq=nseq, nb=seq // CHUNK),
        grid=(S5_TILES,),
        in_specs=[
            pl.BlockSpec((n, LANE), lambda j: (rblk, ZS_U // LANE + j)),
            pl.BlockSpec(x0_block, x0_idx),
            pl.BlockSpec((None, None, S5_SUBS, 5, S5_ROW, S5_W), lambda j: (layer, j, 0, 0, 0, 0)),
            pl.BlockSpec((None, None, 2, 4, S5_R, S5_TILE_STATE), lambda j: (layer, j, 0, 0, 0, 0)),
            pl.BlockSpec((None, None, 2, 2, S5_TILE_STATE), lambda j: (layer, j, 0, 0, 0)),
        ],
        out_specs=[
            pl.BlockSpec((n, LANE), lambda j: (0, j)),
            pl.BlockSpec((nseq, 2, 2, S5_TILE_STATE), lambda j: (0, 0, 0, j)),
        ],
        out_shape=[
            jax.ShapeDtypeStruct((n, S5_WIDTH), F32),
            jax.ShapeDtypeStruct((nseq, 2, 2, S5_NSTATE), F32),
        ],
        scratch_shapes=[pltpu.VMEM((n, LANE), F32)],
        compiler_params=_params(("parallel",)),
        name="s5_scan",
    )(zs, x0, wmat, tab8, tab1)


def _merge_kernel(x_ref, mod_ref, nw_ref, oa_ref, ob_ref, ys_ref, sg_ref, wglu_ref, bglu_ref, wmg_ref,
                  wa_ref, wb_ref, wc_ref, wout_ref, y_ref, wmg_s):
    @pl.when(pl.program_id(0) == 0)
    def _():
        _pack_transposed(wmg_ref.at[0], wmg_s, 0, 3 * D_MODEL // LANE, 0, None)

    x = x_ref[...]
    mod = mod_ref[0]
    h = _mod_rmsnorm(x, nw_ref[...], mod).astype(BF16)
    zg = _dot(jax.nn.gelu(ys_ref[...]), wglu_ref[...]) + bglu_ref[...]
    oc = (zg[:, :S5_WIDTH] * jax.nn.sigmoid(zg[:, S5_WIDTH:])
          * jax.nn.silu(sg_ref[...].astype(F32)))
    mixed = None
    for br, (o_br, w_ref) in enumerate(((oa_ref[...], wa_ref), (ob_ref[...], wb_ref), (oc, wc_ref))):
        gate = jax.nn.sigmoid(jnp.dot(h, wmg_s[:, br * D_MODEL:(br + 1) * D_MODEL], preferred_element_type=F32))
        term = gate * _dot(o_br, w_ref[...])
        mixed = term if mixed is None else mixed + term
    y_ref[...] = x + mod[:, 2 * D_MODEL:] * _dot(mixed, wout_ref[...])


def _merge(x2, row0, mod, mod_idx, nw, oa, ob, ys, zs, layer, wglu, bglu, w_in_t, wa, wb, wc, wout):
    n = x2.shape[0]
    tm = ROW_TILE
    blk0 = row0 // tm
    rows = lambda w: pl.BlockSpec((tm, w), lambda i: (i, 0))
    return pl.pallas_call(
        _merge_kernel,
        grid=(n // tm,),
        in_specs=[
            rows(D_MODEL),
            pl.BlockSpec((None, 1, 1, 3 * D_MODEL), lambda i: (layer, mod_idx(i), 0, 0)),
            _layer_spec((1, D_MODEL), layer),
            rows(GLA_WIDTH), rows(MLA_WIDTH), rows(S5_WIDTH),
            pl.BlockSpec((tm, S5_WIDTH), lambda i: (i + blk0, ZS_GATE // S5_WIDTH)),
            pl.BlockSpec((None, S5_WIDTH, 2 * S5_WIDTH), lambda i: (layer, 0, 0)),
            pl.BlockSpec((None, 1, 2 * S5_WIDTH), lambda i: (layer, 0, 0)),
            pl.BlockSpec((pl.Element(1), pl.Element(3 * D_MODEL), pl.Element(D_MODEL)),
                         lambda i: (layer, MERGE_COL, 0), pipeline_mode=pl.Buffered(1)),
            _layer_spec((GLA_WIDTH, D_MODEL), layer),
            _layer_spec((MLA_WIDTH, D_MODEL), layer),
            _layer_spec((S5_WIDTH, D_MODEL), layer),
            _layer_spec((D_MODEL, D_MODEL), layer),
        ],
        out_specs=rows(D_MODEL),
        out_shape=jax.ShapeDtypeStruct((n, D_MODEL), F32),
        scratch_shapes=[pltpu.VMEM((D_MODEL, 3 * D_MODEL), BF16)],
        compiler_params=_params(("arbitrary",)),
        name="merge",
    )(x2, mod, nw, oa, ob, ys, zs, wglu, bglu, w_in_t, wa, wb, wc, wout)


def _mla_lane_of_dim():
    half = MLA_ROPE // 2
    first_gap = ROPE_SHIFT - half
    lane = np.zeros(MLA_QK, np.int32)
    for j in range(MLA_NOPE):
        lane[j] = half + j if j < first_gap else 2 * half + j
    for r in range(half):
        lane[MLA_NOPE + r] = r
        lane[MLA_NOPE + half + r] = ROPE_SHIFT + r
    return lane


MLA_LANE_OF_DIM = _mla_lane_of_dim()


def _place_heads(w, heads, lane_of_dim):
    width = len(lane_of_dim)
    src = np.zeros(heads * HEAD_PAD, np.int32)
    used = np.zeros(heads * HEAD_PAD, bool)
    for h in range(heads):
        src[h * HEAD_PAD + lane_of_dim] = h * width + np.arange(width)
        used[h * HEAD_PAD + lane_of_dim] = True
    return jnp.where(jnp.asarray(used), jnp.take(w, jnp.asarray(src), axis=-1), 0.0)


def _rope_tables(n_tok):
    rows = n_tok // GRID_W
    r = jnp.repeat(jnp.arange(rows, dtype=F32), GRID_W)
    col = jnp.tile(jnp.arange(GRID_W, dtype=F32), rows)
    n_freq = MLA_ROPE // 4
    inv = ROPE_THETA ** (-jnp.arange(n_freq, dtype=F32) / n_freq)
    ang = jnp.concatenate([r[:, None] * inv, col[:, None] * inv], axis=-1)
    cos, sin = jnp.cos(ang), jnp.sin(ang)
    ones = jnp.ones((n_tok, MLA_NOPE), F32)
    c = _place_heads(jnp.concatenate([ones, cos, cos], axis=1), 1, MLA_LANE_OF_DIM)
    s = _place_heads(jnp.concatenate([0.0 * ones, -sin, sin], axis=1), 1, MLA_LANE_OF_DIM)
    return jnp.stack([c, s])


def kernel(x_prompt, x_sample, c, c_ctx, cache_mla_ckv, cache_mla_krope, state_gla, state_s5,
           norm_w, w_ada, b_ada, w_in, gla_w_a2, gla_b_a, gla_o_norm,
           mla_q_norm, mla_w_uq, mla_kv_norm, mla_w_uk, mla_w_uv, mla_qh_norm, mla_kh_norm,
           s5_a_re, s5_a_im, s5_log_dt, s5_b_re, s5_b_im, s5_c_re, s5_c_im, s5_d, s5_w_glu, s5_b_glu,
           w_bo_gla, w_bo_mla, w_bo_s5, w_out):
    bsz, seq, _ = x_prompt.shape
    dbsz, dseq, _ = x_sample.shape
    ctx_row = 8 - 1
    assert dbsz <= ctx_row and (bsz * seq) % ROW_TILE == 0 and dseq % ROW_TILE == 0

    cond8 = jnp.zeros((8, D_MODEL), F32).at[0:dbsz].set(c).at[ctx_row].set(c_ctx)
    ada = _ada(cond8, w_ada, b_ada)

    vec = lambda a: a.reshape(DEPTH, 2, 1, S5_NSTATE)
    ldt = jnp.repeat(s5_log_dt[..., None], S5_STATE, axis=-1)
    rows_gp = lambda t: t.reshape(DEPTH, S5_TILES, LANE, S5_STATE)
    bt = lambda b: rows_gp(b.transpose(0, 1, 3, 2))
    wmat, tab8, tab1 = _s5_prep(vec(s5_a_re), vec(s5_a_im), vec(ldt), bt(s5_b_re), bt(s5_b_im),
                                rows_gp(s5_c_re), rows_gp(s5_c_im), s5_d.reshape(DEPTH, 1, S5_WIDTH))
    wglu = s5_w_glu.astype(BF16)
    bglu = s5_b_glu.reshape(DEPTH, 1, 2 * S5_WIDTH)

    wuq = _place_heads(mla_w_uq, MLA_HEADS, MLA_LANE_OF_DIM).astype(BF16)
    wuk = _place_heads(mla_w_uk, MLA_HEADS, MLA_LANE_OF_DIM[:MLA_NOPE]).astype(BF16)
    wuv = mla_w_uv.astype(BF16)
    qhn = _place_heads(mla_qh_norm, 1, MLA_LANE_OF_DIM)
    khn = _place_heads(mla_kh_norm, 1, MLA_LANE_OF_DIM)
    e_np = np.zeros((LANE, MLA_HEADS * HEAD_PAD), np.float32)
    for h in range(MLA_HEADS):
        for i in range(MLA_ROPE):
            e_np[i, h * HEAD_PAD + MLA_LANE_OF_DIM[MLA_NOPE + i]] = 1.0
    e_place = jnp.asarray(e_np, BF16)
    rope_tab = _rope_tables(dseq)
    ckr_pad = jnp.pad(cache_mla_krope, ((0, 0), (0, 0), (0, 0), (0, LANE - MLA_ROPE)))

    zrow = lambda n: jnp.zeros((DEPTH, n, GLA_QK), F32)
    waf = jnp.concatenate([gla_w_a2[:, 0], zrow(LANE - GLA_RANK)], axis=1).astype(BF16)
    wab = jnp.concatenate([zrow(GLA_RANK), gla_w_a2[:, 1], zrow(LANE - 2 * GLA_RANK)], axis=1).astype(BF16)
    sgla = state_gla.reshape(dbsz, DEPTH, 2, GLA_QK, GLA_DV)
    ss5 = state_s5.reshape(dbsz, DEPTH, 2, 2, S5_NSTATE)
    zero_s5 = jnp.zeros((bsz, 2, 2, S5_NSTATE), F32)

    hp = x_prompt.reshape(bsz * seq, D_MODEL)
    hs = x_sample.reshape(dbsz * dseq, D_MODEL)
    ckv_l, krope_l, gla_l, s5_l = [], [], [], []
    w_in_t = jnp.swapaxes(w_in, 1, 2)
    mod = ada.reshape(DEPTH, 8, 1, 3 * D_MODEL)
    nw = norm_w.reshape(DEPTH, 1, D_MODEL)
    row = lambda t: t.reshape(DEPTH, 1, -1)
    mla_w = (row(mla_q_norm), wuq, row(mla_kv_norm), wuk, wuv, row(qhn), row(khn), e_place)
    wbo = (w_bo_gla.astype(BF16), w_bo_mla.astype(BF16), w_bo_s5.astype(BF16))
    wout = w_out.astype(BF16)
    onorm = gla_o_norm.reshape(DEPTH, 1, GLA_DV)
    for l in range(DEPTH):
        p_rows, p_blocks, blocks_per_seq = bsz * seq, bsz * seq // ROW_TILE, dseq // ROW_TILE
        mod_idx = lambda i: jnp.where(i < p_blocks, ctx_row, (i - p_blocks) // blocks_per_seq)
        zg, zm, zs = _in_proj(hp, hs, mod, mod_idx, nw, w_in_t, l)

        def mixers(x2, row0, nb, n, ctx):
            if ctx:
                gctx = sgla
                x0, x0_blk = ss5, (nb, None, 2, 2, S5_TILE_STATE)
                x0_idx = lambda j: (0, l, 0, 0, j)
                mctx, rt = (cache_mla_ckv, ckr_pad), rope_tab
            else:
                gctx = None
                x0, x0_blk = zero_s5, (nb, 2, 2, S5_TILE_STATE)
                x0_idx = lambda j: (0, 0, 0, j)
                mctx, rt = None, None
            oa, st_gla = _gla(zg, row0, gctx, l, waf, wab, gla_b_a, onorm, nb, n)
            ob, ckv = _mla(zm, row0, mctx, l, mla_w, rt, nb, n)
            y_ssm, st_s5 = _s5_scan(zs, row0, x0, x0_blk, x0_idx, l, wmat, tab8, tab1, nb, n)
            grp_mod_idx = lambda i: mod_idx(i + row0 // ROW_TILE)
            y = _merge(x2, row0, mod, grp_mod_idx, nw, oa, ob, y_ssm, zs, l, wglu, bglu, w_in_t, *wbo, wout)
            return y, ckv, st_gla, st_s5

        hp_next, ckv_p, st_gla_p, st_s5_p = mixers(hp, 0, bsz, seq, False)
        hs = mixers(hs, p_rows, dbsz, dseq, True)[0]
        hp = hp_next
        ckv_l.append(ckv_p.reshape(bsz, seq, MLA_KV_LORA))
        krope_l.append(zm[:p_rows, ZM_KR:ZM_KR + MLA_ROPE].astype(F32).reshape(bsz, seq, MLA_ROPE))
        gla_l.append(st_gla_p.reshape(bsz, 2, GLA_HEADS, GLA_DK, GLA_DV))
        s5_l.append(st_s5_p.reshape(bsz, 2, 2, S5_GROUPS, S5_STATE))

    return (hp.reshape(bsz, seq, D_MODEL), hs.reshape(dbsz, dseq, D_MODEL),
            jnp.stack(ckv_l, axis=1), jnp.stack(krope_l, axis=1),
            jnp.stack(gla_l, axis=1), jnp.stack(s5_l, axis=1))
```

```python
import functools

import jax
import jax.numpy as jnp
import numpy as np
from jax import lax
from jax.experimental import pallas as pl
from jax.experimental.pallas import tpu as pltpu

F32 = jnp.float32
BF16 = jnp.bfloat16

EPS = 1e-6
D_MODEL = 1024
DEPTH = 2
GRID_W = 64
ROPE_THETA = 10000.0
GLA_HEADS = 4
GLA_DK = 64
GLA_DV = 128
GLA_RANK = 16
GLA_GATE_NORM = 16.0
GLA_QK = GLA_HEADS * GLA_DK
GLA_WIDTH = GLA_HEADS * GLA_DV
MLA_HEADS = 4
MLA_Q_LORA = 384
MLA_KV_LORA = 256
MLA_NOPE = 64
MLA_ROPE = 32
MLA_QK = MLA_NOPE + MLA_ROPE
MLA_DV = 128
MLA_WIDTH = MLA_HEADS * MLA_DV
S5_WIDTH = 512
S5_GROUP = 16
S5_GROUPS = 32
S5_STATE = 64
S5_NSTATE = S5_GROUPS * S5_STATE

LANE = 128
HEAD_PAD = LANE
ROPE_SHIFT = LANE // 2
CHUNK = 64
GLA_STEP = 256
GLA_SEQS_PER_STEP = 2
GLA_ROWS_PER_STEP = 1024
GLA_SKEW = 1
S5_TILES = S5_WIDTH // LANE
S5_TILE_STATE = S5_NSTATE // S5_TILES
ROW_TILE = 512
Q_TILE = 512
PROJ_TILE = 256
VMEM_LIMIT = 56 * 1024 * 1024

MERGE_COL = 3776
ZG_Q, ZG_K, ZG_V, ZG_A, ZG_GATE, ZG_W = 0, 256, 512, 1024, 1152, 1664
ZM_Q, ZM_KV, ZM_KR, ZM_GATE, ZM_W = 0, 384, 640, 768, 1280
ZS_U, ZS_GATE, ZS_W = 0, 512, 1024
ZG_BASE, ZM_BASE, ZS_BASE, PACK_W = 0, ZG_W, ZG_W + ZM_W, ZG_W + ZM_W + ZS_W
IN_PIECES = (
    (0, 8, ZG_BASE + ZG_Q, None),
    (1024, 1, ZG_BASE + ZG_A, 2 * GLA_RANK),
    (1056, 4, ZG_BASE + ZG_GATE, None),
    (1568, 5, ZM_BASE + ZM_Q, None),
    (2208, 1, ZM_BASE + ZM_KR, MLA_ROPE),
    (2240, 4, ZM_BASE + ZM_GATE, None),
    (2752, 8, ZS_BASE + ZS_U, None),
)


def _dot(a, b):
    return jnp.dot(a.astype(BF16), b.astype(BF16), preferred_element_type=F32)


def _dot_nt(a, b):
    return lax.dot_general(a.astype(BF16), b.astype(BF16), (((1,), (1,)), ((), ())),
                           preferred_element_type=F32)


def _split_bf16(x, parts):
    out = []
    r = x
    for _ in range(parts):
        p = r.astype(BF16)
        out.append(p)
        r = r - p.astype(F32)
    return out


def _emit_skewed(chains, skew):
    pending, active, tick = list(chains), [], 0
    while pending or active:
        if pending and tick % skew == 0:
            active.append(pending.pop(0))
        for gen in list(active):
            if next(gen, "done") == "done":
                active.remove(gen)
        tick += 1


def _layer_spec(shape, layer):
    return pl.BlockSpec((None,) + tuple(shape), lambda *_: (layer,) + (0,) * len(shape))


def _params(sem):
    return pltpu.CompilerParams(dimension_semantics=sem, vmem_limit_bytes=VMEM_LIMIT)


def _ada_kernel(c_ref, w_ref, b_ref, o_ref):
    s = jax.nn.silu(c_ref[...])
    o_ref[...] = _dot(s, w_ref[...]) + b_ref[...]


def _ada(cond8, w_ada, b_ada):
    tn = 1024
    return pl.pallas_call(
        _ada_kernel,
        grid=(DEPTH, 3 * D_MODEL // tn),
        in_specs=[
            pl.BlockSpec((8, D_MODEL), lambda l, n: (0, 0)),
            pl.BlockSpec((None, D_MODEL, tn), lambda l, n: (l, 0, n)),
            pl.BlockSpec((None, 1, tn), lambda l, n: (l, 0, n)),
        ],
        out_specs=pl.BlockSpec((None, 8, tn), lambda l, n: (l, 0, n)),
        out_shape=jax.ShapeDtypeStruct((DEPTH, 8, 3 * D_MODEL), F32),
        compiler_params=_params(("parallel", "parallel")),
        name="ada",
    )(cond8, w_ada, b_ada.reshape(DEPTH, 1, 3 * D_MODEL))


def _mod_rmsnorm(x, nw, mod):
    ms = jnp.mean(x * x, axis=-1, keepdims=True)
    y = x * lax.rsqrt(ms + EPS) * nw
    return y * (1.0 + mod[:, D_MODEL:2 * D_MODEL]) + mod[:, 0:D_MODEL]


def _pack_transposed(w_ref, wb_s, src, tiles, dst, keep):
    lane = lax.broadcasted_iota(jnp.int32, (D_MODEL, LANE), 1)
    for t in range(tiles):
        blk = w_ref[src + t * LANE:src + (t + 1) * LANE, :].T
        if keep is not None:
            blk = jnp.where(lane < keep, blk, 0.0)
        wb_s[:, dst + t * LANE:dst + (t + 1) * LANE] = blk.astype(BF16)


def _in_proj_kernel(xp_ref, xs_ref, mod_ref, nw_ref, w_ref, zg_ref, zm_ref, zs_ref, wb_s, *, p_blocks):
    i = pl.program_id(0)

    @pl.when(i == 0)
    def _():
        for src, tiles, dst, keep in IN_PIECES:
            _pack_transposed(w_ref, wb_s, src, tiles, dst, keep)

    x = jnp.where(i < p_blocks, xp_ref[...], xs_ref[...])
    h = _mod_rmsnorm(x, nw_ref[...], mod_ref[0]).astype(BF16)
    z = jnp.dot(h, wb_s[...], preferred_element_type=F32)
    zg_ref[...] = z[:, ZG_BASE:ZG_BASE + ZG_W].astype(BF16)
    zm_ref[...] = z[:, ZM_BASE:ZM_BASE + ZM_W].astype(BF16)
    zs_ref[...] = z[:, ZS_BASE:ZS_BASE + ZS_W].astype(BF16)


def _two_group_rows(width, p_blocks):
    tm = ROW_TILE
    return (pl.BlockSpec((tm, width), lambda i: (jnp.minimum(i, p_blocks - 1), 0)),
            pl.BlockSpec((tm, width), lambda i: (jnp.maximum(i - p_blocks, 0), 0)))


def _in_proj(xp, xs, mod, mod_idx, nw, w_in_t, layer):
    tm = ROW_TILE
    p_blocks = xp.shape[0] // tm
    n = xp.shape[0] + xs.shape[0]
    return pl.pallas_call(
        functools.partial(_in_proj_kernel, p_blocks=p_blocks),
        grid=(n // tm,),
        in_specs=[
            *_two_group_rows(D_MODEL, p_blocks),
            pl.BlockSpec((None, 1, 1, 3 * D_MODEL), lambda i: (layer, mod_idx(i), 0, 0)),
            _layer_spec((1, D_MODEL), layer),
            pl.BlockSpec((None, MERGE_COL, D_MODEL), lambda i: (layer, 0, 0), pipeline_mode=pl.Buffered(1)),
        ],
        out_specs=[
            pl.BlockSpec((tm, ZG_W), lambda i: (i, 0)),
            pl.BlockSpec((tm, ZM_W), lambda i: (i, 0)),
            pl.BlockSpec((tm, ZS_W), lambda i: (i, 0)),
        ],
        out_shape=[
            jax.ShapeDtypeStruct((n, ZG_W), BF16),
            jax.ShapeDtypeStruct((n, ZM_W), BF16),
            jax.ShapeDtypeStruct((n, ZS_W), BF16),
        ],
        scratch_shapes=[pltpu.VMEM((D_MODEL, PACK_W), BF16)],
        compiler_params=_params(("arbitrary",)),
        name="in_proj",
    )(xp, xs, mod, nw, w_in_t)


def _gla_kernel(*refs, nsteps, seq, nseq, has_ctx):
    it = iter(refs)
    zg_ref = next(it)
    s0_ref = next(it) if has_ctx else None
    waf_ref, wab_ref, ba_ref, onorm_ref, o_ref, sfin_ref, la_s, o_s, st_s = (next(it) for _ in range(9))
    chains = [(g, d) for g in range(nseq) for d in (0, 1)]
    inv_norm = 1.0 / GLA_GATE_NORM
    zero_blk = jnp.zeros((GLA_DK, GLA_DV), F32)
    for ch, (g, d) in enumerate(chains):
        if d == 0:
            a_blk = zg_ref[g * seq:(g + 1) * seq, ZG_A:ZG_A + LANE]
        wa_ref = waf_ref if d == 0 else wab_ref
        a_low = _dot(a_blk, wa_ref[...]) + ba_ref[d:d + 1, :]
        la_s[ch] = (jnp.minimum(a_low, 0.0) - jnp.log(1.0 + jnp.exp(-jnp.abs(a_low)))) * inv_norm
        if has_ctx:
            s0 = s0_ref[g, d]
            rows_bd = []
            for h in range(GLA_HEADS):
                sh = s0[h * GLA_DK:(h + 1) * GLA_DK, :]
                rows_bd.append(jnp.concatenate([sh if h2 == h else zero_blk for h2 in range(GLA_HEADS)], axis=1))
            st_s[ch] = jnp.concatenate(rows_bd, axis=0).T
        else:
            st_s[ch] = jnp.zeros((GLA_WIDTH, GLA_QK), F32)

    def iota(shape, axis, shift):
        return lax.shift_right_logical(lax.broadcasted_iota(jnp.int32, shape, axis), shift)

    log_chunk, log_dv = CHUNK.bit_length() - 1, GLA_DV.bit_length() - 1
    row = lax.broadcasted_iota(jnp.int32, (GLA_STEP, GLA_STEP), 0)
    col = lax.broadcasted_iota(jnp.int32, (GLA_STEP, GLA_STEP), 1)
    same_chunk = iota((GLA_STEP, GLA_STEP), 0, log_chunk) == iota((GLA_STEP, GLA_STEP), 1, log_chunk)
    masks = (same_chunk & (row >= col), same_chunk & (row <= col))
    lane_head = iota((GLA_STEP, GLA_QK), 1, log_chunk)
    row_chunk = iota((GLA_STEP, GLA_QK), 0, log_chunk)
    state_blk = iota((GLA_WIDTH, GLA_QK), 0, log_dv) == iota((GLA_WIDTH, GLA_QK), 1, log_chunk)
    qscale = GLA_DK ** -0.5
    nch = GLA_STEP // CHUNK

    def chain_phases(i, ch, g, d):
        r0 = pl.multiple_of((i if d == 0 else nsteps - 1 - i) * GLA_STEP, GLA_STEP)
        zrows, rows = pl.ds(g * seq + r0, GLA_STEP), pl.ds(r0, GLA_STEP)
        a_hi, a_lo = _split_bf16(la_s[ch, rows, :], 2)
        tri = masks[d].astype(BF16)
        cum = (jnp.dot(tri, a_hi, preferred_element_type=F32)
               + jnp.dot(tri, a_lo, preferred_element_type=F32))
        yield
        edge = CHUNK - 1 if d == 0 else 0
        blast = [cum[c * CHUNK + edge:c * CHUNK + edge + 1, :] for c in range(nch)]
        bl = jnp.concatenate([jnp.broadcast_to(b, (CHUNK, GLA_QK)) for b in blast], axis=0)
        q = zg_ref[zrows, ZG_Q:ZG_Q + GLA_QK].astype(F32) * qscale
        k = zg_ref[zrows, ZG_K:ZG_K + GLA_QK].astype(F32)
        v = zg_ref[zrows, ZG_V:ZG_V + GLA_WIDTH]
        v_t = v.astype(F32).T.astype(BF16)
        qd = q * jnp.exp(cum)
        kd = (k * jnp.exp(-cum)).astype(BF16)
        kr = k * jnp.exp(bl - cum)
        yield
        outs = []
        for h in range(GLA_HEADS):
            qh = jnp.where(lane_head == h, qd, 0.0)
            att = _dot_nt(qh, kd)
            yield
            att = jnp.where(masks[d], att, 0.0)
            outs.append(_dot(att, v[:, h * GLA_DV:(h + 1) * GLA_DV]))
            yield
        s = st_s[ch]
        inter = [None] * nch
        for c in (range(nch) if d == 0 else reversed(range(nch))):
            inter[c] = _dot_nt(qd[c * CHUNK:(c + 1) * CHUNK, :], s)
            kv_t = _dot(v_t, jnp.where(row_chunk == c, kr, 0.0))
            yield
            s = s * jnp.exp(blast[c]) + jnp.where(state_blk, kv_t, 0.0)
            yield
        st_s[ch] = s
        o_s[ch, rows, :] = jnp.concatenate(outs, axis=1) + jnp.concatenate(inter, axis=0)

    def step(i, carry):
        _emit_skewed([chain_phases(i, ch, g, d) for ch, (g, d) in enumerate(chains)], GLA_SKEW)
        return carry

    lax.fori_loop(0, nsteps, step, 0)
    onorm = onorm_ref[...]
    for ch, (g, d) in enumerate(chains):
        s_fin = st_s[ch].T
        for h in range(GLA_HEADS):
            sfin_ref[g, d, h * GLA_DK:(h + 1) * GLA_DK, :] = (
                s_fin[h * GLA_DK:(h + 1) * GLA_DK, h * GLA_DV:(h + 1) * GLA_DV])
    for g in range(nseq):
        srows = slice(g * seq, (g + 1) * seq)
        o = o_s[2 * g] + o_s[2 * g + 1]
        gate = zg_ref[srows, ZG_GATE:ZG_GATE + GLA_WIDTH].astype(F32)
        for h in range(GLA_HEADS):
            vs = slice(h * GLA_DV, (h + 1) * GLA_DV)
            oh = o[:, vs]
            ms = jnp.mean(oh * oh, axis=-1, keepdims=True)
            o_ref[srows, vs] = oh * lax.rsqrt(ms + EPS) * onorm * jax.nn.silu(gate[:, vs])


def _gla(zg, row0, ctx, layer, waf, wab, ba, onorm, bsz, seq):
    nseq = max(GLA_SEQS_PER_STEP, GLA_ROWS_PER_STEP // seq)
    blk0 = row0 // (nseq * seq)
    in_specs = [pl.BlockSpec((nseq * seq, ZG_W), lambda b: (b + blk0, 0))]
    args = [zg]
    if ctx is not None:
        in_specs.append(pl.BlockSpec((nseq, None, 2, GLA_QK, GLA_DV), lambda b: (b, layer, 0, 0, 0)))
        args.append(ctx)
    in_specs += [
        _layer_spec((LANE, GLA_QK), layer),
        _layer_spec((LANE, GLA_QK), layer),
        _layer_spec((2, GLA_QK), layer),
        _layer_spec((1, GLA_DV), layer),
    ]
    return pl.pallas_call(
        functools.partial(_gla_kernel, nsteps=seq // GLA_STEP, seq=seq, nseq=nseq, has_ctx=ctx is not None),
        grid=(bsz // nseq,),
        in_specs=in_specs,
        out_specs=[
            pl.BlockSpec((nseq * seq, GLA_WIDTH), lambda b: (b, 0)),
            pl.BlockSpec((nseq, 2, GLA_QK, GLA_DV), lambda b: (b, 0, 0, 0)),
        ],
        out_shape=[
            jax.ShapeDtypeStruct((bsz * seq, GLA_WIDTH), F32),
            jax.ShapeDtypeStruct((bsz, 2, GLA_QK, GLA_DV), F32),
        ],
        scratch_shapes=[
            pltpu.VMEM((2 * nseq, seq, GLA_QK), F32),
            pltpu.VMEM((2 * nseq, seq, GLA_WIDTH), F32),
            pltpu.VMEM((2 * nseq, GLA_WIDTH, GLA_QK), F32),
        ],
        compiler_params=_params(("parallel",)),
        name="gla",
    )(*args, waf, wab, ba, onorm)


def _rms(x, w):
    ms = jnp.mean(x * x, axis=-1, keepdims=True)
    return x * lax.rsqrt(ms + EPS) * w


def _head_sums_mxu(x):
    width = x.shape[-1]
    shift = HEAD_PAD.bit_length() - 1
    gi = lax.shift_right_logical(lax.broadcasted_iota(jnp.int32, (width, width), 0), shift)
    gj = lax.shift_right_logical(lax.broadcasted_iota(jnp.int32, (width, width), 1), shift)
    return _dot(x * x, jnp.where(gi == gj, 1.0, 0.0))


def _head_norm(x, w, rope, on_mxu):
    sums = _head_sums_mxu(x) if on_mxu else None
    outs = []
    for h in range(MLA_HEADS):
        hs = slice(h * HEAD_PAD, (h + 1) * HEAD_PAD)
        xh = x[:, hs]
        ss = sums[:, hs] if on_mxu else jnp.sum(xh * xh, axis=-1, keepdims=True)
        yh = xh * lax.rsqrt(ss * (1.0 / MLA_QK) + EPS) * w
        if rope is not None:
            c, s = rope
            yh = yh * c + pltpu.roll(yh, ROPE_SHIFT, 1) * s
        outs.append(yh)
    return outs


def _place_rope_key(kr, e):
    return sum(jnp.dot(p, e, preferred_element_type=F32) for p in _split_bf16(kr, 3))


def _mla_kernel(*refs, seq, n_ctx, use_rope):
    it = iter(refs)
    zm_ref = next(it)
    if n_ctx:
        cckv_ref, ckr_ref = next(it), next(it)
    qn_ref, wuq_ref, kvn_ref, wuk_ref, wuv_ref, qhn_ref, khn_ref, e_ref = (next(it) for _ in range(8))
    rope_ref = next(it) if use_rope else None
    o_ref, ckv_ref = next(it), next(it)
    q_s, k_s, v_s = next(it), next(it), next(it)

    qscale = MLA_QK ** -0.5
    heads = [slice(h * HEAD_PAD, (h + 1) * HEAD_PAD) for h in range(MLA_HEADS)]

    def keys_values(ckv, k_rope_placed, rope, k_rows):
        kh = _head_norm(_dot(ckv, wuk_ref[...]) + k_rope_placed, khn_ref[...], rope, False)
        for h, hs in enumerate(heads):
            k_s[k_rows, hs] = kh[h].astype(BF16)
        v_s[k_rows, :] = _dot(ckv, wuv_ref[...]).astype(BF16)

    def latent_tile(i, carry):
        r0 = pl.multiple_of(i * PROJ_TILE, PROJ_TILE)
        rows = pl.ds(r0, PROJ_TILE)
        rope = (rope_ref[0, rows, :], rope_ref[1, rows, :]) if use_rope else None
        ckv = _rms(zm_ref[rows, ZM_KV:ZM_KV + MLA_KV_LORA].astype(F32), kvn_ref[...])
        ckv_ref[rows, :] = ckv
        k_pe = jnp.dot(zm_ref[rows, ZM_KR:ZM_KR + LANE], e_ref[...], preferred_element_type=F32)
        keys_values(ckv, k_pe, rope, pl.ds(n_ctx + r0, PROJ_TILE))
        cq = _rms(zm_ref[rows, ZM_Q:ZM_Q + MLA_Q_LORA].astype(F32), qn_ref[...])
        qh = _head_norm(_dot(cq, wuq_ref[...]), qhn_ref[...], rope, True)
        for h, hs in enumerate(heads):
            q_s[rows, hs] = (qh[h] * qscale).astype(BF16)
        return carry

    lax.fori_loop(0, seq // PROJ_TILE, latent_tile, 0)

    def context_tile(i, carry):
        rows = pl.ds(pl.multiple_of(i * PROJ_TILE, PROJ_TILE), PROJ_TILE)
        keys_values(cckv_ref[rows, :], _place_rope_key(ckr_ref[rows, :], e_ref[...]), None, rows)
        return carry

    if n_ctx:
        lax.fori_loop(0, n_ctx // PROJ_TILE, context_tile, 0)

    q_tile = min(seq, Q_TILE)

    def q_block(i, carry):
        rows = pl.ds(pl.multiple_of(i * q_tile, q_tile), q_tile)
        gate = zm_ref[rows, ZM_GATE:ZM_GATE + MLA_WIDTH].astype(F32)
        s = [lax.dot_general(q_s[rows, hs], k_s[:, hs], (((1,), (1,)), ((), ())),
                             preferred_element_type=F32) for hs in heads]
        p, l = [], []
        for h in range(MLA_HEADS):
            e = jnp.exp(s[h] - jnp.max(s[h], axis=-1, keepdims=True))
            l.append(jnp.sum(e, axis=-1, keepdims=True))
            p.append(e.astype(BF16))
        for h, hs in enumerate(heads):
            o = jnp.dot(p[h], v_s[:, hs], preferred_element_type=F32) / l[h]
            o_ref[rows, hs] = o * jax.nn.silu(gate[:, hs])
        return carry

    lax.fori_loop(0, seq // q_tile, q_block, 0)


def _mla(zm, row0, ctx, layer, w, rope_tab, bsz, seq):
    n_ctx = 0 if ctx is None else ctx[0].shape[-2]
    blk0 = row0 // seq
    in_specs = [pl.BlockSpec((seq, ZM_W), lambda b: (b + blk0, 0))]
    args = [zm]
    if ctx is not None:
        cckv, ckr = ctx
        in_specs += [
            pl.BlockSpec((None, None, n_ctx, MLA_KV_LORA), lambda b: (b, layer, 0, 0)),
            pl.BlockSpec((None, None, n_ctx, LANE), lambda b: (b, layer, 0, 0)),
        ]
        args += [cckv, ckr]
    in_specs += [
        _layer_spec((1, MLA_Q_LORA), layer),
        _layer_spec((MLA_Q_LORA, MLA_HEADS * HEAD_PAD), layer),
        _layer_spec((1, MLA_KV_LORA), layer),
        _layer_spec((MLA_KV_LORA, MLA_HEADS * HEAD_PAD), layer),
        _layer_spec((MLA_KV_LORA, MLA_WIDTH), layer),
        _layer_spec((1, HEAD_PAD), layer),
        _layer_spec((1, HEAD_PAD), layer),
        pl.BlockSpec((LANE, MLA_HEADS * HEAD_PAD), lambda b: (0, 0)),
    ]
    args += list(w)
    if rope_tab is not None:
        in_specs.append(pl.BlockSpec((2, seq, HEAD_PAD), lambda b: (0, 0, 0)))
        args.append(rope_tab)
    return pl.pallas_call(
        functools.partial(_mla_kernel, seq=seq, n_ctx=n_ctx, use_rope=rope_tab is not None),
        grid=(bsz,),
        in_specs=in_specs,
        out_specs=[
            pl.BlockSpec((seq, MLA_WIDTH), lambda b: (b, 0)),
            pl.BlockSpec((seq, MLA_KV_LORA), lambda b: (b, 0)),
        ],
        out_shape=[
            jax.ShapeDtypeStruct((bsz * seq, MLA_WIDTH), F32),
            jax.ShapeDtypeStruct((bsz * seq, MLA_KV_LORA), F32),
        ],
        scratch_shapes=[
            pltpu.VMEM((seq, MLA_HEADS * HEAD_PAD), BF16),
            pltpu.VMEM((n_ctx + seq, MLA_HEADS * HEAD_PAD), BF16),
            pltpu.VMEM((n_ctx + seq, MLA_WIDTH), BF16),
        ],
        compiler_params=_params(("parallel",)),
        name="mla",
    )(*args)


S5_T = 8
S5_R = CHUNK // S5_T
S5_SUB_CH = 64
S5_SUBS = LANE // S5_SUB_CH
S5_SUB_STATE = S5_TILE_STATE // S5_SUBS
S5_ROW = S5_T * S5_SUB_CH
S5_W = 2 * S5_SUB_STATE
W_M, W_SF, W_SB, W_CF, W_CB = range(5)


def _cmul(ar, ai, br, bi):
    return ar * br - ai * bi, ar * bi + ai * br


def _s5_prep_kernel(are_ref, aim_ref, ldt_ref, bre_ref, bim_ref, cre_ref, cim_ref, d_ref,
                    w_ref, tab8_ref, tab1_ref):
    gr = lax.shift_right_logical(lax.broadcasted_iota(jnp.int32, (S5_SUB_CH, S5_SUB_STATE), 0),
                                 S5_GROUP.bit_length() - 1)
    gc = lax.shift_right_logical(lax.broadcasted_iota(jnp.int32, (S5_SUB_CH, S5_SUB_STATE), 1),
                                 S5_STATE.bit_length() - 1)

    def spread(ref, h):
        x = ref[h * S5_SUB_CH:(h + 1) * S5_SUB_CH, :]
        return jnp.where(gr == gc, jnp.concatenate([x] * (S5_SUB_CH // S5_GROUP), axis=1), 0.0)

    row = lax.broadcasted_iota(jnp.int32, (S5_SUB_CH, S5_SUB_CH), 0)
    col = lax.broadcasted_iota(jnp.int32, (S5_SUB_CH, S5_SUB_CH), 1)
    taps = [[[], []] for _ in range(S5_SUBS)]
    for d in (0, 1):
        a_re, a_im = are_ref[d], aim_ref[d]
        dt = jnp.exp(ldt_ref[d])
        lam = a_re * dt
        th = a_im * dt
        mag = jnp.exp(lam)
        ab_re = mag * jnp.cos(th)
        ab_im = mag * jnp.sin(th)
        den = a_re * a_re + a_im * a_im
        n_re = ab_re - 1.0
        cf_re = (n_re * a_re + ab_im * a_im) / den
        cf_im = (ab_im * a_re - n_re * a_im) / den
        k = lax.broadcasted_iota(jnp.int32, (2 * S5_T, S5_TILE_STATE), 0).astype(F32)
        pmag = jnp.exp(k * lam)
        pw_re = pmag * jnp.cos(k * th)
        pw_im = pmag * jnp.sin(k * th)
        for h in range(S5_SUBS):
            ss = slice(h * S5_SUB_STATE, (h + 1) * S5_SUB_STATE)
            c_re, c_im = spread(cre_ref, h), spread(cim_ref, h)
            c_cat = jnp.concatenate([c_re, c_im], axis=1).astype(BF16)
            bp_re, bp_im = _cmul(spread(bre_ref, h), spread(bim_ref, h), cf_re[:, ss], cf_im[:, ss])
            for p in range(S5_T + 1):
                ar, ai = pw_re[p:p + 1, ss], pw_im[p:p + 1, ss]
                t_in = S5_T - 1 - p if d == 0 else p
                t_out = p - 1 if d == 0 else S5_T - p
                if p < S5_T:
                    l_re, l_im = _cmul(bp_re, bp_im, ar, ai)
                    w_ref[h, W_SF + d, t_in * S5_SUB_CH:(t_in + 1) * S5_SUB_CH, :] = (
                        jnp.concatenate([l_re, l_im], axis=1).astype(BF16))
                    taps[h][d].append(_dot_nt(jnp.concatenate([l_re, -l_im], axis=1), c_cat))
                if p > 0:
                    v_re, v_im = _cmul(c_re, c_im, ar, ai)
                    w_ref[h, W_CF + d, t_out * S5_SUB_CH:(t_out + 1) * S5_SUB_CH, :] = (
                        jnp.concatenate([v_re, -v_im], axis=1).astype(BF16))
        r = lax.broadcasted_iota(jnp.int32, (S5_R, S5_TILE_STATE), 0).astype(F32) * float(S5_T)
        r1 = r + float(S5_T)
        pm = jnp.exp(r * lam)
        qm = jnp.exp(-(r1 * lam))
        tab8_ref[d, 0] = pm * jnp.cos(r * th)
        tab8_ref[d, 1] = pm * jnp.sin(r * th)
        tab8_ref[d, 2] = qm * jnp.cos(r1 * th)
        tab8_ref[d, 3] = -(qm * jnp.sin(r1 * th))
        mc = jnp.exp(float(CHUNK) * lam)
        tab1_ref[d, 0:1, :] = mc * jnp.cos(float(CHUNK) * th)
        tab1_ref[d, 1:2, :] = mc * jnp.sin(float(CHUNK) * th)
    for h in range(S5_SUBS):
        skip = jnp.where(row == col, d_ref[:, h * S5_SUB_CH:(h + 1) * S5_SUB_CH], 0.0)
        for t in range(S5_T):
            blocks = []
            for t2 in range(S5_T):
                if t < t2:
                    blocks.append(taps[h][0][t2 - t])
                elif t > t2:
                    blocks.append(taps[h][1][t - t2])
                else:
                    blocks.append(taps[h][0][0] + taps[h][1][0] + skip)
            w_ref[h, W_M, t * S5_SUB_CH:(t + 1) * S5_SUB_CH, :] = jnp.concatenate(blocks, axis=1).astype(BF16)


def _s5_prep(a_re, a_im, ldt, b_re, b_im, c_re, c_im, dsk):
    vec = pl.BlockSpec((None, 2, 1, S5_TILE_STATE), lambda l, j: (l, 0, 0, j))
    blk = pl.BlockSpec((None, None, LANE, S5_STATE), lambda l, j: (l, j, 0, 0))
    return pl.pallas_call(
        _s5_prep_kernel,
        grid=(DEPTH, S5_TILES),
        in_specs=[vec, vec, vec, blk, blk, blk, blk,
                  pl.BlockSpec((None, 1, LANE), lambda l, j: (l, 0, j))],
        out_specs=[
            pl.BlockSpec((None, None, S5_SUBS, 5, S5_ROW, S5_W), lambda l, j: (l, j, 0, 0, 0, 0)),
            pl.BlockSpec((None, None, 2, 4, S5_R, S5_TILE_STATE), lambda l, j: (l, j, 0, 0, 0, 0)),
            pl.BlockSpec((None, None, 2, 2, S5_TILE_STATE), lambda l, j: (l, j, 0, 0, 0)),
        ],
        out_shape=[
            jax.ShapeDtypeStruct((DEPTH, S5_TILES, S5_SUBS, 5, S5_ROW, S5_W), BF16),
            jax.ShapeDtypeStruct((DEPTH, S5_TILES, 2, 4, S5_R, S5_TILE_STATE), F32),
            jax.ShapeDtypeStruct((DEPTH, S5_TILES, 2, 2, S5_TILE_STATE), F32),
        ],
        compiler_params=_params(("parallel", "parallel")),
        name="s5_prep",
    )(a_re, a_im, ldt, b_re, b_im, c_re, c_im, dsk)


def _s5_scan_kernel(u_ref, x0_ref, w_ref, tab8_ref, tab1_ref, y_ref, fs_ref, u_s, *, nseq, nb):
    groups = nseq * nb
    nrow = groups * S5_R
    ts = S5_SUB_STATE
    u_s[...] = u_ref[...].astype(F32)
    tokens = [u_s[pl.ds(t, nrow, stride=S5_T), :] for t in range(S5_T)]
    rowi = lax.broadcasted_iota(jnp.int32, (groups, S5_R, ts), 1)

    def prefix(x):
        for s in (1, 2, 4):
            x = x + jnp.where(rowi >= s, pltpu.roll(x, s, 1), 0.0)
        return x

    def suffix(x):
        for s in (1, 2, 4):
            x = x + jnp.where(rowi < S5_R - s, pltpu.roll(x, S5_R - s, 1), 0.0)
        return x

    y_sub = []
    for h in range(S5_SUBS):
        ch = slice(h * S5_SUB_CH, (h + 1) * S5_SUB_CH)
        ss = slice(h * ts, (h + 1) * ts)
        u8 = jnp.concatenate([tok[:, ch] for tok in tokens], axis=1).astype(BF16)
        ef = jnp.dot(u8, w_ref[h, W_SF], preferred_element_type=F32).reshape(groups, S5_R, S5_W)
        eb = jnp.dot(u8, w_ref[h, W_SB], preferred_element_type=F32).reshape(groups, S5_R, S5_W)

        p_re, p_im, q_re, q_im = (tab8_ref[0, i, :, ss] for i in range(4))
        a_re, a_im = tab1_ref[0, 0:1, ss], tab1_ref[0, 1:2, ss]
        w_re, w_im = _cmul(q_re, q_im, ef[:, :, :ts], ef[:, :, ts:])
        cs_re, cs_im = prefix(w_re), prefix(w_im)
        st_re, st_im = [], []
        for s in range(nseq):
            x_re, x_im = x0_ref[s, 0, 0:1, ss], x0_ref[s, 0, 1:2, ss]
            for b in range(nb):
                g = s * nb + b
                st_re.append(x_re)
                st_im.append(x_im)
                x_re, x_im = _cmul(a_re, a_im, x_re + cs_re[g, S5_R - 1:S5_R, :],
                                   x_im + cs_im[g, S5_R - 1:S5_R, :])
            fs_ref[s, 0, 0:1, ss] = x_re
            fs_ref[s, 0, 1:2, ss] = x_im
        xin_re, xin_im = _cmul(p_re, p_im, cs_re - w_re + jnp.stack(st_re), cs_im - w_im + jnp.stack(st_im))
        xin = jnp.concatenate([xin_re, xin_im], axis=2).reshape(nrow, S5_W)

        p_re, p_im, q_re, q_im = (tab8_ref[1, i, :, ss] for i in range(4))
        a_re, a_im = tab1_ref[1, 0:1, ss], tab1_ref[1, 1:2, ss]
        w_re, w_im = _cmul(p_re, p_im, eb[:, :, :ts], eb[:, :, ts:])
        sf_re, sf_im = suffix(w_re), suffix(w_im)
        z_re, z_im = [None] * groups, [None] * groups
        for s in range(nseq):
            x_re, x_im = x0_ref[s, 1, 0:1, ss], x0_ref[s, 1, 1:2, ss]
            for b in reversed(range(nb)):
                g = s * nb + b
                z_re[g], z_im[g] = _cmul(a_re, a_im, x_re, x_im)
                x_re = sf_re[g, 0:1, :] + z_re[g]
                x_im = sf_im[g, 0:1, :] + z_im[g]
            fs_ref[s, 1, 0:1, ss] = x_re
            fs_ref[s, 1, 1:2, ss] = x_im
        xnx_re, xnx_im = _cmul(q_re, q_im, sf_re - w_re + jnp.stack(z_re), sf_im - w_im + jnp.stack(z_im))
        xnx = jnp.concatenate([xnx_re, xnx_im], axis=2).reshape(nrow, S5_W)

        y_sub.append(jnp.dot(u8, w_ref[h, W_M], preferred_element_type=F32)
                     + _dot_nt(xin, w_ref[h, W_CF]) + _dot_nt(xnx, w_ref[h, W_CB]))
    for t in range(S5_T):
        tc = slice(t * S5_SUB_CH, (t + 1) * S5_SUB_CH)
        y_ref[pl.ds(t, nrow, stride=S5_T), :] = jnp.concatenate([y[:, tc] for y in y_sub], axis=1)


def _s5_scan(zs, row0, x0, x0_block, x0_idx, layer, wmat, tab8, tab1, nseq, seq):
    n = nseq * seq
    rblk = row0 // n
    return pl.pallas_call(
        functools.partial(_s5_scan_kernel, nseq=nseq, nb=seq // CHUNK),
        grid=(S5_TILES,),
        in_specs=[
            pl.BlockSpec((n, LANE), lambda j: (rblk, ZS_U // LANE + j)),
            pl.BlockSpec(x0_block, x0_idx),
            pl.BlockSpec((None, None, S5_SUBS, 5, S5_ROW, S5_W), lambda j: (layer, j, 0, 0, 0, 0)),
            pl.BlockSpec((None, None, 2, 4, S5_R, S5_TILE_STATE), lambda j: (layer, j, 0, 0, 0, 0)),
            pl.BlockSpec((None, None, 2, 2, S5_TILE_STATE), lambda j: (layer, j, 0, 0, 0)),
        ],
        out_specs=[
            pl.BlockSpec((n, LANE), lambda j: (0, j)),
            pl.BlockSpec((nseq, 2, 2, S5_TILE_STATE), lambda j: (0, 0, 0, j)),
        ],
        out_shape=[
            jax.ShapeDtypeStruct((n, S5_WIDTH), F32),
            jax.ShapeDtypeStruct((nseq, 2, 2, S5_NSTATE), F32),
        ],
        scratch_shapes=[pltpu.VMEM((n, LANE), F32)],
        compiler_params=_params(("parallel",)),
        name="s5_scan",
    )(zs, x0, wmat, tab8, tab1)


def _merge_kernel(x_ref, mod_ref, nw_ref, oa_ref, ob_ref, ys_ref, sg_ref, wglu_ref, bglu_ref, wmg_ref,
                  wa_ref, wb_ref, wc_ref, wout_ref, y_ref, wmg_s):
    @pl.when(pl.program_id(0) == 0)
    def _():
        _pack_transposed(wmg_ref.at[0], wmg_s, 0, 3 * D_MODEL // LANE, 0, None)

    x = x_ref[...]
    mod = mod_ref[0]
    h = _mod_rmsnorm(x, nw_ref[...], mod).astype(BF16)
    zg = _dot(jax.nn.gelu(ys_ref[...]), wglu_ref[...]) + bglu_ref[...]
    oc = (zg[:, :S5_WIDTH] * jax.nn.sigmoid(zg[:, S5_WIDTH:])
          * jax.nn.silu(sg_ref[...].astype(F32)))
    mixed = None
    for br, (o_br, w_ref) in enumerate(((oa_ref[...], wa_ref), (ob_ref[...], wb_ref), (oc, wc_ref))):
        gate = jax.nn.sigmoid(jnp.dot(h, wmg_s[:, br * D_MODEL:(br + 1) * D_MODEL], preferred_element_type=F32))
        term = gate * _dot(o_br, w_ref[...])
        mixed = term if mixed is None else mixed + term
    y_ref[...] = x + mod[:, 2 * D_MODEL:] * _dot(mixed, wout_ref[...])


def _merge(x2, row0, mod, mod_idx, nw, oa, ob, ys, zs, layer, wglu, bglu, w_in_t, wa, wb, wc, wout):
    n = x2.shape[0]
    tm = ROW_TILE
    blk0 = row0 // tm
    rows = lambda w: pl.BlockSpec((tm, w), lambda i: (i, 0))
    return pl.pallas_call(
        _merge_kernel,
        grid=(n // tm,),
        in_specs=[
            rows(D_MODEL),
            pl.BlockSpec((None, 1, 1, 3 * D_MODEL), lambda i: (layer, mod_idx(i), 0, 0)),
            _layer_spec((1, D_MODEL), layer),
            rows(GLA_WIDTH), rows(MLA_WIDTH), rows(S5_WIDTH),
            pl.BlockSpec((tm, S5_WIDTH), lambda i: (i + blk0, ZS_GATE // S5_WIDTH)),
            pl.BlockSpec((None, S5_WIDTH, 2 * S5_WIDTH), lambda i: (layer, 0, 0)),
            pl.BlockSpec((None, 1, 2 * S5_WIDTH), lambda i: (layer, 0, 0)),
            pl.BlockSpec((pl.Element(1), pl.Element(3 * D_MODEL), pl.Element(D_MODEL)),
                         lambda i: (layer, MERGE_COL, 0), pipeline_mode=pl.Buffered(1)),
            _layer_spec((GLA_WIDTH, D_MODEL), layer),
            _layer_spec((MLA_WIDTH, D_MODEL), layer),
            _layer_spec((S5_WIDTH, D_MODEL), layer),
            _layer_spec((D_MODEL, D_MODEL), layer),
        ],
        out_specs=rows(D_MODEL),
        out_shape=jax.ShapeDtypeStruct((n, D_MODEL), F32),
        scratch_shapes=[pltpu.VMEM((D_MODEL, 3 * D_MODEL), BF16)],
        compiler_params=_params(("arbitrary",)),
        name="merge",
    )(x2, mod, nw, oa, ob, ys, zs, wglu, bglu, w_in_t, wa, wb, wc, wout)


def _mla_lane_of_dim():
    half = MLA_ROPE // 2
    first_gap = ROPE_SHIFT - half
    lane = np.zeros(MLA_QK, np.int32)
    for j in range(MLA_NOPE):
        lane[j] = half + j if j < first_gap else 2 * half + j
    for r in range(half):
        lane[MLA_NOPE + r] = r
        lane[MLA_NOPE + half + r] = ROPE_SHIFT + r
    return lane


MLA_LANE_OF_DIM = _mla_lane_of_dim()


def _place_heads(w, heads, lane_of_dim):
    width = len(lane_of_dim)
    src = np.zeros(heads * HEAD_PAD, np.int32)
    used = np.zeros(heads * HEAD_PAD, bool)
    for h in range(heads):
        src[h * HEAD_PAD + lane_of_dim] = h * width + np.arange(width)
        used[h * HEAD_PAD + lane_of_dim] = True
    return jnp.where(jnp.asarray(used), jnp.take(w, jnp.asarray(src), axis=-1), 0.0)


def _rope_tables(n_tok):
    rows = n_tok // GRID_W
    r = jnp.repeat(jnp.arange(rows, dtype=F32), GRID_W)
    col = jnp.tile(jnp.arange(GRID_W, dtype=F32), rows)
    n_freq = MLA_ROPE // 4
    inv = ROPE_THETA ** (-jnp.arange(n_freq, dtype=F32) / n_freq)
    ang = jnp.concatenate([r[:, None] * inv, col[:, None] * inv], axis=-1)
    cos, sin = jnp.cos(ang), jnp.sin(ang)
    ones = jnp.ones((n_tok, MLA_NOPE), F32)
    c = _place_heads(jnp.concatenate([ones, cos, cos], axis=1), 1, MLA_LANE_OF_DIM)
    s = _place_heads(jnp.concatenate([0.0 * ones, -sin, sin], axis=1), 1, MLA_LANE_OF_DIM)
    return jnp.stack([c, s])


def kernel(x_prompt, x_sample, c, c_ctx, cache_mla_ckv, cache_mla_krope, state_gla, state_s5,
           norm_w, w_ada, b_ada, w_in, gla_w_a2, gla_b_a, gla_o_norm,
           mla_q_norm, mla_w_uq, mla_kv_norm, mla_w_uk, mla_w_uv, mla_qh_norm, mla_kh_norm,
           s5_a_re, s5_a_im, s5_log_dt, s5_b_re, s5_b_im, s5_c_re, s5_c_im, s5_d, s5_w_glu, s5_b_glu,
           w_bo_gla, w_bo_mla, w_bo_s5, w_out):
    bsz, seq, _ = x_prompt.shape
    dbsz, dseq, _ = x_sample.shape
    ctx_row = 8 - 1
    assert dbsz <= ctx_row and (bsz * seq) % ROW_TILE == 0 and dseq % ROW_TILE == 0

    cond8 = jnp.zeros((8, D_MODEL), F32).at[0:dbsz].set(c).at[ctx_row].set(c_ctx)
    ada = _ada(cond8, w_ada, b_ada)

    vec = lambda a: a.reshape(DEPTH, 2, 1, S5_NSTATE)
    ldt = jnp.repeat(s5_log_dt[..., None], S5_STATE, axis=-1)
    rows_gp = lambda t: t.reshape(DEPTH, S5_TILES, LANE, S5_STATE)
    bt = lambda b: rows_gp(b.transpose(0, 1, 3, 2))
    wmat, tab8, tab1 = _s5_prep(vec(s5_a_re), vec(s5_a_im), vec(ldt), bt(s5_b_re), bt(s5_b_im),
                                rows_gp(s5_c_re), rows_gp(s5_c_im), s5_d.reshape(DEPTH, 1, S5_WIDTH))
    wglu = s5_w_glu.astype(BF16)
    bglu = s5_b_glu.reshape(DEPTH, 1, 2 * S5_WIDTH)

    wuq = _place_heads(mla_w_uq, MLA_HEADS, MLA_LANE_OF_DIM).astype(BF16)
    wuk = _place_heads(mla_w_uk, MLA_HEADS, MLA_LANE_OF_DIM[:MLA_NOPE]).astype(BF16)
    wuv = mla_w_uv.astype(BF16)
    qhn = _place_heads(mla_qh_norm, 1, MLA_LANE_OF_DIM)
    khn = _place_heads(mla_kh_norm, 1, MLA_LANE_OF_DIM)
    e_np = np.zeros((LANE, MLA_HEADS * HEAD_PAD), np.float32)
    for h in range(MLA_HEADS):
        for i in range(MLA_ROPE):
            e_np[i, h * HEAD_PAD + MLA_LANE_OF_DIM[MLA_NOPE + i]] = 1.0
    e_place = jnp.asarray(e_np, BF16)
    rope_tab = _rope_tables(dseq)
    ckr_pad = jnp.pad(cache_mla_krope, ((0, 0), (0, 0), (0, 0), (0, LANE - MLA_ROPE)))

    zrow = lambda n: jnp.zeros((DEPTH, n, GLA_QK), F32)
    waf = jnp.concatenate([gla_w_a2[:, 0], zrow(LANE - GLA_RANK)], axis=1).astype(BF16)
    wab = jnp.concatenate([zrow(GLA_RANK), gla_w_a2[:, 1], zrow(LANE - 2 * GLA_RANK)], axis=1).astype(BF16)
    sgla = state_gla.reshape(dbsz, DEPTH, 2, GLA_QK, GLA_DV)
    ss5 = state_s5.reshape(dbsz, DEPTH, 2, 2, S5_NSTATE)
    zero_s5 = jnp.zeros((bsz, 2, 2, S5_NSTATE), F32)

    hp = x_prompt.reshape(bsz * seq, D_MODEL)
    hs = x_sample.reshape(dbsz * dseq, D_MODEL)
    ckv_l, krope_l, gla_l, s5_l = [], [], [], []
    w_in_t = jnp.swapaxes(w_in, 1, 2)
    mod = ada.reshape(DEPTH, 8, 1, 3 * D_MODEL)
    nw = norm_w.reshape(DEPTH, 1, D_MODEL)
    row = lambda t: t.reshape(DEPTH, 1, -1)
    mla_w = (row(mla_q_norm), wuq, row(mla_kv_norm), wuk, wuv, row(qhn), row(khn), e_place)
    wbo = (w_bo_gla.astype(BF16), w_bo_mla.astype(BF16), w_bo_s5.astype(BF16))
    wout = w_out.astype(BF16)
    onorm = gla_o_norm.reshape(DEPTH, 1, GLA_DV)
    for l in range(DEPTH):
        p_rows, p_blocks, blocks_per_seq = bsz * seq, bsz * seq // ROW_TILE, dseq // ROW_TILE
        mod_idx = lambda i: jnp.where(i < p_blocks, ctx_row, (i - p_blocks) // blocks_per_seq)
        zg, zm, zs = _in_proj(hp, hs, mod, mod_idx, nw, w_in_t, l)

        def mixers(x2, row0, nb, n, ctx):
            if ctx:
                gctx = sgla
                x0, x0_blk = ss5, (nb, None, 2, 2, S5_TILE_STATE)
                x0_idx = lambda j: (0, l, 0, 0, j)
                mctx, rt = (cache_mla_ckv, ckr_pad), rope_tab
            else:
                gctx = None
                x0, x0_blk = zero_s5, (nb, 2, 2, S5_TILE_STATE)
                x0_idx = lambda j: (0, 0, 0, j)
                mctx, rt = None, None
            oa, st_gla = _gla(zg, row0, gctx, l, waf, wab, gla_b_a, onorm, nb, n)
            ob, ckv = _mla(zm, row0, mctx, l, mla_w, rt, nb, n)
            y_ssm, st_s5 = _s5_scan(zs, row0, x0, x0_blk, x0_idx, l, wmat, tab8, tab1, nb, n)
            grp_mod_idx = lambda i: mod_idx(i + row0 // ROW_TILE)
            y = _merge(x2, row0, mod, grp_mod_idx, nw, oa, ob, y_ssm, zs, l, wglu, bglu, w_in_t, *wbo, wout)
            return y, ckv, st_gla, st_s5

        hp_next, ckv_p, st_gla_p, st_s5_p = mixers(hp, 0, bsz, seq, False)
        hs = mixers(hs, p_rows, dbsz, dseq, True)[0]
        hp = hp_next
        ckv_l.append(ckv_p.reshape(bsz, seq, MLA_KV_LORA))
        krope_l.append(zm[:p_rows, ZM_KR:ZM_KR + MLA_ROPE].astype(F32).reshape(bsz, seq, MLA_ROPE))
        gla_l.append(st_gla_p.reshape(bsz, 2, GLA_HEADS, GLA_DK, GLA_DV))
        s5_l.append(st_s5_p.reshape(bsz, 2, 2, S5_GROUPS, S5_STATE))

    return (hp.reshape(bsz, seq, D_MODEL), hs.reshape(dbsz, dseq, D_MODEL),
            jnp.stack(ckv_l, axis=1), jnp.stack(krope_l, axis=1),
            jnp.stack(gla_l, axis=1), jnp.stack(s5_l, axis=1))
```

```python
import functools

import jax
import jax.numpy as jnp
import numpy as np
from jax import lax
from jax.experimental import pallas as pl
from jax.experimental.pallas import tpu as pltpu

F32 = jnp.float32
BF16 = jnp.bfloat16

EPS = 1e-6
D_MODEL = 1024
DEPTH = 2
GRID_W = 64
ROPE_THETA = 10000.0
GLA_HEADS = 4
GLA_DK = 64
GLA_DV = 128
GLA_RANK = 16
GLA_GATE_NORM = 16.0
GLA_QK = GLA_HEADS * GLA_DK
GLA_WIDTH = GLA_HEADS * GLA_DV
MLA_HEADS = 4
MLA_Q_LORA = 384
MLA_KV_LORA = 256
MLA_NOPE = 64
MLA_ROPE = 32
MLA_QK = MLA_NOPE + MLA_ROPE
MLA_DV = 128
MLA_WIDTH = MLA_HEADS * MLA_DV
S5_WIDTH = 512
S5_GROUP = 16
S5_GROUPS = 32
S5_STATE = 64
S5_NSTATE = S5_GROUPS * S5_STATE

LANE = 128
HEAD_PAD = LANE
ROPE_SHIFT = LANE // 2
CHUNK = 64
GLA_STEP = 256
GLA_SEQS_PER_STEP = 2
GLA_ROWS_PER_STEP = 1024
GLA_SKEW = 1
S5_TILES = S5_WIDTH // LANE
S5_TILE_STATE = S5_NSTATE // S5_TILES
ROW_TILE = 512
Q_TILE = 512
PROJ_TILE = 256
MLA_ROWS_PER_STEP = 1024
VMEM_LIMIT = 56 * 1024 * 1024

MERGE_COL = 3776
ZG_Q, ZG_K, ZG_V, ZG_A, ZG_GATE, ZG_W = 0, 256, 512, 1024, 1152, 1664
ZM_Q, ZM_KV, ZM_KR, ZM_GATE, ZM_W = 0, 384, 640, 768, 1280
ZS_U, ZS_GATE, ZS_W = 0, 512, 1024
ZG_BASE, ZM_BASE, ZS_BASE, PACK_W = 0, ZG_W, ZG_W + ZM_W, ZG_W + ZM_W + ZS_W
IN_PIECES = (
    (0, 8, ZG_BASE + ZG_Q, None),
    (1024, 1, ZG_BASE + ZG_A, 2 * GLA_RANK),
    (1056, 4, ZG_BASE + ZG_GATE, None),
    (1568, 5, ZM_BASE + ZM_Q, None),
    (2208, 1, ZM_BASE + ZM_KR, MLA_ROPE),
    (2240, 4, ZM_BASE + ZM_GATE, None),
    (2752, 8, ZS_BASE + ZS_U, None),
)


def _dot(a, b):
    return jnp.dot(a.astype(BF16), b.astype(BF16), preferred_element_type=F32)


def _dot_nt(a, b):
    return lax.dot_general(a.astype(BF16), b.astype(BF16), (((1,), (1,)), ((), ())),
                           preferred_element_type=F32)


def _split_bf16(x, parts):
    out = []
    r = x
    for _ in range(parts):
        p = r.astype(BF16)
        out.append(p)
        r = r - p.astype(F32)
    return out


def _emit_skewed(chains, skew):
    pending, active, tick = list(chains), [], 0
    while pending or active:
        while pending and (skew == 0 or tick % skew == 0):
            active.append(pending.pop(0))
            if skew:
                break
        for gen in list(active):
            if next(gen, "done") == "done":
                active.remove(gen)
        tick += 1


def _layer_spec(shape, layer):
    return pl.BlockSpec((None,) + tuple(shape), lambda *_: (layer,) + (0,) * len(shape))


def _params(sem):
    return pltpu.CompilerParams(dimension_semantics=sem, vmem_limit_bytes=VMEM_LIMIT)


def _ada_kernel(c_ref, w_ref, b_ref, o_ref):
    s = jax.nn.silu(c_ref[...])
    o_ref[...] = _dot(s, w_ref[...]) + b_ref[...]


def _ada(cond8, w_ada, b_ada):
    tn = 1024
    return pl.pallas_call(
        _ada_kernel,
        grid=(DEPTH, 3 * D_MODEL // tn),
        in_specs=[
            pl.BlockSpec((8, D_MODEL), lambda l, n: (0, 0)),
            pl.BlockSpec((None, D_MODEL, tn), lambda l, n: (l, 0, n)),
            pl.BlockSpec((None, 1, tn), lambda l, n: (l, 0, n)),
        ],
        out_specs=pl.BlockSpec((None, 8, tn), lambda l, n: (l, 0, n)),
        out_shape=jax.ShapeDtypeStruct((DEPTH, 8, 3 * D_MODEL), F32),
        compiler_params=_params(("parallel", "parallel")),
        name="ada",
    )(cond8, w_ada, b_ada.reshape(DEPTH, 1, 3 * D_MODEL))


def _mod_rmsnorm(x, nw, mod):
    ms = jnp.mean(x * x, axis=-1, keepdims=True)
    y = x * lax.rsqrt(ms + EPS) * nw
    return y * (1.0 + mod[:, D_MODEL:2 * D_MODEL]) + mod[:, 0:D_MODEL]


def _pack_transposed(w_ref, wb_s, src, tiles, dst, keep):
    lane = lax.broadcasted_iota(jnp.int32, (D_MODEL, LANE), 1)
    for t in range(tiles):
        blk = w_ref[src + t * LANE:src + (t + 1) * LANE, :].T
        if keep is not None:
            blk = jnp.where(lane < keep, blk, 0.0)
        wb_s[:, dst + t * LANE:dst + (t + 1) * LANE] = blk.astype(BF16)


def _in_proj_kernel(xp_ref, xs_ref, mod_ref, nw_ref, w_ref, zg_ref, zm_ref, zs_ref, wb_s, *, p_blocks):
    i = pl.program_id(0)

    @pl.when(i == 0)
    def _():
        for src, tiles, dst, keep in IN_PIECES:
            _pack_transposed(w_ref, wb_s, src, tiles, dst, keep)

    x = jnp.where(i < p_blocks, xp_ref[...], xs_ref[...])
    h = _mod_rmsnorm(x, nw_ref[...], mod_ref[0]).astype(BF16)
    z = jnp.dot(h, wb_s[...], preferred_element_type=F32)
    zg_ref[...] = z[:, ZG_BASE:ZG_BASE + ZG_W].astype(BF16)
    zm_ref[...] = z[:, ZM_BASE:ZM_BASE + ZM_W].astype(BF16)
    zs_ref[...] = z[:, ZS_BASE:ZS_BASE + ZS_W].astype(BF16)


def _two_group_rows(width, p_blocks):
    tm = ROW_TILE
    return (pl.BlockSpec((tm, width), lambda i: (jnp.minimum(i, p_blocks - 1), 0)),
            pl.BlockSpec((tm, width), lambda i: (jnp.maximum(i - p_blocks, 0), 0)))


def _in_proj(xp, xs, mod, mod_idx, nw, w_in_t, layer):
    tm = ROW_TILE
    p_blocks = xp.shape[0] // tm
    n = xp.shape[0] + xs.shape[0]
    return pl.pallas_call(
        functools.partial(_in_proj_kernel, p_blocks=p_blocks),
        grid=(n // tm,),
        in_specs=[
            *_two_group_rows(D_MODEL, p_blocks),
            pl.BlockSpec((None, 1, 1, 3 * D_MODEL), lambda i: (layer, mod_idx(i), 0, 0)),
            _layer_spec((1, D_MODEL), layer),
            pl.BlockSpec((None, MERGE_COL, D_MODEL), lambda i: (layer, 0, 0), pipeline_mode=pl.Buffered(1)),
        ],
        out_specs=[
            pl.BlockSpec((tm, ZG_W), lambda i: (i, 0)),
            pl.BlockSpec((tm, ZM_W), lambda i: (i, 0)),
            pl.BlockSpec((tm, ZS_W), lambda i: (i, 0)),
        ],
        out_shape=[
            jax.ShapeDtypeStruct((n, ZG_W), BF16),
            jax.ShapeDtypeStruct((n, ZM_W), BF16),
            jax.ShapeDtypeStruct((n, ZS_W), BF16),
        ],
        scratch_shapes=[pltpu.VMEM((D_MODEL, PACK_W), BF16)],
        compiler_params=_params(("arbitrary",)),
        name="in_proj",
    )(xp, xs, mod, nw, w_in_t)


def _gla_kernel(*refs, nsteps, seq, nseq, has_ctx):
    it = iter(refs)
    zg_ref = next(it)
    s0_ref = next(it) if has_ctx else None
    waf_ref, wab_ref, ba_ref, onorm_ref, o_ref, sfin_ref, la_s, o_s, st_s = (next(it) for _ in range(9))
    chains = [(g, d) for g in range(nseq) for d in (0, 1)]
    inv_norm = 1.0 / GLA_GATE_NORM
    zero_blk = jnp.zeros((GLA_DK, GLA_DV), F32)
    for ch, (g, d) in enumerate(chains):
        if d == 0:
            a_blk = zg_ref[g * seq:(g + 1) * seq, ZG_A:ZG_A + LANE]
        wa_ref = waf_ref if d == 0 else wab_ref
        a_low = _dot(a_blk, wa_ref[...]) + ba_ref[d:d + 1, :]
        la_s[ch] = (jnp.minimum(a_low, 0.0) - jnp.log(1.0 + jnp.exp(-jnp.abs(a_low)))) * inv_norm
        if has_ctx:
            s0 = s0_ref[g, d]
            rows_bd = []
            for h in range(GLA_HEADS):
                sh = s0[h * GLA_DK:(h + 1) * GLA_DK, :]
                rows_bd.append(jnp.concatenate([sh if h2 == h else zero_blk for h2 in range(GLA_HEADS)], axis=1))
            st_s[ch] = jnp.concatenate(rows_bd, axis=0).T
        else:
            st_s[ch] = jnp.zeros((GLA_WIDTH, GLA_QK), F32)

    def iota(shape, axis, shift):
        return lax.shift_right_logical(lax.broadcasted_iota(jnp.int32, shape, axis), shift)

    log_chunk, log_dv = CHUNK.bit_length() - 1, GLA_DV.bit_length() - 1
    row = lax.broadcasted_iota(jnp.int32, (GLA_STEP, GLA_STEP), 0)
    col = lax.broadcasted_iota(jnp.int32, (GLA_STEP, GLA_STEP), 1)
    same_chunk = iota((GLA_STEP, GLA_STEP), 0, log_chunk) == iota((GLA_STEP, GLA_STEP), 1, log_chunk)
    masks = (same_chunk & (row >= col), same_chunk & (row <= col))
    lane_head = iota((GLA_STEP, GLA_QK), 1, log_chunk)
    row_chunk = iota((GLA_STEP, GLA_QK), 0, log_chunk)
    state_blk = iota((GLA_WIDTH, GLA_QK), 0, log_dv) == iota((GLA_WIDTH, GLA_QK), 1, log_chunk)
    qscale = GLA_DK ** -0.5
    nch = GLA_STEP // CHUNK

    def chain_phases(i, ch, g, d):
        r0 = pl.multiple_of((i if d == 0 else nsteps - 1 - i) * GLA_STEP, GLA_STEP)
        zrows, rows = pl.ds(g * seq + r0, GLA_STEP), pl.ds(r0, GLA_STEP)
        a_hi, a_lo = _split_bf16(la_s[ch, rows, :], 2)
        tri = masks[d].astype(BF16)
        cum = (jnp.dot(tri, a_hi, preferred_element_type=F32)
               + jnp.dot(tri, a_lo, preferred_element_type=F32))
        yield
        edge = CHUNK - 1 if d == 0 else 0
        blast = [cum[c * CHUNK + edge:c * CHUNK + edge + 1, :] for c in range(nch)]
        bl = jnp.concatenate([jnp.broadcast_to(b, (CHUNK, GLA_QK)) for b in blast], axis=0)
        q = zg_ref[zrows, ZG_Q:ZG_Q + GLA_QK].astype(F32) * qscale
        k = zg_ref[zrows, ZG_K:ZG_K + GLA_QK].astype(F32)
        v = zg_ref[zrows, ZG_V:ZG_V + GLA_WIDTH]
        v_t = v.astype(F32).T.astype(BF16)
        qd = q * jnp.exp(cum)
        kd = (k * jnp.exp(-cum)).astype(BF16)
        kr = k * jnp.exp(bl - cum)
        yield
        outs = []
        for h in range(GLA_HEADS):
            qh = jnp.where(lane_head == h, qd, 0.0)
            att = _dot_nt(qh, kd)
            yield
            att = jnp.where(masks[d], att, 0.0)
            outs.append(_dot(att, v[:, h * GLA_DV:(h + 1) * GLA_DV]))
            yield
        s = st_s[ch]
        inter = [None] * nch
        for c in (range(nch) if d == 0 else reversed(range(nch))):
            inter[c] = _dot_nt(qd[c * CHUNK:(c + 1) * CHUNK, :], s)
            kv_t = _dot(v_t, jnp.where(row_chunk == c, kr, 0.0))
            yield
            s = s * jnp.exp(blast[c]) + jnp.where(state_blk, kv_t, 0.0)
            yield
        st_s[ch] = s
        o_s[ch, rows, :] = jnp.concatenate(outs, axis=1) + jnp.concatenate(inter, axis=0)

    def step(i, carry):
        _emit_skewed([chain_phases(i, ch, g, d) for ch, (g, d) in enumerate(chains)], GLA_SKEW)
        return carry

    lax.fori_loop(0, nsteps, step, 0)
    onorm = onorm_ref[...]
    for ch, (g, d) in enumerate(chains):
        s_fin = st_s[ch].T
        for h in range(GLA_HEADS):
            sfin_ref[g, d, h * GLA_DK:(h + 1) * GLA_DK, :] = (
                s_fin[h * GLA_DK:(h + 1) * GLA_DK, h * GLA_DV:(h + 1) * GLA_DV])
    for g in range(nseq):
        srows = slice(g * seq, (g + 1) * seq)
        o = o_s[2 * g] + o_s[2 * g + 1]
        gate = zg_ref[srows, ZG_GATE:ZG_GATE + GLA_WIDTH].astype(F32)
        for h in range(GLA_HEADS):
            vs = slice(h * GLA_DV, (h + 1) * GLA_DV)
            oh = o[:, vs]
            ms = jnp.mean(oh * oh, axis=-1, keepdims=True)
            o_ref[srows, vs] = oh * lax.rsqrt(ms + EPS) * onorm * jax.nn.silu(gate[:, vs])


def _gla(zg, row0, ctx, layer, waf, wab, ba, onorm, bsz, seq):
    nseq = max(GLA_SEQS_PER_STEP, GLA_ROWS_PER_STEP // seq)
    blk0 = row0 // (nseq * seq)
    in_specs = [pl.BlockSpec((nseq * seq, ZG_W), lambda b: (b + blk0, 0))]
    args = [zg]
    if ctx is not None:
        in_specs.append(pl.BlockSpec((nseq, None, 2, GLA_QK, GLA_DV), lambda b: (b, layer, 0, 0, 0)))
        args.append(ctx)
    in_specs += [
        _layer_spec((LANE, GLA_QK), layer),
        _layer_spec((LANE, GLA_QK), layer),
        _layer_spec((2, GLA_QK), layer),
        _layer_spec((1, GLA_DV), layer),
    ]
    return pl.pallas_call(
        functools.partial(_gla_kernel, nsteps=seq // GLA_STEP, seq=seq, nseq=nseq, has_ctx=ctx is not None),
        grid=(bsz // nseq,),
        in_specs=in_specs,
        out_specs=[
            pl.BlockSpec((nseq * seq, GLA_WIDTH), lambda b: (b, 0)),
            pl.BlockSpec((nseq, 2, GLA_QK, GLA_DV), lambda b: (b, 0, 0, 0)),
        ],
        out_shape=[
            jax.ShapeDtypeStruct((bsz * seq, GLA_WIDTH), F32),
            jax.ShapeDtypeStruct((bsz, 2, GLA_QK, GLA_DV), F32),
        ],
        scratch_shapes=[
            pltpu.VMEM((2 * nseq, seq, GLA_QK), F32),
            pltpu.VMEM((2 * nseq, seq, GLA_WIDTH), F32),
            pltpu.VMEM((2 * nseq, GLA_WIDTH, GLA_QK), F32),
        ],
        compiler_params=_params(("parallel",)),
        name="gla",
    )(*args, waf, wab, ba, onorm)


def _rms(x, w):
    ms = jnp.mean(x * x, axis=-1, keepdims=True)
    return x * lax.rsqrt(ms + EPS) * w


def _head_sums_mxu(x):
    width = x.shape[-1]
    shift = HEAD_PAD.bit_length() - 1
    gi = lax.shift_right_logical(lax.broadcasted_iota(jnp.int32, (width, width), 0), shift)
    gj = lax.shift_right_logical(lax.broadcasted_iota(jnp.int32, (width, width), 1), shift)
    return _dot(x * x, jnp.where(gi == gj, 1.0, 0.0))


def _head_norm(x, w, rope, on_mxu):
    sums = _head_sums_mxu(x) if on_mxu else None
    outs = []
    for h in range(MLA_HEADS):
        hs = slice(h * HEAD_PAD, (h + 1) * HEAD_PAD)
        xh = x[:, hs]
        ss = sums[:, hs] if on_mxu else jnp.sum(xh * xh, axis=-1, keepdims=True)
        yh = xh * lax.rsqrt(ss * (1.0 / MLA_QK) + EPS) * w
        if rope is not None:
            c, s = rope
            yh = yh * c + pltpu.roll(yh, ROPE_SHIFT, 1) * s
        outs.append(yh)
    return outs


def _place_rope_key(kr, e):
    return sum(jnp.dot(p, e, preferred_element_type=F32) for p in _split_bf16(kr, 3))


def _mla_kernel(*refs, seq, nseq, n_ctx, use_rope):
    it = iter(refs)
    zm_ref = next(it)
    if n_ctx:
        cckv_ref, ckr_ref = next(it), next(it)
    qn_ref, wuq_ref, kvn_ref, wuk_ref, wuv_ref, qhn_ref, khn_ref, e_ref = (next(it) for _ in range(8))
    rope_ref = next(it) if use_rope else None
    o_ref, ckv_ref = next(it), next(it)
    q_s, k_s, v_s = next(it), next(it), next(it)

    qscale = MLA_QK ** -0.5
    heads = [slice(h * HEAD_PAD, (h + 1) * HEAD_PAD) for h in range(MLA_HEADS)]

    def keys_values(g, ckv, k_rope_placed, rope, k_rows):
        k_raw = _dot(ckv, wuk_ref[...]) + k_rope_placed
        yield
        kh = _head_norm(k_raw, khn_ref[...], rope, False)
        for h, hs in enumerate(heads):
            k_s[g, k_rows, hs] = kh[h].astype(BF16)
        yield
        v_s[g, k_rows, :] = _dot(ckv, wuv_ref[...]).astype(BF16)
        yield

    def latent_phases(i, g):
        r0 = pl.multiple_of(i * PROJ_TILE, PROJ_TILE)
        tile, rows = pl.ds(r0, PROJ_TILE), pl.ds(g * seq + r0, PROJ_TILE)
        rope = (rope_ref[0, tile, :], rope_ref[1, tile, :]) if use_rope else None
        ckv = _rms(zm_ref[rows, ZM_KV:ZM_KV + MLA_KV_LORA].astype(F32), kvn_ref[...])
        ckv_ref[rows, :] = ckv
        k_pe = jnp.dot(zm_ref[rows, ZM_KR:ZM_KR + LANE], e_ref[...], preferred_element_type=F32)
        yield from keys_values(g, ckv, k_pe, rope, pl.ds(n_ctx + r0, PROJ_TILE))
        cq = _rms(zm_ref[rows, ZM_Q:ZM_Q + MLA_Q_LORA].astype(F32), qn_ref[...])
        q_raw = _dot(cq, wuq_ref[...])
        yield
        qh = _head_norm(q_raw, qhn_ref[...], rope, True)
        for h, hs in enumerate(heads):
            q_s[rows, hs] = (qh[h] * qscale).astype(BF16)

    def latent_tile(i, carry):
        _emit_skewed([latent_phases(i, g) for g in range(nseq)], 0)
        return carry

    lax.fori_loop(0, seq // PROJ_TILE, latent_tile, 0)

    def context_tile(i, carry):
        rows = pl.ds(pl.multiple_of(i * PROJ_TILE, PROJ_TILE), PROJ_TILE)
        _emit_skewed([keys_values(g, cckv_ref[g, rows, :], _place_rope_key(ckr_ref[g, rows, :], e_ref[...]),
                                  None, rows) for g in range(nseq)], 0)
        return carry

    if n_ctx:
        lax.fori_loop(0, n_ctx // PROJ_TILE, context_tile, 0)

    q_tile = min(seq, Q_TILE)

    def head_phases(g, h, hs, rows, gate):
        s = lax.dot_general(q_s[rows, hs], k_s[g, :, hs], (((1,), (1,)), ((), ())),
                            preferred_element_type=F32)
        yield
        e = jnp.exp(s - jnp.max(s, axis=-1, keepdims=True))
        l = jnp.sum(e, axis=-1, keepdims=True)
        p = e.astype(BF16)
        yield
        o = jnp.dot(p, v_s[g, :, hs], preferred_element_type=F32) / l
        o_ref[rows, hs] = o * jax.nn.silu(gate[:, hs])

    def q_block(i, carry):
        chains = []
        for g in range(nseq):
            rows = pl.ds(g * seq + pl.multiple_of(i * q_tile, q_tile), q_tile)
            gate = zm_ref[rows, ZM_GATE:ZM_GATE + MLA_WIDTH].astype(F32)
            chains += [head_phases(g, h, hs, rows, gate) for h, hs in enumerate(heads)]
        _emit_skewed(chains, 0)
        return carry

    lax.fori_loop(0, seq // q_tile, q_block, 0)


def _mla(zm, row0, ctx, layer, w, rope_tab, bsz, seq):
    n_ctx = 0 if ctx is None else ctx[0].shape[-2]
    nseq = max(1, MLA_ROWS_PER_STEP // seq)
    blk0 = row0 // (nseq * seq)
    in_specs = [pl.BlockSpec((nseq * seq, ZM_W), lambda b: (b + blk0, 0))]
    args = [zm]
    if ctx is not None:
        cckv, ckr = ctx
        in_specs += [
            pl.BlockSpec((nseq, None, n_ctx, MLA_KV_LORA), lambda b: (b, layer, 0, 0)),
            pl.BlockSpec((nseq, None, n_ctx, LANE), lambda b: (b, layer, 0, 0)),
        ]
        args += [cckv, ckr]
    in_specs += [
        _layer_spec((1, MLA_Q_LORA), layer),
        _layer_spec((MLA_Q_LORA, MLA_HEADS * HEAD_PAD), layer),
        _layer_spec((1, MLA_KV_LORA), layer),
        _layer_spec((MLA_KV_LORA, MLA_HEADS * HEAD_PAD), layer),
        _layer_spec((MLA_KV_LORA, MLA_WIDTH), layer),
        _layer_spec((1, HEAD_PAD), layer),
        _layer_spec((1, HEAD_PAD), layer),
        pl.BlockSpec((LANE, MLA_HEADS * HEAD_PAD), lambda b: (0, 0)),
    ]
    args += list(w)
    if rope_tab is not None:
        in_specs.append(pl.BlockSpec((2, seq, HEAD_PAD), lambda b: (0, 0, 0)))
        args.append(rope_tab)
    return pl.pallas_call(
        functools.partial(_mla_kernel, seq=seq, nseq=nseq, n_ctx=n_ctx, use_rope=rope_tab is not None),
        grid=(bsz // nseq,),
        in_specs=in_specs,
        out_specs=[
            pl.BlockSpec((nseq * seq, MLA_WIDTH), lambda b: (b, 0)),
            pl.BlockSpec((nseq * seq, MLA_KV_LORA), lambda b: (b, 0)),
        ],
        out_shape=[
            jax.ShapeDtypeStruct((bsz * seq, MLA_WIDTH), F32),
            jax.ShapeDtypeStruct((bsz * seq, MLA_KV_LORA), F32),
        ],
        scratch_shapes=[
            pltpu.VMEM((nseq * seq, MLA_HEADS * HEAD_PAD), BF16),
            pltpu.VMEM((nseq, n_ctx + seq, MLA_HEADS * HEAD_PAD), BF16),
            pltpu.VMEM((nseq, n_ctx + seq, MLA_WIDTH), BF16),
        ],
        compiler_params=_params(("parallel",)),
        name="mla",
    )(*args)


S5_T = 8
S5_R = CHUNK // S5_T
S5_SUB_CH = 64
S5_SUBS = LANE // S5_SUB_CH
S5_SUB_STATE = S5_TILE_STATE // S5_SUBS
S5_ROW = S5_T * S5_SUB_CH
S5_W = 2 * S5_SUB_STATE
W_M, W_SF, W_SB, W_CF, W_CB = range(5)


def _cmul(ar, ai, br, bi):
    return ar * br - ai * bi, ar * bi + ai * br


def _s5_prep_kernel(are_ref, aim_ref, ldt_ref, bre_ref, bim_ref, cre_ref, cim_ref, d_ref,
                    w_ref, tab8_ref, tab1_ref):
    gr = lax.shift_right_logical(lax.broadcasted_iota(jnp.int32, (S5_SUB_CH, S5_SUB_STATE), 0),
                                 S5_GROUP.bit_length() - 1)
    gc = lax.shift_right_logical(lax.broadcasted_iota(jnp.int32, (S5_SUB_CH, S5_SUB_STATE), 1),
                                 S5_STATE.bit_length() - 1)

    def spread(ref, h):
        x = ref[h * S5_SUB_CH:(h + 1) * S5_SUB_CH, :]
        return jnp.where(gr == gc, jnp.concatenate([x] * (S5_SUB_CH // S5_GROUP), axis=1), 0.0)

    row = lax.broadcasted_iota(jnp.int32, (S5_SUB_CH, S5_SUB_CH), 0)
    col = lax.broadcasted_iota(jnp.int32, (S5_SUB_CH, S5_SUB_CH), 1)
    taps = [[[], []] for _ in range(S5_SUBS)]
    for d in (0, 1):
        a_re, a_im = are_ref[d], aim_ref[d]
        dt = jnp.exp(ldt_ref[d])
        lam = a_re * dt
        th = a_im * dt
        mag = jnp.exp(lam)
        ab_re = mag * jnp.cos(th)
        ab_im = mag * jnp.sin(th)
        den = a_re * a_re + a_im * a_im
        n_re = ab_re - 1.0
        cf_re = (n_re * a_re + ab_im * a_im) / den
        cf_im = (ab_im * a_re - n_re * a_im) / den
        k = lax.broadcasted_iota(jnp.int32, (2 * S5_T, S5_TILE_STATE), 0).astype(F32)
        pmag = jnp.exp(k * lam)
        pw_re = pmag * jnp.cos(k * th)
        pw_im = pmag * jnp.sin(k * th)
        for h in range(S5_SUBS):
            ss = slice(h * S5_SUB_STATE, (h + 1) * S5_SUB_STATE)
            c_re, c_im = spread(cre_ref, h), spread(cim_ref, h)
            c_cat = jnp.concatenate([c_re, c_im], axis=1).astype(BF16)
            bp_re, bp_im = _cmul(spread(bre_ref, h), spread(bim_ref, h), cf_re[:, ss], cf_im[:, ss])
            for p in range(S5_T + 1):
                ar, ai = pw_re[p:p + 1, ss], pw_im[p:p + 1, ss]
                t_in = S5_T - 1 - p if d == 0 else p
                t_out = p - 1 if d == 0 else S5_T - p
                if p < S5_T:
                    l_re, l_im = _cmul(bp_re, bp_im, ar, ai)
                    w_ref[h, W_SF + d, t_in * S5_SUB_CH:(t_in + 1) * S5_SUB_CH, :] = (
                        jnp.concatenate([l_re, l_im], axis=1).astype(BF16))
                    taps[h][d].append(_dot_nt(jnp.concatenate([l_re, -l_im], axis=1), c_cat))
                if p > 0:
                    v_re, v_im = _cmul(c_re, c_im, ar, ai)
                    w_ref[h, W_CF + d, t_out * S5_SUB_CH:(t_out + 1) * S5_SUB_CH, :] = (
                        jnp.concatenate([v_re, -v_im], axis=1).astype(BF16))
        r = lax.broadcasted_iota(jnp.int32, (S5_R, S5_TILE_STATE), 0).astype(F32) * float(S5_T)
        r1 = r + float(S5_T)
        pm = jnp.exp(r * lam)
        qm = jnp.exp(-(r1 * lam))
        tab8_ref[d, 0] = pm * jnp.cos(r * th)
        tab8_ref[d, 1] = pm * jnp.sin(r * th)
        tab8_ref[d, 2] = qm * jnp.cos(r1 * th)
        tab8_ref[d, 3] = -(qm * jnp.sin(r1 * th))
        mc = jnp.exp(float(CHUNK) * lam)
        tab1_ref[d, 0:1, :] = mc * jnp.cos(float(CHUNK) * th)
        tab1_ref[d, 1:2, :] = mc * jnp.sin(float(CHUNK) * th)
    for h in range(S5_SUBS):
        skip = jnp.where(row == col, d_ref[:, h * S5_SUB_CH:(h + 1) * S5_SUB_CH], 0.0)
        for t in range(S5_T):
            blocks = []
            for t2 in range(S5_T):
                if t < t2:
                    blocks.append(taps[h][0][t2 - t])
                elif t > t2:
                    blocks.append(taps[h][1][t - t2])
                else:
                    blocks.append(taps[h][0][0] + taps[h][1][0] + skip)
            w_ref[h, W_M, t * S5_SUB_CH:(t + 1) * S5_SUB_CH, :] = jnp.concatenate(blocks, axis=1).astype(BF16)


def _s5_prep(a_re, a_im, ldt, b_re, b_im, c_re, c_im, dsk):
    vec = pl.BlockSpec((None, 2, 1, S5_TILE_STATE), lambda l, j: (l, 0, 0, j))
    blk = pl.BlockSpec((None, None, LANE, S5_STATE), lambda l, j: (l, j, 0, 0))
    return pl.pallas_call(
        _s5_prep_kernel,
        grid=(DEPTH, S5_TILES),
        in_specs=[vec, vec, vec, blk, blk, blk, blk,
                  pl.BlockSpec((None, 1, LANE), lambda l, j: (l, 0, j))],
        out_specs=[
            pl.BlockSpec((None, None, S5_SUBS, 5, S5_ROW, S5_W), lambda l, j: (l, j, 0, 0, 0, 0)),
            pl.BlockSpec((None, None, 2, 4, S5_R, S5_TILE_STATE), lambda l, j: (l, j, 0, 0, 0, 0)),
            pl.BlockSpec((None, None, 2, 2, S5_TILE_STATE), lambda l, j: (l, j, 0, 0, 0)),
        ],
        out_shape=[
            jax.ShapeDtypeStruct((DEPTH, S5_TILES, S5_SUBS, 5, S5_ROW, S5_W), BF16),
            jax.ShapeDtypeStruct((DEPTH, S5_TILES, 2, 4, S5_R, S5_TILE_STATE), F32),
            jax.ShapeDtypeStruct((DEPTH, S5_TILES, 2, 2, S5_TILE_STATE), F32),
        ],
        compiler_params=_params(("parallel", "parallel")),
        name="s5_prep",
    )(a_re, a_im, ldt, b_re, b_im, c_re, c_im, dsk)


def _s5_scan_kernel(u_ref, x0_ref, w_ref, tab8_ref, tab1_ref, y_ref, fs_ref, u_s, *, nseq, nb):
    groups = nseq * nb
    nrow = groups * S5_R
    ts = S5_SUB_STATE
    u_s[...] = u_ref[...].astype(F32)
    tokens = [u_s[pl.ds(t, nrow, stride=S5_T), :] for t in range(S5_T)]
    rowi = lax.broadcasted_iota(jnp.int32, (groups, S5_R, ts), 1)

    def prefix(x):
        for s in (1, 2, 4):
            x = x + jnp.where(rowi >= s, pltpu.roll(x, s, 1), 0.0)
        return x

    def suffix(x):
        for s in (1, 2, 4):
            x = x + jnp.where(rowi < S5_R - s, pltpu.roll(x, S5_R - s, 1), 0.0)
        return x

    y_sub = []
    for h in range(S5_SUBS):
        ch = slice(h * S5_SUB_CH, (h + 1) * S5_SUB_CH)
        ss = slice(h * ts, (h + 1) * ts)
        u8 = jnp.concatenate([tok[:, ch] for tok in tokens], axis=1).astype(BF16)
        ef = jnp.dot(u8, w_ref[h, W_SF], preferred_element_type=F32).reshape(groups, S5_R, S5_W)
        eb = jnp.dot(u8, w_ref[h, W_SB], preferred_element_type=F32).reshape(groups, S5_R, S5_W)

        p_re, p_im, q_re, q_im = (tab8_ref[0, i, :, ss] for i in range(4))
        a_re, a_im = tab1_ref[0, 0:1, ss], tab1_ref[0, 1:2, ss]
        w_re, w_im = _cmul(q_re, q_im, ef[:, :, :ts], ef[:, :, ts:])
        cs_re, cs_im = prefix(w_re), prefix(w_im)
        st_re, st_im = [], []
        for s in range(nseq):
            x_re, x_im = x0_ref[s, 0, 0:1, ss], x0_ref[s, 0, 1:2, ss]
            for b in range(nb):
                g = s * nb + b
                st_re.append(x_re)
                st_im.append(x_im)
                x_re, x_im = _cmul(a_re, a_im, x_re + cs_re[g, S5_R - 1:S5_R, :],
                                   x_im + cs_im[g, S5_R - 1:S5_R, :])
            fs_ref[s, 0, 0:1, ss] = x_re
            fs_ref[s, 0, 1:2, ss] = x_im
        xin_re, xin_im = _cmul(p_re, p_im, cs_re - w_re + jnp.stack(st_re), cs_im - w_im + jnp.stack(st_im))
        xin = jnp.concatenate([xin_re, xin_im], axis=2).reshape(nrow, S5_W)

        p_re, p_im, q_re, q_im = (tab8_ref[1, i, :, ss] for i in range(4))
        a_re, a_im = tab1_ref[1, 0:1, ss], tab1_ref[1, 1:2, ss]
        w_re, w_im = _cmul(p_re, p_im, eb[:, :, :ts], eb[:, :, ts:])
        sf_re, sf_im = suffix(w_re), suffix(w_im)
        z_re, z_im = [None] * groups, [None] * groups
        for s in range(nseq):
            x_re, x_im = x0_ref[s, 1, 0:1, ss], x0_ref[s, 1, 1:2, ss]
            for b in reversed(range(nb)):
                g = s * nb + b
                z_re[g], z_im[g] = _cmul(a_re, a_im, x_re, x_im)
                x_re = sf_re[g, 0:1, :] + z_re[g]
                x_im = sf_im[g, 0:1, :] + z_im[g]
            fs_ref[s, 1, 0:1, ss] = x_re
            fs_ref[s, 1, 1:2, ss] = x_im
        xnx_re, xnx_im = _cmul(q_re, q_im, sf_re - w_re + jnp.stack(z_re), sf_im - w_im + jnp.stack(z_im))
        xnx = jnp.concatenate([xnx_re, xnx_im], axis=2).reshape(nrow, S5_W)

        y_sub.append(jnp.dot(u8, w_ref[h, W_M], preferred_element_type=F32)
                     + _dot_nt(xin, w_ref[h, W_CF]) + _dot_nt(xnx, w_ref[h, W_CB]))
    for t in range(S5_T):
        tc = slice(t * S5_SUB_CH, (t + 1) * S5_SUB_CH)
        y_ref[pl.ds(t, nrow, stride=S5_T), :] = jnp.concatenate([y[:, tc] for y in y_sub], axis=1)


def _s5_scan(zs, row0, x0, x0_block, x0_idx, layer, wmat, tab8, tab1, nseq, seq):
    n = nseq * seq
    rblk = row0 // n
    return pl.pallas_call(
        functools.partial(_s5_scan_kernel, nseq=nseq, nb=seq // CHUNK),
        grid=(S5_TILES,),
        in_specs=[
            pl.BlockSpec((n, LANE), lambda j: (rblk, ZS_U // LANE + j)),
            pl.BlockSpec(x0_block, x0_idx),
            pl.BlockSpec((None, None, S5_SUBS, 5, S5_ROW, S5_W), lambda j: (layer, j, 0, 0, 0, 0)),
            pl.BlockSpec((None, None, 2, 4, S5_R, S5_TILE_STATE), lambda j: (layer, j, 0, 0, 0, 0)),
            pl.BlockSpec((None, None, 2, 2, S5_TILE_STATE), lambda j: (layer, j, 0, 0, 0)),
        ],
        out_specs=[
            pl.BlockSpec((n, LANE), lambda j: (0, j)),
            pl.BlockSpec((nseq, 2, 2, S5_TILE_STATE), lambda j: (0, 0, 0, j)),
        ],
        out_shape=[
            jax.ShapeDtypeStruct((n, S5_WIDTH), F32),
            jax.ShapeDtypeStruct((nseq, 2, 2, S5_NSTATE), F32),
        ],
        scratch_shapes=[pltpu.VMEM((n, LANE), F32)],
        compiler_params=_params(("parallel",)),
        name="s5_scan",
    )(zs, x0, wmat, tab8, tab1)


def _merge_kernel(x_ref, mod_ref, nw_ref, oa_ref, ob_ref, ys_ref, sg_ref, wglu_ref, bglu_ref, wmg_ref,
                  wa_ref, wb_ref, wc_ref, wout_ref, y_ref, wmg_s):
    @pl.when(pl.program_id(0) == 0)
    def _():
        _pack_transposed(wmg_ref.at[0], wmg_s, 0, 3 * D_MODEL // LANE, 0, None)

    x = x_ref[...]
    mod = mod_ref[0]
    h = _mod_rmsnorm(x, nw_ref[...], mod).astype(BF16)
    zg = _dot(jax.nn.gelu(ys_ref[...]), wglu_ref[...]) + bglu_ref[...]
    oc = (zg[:, :S5_WIDTH] * jax.nn.sigmoid(zg[:, S5_WIDTH:])
          * jax.nn.silu(sg_ref[...].astype(F32)))
    mixed = None
    for br, (o_br, w_ref) in enumerate(((oa_ref[...], wa_ref), (ob_ref[...], wb_ref), (oc, wc_ref))):
        gate = jax.nn.sigmoid(jnp.dot(h, wmg_s[:, br * D_MODEL:(br + 1) * D_MODEL], preferred_element_type=F32))
        term = gate * _dot(o_br, w_ref[...])
        mixed = term if mixed is None else mixed + term
    y_ref[...] = x + mod[:, 2 * D_MODEL:] * _dot(mixed, wout_ref[...])


def _merge(x2, row0, mod, mod_idx, nw, oa, ob, ys, zs, layer, wglu, bglu, w_in_t, wa, wb, wc, wout):
    n = x2.shape[0]
    tm = ROW_TILE
    blk0 = row0 // tm
    rows = lambda w: pl.BlockSpec((tm, w), lambda i: (i, 0))
    return pl.pallas_call(
        _merge_kernel,
        grid=(n // tm,),
        in_specs=[
            rows(D_MODEL),
            pl.BlockSpec((None, 1, 1, 3 * D_MODEL), lambda i: (layer, mod_idx(i), 0, 0)),
            _layer_spec((1, D_MODEL), layer),
            rows(GLA_WIDTH), rows(MLA_WIDTH), rows(S5_WIDTH),
            pl.BlockSpec((tm, S5_WIDTH), lambda i: (i + blk0, ZS_GATE // S5_WIDTH)),
            pl.BlockSpec((None, S5_WIDTH, 2 * S5_WIDTH), lambda i: (layer, 0, 0)),
            pl.BlockSpec((None, 1, 2 * S5_WIDTH), lambda i: (layer, 0, 0)),
            pl.BlockSpec((pl.Element(1), pl.Element(3 * D_MODEL), pl.Element(D_MODEL)),
                         lambda i: (layer, MERGE_COL, 0), pipeline_mode=pl.Buffered(1)),
            _layer_spec((GLA_WIDTH, D_MODEL), layer),
            _layer_spec((MLA_WIDTH, D_MODEL), layer),
            _layer_spec((S5_WIDTH, D_MODEL), layer),
            _layer_spec((D_MODEL, D_MODEL), layer),
        ],
        out_specs=rows(D_MODEL),
        out_shape=jax.ShapeDtypeStruct((n, D_MODEL), F32),
        scratch_shapes=[pltpu.VMEM((D_MODEL, 3 * D_MODEL), BF16)],
        compiler_params=_params(("arbitrary",)),
        name="merge",
    )(x2, mod, nw, oa, ob, ys, zs, wglu, bglu, w_in_t, wa, wb, wc, wout)


def _mla_lane_of_dim():
    half = MLA_ROPE // 2
    first_gap = ROPE_SHIFT - half
    lane = np.zeros(MLA_QK, np.int32)
    for j in range(MLA_NOPE):
        lane[j] = half + j if j < first_gap else 2 * half + j
    for r in range(half):
        lane[MLA_NOPE + r] = r
        lane[MLA_NOPE + half + r] = ROPE_SHIFT + r
    return lane


MLA_LANE_OF_DIM = _mla_lane_of_dim()


def _place_heads(w, heads, lane_of_dim):
    width = len(lane_of_dim)
    src = np.zeros(heads * HEAD_PAD, np.int32)
    used = np.zeros(heads * HEAD_PAD, bool)
    for h in range(heads):
        src[h * HEAD_PAD + lane_of_dim] = h * width + np.arange(width)
        used[h * HEAD_PAD + lane_of_dim] = True
    return jnp.where(jnp.asarray(used), jnp.take(w, jnp.asarray(src), axis=-1), 0.0)


def _rope_tables(n_tok):
    rows = n_tok // GRID_W
    r = jnp.repeat(jnp.arange(rows, dtype=F32), GRID_W)
    col = jnp.tile(jnp.arange(GRID_W, dtype=F32), rows)
    n_freq = MLA_ROPE // 4
    inv = ROPE_THETA ** (-jnp.arange(n_freq, dtype=F32) / n_freq)
    ang = jnp.concatenate([r[:, None] * inv, col[:, None] * inv], axis=-1)
    cos, sin = jnp.cos(ang), jnp.sin(ang)
    ones = jnp.ones((n_tok, MLA_NOPE), F32)
    c = _place_heads(jnp.concatenate([ones, cos, cos], axis=1), 1, MLA_LANE_OF_DIM)
    s = _place_heads(jnp.concatenate([0.0 * ones, -sin, sin], axis=1), 1, MLA_LANE_OF_DIM)
    return jnp.stack([c, s])


def kernel(x_prompt, x_sample, c, c_ctx, cache_mla_ckv, cache_mla_krope, state_gla, state_s5,
           norm_w, w_ada, b_ada, w_in, gla_w_a2, gla_b_a, gla_o_norm,
           mla_q_norm, mla_w_uq, mla_kv_norm, mla_w_uk, mla_w_uv, mla_qh_norm, mla_kh_norm,
           s5_a_re, s5_a_im, s5_log_dt, s5_b_re, s5_b_im, s5_c_re, s5_c_im, s5_d, s5_w_glu, s5_b_glu,
           w_bo_gla, w_bo_mla, w_bo_s5, w_out):
    bsz, seq, _ = x_prompt.shape
    dbsz, dseq, _ = x_sample.shape
    ctx_row = 8 - 1
    assert dbsz <= ctx_row and (bsz * seq) % ROW_TILE == 0 and dseq % ROW_TILE == 0

    cond8 = jnp.zeros((8, D_MODEL), F32).at[0:dbsz].set(c).at[ctx_row].set(c_ctx)
    ada = _ada(cond8, w_ada, b_ada)

    vec = lambda a: a.reshape(DEPTH, 2, 1, S5_NSTATE)
    ldt = jnp.repeat(s5_log_dt[..., None], S5_STATE, axis=-1)
    rows_gp = lambda t: t.reshape(DEPTH, S5_TILES, LANE, S5_STATE)
    bt = lambda b: rows_gp(b.transpose(0, 1, 3, 2))
    wmat, tab8, tab1 = _s5_prep(vec(s5_a_re), vec(s5_a_im), vec(ldt), bt(s5_b_re), bt(s5_b_im),
                                rows_gp(s5_c_re), rows_gp(s5_c_im), s5_d.reshape(DEPTH, 1, S5_WIDTH))
    wglu = s5_w_glu.astype(BF16)
    bglu = s5_b_glu.reshape(DEPTH, 1, 2 * S5_WIDTH)

    wuq = _place_heads(mla_w_uq, MLA_HEADS, MLA_LANE_OF_DIM).astype(BF16)
    wuk = _place_heads(mla_w_uk, MLA_HEADS, MLA_LANE_OF_DIM[:MLA_NOPE]).astype(BF16)
    wuv = mla_w_uv.astype(BF16)
    qhn = _place_heads(mla_qh_norm, 1, MLA_LANE_OF_DIM)
    khn = _place_heads(mla_kh_norm, 1, MLA_LANE_OF_DIM)
    e_np = np.zeros((LANE, MLA_HEADS * HEAD_PAD), np.float32)
    for h in range(MLA_HEADS):
        for i in range(MLA_ROPE):
            e_np[i, h * HEAD_PAD + MLA_LANE_OF_DIM[MLA_NOPE + i]] = 1.0
    e_place = jnp.asarray(e_np, BF16)
    rope_tab = _rope_tables(dseq)
    ckr_pad = jnp.pad(cache_mla_krope, ((0, 0), (0, 0), (0, 0), (0, LANE - MLA_ROPE)))

    zrow = lambda n: jnp.zeros((DEPTH, n, GLA_QK), F32)
    waf = jnp.concatenate([gla_w_a2[:, 0], zrow(LANE - GLA_RANK)], axis=1).astype(BF16)
    wab = jnp.concatenate([zrow(GLA_RANK), gla_w_a2[:, 1], zrow(LANE - 2 * GLA_RANK)], axis=1).astype(BF16)
    sgla = state_gla.reshape(dbsz, DEPTH, 2, GLA_QK, GLA_DV)
    ss5 = state_s5.reshape(dbsz, DEPTH, 2, 2, S5_NSTATE)
    zero_s5 = jnp.zeros((bsz, 2, 2, S5_NSTATE), F32)

    hp = x_prompt.reshape(bsz * seq, D_MODEL)
    hs = x_sample.reshape(dbsz * dseq, D_MODEL)
    ckv_l, krope_l, gla_l, s5_l = [], [], [], []
    w_in_t = jnp.swapaxes(w_in, 1, 2)
    mod = ada.reshape(DEPTH, 8, 1, 3 * D_MODEL)
    nw = norm_w.reshape(DEPTH, 1, D_MODEL)
    row = lambda t: t.reshape(DEPTH, 1, -1)
    mla_w = (row(mla_q_norm), wuq, row(mla_kv_norm), wuk, wuv, row(qhn), row(khn), e_place)
    wbo = (w_bo_gla.astype(BF16), w_bo_mla.astype(BF16), w_bo_s5.astype(BF16))
    wout = w_out.astype(BF16)
    onorm = gla_o_norm.reshape(DEPTH, 1, GLA_DV)
    for l in range(DEPTH):
        p_rows, p_blocks, blocks_per_seq = bsz * seq, bsz * seq // ROW_TILE, dseq // ROW_TILE
        mod_idx = lambda i: jnp.where(i < p_blocks, ctx_row, (i - p_blocks) // blocks_per_seq)
        zg, zm, zs = _in_proj(hp, hs, mod, mod_idx, nw, w_in_t, l)

        def mixers(x2, row0, nb, n, ctx):
            if ctx:
                gctx = sgla
                x0, x0_blk = ss5, (nb, None, 2, 2, S5_TILE_STATE)
                x0_idx = lambda j: (0, l, 0, 0, j)
                mctx, rt = (cache_mla_ckv, ckr_pad), rope_tab
            else:
                gctx = None
                x0, x0_blk = zero_s5, (nb, 2, 2, S5_TILE_STATE)
                x0_idx = lambda j: (0, 0, 0, j)
                mctx, rt = None, None
            oa, st_gla = _gla(zg, row0, gctx, l, waf, wab, gla_b_a, onorm, nb, n)
            ob, ckv = _mla(zm, row0, mctx, l, mla_w, rt, nb, n)
            y_ssm, st_s5 = _s5_scan(zs, row0, x0, x0_blk, x0_idx, l, wmat, tab8, tab1, nb, n)
            grp_mod_idx = lambda i: mod_idx(i + row0 // ROW_TILE)
            y = _merge(x2, row0, mod, grp_mod_idx, nw, oa, ob, y_ssm, zs, l, wglu, bglu, w_in_t, *wbo, wout)
            return y, ckv, st_gla, st_s5

        hp_next, ckv_p, st_gla_p, st_s5_p = mixers(hp, 0, bsz, seq, False)
        hs = mixers(hs, p_rows, dbsz, dseq, True)[0]
        hp = hp_next
        ckv_l.append(ckv_p.reshape(bsz, seq, MLA_KV_LORA))
        krope_l.append(zm[:p_rows, ZM_KR:ZM_KR + MLA_ROPE].astype(F32).reshape(bsz, seq, MLA_ROPE))
        gla_l.append(st_gla_p.reshape(bsz, 2, GLA_HEADS, GLA_DK, GLA_DV))
        s5_l.append(st_s5_p.reshape(bsz, 2, 2, S5_GROUPS, S5_STATE))

    return (hp.reshape(bsz, seq, D_MODEL), hs.reshape(dbsz, dseq, D_MODEL),
            jnp.stack(ckv_l, axis=1), jnp.stack(krope_l, axis=1),
            jnp.stack(gla_l, axis=1), jnp.stack(s5_l, axis=1))
```

```python
import functools

import jax
import jax.numpy as jnp
import numpy as np
from jax import lax
from jax.experimental import pallas as pl
from jax.experimental.pallas import tpu as pltpu

F32 = jnp.float32
BF16 = jnp.bfloat16

EPS = 1e-6
D_MODEL = 1024
DEPTH = 2
GRID_W = 64
ROPE_THETA = 10000.0
GLA_HEADS = 4
GLA_DK = 64
GLA_DV = 128
GLA_RANK = 16
GLA_GATE_NORM = 16.0
GLA_QK = GLA_HEADS * GLA_DK
GLA_WIDTH = GLA_HEADS * GLA_DV
MLA_HEADS = 4
MLA_Q_LORA = 384
MLA_KV_LORA = 256
MLA_NOPE = 64
MLA_ROPE = 32
MLA_QK = MLA_NOPE + MLA_ROPE
MLA_DV = 128
MLA_WIDTH = MLA_HEADS * MLA_DV
S5_WIDTH = 512
S5_GROUP = 16
S5_GROUPS = 32
S5_STATE = 64
S5_NSTATE = S5_GROUPS * S5_STATE

LANE = 128
SUBLANE = 8
COND_ROWS = SUBLANE
HEAD_PAD = LANE
ROPE_SHIFT = LANE // 2
CHUNK = 64
GLA_STEP = 256
GLA_SEQS_PER_STEP = 2
GLA_ROWS_PER_STEP = 1024
GLA_SKEW = 1
S5_TILES = S5_WIDTH // LANE
S5_TILE_STATE = S5_NSTATE // S5_TILES
ROW_TILE = 512
Q_TILE = 512
PROJ_TILE = 256
MLA_ROWS_PER_STEP = 1024
VMEM_LIMIT = 56 * 1024 * 1024

IN_SPLITS = (GLA_QK, GLA_QK, GLA_WIDTH, GLA_RANK, GLA_RANK, GLA_WIDTH,
             MLA_Q_LORA, MLA_KV_LORA, MLA_ROPE, MLA_WIDTH,
             S5_WIDTH, S5_WIDTH, 3 * D_MODEL)
(IN_GQ, IN_GK, IN_GV, IN_GA, IN_GAB, IN_GG, IN_MQ, IN_MKV, IN_MKR, IN_MG, IN_SU, IN_SG,
 MERGE_COL, D_IN) = (int(c) for c in np.cumsum((0,) + IN_SPLITS))
ZG_Q, ZG_K, ZG_V, ZG_A = 0, GLA_QK, 2 * GLA_QK, 2 * GLA_QK + GLA_WIDTH
ZG_GATE = ZG_A + LANE
ZG_W = ZG_GATE + GLA_WIDTH
ZM_Q, ZM_KV, ZM_KR = 0, MLA_Q_LORA, MLA_Q_LORA + MLA_KV_LORA
ZM_GATE = ZM_KR + LANE
ZM_W = ZM_GATE + MLA_WIDTH
ZS_U, ZS_GATE, ZS_W = 0, S5_WIDTH, 2 * S5_WIDTH
ZG_BASE, ZM_BASE, ZS_BASE, PACK_W = 0, ZG_W, ZG_W + ZM_W, ZG_W + ZM_W + ZS_W
IN_PIECES = (
    (IN_GQ, (IN_GA - IN_GQ) // LANE, ZG_BASE + ZG_Q, None),
    (IN_GA, 1, ZG_BASE + ZG_A, 2 * GLA_RANK),
    (IN_GG, GLA_WIDTH // LANE, ZG_BASE + ZG_GATE, None),
    (IN_MQ, (IN_MKR - IN_MQ) // LANE, ZM_BASE + ZM_Q, None),
    (IN_MKR, 1, ZM_BASE + ZM_KR, MLA_ROPE),
    (IN_MG, MLA_WIDTH // LANE, ZM_BASE + ZM_GATE, None),
    (IN_SU, (MERGE_COL - IN_SU) // LANE, ZS_BASE + ZS_U, None),
)


def _dot(a, b):
    return jnp.dot(a.astype(BF16), b.astype(BF16), preferred_element_type=F32)


def _dot_nt(a, b):
    return lax.dot_general(a.astype(BF16), b.astype(BF16), (((1,), (1,)), ((), ())),
                           preferred_element_type=F32)


def _split_bf16(x, parts):
    out = []
    r = x
    for _ in range(parts):
        p = r.astype(BF16)
        out.append(p)
        r = r - p.astype(F32)
    return out


def _emit_skewed(chains, skew):
    pending, active, tick = list(chains), [], 0
    while pending or active:
        while pending and (skew == 0 or tick % skew == 0):
            active.append(pending.pop(0))
            if skew:
                break
        for gen in list(active):
            if next(gen, "done") == "done":
                active.remove(gen)
        tick += 1


def _layer_spec(shape, layer):
    return pl.BlockSpec((None,) + tuple(shape), lambda *_: (layer,) + (0,) * len(shape))


def _params(sem):
    return pltpu.CompilerParams(dimension_semantics=sem, vmem_limit_bytes=VMEM_LIMIT)


def _ada_kernel(c_ref, w_ref, b_ref, o_ref):
    s = jax.nn.silu(c_ref[...])
    o_ref[...] = _dot(s, w_ref[...]) + b_ref[...]


def _ada(cond8, w_ada, b_ada):
    tn = 1024
    return pl.pallas_call(
        _ada_kernel,
        grid=(DEPTH, 3 * D_MODEL // tn),
        in_specs=[
            pl.BlockSpec((COND_ROWS, D_MODEL), lambda l, n: (0, 0)),
            pl.BlockSpec((None, D_MODEL, tn), lambda l, n: (l, 0, n)),
            pl.BlockSpec((None, 1, tn), lambda l, n: (l, 0, n)),
        ],
        out_specs=pl.BlockSpec((None, COND_ROWS, tn), lambda l, n: (l, 0, n)),
        out_shape=jax.ShapeDtypeStruct((DEPTH, COND_ROWS, 3 * D_MODEL), F32),
        compiler_params=_params(("parallel", "parallel")),
        name="ada",
    )(cond8, w_ada, b_ada.reshape(DEPTH, 1, 3 * D_MODEL))


def _mod_rmsnorm(x, nw, mod):
    ms = jnp.mean(x * x, axis=-1, keepdims=True)
    y = x * lax.rsqrt(ms + EPS) * nw
    return y * (1.0 + mod[:, D_MODEL:2 * D_MODEL]) + mod[:, 0:D_MODEL]


def _pack_transposed(w_ref, wb_s, src, tiles, dst, keep):
    lane = lax.broadcasted_iota(jnp.int32, (D_MODEL, LANE), 1)
    for t in range(tiles):
        blk = w_ref[src + t * LANE:src + (t + 1) * LANE, :].T
        if keep is not None:
            blk = jnp.where(lane < keep, blk, 0.0)
        wb_s[:, dst + t * LANE:dst + (t + 1) * LANE] = blk.astype(BF16)


def _in_proj_kernel(xp_ref, xs_ref, mod_ref, nw_ref, w_ref, zg_ref, zm_ref, zs_ref, wb_s, *, p_blocks):
    i = pl.program_id(0)

    @pl.when(i == 0)
    def _():
        for src, tiles, dst, keep in IN_PIECES:
            _pack_transposed(w_ref, wb_s, src, tiles, dst, keep)

    x = jnp.where(i < p_blocks, xp_ref[...], xs_ref[...])
    h = _mod_rmsnorm(x, nw_ref[...], mod_ref[0]).astype(BF16)
    z = jnp.dot(h, wb_s[...], preferred_element_type=F32)
    zg_ref[...] = z[:, ZG_BASE:ZG_BASE + ZG_W].astype(BF16)
    zm_ref[...] = z[:, ZM_BASE:ZM_BASE + ZM_W].astype(BF16)
    zs_ref[...] = z[:, ZS_BASE:ZS_BASE + ZS_W].astype(BF16)


def _two_group_rows(width, p_blocks):
    tm = ROW_TILE
    return (pl.BlockSpec((tm, width), lambda i: (jnp.minimum(i, p_blocks - 1), 0)),
            pl.BlockSpec((tm, width), lambda i: (jnp.maximum(i - p_blocks, 0), 0)))


def _in_proj(xp, xs, mod, mod_idx, nw, w_in_t, layer):
    tm = ROW_TILE
    p_blocks = xp.shape[0] // tm
    n = xp.shape[0] + xs.shape[0]
    return pl.pallas_call(
        functools.partial(_in_proj_kernel, p_blocks=p_blocks),
        grid=(n // tm,),
        in_specs=[
            *_two_group_rows(D_MODEL, p_blocks),
            pl.BlockSpec((None, 1, 1, 3 * D_MODEL), lambda i: (layer, mod_idx(i), 0, 0)),
            _layer_spec((1, D_MODEL), layer),
            pl.BlockSpec((None, MERGE_COL, D_MODEL), lambda i: (layer, 0, 0), pipeline_mode=pl.Buffered(1)),
        ],
        out_specs=[
            pl.BlockSpec((tm, ZG_W), lambda i: (i, 0)),
            pl.BlockSpec((tm, ZM_W), lambda i: (i, 0)),
            pl.BlockSpec((tm, ZS_W), lambda i: (i, 0)),
        ],
        out_shape=[
            jax.ShapeDtypeStruct((n, ZG_W), BF16),
            jax.ShapeDtypeStruct((n, ZM_W), BF16),
            jax.ShapeDtypeStruct((n, ZS_W), BF16),
        ],
        scratch_shapes=[pltpu.VMEM((D_MODEL, PACK_W), BF16)],
        compiler_params=_params(("arbitrary",)),
        name="in_proj",
    )(xp, xs, mod, nw, w_in_t)


def _gla_kernel(*refs, nsteps, seq, nseq, has_ctx):
    it = iter(refs)
    zg_ref = next(it)
    s0_ref = next(it) if has_ctx else None
    waf_ref, wab_ref, ba_ref, onorm_ref, o_ref, sfin_ref, la_s, o_s, st_s = (next(it) for _ in range(9))
    chains = [(g, d) for g in range(nseq) for d in (0, 1)]
    inv_norm = 1.0 / GLA_GATE_NORM
    zero_blk = jnp.zeros((GLA_DK, GLA_DV), F32)
    for ch, (g, d) in enumerate(chains):
        if d == 0:
            a_blk = zg_ref[g * seq:(g + 1) * seq, ZG_A:ZG_A + LANE]
        wa_ref = waf_ref if d == 0 else wab_ref
        a_low = _dot(a_blk, wa_ref[...]) + ba_ref[d:d + 1, :]
        la_s[ch] = (jnp.minimum(a_low, 0.0) - jnp.log(1.0 + jnp.exp(-jnp.abs(a_low)))) * inv_norm
        if has_ctx:
            s0 = s0_ref[g, d]
            rows_bd = []
            for h in range(GLA_HEADS):
                sh = s0[h * GLA_DK:(h + 1) * GLA_DK, :]
                rows_bd.append(jnp.concatenate([sh if h2 == h else zero_blk for h2 in range(GLA_HEADS)], axis=1))
            st_s[ch] = jnp.concatenate(rows_bd, axis=0).T
        else:
            st_s[ch] = jnp.zeros((GLA_WIDTH, GLA_QK), F32)

    def iota(shape, axis, shift):
        return lax.shift_right_logical(lax.broadcasted_iota(jnp.int32, shape, axis), shift)

    log_chunk, log_dv = CHUNK.bit_length() - 1, GLA_DV.bit_length() - 1
    row = lax.broadcasted_iota(jnp.int32, (GLA_STEP, GLA_STEP), 0)
    col = lax.broadcasted_iota(jnp.int32, (GLA_STEP, GLA_STEP), 1)
    same_chunk = iota((GLA_STEP, GLA_STEP), 0, log_chunk) == iota((GLA_STEP, GLA_STEP), 1, log_chunk)
    masks = (same_chunk & (row >= col), same_chunk & (row <= col))
    lane_head = iota((GLA_STEP, GLA_QK), 1, log_chunk)
    row_chunk = iota((GLA_STEP, GLA_QK), 0, log_chunk)
    state_blk = iota((GLA_WIDTH, GLA_QK), 0, log_dv) == iota((GLA_WIDTH, GLA_QK), 1, log_chunk)
    qscale = GLA_DK ** -0.5
    nch = GLA_STEP // CHUNK

    def chain_phases(i, ch, g, d):
        r0 = pl.multiple_of((i if d == 0 else nsteps - 1 - i) * GLA_STEP, GLA_STEP)
        zrows, rows = pl.ds(g * seq + r0, GLA_STEP), pl.ds(r0, GLA_STEP)
        a_hi, a_lo = _split_bf16(la_s[ch, rows, :], 2)
        tri = masks[d].astype(BF16)
        cum = (jnp.dot(tri, a_hi, preferred_element_type=F32)
               + jnp.dot(tri, a_lo, preferred_element_type=F32))
        yield
        edge = CHUNK - 1 if d == 0 else 0
        blast = [cum[c * CHUNK + edge:c * CHUNK + edge + 1, :] for c in range(nch)]
        bl = jnp.concatenate([jnp.broadcast_to(b, (CHUNK, GLA_QK)) for b in blast], axis=0)
        q = zg_ref[zrows, ZG_Q:ZG_Q + GLA_QK].astype(F32) * qscale
        k = zg_ref[zrows, ZG_K:ZG_K + GLA_QK].astype(F32)
        v = zg_ref[zrows, ZG_V:ZG_V + GLA_WIDTH]
        v_t = v.astype(F32).T.astype(BF16)
        qd = q * jnp.exp(cum)
        kd = (k * jnp.exp(-cum)).astype(BF16)
        kr = k * jnp.exp(bl - cum)
        yield
        outs = []
        for h in range(GLA_HEADS):
            qh = jnp.where(lane_head == h, qd, 0.0)
            att = _dot_nt(qh, kd)
            yield
            att = jnp.where(masks[d], att, 0.0)
            outs.append(_dot(att, v[:, h * GLA_DV:(h + 1) * GLA_DV]))
            yield
        s = st_s[ch]
        inter = [None] * nch
        for c in (range(nch) if d == 0 else reversed(range(nch))):
            inter[c] = _dot_nt(qd[c * CHUNK:(c + 1) * CHUNK, :], s)
            kv_t = _dot(v_t, jnp.where(row_chunk == c, kr, 0.0))
            yield
            s = s * jnp.exp(blast[c]) + jnp.where(state_blk, kv_t, 0.0)
            yield
        st_s[ch] = s
        o_s[ch, rows, :] = jnp.concatenate(outs, axis=1) + jnp.concatenate(inter, axis=0)

    def step(i, carry):
        _emit_skewed([chain_phases(i, ch, g, d) for ch, (g, d) in enumerate(chains)], GLA_SKEW)
        return carry

    lax.fori_loop(0, nsteps, step, 0)
    onorm = onorm_ref[...]
    for ch, (g, d) in enumerate(chains):
        s_fin = st_s[ch].T
        for h in range(GLA_HEADS):
            sfin_ref[g, d, h * GLA_DK:(h + 1) * GLA_DK, :] = (
                s_fin[h * GLA_DK:(h + 1) * GLA_DK, h * GLA_DV:(h + 1) * GLA_DV])
    for g in range(nseq):
        srows = slice(g * seq, (g + 1) * seq)
        o = o_s[2 * g] + o_s[2 * g + 1]
        gate = zg_ref[srows, ZG_GATE:ZG_GATE + GLA_WIDTH].astype(F32)
        for h in range(GLA_HEADS):
            vs = slice(h * GLA_DV, (h + 1) * GLA_DV)
            oh = o[:, vs]
            ms = jnp.mean(oh * oh, axis=-1, keepdims=True)
            o_ref[srows, vs] = oh * lax.rsqrt(ms + EPS) * onorm * jax.nn.silu(gate[:, vs])


def _gla(zg, row0, ctx, layer, waf, wab, ba, onorm, bsz, seq):
    nseq = max(GLA_SEQS_PER_STEP, GLA_ROWS_PER_STEP // seq)
    blk0 = row0 // (nseq * seq)
    in_specs = [pl.BlockSpec((nseq * seq, ZG_W), lambda b: (b + blk0, 0))]
    args = [zg]
    if ctx is not None:
        in_specs.append(pl.BlockSpec((nseq, None, 2, GLA_QK, GLA_DV), lambda b: (b, layer, 0, 0, 0)))
        args.append(ctx)
    in_specs += [
        _layer_spec((LANE, GLA_QK), layer),
        _layer_spec((LANE, GLA_QK), layer),
        _layer_spec((2, GLA_QK), layer),
        _layer_spec((1, GLA_DV), layer),
    ]
    return pl.pallas_call(
        functools.partial(_gla_kernel, nsteps=seq // GLA_STEP, seq=seq, nseq=nseq, has_ctx=ctx is not None),
        grid=(bsz // nseq,),
        in_specs=in_specs,
        out_specs=[
            pl.BlockSpec((nseq * seq, GLA_WIDTH), lambda b: (b, 0)),
            pl.BlockSpec((nseq, 2, GLA_QK, GLA_DV), lambda b: (b, 0, 0, 0)),
        ],
        out_shape=[
            jax.ShapeDtypeStruct((bsz * seq, GLA_WIDTH), F32),
            jax.ShapeDtypeStruct((bsz, 2, GLA_QK, GLA_DV), F32),
        ],
        scratch_shapes=[
            pltpu.VMEM((2 * nseq, seq, GLA_QK), F32),
            pltpu.VMEM((2 * nseq, seq, GLA_WIDTH), F32),
            pltpu.VMEM((2 * nseq, GLA_WIDTH, GLA_QK), F32),
        ],
        compiler_params=_params(("parallel",)),
        name="gla",
    )(*args, waf, wab, ba, onorm)


def _rms(x, w):
    ms = jnp.mean(x * x, axis=-1, keepdims=True)
    return x * lax.rsqrt(ms + EPS) * w


def _head_sums_mxu(x):
    width = x.shape[-1]
    shift = HEAD_PAD.bit_length() - 1
    gi = lax.shift_right_logical(lax.broadcasted_iota(jnp.int32, (width, width), 0), shift)
    gj = lax.shift_right_logical(lax.broadcasted_iota(jnp.int32, (width, width), 1), shift)
    return _dot(x * x, jnp.where(gi == gj, 1.0, 0.0))


def _head_norm(x, w, rope, on_mxu):
    sums = _head_sums_mxu(x) if on_mxu else None
    outs = []
    for h in range(MLA_HEADS):
        hs = slice(h * HEAD_PAD, (h + 1) * HEAD_PAD)
        xh = x[:, hs]
        ss = sums[:, hs] if on_mxu else jnp.sum(xh * xh, axis=-1, keepdims=True)
        yh = xh * lax.rsqrt(ss * (1.0 / MLA_QK) + EPS) * w
        if rope is not None:
            c, s = rope
            yh = yh * c + pltpu.roll(yh, ROPE_SHIFT, 1) * s
        outs.append(yh)
    return outs


def _place_rope_key(kr, e):
    return sum(jnp.dot(p, e, preferred_element_type=F32) for p in _split_bf16(kr, 3))


def _mla_kernel(*refs, seq, nseq, n_ctx, use_rope):
    it = iter(refs)
    zm_ref = next(it)
    if n_ctx:
        cckv_ref, ckr_ref = next(it), next(it)
    qn_ref, wuq_ref, kvn_ref, wuk_ref, wuv_ref, qhn_ref, khn_ref, e_ref = (next(it) for _ in range(8))
    rope_ref = next(it) if use_rope else None
    o_ref, ckv_ref = next(it), next(it)
    q_s, k_s, v_s = next(it), next(it), next(it)

    qscale = MLA_QK ** -0.5
    heads = [slice(h * HEAD_PAD, (h + 1) * HEAD_PAD) for h in range(MLA_HEADS)]

    def keys_values(g, ckv, k_rope_placed, rope, k_rows):
        k_raw = _dot(ckv, wuk_ref[...]) + k_rope_placed
        yield
        kh = _head_norm(k_raw, khn_ref[...], rope, False)
        for h, hs in enumerate(heads):
            k_s[g, k_rows, hs] = kh[h].astype(BF16)
        yield
        v_s[g, k_rows, :] = _dot(ckv, wuv_ref[...]).astype(BF16)
        yield

    def latent_phases(i, g):
        r0 = pl.multiple_of(i * PROJ_TILE, PROJ_TILE)
        tile, rows = pl.ds(r0, PROJ_TILE), pl.ds(g * seq + r0, PROJ_TILE)
        rope = (rope_ref[0, tile, :], rope_ref[1, tile, :]) if use_rope else None
        ckv = _rms(zm_ref[rows, ZM_KV:ZM_KV + MLA_KV_LORA].astype(F32), kvn_ref[...])
        ckv_ref[rows, :] = ckv
        k_pe = jnp.dot(zm_ref[rows, ZM_KR:ZM_KR + LANE], e_ref[...], preferred_element_type=F32)
        yield from keys_values(g, ckv, k_pe, rope, pl.ds(n_ctx + r0, PROJ_TILE))
        cq = _rms(zm_ref[rows, ZM_Q:ZM_Q + MLA_Q_LORA].astype(F32), qn_ref[...])
        q_raw = _dot(cq, wuq_ref[...])
        yield
        qh = _head_norm(q_raw, qhn_ref[...], rope, True)
        for h, hs in enumerate(heads):
            q_s[rows, hs] = (qh[h] * qscale).astype(BF16)

    def latent_tile(i, carry):
        _emit_skewed([latent_phases(i, g) for g in range(nseq)], 0)
        return carry

    lax.fori_loop(0, seq // PROJ_TILE, latent_tile, 0)

    def context_tile(i, carry):
        rows = pl.ds(pl.multiple_of(i * PROJ_TILE, PROJ_TILE), PROJ_TILE)
        _emit_skewed([keys_values(g, cckv_ref[g, rows, :], _place_rope_key(ckr_ref[g, rows, :], e_ref[...]),
                                  None, rows) for g in range(nseq)], 0)
        return carry

    if n_ctx:
        lax.fori_loop(0, n_ctx // PROJ_TILE, context_tile, 0)

    q_tile = min(seq, Q_TILE)

    def head_phases(g, h, hs, rows, gate):
        s = lax.dot_general(q_s[rows, hs], k_s[g, :, hs], (((1,), (1,)), ((), ())),
                            preferred_element_type=F32)
        yield
        e = jnp.exp(s - jnp.max(s, axis=-1, keepdims=True))
        l = jnp.sum(e, axis=-1, keepdims=True)
        p = e.astype(BF16)
        yield
        o = jnp.dot(p, v_s[g, :, hs], preferred_element_type=F32) / l
        o_ref[rows, hs] = o * jax.nn.silu(gate[:, hs])

    def q_block(i, carry):
        chains = []
        for g in range(nseq):
            rows = pl.ds(g * seq + pl.multiple_of(i * q_tile, q_tile), q_tile)
            gate = zm_ref[rows, ZM_GATE:ZM_GATE + MLA_WIDTH].astype(F32)
            chains += [head_phases(g, h, hs, rows, gate) for h, hs in enumerate(heads)]
        _emit_skewed(chains, 0)
        return carry

    lax.fori_loop(0, seq // q_tile, q_block, 0)


def _mla(zm, row0, ctx, layer, w, rope_tab, bsz, seq):
    n_ctx = 0 if ctx is None else ctx[0].shape[-2]
    nseq = max(1, MLA_ROWS_PER_STEP // seq)
    blk0 = row0 // (nseq * seq)
    in_specs = [pl.BlockSpec((nseq * seq, ZM_W), lambda b: (b + blk0, 0))]
    args = [zm]
    if ctx is not None:
        cckv, ckr = ctx
        in_specs += [
            pl.BlockSpec((nseq, None, n_ctx, MLA_KV_LORA), lambda b: (b, layer, 0, 0)),
            pl.BlockSpec((nseq, None, n_ctx, LANE), lambda b: (b, layer, 0, 0)),
        ]
        args += [cckv, ckr]
    in_specs += [
        _layer_spec((1, MLA_Q_LORA), layer),
        _layer_spec((MLA_Q_LORA, MLA_HEADS * HEAD_PAD), layer),
        _layer_spec((1, MLA_KV_LORA), layer),
        _layer_spec((MLA_KV_LORA, MLA_HEADS * HEAD_PAD), layer),
        _layer_spec((MLA_KV_LORA, MLA_WIDTH), layer),
        _layer_spec((1, HEAD_PAD), layer),
        _layer_spec((1, HEAD_PAD), layer),
        pl.BlockSpec((LANE, MLA_HEADS * HEAD_PAD), lambda b: (0, 0)),
    ]
    args += list(w)
    if rope_tab is not None:
        in_specs.append(pl.BlockSpec((2, seq, HEAD_PAD), lambda b: (0, 0, 0)))
        args.append(rope_tab)
    return pl.pallas_call(
        functools.partial(_mla_kernel, seq=seq, nseq=nseq, n_ctx=n_ctx, use_rope=rope_tab is not None),
        grid=(bsz // nseq,),
        in_specs=in_specs,
        out_specs=[
            pl.BlockSpec((nseq * seq, MLA_WIDTH), lambda b: (b, 0)),
            pl.BlockSpec((nseq * seq, MLA_KV_LORA), lambda b: (b, 0)),
        ],
        out_shape=[
            jax.ShapeDtypeStruct((bsz * seq, MLA_WIDTH), F32),
            jax.ShapeDtypeStruct((bsz * seq, MLA_KV_LORA), F32),
        ],
        scratch_shapes=[
            pltpu.VMEM((nseq * seq, MLA_HEADS * HEAD_PAD), BF16),
            pltpu.VMEM((nseq, n_ctx + seq, MLA_HEADS * HEAD_PAD), BF16),
            pltpu.VMEM((nseq, n_ctx + seq, MLA_WIDTH), BF16),
        ],
        compiler_params=_params(("parallel",)),
        name="mla",
    )(*args)


S5_T = 8
S5_R = CHUNK // S5_T
S5_SUB_CH = 64
S5_SUBS = LANE // S5_SUB_CH
S5_SUB_STATE = S5_TILE_STATE // S5_SUBS
S5_ROW = S5_T * S5_SUB_CH
S5_W = 2 * S5_SUB_STATE
W_M, W_SF, W_SB, W_CF, W_CB = range(5)


def _cmul(ar, ai, br, bi):
    return ar * br - ai * bi, ar * bi + ai * br


def _s5_prep_kernel(are_ref, aim_ref, ldt_ref, bre_ref, bim_ref, cre_ref, cim_ref, d_ref,
                    w_ref, tab8_ref, tab1_ref):
    gr = lax.shift_right_logical(lax.broadcasted_iota(jnp.int32, (S5_SUB_CH, S5_SUB_STATE), 0),
                                 S5_GROUP.bit_length() - 1)
    gc = lax.shift_right_logical(lax.broadcasted_iota(jnp.int32, (S5_SUB_CH, S5_SUB_STATE), 1),
                                 S5_STATE.bit_length() - 1)

    def spread(ref, h):
        x = ref[h * S5_SUB_CH:(h + 1) * S5_SUB_CH, :]
        return jnp.where(gr == gc, jnp.concatenate([x] * (S5_SUB_CH // S5_GROUP), axis=1), 0.0)

    row = lax.broadcasted_iota(jnp.int32, (S5_SUB_CH, S5_SUB_CH), 0)
    col = lax.broadcasted_iota(jnp.int32, (S5_SUB_CH, S5_SUB_CH), 1)
    taps = [[[], []] for _ in range(S5_SUBS)]
    for d in (0, 1):
        a_re, a_im = are_ref[d], aim_ref[d]
        dt = jnp.exp(ldt_ref[d])
        lam = a_re * dt
        th = a_im * dt
        mag = jnp.exp(lam)
        ab_re = mag * jnp.cos(th)
        ab_im = mag * jnp.sin(th)
        den = a_re * a_re + a_im * a_im
        n_re = ab_re - 1.0
        cf_re = (n_re * a_re + ab_im * a_im) / den
        cf_im = (ab_im * a_re - n_re * a_im) / den
        k = lax.broadcasted_iota(jnp.int32, (2 * S5_T, S5_TILE_STATE), 0).astype(F32)
        pmag = jnp.exp(k * lam)
        pw_re = pmag * jnp.cos(k * th)
        pw_im = pmag * jnp.sin(k * th)
        for h in range(S5_SUBS):
            ss = slice(h * S5_SUB_STATE, (h + 1) * S5_SUB_STATE)
            c_re, c_im = spread(cre_ref, h), spread(cim_ref, h)
            c_cat = jnp.concatenate([c_re, c_im], axis=1).astype(BF16)
            bp_re, bp_im = _cmul(spread(bre_ref, h), spread(bim_ref, h), cf_re[:, ss], cf_im[:, ss])
            for p in range(S5_T + 1):
                ar, ai = pw_re[p:p + 1, ss], pw_im[p:p + 1, ss]
                t_in = S5_T - 1 - p if d == 0 else p
                t_out = p - 1 if d == 0 else S5_T - p
                if p < S5_T:
                    l_re, l_im = _cmul(bp_re, bp_im, ar, ai)
                    w_ref[h, W_SF + d, t_in * S5_SUB_CH:(t_in + 1) * S5_SUB_CH, :] = (
                        jnp.concatenate([l_re, l_im], axis=1).astype(BF16))
                    taps[h][d].append(_dot_nt(jnp.concatenate([l_re, -l_im], axis=1), c_cat))
                if p > 0:
                    v_re, v_im = _cmul(c_re, c_im, ar, ai)
                    w_ref[h, W_CF + d, t_out * S5_SUB_CH:(t_out + 1) * S5_SUB_CH, :] = (
                        jnp.concatenate([v_re, -v_im], axis=1).astype(BF16))
        r = lax.broadcasted_iota(jnp.int32, (S5_R, S5_TILE_STATE), 0).astype(F32) * float(S5_T)
        r1 = r + float(S5_T)
        pm = jnp.exp(r * lam)
        qm = jnp.exp(-(r1 * lam))
        tab8_ref[d, 0] = pm * jnp.cos(r * th)
        tab8_ref[d, 1] = pm * jnp.sin(r * th)
        tab8_ref[d, 2] = qm * jnp.cos(r1 * th)
        tab8_ref[d, 3] = -(qm * jnp.sin(r1 * th))
        mc = jnp.exp(float(CHUNK) * lam)
        tab1_ref[d, 0:1, :] = mc * jnp.cos(float(CHUNK) * th)
        tab1_ref[d, 1:2, :] = mc * jnp.sin(float(CHUNK) * th)
    for h in range(S5_SUBS):
        skip = jnp.where(row == col, d_ref[:, h * S5_SUB_CH:(h + 1) * S5_SUB_CH], 0.0)
        for t in range(S5_T):
            blocks = []
            for t2 in range(S5_T):
                if t < t2:
                    blocks.append(taps[h][0][t2 - t])
                elif t > t2:
                    blocks.append(taps[h][1][t - t2])
                else:
                    blocks.append(taps[h][0][0] + taps[h][1][0] + skip)
            w_ref[h, W_M, t * S5_SUB_CH:(t + 1) * S5_SUB_CH, :] = jnp.concatenate(blocks, axis=1).astype(BF16)


def _s5_prep(a_re, a_im, ldt, b_re, b_im, c_re, c_im, dsk):
    vec = pl.BlockSpec((None, 2, 1, S5_TILE_STATE), lambda l, j: (l, 0, 0, j))
    blk = pl.BlockSpec((None, None, LANE, S5_STATE), lambda l, j: (l, j, 0, 0))
    return pl.pallas_call(
        _s5_prep_kernel,
        grid=(DEPTH, S5_TILES),
        in_specs=[vec, vec, vec, blk, blk, blk, blk,
                  pl.BlockSpec((None, 1, LANE), lambda l, j: (l, 0, j))],
        out_specs=[
            pl.BlockSpec((None, None, S5_SUBS, 5, S5_ROW, S5_W), lambda l, j: (l, j, 0, 0, 0, 0)),
            pl.BlockSpec((None, None, 2, 4, S5_R, S5_TILE_STATE), lambda l, j: (l, j, 0, 0, 0, 0)),
            pl.BlockSpec((None, None, 2, 2, S5_TILE_STATE), lambda l, j: (l, j, 0, 0, 0)),
        ],
        out_shape=[
            jax.ShapeDtypeStruct((DEPTH, S5_TILES, S5_SUBS, 5, S5_ROW, S5_W), BF16),
            jax.ShapeDtypeStruct((DEPTH, S5_TILES, 2, 4, S5_R, S5_TILE_STATE), F32),
            jax.ShapeDtypeStruct((DEPTH, S5_TILES, 2, 2, S5_TILE_STATE), F32),
        ],
        compiler_params=_params(("parallel", "parallel")),
        name="s5_prep",
    )(a_re, a_im, ldt, b_re, b_im, c_re, c_im, dsk)


def _s5_scan_kernel(u_ref, x0_ref, w_ref, tab8_ref, tab1_ref, y_ref, fs_ref, u_s, *, nseq, nb):
    groups = nseq * nb
    nrow = groups * S5_R
    ts = S5_SUB_STATE
    u_s[...] = u_ref[...].astype(F32)
    tokens = [u_s[pl.ds(t, nrow, stride=S5_T), :] for t in range(S5_T)]
    rowi = lax.broadcasted_iota(jnp.int32, (groups, S5_R, ts), 1)

    def prefix(x):
        for s in (1, 2, 4):
            x = x + jnp.where(rowi >= s, pltpu.roll(x, s, 1), 0.0)
        return x

    def suffix(x):
        for s in (1, 2, 4):
            x = x + jnp.where(rowi < S5_R - s, pltpu.roll(x, S5_R - s, 1), 0.0)
        return x

    y_sub = []
    for h in range(S5_SUBS):
        ch = slice(h * S5_SUB_CH, (h + 1) * S5_SUB_CH)
        ss = slice(h * ts, (h + 1) * ts)
        u8 = jnp.concatenate([tok[:, ch] for tok in tokens], axis=1).astype(BF16)
        ef = jnp.dot(u8, w_ref[h, W_SF], preferred_element_type=F32).reshape(groups, S5_R, S5_W)
        eb = jnp.dot(u8, w_ref[h, W_SB], preferred_element_type=F32).reshape(groups, S5_R, S5_W)

        p_re, p_im, q_re, q_im = (tab8_ref[0, i, :, ss] for i in range(4))
        a_re, a_im = tab1_ref[0, 0:1, ss], tab1_ref[0, 1:2, ss]
        w_re, w_im = _cmul(q_re, q_im, ef[:, :, :ts], ef[:, :, ts:])
        cs_re, cs_im = prefix(w_re), prefix(w_im)
        st_re, st_im = [], []
        for s in range(nseq):
            x_re, x_im = x0_ref[s, 0, 0:1, ss], x0_ref[s, 0, 1:2, ss]
            for b in range(nb):
                g = s * nb + b
                st_re.append(x_re)
                st_im.append(x_im)
                x_re, x_im = _cmul(a_re, a_im, x_re + cs_re[g, S5_R - 1:S5_R, :],
                                   x_im + cs_im[g, S5_R - 1:S5_R, :])
            fs_ref[s, 0, 0:1, ss] = x_re
            fs_ref[s, 0, 1:2, ss] = x_im
        xin_re, xin_im = _cmul(p_re, p_im, cs_re - w_re + jnp.stack(st_re), cs_im - w_im + jnp.stack(st_im))
        xin = jnp.concatenate([xin_re, xin_im], axis=2).reshape(nrow, S5_W)

        p_re, p_im, q_re, q_im = (tab8_ref[1, i, :, ss] for i in range(4))
        a_re, a_im = tab1_ref[1, 0:1, ss], tab1_ref[1, 1:2, ss]
        w_re, w_im = _cmul(p_re, p_im, eb[:, :, :ts], eb[:, :, ts:])
        sf_re, sf_im = suffix(w_re), suffix(w_im)
        z_re, z_im = [None] * groups, [None] * groups
        for s in range(nseq):
            x_re, x_im = x0_ref[s, 1, 0:1, ss], x0_ref[s, 1, 1:2, ss]
            for b in reversed(range(nb)):
                g = s * nb + b
                z_re[g], z_im[g] = _cmul(a_re, a_im, x_re, x_im)
                x_re = sf_re[g, 0:1, :] + z_re[g]
                x_im = sf_im[g, 0:1, :] + z_im[g]
            fs_ref[s, 1, 0:1, ss] = x_re
            fs_ref[s, 1, 1:2, ss] = x_im
        xnx_re, xnx_im = _cmul(q_re, q_im, sf_re - w_re + jnp.stack(z_re), sf_im - w_im + jnp.stack(z_im))
        xnx = jnp.concatenate([xnx_re, xnx_im], axis=2).reshape(nrow, S5_W)

        y_sub.append(jnp.dot(u8, w_ref[h, W_M], preferred_element_type=F32)
                     + _dot_nt(xin, w_ref[h, W_CF]) + _dot_nt(xnx, w_ref[h, W_CB]))
    for t in range(S5_T):
        tc = slice(t * S5_SUB_CH, (t + 1) * S5_SUB_CH)
        y_ref[pl.ds(t, nrow, stride=S5_T), :] = jnp.concatenate([y[:, tc] for y in y_sub], axis=1)


def _s5_scan(zs, row0, x0, x0_block, x0_idx, layer, wmat, tab8, tab1, nseq, seq):
    n = nseq * seq
    rblk = row0 // n
    return pl.pallas_call(
        functools.partial(_s5_scan_kernel, nseq=nseq, nb=seq // CHUNK),
        grid=(S5_TILES,),
        in_specs=[
            pl.BlockSpec((n, LANE), lambda j: (rblk, ZS_U // LANE + j)),
            pl.BlockSpec(x0_block, x0_idx),
            pl.BlockSpec((None, None, S5_SUBS, 5, S5_ROW, S5_W), lambda j: (layer, j, 0, 0, 0, 0)),
            pl.BlockSpec((None, None, 2, 4, S5_R, S5_TILE_STATE), lambda j: (layer, j, 0, 0, 0, 0)),
            pl.BlockSpec((None, None, 2, 2, S5_TILE_STATE), lambda j: (layer, j, 0, 0, 0)),
        ],
        out_specs=[
            pl.BlockSpec((n, LANE), lambda j: (0, j)),
            pl.BlockSpec((nseq, 2, 2, S5_TILE_STATE), lambda j: (0, 0, 0, j)),
        ],
        out_shape=[
            jax.ShapeDtypeStruct((n, S5_WIDTH), F32),
            jax.ShapeDtypeStruct((nseq, 2, 2, S5_NSTATE), F32),
        ],
        scratch_shapes=[pltpu.VMEM((n, LANE), F32)],
        compiler_params=_params(("parallel",)),
        name="s5_scan",
    )(zs, x0, wmat, tab8, tab1)


def _merge_kernel(x_ref, mod_ref, nw_ref, oa_ref, ob_ref, ys_ref, sg_ref, wglu_ref, bglu_ref, wmg_ref,
                  wa_ref, wb_ref, wc_ref, wout_ref, y_ref, wmg_s):
    @pl.when(pl.program_id(0) == 0)
    def _():
        _pack_transposed(wmg_ref.at[0], wmg_s, 0, 3 * D_MODEL // LANE, 0, None)

    x = x_ref[...]
    mod = mod_ref[0]
    h = _mod_rmsnorm(x, nw_ref[...], mod).astype(BF16)
    zg = _dot(jax.nn.gelu(ys_ref[...]), wglu_ref[...]) + bglu_ref[...]
    oc = (zg[:, :S5_WIDTH] * jax.nn.sigmoid(zg[:, S5_WIDTH:])
          * jax.nn.silu(sg_ref[...].astype(F32)))
    mixed = None
    for br, (o_br, w_ref) in enumerate(((oa_ref[...], wa_ref), (ob_ref[...], wb_ref), (oc, wc_ref))):
        gate = jax.nn.sigmoid(jnp.dot(h, wmg_s[:, br * D_MODEL:(br + 1) * D_MODEL], preferred_element_type=F32))
        term = gate * _dot(o_br, w_ref[...])
        mixed = term if mixed is None else mixed + term
    y_ref[...] = x + mod[:, 2 * D_MODEL:] * _dot(mixed, wout_ref[...])


def _merge(x2, row0, mod, mod_idx, nw, oa, ob, ys, zs, layer, wglu, bglu, w_in_t, wa, wb, wc, wout):
    n = x2.shape[0]
    tm = ROW_TILE
    blk0 = row0 // tm
    rows = lambda w: pl.BlockSpec((tm, w), lambda i: (i, 0))
    return pl.pallas_call(
        _merge_kernel,
        grid=(n // tm,),
        in_specs=[
            rows(D_MODEL),
            pl.BlockSpec((None, 1, 1, 3 * D_MODEL), lambda i: (layer, mod_idx(i), 0, 0)),
            _layer_spec((1, D_MODEL), layer),
            rows(GLA_WIDTH), rows(MLA_WIDTH), rows(S5_WIDTH),
            pl.BlockSpec((tm, S5_WIDTH), lambda i: (i + blk0, ZS_GATE // S5_WIDTH)),
            pl.BlockSpec((None, S5_WIDTH, 2 * S5_WIDTH), lambda i: (layer, 0, 0)),
            pl.BlockSpec((None, 1, 2 * S5_WIDTH), lambda i: (layer, 0, 0)),
            pl.BlockSpec((pl.Element(1), pl.Element(3 * D_MODEL), pl.Element(D_MODEL)),
                         lambda i: (layer, MERGE_COL, 0), pipeline_mode=pl.Buffered(1)),
            _layer_spec((GLA_WIDTH, D_MODEL), layer),
            _layer_spec((MLA_WIDTH, D_MODEL), layer),
            _layer_spec((S5_WIDTH, D_MODEL), layer),
            _layer_spec((D_MODEL, D_MODEL), layer),
        ],
        out_specs=rows(D_MODEL),
        out_shape=jax.ShapeDtypeStruct((n, D_MODEL), F32),
        scratch_shapes=[pltpu.VMEM((D_MODEL, 3 * D_MODEL), BF16)],
        compiler_params=_params(("arbitrary",)),
        name="merge",
    )(x2, mod, nw, oa, ob, ys, zs, wglu, bglu, w_in_t, wa, wb, wc, wout)


def _mla_lane_of_dim():
    half = MLA_ROPE // 2
    first_gap = ROPE_SHIFT - half
    lane = np.zeros(MLA_QK, np.int32)
    for j in range(MLA_NOPE):
        lane[j] = half + j if j < first_gap else 2 * half + j
    for r in range(half):
        lane[MLA_NOPE + r] = r
        lane[MLA_NOPE + half + r] = ROPE_SHIFT + r
    return lane


MLA_LANE_OF_DIM = _mla_lane_of_dim()


def _place_heads(w, heads, lane_of_dim):
    width = len(lane_of_dim)
    src = np.zeros(heads * HEAD_PAD, np.int32)
    used = np.zeros(heads * HEAD_PAD, bool)
    for h in range(heads):
        src[h * HEAD_PAD + lane_of_dim] = h * width + np.arange(width)
        used[h * HEAD_PAD + lane_of_dim] = True
    return jnp.where(jnp.asarray(used), jnp.take(w, jnp.asarray(src), axis=-1), 0.0)


def _rope_tables(n_tok):
    rows = n_tok // GRID_W
    r = jnp.repeat(jnp.arange(rows, dtype=F32), GRID_W)
    col = jnp.tile(jnp.arange(GRID_W, dtype=F32), rows)
    n_freq = MLA_ROPE // 4
    inv = ROPE_THETA ** (-jnp.arange(n_freq, dtype=F32) / n_freq)
    ang = jnp.concatenate([r[:, None] * inv, col[:, None] * inv], axis=-1)
    cos, sin = jnp.cos(ang), jnp.sin(ang)
    ones = jnp.ones((n_tok, MLA_NOPE), F32)
    c = _place_heads(jnp.concatenate([ones, cos, cos], axis=1), 1, MLA_LANE_OF_DIM)
    s = _place_heads(jnp.concatenate([0.0 * ones, -sin, sin], axis=1), 1, MLA_LANE_OF_DIM)
    return jnp.stack([c, s])


def kernel(x_prompt, x_sample, c, c_ctx, cache_mla_ckv, cache_mla_krope, state_gla, state_s5,
           norm_w, w_ada, b_ada, w_in, gla_w_a2, gla_b_a, gla_o_norm,
           mla_q_norm, mla_w_uq, mla_kv_norm, mla_w_uk, mla_w_uv, mla_qh_norm, mla_kh_norm,
           s5_a_re, s5_a_im, s5_log_dt, s5_b_re, s5_b_im, s5_c_re, s5_c_im, s5_d, s5_w_glu, s5_b_glu,
           w_bo_gla, w_bo_mla, w_bo_s5, w_out):
    bsz, seq, _ = x_prompt.shape
    dbsz, dseq, _ = x_sample.shape
    ctx_row = COND_ROWS - 1
    assert dbsz <= ctx_row and (bsz * seq) % ROW_TILE == 0 and dseq % ROW_TILE == 0

    cond = jnp.zeros((COND_ROWS, D_MODEL), F32).at[0:dbsz].set(c).at[ctx_row].set(c_ctx)
    ada = _ada(cond, w_ada, b_ada)

    vec = lambda a: a.reshape(DEPTH, 2, 1, S5_NSTATE)
    ldt = jnp.repeat(s5_log_dt[..., None], S5_STATE, axis=-1)
    rows_gp = lambda t: t.reshape(DEPTH, S5_TILES, LANE, S5_STATE)
    bt = lambda b: rows_gp(b.transpose(0, 1, 3, 2))
    wmat, tab8, tab1 = _s5_prep(vec(s5_a_re), vec(s5_a_im), vec(ldt), bt(s5_b_re), bt(s5_b_im),
                                rows_gp(s5_c_re), rows_gp(s5_c_im), s5_d.reshape(DEPTH, 1, S5_WIDTH))
    wglu = s5_w_glu.astype(BF16)
    bglu = s5_b_glu.reshape(DEPTH, 1, 2 * S5_WIDTH)

    wuq = _place_heads(mla_w_uq, MLA_HEADS, MLA_LANE_OF_DIM).astype(BF16)
    wuk = _place_heads(mla_w_uk, MLA_HEADS, MLA_LANE_OF_DIM[:MLA_NOPE]).astype(BF16)
    wuv = mla_w_uv.astype(BF16)
    qhn = _place_heads(mla_qh_norm, 1, MLA_LANE_OF_DIM)
    khn = _place_heads(mla_kh_norm, 1, MLA_LANE_OF_DIM)
    e_np = np.zeros((LANE, MLA_HEADS * HEAD_PAD), np.float32)
    for h in range(MLA_HEADS):
        for i in range(MLA_ROPE):
            e_np[i, h * HEAD_PAD + MLA_LANE_OF_DIM[MLA_NOPE + i]] = 1.0
    e_place = jnp.asarray(e_np, BF16)
    rope_tab = _rope_tables(dseq)
    ckr_pad = jnp.pad(cache_mla_krope, ((0, 0), (0, 0), (0, 0), (0, LANE - MLA_ROPE)))

    zrow = lambda n: jnp.zeros((DEPTH, n, GLA_QK), F32)
    waf = jnp.concatenate([gla_w_a2[:, 0], zrow(LANE - GLA_RANK)], axis=1).astype(BF16)
    wab = jnp.concatenate([zrow(GLA_RANK), gla_w_a2[:, 1], zrow(LANE - 2 * GLA_RANK)], axis=1).astype(BF16)
    sgla = state_gla.reshape(dbsz, DEPTH, 2, GLA_QK, GLA_DV)
    ss5 = state_s5.reshape(dbsz, DEPTH, 2, 2, S5_NSTATE)
    zero_s5 = jnp.zeros((bsz, 2, 2, S5_NSTATE), F32)

    hp = x_prompt.reshape(bsz * seq, D_MODEL)
    hs = x_sample.reshape(dbsz * dseq, D_MODEL)
    ckv_l, krope_l, gla_l, s5_l = [], [], [], []
    w_in_t = jnp.swapaxes(w_in, 1, 2)
    mod = ada.reshape(DEPTH, COND_ROWS, 1, 3 * D_MODEL)
    nw = norm_w.reshape(DEPTH, 1, D_MODEL)
    row = lambda t: t.reshape(DEPTH, 1, -1)
    mla_w = (row(mla_q_norm), wuq, row(mla_kv_norm), wuk, wuv, row(qhn), row(khn), e_place)
    wbo = (w_bo_gla.astype(BF16), w_bo_mla.astype(BF16), w_bo_s5.astype(BF16))
    wout = w_out.astype(BF16)
    onorm = gla_o_norm.reshape(DEPTH, 1, GLA_DV)
    for l in range(DEPTH):
        p_rows, p_blocks, blocks_per_seq = bsz * seq, bsz * seq // ROW_TILE, dseq // ROW_TILE
        mod_idx = lambda i: jnp.where(i < p_blocks, ctx_row, (i - p_blocks) // blocks_per_seq)
        zg, zm, zs = _in_proj(hp, hs, mod, mod_idx, nw, w_in_t, l)

        def mixers(x2, row0, nb, n, ctx):
            if ctx:
                gctx = sgla
                x0, x0_blk = ss5, (nb, None, 2, 2, S5_TILE_STATE)
                x0_idx = lambda j: (0, l, 0, 0, j)
                mctx, rt = (cache_mla_ckv, ckr_pad), rope_tab
            else:
                gctx = None
                x0, x0_blk = zero_s5, (nb, 2, 2, S5_TILE_STATE)
                x0_idx = lambda j: (0, 0, 0, j)
                mctx, rt = None, None
            oa, st_gla = _gla(zg, row0, gctx, l, waf, wab, gla_b_a, onorm, nb, n)
            ob, ckv = _mla(zm, row0, mctx, l, mla_w, rt, nb, n)
            y_ssm, st_s5 = _s5_scan(zs, row0, x0, x0_blk, x0_idx, l, wmat, tab8, tab1, nb, n)
            grp_mod_idx = lambda i: mod_idx(i + row0 // ROW_TILE)
            y = _merge(x2, row0, mod, grp_mod_idx, nw, oa, ob, y_ssm, zs, l, wglu, bglu, w_in_t, *wbo, wout)
            return y, ckv, st_gla, st_s5

        hp_next, ckv_p, st_gla_p, st_s5_p = mixers(hp, 0, bsz, seq, False)
        hs = mixers(hs, p_rows, dbsz, dseq, True)[0]
        hp = hp_next
        ckv_l.append(ckv_p.reshape(bsz, seq, MLA_KV_LORA))
        krope_l.append(zm[:p_rows, ZM_KR:ZM_KR + MLA_ROPE].astype(F32).reshape(bsz, seq, MLA_ROPE))
        gla_l.append(st_gla_p.reshape(bsz, 2, GLA_HEADS, GLA_DK, GLA_DV))
        s5_l.append(st_s5_p.reshape(bsz, 2, 2, S5_GROUPS, S5_STATE))

    return (hp.reshape(bsz, seq, D_MODEL), hs.reshape(dbsz, dseq, D_MODEL),
            jnp.stack(ckv_l, axis=1), jnp.stack(krope_l, axis=1),
            jnp.stack(gla_l, axis=1), jnp.stack(s5_l, axis=1))
```

```python
import functools

import jax
import jax.numpy as jnp
import numpy as np
from jax import lax
from jax.experimental import pallas as pl
from jax.experimental.pallas import tpu as pltpu

F32 = jnp.float32
BF16 = jnp.bfloat16

EPS = 1e-6
D_MODEL = 1024
DEPTH = 2
GRID_W = 64
ROPE_THETA = 10000.0
GLA_HEADS = 4
GLA_DK = 64
GLA_DV = 128
GLA_RANK = 16
GLA_GATE_NORM = 16.0
GLA_QK = GLA_HEADS * GLA_DK
GLA_WIDTH = GLA_HEADS * GLA_DV
MLA_HEADS = 4
MLA_Q_LORA = 384
MLA_KV_LORA = 256
MLA_NOPE = 64
MLA_ROPE = 32
MLA_QK = MLA_NOPE + MLA_ROPE
MLA_DV = 128
MLA_WIDTH = MLA_HEADS * MLA_DV
S5_WIDTH = 512
S5_GROUP = 16
S5_GROUPS = 32
S5_STATE = 64
S5_NSTATE = S5_GROUPS * S5_STATE

LANE = 128
SUBLANE = 8
COND_ROWS = SUBLANE
HEAD_PAD = LANE
ROPE_SHIFT = LANE // 2
CHUNK = 64
GLA_STEP = 256
GLA_SEQS_PER_STEP = 2
GLA_ROWS_PER_STEP = 1024
GLA_SKEW = 1
S5_TILES = S5_WIDTH // LANE
S5_TILE_STATE = S5_NSTATE // S5_TILES
ROW_TILE = 512
Q_TILE = 512
PROJ_TILE = 256
MLA_ROWS_PER_STEP = 1024
VMEM_LIMIT = 56 * 1024 * 1024

IN_SPLITS = (GLA_QK, GLA_QK, GLA_WIDTH, GLA_RANK, GLA_RANK, GLA_WIDTH,
             MLA_Q_LORA, MLA_KV_LORA, MLA_ROPE, MLA_WIDTH,
             S5_WIDTH, S5_WIDTH, 3 * D_MODEL)
(IN_GQ, IN_GK, IN_GV, IN_GA, IN_GAB, IN_GG, IN_MQ, IN_MKV, IN_MKR, IN_MG, IN_SU, IN_SG,
 MERGE_COL, D_IN) = (int(c) for c in np.cumsum((0,) + IN_SPLITS))
ZG_Q, ZG_K, ZG_V, ZG_A = 0, GLA_QK, 2 * GLA_QK, 2 * GLA_QK + GLA_WIDTH
ZG_GATE = ZG_A + LANE
ZG_W = ZG_GATE + GLA_WIDTH
ZM_Q, ZM_KV, ZM_KR = 0, MLA_Q_LORA, MLA_Q_LORA + MLA_KV_LORA
ZM_GATE = ZM_KR + LANE
ZM_W = ZM_GATE + MLA_WIDTH
ZS_U, ZS_GATE, ZS_W = 0, S5_WIDTH, 2 * S5_WIDTH
ZG_BASE, ZM_BASE, ZS_BASE, PACK_W = 0, ZG_W, ZG_W + ZM_W, ZG_W + ZM_W + ZS_W
IN_PIECES = (
    (IN_GQ, (IN_GA - IN_GQ) // LANE, ZG_BASE + ZG_Q, None),
    (IN_GA, 1, ZG_BASE + ZG_A, 2 * GLA_RANK),
    (IN_GG, GLA_WIDTH // LANE, ZG_BASE + ZG_GATE, None),
    (IN_MQ, (IN_MKR - IN_MQ) // LANE, ZM_BASE + ZM_Q, None),
    (IN_MKR, 1, ZM_BASE + ZM_KR, MLA_ROPE),
    (IN_MG, MLA_WIDTH // LANE, ZM_BASE + ZM_GATE, None),
    (IN_SU, (MERGE_COL - IN_SU) // LANE, ZS_BASE + ZS_U, None),
)


def _dot(a, b):
    return jnp.dot(a.astype(BF16), b.astype(BF16), preferred_element_type=F32)


def _dot_nt(a, b):
    return lax.dot_general(a.astype(BF16), b.astype(BF16), (((1,), (1,)), ((), ())),
                           preferred_element_type=F32)


def _split_bf16(x, parts):
    out = []
    r = x
    for _ in range(parts):
        p = r.astype(BF16)
        out.append(p)
        r = r - p.astype(F32)
    return out


def _emit_skewed(chains, skew):
    pending, active, tick = list(chains), [], 0
    while pending or active:
        while pending and (skew == 0 or tick % skew == 0):
            active.append(pending.pop(0))
            if skew:
                break
        for gen in list(active):
            if next(gen, "done") == "done":
                active.remove(gen)
        tick += 1


def _vector_spec(width):
    return pl.BlockSpec((DEPTH, width), lambda *_: (0, 0))


def _layer_spec(shape, layer):
    return pl.BlockSpec((None,) + tuple(shape), lambda *_: (layer,) + (0,) * len(shape))


def _params(sem):
    return pltpu.CompilerParams(dimension_semantics=sem, vmem_limit_bytes=VMEM_LIMIT)


def _ada_kernel(c_ref, w_ref, b_ref, o_ref):
    s = jax.nn.silu(c_ref[...])
    o_ref[...] = _dot(s, w_ref[...]) + b_ref[...]


def _ada(cond8, w_ada, b_ada):
    tn = 1024
    return pl.pallas_call(
        _ada_kernel,
        grid=(DEPTH, 3 * D_MODEL // tn),
        in_specs=[
            pl.BlockSpec((COND_ROWS, D_MODEL), lambda l, n: (0, 0)),
            pl.BlockSpec((None, D_MODEL, tn), lambda l, n: (l, 0, n)),
            pl.BlockSpec((None, 1, tn), lambda l, n: (l, 0, n)),
        ],
        out_specs=pl.BlockSpec((None, COND_ROWS, tn), lambda l, n: (l, 0, n)),
        out_shape=jax.ShapeDtypeStruct((DEPTH, COND_ROWS, 3 * D_MODEL), F32),
        compiler_params=_params(("parallel", "parallel")),
        name="ada",
    )(cond8, w_ada, b_ada.reshape(DEPTH, 1, 3 * D_MODEL))


def _mod_rmsnorm(x, nw, mod):
    ms = jnp.mean(x * x, axis=-1, keepdims=True)
    y = x * lax.rsqrt(ms + EPS) * nw
    return y * (1.0 + mod[:, D_MODEL:2 * D_MODEL]) + mod[:, 0:D_MODEL]


def _pack_transposed(w_ref, wb_s, src, tiles, dst, keep):
    lane = lax.broadcasted_iota(jnp.int32, (D_MODEL, LANE), 1)
    for t in range(tiles):
        blk = w_ref[src + t * LANE:src + (t + 1) * LANE, :].T
        if keep is not None:
            blk = jnp.where(lane < keep, blk, 0.0)
        wb_s[:, dst + t * LANE:dst + (t + 1) * LANE] = blk.astype(BF16)


def _in_proj_kernel(xp_ref, xs_ref, mod_ref, nw_ref, w_ref, zg_ref, zm_ref, zs_ref, wb_s, *, p_blocks, layer):
    i = pl.program_id(0)

    @pl.when(i == 0)
    def _():
        for src, tiles, dst, keep in IN_PIECES:
            _pack_transposed(w_ref, wb_s, src, tiles, dst, keep)

    x = jnp.where(i < p_blocks, xp_ref[...], xs_ref[...])
    h = _mod_rmsnorm(x, nw_ref[layer:layer + 1, :], mod_ref[0]).astype(BF16)
    z = jnp.dot(h, wb_s[...], preferred_element_type=F32)
    zg_ref[...] = z[:, ZG_BASE:ZG_BASE + ZG_W].astype(BF16)
    zm_ref[...] = z[:, ZM_BASE:ZM_BASE + ZM_W].astype(BF16)
    zs_ref[...] = z[:, ZS_BASE:ZS_BASE + ZS_W].astype(BF16)


def _two_group_rows(width, p_blocks):
    tm = ROW_TILE
    return (pl.BlockSpec((tm, width), lambda i: (jnp.minimum(i, p_blocks - 1), 0)),
            pl.BlockSpec((tm, width), lambda i: (jnp.maximum(i - p_blocks, 0), 0)))


def _in_proj(xp, xs, mod, mod_idx, nw, w_in_t, layer):
    tm = ROW_TILE
    p_blocks = xp.shape[0] // tm
    n = xp.shape[0] + xs.shape[0]
    return pl.pallas_call(
        functools.partial(_in_proj_kernel, p_blocks=p_blocks, layer=layer),
        grid=(n // tm,),
        in_specs=[
            *_two_group_rows(D_MODEL, p_blocks),
            pl.BlockSpec((None, 1, 1, 3 * D_MODEL), lambda i: (layer, mod_idx(i), 0, 0)),
            _vector_spec(D_MODEL),
            pl.BlockSpec((None, MERGE_COL, D_MODEL), lambda i: (layer, 0, 0), pipeline_mode=pl.Buffered(1)),
        ],
        out_specs=[
            pl.BlockSpec((tm, ZG_W), lambda i: (i, 0)),
            pl.BlockSpec((tm, ZM_W), lambda i: (i, 0)),
            pl.BlockSpec((tm, ZS_W), lambda i: (i, 0)),
        ],
        out_shape=[
            jax.ShapeDtypeStruct((n, ZG_W), BF16),
            jax.ShapeDtypeStruct((n, ZM_W), BF16),
            jax.ShapeDtypeStruct((n, ZS_W), BF16),
        ],
        scratch_shapes=[pltpu.VMEM((D_MODEL, PACK_W), BF16)],
        compiler_params=_params(("arbitrary",)),
        name="in_proj",
    )(xp, xs, mod, nw, w_in_t)


def _gla_kernel(*refs, nsteps, seq, nseq, has_ctx, layer):
    it = iter(refs)
    zg_ref = next(it)
    s0_ref = next(it) if has_ctx else None
    waf_ref, wab_ref, ba_ref, onorm_ref, o_ref, sfin_ref, la_s, o_s, st_s = (next(it) for _ in range(9))
    chains = [(g, d) for g in range(nseq) for d in (0, 1)]
    inv_norm = 1.0 / GLA_GATE_NORM
    zero_blk = jnp.zeros((GLA_DK, GLA_DV), F32)
    for ch, (g, d) in enumerate(chains):
        if d == 0:
            a_blk = zg_ref[g * seq:(g + 1) * seq, ZG_A:ZG_A + LANE]
        wa_ref = waf_ref if d == 0 else wab_ref
        a_low = _dot(a_blk, wa_ref[...]) + ba_ref[d:d + 1, :]
        la_s[ch] = (jnp.minimum(a_low, 0.0) - jnp.log(1.0 + jnp.exp(-jnp.abs(a_low)))) * inv_norm
        if has_ctx:
            s0 = s0_ref[g, d]
            rows_bd = []
            for h in range(GLA_HEADS):
                sh = s0[h * GLA_DK:(h + 1) * GLA_DK, :]
                rows_bd.append(jnp.concatenate([sh if h2 == h else zero_blk for h2 in range(GLA_HEADS)], axis=1))
            st_s[ch] = jnp.concatenate(rows_bd, axis=0).T
        else:
            st_s[ch] = jnp.zeros((GLA_WIDTH, GLA_QK), F32)

    def iota(shape, axis, shift):
        return lax.shift_right_logical(lax.broadcasted_iota(jnp.int32, shape, axis), shift)

    log_chunk, log_dv = CHUNK.bit_length() - 1, GLA_DV.bit_length() - 1
    row = lax.broadcasted_iota(jnp.int32, (GLA_STEP, GLA_STEP), 0)
    col = lax.broadcasted_iota(jnp.int32, (GLA_STEP, GLA_STEP), 1)
    same_chunk = iota((GLA_STEP, GLA_STEP), 0, log_chunk) == iota((GLA_STEP, GLA_STEP), 1, log_chunk)
    masks = (same_chunk & (row >= col), same_chunk & (row <= col))
    lane_head = iota((GLA_STEP, GLA_QK), 1, log_chunk)
    row_chunk = iota((GLA_STEP, GLA_QK), 0, log_chunk)
    state_blk = iota((GLA_WIDTH, GLA_QK), 0, log_dv) == iota((GLA_WIDTH, GLA_QK), 1, log_chunk)
    qscale = GLA_DK ** -0.5
    nch = GLA_STEP // CHUNK

    def chain_phases(i, ch, g, d):
        r0 = pl.multiple_of((i if d == 0 else nsteps - 1 - i) * GLA_STEP, GLA_STEP)
        zrows, rows = pl.ds(g * seq + r0, GLA_STEP), pl.ds(r0, GLA_STEP)
        a_hi, a_lo = _split_bf16(la_s[ch, rows, :], 2)
        tri = masks[d].astype(BF16)
        cum = (jnp.dot(tri, a_hi, preferred_element_type=F32)
               + jnp.dot(tri, a_lo, preferred_element_type=F32))
        yield
        edge = CHUNK - 1 if d == 0 else 0
        blast = [cum[c * CHUNK + edge:c * CHUNK + edge + 1, :] for c in range(nch)]
        bl = jnp.concatenate([jnp.broadcast_to(b, (CHUNK, GLA_QK)) for b in blast], axis=0)
        q = zg_ref[zrows, ZG_Q:ZG_Q + GLA_QK].astype(F32) * qscale
        k = zg_ref[zrows, ZG_K:ZG_K + GLA_QK].astype(F32)
        v = zg_ref[zrows, ZG_V:ZG_V + GLA_WIDTH]
        v_t = v.astype(F32).T.astype(BF16)
        qd = q * jnp.exp(cum)
        kd = (k * jnp.exp(-cum)).astype(BF16)
        kr = k * jnp.exp(bl - cum)
        yield
        outs = []
        for h in range(GLA_HEADS):
            qh = jnp.where(lane_head == h, qd, 0.0)
            att = _dot_nt(qh, kd)
            yield
            att = jnp.where(masks[d], att, 0.0)
            outs.append(_dot(att, v[:, h * GLA_DV:(h + 1) * GLA_DV]))
            yield
        s = st_s[ch]
        inter = [None] * nch
        for c in (range(nch) if d == 0 else reversed(range(nch))):
            inter[c] = _dot_nt(qd[c * CHUNK:(c + 1) * CHUNK, :], s)
            kv_t = _dot(v_t, jnp.where(row_chunk == c, kr, 0.0))
            yield
            s = s * jnp.exp(blast[c]) + jnp.where(state_blk, kv_t, 0.0)
            yield
        st_s[ch] = s
        o_s[ch, rows, :] = jnp.concatenate(outs, axis=1) + jnp.concatenate(inter, axis=0)

    def step(i, carry):
        _emit_skewed([chain_phases(i, ch, g, d) for ch, (g, d) in enumerate(chains)], GLA_SKEW)
        return carry

    lax.fori_loop(0, nsteps, step, 0)
    onorm = onorm_ref[layer:layer + 1, :]
    for ch, (g, d) in enumerate(chains):
        s_fin = st_s[ch].T
        for h in range(GLA_HEADS):
            sfin_ref[g, d, h * GLA_DK:(h + 1) * GLA_DK, :] = (
                s_fin[h * GLA_DK:(h + 1) * GLA_DK, h * GLA_DV:(h + 1) * GLA_DV])
    for g in range(nseq):
        srows = slice(g * seq, (g + 1) * seq)
        o = o_s[2 * g] + o_s[2 * g + 1]
        gate = zg_ref[srows, ZG_GATE:ZG_GATE + GLA_WIDTH].astype(F32)
        for h in range(GLA_HEADS):
            vs = slice(h * GLA_DV, (h + 1) * GLA_DV)
            oh = o[:, vs]
            ms = jnp.mean(oh * oh, axis=-1, keepdims=True)
            o_ref[srows, vs] = oh * lax.rsqrt(ms + EPS) * onorm * jax.nn.silu(gate[:, vs])


def _gla(zg, row0, ctx, layer, waf, wab, ba, onorm, bsz, seq):
    nseq = max(GLA_SEQS_PER_STEP, GLA_ROWS_PER_STEP // seq)
    blk0 = row0 // (nseq * seq)
    in_specs = [pl.BlockSpec((nseq * seq, ZG_W), lambda b: (b + blk0, 0))]
    args = [zg]
    if ctx is not None:
        in_specs.append(pl.BlockSpec((nseq, None, 2, GLA_QK, GLA_DV), lambda b: (b, layer, 0, 0, 0)))
        args.append(ctx)
    in_specs += [
        _layer_spec((LANE, GLA_QK), layer),
        _layer_spec((LANE, GLA_QK), layer),
        _layer_spec((2, GLA_QK), layer),
        _vector_spec(GLA_DV),
    ]
    return pl.pallas_call(
        functools.partial(_gla_kernel, nsteps=seq // GLA_STEP, seq=seq, nseq=nseq, has_ctx=ctx is not None,
                          layer=layer),
        grid=(bsz // nseq,),
        in_specs=in_specs,
        out_specs=[
            pl.BlockSpec((nseq * seq, GLA_WIDTH), lambda b: (b, 0)),
            pl.BlockSpec((nseq, 2, GLA_QK, GLA_DV), lambda b: (b, 0, 0, 0)),
        ],
        out_shape=[
            jax.ShapeDtypeStruct((bsz * seq, GLA_WIDTH), F32),
            jax.ShapeDtypeStruct((bsz, 2, GLA_QK, GLA_DV), F32),
        ],
        scratch_shapes=[
            pltpu.VMEM((2 * nseq, seq, GLA_QK), F32),
            pltpu.VMEM((2 * nseq, seq, GLA_WIDTH), F32),
            pltpu.VMEM((2 * nseq, GLA_WIDTH, GLA_QK), F32),
        ],
        compiler_params=_params(("parallel",)),
        name="gla",
    )(*args, waf, wab, ba, onorm)


def _rms(x, w):
    ms = jnp.mean(x * x, axis=-1, keepdims=True)
    return x * lax.rsqrt(ms + EPS) * w


def _head_sums_mxu(x):
    width = x.shape[-1]
    shift = HEAD_PAD.bit_length() - 1
    gi = lax.shift_right_logical(lax.broadcasted_iota(jnp.int32, (width, width), 0), shift)
    gj = lax.shift_right_logical(lax.broadcasted_iota(jnp.int32, (width, width), 1), shift)
    return _dot(x * x, jnp.where(gi == gj, 1.0, 0.0))


def _head_norm(x, w, rope, on_mxu):
    sums = _head_sums_mxu(x) if on_mxu else None
    outs = []
    for h in range(MLA_HEADS):
        hs = slice(h * HEAD_PAD, (h + 1) * HEAD_PAD)
        xh = x[:, hs]
        ss = sums[:, hs] if on_mxu else jnp.sum(xh * xh, axis=-1, keepdims=True)
        yh = xh * lax.rsqrt(ss * (1.0 / MLA_QK) + EPS) * w
        if rope is not None:
            c, s = rope
            yh = yh * c + pltpu.roll(yh, ROPE_SHIFT, 1) * s
        outs.append(yh)
    return outs


def _place_rope_key(kr, e):
    return sum(jnp.dot(p, e, preferred_element_type=F32) for p in _split_bf16(kr, 3))


def _mla_kernel(*refs, seq, nseq, n_ctx, use_rope, layer):
    it = iter(refs)
    zm_ref = next(it)
    if n_ctx:
        cckv_ref, ckr_ref = next(it), next(it)
    qn_ref, wuq_ref, kvn_ref, wuk_ref, wuv_ref, qhn_ref, khn_ref, e_ref = (next(it) for _ in range(8))
    rope_ref = next(it) if use_rope else None
    o_ref, ckv_ref = next(it), next(it)
    q_s, k_s, v_s = next(it), next(it), next(it)
    qn, kvn, qhn, khn = (r[layer:layer + 1, :] for r in (qn_ref, kvn_ref, qhn_ref, khn_ref))

    qscale = MLA_QK ** -0.5
    heads = [slice(h * HEAD_PAD, (h + 1) * HEAD_PAD) for h in range(MLA_HEADS)]

    def keys_values(g, ckv, k_rope_placed, rope, k_rows):
        k_raw = _dot(ckv, wuk_ref[...]) + k_rope_placed
        yield
        kh = _head_norm(k_raw, khn, rope, False)
        for h, hs in enumerate(heads):
            k_s[g, k_rows, hs] = kh[h].astype(BF16)
        yield
        v_s[g, k_rows, :] = _dot(ckv, wuv_ref[...]).astype(BF16)
        yield

    def latent_phases(i, g):
        r0 = pl.multiple_of(i * PROJ_TILE, PROJ_TILE)
        tile, rows = pl.ds(r0, PROJ_TILE), pl.ds(g * seq + r0, PROJ_TILE)
        rope = (rope_ref[0, tile, :], rope_ref[1, tile, :]) if use_rope else None
        ckv = _rms(zm_ref[rows, ZM_KV:ZM_KV + MLA_KV_LORA].astype(F32), kvn)
        ckv_ref[rows, :] = ckv
        k_pe = jnp.dot(zm_ref[rows, ZM_KR:ZM_KR + LANE], e_ref[...], preferred_element_type=F32)
        yield from keys_values(g, ckv, k_pe, rope, pl.ds(n_ctx + r0, PROJ_TILE))
        cq = _rms(zm_ref[rows, ZM_Q:ZM_Q + MLA_Q_LORA].astype(F32), qn)
        q_raw = _dot(cq, wuq_ref[...])
        yield
        qh = _head_norm(q_raw, qhn, rope, True)
        for h, hs in enumerate(heads):
            q_s[rows, hs] = (qh[h] * qscale).astype(BF16)

    def latent_tile(i, carry):
        _emit_skewed([latent_phases(i, g) for g in range(nseq)], 0)
        return carry

    lax.fori_loop(0, seq // PROJ_TILE, latent_tile, 0)

    def context_tile(i, carry):
        rows = pl.ds(pl.multiple_of(i * PROJ_TILE, PROJ_TILE), PROJ_TILE)
        _emit_skewed([keys_values(g, cckv_ref[g, rows, :], _place_rope_key(ckr_ref[g, rows, :], e_ref[...]),
                                  None, rows) for g in range(nseq)], 0)
        return carry

    if n_ctx:
        lax.fori_loop(0, n_ctx // PROJ_TILE, context_tile, 0)

    q_tile = min(seq, Q_TILE)

    def head_phases(g, h, hs, rows, gate):
        s = lax.dot_general(q_s[rows, hs], k_s[g, :, hs], (((1,), (1,)), ((), ())),
                            preferred_element_type=F32)
        yield
        e = jnp.exp(s - jnp.max(s, axis=-1, keepdims=True))
        l = jnp.sum(e, axis=-1, keepdims=True)
        p = e.astype(BF16)
        yield
        o = jnp.dot(p, v_s[g, :, hs], preferred_element_type=F32) / l
        o_ref[rows, hs] = o * jax.nn.silu(gate[:, hs])

    def q_block(i, carry):
        chains = []
        for g in range(nseq):
            rows = pl.ds(g * seq + pl.multiple_of(i * q_tile, q_tile), q_tile)
            gate = zm_ref[rows, ZM_GATE:ZM_GATE + MLA_WIDTH].astype(F32)
            chains += [head_phases(g, h, hs, rows, gate) for h, hs in enumerate(heads)]
        _emit_skewed(chains, 0)
        return carry

    lax.fori_loop(0, seq // q_tile, q_block, 0)


def _mla(zm, row0, ctx, layer, w, rope_tab, bsz, seq):
    n_ctx = 0 if ctx is None else ctx[0].shape[-2]
    nseq = max(1, MLA_ROWS_PER_STEP // seq)
    blk0 = row0 // (nseq * seq)
    in_specs = [pl.BlockSpec((nseq * seq, ZM_W), lambda b: (b + blk0, 0))]
    args = [zm]
    if ctx is not None:
        cckv, ckr = ctx
        in_specs += [
            pl.BlockSpec((nseq, None, n_ctx, MLA_KV_LORA), lambda b: (b, layer, 0, 0)),
            pl.BlockSpec((nseq, None, n_ctx, LANE), lambda b: (b, layer, 0, 0)),
        ]
        args += [cckv, ckr]
    in_specs += [
        _vector_spec(MLA_Q_LORA),
        _layer_spec((MLA_Q_LORA, MLA_HEADS * HEAD_PAD), layer),
        _vector_spec(MLA_KV_LORA),
        _layer_spec((MLA_KV_LORA, MLA_HEADS * HEAD_PAD), layer),
        _layer_spec((MLA_KV_LORA, MLA_WIDTH), layer),
        _vector_spec(HEAD_PAD),
        _vector_spec(HEAD_PAD),
        pl.BlockSpec((LANE, MLA_HEADS * HEAD_PAD), lambda b: (0, 0)),
    ]
    args += list(w)
    if rope_tab is not None:
        in_specs.append(pl.BlockSpec((2, seq, HEAD_PAD), lambda b: (0, 0, 0)))
        args.append(rope_tab)
    return pl.pallas_call(
        functools.partial(_mla_kernel, seq=seq, nseq=nseq, n_ctx=n_ctx, use_rope=rope_tab is not None,
                          layer=layer),
        grid=(bsz // nseq,),
        in_specs=in_specs,
        out_specs=[
            pl.BlockSpec((nseq * seq, MLA_WIDTH), lambda b: (b, 0)),
            pl.BlockSpec((nseq * seq, MLA_KV_LORA), lambda b: (b, 0)),
        ],
        out_shape=[
            jax.ShapeDtypeStruct((bsz * seq, MLA_WIDTH), F32),
            jax.ShapeDtypeStruct((bsz * seq, MLA_KV_LORA), F32),
        ],
        scratch_shapes=[
            pltpu.VMEM((nseq * seq, MLA_HEADS * HEAD_PAD), BF16),
            pltpu.VMEM((nseq, n_ctx + seq, MLA_HEADS * HEAD_PAD), BF16),
            pltpu.VMEM((nseq, n_ctx + seq, MLA_WIDTH), BF16),
        ],
        compiler_params=_params(("parallel",)),
        name="mla",
    )(*args)


S5_T = 8
S5_R = CHUNK // S5_T
S5_SUB_CH = 64
S5_SUBS = LANE // S5_SUB_CH
S5_SUB_STATE = S5_TILE_STATE // S5_SUBS
S5_ROW = S5_T * S5_SUB_CH
S5_W = 2 * S5_SUB_STATE
W_M, W_SF, W_SB, W_CF, W_CB = range(5)


def _cmul(ar, ai, br, bi):
    return ar * br - ai * bi, ar * bi + ai * br


def _s5_prep_kernel(are_ref, aim_ref, ldt_ref, bre_ref, bim_ref, cre_ref, cim_ref, d_ref,
                    w_ref, tab8_ref, tab1_ref):
    gr = lax.shift_right_logical(lax.broadcasted_iota(jnp.int32, (S5_SUB_CH, S5_SUB_STATE), 0),
                                 S5_GROUP.bit_length() - 1)
    gc = lax.shift_right_logical(lax.broadcasted_iota(jnp.int32, (S5_SUB_CH, S5_SUB_STATE), 1),
                                 S5_STATE.bit_length() - 1)

    def spread(ref, h):
        x = ref[h * S5_SUB_CH:(h + 1) * S5_SUB_CH, :]
        return jnp.where(gr == gc, jnp.concatenate([x] * (S5_SUB_CH // S5_GROUP), axis=1), 0.0)

    row = lax.broadcasted_iota(jnp.int32, (S5_SUB_CH, S5_SUB_CH), 0)
    col = lax.broadcasted_iota(jnp.int32, (S5_SUB_CH, S5_SUB_CH), 1)
    taps = [[[], []] for _ in range(S5_SUBS)]
    for d in (0, 1):
        a_re, a_im = are_ref[d], aim_ref[d]
        dt = jnp.exp(ldt_ref[d])
        lam = a_re * dt
        th = a_im * dt
        mag = jnp.exp(lam)
        ab_re = mag * jnp.cos(th)
        ab_im = mag * jnp.sin(th)
        den = a_re * a_re + a_im * a_im
        n_re = ab_re - 1.0
        cf_re = (n_re * a_re + ab_im * a_im) / den
        cf_im = (ab_im * a_re - n_re * a_im) / den
        k = lax.broadcasted_iota(jnp.int32, (2 * S5_T, S5_TILE_STATE), 0).astype(F32)
        pmag = jnp.exp(k * lam)
        pw_re = pmag * jnp.cos(k * th)
        pw_im = pmag * jnp.sin(k * th)
        for h in range(S5_SUBS):
            ss = slice(h * S5_SUB_STATE, (h + 1) * S5_SUB_STATE)
            c_re, c_im = spread(cre_ref, h), spread(cim_ref, h)
            c_cat = jnp.concatenate([c_re, c_im], axis=1).astype(BF16)
            bp_re, bp_im = _cmul(spread(bre_ref, h), spread(bim_ref, h), cf_re[:, ss], cf_im[:, ss])
            for p in range(S5_T + 1):
                ar, ai = pw_re[p:p + 1, ss], pw_im[p:p + 1, ss]
                t_in = S5_T - 1 - p if d == 0 else p
                t_out = p - 1 if d == 0 else S5_T - p
                if p < S5_T:
                    l_re, l_im = _cmul(bp_re, bp_im, ar, ai)
                    w_ref[h, W_SF + d, t_in * S5_SUB_CH:(t_in + 1) * S5_SUB_CH, :] = (
                        jnp.concatenate([l_re, l_im], axis=1).astype(BF16))
                    taps[h][d].append(_dot_nt(jnp.concatenate([l_re, -l_im], axis=1), c_cat))
                if p > 0:
                    v_re, v_im = _cmul(c_re, c_im, ar, ai)
                    w_ref[h, W_CF + d, t_out * S5_SUB_CH:(t_out + 1) * S5_SUB_CH, :] = (
                        jnp.concatenate([v_re, -v_im], axis=1).astype(BF16))
        r = lax.broadcasted_iota(jnp.int32, (S5_R, S5_TILE_STATE), 0).astype(F32) * float(S5_T)
        r1 = r + float(S5_T)
        pm = jnp.exp(r * lam)
        qm = jnp.exp(-(r1 * lam))
        tab8_ref[d, 0] = pm * jnp.cos(r * th)
        tab8_ref[d, 1] = pm * jnp.sin(r * th)
        tab8_ref[d, 2] = qm * jnp.cos(r1 * th)
        tab8_ref[d, 3] = -(qm * jnp.sin(r1 * th))
        mc = jnp.exp(float(CHUNK) * lam)
        tab1_ref[d, 0:1, :] = mc * jnp.cos(float(CHUNK) * th)
        tab1_ref[d, 1:2, :] = mc * jnp.sin(float(CHUNK) * th)
    for h in range(S5_SUBS):
        skip = jnp.where(row == col, d_ref[:, h * S5_SUB_CH:(h + 1) * S5_SUB_CH], 0.0)
        for t in range(S5_T):
            blocks = []
            for t2 in range(S5_T):
                if t < t2:
                    blocks.append(taps[h][0][t2 - t])
                elif t > t2:
                    blocks.append(taps[h][1][t - t2])
                else:
                    blocks.append(taps[h][0][0] + taps[h][1][0] + skip)
            w_ref[h, W_M, t * S5_SUB_CH:(t + 1) * S5_SUB_CH, :] = jnp.concatenate(blocks, axis=1).astype(BF16)


def _s5_prep(a_re, a_im, ldt, b_re, b_im, c_re, c_im, dsk):
    vec = pl.BlockSpec((None, 2, 1, S5_TILE_STATE), lambda l, j: (l, 0, 0, j))
    blk = pl.BlockSpec((None, None, LANE, S5_STATE), lambda l, j: (l, j, 0, 0))
    return pl.pallas_call(
        _s5_prep_kernel,
        grid=(DEPTH, S5_TILES),
        in_specs=[vec, vec, vec, blk, blk, blk, blk,
                  pl.BlockSpec((None, 1, LANE), lambda l, j: (l, 0, j))],
        out_specs=[
            pl.BlockSpec((None, None, S5_SUBS, 5, S5_ROW, S5_W), lambda l, j: (l, j, 0, 0, 0, 0)),
            pl.BlockSpec((None, None, 2, 4, S5_R, S5_TILE_STATE), lambda l, j: (l, j, 0, 0, 0, 0)),
            pl.BlockSpec((None, None, 2, 2, S5_TILE_STATE), lambda l, j: (l, j, 0, 0, 0)),
        ],
        out_shape=[
            jax.ShapeDtypeStruct((DEPTH, S5_TILES, S5_SUBS, 5, S5_ROW, S5_W), BF16),
            jax.ShapeDtypeStruct((DEPTH, S5_TILES, 2, 4, S5_R, S5_TILE_STATE), F32),
            jax.ShapeDtypeStruct((DEPTH, S5_TILES, 2, 2, S5_TILE_STATE), F32),
        ],
        compiler_params=_params(("parallel", "parallel")),
        name="s5_prep",
    )(a_re, a_im, ldt, b_re, b_im, c_re, c_im, dsk)


def _s5_scan_kernel(u_ref, x0_ref, w_ref, tab8_ref, tab1_ref, y_ref, fs_ref, u_s, *, nseq, nb):
    groups = nseq * nb
    nrow = groups * S5_R
    ts = S5_SUB_STATE
    u_s[...] = u_ref[...].astype(F32)
    tokens = [u_s[pl.ds(t, nrow, stride=S5_T), :] for t in range(S5_T)]
    rowi = lax.broadcasted_iota(jnp.int32, (groups, S5_R, ts), 1)

    def prefix(x):
        for s in (1, 2, 4):
            x = x + jnp.where(rowi >= s, pltpu.roll(x, s, 1), 0.0)
        return x

    def suffix(x):
        for s in (1, 2, 4):
            x = x + jnp.where(rowi < S5_R - s, pltpu.roll(x, S5_R - s, 1), 0.0)
        return x

    y_sub = []
    for h in range(S5_SUBS):
        ch = slice(h * S5_SUB_CH, (h + 1) * S5_SUB_CH)
        ss = slice(h * ts, (h + 1) * ts)
        u8 = jnp.concatenate([tok[:, ch] for tok in tokens], axis=1).astype(BF16)
        ef = jnp.dot(u8, w_ref[h, W_SF], preferred_element_type=F32).reshape(groups, S5_R, S5_W)
        eb = jnp.dot(u8, w_ref[h, W_SB], preferred_element_type=F32).reshape(groups, S5_R, S5_W)

        p_re, p_im, q_re, q_im = (tab8_ref[0, i, :, ss] for i in range(4))
        a_re, a_im = tab1_ref[0, 0:1, ss], tab1_ref[0, 1:2, ss]
        w_re, w_im = _cmul(q_re, q_im, ef[:, :, :ts], ef[:, :, ts:])
        cs_re, cs_im = prefix(w_re), prefix(w_im)
        st_re, st_im = [], []
        for s in range(nseq):
            x_re, x_im = x0_ref[s, 0, 0:1, ss], x0_ref[s, 0, 1:2, ss]
            for b in range(nb):
                g = s * nb + b
                st_re.append(x_re)
                st_im.append(x_im)
                x_re, x_im = _cmul(a_re, a_im, x_re + cs_re[g, S5_R - 1:S5_R, :],
                                   x_im + cs_im[g, S5_R - 1:S5_R, :])
            fs_ref[s, 0, 0:1, ss] = x_re
            fs_ref[s, 0, 1:2, ss] = x_im
        xin_re, xin_im = _cmul(p_re, p_im, cs_re - w_re + jnp.stack(st_re), cs_im - w_im + jnp.stack(st_im))
        xin = jnp.concatenate([xin_re, xin_im], axis=2).reshape(nrow, S5_W)

        p_re, p_im, q_re, q_im = (tab8_ref[1, i, :, ss] for i in range(4))
        a_re, a_im = tab1_ref[1, 0:1, ss], tab1_ref[1, 1:2, ss]
        w_re, w_im = _cmul(p_re, p_im, eb[:, :, :ts], eb[:, :, ts:])
        sf_re, sf_im = suffix(w_re), suffix(w_im)
        z_re, z_im = [None] * groups, [None] * groups
        for s in range(nseq):
            x_re, x_im = x0_ref[s, 1, 0:1, ss], x0_ref[s, 1, 1:2, ss]
            for b in reversed(range(nb)):
                g = s * nb + b
                z_re[g], z_im[g] = _cmul(a_re, a_im, x_re, x_im)
                x_re = sf_re[g, 0:1, :] + z_re[g]
                x_im = sf_im[g, 0:1, :] + z_im[g]
            fs_ref[s, 1, 0:1, ss] = x_re
            fs_ref[s, 1, 1:2, ss] = x_im
        xnx_re, xnx_im = _cmul(q_re, q_im, sf_re - w_re + jnp.stack(z_re), sf_im - w_im + jnp.stack(z_im))
        xnx = jnp.concatenate([xnx_re, xnx_im], axis=2).reshape(nrow, S5_W)

        y_sub.append(jnp.dot(u8, w_ref[h, W_M], preferred_element_type=F32)
                     + _dot_nt(xin, w_ref[h, W_CF]) + _dot_nt(xnx, w_ref[h, W_CB]))
    for t in range(S5_T):
        tc = slice(t * S5_SUB_CH, (t + 1) * S5_SUB_CH)
        y_ref[pl.ds(t, nrow, stride=S5_T), :] = jnp.concatenate([y[:, tc] for y in y_sub], axis=1)


def _s5_scan(zs, row0, x0, x0_block, x0_idx, layer, wmat, tab8, tab1, nseq, seq):
    n = nseq * seq
    rblk = row0 // n
    return pl.pallas_call(
        functools.partial(_s5_scan_kernel, nseq=nseq, nb=seq // CHUNK),
        grid=(S5_TILES,),
        in_specs=[
            pl.BlockSpec((n, LANE), lambda j: (rblk, ZS_U // LANE + j)),
            pl.BlockSpec(x0_block, x0_idx),
            pl.BlockSpec((None, None, S5_SUBS, 5, S5_ROW, S5_W), lambda j: (layer, j, 0, 0, 0, 0)),
            pl.BlockSpec((None, None, 2, 4, S5_R, S5_TILE_STATE), lambda j: (layer, j, 0, 0, 0, 0)),
            pl.BlockSpec((None, None, 2, 2, S5_TILE_STATE), lambda j: (layer, j, 0, 0, 0)),
        ],
        out_specs=[
            pl.BlockSpec((n, LANE), lambda j: (0, j)),
            pl.BlockSpec((nseq, 2, 2, S5_TILE_STATE), lambda j: (0, 0, 0, j)),
        ],
        out_shape=[
            jax.ShapeDtypeStruct((n, S5_WIDTH), F32),
            jax.ShapeDtypeStruct((nseq, 2, 2, S5_NSTATE), F32),
        ],
        scratch_shapes=[pltpu.VMEM((n, LANE), F32)],
        compiler_params=_params(("parallel",)),
        name="s5_scan",
    )(zs, x0, wmat, tab8, tab1)


def _merge_kernel(x_ref, mod_ref, nw_ref, oa_ref, ob_ref, ys_ref, sg_ref, wglu_ref, bglu_ref, wmg_ref,
                  wa_ref, wb_ref, wc_ref, wout_ref, y_ref, wmg_s, *, layer):
    @pl.when(pl.program_id(0) == 0)
    def _():
        _pack_transposed(wmg_ref.at[0], wmg_s, 0, 3 * D_MODEL // LANE, 0, None)

    x = x_ref[...]
    mod = mod_ref[0]
    h = _mod_rmsnorm(x, nw_ref[layer:layer + 1, :], mod).astype(BF16)
    zg = _dot(jax.nn.gelu(ys_ref[...]), wglu_ref[...]) + bglu_ref[layer:layer + 1, :]
    oc = (zg[:, :S5_WIDTH] * jax.nn.sigmoid(zg[:, S5_WIDTH:])
          * jax.nn.silu(sg_ref[...].astype(F32)))
    mixed = None
    for br, (o_br, w_ref) in enumerate(((oa_ref[...], wa_ref), (ob_ref[...], wb_ref), (oc, wc_ref))):
        gate = jax.nn.sigmoid(jnp.dot(h, wmg_s[:, br * D_MODEL:(br + 1) * D_MODEL], preferred_element_type=F32))
        term = gate * _dot(o_br, w_ref[...])
        mixed = term if mixed is None else mixed + term
    y_ref[...] = x + mod[:, 2 * D_MODEL:] * _dot(mixed, wout_ref[...])


def _merge(x2, row0, mod, mod_idx, nw, oa, ob, ys, zs, layer, wglu, bglu, w_in_t, wa, wb, wc, wout):
    n = x2.shape[0]
    tm = ROW_TILE
    blk0 = row0 // tm
    rows = lambda w: pl.BlockSpec((tm, w), lambda i: (i, 0))
    return pl.pallas_call(
        functools.partial(_merge_kernel, layer=layer),
        grid=(n // tm,),
        in_specs=[
            rows(D_MODEL),
            pl.BlockSpec((None, 1, 1, 3 * D_MODEL), lambda i: (layer, mod_idx(i), 0, 0)),
            _vector_spec(D_MODEL),
            rows(GLA_WIDTH), rows(MLA_WIDTH), rows(S5_WIDTH),
            pl.BlockSpec((tm, S5_WIDTH), lambda i: (i + blk0, ZS_GATE // S5_WIDTH)),
            pl.BlockSpec((None, S5_WIDTH, 2 * S5_WIDTH), lambda i: (layer, 0, 0)),
            _vector_spec(2 * S5_WIDTH),
            pl.BlockSpec((pl.Element(1), pl.Element(3 * D_MODEL), pl.Element(D_MODEL)),
                         lambda i: (layer, MERGE_COL, 0), pipeline_mode=pl.Buffered(1)),
            _layer_spec((GLA_WIDTH, D_MODEL), layer),
            _layer_spec((MLA_WIDTH, D_MODEL), layer),
            _layer_spec((S5_WIDTH, D_MODEL), layer),
            _layer_spec((D_MODEL, D_MODEL), layer),
        ],
        out_specs=rows(D_MODEL),
        out_shape=jax.ShapeDtypeStruct((n, D_MODEL), F32),
        scratch_shapes=[pltpu.VMEM((D_MODEL, 3 * D_MODEL), BF16)],
        compiler_params=_params(("arbitrary",)),
        name="merge",
    )(x2, mod, nw, oa, ob, ys, zs, wglu, bglu, w_in_t, wa, wb, wc, wout)


def _mla_lane_of_dim():
    half = MLA_ROPE // 2
    first_gap = ROPE_SHIFT - half
    lane = np.zeros(MLA_QK, np.int32)
    for j in range(MLA_NOPE):
        lane[j] = half + j if j < first_gap else 2 * half + j
    for r in range(half):
        lane[MLA_NOPE + r] = r
        lane[MLA_NOPE + half + r] = ROPE_SHIFT + r
    return lane


MLA_LANE_OF_DIM = _mla_lane_of_dim()


def _place_heads(w, heads, lane_of_dim):
    width = len(lane_of_dim)
    order = np.argsort(lane_of_dim)
    zeros = lambda n: jnp.zeros(w.shape[:-1] + (n,), w.dtype)
    pieces = []
    for h in range(heads):
        lane, i = 0, 0
        while i < width:
            j = i
            while (j + 1 < width and order[j + 1] == order[j] + 1
                   and lane_of_dim[order[j + 1]] == lane_of_dim[order[j]] + 1):
                j += 1
            dst = int(lane_of_dim[order[i]])
            if dst > lane:
                pieces.append(zeros(dst - lane))
            pieces.append(w[..., h * width + int(order[i]):h * width + int(order[j]) + 1])
            lane, i = dst + (j - i + 1), j + 1
        if lane < HEAD_PAD:
            pieces.append(zeros(HEAD_PAD - lane))
    return jnp.concatenate(pieces, axis=-1)


def _rope_tables(n_tok):
    rows = n_tok // GRID_W
    r = jnp.repeat(jnp.arange(rows, dtype=F32), GRID_W)
    col = jnp.tile(jnp.arange(GRID_W, dtype=F32), rows)
    n_freq = MLA_ROPE // 4
    inv = ROPE_THETA ** (-jnp.arange(n_freq, dtype=F32) / n_freq)
    ang = jnp.concatenate([r[:, None] * inv, col[:, None] * inv], axis=-1)
    cos, sin = jnp.cos(ang), jnp.sin(ang)
    ones = jnp.ones((n_tok, MLA_NOPE), F32)
    c = _place_heads(jnp.concatenate([ones, cos, cos], axis=1), 1, MLA_LANE_OF_DIM)
    s = _place_heads(jnp.concatenate([0.0 * ones, -sin, sin], axis=1), 1, MLA_LANE_OF_DIM)
    return jnp.stack([c, s])


def kernel(x_prompt, x_sample, c, c_ctx, cache_mla_ckv, cache_mla_krope, state_gla, state_s5,
           norm_w, w_ada, b_ada, w_in, gla_w_a2, gla_b_a, gla_o_norm,
           mla_q_norm, mla_w_uq, mla_kv_norm, mla_w_uk, mla_w_uv, mla_qh_norm, mla_kh_norm,
           s5_a_re, s5_a_im, s5_log_dt, s5_b_re, s5_b_im, s5_c_re, s5_c_im, s5_d, s5_w_glu, s5_b_glu,
           w_bo_gla, w_bo_mla, w_bo_s5, w_out):
    bsz, seq, _ = x_prompt.shape
    dbsz, dseq, _ = x_sample.shape
    ctx_row = COND_ROWS - 1
    assert dbsz <= ctx_row and (bsz * seq) % ROW_TILE == 0 and dseq % ROW_TILE == 0

    cond = jnp.zeros((COND_ROWS, D_MODEL), F32).at[0:dbsz].set(c).at[ctx_row].set(c_ctx)
    ada = _ada(cond, w_ada, b_ada)

    vec = lambda a: a.reshape(DEPTH, 2, 1, S5_NSTATE)
    ldt = jnp.repeat(s5_log_dt[..., None], S5_STATE, axis=-1)
    rows_gp = lambda t: t.reshape(DEPTH, S5_TILES, LANE, S5_STATE)
    bt = lambda b: rows_gp(b.transpose(0, 1, 3, 2))
    wmat, tab8, tab1 = _s5_prep(vec(s5_a_re), vec(s5_a_im), vec(ldt), bt(s5_b_re), bt(s5_b_im),
                                rows_gp(s5_c_re), rows_gp(s5_c_im), s5_d.reshape(DEPTH, 1, S5_WIDTH))
    wglu = s5_w_glu.astype(BF16)

    wuq = _place_heads(mla_w_uq, MLA_HEADS, MLA_LANE_OF_DIM).astype(BF16)
    wuk = _place_heads(mla_w_uk, MLA_HEADS, MLA_LANE_OF_DIM[:MLA_NOPE]).astype(BF16)
    wuv = mla_w_uv.astype(BF16)
    qhn = _place_heads(mla_qh_norm, 1, MLA_LANE_OF_DIM)
    khn = _place_heads(mla_kh_norm, 1, MLA_LANE_OF_DIM)
    e_np = np.zeros((LANE, MLA_HEADS * HEAD_PAD), np.float32)
    for h in range(MLA_HEADS):
        for i in range(MLA_ROPE):
            e_np[i, h * HEAD_PAD + MLA_LANE_OF_DIM[MLA_NOPE + i]] = 1.0
    e_place = jnp.asarray(e_np, BF16)
    rope_tab = _rope_tables(dseq)
    ckr_pad = jnp.pad(cache_mla_krope, ((0, 0), (0, 0), (0, 0), (0, LANE - MLA_ROPE)))

    zrow = lambda n: jnp.zeros((DEPTH, n, GLA_QK), F32)
    waf = jnp.concatenate([gla_w_a2[:, 0], zrow(LANE - GLA_RANK)], axis=1).astype(BF16)
    wab = jnp.concatenate([zrow(GLA_RANK), gla_w_a2[:, 1], zrow(LANE - 2 * GLA_RANK)], axis=1).astype(BF16)
    sgla = state_gla.reshape(dbsz, DEPTH, 2, GLA_QK, GLA_DV)
    ss5 = state_s5.reshape(dbsz, DEPTH, 2, 2, S5_NSTATE)
    zero_s5 = jnp.zeros((bsz, 2, 2, S5_NSTATE), F32)

    hp = x_prompt.reshape(bsz * seq, D_MODEL)
    hs = x_sample.reshape(dbsz * dseq, D_MODEL)
    ckv_l, krope_l, gla_l, s5_l = [], [], [], []
    w_in_t = jnp.swapaxes(w_in, 1, 2)
    mod = ada.reshape(DEPTH, COND_ROWS, 1, 3 * D_MODEL)
    mla_w = (mla_q_norm, wuq, mla_kv_norm, wuk, wuv, qhn, khn, e_place)
    wbo = (w_bo_gla.astype(BF16), w_bo_mla.astype(BF16), w_bo_s5.astype(BF16))
    wout = w_out.astype(BF16)
    for l in range(DEPTH):
        p_rows, p_blocks, blocks_per_seq = bsz * seq, bsz * seq // ROW_TILE, dseq // ROW_TILE
        mod_idx = lambda i: jnp.where(i < p_blocks, ctx_row, (i - p_blocks) // blocks_per_seq)
        zg, zm, zs = _in_proj(hp, hs, mod, mod_idx, norm_w, w_in_t, l)

        def mixers(x2, row0, nb, n, ctx):
            if ctx:
                gctx = sgla
                x0, x0_blk = ss5, (nb, None, 2, 2, S5_TILE_STATE)
                x0_idx = lambda j: (0, l, 0, 0, j)
                mctx, rt = (cache_mla_ckv, ckr_pad), rope_tab
            else:
                gctx = None
                x0, x0_blk = zero_s5, (nb, 2, 2, S5_TILE_STATE)
                x0_idx = lambda j: (0, 0, 0, j)
                mctx, rt = None, None
            oa, st_gla = _gla(zg, row0, gctx, l, waf, wab, gla_b_a, gla_o_norm, nb, n)
            ob, ckv = _mla(zm, row0, mctx, l, mla_w, rt, nb, n)
            y_ssm, st_s5 = _s5_scan(zs, row0, x0, x0_blk, x0_idx, l, wmat, tab8, tab1, nb, n)
            grp_mod_idx = lambda i: mod_idx(i + row0 // ROW_TILE)
            y = _merge(x2, row0, mod, grp_mod_idx, norm_w, oa, ob, y_ssm, zs, l, wglu, s5_b_glu, w_in_t, *wbo, wout)
            return y, ckv, st_gla, st_s5

        hp_next, ckv_p, st_gla_p, st_s5_p = mixers(hp, 0, bsz, seq, False)
        hs = mixers(hs, p_rows, dbsz, dseq, True)[0]
        hp = hp_next
        ckv_l.append(ckv_p.reshape(bsz, seq, MLA_KV_LORA))
        krope_l.append(zm[:p_rows, ZM_KR:ZM_KR + MLA_ROPE].astype(F32).reshape(bsz, seq, MLA_ROPE))
        gla_l.append(st_gla_p.reshape(bsz, 2, GLA_HEADS, GLA_DK, GLA_DV))
        s5_l.append(st_s5_p.reshape(bsz, 2, 2, S5_GROUPS, S5_STATE))

    return (hp.reshape(bsz, seq, D_MODEL), hs.reshape(dbsz, dseq, D_MODEL),
            jnp.stack(ckv_l, axis=1), jnp.stack(krope_l, axis=1),
            jnp.stack(gla_l, axis=1), jnp.stack(s5_l, axis=1))
```

```python
import functools

import jax
import jax.numpy as jnp
import numpy as np
from jax import lax
from jax.experimental import pallas as pl
from jax.experimental.pallas import tpu as pltpu

F32 = jnp.float32
BF16 = jnp.bfloat16

EPS = 1e-6
D_MODEL = 1024
DEPTH = 2
GRID_W = 64
ROPE_THETA = 10000.0
GLA_HEADS = 4
GLA_DK = 64
GLA_DV = 128
GLA_RANK = 16
GLA_GATE_NORM = 16.0
GLA_QK = GLA_HEADS * GLA_DK
GLA_WIDTH = GLA_HEADS * GLA_DV
MLA_HEADS = 4
MLA_Q_LORA = 384
MLA_KV_LORA = 256
MLA_NOPE = 64
MLA_ROPE = 32
MLA_QK = MLA_NOPE + MLA_ROPE
MLA_DV = 128
MLA_WIDTH = MLA_HEADS * MLA_DV
S5_WIDTH = 512
S5_GROUP = 16
S5_GROUPS = 32
S5_STATE = 64
S5_NSTATE = S5_GROUPS * S5_STATE

LANE = 128
SUBLANE = 8
COND_ROWS = SUBLANE
HEAD_PAD = LANE
ROPE_SHIFT = LANE // 2
CHUNK = 64
GLA_STEP = 256
GLA_SEQS_PER_STEP = 2
GLA_ROWS_PER_STEP = 1024
GLA_SKEW = 1
S5_TILES = S5_WIDTH // LANE
S5_TILE_STATE = S5_NSTATE // S5_TILES
ROW_TILE = 512
Q_TILE = 512
PROJ_TILE = 256
MLA_ROWS_PER_STEP = 1024
VMEM_LIMIT = 56 * 1024 * 1024

IN_SPLITS = (GLA_QK, GLA_QK, GLA_WIDTH, GLA_RANK, GLA_RANK, GLA_WIDTH,
             MLA_Q_LORA, MLA_KV_LORA, MLA_ROPE, MLA_WIDTH,
             S5_WIDTH, S5_WIDTH, 3 * D_MODEL)
(IN_GQ, IN_GK, IN_GV, IN_GA, IN_GAB, IN_GG, IN_MQ, IN_MKV, IN_MKR, IN_MG, IN_SU, IN_SG,
 MERGE_COL, D_IN) = (int(c) for c in np.cumsum((0,) + IN_SPLITS))
ZG_Q, ZG_K, ZG_V, ZG_A = 0, GLA_QK, 2 * GLA_QK, 2 * GLA_QK + GLA_WIDTH
ZG_GATE = ZG_A + LANE
ZG_W = ZG_GATE + GLA_WIDTH
ZM_Q, ZM_KV, ZM_KR = 0, MLA_Q_LORA, MLA_Q_LORA + MLA_KV_LORA
ZM_GATE = ZM_KR + LANE
ZM_W = ZM_GATE + MLA_WIDTH
ZS_U, ZS_GATE, ZS_W = 0, S5_WIDTH, 2 * S5_WIDTH
ZG_BASE, ZM_BASE, ZS_BASE, PACK_W = 0, ZG_W, ZG_W + ZM_W, ZG_W + ZM_W + ZS_W
IN_PIECES = (
    (IN_GQ, (IN_GA - IN_GQ) // LANE, ZG_BASE + ZG_Q, None),
    (IN_GA, 1, ZG_BASE + ZG_A, 2 * GLA_RANK),
    (IN_GG, GLA_WIDTH // LANE, ZG_BASE + ZG_GATE, None),
    (IN_MQ, (IN_MKR - IN_MQ) // LANE, ZM_BASE + ZM_Q, None),
    (IN_MKR, 1, ZM_BASE + ZM_KR, MLA_ROPE),
    (IN_MG, MLA_WIDTH // LANE, ZM_BASE + ZM_GATE, None),
    (IN_SU, (MERGE_COL - IN_SU) // LANE, ZS_BASE + ZS_U, None),
)


def _dot(a, b):
    return jnp.dot(a.astype(BF16), b.astype(BF16), preferred_element_type=F32)


def _dot_nt(a, b):
    return lax.dot_general(a.astype(BF16), b.astype(BF16), (((1,), (1,)), ((), ())),
                           preferred_element_type=F32)


def _split_bf16(x, parts):
    out = []
    r = x
    for _ in range(parts):
        p = r.astype(BF16)
        out.append(p)
        r = r - p.astype(F32)
    return out


def _emit_skewed(chains, skew):
    pending, active, tick = list(chains), [], 0
    while pending or active:
        while pending and (skew == 0 or tick % skew == 0):
            active.append(pending.pop(0))
            if skew:
                break
        for gen in list(active):
            if next(gen, "done") == "done":
                active.remove(gen)
        tick += 1


def _vector_spec(width):
    return pl.BlockSpec((DEPTH, width), lambda *_: (0, 0))


def _layer_spec(shape, layer):
    return pl.BlockSpec((None,) + tuple(shape), lambda *_: (layer,) + (0,) * len(shape))


def _params(sem):
    return pltpu.CompilerParams(dimension_semantics=sem, vmem_limit_bytes=VMEM_LIMIT)


def _ada_kernel(c_ref, w_ref, b_ref, o_ref):
    s = jax.nn.silu(c_ref[...])
    o_ref[...] = _dot(s, w_ref[...]) + b_ref[...]


def _ada(cond8, w_ada, b_ada):
    tn = 1024
    return pl.pallas_call(
        _ada_kernel,
        grid=(DEPTH, 3 * D_MODEL // tn),
        in_specs=[
            pl.BlockSpec((COND_ROWS, D_MODEL), lambda l, n: (0, 0)),
            pl.BlockSpec((None, D_MODEL, tn), lambda l, n: (l, 0, n)),
            pl.BlockSpec((None, 1, tn), lambda l, n: (l, 0, n)),
        ],
        out_specs=pl.BlockSpec((None, COND_ROWS, tn), lambda l, n: (l, 0, n)),
        out_shape=jax.ShapeDtypeStruct((DEPTH, COND_ROWS, 3 * D_MODEL), F32),
        compiler_params=_params(("parallel", "parallel")),
        name="ada",
    )(cond8, w_ada, b_ada.reshape(DEPTH, 1, 3 * D_MODEL))


def _mod_rmsnorm(x, nw, mod):
    ms = jnp.mean(x * x, axis=-1, keepdims=True)
    y = x * lax.rsqrt(ms + EPS) * nw
    return y * (1.0 + mod[:, D_MODEL:2 * D_MODEL]) + mod[:, 0:D_MODEL]


def _pack_transposed(w_ref, wb_s, src, tiles, dst, keep):
    lane = lax.broadcasted_iota(jnp.int32, (D_MODEL, LANE), 1)
    for t in range(tiles):
        blk = w_ref[src + t * LANE:src + (t + 1) * LANE, :].T
        if keep is not None:
            blk = jnp.where(lane < keep, blk, 0.0)
        wb_s[:, dst + t * LANE:dst + (t + 1) * LANE] = blk.astype(BF16)


def _in_proj_kernel(xp_ref, xs_ref, mod_ref, nw_ref, w_ref, zg_ref, zm_ref, zs_ref, wb_s, *, p_blocks, layer):
    i = pl.program_id(0)

    @pl.when(i == 0)
    def _():
        for src, tiles, dst, keep in IN_PIECES:
            _pack_transposed(w_ref, wb_s, src, tiles, dst, keep)

    x = jnp.where(i < p_blocks, xp_ref[...], xs_ref[...])
    h = _mod_rmsnorm(x, nw_ref[layer:layer + 1, :], mod_ref[0]).astype(BF16)
    z = jnp.dot(h, wb_s[...], preferred_element_type=F32)
    zg_ref[...] = z[:, ZG_BASE:ZG_BASE + ZG_W].astype(BF16)
    zm_ref[...] = z[:, ZM_BASE:ZM_BASE + ZM_W].astype(BF16)
    zs_ref[...] = z[:, ZS_BASE:ZS_BASE + ZS_W].astype(BF16)


def _two_group_rows(width, p_blocks):
    tm = ROW_TILE
    return (pl.BlockSpec((tm, width), lambda i: (jnp.minimum(i, p_blocks - 1), 0)),
            pl.BlockSpec((tm, width), lambda i: (jnp.maximum(i - p_blocks, 0), 0)))


def _in_proj(xp, xs, mod, mod_idx, nw, w_in_t, layer):
    tm = ROW_TILE
    p_blocks = xp.shape[0] // tm
    n = xp.shape[0] + xs.shape[0]
    return pl.pallas_call(
        functools.partial(_in_proj_kernel, p_blocks=p_blocks, layer=layer),
        grid=(n // tm,),
        in_specs=[
            *_two_group_rows(D_MODEL, p_blocks),
            pl.BlockSpec((None, 1, 1, 3 * D_MODEL), lambda i: (layer, mod_idx(i), 0, 0)),
            _vector_spec(D_MODEL),
            pl.BlockSpec((None, MERGE_COL, D_MODEL), lambda i: (layer, 0, 0), pipeline_mode=pl.Buffered(1)),
        ],
        out_specs=[
            pl.BlockSpec((tm, ZG_W), lambda i: (i, 0)),
            pl.BlockSpec((tm, ZM_W), lambda i: (i, 0)),
            pl.BlockSpec((tm, ZS_W), lambda i: (i, 0)),
        ],
        out_shape=[
            jax.ShapeDtypeStruct((n, ZG_W), BF16),
            jax.ShapeDtypeStruct((n, ZM_W), BF16),
            jax.ShapeDtypeStruct((n, ZS_W), BF16),
        ],
        scratch_shapes=[pltpu.VMEM((D_MODEL, PACK_W), BF16)],
        compiler_params=_params(("arbitrary",)),
        name="in_proj",
    )(xp, xs, mod, nw, w_in_t)


def _gla_kernel(*refs, nsteps, seq, nseq, has_ctx, layer):
    it = iter(refs)
    zg_ref = next(it)
    s0_ref = next(it) if has_ctx else None
    waf_ref, wab_ref, ba_ref, onorm_ref, o_ref, sfin_ref, la_s, o_s, st_s = (next(it) for _ in range(9))
    chains = [(g, d) for g in range(nseq) for d in (0, 1)]
    inv_norm = 1.0 / GLA_GATE_NORM
    zero_blk = jnp.zeros((GLA_DK, GLA_DV), F32)
    for ch, (g, d) in enumerate(chains):
        if d == 0:
            a_blk = zg_ref[g * seq:(g + 1) * seq, ZG_A:ZG_A + LANE]
        wa_ref = waf_ref if d == 0 else wab_ref
        a_low = _dot(a_blk, wa_ref[...]) + ba_ref[d:d + 1, :]
        la_s[ch] = (jnp.minimum(a_low, 0.0) - jnp.log(1.0 + jnp.exp(-jnp.abs(a_low)))) * inv_norm
        if has_ctx:
            s0 = s0_ref[g, d]
            rows_bd = []
            for h in range(GLA_HEADS):
                sh = s0[h * GLA_DK:(h + 1) * GLA_DK, :]
                rows_bd.append(jnp.concatenate([sh if h2 == h else zero_blk for h2 in range(GLA_HEADS)], axis=1))
            st_s[ch] = jnp.concatenate(rows_bd, axis=0).T
        else:
            st_s[ch] = jnp.zeros((GLA_WIDTH, GLA_QK), F32)

    def iota(shape, axis, shift):
        return lax.shift_right_logical(lax.broadcasted_iota(jnp.int32, shape, axis), shift)

    log_chunk, log_dv = CHUNK.bit_length() - 1, GLA_DV.bit_length() - 1
    row = lax.broadcasted_iota(jnp.int32, (GLA_STEP, GLA_STEP), 0)
    col = lax.broadcasted_iota(jnp.int32, (GLA_STEP, GLA_STEP), 1)
    same_chunk = iota((GLA_STEP, GLA_STEP), 0, log_chunk) == iota((GLA_STEP, GLA_STEP), 1, log_chunk)
    masks = (same_chunk & (row >= col), same_chunk & (row <= col))
    lane_head = iota((GLA_STEP, GLA_QK), 1, log_chunk)
    row_chunk = iota((GLA_STEP, GLA_QK), 0, log_chunk)
    state_blk = iota((GLA_WIDTH, GLA_QK), 0, log_dv) == iota((GLA_WIDTH, GLA_QK), 1, log_chunk)
    qscale = GLA_DK ** -0.5
    nch = GLA_STEP // CHUNK

    def chain_phases(i, ch, g, d):
        r0 = pl.multiple_of((i if d == 0 else nsteps - 1 - i) * GLA_STEP, GLA_STEP)
        zrows, rows = pl.ds(g * seq + r0, GLA_STEP), pl.ds(r0, GLA_STEP)
        a_hi, a_lo = _split_bf16(la_s[ch, rows, :], 2)
        tri = masks[d].astype(BF16)
        cum = (jnp.dot(tri, a_hi, preferred_element_type=F32)
               + jnp.dot(tri, a_lo, preferred_element_type=F32))
        yield
        edge = CHUNK - 1 if d == 0 else 0
        blast = [cum[c * CHUNK + edge:c * CHUNK + edge + 1, :] for c in range(nch)]
        bl = jnp.concatenate([jnp.broadcast_to(b, (CHUNK, GLA_QK)) for b in blast], axis=0)
        q = zg_ref[zrows, ZG_Q:ZG_Q + GLA_QK].astype(F32) * qscale
        k = zg_ref[zrows, ZG_K:ZG_K + GLA_QK].astype(F32)
        v = zg_ref[zrows, ZG_V:ZG_V + GLA_WIDTH]
        v_t = v.astype(F32).T.astype(BF16)
        qd = q * jnp.exp(cum)
        kd = (k * jnp.exp(-cum)).astype(BF16)
        kr = k * jnp.exp(bl - cum)
        yield
        outs = []
        for h in range(GLA_HEADS):
            qh = jnp.where(lane_head == h, qd, 0.0)
            att = _dot_nt(qh, kd)
            yield
            att = jnp.where(masks[d], att, 0.0)
            outs.append(_dot(att, v[:, h * GLA_DV:(h + 1) * GLA_DV]))
            yield
        s = st_s[ch]
        inter = [None] * nch
        for c in (range(nch) if d == 0 else reversed(range(nch))):
            inter[c] = _dot_nt(qd[c * CHUNK:(c + 1) * CHUNK, :], s)
            kv_t = _dot(v_t, jnp.where(row_chunk == c, kr, 0.0))
            yield
            s = s * jnp.exp(blast[c]) + jnp.where(state_blk, kv_t, 0.0)
            yield
        st_s[ch] = s
        o_s[ch, rows, :] = jnp.concatenate(outs, axis=1) + jnp.concatenate(inter, axis=0)

    def step(i, carry):
        _emit_skewed([chain_phases(i, ch, g, d) for ch, (g, d) in enumerate(chains)], GLA_SKEW)
        return carry

    lax.fori_loop(0, nsteps, step, 0)
    onorm = onorm_ref[layer:layer + 1, :]
    for ch, (g, d) in enumerate(chains):
        s_fin = st_s[ch].T
        for h in range(GLA_HEADS):
            sfin_ref[g, d, h * GLA_DK:(h + 1) * GLA_DK, :] = (
                s_fin[h * GLA_DK:(h + 1) * GLA_DK, h * GLA_DV:(h + 1) * GLA_DV])
    for g in range(nseq):
        srows = slice(g * seq, (g + 1) * seq)
        o = o_s[2 * g] + o_s[2 * g + 1]
        gate = zg_ref[srows, ZG_GATE:ZG_GATE + GLA_WIDTH].astype(F32)
        for h in range(GLA_HEADS):
            vs = slice(h * GLA_DV, (h + 1) * GLA_DV)
            oh = o[:, vs]
            ms = jnp.mean(oh * oh, axis=-1, keepdims=True)
            o_ref[srows, vs] = oh * lax.rsqrt(ms + EPS) * onorm * jax.nn.silu(gate[:, vs])


def _gla(zg, row0, ctx, layer, waf, wab, ba, onorm, bsz, seq):
    nseq = max(GLA_SEQS_PER_STEP, GLA_ROWS_PER_STEP // seq)
    blk0 = row0 // (nseq * seq)
    in_specs = [pl.BlockSpec((nseq * seq, ZG_W), lambda b: (b + blk0, 0))]
    args = [zg]
    if ctx is not None:
        in_specs.append(pl.BlockSpec((nseq, None, 2, GLA_QK, GLA_DV), lambda b: (b, layer, 0, 0, 0)))
        args.append(ctx)
    in_specs += [
        _layer_spec((LANE, GLA_QK), layer),
        _layer_spec((LANE, GLA_QK), layer),
        _layer_spec((2, GLA_QK), layer),
        _vector_spec(GLA_DV),
    ]
    return pl.pallas_call(
        functools.partial(_gla_kernel, nsteps=seq // GLA_STEP, seq=seq, nseq=nseq, has_ctx=ctx is not None,
                          layer=layer),
        grid=(bsz // nseq,),
        in_specs=in_specs,
        out_specs=[
            pl.BlockSpec((nseq * seq, GLA_WIDTH), lambda b: (b, 0)),
            pl.BlockSpec((nseq, 2, GLA_QK, GLA_DV), lambda b: (b, 0, 0, 0)),
        ],
        out_shape=[
            jax.ShapeDtypeStruct((bsz * seq, GLA_WIDTH), F32),
            jax.ShapeDtypeStruct((bsz, 2, GLA_QK, GLA_DV), F32),
        ],
        scratch_shapes=[
            pltpu.VMEM((2 * nseq, seq, GLA_QK), F32),
            pltpu.VMEM((2 * nseq, seq, GLA_WIDTH), F32),
            pltpu.VMEM((2 * nseq, GLA_WIDTH, GLA_QK), F32),
        ],
        compiler_params=_params(("parallel",)),
        name="gla",
    )(*args, waf, wab, ba, onorm)


def _rms(x, w):
    ms = jnp.mean(x * x, axis=-1, keepdims=True)
    return x * lax.rsqrt(ms + EPS) * w


def _head_sums_mxu(x):
    width = x.shape[-1]
    shift = HEAD_PAD.bit_length() - 1
    gi = lax.shift_right_logical(lax.broadcasted_iota(jnp.int32, (width, width), 0), shift)
    gj = lax.shift_right_logical(lax.broadcasted_iota(jnp.int32, (width, width), 1), shift)
    return _dot(x * x, jnp.where(gi == gj, 1.0, 0.0))


def _head_norm(x, w, rope, on_mxu):
    sums = _head_sums_mxu(x) if on_mxu else None
    outs = []
    for h in range(MLA_HEADS):
        hs = slice(h * HEAD_PAD, (h + 1) * HEAD_PAD)
        xh = x[:, hs]
        ss = sums[:, hs] if on_mxu else jnp.sum(xh * xh, axis=-1, keepdims=True)
        yh = xh * lax.rsqrt(ss * (1.0 / MLA_QK) + EPS) * w
        if rope is not None:
            c, s = rope
            yh = yh * c + pltpu.roll(yh, ROPE_SHIFT, 1) * s
        outs.append(yh)
    return outs


def _place_rope_key(kr, e):
    return sum(jnp.dot(p, e, preferred_element_type=F32) for p in _split_bf16(kr, 3))


def _mla_kernel(*refs, seq, nseq, n_ctx, use_rope, layer):
    it = iter(refs)
    zm_ref = next(it)
    if n_ctx:
        cckv_ref, ckr_ref = next(it), next(it)
    qn_ref, wuq_ref, kvn_ref, wuk_ref, wuv_ref, qhn_ref, khn_ref, e_ref = (next(it) for _ in range(8))
    rope_ref = next(it) if use_rope else None
    o_ref, ckv_ref = next(it), next(it)
    q_s, k_s, v_s = next(it), next(it), next(it)
    qn, kvn, qhn, khn = (r[layer:layer + 1, :] for r in (qn_ref, kvn_ref, qhn_ref, khn_ref))

    qscale = MLA_QK ** -0.5
    heads = [slice(h * HEAD_PAD, (h + 1) * HEAD_PAD) for h in range(MLA_HEADS)]

    def keys_values(g, ckv, k_rope_placed, rope, k_rows):
        k_raw = _dot(ckv, wuk_ref[...]) + k_rope_placed
        yield
        kh = _head_norm(k_raw, khn, rope, False)
        for h, hs in enumerate(heads):
            k_s[g, k_rows, hs] = kh[h].astype(BF16)
        yield
        v_s[g, k_rows, :] = _dot(ckv, wuv_ref[...]).astype(BF16)
        yield

    def latent_phases(i, g):
        r0 = pl.multiple_of(i * PROJ_TILE, PROJ_TILE)
        tile, rows = pl.ds(r0, PROJ_TILE), pl.ds(g * seq + r0, PROJ_TILE)
        rope = (rope_ref[0, tile, :], rope_ref[1, tile, :]) if use_rope else None
        ckv = _rms(zm_ref[rows, ZM_KV:ZM_KV + MLA_KV_LORA].astype(F32), kvn)
        ckv_ref[rows, :] = ckv
        k_pe = jnp.dot(zm_ref[rows, ZM_KR:ZM_KR + LANE], e_ref[...], preferred_element_type=F32)
        yield from keys_values(g, ckv, k_pe, rope, pl.ds(n_ctx + r0, PROJ_TILE))
        cq = _rms(zm_ref[rows, ZM_Q:ZM_Q + MLA_Q_LORA].astype(F32), qn)
        q_raw = _dot(cq, wuq_ref[...])
        yield
        qh = _head_norm(q_raw, qhn, rope, True)
        for h, hs in enumerate(heads):
            q_s[rows, hs] = (qh[h] * qscale).astype(BF16)

    def latent_tile(i, carry):
        _emit_skewed([latent_phases(i, g) for g in range(nseq)], 0)
        return carry

    lax.fori_loop(0, seq // PROJ_TILE, latent_tile, 0)

    def context_tile(i, carry):
        rows = pl.ds(pl.multiple_of(i * PROJ_TILE, PROJ_TILE), PROJ_TILE)
        _emit_skewed([keys_values(g, cckv_ref[g, rows, :], _place_rope_key(ckr_ref[g, rows, :], e_ref[...]),
                                  None, rows) for g in range(nseq)], 0)
        return carry

    if n_ctx:
        lax.fori_loop(0, n_ctx // PROJ_TILE, context_tile, 0)

    q_tile = min(seq, Q_TILE)

    def head_phases(g, h, hs, rows, gate):
        s = lax.dot_general(q_s[rows, hs], k_s[g, :, hs], (((1,), (1,)), ((), ())),
                            preferred_element_type=F32)
        yield
        e = jnp.exp(s - jnp.max(s, axis=-1, keepdims=True))
        l = jnp.sum(e, axis=-1, keepdims=True)
        p = e.astype(BF16)
        yield
        o = jnp.dot(p, v_s[g, :, hs], preferred_element_type=F32) / l
        o_ref[rows, hs] = o * jax.nn.silu(gate[:, hs])

    def q_block(i, carry):
        chains = []
        for g in range(nseq):
            rows = pl.ds(g * seq + pl.multiple_of(i * q_tile, q_tile), q_tile)
            gate = zm_ref[rows, ZM_GATE:ZM_GATE + MLA_WIDTH].astype(F32)
            chains += [head_phases(g, h, hs, rows, gate) for h, hs in enumerate(heads)]
        _emit_skewed(chains, 0)
        return carry

    lax.fori_loop(0, seq // q_tile, q_block, 0)


def _mla(zm, row0, ctx, layer, w, rope_tab, bsz, seq):
    n_ctx = 0 if ctx is None else ctx[0].shape[-2]
    nseq = max(1, MLA_ROWS_PER_STEP // seq)
    blk0 = row0 // (nseq * seq)
    in_specs = [pl.BlockSpec((nseq * seq, ZM_W), lambda b: (b + blk0, 0))]
    args = [zm]
    if ctx is not None:
        cckv, ckr = ctx
        in_specs += [
            pl.BlockSpec((nseq, None, n_ctx, MLA_KV_LORA), lambda b: (b, layer, 0, 0)),
            pl.BlockSpec((nseq, None, n_ctx, LANE), lambda b: (b, layer, 0, 0)),
        ]
        args += [cckv, ckr]
    in_specs += [
        _vector_spec(MLA_Q_LORA),
        _layer_spec((MLA_Q_LORA, MLA_HEADS * HEAD_PAD), layer),
        _vector_spec(MLA_KV_LORA),
        _layer_spec((MLA_KV_LORA, MLA_HEADS * HEAD_PAD), layer),
        _layer_spec((MLA_KV_LORA, MLA_WIDTH), layer),
        _vector_spec(HEAD_PAD),
        _vector_spec(HEAD_PAD),
        pl.BlockSpec((LANE, MLA_HEADS * HEAD_PAD), lambda b: (0, 0)),
    ]
    args += list(w)
    if rope_tab is not None:
        in_specs.append(pl.BlockSpec((2, seq, HEAD_PAD), lambda b: (0, 0, 0)))
        args.append(rope_tab)
    return pl.pallas_call(
        functools.partial(_mla_kernel, seq=seq, nseq=nseq, n_ctx=n_ctx, use_rope=rope_tab is not None,
                          layer=layer),
        grid=(bsz // nseq,),
        in_specs=in_specs,
        out_specs=[
            pl.BlockSpec((nseq * seq, MLA_WIDTH), lambda b: (b, 0)),
            pl.BlockSpec((nseq * seq, MLA_KV_LORA), lambda b: (b, 0)),
        ],
        out_shape=[
            jax.ShapeDtypeStruct((bsz * seq, MLA_WIDTH), F32),
            jax.ShapeDtypeStruct((bsz * seq, MLA_KV_LORA), F32),
        ],
        scratch_shapes=[
            pltpu.VMEM((nseq * seq, MLA_HEADS * HEAD_PAD), BF16),
            pltpu.VMEM((nseq, n_ctx + seq, MLA_HEADS * HEAD_PAD), BF16),
            pltpu.VMEM((nseq, n_ctx + seq, MLA_WIDTH), BF16),
        ],
        compiler_params=_params(("parallel",)),
        name="mla",
    )(*args)


S5_T = 8
S5_R = CHUNK // S5_T
S5_SUB_CH = 64
S5_SUBS = LANE // S5_SUB_CH
S5_SUB_STATE = S5_TILE_STATE // S5_SUBS
S5_ROW = S5_T * S5_SUB_CH
S5_W = 2 * S5_SUB_STATE
W_M, W_SF, W_SB, W_CF, W_CB = range(5)


def _cmul(ar, ai, br, bi):
    return ar * br - ai * bi, ar * bi + ai * br


def _s5_prep_kernel(are_ref, aim_ref, ldt_ref, bre_ref, bim_ref, cre_ref, cim_ref, d_ref,
                    w_ref, tab8_ref, tab1_ref):
    gr = lax.shift_right_logical(lax.broadcasted_iota(jnp.int32, (S5_SUB_CH, S5_SUB_STATE), 0),
                                 S5_GROUP.bit_length() - 1)
    gc = lax.shift_right_logical(lax.broadcasted_iota(jnp.int32, (S5_SUB_CH, S5_SUB_STATE), 1),
                                 S5_STATE.bit_length() - 1)

    def spread(ref, h):
        x = ref[h * S5_SUB_CH:(h + 1) * S5_SUB_CH, :]
        return jnp.where(gr == gc, jnp.concatenate([x] * (S5_SUB_CH // S5_GROUP), axis=1), 0.0)

    row = lax.broadcasted_iota(jnp.int32, (S5_SUB_CH, S5_SUB_CH), 0)
    col = lax.broadcasted_iota(jnp.int32, (S5_SUB_CH, S5_SUB_CH), 1)
    taps = [[[], []] for _ in range(S5_SUBS)]
    for d in (0, 1):
        prow = pl.ds(2 * pl.program_id(0) + d, 1)
        a_re, a_im = are_ref[prow, :], aim_ref[prow, :]
        dt = jnp.exp(ldt_ref[prow, :])
        lam = a_re * dt
        th = a_im * dt
        mag = jnp.exp(lam)
        ab_re = mag * jnp.cos(th)
        ab_im = mag * jnp.sin(th)
        den = a_re * a_re + a_im * a_im
        n_re = ab_re - 1.0
        cf_re = (n_re * a_re + ab_im * a_im) / den
        cf_im = (ab_im * a_re - n_re * a_im) / den
        k = lax.broadcasted_iota(jnp.int32, (2 * S5_T, S5_TILE_STATE), 0).astype(F32)
        pmag = jnp.exp(k * lam)
        pw_re = pmag * jnp.cos(k * th)
        pw_im = pmag * jnp.sin(k * th)
        for h in range(S5_SUBS):
            ss = slice(h * S5_SUB_STATE, (h + 1) * S5_SUB_STATE)
            c_re, c_im = spread(cre_ref, h), spread(cim_ref, h)
            c_cat = jnp.concatenate([c_re, c_im], axis=1).astype(BF16)
            bp_re, bp_im = _cmul(spread(bre_ref, h), spread(bim_ref, h), cf_re[:, ss], cf_im[:, ss])
            for p in range(S5_T + 1):
                ar, ai = pw_re[p:p + 1, ss], pw_im[p:p + 1, ss]
                t_in = S5_T - 1 - p if d == 0 else p
                t_out = p - 1 if d == 0 else S5_T - p
                if p < S5_T:
                    l_re, l_im = _cmul(bp_re, bp_im, ar, ai)
                    w_ref[h, W_SF + d, t_in * S5_SUB_CH:(t_in + 1) * S5_SUB_CH, :] = (
                        jnp.concatenate([l_re, l_im], axis=1).astype(BF16))
                    taps[h][d].append(_dot_nt(jnp.concatenate([l_re, -l_im], axis=1), c_cat))
                if p > 0:
                    v_re, v_im = _cmul(c_re, c_im, ar, ai)
                    w_ref[h, W_CF + d, t_out * S5_SUB_CH:(t_out + 1) * S5_SUB_CH, :] = (
                        jnp.concatenate([v_re, -v_im], axis=1).astype(BF16))
        r = lax.broadcasted_iota(jnp.int32, (S5_R, S5_TILE_STATE), 0).astype(F32) * float(S5_T)
        r1 = r + float(S5_T)
        pm = jnp.exp(r * lam)
        qm = jnp.exp(-(r1 * lam))
        tab8_ref[d, 0] = pm * jnp.cos(r * th)
        tab8_ref[d, 1] = pm * jnp.sin(r * th)
        tab8_ref[d, 2] = qm * jnp.cos(r1 * th)
        tab8_ref[d, 3] = -(qm * jnp.sin(r1 * th))
        mc = jnp.exp(float(CHUNK) * lam)
        tab1_ref[d, 0:1, :] = mc * jnp.cos(float(CHUNK) * th)
        tab1_ref[d, 1:2, :] = mc * jnp.sin(float(CHUNK) * th)
    for h in range(S5_SUBS):
        skip = jnp.where(row == col, d_ref[:, h * S5_SUB_CH:(h + 1) * S5_SUB_CH], 0.0)
        for t in range(S5_T):
            blocks = []
            for t2 in range(S5_T):
                if t < t2:
                    blocks.append(taps[h][0][t2 - t])
                elif t > t2:
                    blocks.append(taps[h][1][t - t2])
                else:
                    blocks.append(taps[h][0][0] + taps[h][1][0] + skip)
            w_ref[h, W_M, t * S5_SUB_CH:(t + 1) * S5_SUB_CH, :] = jnp.concatenate(blocks, axis=1).astype(BF16)


def _s5_prep(a_re, a_im, ldt, b_re, b_im, c_re, c_im, dsk):
    vec = pl.BlockSpec((2 * DEPTH, S5_TILE_STATE), lambda l, j: (0, j))
    blk = pl.BlockSpec((None, None, LANE, S5_STATE), lambda l, j: (l, j, 0, 0))
    return pl.pallas_call(
        _s5_prep_kernel,
        grid=(DEPTH, S5_TILES),
        in_specs=[vec, vec, vec, blk, blk, blk, blk,
                  pl.BlockSpec((None, 1, LANE), lambda l, j: (l, 0, j))],
        out_specs=[
            pl.BlockSpec((None, None, S5_SUBS, 5, S5_ROW, S5_W), lambda l, j: (l, j, 0, 0, 0, 0)),
            pl.BlockSpec((None, None, 2, 4, S5_R, S5_TILE_STATE), lambda l, j: (l, j, 0, 0, 0, 0)),
            pl.BlockSpec((None, None, 2, 2, S5_TILE_STATE), lambda l, j: (l, j, 0, 0, 0)),
        ],
        out_shape=[
            jax.ShapeDtypeStruct((DEPTH, S5_TILES, S5_SUBS, 5, S5_ROW, S5_W), BF16),
            jax.ShapeDtypeStruct((DEPTH, S5_TILES, 2, 4, S5_R, S5_TILE_STATE), F32),
            jax.ShapeDtypeStruct((DEPTH, S5_TILES, 2, 2, S5_TILE_STATE), F32),
        ],
        compiler_params=_params(("parallel", "parallel")),
        name="s5_prep",
    )(a_re, a_im, ldt, b_re, b_im, c_re, c_im, dsk)


def _s5_scan_kernel(u_ref, x0_ref, w_ref, tab8_ref, tab1_ref, y_ref, fs_ref, u_s, *, nseq, nb):
    groups = nseq * nb
    nrow = groups * S5_R
    ts = S5_SUB_STATE
    u_s[...] = u_ref[...].astype(F32)
    tokens = [u_s[pl.ds(t, nrow, stride=S5_T), :] for t in range(S5_T)]
    rowi = lax.broadcasted_iota(jnp.int32, (groups, S5_R, ts), 1)

    def prefix(x):
        for s in (1, 2, 4):
            x = x + jnp.where(rowi >= s, pltpu.roll(x, s, 1), 0.0)
        return x

    def suffix(x):
        for s in (1, 2, 4):
            x = x + jnp.where(rowi < S5_R - s, pltpu.roll(x, S5_R - s, 1), 0.0)
        return x

    y_sub = []
    for h in range(S5_SUBS):
        ch = slice(h * S5_SUB_CH, (h + 1) * S5_SUB_CH)
        ss = slice(h * ts, (h + 1) * ts)
        u8 = jnp.concatenate([tok[:, ch] for tok in tokens], axis=1).astype(BF16)
        ef = jnp.dot(u8, w_ref[h, W_SF], preferred_element_type=F32).reshape(groups, S5_R, S5_W)
        eb = jnp.dot(u8, w_ref[h, W_SB], preferred_element_type=F32).reshape(groups, S5_R, S5_W)

        p_re, p_im, q_re, q_im = (tab8_ref[0, i, :, ss] for i in range(4))
        a_re, a_im = tab1_ref[0, 0:1, ss], tab1_ref[0, 1:2, ss]
        w_re, w_im = _cmul(q_re, q_im, ef[:, :, :ts], ef[:, :, ts:])
        cs_re, cs_im = prefix(w_re), prefix(w_im)
        st_re, st_im = [], []
        for s in range(nseq):
            x_re, x_im = x0_ref[s, 0, 0:1, ss], x0_ref[s, 0, 1:2, ss]
            for b in range(nb):
                g = s * nb + b
                st_re.append(x_re)
                st_im.append(x_im)
                x_re, x_im = _cmul(a_re, a_im, x_re + cs_re[g, S5_R - 1:S5_R, :],
                                   x_im + cs_im[g, S5_R - 1:S5_R, :])
            fs_ref[s, 0, 0:1, ss] = x_re
            fs_ref[s, 0, 1:2, ss] = x_im
        xin_re, xin_im = _cmul(p_re, p_im, cs_re - w_re + jnp.stack(st_re), cs_im - w_im + jnp.stack(st_im))
        xin = jnp.concatenate([xin_re, xin_im], axis=2).reshape(nrow, S5_W)

        p_re, p_im, q_re, q_im = (tab8_ref[1, i, :, ss] for i in range(4))
        a_re, a_im = tab1_ref[1, 0:1, ss], tab1_ref[1, 1:2, ss]
        w_re, w_im = _cmul(p_re, p_im, eb[:, :, :ts], eb[:, :, ts:])
        sf_re, sf_im = suffix(w_re), suffix(w_im)
        z_re, z_im = [None] * groups, [None] * groups
        for s in range(nseq):
            x_re, x_im = x0_ref[s, 1, 0:1, ss], x0_ref[s, 1, 1:2, ss]
            for b in reversed(range(nb)):
                g = s * nb + b
                z_re[g], z_im[g] = _cmul(a_re, a_im, x_re, x_im)
                x_re = sf_re[g, 0:1, :] + z_re[g]
                x_im = sf_im[g, 0:1, :] + z_im[g]
            fs_ref[s, 1, 0:1, ss] = x_re
            fs_ref[s, 1, 1:2, ss] = x_im
        xnx_re, xnx_im = _cmul(q_re, q_im, sf_re - w_re + jnp.stack(z_re), sf_im - w_im + jnp.stack(z_im))
        xnx = jnp.concatenate([xnx_re, xnx_im], axis=2).reshape(nrow, S5_W)

        y_sub.append(jnp.dot(u8, w_ref[h, W_M], preferred_element_type=F32)
                     + _dot_nt(xin, w_ref[h, W_CF]) + _dot_nt(xnx, w_ref[h, W_CB]))
    for t in range(S5_T):
        tc = slice(t * S5_SUB_CH, (t + 1) * S5_SUB_CH)
        y_ref[pl.ds(t, nrow, stride=S5_T), :] = jnp.concatenate([y[:, tc] for y in y_sub], axis=1)


def _s5_scan(zs, row0, x0, x0_block, x0_idx, layer, wmat, tab8, tab1, nseq, seq):
    n = nseq * seq
    rblk = row0 // n
    return pl.pallas_call(
        functools.partial(_s5_scan_kernel, nseq=nseq, nb=seq // CHUNK),
        grid=(S5_TILES,),
        in_specs=[
            pl.BlockSpec((n, LANE), lambda j: (rblk, ZS_U // LANE + j)),
            pl.BlockSpec(x0_block, x0_idx),
            pl.BlockSpec((None, None, S5_SUBS, 5, S5_ROW, S5_W), lambda j: (layer, j, 0, 0, 0, 0)),
            pl.BlockSpec((None, None, 2, 4, S5_R, S5_TILE_STATE), lambda j: (layer, j, 0, 0, 0, 0)),
            pl.BlockSpec((None, None, 2, 2, S5_TILE_STATE), lambda j: (layer, j, 0, 0, 0)),
        ],
        out_specs=[
            pl.BlockSpec((n, LANE), lambda j: (0, j)),
            pl.BlockSpec((nseq, 2, 2, S5_TILE_STATE), lambda j: (0, 0, 0, j)),
        ],
        out_shape=[
            jax.ShapeDtypeStruct((n, S5_WIDTH), F32),
            jax.ShapeDtypeStruct((nseq, 2, 2, S5_NSTATE), F32),
        ],
        scratch_shapes=[pltpu.VMEM((n, LANE), F32)],
        compiler_params=_params(("parallel",)),
        name="s5_scan",
    )(zs, x0, wmat, tab8, tab1)


def _merge_kernel(x_ref, mod_ref, nw_ref, oa_ref, ob_ref, ys_ref, sg_ref, wglu_ref, bglu_ref, wmg_ref,
                  wa_ref, wb_ref, wc_ref, wout_ref, y_ref, wmg_s, *, layer):
    @pl.when(pl.program_id(0) == 0)
    def _():
        _pack_transposed(wmg_ref.at[0], wmg_s, 0, 3 * D_MODEL // LANE, 0, None)

    x = x_ref[...]
    mod = mod_ref[0]
    h = _mod_rmsnorm(x, nw_ref[layer:layer + 1, :], mod).astype(BF16)
    zg = _dot(jax.nn.gelu(ys_ref[...]), wglu_ref[...]) + bglu_ref[layer:layer + 1, :]
    oc = (zg[:, :S5_WIDTH] * jax.nn.sigmoid(zg[:, S5_WIDTH:])
          * jax.nn.silu(sg_ref[...].astype(F32)))
    mixed = None
    for br, (o_br, w_ref) in enumerate(((oa_ref[...], wa_ref), (ob_ref[...], wb_ref), (oc, wc_ref))):
        gate = jax.nn.sigmoid(jnp.dot(h, wmg_s[:, br * D_MODEL:(br + 1) * D_MODEL], preferred_element_type=F32))
        term = gate * _dot(o_br, w_ref[...])
        mixed = term if mixed is None else mixed + term
    y_ref[...] = x + mod[:, 2 * D_MODEL:] * _dot(mixed, wout_ref[...])


def _merge(x2, row0, mod, mod_idx, nw, oa, ob, ys, zs, layer, wglu, bglu, w_in_t, wa, wb, wc, wout):
    n = x2.shape[0]
    tm = ROW_TILE
    blk0 = row0 // tm
    rows = lambda w: pl.BlockSpec((tm, w), lambda i: (i, 0))
    return pl.pallas_call(
        functools.partial(_merge_kernel, layer=layer),
        grid=(n // tm,),
        in_specs=[
            rows(D_MODEL),
            pl.BlockSpec((None, 1, 1, 3 * D_MODEL), lambda i: (layer, mod_idx(i), 0, 0)),
            _vector_spec(D_MODEL),
            rows(GLA_WIDTH), rows(MLA_WIDTH), rows(S5_WIDTH),
            pl.BlockSpec((tm, S5_WIDTH), lambda i: (i + blk0, ZS_GATE // S5_WIDTH)),
            pl.BlockSpec((None, S5_WIDTH, 2 * S5_WIDTH), lambda i: (layer, 0, 0)),
            _vector_spec(2 * S5_WIDTH),
            pl.BlockSpec((pl.Element(1), pl.Element(3 * D_MODEL), pl.Element(D_MODEL)),
                         lambda i: (layer, MERGE_COL, 0), pipeline_mode=pl.Buffered(1)),
            _layer_spec((GLA_WIDTH, D_MODEL), layer),
            _layer_spec((MLA_WIDTH, D_MODEL), layer),
            _layer_spec((S5_WIDTH, D_MODEL), layer),
            _layer_spec((D_MODEL, D_MODEL), layer),
        ],
        out_specs=rows(D_MODEL),
        out_shape=jax.ShapeDtypeStruct((n, D_MODEL), F32),
        scratch_shapes=[pltpu.VMEM((D_MODEL, 3 * D_MODEL), BF16)],
        compiler_params=_params(("arbitrary",)),
        name="merge",
    )(x2, mod, nw, oa, ob, ys, zs, wglu, bglu, w_in_t, wa, wb, wc, wout)


def _mla_lane_of_dim():
    half = MLA_ROPE // 2
    first_gap = ROPE_SHIFT - half
    lane = np.zeros(MLA_QK, np.int32)
    for j in range(MLA_NOPE):
        lane[j] = half + j if j < first_gap else 2 * half + j
    for r in range(half):
        lane[MLA_NOPE + r] = r
        lane[MLA_NOPE + half + r] = ROPE_SHIFT + r
    return lane


MLA_LANE_OF_DIM = _mla_lane_of_dim()


def _place_heads(w, heads, lane_of_dim):
    width = len(lane_of_dim)
    order = np.argsort(lane_of_dim)
    zeros = lambda n: jnp.zeros(w.shape[:-1] + (n,), w.dtype)
    pieces = []
    for h in range(heads):
        lane, i = 0, 0
        while i < width:
            j = i
            while (j + 1 < width and order[j + 1] == order[j] + 1
                   and lane_of_dim[order[j + 1]] == lane_of_dim[order[j]] + 1):
                j += 1
            dst = int(lane_of_dim[order[i]])
            if dst > lane:
                pieces.append(zeros(dst - lane))
            pieces.append(w[..., h * width + int(order[i]):h * width + int(order[j]) + 1])
            lane, i = dst + (j - i + 1), j + 1
        if lane < HEAD_PAD:
            pieces.append(zeros(HEAD_PAD - lane))
    return jnp.concatenate(pieces, axis=-1)


def _place_heads_bf16(w, heads, lane_of_dim):
    width = len(lane_of_dim)
    place = np.zeros((heads * width, heads * HEAD_PAD), np.float32)
    for h in range(heads):
        place[h * width + np.arange(width), h * HEAD_PAD + lane_of_dim] = 1.0
    return jnp.dot(w.astype(BF16), jnp.asarray(place, BF16), preferred_element_type=BF16)


def _rope_tables(n_tok):
    rows = n_tok // GRID_W
    r = jnp.repeat(jnp.arange(rows, dtype=F32), GRID_W)
    col = jnp.tile(jnp.arange(GRID_W, dtype=F32), rows)
    n_freq = MLA_ROPE // 4
    inv = ROPE_THETA ** (-jnp.arange(n_freq, dtype=F32) / n_freq)
    ang = jnp.concatenate([r[:, None] * inv, col[:, None] * inv], axis=-1)
    cos, sin = jnp.cos(ang), jnp.sin(ang)
    ones = jnp.ones((n_tok, MLA_NOPE), F32)
    c = _place_heads(jnp.concatenate([ones, cos, cos], axis=1), 1, MLA_LANE_OF_DIM)
    s = _place_heads(jnp.concatenate([0.0 * ones, -sin, sin], axis=1), 1, MLA_LANE_OF_DIM)
    return jnp.stack([c, s])


def kernel(x_prompt, x_sample, c, c_ctx, cache_mla_ckv, cache_mla_krope, state_gla, state_s5,
           norm_w, w_ada, b_ada, w_in, gla_w_a2, gla_b_a, gla_o_norm,
           mla_q_norm, mla_w_uq, mla_kv_norm, mla_w_uk, mla_w_uv, mla_qh_norm, mla_kh_norm,
           s5_a_re, s5_a_im, s5_log_dt, s5_b_re, s5_b_im, s5_c_re, s5_c_im, s5_d, s5_w_glu, s5_b_glu,
           w_bo_gla, w_bo_mla, w_bo_s5, w_out):
    bsz, seq, _ = x_prompt.shape
    dbsz, dseq, _ = x_sample.shape
    ctx_row = COND_ROWS - 1
    assert dbsz <= ctx_row and (bsz * seq) % ROW_TILE == 0 and dseq % ROW_TILE == 0

    cond = jnp.zeros((COND_ROWS, D_MODEL), F32).at[0:dbsz].set(c).at[ctx_row].set(c_ctx)
    ada = _ada(cond, w_ada, b_ada)

    vec = lambda a: a.reshape(2 * DEPTH, S5_NSTATE)
    ldt = jnp.repeat(s5_log_dt[..., None], S5_STATE, axis=-1)
    rows_gp = lambda t: t.reshape(DEPTH, S5_TILES, LANE, S5_STATE)
    bt = lambda b: rows_gp(b.transpose(0, 1, 3, 2))
    wmat, tab8, tab1 = _s5_prep(vec(s5_a_re), vec(s5_a_im), vec(ldt), bt(s5_b_re), bt(s5_b_im),
                                rows_gp(s5_c_re), rows_gp(s5_c_im), s5_d.reshape(DEPTH, 1, S5_WIDTH))
    wglu = s5_w_glu.astype(BF16)

    wuq = _place_heads_bf16(mla_w_uq, MLA_HEADS, MLA_LANE_OF_DIM)
    wuk = _place_heads_bf16(mla_w_uk, MLA_HEADS, MLA_LANE_OF_DIM[:MLA_NOPE])
    wuv = mla_w_uv.astype(BF16)
    qhn = _place_heads(mla_qh_norm, 1, MLA_LANE_OF_DIM)
    khn = _place_heads(mla_kh_norm, 1, MLA_LANE_OF_DIM)
    e_np = np.zeros((LANE, MLA_HEADS * HEAD_PAD), np.float32)
    for h in range(MLA_HEADS):
        for i in range(MLA_ROPE):
            e_np[i, h * HEAD_PAD + MLA_LANE_OF_DIM[MLA_NOPE + i]] = 1.0
    e_place = jnp.asarray(e_np, BF16)
    rope_tab = _rope_tables(dseq)
    ckr_pad = jnp.pad(cache_mla_krope, ((0, 0), (0, 0), (0, 0), (0, LANE - MLA_ROPE)))

    zrow = lambda n: jnp.zeros((DEPTH, n, GLA_QK), F32)
    waf = jnp.concatenate([gla_w_a2[:, 0], zrow(LANE - GLA_RANK)], axis=1).astype(BF16)
    wab = jnp.concatenate([zrow(GLA_RANK), gla_w_a2[:, 1], zrow(LANE - 2 * GLA_RANK)], axis=1).astype(BF16)
    sgla = state_gla.reshape(dbsz, DEPTH, 2, GLA_QK, GLA_DV)
    ss5 = state_s5.reshape(dbsz, DEPTH, 2, 2, S5_NSTATE)
    zero_s5 = jnp.zeros((bsz, 2, 2, S5_NSTATE), F32)

    hp = x_prompt.reshape(bsz * seq, D_MODEL)
    hs = x_sample.reshape(dbsz * dseq, D_MODEL)
    ckv_l, krope_l, gla_l, s5_l = [], [], [], []
    w_in_t = jnp.swapaxes(w_in, 1, 2)
    mod = ada.reshape(DEPTH, COND_ROWS, 1, 3 * D_MODEL)
    mla_w = (mla_q_norm, wuq, mla_kv_norm, wuk, wuv, qhn, khn, e_place)
    wbo = (w_bo_gla.astype(BF16), w_bo_mla.astype(BF16), w_bo_s5.astype(BF16))
    wout = w_out.astype(BF16)
    for l in range(DEPTH):
        p_rows, p_blocks, blocks_per_seq = bsz * seq, bsz * seq // ROW_TILE, dseq // ROW_TILE
        mod_idx = lambda i: jnp.where(i < p_blocks, ctx_row, (i - p_blocks) // blocks_per_seq)
        zg, zm, zs = _in_proj(hp, hs, mod, mod_idx, norm_w, w_in_t, l)

        def mixers(x2, row0, nb, n, ctx):
            if ctx:
                gctx = sgla
                x0, x0_blk = ss5, (nb, None, 2, 2, S5_TILE_STATE)
                x0_idx = lambda j: (0, l, 0, 0, j)
                mctx, rt = (cache_mla_ckv, ckr_pad), rope_tab
            else:
                gctx = None
                x0, x0_blk = zero_s5, (nb, 2, 2, S5_TILE_STATE)
                x0_idx = lambda j: (0, 0, 0, j)
                mctx, rt = None, None
            oa, st_gla = _gla(zg, row0, gctx, l, waf, wab, gla_b_a, gla_o_norm, nb, n)
            ob, ckv = _mla(zm, row0, mctx, l, mla_w, rt, nb, n)
            y_ssm, st_s5 = _s5_scan(zs, row0, x0, x0_blk, x0_idx, l, wmat, tab8, tab1, nb, n)
            grp_mod_idx = lambda i: mod_idx(i + row0 // ROW_TILE)
            y = _merge(x2, row0, mod, grp_mod_idx, norm_w, oa, ob, y_ssm, zs, l, wglu, s5_b_glu, w_in_t, *wbo, wout)
            return y, ckv, st_gla, st_s5

        hp_next, ckv_p, st_gla_p, st_s5_p = mixers(hp, 0, bsz, seq, False)
        hs = mixers(hs, p_rows, dbsz, dseq, True)[0]
        hp = hp_next
        ckv_l.append(ckv_p.reshape(bsz, seq, MLA_KV_LORA))
        krope_l.append(zm[:p_rows, ZM_KR:ZM_KR + MLA_ROPE].astype(F32).reshape(bsz, seq, MLA_ROPE))
        gla_l.append(st_gla_p.reshape(bsz, 2, GLA_HEADS, GLA_DK, GLA_DV))
        s5_l.append(st_s5_p.reshape(bsz, 2, 2, S5_GROUPS, S5_STATE))

    return (hp.reshape(bsz, seq, D_MODEL), hs.reshape(dbsz, dseq, D_MODEL),
            jnp.stack(ckv_l, axis=1), jnp.stack(krope_l, axis=1),
            jnp.stack(gla_l, axis=1), jnp.stack(s5_l, axis=1))
```

```python
import functools

import jax
import jax.numpy as jnp
import numpy as np
from jax import lax
from jax.experimental import pallas as pl
from jax.experimental.pallas import tpu as pltpu

F32 = jnp.float32
BF16 = jnp.bfloat16

EPS = 1e-6
D_MODEL = 1024
DEPTH = 2
GRID_W = 64
ROPE_THETA = 10000.0
GLA_HEADS = 4
GLA_DK = 64
GLA_DV = 128
GLA_RANK = 16
GLA_GATE_NORM = 16.0
GLA_QK = GLA_HEADS * GLA_DK
GLA_WIDTH = GLA_HEADS * GLA_DV
MLA_HEADS = 4
MLA_Q_LORA = 384
MLA_KV_LORA = 256
MLA_NOPE = 64
MLA_ROPE = 32
MLA_QK = MLA_NOPE + MLA_ROPE
MLA_DV = 128
MLA_WIDTH = MLA_HEADS * MLA_DV
S5_WIDTH = 512
S5_GROUP = 16
S5_GROUPS = 32
S5_STATE = 64
S5_NSTATE = S5_GROUPS * S5_STATE

LANE = 128
SUBLANE = 8
COND_ROWS = SUBLANE
HEAD_PAD = LANE
ROPE_SHIFT = LANE // 2
CHUNK = 64
GLA_STEP = 256
GLA_SEQS_PER_STEP = 2
GLA_ROWS_PER_STEP = 1024
GLA_SKEW = 1
S5_TILES = S5_WIDTH // LANE
S5_TILE_STATE = S5_NSTATE // S5_TILES
ROW_TILE = 512
Q_TILE = 512
PROJ_TILE = 256
MLA_ROWS_PER_STEP = 1024
VMEM_LIMIT = 56 * 1024 * 1024

IN_SPLITS = (GLA_QK, GLA_QK, GLA_WIDTH, GLA_RANK, GLA_RANK, GLA_WIDTH,
             MLA_Q_LORA, MLA_KV_LORA, MLA_ROPE, MLA_WIDTH,
             S5_WIDTH, S5_WIDTH, 3 * D_MODEL)
(IN_GQ, IN_GK, IN_GV, IN_GA, IN_GAB, IN_GG, IN_MQ, IN_MKV, IN_MKR, IN_MG, IN_SU, IN_SG,
 MERGE_COL, D_IN) = (int(c) for c in np.cumsum((0,) + IN_SPLITS))
ZG_Q, ZG_K, ZG_V, ZG_A = 0, GLA_QK, 2 * GLA_QK, 2 * GLA_QK + GLA_WIDTH
ZG_GATE = ZG_A + LANE
ZG_W = ZG_GATE + GLA_WIDTH
ZM_Q, ZM_KV, ZM_KR = 0, MLA_Q_LORA, MLA_Q_LORA + MLA_KV_LORA
ZM_GATE = ZM_KR + LANE
ZM_W = ZM_GATE + MLA_WIDTH
ZS_U, ZS_GATE, ZS_W = 0, S5_WIDTH, 2 * S5_WIDTH
ZG_BASE, ZM_BASE, ZS_BASE, PACK_W = 0, ZG_W, ZG_W + ZM_W, ZG_W + ZM_W + ZS_W
IN_PIECES = (
    (IN_GQ, (IN_GA - IN_GQ) // LANE, ZG_BASE + ZG_Q, None),
    (IN_GA, 1, ZG_BASE + ZG_A, 2 * GLA_RANK),
    (IN_GG, GLA_WIDTH // LANE, ZG_BASE + ZG_GATE, None),
    (IN_MQ, (IN_MKR - IN_MQ) // LANE, ZM_BASE + ZM_Q, None),
    (IN_MKR, 1, ZM_BASE + ZM_KR, MLA_ROPE),
    (IN_MG, MLA_WIDTH // LANE, ZM_BASE + ZM_GATE, None),
    (IN_SU, (MERGE_COL - IN_SU) // LANE, ZS_BASE + ZS_U, None),
)


def _dot(a, b):
    return jnp.dot(a.astype(BF16), b.astype(BF16), preferred_element_type=F32)


def _dot_nt(a, b):
    return lax.dot_general(a.astype(BF16), b.astype(BF16), (((1,), (1,)), ((), ())),
                           preferred_element_type=F32)


def _split_bf16(x, parts):
    out = []
    r = x
    for _ in range(parts):
        p = r.astype(BF16)
        out.append(p)
        r = r - p.astype(F32)
    return out


def _emit_skewed(chains, skew):
    pending, active, tick = list(chains), [], 0
    while pending or active:
        while pending and (skew == 0 or tick % skew == 0):
            active.append(pending.pop(0))
            if skew:
                break
        for gen in list(active):
            if next(gen, "done") == "done":
                active.remove(gen)
        tick += 1


def _vector_spec(width):
    return pl.BlockSpec((DEPTH, width), lambda *_: (0, 0))


def _layer_spec(shape, layer, pipeline_mode=None):
    kwargs = {} if pipeline_mode is None else {"pipeline_mode": pipeline_mode}
    return pl.BlockSpec((None,) + tuple(shape), lambda *_: (layer,) + (0,) * len(shape), **kwargs)


def _params(sem):
    return pltpu.CompilerParams(dimension_semantics=sem, vmem_limit_bytes=VMEM_LIMIT)


def _ada_kernel(c_ref, w_ref, b_ref, o_ref):
    s = jax.nn.silu(c_ref[...])
    o_ref[...] = _dot(s, w_ref[...]) + b_ref[...]


def _ada(cond8, w_ada, b_ada):
    tn = 1024
    return pl.pallas_call(
        _ada_kernel,
        grid=(DEPTH, 3 * D_MODEL // tn),
        in_specs=[
            pl.BlockSpec((COND_ROWS, D_MODEL), lambda l, n: (0, 0)),
            pl.BlockSpec((None, D_MODEL, tn), lambda l, n: (l, 0, n)),
            pl.BlockSpec((None, 1, tn), lambda l, n: (l, 0, n)),
        ],
        out_specs=pl.BlockSpec((None, COND_ROWS, tn), lambda l, n: (l, 0, n)),
        out_shape=jax.ShapeDtypeStruct((DEPTH, COND_ROWS, 3 * D_MODEL), F32),
        compiler_params=_params(("parallel", "parallel")),
        name="ada",
    )(cond8, w_ada, b_ada.reshape(DEPTH, 1, 3 * D_MODEL))


def _mod_rmsnorm(x, nw, mod):
    ms = jnp.mean(x * x, axis=-1, keepdims=True)
    y = x * lax.rsqrt(ms + EPS) * nw
    return y * (1.0 + mod[:, D_MODEL:2 * D_MODEL]) + mod[:, 0:D_MODEL]


def _pack_transposed(w_ref, wb_s, src, tiles, dst, keep):
    lane = lax.broadcasted_iota(jnp.int32, (D_MODEL, LANE), 1)
    for t in range(tiles):
        blk = w_ref[src + t * LANE:src + (t + 1) * LANE, :].T
        if keep is not None:
            blk = jnp.where(lane < keep, blk, 0.0)
        wb_s[:, dst + t * LANE:dst + (t + 1) * LANE] = blk.astype(BF16)


def _in_proj_kernel(xp_ref, xs_ref, mod_ref, nw_ref, w_ref, zg_ref, zm_ref, zs_ref, wb_s, *, p_blocks, layer):
    i = pl.program_id(0)

    @pl.when(i == 0)
    def _():
        for src, tiles, dst, keep in IN_PIECES:
            _pack_transposed(w_ref, wb_s, src, tiles, dst, keep)

    x = jnp.where(i < p_blocks, xp_ref[...], xs_ref[...])
    h = _mod_rmsnorm(x, nw_ref[layer:layer + 1, :], mod_ref[0]).astype(BF16)
    z = jnp.dot(h, wb_s[...], preferred_element_type=F32)
    zg_ref[...] = z[:, ZG_BASE:ZG_BASE + ZG_W].astype(BF16)
    zm_ref[...] = z[:, ZM_BASE:ZM_BASE + ZM_W].astype(BF16)
    zs_ref[...] = z[:, ZS_BASE:ZS_BASE + ZS_W].astype(BF16)


def _two_group_rows(width, p_blocks):
    tm = ROW_TILE
    return (pl.BlockSpec((tm, width), lambda i: (jnp.minimum(i, p_blocks - 1), 0)),
            pl.BlockSpec((tm, width), lambda i: (jnp.maximum(i - p_blocks, 0), 0)))


def _in_proj(xp, xs, mod, mod_idx, nw, w_in_t, layer):
    tm = ROW_TILE
    p_blocks = xp.shape[0] // tm
    n = xp.shape[0] + xs.shape[0]
    return pl.pallas_call(
        functools.partial(_in_proj_kernel, p_blocks=p_blocks, layer=layer),
        grid=(n // tm,),
        in_specs=[
            *_two_group_rows(D_MODEL, p_blocks),
            pl.BlockSpec((None, 1, 1, 3 * D_MODEL), lambda i: (layer, mod_idx(i), 0, 0)),
            _vector_spec(D_MODEL),
            pl.BlockSpec((None, MERGE_COL, D_MODEL), lambda i: (layer, 0, 0), pipeline_mode=pl.Buffered(1)),
        ],
        out_specs=[
            pl.BlockSpec((tm, ZG_W), lambda i: (i, 0)),
            pl.BlockSpec((tm, ZM_W), lambda i: (i, 0)),
            pl.BlockSpec((tm, ZS_W), lambda i: (i, 0)),
        ],
        out_shape=[
            jax.ShapeDtypeStruct((n, ZG_W), BF16),
            jax.ShapeDtypeStruct((n, ZM_W), BF16),
            jax.ShapeDtypeStruct((n, ZS_W), BF16),
        ],
        scratch_shapes=[pltpu.VMEM((D_MODEL, PACK_W), BF16)],
        compiler_params=_params(("arbitrary",)),
        name="in_proj",
    )(xp, xs, mod, nw, w_in_t)


def _gla_kernel(*refs, nsteps, seq, nseq, has_ctx, layer):
    it = iter(refs)
    zg_ref = next(it)
    s0_ref = next(it) if has_ctx else None
    waf_ref, wab_ref, ba_ref, onorm_ref, o_ref, sfin_ref, la_s, o_s, st_s = (next(it) for _ in range(9))
    chains = [(g, d) for g in range(nseq) for d in (0, 1)]
    inv_norm = 1.0 / GLA_GATE_NORM
    zero_blk = jnp.zeros((GLA_DK, GLA_DV), F32)
    for ch, (g, d) in enumerate(chains):
        if d == 0:
            a_blk = zg_ref[g * seq:(g + 1) * seq, ZG_A:ZG_A + LANE]
        wa_ref = waf_ref if d == 0 else wab_ref
        a_low = _dot(a_blk, wa_ref[...]) + ba_ref[d:d + 1, :]
        la_s[ch] = (jnp.minimum(a_low, 0.0) - jnp.log(1.0 + jnp.exp(-jnp.abs(a_low)))) * inv_norm
        if has_ctx:
            s0 = s0_ref[g, d]
            rows_bd = []
            for h in range(GLA_HEADS):
                sh = s0[h * GLA_DK:(h + 1) * GLA_DK, :]
                rows_bd.append(jnp.concatenate([sh if h2 == h else zero_blk for h2 in range(GLA_HEADS)], axis=1))
            st_s[ch] = jnp.concatenate(rows_bd, axis=0).T
        else:
            st_s[ch] = jnp.zeros((GLA_WIDTH, GLA_QK), F32)

    def iota(shape, axis, shift):
        return lax.shift_right_logical(lax.broadcasted_iota(jnp.int32, shape, axis), shift)

    log_chunk, log_dv = CHUNK.bit_length() - 1, GLA_DV.bit_length() - 1
    row = lax.broadcasted_iota(jnp.int32, (GLA_STEP, GLA_STEP), 0)
    col = lax.broadcasted_iota(jnp.int32, (GLA_STEP, GLA_STEP), 1)
    same_chunk = iota((GLA_STEP, GLA_STEP), 0, log_chunk) == iota((GLA_STEP, GLA_STEP), 1, log_chunk)
    masks = (same_chunk & (row >= col), same_chunk & (row <= col))
    lane_head = iota((GLA_STEP, GLA_QK), 1, log_chunk)
    row_chunk = iota((GLA_STEP, GLA_QK), 0, log_chunk)
    state_blk = iota((GLA_WIDTH, GLA_QK), 0, log_dv) == iota((GLA_WIDTH, GLA_QK), 1, log_chunk)
    qscale = GLA_DK ** -0.5
    nch = GLA_STEP // CHUNK

    def chain_phases(i, ch, g, d):
        r0 = pl.multiple_of((i if d == 0 else nsteps - 1 - i) * GLA_STEP, GLA_STEP)
        zrows, rows = pl.ds(g * seq + r0, GLA_STEP), pl.ds(r0, GLA_STEP)
        a_hi, a_lo = _split_bf16(la_s[ch, rows, :], 2)
        tri = masks[d].astype(BF16)
        cum = (jnp.dot(tri, a_hi, preferred_element_type=F32)
               + jnp.dot(tri, a_lo, preferred_element_type=F32))
        yield
        edge = CHUNK - 1 if d == 0 else 0
        blast = [cum[c * CHUNK + edge:c * CHUNK + edge + 1, :] for c in range(nch)]
        bl = jnp.concatenate([jnp.broadcast_to(b, (CHUNK, GLA_QK)) for b in blast], axis=0)
        q = zg_ref[zrows, ZG_Q:ZG_Q + GLA_QK].astype(F32) * qscale
        k = zg_ref[zrows, ZG_K:ZG_K + GLA_QK].astype(F32)
        v = zg_ref[zrows, ZG_V:ZG_V + GLA_WIDTH]
        v_t = v.astype(F32).T.astype(BF16)
        qd = q * jnp.exp(cum)
        kd = (k * jnp.exp(-cum)).astype(BF16)
        kr = k * jnp.exp(bl - cum)
        yield
        outs = []
        for h in range(GLA_HEADS):
            qh = jnp.where(lane_head == h, qd, 0.0)
            att = _dot_nt(qh, kd)
            yield
            att = jnp.where(masks[d], att, 0.0)
            outs.append(_dot(att, v[:, h * GLA_DV:(h + 1) * GLA_DV]))
            yield
        s = st_s[ch]
        inter = [None] * nch
        for c in (range(nch) if d == 0 else reversed(range(nch))):
            inter[c] = _dot_nt(qd[c * CHUNK:(c + 1) * CHUNK, :], s)
            kv_t = _dot(v_t, jnp.where(row_chunk == c, kr, 0.0))
            yield
            s = s * jnp.exp(blast[c]) + jnp.where(state_blk, kv_t, 0.0)
            yield
        st_s[ch] = s
        o_s[ch, rows, :] = jnp.concatenate(outs, axis=1) + jnp.concatenate(inter, axis=0)

    def step(i, carry):
        _emit_skewed([chain_phases(i, ch, g, d) for ch, (g, d) in enumerate(chains)], GLA_SKEW)
        return carry

    lax.fori_loop(0, nsteps, step, 0)
    onorm = onorm_ref[layer:layer + 1, :]
    for ch, (g, d) in enumerate(chains):
        s_fin = st_s[ch].T
        for h in range(GLA_HEADS):
            sfin_ref[g, d, h * GLA_DK:(h + 1) * GLA_DK, :] = (
                s_fin[h * GLA_DK:(h + 1) * GLA_DK, h * GLA_DV:(h + 1) * GLA_DV])
    for g in range(nseq):
        srows = slice(g * seq, (g + 1) * seq)
        o = o_s[2 * g] + o_s[2 * g + 1]
        gate = zg_ref[srows, ZG_GATE:ZG_GATE + GLA_WIDTH].astype(F32)
        for h in range(GLA_HEADS):
            vs = slice(h * GLA_DV, (h + 1) * GLA_DV)
            oh = o[:, vs]
            ms = jnp.mean(oh * oh, axis=-1, keepdims=True)
            o_ref[srows, vs] = oh * lax.rsqrt(ms + EPS) * onorm * jax.nn.silu(gate[:, vs])


def _gla(zg, row0, ctx, layer, waf, wab, ba, onorm, bsz, seq):
    nseq = max(GLA_SEQS_PER_STEP, GLA_ROWS_PER_STEP // seq)
    blk0 = row0 // (nseq * seq)
    in_specs = [pl.BlockSpec((nseq * seq, ZG_W), lambda b: (b + blk0, 0))]
    args = [zg]
    if ctx is not None:
        in_specs.append(pl.BlockSpec((nseq, None, 2, GLA_QK, GLA_DV), lambda b: (b, layer, 0, 0, 0)))
        args.append(ctx)
    in_specs += [
        _layer_spec((LANE, GLA_QK), layer),
        _layer_spec((LANE, GLA_QK), layer),
        _layer_spec((2, GLA_QK), layer),
        _vector_spec(GLA_DV),
    ]
    return pl.pallas_call(
        functools.partial(_gla_kernel, nsteps=seq // GLA_STEP, seq=seq, nseq=nseq, has_ctx=ctx is not None,
                          layer=layer),
        grid=(bsz // nseq,),
        in_specs=in_specs,
        out_specs=[
            pl.BlockSpec((nseq * seq, GLA_WIDTH), lambda b: (b, 0)),
            pl.BlockSpec((nseq, 2, GLA_QK, GLA_DV), lambda b: (b, 0, 0, 0)),
        ],
        out_shape=[
            jax.ShapeDtypeStruct((bsz * seq, GLA_WIDTH), F32),
            jax.ShapeDtypeStruct((bsz, 2, GLA_QK, GLA_DV), F32),
        ],
        scratch_shapes=[
            pltpu.VMEM((2 * nseq, seq, GLA_QK), F32),
            pltpu.VMEM((2 * nseq, seq, GLA_WIDTH), F32),
            pltpu.VMEM((2 * nseq, GLA_WIDTH, GLA_QK), F32),
        ],
        compiler_params=_params(("parallel",)),
        name="gla",
    )(*args, waf, wab, ba, onorm)


def _rms(x, w):
    ms = jnp.mean(x * x, axis=-1, keepdims=True)
    return x * lax.rsqrt(ms + EPS) * w


def _head_sums_mxu(x):
    width = x.shape[-1]
    shift = HEAD_PAD.bit_length() - 1
    gi = lax.shift_right_logical(lax.broadcasted_iota(jnp.int32, (width, width), 0), shift)
    gj = lax.shift_right_logical(lax.broadcasted_iota(jnp.int32, (width, width), 1), shift)
    return _dot(x * x, jnp.where(gi == gj, 1.0, 0.0))


def _head_norm(x, w, rope, on_mxu):
    sums = _head_sums_mxu(x) if on_mxu else None
    outs = []
    for h in range(MLA_HEADS):
        hs = slice(h * HEAD_PAD, (h + 1) * HEAD_PAD)
        xh = x[:, hs]
        ss = sums[:, hs] if on_mxu else jnp.sum(xh * xh, axis=-1, keepdims=True)
        yh = xh * lax.rsqrt(ss * (1.0 / MLA_QK) + EPS) * w
        if rope is not None:
            c, s = rope
            yh = yh * c + pltpu.roll(yh, ROPE_SHIFT, 1) * s
        outs.append(yh)
    return outs


def _place_rope_key(kr, e):
    return sum(jnp.dot(p, e, preferred_element_type=F32) for p in _split_bf16(kr, 3))


def _mla_kernel(*refs, seq, nseq, n_ctx, use_rope, layer):
    it = iter(refs)
    zm_ref = next(it)
    if n_ctx:
        cckv_ref, ckr_ref = next(it), next(it)
    qn_ref, wuq_ref, kvn_ref, wuk_ref, wuv_ref, qhn_ref, khn_ref, e_ref = (next(it) for _ in range(8))
    rope_ref = next(it) if use_rope else None
    o_ref, ckv_ref = next(it), next(it)
    q_s, k_s, v_s = next(it), next(it), next(it)
    qn, kvn, qhn, khn = (r[layer:layer + 1, :] for r in (qn_ref, kvn_ref, qhn_ref, khn_ref))

    qscale = MLA_QK ** -0.5
    heads = [slice(h * HEAD_PAD, (h + 1) * HEAD_PAD) for h in range(MLA_HEADS)]

    def keys_values(g, ckv, k_rope_placed, rope, k_rows):
        k_raw = _dot(ckv, wuk_ref[...]) + k_rope_placed
        yield
        kh = _head_norm(k_raw, khn, rope, False)
        for h, hs in enumerate(heads):
            k_s[g, k_rows, hs] = kh[h].astype(BF16)
        yield
        v_s[g, k_rows, :] = _dot(ckv, wuv_ref[...]).astype(BF16)
        yield

    def latent_phases(i, g):
        r0 = pl.multiple_of(i * PROJ_TILE, PROJ_TILE)
        tile, rows = pl.ds(r0, PROJ_TILE), pl.ds(g * seq + r0, PROJ_TILE)
        rope = (rope_ref[0, tile, :], rope_ref[1, tile, :]) if use_rope else None
        ckv = _rms(zm_ref[rows, ZM_KV:ZM_KV + MLA_KV_LORA].astype(F32), kvn)
        ckv_ref[rows, :] = ckv
        k_pe = jnp.dot(zm_ref[rows, ZM_KR:ZM_KR + LANE], e_ref[...], preferred_element_type=F32)
        yield from keys_values(g, ckv, k_pe, rope, pl.ds(n_ctx + r0, PROJ_TILE))
        cq = _rms(zm_ref[rows, ZM_Q:ZM_Q + MLA_Q_LORA].astype(F32), qn)
        q_raw = _dot(cq, wuq_ref[...])
        yield
        qh = _head_norm(q_raw, qhn, rope, True)
        for h, hs in enumerate(heads):
            q_s[rows, hs] = (qh[h] * qscale).astype(BF16)

    def latent_tile(i, carry):
        _emit_skewed([latent_phases(i, g) for g in range(nseq)], 0)
        return carry

    lax.fori_loop(0, seq // PROJ_TILE, latent_tile, 0)

    def context_tile(i, carry):
        rows = pl.ds(pl.multiple_of(i * PROJ_TILE, PROJ_TILE), PROJ_TILE)
        _emit_skewed([keys_values(g, cckv_ref[g, rows, :], _place_rope_key(ckr_ref[g, rows, :], e_ref[...]),
                                  None, rows) for g in range(nseq)], 0)
        return carry

    if n_ctx:
        lax.fori_loop(0, n_ctx // PROJ_TILE, context_tile, 0)

    q_tile = min(seq, Q_TILE)

    def head_phases(g, h, hs, rows, gate):
        s = lax.dot_general(q_s[rows, hs], k_s[g, :, hs], (((1,), (1,)), ((), ())),
                            preferred_element_type=F32)
        yield
        e = jnp.exp(s - jnp.max(s, axis=-1, keepdims=True))
        l = jnp.sum(e, axis=-1, keepdims=True)
        p = e.astype(BF16)
        yield
        o = jnp.dot(p, v_s[g, :, hs], preferred_element_type=F32) / l
        o_ref[rows, hs] = o * jax.nn.silu(gate[:, hs])

    def q_block(i, carry):
        chains = []
        for g in range(nseq):
            rows = pl.ds(g * seq + pl.multiple_of(i * q_tile, q_tile), q_tile)
            gate = zm_ref[rows, ZM_GATE:ZM_GATE + MLA_WIDTH].astype(F32)
            chains += [head_phases(g, h, hs, rows, gate) for h, hs in enumerate(heads)]
        _emit_skewed(chains, 0)
        return carry

    lax.fori_loop(0, seq // q_tile, q_block, 0)


def _mla(zm, row0, ctx, layer, w, rope_tab, bsz, seq):
    n_ctx = 0 if ctx is None else ctx[0].shape[-2]
    nseq = max(1, MLA_ROWS_PER_STEP // seq)
    blk0 = row0 // (nseq * seq)
    in_specs = [pl.BlockSpec((nseq * seq, ZM_W), lambda b: (b + blk0, 0))]
    args = [zm]
    if ctx is not None:
        cckv, ckr = ctx
        in_specs += [
            pl.BlockSpec((nseq, None, n_ctx, MLA_KV_LORA), lambda b: (b, layer, 0, 0)),
            pl.BlockSpec((nseq, None, n_ctx, LANE), lambda b: (b, layer, 0, 0)),
        ]
        args += [cckv, ckr]
    in_specs += [
        _vector_spec(MLA_Q_LORA),
        _layer_spec((MLA_Q_LORA, MLA_HEADS * HEAD_PAD), layer),
        _vector_spec(MLA_KV_LORA),
        _layer_spec((MLA_KV_LORA, MLA_HEADS * HEAD_PAD), layer),
        _layer_spec((MLA_KV_LORA, MLA_WIDTH), layer),
        _vector_spec(HEAD_PAD),
        _vector_spec(HEAD_PAD),
        pl.BlockSpec((LANE, MLA_HEADS * HEAD_PAD), lambda b: (0, 0)),
    ]
    args += list(w)
    if rope_tab is not None:
        in_specs.append(pl.BlockSpec((2, seq, HEAD_PAD), lambda b: (0, 0, 0)))
        args.append(rope_tab)
    return pl.pallas_call(
        functools.partial(_mla_kernel, seq=seq, nseq=nseq, n_ctx=n_ctx, use_rope=rope_tab is not None,
                          layer=layer),
        grid=(bsz // nseq,),
        in_specs=in_specs,
        out_specs=[
            pl.BlockSpec((nseq * seq, MLA_WIDTH), lambda b: (b, 0)),
            pl.BlockSpec((nseq * seq, MLA_KV_LORA), lambda b: (b, 0)),
        ],
        out_shape=[
            jax.ShapeDtypeStruct((bsz * seq, MLA_WIDTH), F32),
            jax.ShapeDtypeStruct((bsz * seq, MLA_KV_LORA), F32),
        ],
        scratch_shapes=[
            pltpu.VMEM((nseq * seq, MLA_HEADS * HEAD_PAD), BF16),
            pltpu.VMEM((nseq, n_ctx + seq, MLA_HEADS * HEAD_PAD), BF16),
            pltpu.VMEM((nseq, n_ctx + seq, MLA_WIDTH), BF16),
        ],
        compiler_params=_params(("parallel",)),
        name="mla",
    )(*args)


S5_T = 8
S5_R = CHUNK // S5_T
S5_SUB_CH = 64
S5_SUBS = LANE // S5_SUB_CH
S5_SUB_STATE = S5_TILE_STATE // S5_SUBS
S5_ROW = S5_T * S5_SUB_CH
S5_W = 2 * S5_SUB_STATE
W_M, W_SF, W_SB, W_CF, W_CB = range(5)


def _cmul(ar, ai, br, bi):
    return ar * br - ai * bi, ar * bi + ai * br


def _s5_prep_kernel(are_ref, aim_ref, ldt_ref, bre_ref, bim_ref, cre_ref, cim_ref, d_ref,
                    w_ref, tab8_ref, tab1_ref):
    gr = lax.shift_right_logical(lax.broadcasted_iota(jnp.int32, (S5_SUB_CH, S5_SUB_STATE), 0),
                                 S5_GROUP.bit_length() - 1)
    gc = lax.shift_right_logical(lax.broadcasted_iota(jnp.int32, (S5_SUB_CH, S5_SUB_STATE), 1),
                                 S5_STATE.bit_length() - 1)

    def spread(ref, h):
        x = ref[h * S5_SUB_CH:(h + 1) * S5_SUB_CH, :]
        return jnp.where(gr == gc, jnp.concatenate([x] * (S5_SUB_CH // S5_GROUP), axis=1), 0.0)

    row = lax.broadcasted_iota(jnp.int32, (S5_SUB_CH, S5_SUB_CH), 0)
    col = lax.broadcasted_iota(jnp.int32, (S5_SUB_CH, S5_SUB_CH), 1)
    taps = [[[], []] for _ in range(S5_SUBS)]
    for d in (0, 1):
        prow = pl.ds(2 * pl.program_id(0) + d, 1)
        a_re, a_im = are_ref[prow, :], aim_ref[prow, :]
        dt = jnp.exp(ldt_ref[prow, :])
        lam = a_re * dt
        th = a_im * dt
        mag = jnp.exp(lam)
        ab_re = mag * jnp.cos(th)
        ab_im = mag * jnp.sin(th)
        den = a_re * a_re + a_im * a_im
        n_re = ab_re - 1.0
        cf_re = (n_re * a_re + ab_im * a_im) / den
        cf_im = (ab_im * a_re - n_re * a_im) / den
        k = lax.broadcasted_iota(jnp.int32, (2 * S5_T, S5_TILE_STATE), 0).astype(F32)
        pmag = jnp.exp(k * lam)
        pw_re = pmag * jnp.cos(k * th)
        pw_im = pmag * jnp.sin(k * th)
        for h in range(S5_SUBS):
            ss = slice(h * S5_SUB_STATE, (h + 1) * S5_SUB_STATE)
            c_re, c_im = spread(cre_ref, h), spread(cim_ref, h)
            c_cat = jnp.concatenate([c_re, c_im], axis=1).astype(BF16)
            bp_re, bp_im = _cmul(spread(bre_ref, h), spread(bim_ref, h), cf_re[:, ss], cf_im[:, ss])
            for p in range(S5_T + 1):
                ar, ai = pw_re[p:p + 1, ss], pw_im[p:p + 1, ss]
                t_in = S5_T - 1 - p if d == 0 else p
                t_out = p - 1 if d == 0 else S5_T - p
                if p < S5_T:
                    l_re, l_im = _cmul(bp_re, bp_im, ar, ai)
                    w_ref[h, W_SF + d, t_in * S5_SUB_CH:(t_in + 1) * S5_SUB_CH, :] = (
                        jnp.concatenate([l_re, l_im], axis=1).astype(BF16))
                    taps[h][d].append(_dot_nt(jnp.concatenate([l_re, -l_im], axis=1), c_cat))
                if p > 0:
                    v_re, v_im = _cmul(c_re, c_im, ar, ai)
                    w_ref[h, W_CF + d, t_out * S5_SUB_CH:(t_out + 1) * S5_SUB_CH, :] = (
                        jnp.concatenate([v_re, -v_im], axis=1).astype(BF16))
        r = lax.broadcasted_iota(jnp.int32, (S5_R, S5_TILE_STATE), 0).astype(F32) * float(S5_T)
        r1 = r + float(S5_T)
        pm = jnp.exp(r * lam)
        qm = jnp.exp(-(r1 * lam))
        tab8_ref[d, 0] = pm * jnp.cos(r * th)
        tab8_ref[d, 1] = pm * jnp.sin(r * th)
        tab8_ref[d, 2] = qm * jnp.cos(r1 * th)
        tab8_ref[d, 3] = -(qm * jnp.sin(r1 * th))
        mc = jnp.exp(float(CHUNK) * lam)
        tab1_ref[d, 0:1, :] = mc * jnp.cos(float(CHUNK) * th)
        tab1_ref[d, 1:2, :] = mc * jnp.sin(float(CHUNK) * th)
    for h in range(S5_SUBS):
        skip = jnp.where(row == col, d_ref[:, h * S5_SUB_CH:(h + 1) * S5_SUB_CH], 0.0)
        for t in range(S5_T):
            blocks = []
            for t2 in range(S5_T):
                if t < t2:
                    blocks.append(taps[h][0][t2 - t])
                elif t > t2:
                    blocks.append(taps[h][1][t - t2])
                else:
                    blocks.append(taps[h][0][0] + taps[h][1][0] + skip)
            w_ref[h, W_M, t * S5_SUB_CH:(t + 1) * S5_SUB_CH, :] = jnp.concatenate(blocks, axis=1).astype(BF16)


def _s5_prep(a_re, a_im, ldt, b_re, b_im, c_re, c_im, dsk):
    vec = pl.BlockSpec((2 * DEPTH, S5_TILE_STATE), lambda l, j: (0, j))
    blk = pl.BlockSpec((None, None, LANE, S5_STATE), lambda l, j: (l, j, 0, 0))
    return pl.pallas_call(
        _s5_prep_kernel,
        grid=(DEPTH, S5_TILES),
        in_specs=[vec, vec, vec, blk, blk, blk, blk,
                  pl.BlockSpec((None, 1, LANE), lambda l, j: (l, 0, j))],
        out_specs=[
            pl.BlockSpec((None, None, S5_SUBS, 5, S5_ROW, S5_W), lambda l, j: (l, j, 0, 0, 0, 0)),
            pl.BlockSpec((None, None, 2, 4, S5_R, S5_TILE_STATE), lambda l, j: (l, j, 0, 0, 0, 0)),
            pl.BlockSpec((None, None, 2, 2, S5_TILE_STATE), lambda l, j: (l, j, 0, 0, 0)),
        ],
        out_shape=[
            jax.ShapeDtypeStruct((DEPTH, S5_TILES, S5_SUBS, 5, S5_ROW, S5_W), BF16),
            jax.ShapeDtypeStruct((DEPTH, S5_TILES, 2, 4, S5_R, S5_TILE_STATE), F32),
            jax.ShapeDtypeStruct((DEPTH, S5_TILES, 2, 2, S5_TILE_STATE), F32),
        ],
        compiler_params=_params(("parallel", "parallel")),
        name="s5_prep",
    )(a_re, a_im, ldt, b_re, b_im, c_re, c_im, dsk)


def _s5_scan_kernel(u_ref, x0_ref, w_ref, tab8_ref, tab1_ref, y_ref, fs_ref, u_s, *, nseq, nb):
    groups = nseq * nb
    nrow = groups * S5_R
    ts = S5_SUB_STATE
    u_s[...] = u_ref[...].astype(F32)
    tokens = [u_s[pl.ds(t, nrow, stride=S5_T), :] for t in range(S5_T)]
    rowi = lax.broadcasted_iota(jnp.int32, (groups, S5_R, ts), 1)

    def prefix(x):
        for s in (1, 2, 4):
            x = x + jnp.where(rowi >= s, pltpu.roll(x, s, 1), 0.0)
        return x

    def suffix(x):
        for s in (1, 2, 4):
            x = x + jnp.where(rowi < S5_R - s, pltpu.roll(x, S5_R - s, 1), 0.0)
        return x

    y_sub = []
    for h in range(S5_SUBS):
        ch = slice(h * S5_SUB_CH, (h + 1) * S5_SUB_CH)
        ss = slice(h * ts, (h + 1) * ts)
        u8 = jnp.concatenate([tok[:, ch] for tok in tokens], axis=1).astype(BF16)
        ef = jnp.dot(u8, w_ref[h, W_SF], preferred_element_type=F32).reshape(groups, S5_R, S5_W)
        eb = jnp.dot(u8, w_ref[h, W_SB], preferred_element_type=F32).reshape(groups, S5_R, S5_W)

        p_re, p_im, q_re, q_im = (tab8_ref[0, i, :, ss] for i in range(4))
        a_re, a_im = tab1_ref[0, 0:1, ss], tab1_ref[0, 1:2, ss]
        w_re, w_im = _cmul(q_re, q_im, ef[:, :, :ts], ef[:, :, ts:])
        cs_re, cs_im = prefix(w_re), prefix(w_im)
        st_re, st_im = [], []
        for s in range(nseq):
            x_re, x_im = x0_ref[s, 0, 0:1, ss], x0_ref[s, 0, 1:2, ss]
            for b in range(nb):
                g = s * nb + b
                st_re.append(x_re)
                st_im.append(x_im)
                x_re, x_im = _cmul(a_re, a_im, x_re + cs_re[g, S5_R - 1:S5_R, :],
                                   x_im + cs_im[g, S5_R - 1:S5_R, :])
            fs_ref[s, 0, 0:1, ss] = x_re
            fs_ref[s, 0, 1:2, ss] = x_im
        xin_re, xin_im = _cmul(p_re, p_im, cs_re - w_re + jnp.stack(st_re), cs_im - w_im + jnp.stack(st_im))
        xin = jnp.concatenate([xin_re, xin_im], axis=2).reshape(nrow, S5_W)

        p_re, p_im, q_re, q_im = (tab8_ref[1, i, :, ss] for i in range(4))
        a_re, a_im = tab1_ref[1, 0:1, ss], tab1_ref[1, 1:2, ss]
        w_re, w_im = _cmul(p_re, p_im, eb[:, :, :ts], eb[:, :, ts:])
        sf_re, sf_im = suffix(w_re), suffix(w_im)
        z_re, z_im = [None] * groups, [None] * groups
        for s in range(nseq):
            x_re, x_im = x0_ref[s, 1, 0:1, ss], x0_ref[s, 1, 1:2, ss]
            for b in reversed(range(nb)):
                g = s * nb + b
                z_re[g], z_im[g] = _cmul(a_re, a_im, x_re, x_im)
                x_re = sf_re[g, 0:1, :] + z_re[g]
                x_im = sf_im[g, 0:1, :] + z_im[g]
            fs_ref[s, 1, 0:1, ss] = x_re
            fs_ref[s, 1, 1:2, ss] = x_im
        xnx_re, xnx_im = _cmul(q_re, q_im, sf_re - w_re + jnp.stack(z_re), sf_im - w_im + jnp.stack(z_im))
        xnx = jnp.concatenate([xnx_re, xnx_im], axis=2).reshape(nrow, S5_W)

        y_sub.append(jnp.dot(u8, w_ref[h, W_M], preferred_element_type=F32)
                     + _dot_nt(xin, w_ref[h, W_CF]) + _dot_nt(xnx, w_ref[h, W_CB]))
    for t in range(S5_T):
        tc = slice(t * S5_SUB_CH, (t + 1) * S5_SUB_CH)
        y_ref[pl.ds(t, nrow, stride=S5_T), :] = jnp.concatenate([y[:, tc] for y in y_sub], axis=1)


def _s5_scan(zs, row0, x0, x0_block, x0_idx, layer, wmat, tab8, tab1, nseq, seq):
    n = nseq * seq
    rblk = row0 // n
    return pl.pallas_call(
        functools.partial(_s5_scan_kernel, nseq=nseq, nb=seq // CHUNK),
        grid=(S5_TILES,),
        in_specs=[
            pl.BlockSpec((n, LANE), lambda j: (rblk, ZS_U // LANE + j)),
            pl.BlockSpec(x0_block, x0_idx),
            pl.BlockSpec((None, None, S5_SUBS, 5, S5_ROW, S5_W), lambda j: (layer, j, 0, 0, 0, 0)),
            pl.BlockSpec((None, None, 2, 4, S5_R, S5_TILE_STATE), lambda j: (layer, j, 0, 0, 0, 0)),
            pl.BlockSpec((None, None, 2, 2, S5_TILE_STATE), lambda j: (layer, j, 0, 0, 0)),
        ],
        out_specs=[
            pl.BlockSpec((n, LANE), lambda j: (0, j)),
            pl.BlockSpec((nseq, 2, 2, S5_TILE_STATE), lambda j: (0, 0, 0, j)),
        ],
        out_shape=[
            jax.ShapeDtypeStruct((n, S5_WIDTH), F32),
            jax.ShapeDtypeStruct((nseq, 2, 2, S5_NSTATE), F32),
        ],
        scratch_shapes=[pltpu.VMEM((n, LANE), F32)],
        compiler_params=_params(("parallel",)),
        name="s5_scan",
    )(zs, x0, wmat, tab8, tab1)


def _merge_kernel(x_ref, mod_ref, nw_ref, oa_ref, ob_ref, ys_ref, sg_ref, wglu_ref, bglu_ref, wmg_ref,
                  wa_ref, wb_ref, wc_ref, wout_ref, y_ref, wmg_s, *, layer):
    @pl.when(pl.program_id(0) == 0)
    def _():
        _pack_transposed(wmg_ref.at[0], wmg_s, 0, 3 * D_MODEL // LANE, 0, None)

    x = x_ref[...]
    mod = mod_ref[0]
    h = _mod_rmsnorm(x, nw_ref[layer:layer + 1, :], mod).astype(BF16)
    zg = _dot(jax.nn.gelu(ys_ref[...]), wglu_ref[...]) + bglu_ref[layer:layer + 1, :]
    oc = (zg[:, :S5_WIDTH] * jax.nn.sigmoid(zg[:, S5_WIDTH:])
          * jax.nn.silu(sg_ref[...].astype(F32)))
    mixed = None
    for br, (o_br, w_ref) in enumerate(((oa_ref[...], wa_ref), (ob_ref[...], wb_ref), (oc, wc_ref))):
        gate = jax.nn.sigmoid(jnp.dot(h, wmg_s[:, br * D_MODEL:(br + 1) * D_MODEL], preferred_element_type=F32))
        term = gate * _dot(o_br, w_ref[...])
        mixed = term if mixed is None else mixed + term
    y_ref[...] = x + mod[:, 2 * D_MODEL:] * _dot(mixed, wout_ref[...])


def _merge(x2, row0, mod, mod_idx, nw, oa, ob, ys, zs, layer, wglu, bglu, w_in_t, wa, wb, wc, wout):
    n = x2.shape[0]
    tm = ROW_TILE
    blk0 = row0 // tm
    once = pl.Buffered(1)
    rows = lambda w: pl.BlockSpec((tm, w), lambda i: (i, 0))
    return pl.pallas_call(
        functools.partial(_merge_kernel, layer=layer),
        grid=(n // tm,),
        in_specs=[
            rows(D_MODEL),
            pl.BlockSpec((None, 1, 1, 3 * D_MODEL), lambda i: (layer, mod_idx(i), 0, 0)),
            _vector_spec(D_MODEL),
            rows(GLA_WIDTH), rows(MLA_WIDTH), rows(S5_WIDTH),
            pl.BlockSpec((tm, S5_WIDTH), lambda i: (i + blk0, ZS_GATE // S5_WIDTH)),
            pl.BlockSpec((None, S5_WIDTH, 2 * S5_WIDTH), lambda i: (layer, 0, 0), pipeline_mode=once),
            _vector_spec(2 * S5_WIDTH),
            pl.BlockSpec((pl.Element(1), pl.Element(3 * D_MODEL), pl.Element(D_MODEL)),
                         lambda i: (layer, MERGE_COL, 0), pipeline_mode=once),
            _layer_spec((GLA_WIDTH, D_MODEL), layer, once),
            _layer_spec((MLA_WIDTH, D_MODEL), layer, once),
            _layer_spec((S5_WIDTH, D_MODEL), layer, once),
            _layer_spec((D_MODEL, D_MODEL), layer, once),
        ],
        out_specs=rows(D_MODEL),
        out_shape=jax.ShapeDtypeStruct((n, D_MODEL), F32),
        scratch_shapes=[pltpu.VMEM((D_MODEL, 3 * D_MODEL), BF16)],
        compiler_params=_params(("arbitrary",)),
        name="merge",
    )(x2, mod, nw, oa, ob, ys, zs, wglu, bglu, w_in_t, wa, wb, wc, wout)


def _mla_lane_of_dim():
    half = MLA_ROPE // 2
    first_gap = ROPE_SHIFT - half
    lane = np.zeros(MLA_QK, np.int32)
    for j in range(MLA_NOPE):
        lane[j] = half + j if j < first_gap else 2 * half + j
    for r in range(half):
        lane[MLA_NOPE + r] = r
        lane[MLA_NOPE + half + r] = ROPE_SHIFT + r
    return lane


MLA_LANE_OF_DIM = _mla_lane_of_dim()


def _place_heads(w, heads, lane_of_dim):
    width = len(lane_of_dim)
    order = np.argsort(lane_of_dim)
    zeros = lambda n: jnp.zeros(w.shape[:-1] + (n,), w.dtype)
    pieces = []
    for h in range(heads):
        lane, i = 0, 0
        while i < width:
            j = i
            while (j + 1 < width and order[j + 1] == order[j] + 1
                   and lane_of_dim[order[j + 1]] == lane_of_dim[order[j]] + 1):
                j += 1
            dst = int(lane_of_dim[order[i]])
            if dst > lane:
                pieces.append(zeros(dst - lane))
            pieces.append(w[..., h * width + int(order[i]):h * width + int(order[j]) + 1])
            lane, i = dst + (j - i + 1), j + 1
        if lane < HEAD_PAD:
            pieces.append(zeros(HEAD_PAD - lane))
    return jnp.concatenate(pieces, axis=-1)


def _place_heads_bf16(w, heads, lane_of_dim):
    width = len(lane_of_dim)
    place = np.zeros((heads * width, heads * HEAD_PAD), np.float32)
    for h in range(heads):
        place[h * width + np.arange(width), h * HEAD_PAD + lane_of_dim] = 1.0
    return jnp.dot(w.astype(BF16), jnp.asarray(place, BF16), preferred_element_type=BF16)


def _rope_tables(n_tok):
    rows = n_tok // GRID_W
    r = jnp.repeat(jnp.arange(rows, dtype=F32), GRID_W)
    col = jnp.tile(jnp.arange(GRID_W, dtype=F32), rows)
    n_freq = MLA_ROPE // 4
    inv = ROPE_THETA ** (-jnp.arange(n_freq, dtype=F32) / n_freq)
    ang = jnp.concatenate([r[:, None] * inv, col[:, None] * inv], axis=-1)
    cos, sin = jnp.cos(ang), jnp.sin(ang)
    ones = jnp.ones((n_tok, MLA_NOPE), F32)
    c = _place_heads(jnp.concatenate([ones, cos, cos], axis=1), 1, MLA_LANE_OF_DIM)
    s = _place_heads(jnp.concatenate([0.0 * ones, -sin, sin], axis=1), 1, MLA_LANE_OF_DIM)
    return jnp.stack([c, s])


def kernel(x_prompt, x_sample, c, c_ctx, cache_mla_ckv, cache_mla_krope, state_gla, state_s5,
           norm_w, w_ada, b_ada, w_in, gla_w_a2, gla_b_a, gla_o_norm,
           mla_q_norm, mla_w_uq, mla_kv_norm, mla_w_uk, mla_w_uv, mla_qh_norm, mla_kh_norm,
           s5_a_re, s5_a_im, s5_log_dt, s5_b_re, s5_b_im, s5_c_re, s5_c_im, s5_d, s5_w_glu, s5_b_glu,
           w_bo_gla, w_bo_mla, w_bo_s5, w_out):
    bsz, seq, _ = x_prompt.shape
    dbsz, dseq, _ = x_sample.shape
    ctx_row = COND_ROWS - 1
    assert dbsz <= ctx_row and (bsz * seq) % ROW_TILE == 0 and dseq % ROW_TILE == 0

    cond = jnp.zeros((COND_ROWS, D_MODEL), F32).at[0:dbsz].set(c).at[ctx_row].set(c_ctx)
    ada = _ada(cond, w_ada, b_ada)

    vec = lambda a: a.reshape(2 * DEPTH, S5_NSTATE)
    ldt = jnp.repeat(s5_log_dt[..., None], S5_STATE, axis=-1)
    rows_gp = lambda t: t.reshape(DEPTH, S5_TILES, LANE, S5_STATE)
    bt = lambda b: rows_gp(b.transpose(0, 1, 3, 2))
    wmat, tab8, tab1 = _s5_prep(vec(s5_a_re), vec(s5_a_im), vec(ldt), bt(s5_b_re), bt(s5_b_im),
                                rows_gp(s5_c_re), rows_gp(s5_c_im), s5_d.reshape(DEPTH, 1, S5_WIDTH))

    wuq = _place_heads_bf16(mla_w_uq, MLA_HEADS, MLA_LANE_OF_DIM)
    wuk = _place_heads_bf16(mla_w_uk, MLA_HEADS, MLA_LANE_OF_DIM[:MLA_NOPE])
    wuv = mla_w_uv.astype(BF16)
    qhn = _place_heads(mla_qh_norm, 1, MLA_LANE_OF_DIM)
    khn = _place_heads(mla_kh_norm, 1, MLA_LANE_OF_DIM)
    e_np = np.zeros((LANE, MLA_HEADS * HEAD_PAD), np.float32)
    for h in range(MLA_HEADS):
        for i in range(MLA_ROPE):
            e_np[i, h * HEAD_PAD + MLA_LANE_OF_DIM[MLA_NOPE + i]] = 1.0
    e_place = jnp.asarray(e_np, BF16)
    rope_tab = _rope_tables(dseq)
    ckr_pad = jnp.pad(cache_mla_krope, ((0, 0), (0, 0), (0, 0), (0, LANE - MLA_ROPE)))

    zrow = lambda n: jnp.zeros((DEPTH, n, GLA_QK), F32)
    waf = jnp.concatenate([gla_w_a2[:, 0], zrow(LANE - GLA_RANK)], axis=1).astype(BF16)
    wab = jnp.concatenate([zrow(GLA_RANK), gla_w_a2[:, 1], zrow(LANE - 2 * GLA_RANK)], axis=1).astype(BF16)
    sgla = state_gla.reshape(dbsz, DEPTH, 2, GLA_QK, GLA_DV)
    ss5 = state_s5.reshape(dbsz, DEPTH, 2, 2, S5_NSTATE)
    zero_s5 = jnp.zeros((bsz, 2, 2, S5_NSTATE), F32)

    hp = x_prompt.reshape(bsz * seq, D_MODEL)
    hs = x_sample.reshape(dbsz * dseq, D_MODEL)
    ckv_l, krope_l, gla_l, s5_l = [], [], [], []
    w_in_t = jnp.swapaxes(w_in, 1, 2)
    mod = ada.reshape(DEPTH, COND_ROWS, 1, 3 * D_MODEL)
    mla_w = (mla_q_norm, wuq, mla_kv_norm, wuk, wuv, qhn, khn, e_place)
    wbo = (w_bo_gla, w_bo_mla, w_bo_s5)
    for l in range(DEPTH):
        p_rows, p_blocks, blocks_per_seq = bsz * seq, bsz * seq // ROW_TILE, dseq // ROW_TILE
        mod_idx = lambda i: jnp.where(i < p_blocks, ctx_row, (i - p_blocks) // blocks_per_seq)
        zg, zm, zs = _in_proj(hp, hs, mod, mod_idx, norm_w, w_in_t, l)

        def mixers(x2, row0, nb, n, ctx):
            if ctx:
                gctx = sgla
                x0, x0_blk = ss5, (nb, None, 2, 2, S5_TILE_STATE)
                x0_idx = lambda j: (0, l, 0, 0, j)
                mctx, rt = (cache_mla_ckv, ckr_pad), rope_tab
            else:
                gctx = None
                x0, x0_blk = zero_s5, (nb, 2, 2, S5_TILE_STATE)
                x0_idx = lambda j: (0, 0, 0, j)
                mctx, rt = None, None
            oa, st_gla = _gla(zg, row0, gctx, l, waf, wab, gla_b_a, gla_o_norm, nb, n)
            ob, ckv = _mla(zm, row0, mctx, l, mla_w, rt, nb, n)
            y_ssm, st_s5 = _s5_scan(zs, row0, x0, x0_blk, x0_idx, l, wmat, tab8, tab1, nb, n)
            grp_mod_idx = lambda i: mod_idx(i + row0 // ROW_TILE)
            y = _merge(x2, row0, mod, grp_mod_idx, norm_w, oa, ob, y_ssm, zs, l, s5_w_glu, s5_b_glu, w_in_t, *wbo, w_out)
            return y, ckv, st_gla, st_s5

        hp_next, ckv_p, st_gla_p, st_s5_p = mixers(hp, 0, bsz, seq, False)
        hs = mixers(hs, p_rows, dbsz, dseq, True)[0]
        hp = hp_next
        ckv_l.append(ckv_p.reshape(bsz, seq, MLA_KV_LORA))
        krope_l.append(zm[:p_rows, ZM_KR:ZM_KR + MLA_ROPE].astype(F32).reshape(bsz, seq, MLA_ROPE))
        gla_l.append(st_gla_p.reshape(bsz, 2, GLA_HEADS, GLA_DK, GLA_DV))
        s5_l.append(st_s5_p.reshape(bsz, 2, 2, S5_GROUPS, S5_STATE))

    return (hp.reshape(bsz, seq, D_MODEL), hs.reshape(dbsz, dseq, D_MODEL),
            jnp.stack(ckv_l, axis=1), jnp.stack(krope_l, axis=1),
            jnp.stack(gla_l, axis=1), jnp.stack(s5_l, axis=1))
```

```python
import functools

import jax
import jax.numpy as jnp
import numpy as np
from jax import lax
from jax.experimental import pallas as pl
from jax.experimental.pallas import tpu as pltpu

F32 = jnp.float32
BF16 = jnp.bfloat16

EPS = 1e-6
D_MODEL = 1024
DEPTH = 2
GRID_W = 64
ROPE_THETA = 10000.0
GLA_HEADS = 4
GLA_DK = 64
GLA_DV = 128
GLA_RANK = 16
GLA_GATE_NORM = 16.0
GLA_QK = GLA_HEADS * GLA_DK
GLA_WIDTH = GLA_HEADS * GLA_DV
MLA_HEADS = 4
MLA_Q_LORA = 384
MLA_KV_LORA = 256
MLA_NOPE = 64
MLA_ROPE = 32
MLA_QK = MLA_NOPE + MLA_ROPE
MLA_DV = 128
MLA_WIDTH = MLA_HEADS * MLA_DV
S5_WIDTH = 512
S5_GROUP = 16
S5_GROUPS = 32
S5_STATE = 64
S5_NSTATE = S5_GROUPS * S5_STATE

LANE = 128
SUBLANE = 8
COND_ROWS = SUBLANE
HEAD_PAD = LANE
ROPE_SHIFT = LANE // 2
CHUNK = 64
GLA_STEP = 256
GLA_SEQS_PER_STEP = 2
GLA_ROWS_PER_STEP = 1024
GLA_SKEW = 1
S5_TILES = S5_WIDTH // LANE
S5_TILE_STATE = S5_NSTATE // S5_TILES
ROW_TILE = 512
Q_TILE = 512
PROJ_TILE = 256
MLA_ROWS_PER_STEP = 1024
VMEM_LIMIT = 56 * 1024 * 1024

IN_SPLITS = (GLA_QK, GLA_QK, GLA_WIDTH, GLA_RANK, GLA_RANK, GLA_WIDTH,
             MLA_Q_LORA, MLA_KV_LORA, MLA_ROPE, MLA_WIDTH,
             S5_WIDTH, S5_WIDTH, 3 * D_MODEL)
(IN_GQ, IN_GK, IN_GV, IN_GA, IN_GAB, IN_GG, IN_MQ, IN_MKV, IN_MKR, IN_MG, IN_SU, IN_SG,
 MERGE_COL, D_IN) = (int(c) for c in np.cumsum((0,) + IN_SPLITS))
ZG_Q, ZG_K, ZG_V, ZG_A = 0, GLA_QK, 2 * GLA_QK, 2 * GLA_QK + GLA_WIDTH
ZG_GATE = ZG_A + LANE
ZG_W = ZG_GATE + GLA_WIDTH
ZM_Q, ZM_KV, ZM_KR = 0, MLA_Q_LORA, MLA_Q_LORA + MLA_KV_LORA
ZM_GATE = ZM_KR + LANE
ZM_W = ZM_GATE + MLA_WIDTH
ZS_U, ZS_GATE, ZS_W = 0, S5_WIDTH, 2 * S5_WIDTH
ZG_BASE, ZM_BASE, ZS_BASE, PACK_W = 0, ZG_W, ZG_W + ZM_W, ZG_W + ZM_W + ZS_W
IN_PIECES = (
    (IN_GQ, (IN_GA - IN_GQ) // LANE, ZG_BASE + ZG_Q, None),
    (IN_GA, 1, ZG_BASE + ZG_A, 2 * GLA_RANK),
    (IN_GG, GLA_WIDTH // LANE, ZG_BASE + ZG_GATE, None),
    (IN_MQ, (IN_MKR - IN_MQ) // LANE, ZM_BASE + ZM_Q, None),
    (IN_MKR, 1, ZM_BASE + ZM_KR, MLA_ROPE),
    (IN_MG, MLA_WIDTH // LANE, ZM_BASE + ZM_GATE, None),
    (IN_SU, (MERGE_COL - IN_SU) // LANE, ZS_BASE + ZS_U, None),
)


def _dot(a, b):
    return jnp.dot(a.astype(BF16), b.astype(BF16), preferred_element_type=F32)


def _dot_nt(a, b):
    return lax.dot_general(a.astype(BF16), b.astype(BF16), (((1,), (1,)), ((), ())),
                           preferred_element_type=F32)


def _split_bf16(x, parts):
    out = []
    r = x
    for _ in range(parts):
        p = r.astype(BF16)
        out.append(p)
        r = r - p.astype(F32)
    return out


def _emit_skewed(chains, skew):
    pending, active, tick = list(chains), [], 0
    while pending or active:
        while pending and (skew == 0 or tick % skew == 0):
            active.append(pending.pop(0))
            if skew:
                break
        for gen in list(active):
            if next(gen, "done") == "done":
                active.remove(gen)
        tick += 1


def _vector_spec(width):
    return pl.BlockSpec((DEPTH, width), lambda *_: (0, 0))


def _layer_spec(shape, layer, pipeline_mode=None):
    kwargs = {} if pipeline_mode is None else {"pipeline_mode": pipeline_mode}
    return pl.BlockSpec((None,) + tuple(shape), lambda *_: (layer,) + (0,) * len(shape), **kwargs)


def _params(sem):
    return pltpu.CompilerParams(dimension_semantics=sem, vmem_limit_bytes=VMEM_LIMIT)


def _ada_kernel(c_ref, w_ref, b_ref, o_ref):
    s = jax.nn.silu(c_ref[...])
    o_ref[...] = _dot(s, w_ref[...]) + b_ref[pl.ds(pl.program_id(0), 1), :]


def _ada(cond8, w_ada, b_ada):
    tn = 3 * D_MODEL // 2
    return pl.pallas_call(
        _ada_kernel,
        grid=(DEPTH, 3 * D_MODEL // tn),
        in_specs=[
            pl.BlockSpec((COND_ROWS, D_MODEL), lambda l, n: (0, 0)),
            pl.BlockSpec((None, D_MODEL, tn), lambda l, n: (l, 0, n)),
            pl.BlockSpec((DEPTH, tn), lambda l, n: (0, n)),
        ],
        out_specs=pl.BlockSpec((None, COND_ROWS, tn), lambda l, n: (l, 0, n)),
        out_shape=jax.ShapeDtypeStruct((DEPTH, COND_ROWS, 3 * D_MODEL), F32),
        compiler_params=_params(("parallel", "parallel")),
        name="ada",
    )(cond8, w_ada, b_ada)


def _mod_rmsnorm(x, nw, mod):
    ms = jnp.mean(x * x, axis=-1, keepdims=True)
    y = x * lax.rsqrt(ms + EPS) * nw
    return y * (1.0 + mod[:, D_MODEL:2 * D_MODEL]) + mod[:, 0:D_MODEL]


def _pack_transposed(w_ref, wb_s, src, tiles, dst, keep):
    lane = lax.broadcasted_iota(jnp.int32, (D_MODEL, LANE), 1)
    for t in range(tiles):
        blk = w_ref[src + t * LANE:src + (t + 1) * LANE, :].T
        if keep is not None:
            blk = jnp.where(lane < keep, blk, 0.0)
        wb_s[:, dst + t * LANE:dst + (t + 1) * LANE] = blk.astype(BF16)


def _in_proj_kernel(xp_ref, xs_ref, mod_ref, nw_ref, w_ref, zg_ref, zm_ref, zs_ref, wb_s, *, p_blocks, layer,
                    mod_idx):
    i = pl.program_id(0)

    @pl.when(i == 0)
    def _():
        for src, tiles, dst, keep in IN_PIECES:
            _pack_transposed(w_ref, wb_s, src, tiles, dst, keep)

    x = jnp.where(i < p_blocks, xp_ref[...], xs_ref[...])
    h = _mod_rmsnorm(x, nw_ref[layer:layer + 1, :], mod_ref[pl.ds(mod_idx(i), 1), :]).astype(BF16)
    z = jnp.dot(h, wb_s[...], preferred_element_type=F32)
    zg_ref[...] = z[:, ZG_BASE:ZG_BASE + ZG_W].astype(BF16)
    zm_ref[...] = z[:, ZM_BASE:ZM_BASE + ZM_W].astype(BF16)
    zs_ref[...] = z[:, ZS_BASE:ZS_BASE + ZS_W].astype(BF16)


def _two_group_rows(width, p_blocks):
    tm = ROW_TILE
    return (pl.BlockSpec((tm, width), lambda i: (jnp.minimum(i, p_blocks - 1), 0)),
            pl.BlockSpec((tm, width), lambda i: (jnp.maximum(i - p_blocks, 0), 0)))


def _in_proj(xp, xs, mod, mod_idx, nw, w_in_t, layer):
    tm = ROW_TILE
    p_blocks = xp.shape[0] // tm
    n = xp.shape[0] + xs.shape[0]
    return pl.pallas_call(
        functools.partial(_in_proj_kernel, p_blocks=p_blocks, layer=layer, mod_idx=mod_idx),
        grid=(n // tm,),
        in_specs=[
            *_two_group_rows(D_MODEL, p_blocks),
            _layer_spec((COND_ROWS, 3 * D_MODEL), layer),
            _vector_spec(D_MODEL),
            pl.BlockSpec((None, MERGE_COL, D_MODEL), lambda i: (layer, 0, 0), pipeline_mode=pl.Buffered(1)),
        ],
        out_specs=[
            pl.BlockSpec((tm, ZG_W), lambda i: (i, 0)),
            pl.BlockSpec((tm, ZM_W), lambda i: (i, 0)),
            pl.BlockSpec((tm, ZS_W), lambda i: (i, 0)),
        ],
        out_shape=[
            jax.ShapeDtypeStruct((n, ZG_W), BF16),
            jax.ShapeDtypeStruct((n, ZM_W), BF16),
            jax.ShapeDtypeStruct((n, ZS_W), BF16),
        ],
        scratch_shapes=[pltpu.VMEM((D_MODEL, PACK_W), BF16)],
        compiler_params=_params(("arbitrary",)),
        name="in_proj",
    )(xp, xs, mod, nw, w_in_t)


def _gla_kernel(*refs, nsteps, seq, nseq, has_ctx, layer):
    it = iter(refs)
    zg_ref = next(it)
    s0_ref = next(it) if has_ctx else None
    waf_ref, wab_ref, ba_ref, onorm_ref, o_ref, sfin_ref, la_s, o_s, st_s = (next(it) for _ in range(9))
    chains = [(g, d) for g in range(nseq) for d in (0, 1)]
    inv_norm = 1.0 / GLA_GATE_NORM
    zero_blk = jnp.zeros((GLA_DK, GLA_DV), F32)
    for ch, (g, d) in enumerate(chains):
        if d == 0:
            a_blk = zg_ref[g * seq:(g + 1) * seq, ZG_A:ZG_A + LANE]
        wa_ref = waf_ref if d == 0 else wab_ref
        a_low = _dot(a_blk, wa_ref[...]) + ba_ref[d:d + 1, :]
        la_s[ch] = (jnp.minimum(a_low, 0.0) - jnp.log(1.0 + jnp.exp(-jnp.abs(a_low)))) * inv_norm
        if has_ctx:
            s0 = s0_ref[g, d]
            rows_bd = []
            for h in range(GLA_HEADS):
                sh = s0[h * GLA_DK:(h + 1) * GLA_DK, :]
                rows_bd.append(jnp.concatenate([sh if h2 == h else zero_blk for h2 in range(GLA_HEADS)], axis=1))
            st_s[ch] = jnp.concatenate(rows_bd, axis=0).T
        else:
            st_s[ch] = jnp.zeros((GLA_WIDTH, GLA_QK), F32)

    def iota(shape, axis, shift):
        return lax.shift_right_logical(lax.broadcasted_iota(jnp.int32, shape, axis), shift)

    log_chunk, log_dv = CHUNK.bit_length() - 1, GLA_DV.bit_length() - 1
    row = lax.broadcasted_iota(jnp.int32, (GLA_STEP, GLA_STEP), 0)
    col = lax.broadcasted_iota(jnp.int32, (GLA_STEP, GLA_STEP), 1)
    same_chunk = iota((GLA_STEP, GLA_STEP), 0, log_chunk) == iota((GLA_STEP, GLA_STEP), 1, log_chunk)
    masks = (same_chunk & (row >= col), same_chunk & (row <= col))
    lane_head = iota((GLA_STEP, GLA_QK), 1, log_chunk)
    row_chunk = iota((GLA_STEP, GLA_QK), 0, log_chunk)
    state_blk = iota((GLA_WIDTH, GLA_QK), 0, log_dv) == iota((GLA_WIDTH, GLA_QK), 1, log_chunk)
    qscale = GLA_DK ** -0.5
    nch = GLA_STEP // CHUNK

    def chain_phases(i, ch, g, d):
        r0 = pl.multiple_of((i if d == 0 else nsteps - 1 - i) * GLA_STEP, GLA_STEP)
        zrows, rows = pl.ds(g * seq + r0, GLA_STEP), pl.ds(r0, GLA_STEP)
        a_hi, a_lo = _split_bf16(la_s[ch, rows, :], 2)
        tri = masks[d].astype(BF16)
        cum = (jnp.dot(tri, a_hi, preferred_element_type=F32)
               + jnp.dot(tri, a_lo, preferred_element_type=F32))
        yield
        edge = CHUNK - 1 if d == 0 else 0
        blast = [cum[c * CHUNK + edge:c * CHUNK + edge + 1, :] for c in range(nch)]
        bl = jnp.concatenate([jnp.broadcast_to(b, (CHUNK, GLA_QK)) for b in blast], axis=0)
        q = zg_ref[zrows, ZG_Q:ZG_Q + GLA_QK].astype(F32) * qscale
        k = zg_ref[zrows, ZG_K:ZG_K + GLA_QK].astype(F32)
        v = zg_ref[zrows, ZG_V:ZG_V + GLA_WIDTH]
        v_t = v.astype(F32).T.astype(BF16)
        qd = q * jnp.exp(cum)
        kd = (k * jnp.exp(-cum)).astype(BF16)
        kr = k * jnp.exp(bl - cum)
        yield
        outs = []
        for h in range(GLA_HEADS):
            qh = jnp.where(lane_head == h, qd, 0.0)
            att = _dot_nt(qh, kd)
            yield
            att = jnp.where(masks[d], att, 0.0)
            outs.append(_dot(att, v[:, h * GLA_DV:(h + 1) * GLA_DV]))
            yield
        s = st_s[ch]
        inter = [None] * nch
        for c in (range(nch) if d == 0 else reversed(range(nch))):
            inter[c] = _dot_nt(qd[c * CHUNK:(c + 1) * CHUNK, :], s)
            kv_t = _dot(v_t, jnp.where(row_chunk == c, kr, 0.0))
            yield
            s = s * jnp.exp(blast[c]) + jnp.where(state_blk, kv_t, 0.0)
            yield
        st_s[ch] = s
        o_s[ch, rows, :] = jnp.concatenate(outs, axis=1) + jnp.concatenate(inter, axis=0)

    def step(i, carry):
        _emit_skewed([chain_phases(i, ch, g, d) for ch, (g, d) in enumerate(chains)], GLA_SKEW)
        return carry

    lax.fori_loop(0, nsteps, step, 0)
    onorm = onorm_ref[layer:layer + 1, :]
    for ch, (g, d) in enumerate(chains):
        s_fin = st_s[ch].T
        for h in range(GLA_HEADS):
            sfin_ref[g, d, h * GLA_DK:(h + 1) * GLA_DK, :] = (
                s_fin[h * GLA_DK:(h + 1) * GLA_DK, h * GLA_DV:(h + 1) * GLA_DV])
    for g in range(nseq):
        srows = slice(g * seq, (g + 1) * seq)
        o = o_s[2 * g] + o_s[2 * g + 1]
        gate = zg_ref[srows, ZG_GATE:ZG_GATE + GLA_WIDTH].astype(F32)
        for h in range(GLA_HEADS):
            vs = slice(h * GLA_DV, (h + 1) * GLA_DV)
            oh = o[:, vs]
            ms = jnp.mean(oh * oh, axis=-1, keepdims=True)
            o_ref[srows, vs] = oh * lax.rsqrt(ms + EPS) * onorm * jax.nn.silu(gate[:, vs])


def _gla(zg, row0, ctx, layer, waf, wab, ba, onorm, bsz, seq):
    nseq = max(GLA_SEQS_PER_STEP, GLA_ROWS_PER_STEP // seq)
    blk0 = row0 // (nseq * seq)
    in_specs = [pl.BlockSpec((nseq * seq, ZG_W), lambda b: (b + blk0, 0))]
    args = [zg]
    if ctx is not None:
        in_specs.append(pl.BlockSpec((nseq, None, 2, GLA_QK, GLA_DV), lambda b: (b, layer, 0, 0, 0)))
        args.append(ctx)
    in_specs += [
        _layer_spec((LANE, GLA_QK), layer),
        _layer_spec((LANE, GLA_QK), layer),
        _layer_spec((2, GLA_QK), layer),
        _vector_spec(GLA_DV),
    ]
    return pl.pallas_call(
        functools.partial(_gla_kernel, nsteps=seq // GLA_STEP, seq=seq, nseq=nseq, has_ctx=ctx is not None,
                          layer=layer),
        grid=(bsz // nseq,),
        in_specs=in_specs,
        out_specs=[
            pl.BlockSpec((nseq * seq, GLA_WIDTH), lambda b: (b, 0)),
            pl.BlockSpec((nseq, 2, GLA_QK, GLA_DV), lambda b: (b, 0, 0, 0)),
        ],
        out_shape=[
            jax.ShapeDtypeStruct((bsz * seq, GLA_WIDTH), F32),
            jax.ShapeDtypeStruct((bsz, 2, GLA_QK, GLA_DV), F32),
        ],
        scratch_shapes=[
            pltpu.VMEM((2 * nseq, seq, GLA_QK), F32),
            pltpu.VMEM((2 * nseq, seq, GLA_WIDTH), F32),
            pltpu.VMEM((2 * nseq, GLA_WIDTH, GLA_QK), F32),
        ],
        compiler_params=_params(("parallel",)),
        name="gla",
    )(*args, waf, wab, ba, onorm)


def _rms(x, w):
    ms = jnp.mean(x * x, axis=-1, keepdims=True)
    return x * lax.rsqrt(ms + EPS) * w


def _head_sums_mxu(x):
    width = x.shape[-1]
    shift = HEAD_PAD.bit_length() - 1
    gi = lax.shift_right_logical(lax.broadcasted_iota(jnp.int32, (width, width), 0), shift)
    gj = lax.shift_right_logical(lax.broadcasted_iota(jnp.int32, (width, width), 1), shift)
    return _dot(x * x, jnp.where(gi == gj, 1.0, 0.0))


def _head_norm(x, w, rope, on_mxu):
    sums = _head_sums_mxu(x) if on_mxu else None
    outs = []
    for h in range(MLA_HEADS):
        hs = slice(h * HEAD_PAD, (h + 1) * HEAD_PAD)
        xh = x[:, hs]
        ss = sums[:, hs] if on_mxu else jnp.sum(xh * xh, axis=-1, keepdims=True)
        yh = xh * lax.rsqrt(ss * (1.0 / MLA_QK) + EPS) * w
        if rope is not None:
            c, s = rope
            yh = yh * c + pltpu.roll(yh, ROPE_SHIFT, 1) * s
        outs.append(yh)
    return outs


def _place_rope_key(kr, e):
    return sum(jnp.dot(p, e, preferred_element_type=F32) for p in _split_bf16(kr, 3))


def _mla_kernel(*refs, seq, nseq, n_ctx, use_rope, layer):
    it = iter(refs)
    zm_ref = next(it)
    if n_ctx:
        cckv_ref, ckr_ref = next(it), next(it)
    qn_ref, wuq_ref, kvn_ref, wuk_ref, wuv_ref, qhn_ref, khn_ref, e_ref = (next(it) for _ in range(8))
    rope_ref = next(it) if use_rope else None
    o_ref, ckv_ref = next(it), next(it)
    q_s, k_s, v_s = next(it), next(it), next(it)
    qn, kvn, qhn, khn = (r[layer:layer + 1, :] for r in (qn_ref, kvn_ref, qhn_ref, khn_ref))

    qscale = MLA_QK ** -0.5
    heads = [slice(h * HEAD_PAD, (h + 1) * HEAD_PAD) for h in range(MLA_HEADS)]

    def keys_values(g, ckv, k_rope_placed, rope, k_rows):
        k_raw = _dot(ckv, wuk_ref[...]) + k_rope_placed
        yield
        kh = _head_norm(k_raw, khn, rope, False)
        for h, hs in enumerate(heads):
            k_s[g, k_rows, hs] = kh[h].astype(BF16)
        yield
        v_s[g, k_rows, :] = _dot(ckv, wuv_ref[...]).astype(BF16)
        yield

    def latent_phases(i, g):
        r0 = pl.multiple_of(i * PROJ_TILE, PROJ_TILE)
        tile, rows = pl.ds(r0, PROJ_TILE), pl.ds(g * seq + r0, PROJ_TILE)
        rope = (rope_ref[0, tile, :], rope_ref[1, tile, :]) if use_rope else None
        ckv = _rms(zm_ref[rows, ZM_KV:ZM_KV + MLA_KV_LORA].astype(F32), kvn)
        ckv_ref[rows, :] = ckv
        k_pe = jnp.dot(zm_ref[rows, ZM_KR:ZM_KR + LANE], e_ref[...], preferred_element_type=F32)
        yield from keys_values(g, ckv, k_pe, rope, pl.ds(n_ctx + r0, PROJ_TILE))
        cq = _rms(zm_ref[rows, ZM_Q:ZM_Q + MLA_Q_LORA].astype(F32), qn)
        q_raw = _dot(cq, wuq_ref[...])
        yield
        qh = _head_norm(q_raw, qhn, rope, True)
        for h, hs in enumerate(heads):
            q_s[rows, hs] = (qh[h] * qscale).astype(BF16)

    def latent_tile(i, carry):
        _emit_skewed([latent_phases(i, g) for g in range(nseq)], 0)
        return carry

    lax.fori_loop(0, seq // PROJ_TILE, latent_tile, 0)

    def context_tile(i, carry):
        rows = pl.ds(pl.multiple_of(i * PROJ_TILE, PROJ_TILE), PROJ_TILE)
        _emit_skewed([keys_values(g, cckv_ref[g, rows, :], _place_rope_key(ckr_ref[g, rows, :], e_ref[...]),
                                  None, rows) for g in range(nseq)], 0)
        return carry

    if n_ctx:
        lax.fori_loop(0, n_ctx // PROJ_TILE, context_tile, 0)

    q_tile = min(seq, Q_TILE)

    def head_phases(g, h, hs, rows, gate):
        s = lax.dot_general(q_s[rows, hs], k_s[g, :, hs], (((1,), (1,)), ((), ())),
                            preferred_element_type=F32)
        yield
        e = jnp.exp(s - jnp.max(s, axis=-1, keepdims=True))
        l = jnp.sum(e, axis=-1, keepdims=True)
        p = e.astype(BF16)
        yield
        o = jnp.dot(p, v_s[g, :, hs], preferred_element_type=F32) / l
        o_ref[rows, hs] = o * jax.nn.silu(gate[:, hs])

    def q_block(i, carry):
        chains = []
        for g in range(nseq):
            rows = pl.ds(g * seq + pl.multiple_of(i * q_tile, q_tile), q_tile)
            gate = zm_ref[rows, ZM_GATE:ZM_GATE + MLA_WIDTH].astype(F32)
            chains += [head_phases(g, h, hs, rows, gate) for h, hs in enumerate(heads)]
        _emit_skewed(chains, 0)
        return carry

    lax.fori_loop(0, seq // q_tile, q_block, 0)


def _mla(zm, row0, ctx, layer, w, rope_tab, bsz, seq):
    n_ctx = 0 if ctx is None else ctx[0].shape[-2]
    nseq = max(1, MLA_ROWS_PER_STEP // seq)
    blk0 = row0 // (nseq * seq)
    in_specs = [pl.BlockSpec((nseq * seq, ZM_W), lambda b: (b + blk0, 0))]
    args = [zm]
    if ctx is not None:
        cckv, ckr = ctx
        in_specs += [
            pl.BlockSpec((nseq, None, n_ctx, MLA_KV_LORA), lambda b: (b, layer, 0, 0)),
            pl.BlockSpec((nseq, None, n_ctx, LANE), lambda b: (b, layer, 0, 0)),
        ]
        args += [cckv, ckr]
    in_specs += [
        _vector_spec(MLA_Q_LORA),
        _layer_spec((MLA_Q_LORA, MLA_HEADS * HEAD_PAD), layer),
        _vector_spec(MLA_KV_LORA),
        _layer_spec((MLA_KV_LORA, MLA_HEADS * HEAD_PAD), layer),
        _layer_spec((MLA_KV_LORA, MLA_WIDTH), layer),
        _vector_spec(HEAD_PAD),
        _vector_spec(HEAD_PAD),
        pl.BlockSpec((LANE, MLA_HEADS * HEAD_PAD), lambda b: (0, 0)),
    ]
    args += list(w)
    if rope_tab is not None:
        in_specs.append(pl.BlockSpec((2, seq, HEAD_PAD), lambda b: (0, 0, 0)))
        args.append(rope_tab)
    return pl.pallas_call(
        functools.partial(_mla_kernel, seq=seq, nseq=nseq, n_ctx=n_ctx, use_rope=rope_tab is not None,
                          layer=layer),
        grid=(bsz // nseq,),
        in_specs=in_specs,
        out_specs=[
            pl.BlockSpec((nseq * seq, MLA_WIDTH), lambda b: (b, 0)),
            pl.BlockSpec((nseq * seq, MLA_KV_LORA), lambda b: (b, 0)),
        ],
        out_shape=[
            jax.ShapeDtypeStruct((bsz * seq, MLA_WIDTH), F32),
            jax.ShapeDtypeStruct((bsz * seq, MLA_KV_LORA), F32),
        ],
        scratch_shapes=[
            pltpu.VMEM((nseq * seq, MLA_HEADS * HEAD_PAD), BF16),
            pltpu.VMEM((nseq, n_ctx + seq, MLA_HEADS * HEAD_PAD), BF16),
            pltpu.VMEM((nseq, n_ctx + seq, MLA_WIDTH), BF16),
        ],
        compiler_params=_params(("parallel",)),
        name="mla",
    )(*args)


S5_T = 8
S5_R = CHUNK // S5_T
S5_SUB_CH = 64
S5_SUBS = LANE // S5_SUB_CH
S5_SUB_STATE = S5_TILE_STATE // S5_SUBS
S5_ROW = S5_T * S5_SUB_CH
S5_W = 2 * S5_SUB_STATE
W_M, W_SF, W_SB, W_CF, W_CB = range(5)


def _cmul(ar, ai, br, bi):
    return ar * br - ai * bi, ar * bi + ai * br


def _s5_prep_kernel(are_ref, aim_ref, ldt_ref, bre_ref, bim_ref, cre_ref, cim_ref, d_ref,
                    w_ref, tab8_ref, tab1_ref):
    gr = lax.shift_right_logical(lax.broadcasted_iota(jnp.int32, (S5_SUB_CH, S5_SUB_STATE), 0),
                                 S5_GROUP.bit_length() - 1)
    gc = lax.shift_right_logical(lax.broadcasted_iota(jnp.int32, (S5_SUB_CH, S5_SUB_STATE), 1),
                                 S5_STATE.bit_length() - 1)

    def spread(ref, h):
        x = ref[h * S5_SUB_CH:(h + 1) * S5_SUB_CH, :]
        return jnp.where(gr == gc, jnp.concatenate([x] * (S5_SUB_CH // S5_GROUP), axis=1), 0.0)

    row = lax.broadcasted_iota(jnp.int32, (S5_SUB_CH, S5_SUB_CH), 0)
    col = lax.broadcasted_iota(jnp.int32, (S5_SUB_CH, S5_SUB_CH), 1)
    taps = [[[], []] for _ in range(S5_SUBS)]
    for d in (0, 1):
        prow = pl.ds(2 * pl.program_id(0) + d, 1)
        a_re, a_im = are_ref[prow, :], aim_ref[prow, :]
        dt = jnp.exp(ldt_ref[prow, :])
        lam = a_re * dt
        th = a_im * dt
        mag = jnp.exp(lam)
        ab_re = mag * jnp.cos(th)
        ab_im = mag * jnp.sin(th)
        den = a_re * a_re + a_im * a_im
        n_re = ab_re - 1.0
        cf_re = (n_re * a_re + ab_im * a_im) / den
        cf_im = (ab_im * a_re - n_re * a_im) / den
        k = lax.broadcasted_iota(jnp.int32, (2 * S5_T, S5_TILE_STATE), 0).astype(F32)
        pmag = jnp.exp(k * lam)
        pw_re = pmag * jnp.cos(k * th)
        pw_im = pmag * jnp.sin(k * th)
        for h in range(S5_SUBS):
            ss = slice(h * S5_SUB_STATE, (h + 1) * S5_SUB_STATE)
            c_re, c_im = spread(cre_ref, h), spread(cim_ref, h)
            c_cat = jnp.concatenate([c_re, c_im], axis=1).astype(BF16)
            bp_re, bp_im = _cmul(spread(bre_ref, h), spread(bim_ref, h), cf_re[:, ss], cf_im[:, ss])
            for p in range(S5_T + 1):
                ar, ai = pw_re[p:p + 1, ss], pw_im[p:p + 1, ss]
                t_in = S5_T - 1 - p if d == 0 else p
                t_out = p - 1 if d == 0 else S5_T - p
                if p < S5_T:
                    l_re, l_im = _cmul(bp_re, bp_im, ar, ai)
                    w_ref[h, W_SF + d, t_in * S5_SUB_CH:(t_in + 1) * S5_SUB_CH, :] = (
                        jnp.concatenate([l_re, l_im], axis=1).astype(BF16))
                    taps[h][d].append(_dot_nt(jnp.concatenate([l_re, -l_im], axis=1), c_cat))
                if p > 0:
                    v_re, v_im = _cmul(c_re, c_im, ar, ai)
                    w_ref[h, W_CF + d, t_out * S5_SUB_CH:(t_out + 1) * S5_SUB_CH, :] = (
                        jnp.concatenate([v_re, -v_im], axis=1).astype(BF16))
        r = lax.broadcasted_iota(jnp.int32, (S5_R, S5_TILE_STATE), 0).astype(F32) * float(S5_T)
        r1 = r + float(S5_T)
        pm = jnp.exp(r * lam)
        qm = jnp.exp(-(r1 * lam))
        tab8_ref[d, 0] = pm * jnp.cos(r * th)
        tab8_ref[d, 1] = pm * jnp.sin(r * th)
        tab8_ref[d, 2] = qm * jnp.cos(r1 * th)
        tab8_ref[d, 3] = -(qm * jnp.sin(r1 * th))
        mc = jnp.exp(float(CHUNK) * lam)
        tab1_ref[d, 0:1, :] = mc * jnp.cos(float(CHUNK) * th)
        tab1_ref[d, 1:2, :] = mc * jnp.sin(float(CHUNK) * th)
    for h in range(S5_SUBS):
        skip = jnp.where(row == col, d_ref[:, h * S5_SUB_CH:(h + 1) * S5_SUB_CH], 0.0)
        for t in range(S5_T):
            blocks = []
            for t2 in range(S5_T):
                if t < t2:
                    blocks.append(taps[h][0][t2 - t])
                elif t > t2:
                    blocks.append(taps[h][1][t - t2])
                else:
                    blocks.append(taps[h][0][0] + taps[h][1][0] + skip)
            w_ref[h, W_M, t * S5_SUB_CH:(t + 1) * S5_SUB_CH, :] = jnp.concatenate(blocks, axis=1).astype(BF16)


def _s5_prep(a_re, a_im, ldt, b_re, b_im, c_re, c_im, dsk):
    vec = pl.BlockSpec((2 * DEPTH, S5_TILE_STATE), lambda l, j: (0, j))
    blk = pl.BlockSpec((None, None, LANE, S5_STATE), lambda l, j: (l, j, 0, 0))
    return pl.pallas_call(
        _s5_prep_kernel,
        grid=(DEPTH, S5_TILES),
        in_specs=[vec, vec, vec, blk, blk, blk, blk,
                  pl.BlockSpec((None, 1, LANE), lambda l, j: (l, 0, j))],
        out_specs=[
            pl.BlockSpec((None, None, S5_SUBS, 5, S5_ROW, S5_W), lambda l, j: (l, j, 0, 0, 0, 0)),
            pl.BlockSpec((None, None, 2, 4, S5_R, S5_TILE_STATE), lambda l, j: (l, j, 0, 0, 0, 0)),
            pl.BlockSpec((None, None, 2, 2, S5_TILE_STATE), lambda l, j: (l, j, 0, 0, 0)),
        ],
        out_shape=[
            jax.ShapeDtypeStruct((DEPTH, S5_TILES, S5_SUBS, 5, S5_ROW, S5_W), BF16),
            jax.ShapeDtypeStruct((DEPTH, S5_TILES, 2, 4, S5_R, S5_TILE_STATE), F32),
            jax.ShapeDtypeStruct((DEPTH, S5_TILES, 2, 2, S5_TILE_STATE), F32),
        ],
        compiler_params=_params(("parallel", "parallel")),
        name="s5_prep",
    )(a_re, a_im, ldt, b_re, b_im, c_re, c_im, dsk)


def _s5_scan_kernel(u_ref, x0_ref, w_ref, tab8_ref, tab1_ref, y_ref, fs_ref, u_s, *, nseq, nb):
    groups = nseq * nb
    nrow = groups * S5_R
    ts = S5_SUB_STATE
    u_s[...] = u_ref[...].astype(F32)
    tokens = [u_s[pl.ds(t, nrow, stride=S5_T), :] for t in range(S5_T)]
    rowi = lax.broadcasted_iota(jnp.int32, (groups, S5_R, ts), 1)

    def prefix(x):
        for s in (1, 2, 4):
            x = x + jnp.where(rowi >= s, pltpu.roll(x, s, 1), 0.0)
        return x

    def suffix(x):
        for s in (1, 2, 4):
            x = x + jnp.where(rowi < S5_R - s, pltpu.roll(x, S5_R - s, 1), 0.0)
        return x

    y_sub = []
    for h in range(S5_SUBS):
        ch = slice(h * S5_SUB_CH, (h + 1) * S5_SUB_CH)
        ss = slice(h * ts, (h + 1) * ts)
        u8 = jnp.concatenate([tok[:, ch] for tok in tokens], axis=1).astype(BF16)
        ef = jnp.dot(u8, w_ref[h, W_SF], preferred_element_type=F32).reshape(groups, S5_R, S5_W)
        eb = jnp.dot(u8, w_ref[h, W_SB], preferred_element_type=F32).reshape(groups, S5_R, S5_W)

        p_re, p_im, q_re, q_im = (tab8_ref[0, i, :, ss] for i in range(4))
        a_re, a_im = tab1_ref[0, 0:1, ss], tab1_ref[0, 1:2, ss]
        w_re, w_im = _cmul(q_re, q_im, ef[:, :, :ts], ef[:, :, ts:])
        cs_re, cs_im = prefix(w_re), prefix(w_im)
        st_re, st_im = [], []
        for s in range(nseq):
            x_re, x_im = x0_ref[s, 0, 0:1, ss], x0_ref[s, 0, 1:2, ss]
            for b in range(nb):
                g = s * nb + b
                st_re.append(x_re)
                st_im.append(x_im)
                x_re, x_im = _cmul(a_re, a_im, x_re + cs_re[g, S5_R - 1:S5_R, :],
                                   x_im + cs_im[g, S5_R - 1:S5_R, :])
            fs_ref[s, 0, 0:1, ss] = x_re
            fs_ref[s, 0, 1:2, ss] = x_im
        xin_re, xin_im = _cmul(p_re, p_im, cs_re - w_re + jnp.stack(st_re), cs_im - w_im + jnp.stack(st_im))
        xin = jnp.concatenate([xin_re, xin_im], axis=2).reshape(nrow, S5_W)

        p_re, p_im, q_re, q_im = (tab8_ref[1, i, :, ss] for i in range(4))
        a_re, a_im = tab1_ref[1, 0:1, ss], tab1_ref[1, 1:2, ss]
        w_re, w_im = _cmul(p_re, p_im, eb[:, :, :ts], eb[:, :, ts:])
        sf_re, sf_im = suffix(w_re), suffix(w_im)
        z_re, z_im = [None] * groups, [None] * groups
        for s in range(nseq):
            x_re, x_im = x0_ref[s, 1, 0:1, ss], x0_ref[s, 1, 1:2, ss]
            for b in reversed(range(nb)):
                g = s * nb + b
                z_re[g], z_im[g] = _cmul(a_re, a_im, x_re, x_im)
                x_re = sf_re[g, 0:1, :] + z_re[g]
                x_im = sf_im[g, 0:1, :] + z_im[g]
            fs_ref[s, 1, 0:1, ss] = x_re
            fs_ref[s, 1, 1:2, ss] = x_im
        xnx_re, xnx_im = _cmul(q_re, q_im, sf_re - w_re + jnp.stack(z_re), sf_im - w_im + jnp.stack(z_im))
        xnx = jnp.concatenate([xnx_re, xnx_im], axis=2).reshape(nrow, S5_W)

        y_sub.append(jnp.dot(u8, w_ref[h, W_M], preferred_element_type=F32)
                     + _dot_nt(xin, w_ref[h, W_CF]) + _dot_nt(xnx, w_ref[h, W_CB]))
    for t in range(S5_T):
        tc = slice(t * S5_SUB_CH, (t + 1) * S5_SUB_CH)
        y_ref[pl.ds(t, nrow, stride=S5_T), :] = jnp.concatenate([y[:, tc] for y in y_sub], axis=1)


def _s5_scan(zs, row0, x0, x0_block, x0_idx, layer, wmat, tab8, tab1, nseq, seq):
    n = nseq * seq
    rblk = row0 // n
    return pl.pallas_call(
        functools.partial(_s5_scan_kernel, nseq=nseq, nb=seq // CHUNK),
        grid=(S5_TILES,),
        in_specs=[
            pl.BlockSpec((n, LANE), lambda j: (rblk, ZS_U // LANE + j)),
            pl.BlockSpec(x0_block, x0_idx),
            pl.BlockSpec((None, None, S5_SUBS, 5, S5_ROW, S5_W), lambda j: (layer, j, 0, 0, 0, 0)),
            pl.BlockSpec((None, None, 2, 4, S5_R, S5_TILE_STATE), lambda j: (layer, j, 0, 0, 0, 0)),
            pl.BlockSpec((None, None, 2, 2, S5_TILE_STATE), lambda j: (layer, j, 0, 0, 0)),
        ],
        out_specs=[
            pl.BlockSpec((n, LANE), lambda j: (0, j)),
            pl.BlockSpec((nseq, 2, 2, S5_TILE_STATE), lambda j: (0, 0, 0, j)),
        ],
        out_shape=[
            jax.ShapeDtypeStruct((n, S5_WIDTH), F32),
            jax.ShapeDtypeStruct((nseq, 2, 2, S5_NSTATE), F32),
        ],
        scratch_shapes=[pltpu.VMEM((n, LANE), F32)],
        compiler_params=_params(("parallel",)),
        name="s5_scan",
    )(zs, x0, wmat, tab8, tab1)


def _merge_kernel(x_ref, mod_ref, nw_ref, oa_ref, ob_ref, ys_ref, sg_ref, wglu_ref, bglu_ref, wmg_ref,
                  wa_ref, wb_ref, wc_ref, wout_ref, y_ref, wmg_s, *, layer, mod_idx):
    @pl.when(pl.program_id(0) == 0)
    def _():
        _pack_transposed(wmg_ref.at[0], wmg_s, 0, 3 * D_MODEL // LANE, 0, None)

    x = x_ref[...]
    mod = mod_ref[pl.ds(mod_idx(pl.program_id(0)), 1), :]
    h = _mod_rmsnorm(x, nw_ref[layer:layer + 1, :], mod).astype(BF16)
    zg = _dot(jax.nn.gelu(ys_ref[...]), wglu_ref[...]) + bglu_ref[layer:layer + 1, :]
    oc = (zg[:, :S5_WIDTH] * jax.nn.sigmoid(zg[:, S5_WIDTH:])
          * jax.nn.silu(sg_ref[...].astype(F32)))
    mixed = None
    for br, (o_br, w_ref) in enumerate(((oa_ref[...], wa_ref), (ob_ref[...], wb_ref), (oc, wc_ref))):
        gate = jax.nn.sigmoid(jnp.dot(h, wmg_s[:, br * D_MODEL:(br + 1) * D_MODEL], preferred_element_type=F32))
        term = gate * _dot(o_br, w_ref[...])
        mixed = term if mixed is None else mixed + term
    y_ref[...] = x + mod[:, 2 * D_MODEL:] * _dot(mixed, wout_ref[...])


def _merge(x2, row0, mod, mod_idx, nw, oa, ob, ys, zs, layer, wglu, bglu, w_in_t, wa, wb, wc, wout):
    n = x2.shape[0]
    tm = ROW_TILE
    blk0 = row0 // tm
    once = pl.Buffered(1)
    rows = lambda w: pl.BlockSpec((tm, w), lambda i: (i, 0))
    return pl.pallas_call(
        functools.partial(_merge_kernel, layer=layer, mod_idx=mod_idx),
        grid=(n // tm,),
        in_specs=[
            rows(D_MODEL),
            _layer_spec((COND_ROWS, 3 * D_MODEL), layer),
            _vector_spec(D_MODEL),
            rows(GLA_WIDTH), rows(MLA_WIDTH), rows(S5_WIDTH),
            pl.BlockSpec((tm, S5_WIDTH), lambda i: (i + blk0, ZS_GATE // S5_WIDTH)),
            pl.BlockSpec((None, S5_WIDTH, 2 * S5_WIDTH), lambda i: (layer, 0, 0), pipeline_mode=once),
            _vector_spec(2 * S5_WIDTH),
            pl.BlockSpec((pl.Element(1), pl.Element(3 * D_MODEL), pl.Element(D_MODEL)),
                         lambda i: (layer, MERGE_COL, 0), pipeline_mode=once),
            _layer_spec((GLA_WIDTH, D_MODEL), layer, once),
            _layer_spec((MLA_WIDTH, D_MODEL), layer, once),
            _layer_spec((S5_WIDTH, D_MODEL), layer, once),
            _layer_spec((D_MODEL, D_MODEL), layer, once),
        ],
        out_specs=rows(D_MODEL),
        out_shape=jax.ShapeDtypeStruct((n, D_MODEL), F32),
        scratch_shapes=[pltpu.VMEM((D_MODEL, 3 * D_MODEL), BF16)],
        compiler_params=_params(("arbitrary",)),
        name="merge",
    )(x2, mod, nw, oa, ob, ys, zs, wglu, bglu, w_in_t, wa, wb, wc, wout)


def _mla_lane_of_dim():
    half = MLA_ROPE // 2
    first_gap = ROPE_SHIFT - half
    lane = np.zeros(MLA_QK, np.int32)
    for j in range(MLA_NOPE):
        lane[j] = half + j if j < first_gap else 2 * half + j
    for r in range(half):
        lane[MLA_NOPE + r] = r
        lane[MLA_NOPE + half + r] = ROPE_SHIFT + r
    return lane


MLA_LANE_OF_DIM = _mla_lane_of_dim()


def _place_heads(w, heads, lane_of_dim):
    width = len(lane_of_dim)
    order = np.argsort(lane_of_dim)
    zeros = lambda n: jnp.zeros(w.shape[:-1] + (n,), w.dtype)
    pieces = []
    for h in range(heads):
        lane, i = 0, 0
        while i < width:
            j = i
            while (j + 1 < width and order[j + 1] == order[j] + 1
                   and lane_of_dim[order[j + 1]] == lane_of_dim[order[j]] + 1):
                j += 1
            dst = int(lane_of_dim[order[i]])
            if dst > lane:
                pieces.append(zeros(dst - lane))
            pieces.append(w[..., h * width + int(order[i]):h * width + int(order[j]) + 1])
            lane, i = dst + (j - i + 1), j + 1
        if lane < HEAD_PAD:
            pieces.append(zeros(HEAD_PAD - lane))
    return jnp.concatenate(pieces, axis=-1)


def _place_heads_bf16(w, heads, lane_of_dim):
    width = len(lane_of_dim)
    place = np.zeros((heads * width, heads * HEAD_PAD), np.float32)
    for h in range(heads):
        place[h * width + np.arange(width), h * HEAD_PAD + lane_of_dim] = 1.0
    return jnp.dot(w.astype(BF16), jnp.asarray(place, BF16), preferred_element_type=BF16)


def _rope_tables(n_tok):
    rows = n_tok // GRID_W
    r = jnp.repeat(jnp.arange(rows, dtype=F32), GRID_W)
    col = jnp.tile(jnp.arange(GRID_W, dtype=F32), rows)
    n_freq = MLA_ROPE // 4
    inv = ROPE_THETA ** (-jnp.arange(n_freq, dtype=F32) / n_freq)
    ang = jnp.concatenate([r[:, None] * inv, col[:, None] * inv], axis=-1)
    cos, sin = jnp.cos(ang), jnp.sin(ang)
    ones = jnp.ones((n_tok, MLA_NOPE), F32)
    c = _place_heads(jnp.concatenate([ones, cos, cos], axis=1), 1, MLA_LANE_OF_DIM)
    s = _place_heads(jnp.concatenate([0.0 * ones, -sin, sin], axis=1), 1, MLA_LANE_OF_DIM)
    return jnp.stack([c, s])


def kernel(x_prompt, x_sample, c, c_ctx, cache_mla_ckv, cache_mla_krope, state_gla, state_s5,
           norm_w, w_ada, b_ada, w_in, gla_w_a2, gla_b_a, gla_o_norm,
           mla_q_norm, mla_w_uq, mla_kv_norm, mla_w_uk, mla_w_uv, mla_qh_norm, mla_kh_norm,
           s5_a_re, s5_a_im, s5_log_dt, s5_b_re, s5_b_im, s5_c_re, s5_c_im, s5_d, s5_w_glu, s5_b_glu,
           w_bo_gla, w_bo_mla, w_bo_s5, w_out):
    bsz, seq, _ = x_prompt.shape
    dbsz, dseq, _ = x_sample.shape
    ctx_row = COND_ROWS - 1
    assert dbsz <= ctx_row and (bsz * seq) % ROW_TILE == 0 and dseq % ROW_TILE == 0

    cond = jnp.zeros((COND_ROWS, D_MODEL), F32).at[0:dbsz].set(c).at[ctx_row].set(c_ctx)
    ada = _ada(cond, w_ada, b_ada)

    vec = lambda a: a.reshape(2 * DEPTH, S5_NSTATE)
    ldt = jnp.repeat(s5_log_dt[..., None], S5_STATE, axis=-1)
    rows_gp = lambda t: t.reshape(DEPTH, S5_TILES, LANE, S5_STATE)
    bt = lambda b: rows_gp(b.transpose(0, 1, 3, 2))
    wmat, tab8, tab1 = _s5_prep(vec(s5_a_re), vec(s5_a_im), vec(ldt), bt(s5_b_re), bt(s5_b_im),
                                rows_gp(s5_c_re), rows_gp(s5_c_im), s5_d.reshape(DEPTH, 1, S5_WIDTH))

    wuq = _place_heads_bf16(mla_w_uq, MLA_HEADS, MLA_LANE_OF_DIM)
    wuk = _place_heads_bf16(mla_w_uk, MLA_HEADS, MLA_LANE_OF_DIM[:MLA_NOPE])
    wuv = mla_w_uv.astype(BF16)
    qhn = _place_heads(mla_qh_norm, 1, MLA_LANE_OF_DIM)
    khn = _place_heads(mla_kh_norm, 1, MLA_LANE_OF_DIM)
    e_np = np.zeros((LANE, MLA_HEADS * HEAD_PAD), np.float32)
    for h in range(MLA_HEADS):
        for i in range(MLA_ROPE):
            e_np[i, h * HEAD_PAD + MLA_LANE_OF_DIM[MLA_NOPE + i]] = 1.0
    e_place = jnp.asarray(e_np, BF16)
    rope_tab = _rope_tables(dseq)
    ckr_pad = jnp.pad(cache_mla_krope, ((0, 0), (0, 0), (0, 0), (0, LANE - MLA_ROPE)))

    zrow = lambda n: jnp.zeros((DEPTH, n, GLA_QK), F32)
    waf = jnp.concatenate([gla_w_a2[:, 0], zrow(LANE - GLA_RANK)], axis=1).astype(BF16)
    wab = jnp.concatenate([zrow(GLA_RANK), gla_w_a2[:, 1], zrow(LANE - 2 * GLA_RANK)], axis=1).astype(BF16)
    sgla = state_gla.reshape(dbsz, DEPTH, 2, GLA_QK, GLA_DV)
    ss5 = state_s5.reshape(dbsz, DEPTH, 2, 2, S5_NSTATE)
    zero_s5 = jnp.zeros((bsz, 2, 2, S5_NSTATE), F32)

    hp = x_prompt.reshape(bsz * seq, D_MODEL)
    hs = x_sample.reshape(dbsz * dseq, D_MODEL)
    ckv_l, krope_l, gla_l, s5_l = [], [], [], []
    w_in_t = jnp.swapaxes(w_in, 1, 2)
    mod = ada
    mla_w = (mla_q_norm, wuq, mla_kv_norm, wuk, wuv, qhn, khn, e_place)
    wbo = (w_bo_gla, w_bo_mla, w_bo_s5)
    for l in range(DEPTH):
        p_rows, p_blocks, blocks_per_seq = bsz * seq, bsz * seq // ROW_TILE, dseq // ROW_TILE
        mod_idx = lambda i: jnp.where(i < p_blocks, ctx_row, (i - p_blocks) // blocks_per_seq)
        zg, zm, zs = _in_proj(hp, hs, mod, mod_idx, norm_w, w_in_t, l)

        def mixers(x2, row0, nb, n, ctx):
            if ctx:
                gctx = sgla
                x0, x0_blk = ss5, (nb, None, 2, 2, S5_TILE_STATE)
                x0_idx = lambda j: (0, l, 0, 0, j)
                mctx, rt = (cache_mla_ckv, ckr_pad), rope_tab
            else:
                gctx = None
                x0, x0_blk = zero_s5, (nb, 2, 2, S5_TILE_STATE)
                x0_idx = lambda j: (0, 0, 0, j)
                mctx, rt = None, None
            oa, st_gla = _gla(zg, row0, gctx, l, waf, wab, gla_b_a, gla_o_norm, nb, n)
            ob, ckv = _mla(zm, row0, mctx, l, mla_w, rt, nb, n)
            y_ssm, st_s5 = _s5_scan(zs, row0, x0, x0_blk, x0_idx, l, wmat, tab8, tab1, nb, n)
            grp_mod_idx = lambda i: mod_idx(i + row0 // ROW_TILE)
            y = _merge(x2, row0, mod, grp_mod_idx, norm_w, oa, ob, y_ssm, zs, l, s5_w_glu, s5_b_glu, w_in_t, *wbo, w_out)
            return y, ckv, st_gla, st_s5

        hp_next, ckv_p, st_gla_p, st_s5_p = mixers(hp, 0, bsz, seq, False)
        hs = mixers(hs, p_rows, dbsz, dseq, True)[0]
        hp = hp_next
        ckv_l.append(ckv_p.reshape(bsz, seq, MLA_KV_LORA))
        krope_l.append(zm[:p_rows, ZM_KR:ZM_KR + MLA_ROPE].astype(F32).reshape(bsz, seq, MLA_ROPE))
        gla_l.append(st_gla_p.reshape(bsz, 2, GLA_HEADS, GLA_DK, GLA_DV))
        s5_l.append(st_s5_p.reshape(bsz, 2, 2, S5_GROUPS, S5_STATE))

    return (hp.reshape(bsz, seq, D_MODEL), hs.reshape(dbsz, dseq, D_MODEL),
            jnp.stack(ckv_l, axis=1), jnp.stack(krope_l, axis=1),
            jnp.stack(gla_l, axis=1), jnp.stack(s5_l, axis=1))
```

```python
import functools

import jax
import jax.numpy as jnp
import numpy as np
from jax import lax
from jax.experimental import pallas as pl
from jax.experimental.pallas import tpu as pltpu

F32 = jnp.float32
BF16 = jnp.bfloat16

EPS = 1e-6
D_MODEL = 1024
DEPTH = 2
GRID_W = 64
ROPE_THETA = 10000.0
GLA_HEADS = 4
GLA_DK = 64
GLA_DV = 128
GLA_RANK = 16
GLA_GATE_NORM = 16.0
GLA_QK = GLA_HEADS * GLA_DK
GLA_WIDTH = GLA_HEADS * GLA_DV
MLA_HEADS = 4
MLA_Q_LORA = 384
MLA_KV_LORA = 256
MLA_NOPE = 64
MLA_ROPE = 32
MLA_QK = MLA_NOPE + MLA_ROPE
MLA_DV = 128
MLA_WIDTH = MLA_HEADS * MLA_DV
S5_WIDTH = 512
S5_GROUP = 16
S5_GROUPS = 32
S5_STATE = 64
S5_NSTATE = S5_GROUPS * S5_STATE

LANE = 128
SUBLANE = 8
COND_ROWS = SUBLANE
HEAD_PAD = LANE
ROPE_SHIFT = LANE // 2
CHUNK = 64
GLA_STEP = 256
GLA_SEQS_PER_STEP = 2
GLA_ROWS_PER_STEP = 1024
GLA_SKEW = 1
S5_TILES = S5_WIDTH // LANE
S5_TILE_STATE = S5_NSTATE // S5_TILES
ROW_TILE = 512
Q_TILE = 512
PROJ_TILE = 256
MLA_ROWS_PER_STEP = 1024
VMEM_LIMIT = 56 * 1024 * 1024

IN_SPLITS = (GLA_QK, GLA_QK, GLA_WIDTH, GLA_RANK, GLA_RANK, GLA_WIDTH,
             MLA_Q_LORA, MLA_KV_LORA, MLA_ROPE, MLA_WIDTH,
             S5_WIDTH, S5_WIDTH, 3 * D_MODEL)
(IN_GQ, IN_GK, IN_GV, IN_GA, IN_GAB, IN_GG, IN_MQ, IN_MKV, IN_MKR, IN_MG, IN_SU, IN_SG,
 MERGE_COL, D_IN) = (int(c) for c in np.cumsum((0,) + IN_SPLITS))
ZG_Q, ZG_K, ZG_V, ZG_A = 0, GLA_QK, 2 * GLA_QK, 2 * GLA_QK + GLA_WIDTH
ZG_GATE = ZG_A + LANE
ZG_W = ZG_GATE + GLA_WIDTH
ZM_Q, ZM_KV, ZM_KR = 0, MLA_Q_LORA, MLA_Q_LORA + MLA_KV_LORA
ZM_GATE = ZM_KR + LANE
ZM_W = ZM_GATE + MLA_WIDTH
ZS_U, ZS_GATE, ZS_W = 0, S5_WIDTH, 2 * S5_WIDTH
ZG_BASE, ZM_BASE, ZS_BASE, PACK_W = 0, ZG_W, ZG_W + ZM_W, ZG_W + ZM_W + ZS_W
IN_PIECES = (
    (IN_GQ, (IN_GA - IN_GQ) // LANE, ZG_BASE + ZG_Q, None),
    (IN_GA, 1, ZG_BASE + ZG_A, 2 * GLA_RANK),
    (IN_GG, GLA_WIDTH // LANE, ZG_BASE + ZG_GATE, None),
    (IN_MQ, (IN_MKR - IN_MQ) // LANE, ZM_BASE + ZM_Q, None),
    (IN_MKR, 1, ZM_BASE + ZM_KR, MLA_ROPE),
    (IN_MG, MLA_WIDTH // LANE, ZM_BASE + ZM_GATE, None),
    (IN_SU, (MERGE_COL - IN_SU) // LANE, ZS_BASE + ZS_U, None),
)


def _dot(a, b):
    return jnp.dot(a.astype(BF16), b.astype(BF16), preferred_element_type=F32)


def _dot_nt(a, b):
    return lax.dot_general(a.astype(BF16), b.astype(BF16), (((1,), (1,)), ((), ())),
                           preferred_element_type=F32)


def _split_bf16(x, parts):
    out = []
    r = x
    for _ in range(parts):
        p = r.astype(BF16)
        out.append(p)
        r = r - p.astype(F32)
    return out


def _emit_skewed(chains, skew):
    pending, active, tick = list(chains), [], 0
    while pending or active:
        while pending and (skew == 0 or tick % skew == 0):
            active.append(pending.pop(0))
            if skew:
                break
        for gen in list(active):
            if next(gen, "done") == "done":
                active.remove(gen)
        tick += 1


def _vector_spec(width):
    return pl.BlockSpec((DEPTH, width), lambda *_: (0, 0))


def _layer_spec(shape, layer, pipeline_mode=None):
    kwargs = {} if pipeline_mode is None else {"pipeline_mode": pipeline_mode}
    return pl.BlockSpec((None,) + tuple(shape), lambda *_: (layer,) + (0,) * len(shape), **kwargs)


def _params(sem):
    return pltpu.CompilerParams(dimension_semantics=sem, vmem_limit_bytes=VMEM_LIMIT)


def _ada_kernel(c_ref, w_ref, b_ref, o_ref):
    s = jax.nn.silu(c_ref[...])
    o_ref[...] = _dot(s, w_ref[...]) + b_ref[pl.ds(pl.program_id(0), 1), :]


def _ada(cond8, w_ada, b_ada):
    tn = 3 * D_MODEL // 2
    return pl.pallas_call(
        _ada_kernel,
        grid=(DEPTH, 3 * D_MODEL // tn),
        in_specs=[
            pl.BlockSpec((COND_ROWS, D_MODEL), lambda l, n: (0, 0)),
            pl.BlockSpec((None, D_MODEL, tn), lambda l, n: (l, 0, n)),
            pl.BlockSpec((DEPTH, tn), lambda l, n: (0, n)),
        ],
        out_specs=pl.BlockSpec((None, COND_ROWS, tn), lambda l, n: (l, 0, n)),
        out_shape=jax.ShapeDtypeStruct((DEPTH, COND_ROWS, 3 * D_MODEL), F32),
        compiler_params=_params(("parallel", "parallel")),
        name="ada",
    )(cond8, w_ada, b_ada)


def _mod_rmsnorm(x, nw, mod):
    ms = jnp.mean(x * x, axis=-1, keepdims=True)
    y = x * lax.rsqrt(ms + EPS) * nw
    return y * (1.0 + mod[:, D_MODEL:2 * D_MODEL]) + mod[:, 0:D_MODEL]


def _pack_transposed(w_ref, wb_s, src, tiles, dst, keep):
    lane = lax.broadcasted_iota(jnp.int32, (D_MODEL, LANE), 1)
    for t in range(tiles):
        blk = w_ref[src + t * LANE:src + (t + 1) * LANE, :].T
        if keep is not None:
            blk = jnp.where(lane < keep, blk, 0.0)
        wb_s[:, dst + t * LANE:dst + (t + 1) * LANE] = blk.astype(BF16)


def _in_proj_kernel(xp_ref, xs_ref, mod_ref, nw_ref, w_ref, zg_ref, zm_ref, zs_ref, wb_s, *, p_blocks, layer,
                    mod_idx):
    i = pl.program_id(0)

    @pl.when(i == 0)
    def _():
        for src, tiles, dst, keep in IN_PIECES:
            _pack_transposed(w_ref, wb_s, src, tiles, dst, keep)

    x = jnp.where(i < p_blocks, xp_ref[...], xs_ref[...])
    h = _mod_rmsnorm(x, nw_ref[layer:layer + 1, :], mod_ref[pl.ds(mod_idx(i), 1), :]).astype(BF16)
    z = jnp.dot(h, wb_s[...], preferred_element_type=F32)
    zg_ref[...] = z[:, ZG_BASE:ZG_BASE + ZG_W].astype(BF16)
    zm_ref[...] = z[:, ZM_BASE:ZM_BASE + ZM_W].astype(BF16)
    zs_ref[...] = z[:, ZS_BASE:ZS_BASE + ZS_W].astype(BF16)


def _two_group_rows(width, p_blocks):
    tm = ROW_TILE
    return (pl.BlockSpec((tm, width), lambda i: (jnp.minimum(i, p_blocks - 1), 0)),
            pl.BlockSpec((tm, width), lambda i: (jnp.maximum(i - p_blocks, 0), 0)))


def _in_proj(xp, xs, mod, mod_idx, nw, w_in_t, layer):
    tm = ROW_TILE
    p_blocks = xp.shape[0] // tm
    n = xp.shape[0] + xs.shape[0]
    return pl.pallas_call(
        functools.partial(_in_proj_kernel, p_blocks=p_blocks, layer=layer, mod_idx=mod_idx),
        grid=(n // tm,),
        in_specs=[
            *_two_group_rows(D_MODEL, p_blocks),
            _layer_spec((COND_ROWS, 3 * D_MODEL), layer),
            _vector_spec(D_MODEL),
            pl.BlockSpec((None, MERGE_COL, D_MODEL), lambda i: (layer, 0, 0), pipeline_mode=pl.Buffered(1)),
        ],
        out_specs=[
            pl.BlockSpec((tm, ZG_W), lambda i: (i, 0)),
            pl.BlockSpec((tm, ZM_W), lambda i: (i, 0)),
            pl.BlockSpec((tm, ZS_W), lambda i: (i, 0)),
        ],
        out_shape=[
            jax.ShapeDtypeStruct((n, ZG_W), BF16),
            jax.ShapeDtypeStruct((n, ZM_W), BF16),
            jax.ShapeDtypeStruct((n, ZS_W), BF16),
        ],
        scratch_shapes=[pltpu.VMEM((D_MODEL, PACK_W), BF16)],
        compiler_params=_params(("arbitrary",)),
        name="in_proj",
    )(xp, xs, mod, nw, w_in_t)


def _gla_kernel(*refs, nsteps, seq, nseq, has_ctx, layer):
    it = iter(refs)
    zg_ref = next(it)
    s0_ref = next(it) if has_ctx else None
    waf_ref, wab_ref, ba_ref, onorm_ref, o_ref, sfin_ref, la_s, o_s, st_s = (next(it) for _ in range(9))
    chains = [(g, d) for g in range(nseq) for d in (0, 1)]
    inv_norm = 1.0 / GLA_GATE_NORM
    zero_blk = jnp.zeros((GLA_DK, GLA_DV), F32)
    for ch, (g, d) in enumerate(chains):
        if d == 0:
            a_blk = zg_ref[g * seq:(g + 1) * seq, ZG_A:ZG_A + LANE]
        wa_ref = waf_ref if d == 0 else wab_ref
        a_low = _dot(a_blk, wa_ref[...]) + ba_ref[d:d + 1, :]
        la_s[ch] = (jnp.minimum(a_low, 0.0) - jnp.log(1.0 + jnp.exp(-jnp.abs(a_low)))) * inv_norm
        if has_ctx:
            s0 = s0_ref[g, d]
            rows_bd = []
            for h in range(GLA_HEADS):
                sh = s0[h * GLA_DK:(h + 1) * GLA_DK, :]
                rows_bd.append(jnp.concatenate([sh if h2 == h else zero_blk for h2 in range(GLA_HEADS)], axis=1))
            st_s[ch] = jnp.concatenate(rows_bd, axis=0).T
        else:
            st_s[ch] = jnp.zeros((GLA_WIDTH, GLA_QK), F32)

    def iota(shape, axis, shift):
        return lax.shift_right_logical(lax.broadcasted_iota(jnp.int32, shape, axis), shift)

    log_chunk, log_dv = CHUNK.bit_length() - 1, GLA_DV.bit_length() - 1
    row = lax.broadcasted_iota(jnp.int32, (GLA_STEP, GLA_STEP), 0)
    col = lax.broadcasted_iota(jnp.int32, (GLA_STEP, GLA_STEP), 1)
    same_chunk = iota((GLA_STEP, GLA_STEP), 0, log_chunk) == iota((GLA_STEP, GLA_STEP), 1, log_chunk)
    masks = (same_chunk & (row >= col), same_chunk & (row <= col))
    lane_head = iota((GLA_STEP, GLA_QK), 1, log_chunk)
    row_chunk = iota((GLA_STEP, GLA_QK), 0, log_chunk)
    state_blk = iota((GLA_WIDTH, GLA_QK), 0, log_dv) == iota((GLA_WIDTH, GLA_QK), 1, log_chunk)
    qscale = GLA_DK ** -0.5
    nch = GLA_STEP // CHUNK

    def chain_phases(i, ch, g, d):
        r0 = pl.multiple_of((i if d == 0 else nsteps - 1 - i) * GLA_STEP, GLA_STEP)
        zrows, rows = pl.ds(g * seq + r0, GLA_STEP), pl.ds(r0, GLA_STEP)
        a_hi, a_lo = _split_bf16(la_s[ch, rows, :], 2)
        tri = masks[d].astype(BF16)
        cum = (jnp.dot(tri, a_hi, preferred_element_type=F32)
               + jnp.dot(tri, a_lo, preferred_element_type=F32))
        yield
        edge = CHUNK - 1 if d == 0 else 0
        blast = [cum[c * CHUNK + edge:c * CHUNK + edge + 1, :] for c in range(nch)]
        bl = jnp.concatenate([jnp.broadcast_to(b, (CHUNK, GLA_QK)) for b in blast], axis=0)
        q = zg_ref[zrows, ZG_Q:ZG_Q + GLA_QK].astype(F32) * qscale
        k = zg_ref[zrows, ZG_K:ZG_K + GLA_QK].astype(F32)
        v = zg_ref[zrows, ZG_V:ZG_V + GLA_WIDTH]
        v_t = v.astype(F32).T.astype(BF16)
        qd = q * jnp.exp(cum)
        kd = (k * jnp.exp(-cum)).astype(BF16)
        kr = k * jnp.exp(bl - cum)
        yield
        outs = []
        for h in range(GLA_HEADS):
            qh = jnp.where(lane_head == h, qd, 0.0)
            att = _dot_nt(qh, kd)
            yield
            att = jnp.where(masks[d], att, 0.0)
            outs.append(_dot(att, v[:, h * GLA_DV:(h + 1) * GLA_DV]))
            yield
        s = st_s[ch]
        inter = [None] * nch
        for c in (range(nch) if d == 0 else reversed(range(nch))):
            inter[c] = _dot_nt(qd[c * CHUNK:(c + 1) * CHUNK, :], s)
            kv_t = _dot(v_t, jnp.where(row_chunk == c, kr, 0.0))
            yield
            s = s * jnp.exp(blast[c]) + jnp.where(state_blk, kv_t, 0.0)
            yield
        st_s[ch] = s
        o_s[ch, rows, :] = jnp.concatenate(outs, axis=1) + jnp.concatenate(inter, axis=0)

    def step(i, carry):
        _emit_skewed([chain_phases(i, ch, g, d) for ch, (g, d) in enumerate(chains)], GLA_SKEW)
        return carry

    lax.fori_loop(0, nsteps, step, 0)
    onorm = onorm_ref[layer:layer + 1, :]
    for ch, (g, d) in enumerate(chains):
        s_fin = st_s[ch].T
        for h in range(GLA_HEADS):
            sfin_ref[g, d, h * GLA_DK:(h + 1) * GLA_DK, :] = (
                s_fin[h * GLA_DK:(h + 1) * GLA_DK, h * GLA_DV:(h + 1) * GLA_DV])
    for g in range(nseq):
        srows = slice(g * seq, (g + 1) * seq)
        o = o_s[2 * g] + o_s[2 * g + 1]
        gate = zg_ref[srows, ZG_GATE:ZG_GATE + GLA_WIDTH].astype(F32)
        for h in range(GLA_HEADS):
            vs = slice(h * GLA_DV, (h + 1) * GLA_DV)
            oh = o[:, vs]
            ms = jnp.mean(oh * oh, axis=-1, keepdims=True)
            o_ref[srows, vs] = oh * lax.rsqrt(ms + EPS) * onorm * jax.nn.silu(gate[:, vs])


def _gla(zg, row0, ctx, layer, waf, wab, ba, onorm, bsz, seq):
    nseq = max(GLA_SEQS_PER_STEP, GLA_ROWS_PER_STEP // seq)
    blk0 = row0 // (nseq * seq)
    in_specs = [pl.BlockSpec((nseq * seq, ZG_W), lambda b: (b + blk0, 0))]
    args = [zg]
    if ctx is not None:
        in_specs.append(pl.BlockSpec((nseq, None, 2, GLA_QK, GLA_DV), lambda b: (b, layer, 0, 0, 0)))
        args.append(ctx)
    in_specs += [
        _layer_spec((LANE, GLA_QK), layer),
        _layer_spec((LANE, GLA_QK), layer),
        _layer_spec((2, GLA_QK), layer),
        _vector_spec(GLA_DV),
    ]
    return pl.pallas_call(
        functools.partial(_gla_kernel, nsteps=seq // GLA_STEP, seq=seq, nseq=nseq, has_ctx=ctx is not None,
                          layer=layer),
        grid=(bsz // nseq,),
        in_specs=in_specs,
        out_specs=[
            pl.BlockSpec((nseq * seq, GLA_WIDTH), lambda b: (b, 0)),
            pl.BlockSpec((nseq, 2, GLA_QK, GLA_DV), lambda b: (b, 0, 0, 0)),
        ],
        out_shape=[
            jax.ShapeDtypeStruct((bsz * seq, GLA_WIDTH), F32),
            jax.ShapeDtypeStruct((bsz, 2, GLA_QK, GLA_DV), F32),
        ],
        scratch_shapes=[
            pltpu.VMEM((2 * nseq, seq, GLA_QK), F32),
            pltpu.VMEM((2 * nseq, seq, GLA_WIDTH), F32),
            pltpu.VMEM((2 * nseq, GLA_WIDTH, GLA_QK), F32),
        ],
        compiler_params=_params(("parallel",)),
        name="gla",
    )(*args, waf, wab, ba, onorm)


def _rms(x, w):
    ms = jnp.mean(x * x, axis=-1, keepdims=True)
    return x * lax.rsqrt(ms + EPS) * w


def _head_sums_mxu(x):
    width = x.shape[-1]
    shift = HEAD_PAD.bit_length() - 1
    gi = lax.shift_right_logical(lax.broadcasted_iota(jnp.int32, (width, width), 0), shift)
    gj = lax.shift_right_logical(lax.broadcasted_iota(jnp.int32, (width, width), 1), shift)
    return _dot(x * x, jnp.where(gi == gj, 1.0, 0.0))


def _head_norm(x, w, rope, on_mxu):
    sums = _head_sums_mxu(x) if on_mxu else None
    outs = []
    for h in range(MLA_HEADS):
        hs = slice(h * HEAD_PAD, (h + 1) * HEAD_PAD)
        xh = x[:, hs]
        ss = sums[:, hs] if on_mxu else jnp.sum(xh * xh, axis=-1, keepdims=True)
        yh = xh * lax.rsqrt(ss * (1.0 / MLA_QK) + EPS) * w
        if rope is not None:
            c, s = rope
            yh = yh * c + pltpu.roll(yh, ROPE_SHIFT, 1) * s
        outs.append(yh)
    return outs


def _place_rope_key(kr, e):
    return sum(jnp.dot(p, e, preferred_element_type=F32) for p in _split_bf16(kr, 3))


def _mla_kernel(*refs, seq, nseq, n_ctx, use_rope, layer):
    it = iter(refs)
    zm_ref = next(it)
    if n_ctx:
        cckv_ref, ckr_ref = next(it), next(it)
    qn_ref, wuq_ref, kvn_ref, wuk_ref, wuv_ref, qhn_ref, khn_ref, e_ref = (next(it) for _ in range(8))
    rope_ref = next(it) if use_rope else None
    o_ref, ckv_ref = next(it), next(it)
    q_s, k_s, v_s = next(it), next(it), next(it)
    qn, kvn, qhn, khn = (r[layer:layer + 1, :] for r in (qn_ref, kvn_ref, qhn_ref, khn_ref))

    qscale = MLA_QK ** -0.5
    heads = [slice(h * HEAD_PAD, (h + 1) * HEAD_PAD) for h in range(MLA_HEADS)]

    def keys_values(g, ckv, k_rope_placed, rope, k_rows):
        k_raw = _dot(ckv, wuk_ref[...]) + k_rope_placed
        yield
        kh = _head_norm(k_raw, khn, rope, False)
        for h, hs in enumerate(heads):
            k_s[g, k_rows, hs] = kh[h].astype(BF16)
        yield
        v_s[g, k_rows, :] = _dot(ckv, wuv_ref[...]).astype(BF16)
        yield

    def latent_phases(i, g):
        r0 = pl.multiple_of(i * PROJ_TILE, PROJ_TILE)
        tile, rows = pl.ds(r0, PROJ_TILE), pl.ds(g * seq + r0, PROJ_TILE)
        rope = (rope_ref[0, tile, :], rope_ref[1, tile, :]) if use_rope else None
        ckv = _rms(zm_ref[rows, ZM_KV:ZM_KV + MLA_KV_LORA].astype(F32), kvn)
        ckv_ref[rows, :] = ckv
        k_pe = jnp.dot(zm_ref[rows, ZM_KR:ZM_KR + LANE], e_ref[...], preferred_element_type=F32)
        yield from keys_values(g, ckv, k_pe, rope, pl.ds(n_ctx + r0, PROJ_TILE))
        cq = _rms(zm_ref[rows, ZM_Q:ZM_Q + MLA_Q_LORA].astype(F32), qn)
        q_raw = _dot(cq, wuq_ref[...])
        yield
        qh = _head_norm(q_raw, qhn, rope, True)
        for h, hs in enumerate(heads):
            q_s[rows, hs] = (qh[h] * qscale).astype(BF16)

    def latent_tile(i, carry):
        _emit_skewed([latent_phases(i, g) for g in range(nseq)], 0)
        return carry

    lax.fori_loop(0, seq // PROJ_TILE, latent_tile, 0)

    def context_tile(i, carry):
        rows = pl.ds(pl.multiple_of(i * PROJ_TILE, PROJ_TILE), PROJ_TILE)
        _emit_skewed([keys_values(g, cckv_ref[g, rows, :], _place_rope_key(ckr_ref[g, rows, :], e_ref[...]),
                                  None, rows) for g in range(nseq)], 0)
        return carry

    if n_ctx:
        lax.fori_loop(0, n_ctx // PROJ_TILE, context_tile, 0)

    q_tile = min(seq, Q_TILE)

    def head_phases(g, h, hs, rows, gate):
        s = lax.dot_general(q_s[rows, hs], k_s[g, :, hs], (((1,), (1,)), ((), ())),
                            preferred_element_type=F32)
        yield
        e = jnp.exp(s - jnp.max(s, axis=-1, keepdims=True))
        l = jnp.sum(e, axis=-1, keepdims=True)
        p = e.astype(BF16)
        yield
        o = jnp.dot(p, v_s[g, :, hs], preferred_element_type=F32) / l
        o_ref[rows, hs] = o * jax.nn.silu(gate[:, hs])

    def q_block(i, carry):
        chains = []
        for g in range(nseq):
            rows = pl.ds(g * seq + pl.multiple_of(i * q_tile, q_tile), q_tile)
            gate = zm_ref[rows, ZM_GATE:ZM_GATE + MLA_WIDTH].astype(F32)
            chains += [head_phases(g, h, hs, rows, gate) for h, hs in enumerate(heads)]
        _emit_skewed(chains, 0)
        return carry

    lax.fori_loop(0, seq // q_tile, q_block, 0)


def _mla(zm, row0, ctx, layer, w, rope_tab, bsz, seq):
    n_ctx = 0 if ctx is None else ctx[0].shape[-2]
    nseq = max(1, MLA_ROWS_PER_STEP // seq)
    blk0 = row0 // (nseq * seq)
    in_specs = [pl.BlockSpec((nseq * seq, ZM_W), lambda b: (b + blk0, 0))]
    args = [zm]
    if ctx is not None:
        cckv, ckr = ctx
        in_specs += [
            pl.BlockSpec((nseq, None, n_ctx, MLA_KV_LORA), lambda b: (b, layer, 0, 0)),
            pl.BlockSpec((nseq, None, n_ctx, LANE), lambda b: (b, layer, 0, 0)),
        ]
        args += [cckv, ckr]
    in_specs += [
        _vector_spec(MLA_Q_LORA),
        _layer_spec((MLA_Q_LORA, MLA_HEADS * HEAD_PAD), layer),
        _vector_spec(MLA_KV_LORA),
        _layer_spec((MLA_KV_LORA, MLA_HEADS * HEAD_PAD), layer),
        _layer_spec((MLA_KV_LORA, MLA_WIDTH), layer),
        _vector_spec(HEAD_PAD),
        _vector_spec(HEAD_PAD),
        pl.BlockSpec((LANE, MLA_HEADS * HEAD_PAD), lambda b: (0, 0)),
    ]
    args += list(w)
    if rope_tab is not None:
        in_specs.append(pl.BlockSpec((2, seq, HEAD_PAD), lambda b: (0, 0, 0)))
        args.append(rope_tab)
    return pl.pallas_call(
        functools.partial(_mla_kernel, seq=seq, nseq=nseq, n_ctx=n_ctx, use_rope=rope_tab is not None,
                          layer=layer),
        grid=(bsz // nseq,),
        in_specs=in_specs,
        out_specs=[
            pl.BlockSpec((nseq * seq, MLA_WIDTH), lambda b: (b, 0)),
            pl.BlockSpec((nseq * seq, MLA_KV_LORA), lambda b: (b, 0)),
        ],
        out_shape=[
            jax.ShapeDtypeStruct((bsz * seq, MLA_WIDTH), F32),
            jax.ShapeDtypeStruct((bsz * seq, MLA_KV_LORA), F32),
        ],
        scratch_shapes=[
            pltpu.VMEM((nseq * seq, MLA_HEADS * HEAD_PAD), BF16),
            pltpu.VMEM((nseq, n_ctx + seq, MLA_HEADS * HEAD_PAD), BF16),
            pltpu.VMEM((nseq, n_ctx + seq, MLA_WIDTH), BF16),
        ],
        compiler_params=_params(("parallel",)),
        name="mla",
    )(*args)


S5_T = 8
S5_R = CHUNK // S5_T
S5_SUB_CH = 64
S5_SUBS = LANE // S5_SUB_CH
S5_SUB_STATE = S5_TILE_STATE // S5_SUBS
S5_ROW = S5_T * S5_SUB_CH
S5_W = 2 * S5_SUB_STATE
W_M, W_SF, W_SB, W_CF, W_CB = range(5)


def _cmul(ar, ai, br, bi):
    return ar * br - ai * bi, ar * bi + ai * br


def _s5_prep_kernel(are_ref, aim_ref, ldt_ref, bre_ref, bim_ref, cre_ref, cim_ref, d_ref,
                    w_ref, tab8_ref, tab1_ref):
    gr = lax.shift_right_logical(lax.broadcasted_iota(jnp.int32, (S5_SUB_CH, S5_SUB_STATE), 0),
                                 S5_GROUP.bit_length() - 1)
    gc = lax.shift_right_logical(lax.broadcasted_iota(jnp.int32, (S5_SUB_CH, S5_SUB_STATE), 1),
                                 S5_STATE.bit_length() - 1)

    def spread(ref, h):
        x = ref[h * S5_SUB_CH:(h + 1) * S5_SUB_CH, :]
        return jnp.where(gr == gc, jnp.concatenate([x] * (S5_SUB_CH // S5_GROUP), axis=1), 0.0)

    row = lax.broadcasted_iota(jnp.int32, (S5_SUB_CH, S5_SUB_CH), 0)
    col = lax.broadcasted_iota(jnp.int32, (S5_SUB_CH, S5_SUB_CH), 1)
    taps = [[[], []] for _ in range(S5_SUBS)]
    for d in (0, 1):
        prow = pl.ds(2 * pl.program_id(0) + d, 1)
        a_re, a_im = are_ref[prow, :], aim_ref[prow, :]
        dt = jnp.exp(ldt_ref[prow, :])
        lam = a_re * dt
        th = a_im * dt
        mag = jnp.exp(lam)
        ab_re = mag * jnp.cos(th)
        ab_im = mag * jnp.sin(th)
        den = a_re * a_re + a_im * a_im
        n_re = ab_re - 1.0
        cf_re = (n_re * a_re + ab_im * a_im) / den
        cf_im = (ab_im * a_re - n_re * a_im) / den
        k = lax.broadcasted_iota(jnp.int32, (2 * S5_T, S5_TILE_STATE), 0).astype(F32)
        pmag = jnp.exp(k * lam)
        pw_re = pmag * jnp.cos(k * th)
        pw_im = pmag * jnp.sin(k * th)
        for h in range(S5_SUBS):
            ss = slice(h * S5_SUB_STATE, (h + 1) * S5_SUB_STATE)
            c_re, c_im = spread(cre_ref, h), spread(cim_ref, h)
            c_cat = jnp.concatenate([c_re, c_im], axis=1).astype(BF16)
            bp_re, bp_im = _cmul(spread(bre_ref, h), spread(bim_ref, h), cf_re[:, ss], cf_im[:, ss])
            for p in range(S5_T + 1):
                ar, ai = pw_re[p:p + 1, ss], pw_im[p:p + 1, ss]
                t_in = S5_T - 1 - p if d == 0 else p
                t_out = p - 1 if d == 0 else S5_T - p
                if p < S5_T:
                    l_re, l_im = _cmul(bp_re, bp_im, ar, ai)
                    w_ref[h, W_SF + d, t_in * S5_SUB_CH:(t_in + 1) * S5_SUB_CH, :] = (
                        jnp.concatenate([l_re, l_im], axis=1).astype(BF16))
                    taps[h][d].append(_dot_nt(jnp.concatenate([l_re, -l_im], axis=1), c_cat))
                if p > 0:
                    v_re, v_im = _cmul(c_re, c_im, ar, ai)
                    w_ref[h, W_CF + d, t_out * S5_SUB_CH:(t_out + 1) * S5_SUB_CH, :] = (
                        jnp.concatenate([v_re, -v_im], axis=1).astype(BF16))
        r = lax.broadcasted_iota(jnp.int32, (S5_R, S5_TILE_STATE), 0).astype(F32) * float(S5_T)
        r1 = float(S5_T * (S5_R - 1)) - r
        pm = jnp.exp(r * lam)
        qm = jnp.exp(r1 * lam)
        tab8_ref[d, 0] = pm * jnp.cos(r * th)
        tab8_ref[d, 1] = pm * jnp.sin(r * th)
        tab8_ref[d, 2] = qm * jnp.cos(r1 * th)
        tab8_ref[d, 3] = qm * jnp.sin(r1 * th)
        mc = jnp.exp(float(CHUNK) * lam)
        tab1_ref[d, 0:1, :] = mc * jnp.cos(float(CHUNK) * th)
        tab1_ref[d, 1:2, :] = mc * jnp.sin(float(CHUNK) * th)
    for h in range(S5_SUBS):
        skip = jnp.where(row == col, d_ref[:, h * S5_SUB_CH:(h + 1) * S5_SUB_CH], 0.0)
        for t in range(S5_T):
            blocks = []
            for t2 in range(S5_T):
                if t < t2:
                    blocks.append(taps[h][0][t2 - t])
                elif t > t2:
                    blocks.append(taps[h][1][t - t2])
                else:
                    blocks.append(taps[h][0][0] + taps[h][1][0] + skip)
            w_ref[h, W_M, t * S5_SUB_CH:(t + 1) * S5_SUB_CH, :] = jnp.concatenate(blocks, axis=1).astype(BF16)


def _s5_prep(a_re, a_im, ldt, b_re, b_im, c_re, c_im, dsk):
    vec = pl.BlockSpec((2 * DEPTH, S5_TILE_STATE), lambda l, j: (0, j))
    blk = pl.BlockSpec((None, None, LANE, S5_STATE), lambda l, j: (l, j, 0, 0))
    return pl.pallas_call(
        _s5_prep_kernel,
        grid=(DEPTH, S5_TILES),
        in_specs=[vec, vec, vec, blk, blk, blk, blk,
                  pl.BlockSpec((None, 1, LANE), lambda l, j: (l, 0, j))],
        out_specs=[
            pl.BlockSpec((None, None, S5_SUBS, 5, S5_ROW, S5_W), lambda l, j: (l, j, 0, 0, 0, 0)),
            pl.BlockSpec((None, None, 2, 4, S5_R, S5_TILE_STATE), lambda l, j: (l, j, 0, 0, 0, 0)),
            pl.BlockSpec((None, None, 2, 2, S5_TILE_STATE), lambda l, j: (l, j, 0, 0, 0)),
        ],
        out_shape=[
            jax.ShapeDtypeStruct((DEPTH, S5_TILES, S5_SUBS, 5, S5_ROW, S5_W), BF16),
            jax.ShapeDtypeStruct((DEPTH, S5_TILES, 2, 4, S5_R, S5_TILE_STATE), F32),
            jax.ShapeDtypeStruct((DEPTH, S5_TILES, 2, 2, S5_TILE_STATE), F32),
        ],
        compiler_params=_params(("parallel", "parallel")),
        name="s5_prep",
    )(a_re, a_im, ldt, b_re, b_im, c_re, c_im, dsk)


def _s5_scan_kernel(u_ref, x0_ref, w_ref, tab8_ref, tab1_ref, y_ref, fs_ref, u_s, *, nseq, nb):
    groups = nseq * nb
    nrow = groups * S5_R
    ts = S5_SUB_STATE
    u_s[...] = u_ref[...].astype(F32)
    tokens = [u_s[pl.ds(t, nrow, stride=S5_T), :] for t in range(S5_T)]
    rowi = lax.broadcasted_iota(jnp.int32, (groups, S5_R, ts), 1)

    def shift(x, n, down):
        if down:
            return jnp.where(rowi >= n, pltpu.roll(x, n, 1), 0.0)
        return jnp.where(rowi < S5_R - n, pltpu.roll(x, S5_R - n, 1), 0.0)

    def block_scan(e_re, e_im, p_re, p_im, down):
        for n in (1, 2, 4):
            s_re, s_im = _cmul(p_re[n:n + 1, :], p_im[n:n + 1, :], shift(e_re, n, down), shift(e_im, n, down))
            e_re, e_im = e_re + s_re, e_im + s_im
        return e_re, e_im

    y_sub = []
    for h in range(S5_SUBS):
        ch = slice(h * S5_SUB_CH, (h + 1) * S5_SUB_CH)
        ss = slice(h * ts, (h + 1) * ts)
        u8 = jnp.concatenate([tok[:, ch] for tok in tokens], axis=1).astype(BF16)
        ef = jnp.dot(u8, w_ref[h, W_SF], preferred_element_type=F32).reshape(groups, S5_R, S5_W)
        eb = jnp.dot(u8, w_ref[h, W_SB], preferred_element_type=F32).reshape(groups, S5_R, S5_W)

        p_re, p_im, q_re, q_im = (tab8_ref[0, i, :, ss] for i in range(4))
        a_re, a_im = tab1_ref[0, 0:1, ss], tab1_ref[0, 1:2, ss]
        cs_re, cs_im = block_scan(ef[:, :, :ts], ef[:, :, ts:], p_re, p_im, True)
        st_re, st_im = [], []
        for s in range(nseq):
            x_re, x_im = x0_ref[s, 0, 0:1, ss], x0_ref[s, 0, 1:2, ss]
            for b in range(nb):
                g = s * nb + b
                st_re.append(x_re)
                st_im.append(x_im)
                k_re, k_im = _cmul(a_re, a_im, x_re, x_im)
                x_re = k_re + cs_re[g, S5_R - 1:S5_R, :]
                x_im = k_im + cs_im[g, S5_R - 1:S5_R, :]
            fs_ref[s, 0, 0:1, ss] = x_re
            fs_ref[s, 0, 1:2, ss] = x_im
        k_re, k_im = _cmul(p_re, p_im, jnp.stack(st_re), jnp.stack(st_im))
        xin = jnp.concatenate([shift(cs_re, 1, True) + k_re, shift(cs_im, 1, True) + k_im],
                              axis=2).reshape(nrow, S5_W)

        p_re, p_im, q_re, q_im = (tab8_ref[1, i, :, ss] for i in range(4))
        a_re, a_im = tab1_ref[1, 0:1, ss], tab1_ref[1, 1:2, ss]
        sf_re, sf_im = block_scan(eb[:, :, :ts], eb[:, :, ts:], p_re, p_im, False)
        z_re, z_im = [None] * groups, [None] * groups
        for s in range(nseq):
            x_re, x_im = x0_ref[s, 1, 0:1, ss], x0_ref[s, 1, 1:2, ss]
            for b in reversed(range(nb)):
                g = s * nb + b
                z_re[g], z_im[g] = x_re, x_im
                k_re, k_im = _cmul(a_re, a_im, x_re, x_im)
                x_re = sf_re[g, 0:1, :] + k_re
                x_im = sf_im[g, 0:1, :] + k_im
            fs_ref[s, 1, 0:1, ss] = x_re
            fs_ref[s, 1, 1:2, ss] = x_im
        k_re, k_im = _cmul(q_re, q_im, jnp.stack(z_re), jnp.stack(z_im))
        xnx = jnp.concatenate([shift(sf_re, 1, False) + k_re, shift(sf_im, 1, False) + k_im],
                              axis=2).reshape(nrow, S5_W)

        y_sub.append(jnp.dot(u8, w_ref[h, W_M], preferred_element_type=F32)
                     + _dot_nt(xin, w_ref[h, W_CF]) + _dot_nt(xnx, w_ref[h, W_CB]))
    for t in range(S5_T):
        tc = slice(t * S5_SUB_CH, (t + 1) * S5_SUB_CH)
        y_ref[pl.ds(t, nrow, stride=S5_T), :] = jnp.concatenate([y[:, tc] for y in y_sub], axis=1)


def _s5_scan(zs, row0, x0, x0_block, x0_idx, layer, wmat, tab8, tab1, nseq, seq):
    n = nseq * seq
    rblk = row0 // n
    return pl.pallas_call(
        functools.partial(_s5_scan_kernel, nseq=nseq, nb=seq // CHUNK),
        grid=(S5_TILES,),
        in_specs=[
            pl.BlockSpec((n, LANE), lambda j: (rblk, ZS_U // LANE + j)),
            pl.BlockSpec(x0_block, x0_idx),
            pl.BlockSpec((None, None, S5_SUBS, 5, S5_ROW, S5_W), lambda j: (layer, j, 0, 0, 0, 0)),
            pl.BlockSpec((None, None, 2, 4, S5_R, S5_TILE_STATE), lambda j: (layer, j, 0, 0, 0, 0)),
            pl.BlockSpec((None, None, 2, 2, S5_TILE_STATE), lambda j: (layer, j, 0, 0, 0)),
        ],
        out_specs=[
            pl.BlockSpec((n, LANE), lambda j: (0, j)),
            pl.BlockSpec((nseq, 2, 2, S5_TILE_STATE), lambda j: (0, 0, 0, j)),
        ],
        out_shape=[
            jax.ShapeDtypeStruct((n, S5_WIDTH), F32),
            jax.ShapeDtypeStruct((nseq, 2, 2, S5_NSTATE), F32),
        ],
        scratch_shapes=[pltpu.VMEM((n, LANE), F32)],
        compiler_params=_params(("parallel",)),
        name="s5_scan",
    )(zs, x0, wmat, tab8, tab1)


def _merge_kernel(x_ref, mod_ref, nw_ref, oa_ref, ob_ref, ys_ref, sg_ref, wglu_ref, bglu_ref, wmg_ref,
                  wa_ref, wb_ref, wc_ref, wout_ref, y_ref, wmg_s, *, layer, mod_idx):
    @pl.when(pl.program_id(0) == 0)
    def _():
        _pack_transposed(wmg_ref.at[0], wmg_s, 0, 3 * D_MODEL // LANE, 0, None)

    x = x_ref[...]
    mod = mod_ref[pl.ds(mod_idx(pl.program_id(0)), 1), :]
    h = _mod_rmsnorm(x, nw_ref[layer:layer + 1, :], mod).astype(BF16)
    zg = _dot(jax.nn.gelu(ys_ref[...]), wglu_ref[...]) + bglu_ref[layer:layer + 1, :]
    oc = (zg[:, :S5_WIDTH] * jax.nn.sigmoid(zg[:, S5_WIDTH:])
          * jax.nn.silu(sg_ref[...].astype(F32)))
    mixed = None
    for br, (o_br, w_ref) in enumerate(((oa_ref[...], wa_ref), (ob_ref[...], wb_ref), (oc, wc_ref))):
        gate = jax.nn.sigmoid(jnp.dot(h, wmg_s[:, br * D_MODEL:(br + 1) * D_MODEL], preferred_element_type=F32))
        term = gate * _dot(o_br, w_ref[...])
        mixed = term if mixed is None else mixed + term
    y_ref[...] = x + mod[:, 2 * D_MODEL:] * _dot(mixed, wout_ref[...])


def _merge(x2, row0, mod, mod_idx, nw, oa, ob, ys, zs, layer, wglu, bglu, w_in_t, wa, wb, wc, wout):
    n = x2.shape[0]
    tm = ROW_TILE
    blk0 = row0 // tm
    once = pl.Buffered(1)
    rows = lambda w: pl.BlockSpec((tm, w), lambda i: (i, 0))
    return pl.pallas_call(
        functools.partial(_merge_kernel, layer=layer, mod_idx=mod_idx),
        grid=(n // tm,),
        in_specs=[
            rows(D_MODEL),
            _layer_spec((COND_ROWS, 3 * D_MODEL), layer),
            _vector_spec(D_MODEL),
            rows(GLA_WIDTH), rows(MLA_WIDTH), rows(S5_WIDTH),
            pl.BlockSpec((tm, S5_WIDTH), lambda i: (i + blk0, ZS_GATE // S5_WIDTH)),
            pl.BlockSpec((None, S5_WIDTH, 2 * S5_WIDTH), lambda i: (layer, 0, 0), pipeline_mode=once),
            _vector_spec(2 * S5_WIDTH),
            pl.BlockSpec((pl.Element(1), pl.Element(3 * D_MODEL), pl.Element(D_MODEL)),
                         lambda i: (layer, MERGE_COL, 0), pipeline_mode=once),
            _layer_spec((GLA_WIDTH, D_MODEL), layer, once),
            _layer_spec((MLA_WIDTH, D_MODEL), layer, once),
            _layer_spec((S5_WIDTH, D_MODEL), layer, once),
            _layer_spec((D_MODEL, D_MODEL), layer, once),
        ],
        out_specs=rows(D_MODEL),
        out_shape=jax.ShapeDtypeStruct((n, D_MODEL), F32),
        scratch_shapes=[pltpu.VMEM((D_MODEL, 3 * D_MODEL), BF16)],
        compiler_params=_params(("arbitrary",)),
        name="merge",
    )(x2, mod, nw, oa, ob, ys, zs, wglu, bglu, w_in_t, wa, wb, wc, wout)


def _mla_lane_of_dim():
    half = MLA_ROPE // 2
    first_gap = ROPE_SHIFT - half
    lane = np.zeros(MLA_QK, np.int32)
    for j in range(MLA_NOPE):
        lane[j] = half + j if j < first_gap else 2 * half + j
    for r in range(half):
        lane[MLA_NOPE + r] = r
        lane[MLA_NOPE + half + r] = ROPE_SHIFT + r
    return lane


MLA_LANE_OF_DIM = _mla_lane_of_dim()


def _place_heads(w, heads, lane_of_dim):
    width = len(lane_of_dim)
    order = np.argsort(lane_of_dim)
    zeros = lambda n: jnp.zeros(w.shape[:-1] + (n,), w.dtype)
    pieces = []
    for h in range(heads):
        lane, i = 0, 0
        while i < width:
            j = i
            while (j + 1 < width and order[j + 1] == order[j] + 1
                   and lane_of_dim[order[j + 1]] == lane_of_dim[order[j]] + 1):
                j += 1
            dst = int(lane_of_dim[order[i]])
            if dst > lane:
                pieces.append(zeros(dst - lane))
            pieces.append(w[..., h * width + int(order[i]):h * width + int(order[j]) + 1])
            lane, i = dst + (j - i + 1), j + 1
        if lane < HEAD_PAD:
            pieces.append(zeros(HEAD_PAD - lane))
    return jnp.concatenate(pieces, axis=-1)


def _place_heads_bf16(w, heads, lane_of_dim):
    width = len(lane_of_dim)
    place = np.zeros((heads * width, heads * HEAD_PAD), np.float32)
    for h in range(heads):
        place[h * width + np.arange(width), h * HEAD_PAD + lane_of_dim] = 1.0
    return jnp.dot(w.astype(BF16), jnp.asarray(place, BF16), preferred_element_type=BF16)


def _rope_tables(n_tok):
    rows = n_tok // GRID_W
    r = jnp.repeat(jnp.arange(rows, dtype=F32), GRID_W)
    col = jnp.tile(jnp.arange(GRID_W, dtype=F32), rows)
    n_freq = MLA_ROPE // 4
    inv = ROPE_THETA ** (-jnp.arange(n_freq, dtype=F32) / n_freq)
    ang = jnp.concatenate([r[:, None] * inv, col[:, None] * inv], axis=-1)
    cos, sin = jnp.cos(ang), jnp.sin(ang)
    ones = jnp.ones((n_tok, MLA_NOPE), F32)
    c = _place_heads(jnp.concatenate([ones, cos, cos], axis=1), 1, MLA_LANE_OF_DIM)
    s = _place_heads(jnp.concatenate([0.0 * ones, -sin, sin], axis=1), 1, MLA_LANE_OF_DIM)
    return jnp.stack([c, s])


def kernel(x_prompt, x_sample, c, c_ctx, cache_mla_ckv, cache_mla_krope, state_gla, state_s5,
           norm_w, w_ada, b_ada, w_in, gla_w_a2, gla_b_a, gla_o_norm,
           mla_q_norm, mla_w_uq, mla_kv_norm, mla_w_uk, mla_w_uv, mla_qh_norm, mla_kh_norm,
           s5_a_re, s5_a_im, s5_log_dt, s5_b_re, s5_b_im, s5_c_re, s5_c_im, s5_d, s5_w_glu, s5_b_glu,
           w_bo_gla, w_bo_mla, w_bo_s5, w_out):
    bsz, seq, _ = x_prompt.shape
    dbsz, dseq, _ = x_sample.shape
    ctx_row = COND_ROWS - 1
    assert dbsz <= ctx_row and (bsz * seq) % ROW_TILE == 0 and dseq % ROW_TILE == 0

    cond = jnp.zeros((COND_ROWS, D_MODEL), F32).at[0:dbsz].set(c).at[ctx_row].set(c_ctx)
    ada = _ada(cond, w_ada, b_ada)

    vec = lambda a: a.reshape(2 * DEPTH, S5_NSTATE)
    ldt = jnp.repeat(s5_log_dt[..., None], S5_STATE, axis=-1)
    rows_gp = lambda t: t.reshape(DEPTH, S5_TILES, LANE, S5_STATE)
    bt = lambda b: rows_gp(b.transpose(0, 1, 3, 2))
    wmat, tab8, tab1 = _s5_prep(vec(s5_a_re), vec(s5_a_im), vec(ldt), bt(s5_b_re), bt(s5_b_im),
                                rows_gp(s5_c_re), rows_gp(s5_c_im), s5_d.reshape(DEPTH, 1, S5_WIDTH))

    wuq = _place_heads_bf16(mla_w_uq, MLA_HEADS, MLA_LANE_OF_DIM)
    wuk = _place_heads_bf16(mla_w_uk, MLA_HEADS, MLA_LANE_OF_DIM[:MLA_NOPE])
    wuv = mla_w_uv.astype(BF16)
    qhn = _place_heads(mla_qh_norm, 1, MLA_LANE_OF_DIM)
    khn = _place_heads(mla_kh_norm, 1, MLA_LANE_OF_DIM)
    e_np = np.zeros((LANE, MLA_HEADS * HEAD_PAD), np.float32)
    for h in range(MLA_HEADS):
        for i in range(MLA_ROPE):
            e_np[i, h * HEAD_PAD + MLA_LANE_OF_DIM[MLA_NOPE + i]] = 1.0
    e_place = jnp.asarray(e_np, BF16)
    rope_tab = _rope_tables(dseq)
    ckr_pad = jnp.pad(cache_mla_krope, ((0, 0), (0, 0), (0, 0), (0, LANE - MLA_ROPE)))

    zrow = lambda n: jnp.zeros((DEPTH, n, GLA_QK), F32)
    waf = jnp.concatenate([gla_w_a2[:, 0], zrow(LANE - GLA_RANK)], axis=1).astype(BF16)
    wab = jnp.concatenate([zrow(GLA_RANK), gla_w_a2[:, 1], zrow(LANE - 2 * GLA_RANK)], axis=1).astype(BF16)
    sgla = state_gla.reshape(dbsz, DEPTH, 2, GLA_QK, GLA_DV)
    ss5 = state_s5.reshape(dbsz, DEPTH, 2, 2, S5_NSTATE)
    zero_s5 = jnp.zeros((bsz, 2, 2, S5_NSTATE), F32)

    hp = x_prompt.reshape(bsz * seq, D_MODEL)
    hs = x_sample.reshape(dbsz * dseq, D_MODEL)
    ckv_l, krope_l, gla_l, s5_l = [], [], [], []
    w_in_t = jnp.swapaxes(w_in, 1, 2)
    mod = ada
    mla_w = (mla_q_norm, wuq, mla_kv_norm, wuk, wuv, qhn, khn, e_place)
    wbo = (w_bo_gla, w_bo_mla, w_bo_s5)
    for l in range(DEPTH):
        p_rows, p_blocks, blocks_per_seq = bsz * seq, bsz * seq // ROW_TILE, dseq // ROW_TILE
        mod_idx = lambda i: jnp.where(i < p_blocks, ctx_row, (i - p_blocks) // blocks_per_seq)
        zg, zm, zs = _in_proj(hp, hs, mod, mod_idx, norm_w, w_in_t, l)

        def mixers(x2, row0, nb, n, ctx):
            if ctx:
                gctx = sgla
                x0, x0_blk = ss5, (nb, None, 2, 2, S5_TILE_STATE)
                x0_idx = lambda j: (0, l, 0, 0, j)
                mctx, rt = (cache_mla_ckv, ckr_pad), rope_tab
            else:
                gctx = None
                x0, x0_blk = zero_s5, (nb, 2, 2, S5_TILE_STATE)
                x0_idx = lambda j: (0, 0, 0, j)
                mctx, rt = None, None
            oa, st_gla = _gla(zg, row0, gctx, l, waf, wab, gla_b_a, gla_o_norm, nb, n)
            ob, ckv = _mla(zm, row0, mctx, l, mla_w, rt, nb, n)
            y_ssm, st_s5 = _s5_scan(zs, row0, x0, x0_blk, x0_idx, l, wmat, tab8, tab1, nb, n)
            grp_mod_idx = lambda i: mod_idx(i + row0 // ROW_TILE)
            y = _merge(x2, row0, mod, grp_mod_idx, norm_w, oa, ob, y_ssm, zs, l, s5_w_glu, s5_b_glu, w_in_t, *wbo, w_out)
            return y, ckv, st_gla, st_s5

        hp_next, ckv_p, st_gla_p, st_s5_p = mixers(hp, 0, bsz, seq, False)
        hs = mixers(hs, p_rows, dbsz, dseq, True)[0]
        hp = hp_next
        ckv_l.append(ckv_p.reshape(bsz, seq, MLA_KV_LORA))
        krope_l.append(zm[:p_rows, ZM_KR:ZM_KR + MLA_ROPE].astype(F32).reshape(bsz, seq, MLA_ROPE))
        gla_l.append(st_gla_p.reshape(bsz, 2, GLA_HEADS, GLA_DK, GLA_DV))
        s5_l.append(st_s5_p.reshape(bsz, 2, 2, S5_GROUPS, S5_STATE))

    return (hp.reshape(bsz, seq, D_MODEL), hs.reshape(dbsz, dseq, D_MODEL),
            jnp.stack(ckv_l, axis=1), jnp.stack(krope_l, axis=1),
            jnp.stack(gla_l, axis=1), jnp.stack(s5_l, axis=1))
```

```python
import functools

import jax
import jax.numpy as jnp
import numpy as np
from jax import lax
from jax.experimental import pallas as pl
from jax.experimental.pallas import tpu as pltpu

F32 = jnp.float32
BF16 = jnp.bfloat16

EPS = 1e-6
D_MODEL = 1024
DEPTH = 2
GRID_W = 64
ROPE_THETA = 10000.0
GLA_HEADS = 4
GLA_DK = 64
GLA_DV = 128
GLA_RANK = 16
GLA_GATE_NORM = 16.0
GLA_QK = GLA_HEADS * GLA_DK
GLA_WIDTH = GLA_HEADS * GLA_DV
MLA_HEADS = 4
MLA_Q_LORA = 384
MLA_KV_LORA = 256
MLA_NOPE = 64
MLA_ROPE = 32
MLA_QK = MLA_NOPE + MLA_ROPE
MLA_DV = 128
MLA_WIDTH = MLA_HEADS * MLA_DV
S5_WIDTH = 512
S5_GROUP = 16
S5_GROUPS = 32
S5_STATE = 64
S5_NSTATE = S5_GROUPS * S5_STATE

LANE = 128
SUBLANE = 8
COND_ROWS = SUBLANE
HEAD_PAD = LANE
ROPE_SHIFT = LANE // 2
CHUNK = 64
GLA_STEP = 256
GLA_SEQS_PER_STEP = 2
GLA_ROWS_PER_STEP = 1024
GLA_SKEW = 1
S5_TILES = S5_WIDTH // LANE
S5_TILE_STATE = S5_NSTATE // S5_TILES
ROW_TILE = 512
Q_TILE = 512
PROJ_TILE = 256
MLA_ROWS_PER_STEP = 1024
VMEM_LIMIT = 56 * 1024 * 1024

IN_SPLITS = (GLA_QK, GLA_QK, GLA_WIDTH, GLA_RANK, GLA_RANK, GLA_WIDTH,
             MLA_Q_LORA, MLA_KV_LORA, MLA_ROPE, MLA_WIDTH,
             S5_WIDTH, S5_WIDTH, 3 * D_MODEL)
(IN_GQ, IN_GK, IN_GV, IN_GA, IN_GAB, IN_GG, IN_MQ, IN_MKV, IN_MKR, IN_MG, IN_SU, IN_SG,
 MERGE_COL, D_IN) = (int(c) for c in np.cumsum((0,) + IN_SPLITS))
ZG_Q, ZG_K, ZG_V, ZG_A = 0, GLA_QK, 2 * GLA_QK, 2 * GLA_QK + GLA_WIDTH
ZG_GATE = ZG_A + LANE
ZG_W = ZG_GATE + GLA_WIDTH
ZM_Q, ZM_KV, ZM_KR = 0, MLA_Q_LORA, MLA_Q_LORA + MLA_KV_LORA
ZM_GATE = ZM_KR + LANE
ZM_W = ZM_GATE + MLA_WIDTH
ZS_U, ZS_GATE, ZS_W = 0, S5_WIDTH, 2 * S5_WIDTH
ZG_BASE, ZM_BASE, ZS_BASE, PACK_W = 0, ZG_W, ZG_W + ZM_W, ZG_W + ZM_W + ZS_W
IN_PIECES = (
    (IN_GQ, (IN_GA - IN_GQ) // LANE, ZG_BASE + ZG_Q, None),
    (IN_GA, 1, ZG_BASE + ZG_A, 2 * GLA_RANK),
    (IN_GG, GLA_WIDTH // LANE, ZG_BASE + ZG_GATE, None),
    (IN_MQ, (IN_MKR - IN_MQ) // LANE, ZM_BASE + ZM_Q, None),
    (IN_MKR, 1, ZM_BASE + ZM_KR, MLA_ROPE),
    (IN_MG, MLA_WIDTH // LANE, ZM_BASE + ZM_GATE, None),
    (IN_SU, (MERGE_COL - IN_SU) // LANE, ZS_BASE + ZS_U, None),
)


def _dot(a, b):
    return jnp.dot(a.astype(BF16), b.astype(BF16), preferred_element_type=F32)


def _dot_nt(a, b):
    return lax.dot_general(a.astype(BF16), b.astype(BF16), (((1,), (1,)), ((), ())),
                           preferred_element_type=F32)


def _split_bf16(x, parts):
    out = []
    r = x
    for _ in range(parts):
        p = r.astype(BF16)
        out.append(p)
        r = r - p.astype(F32)
    return out


def _emit_skewed(chains, skew):
    pending, active, tick = list(chains), [], 0
    while pending or active:
        while pending and (skew == 0 or tick % skew == 0):
            active.append(pending.pop(0))
            if skew:
                break
        for gen in list(active):
            if next(gen, "done") == "done":
                active.remove(gen)
        tick += 1


def _vector_spec(width):
    return pl.BlockSpec((DEPTH, width), lambda *_: (0, 0))


def _layer_spec(shape, layer, pipeline_mode=None):
    kwargs = {} if pipeline_mode is None else {"pipeline_mode": pipeline_mode}
    return pl.BlockSpec((None,) + tuple(shape), lambda *_: (layer,) + (0,) * len(shape), **kwargs)


def _params(sem):
    return pltpu.CompilerParams(dimension_semantics=sem, vmem_limit_bytes=VMEM_LIMIT)


def _ada_kernel(c_ref, w_ref, b_ref, o_ref):
    s = jax.nn.silu(c_ref[...])
    o_ref[...] = _dot(s, w_ref[...]) + b_ref[pl.ds(pl.program_id(0), 1), :]


def _ada(cond8, w_ada, b_ada):
    tn = 3 * D_MODEL // 2
    return pl.pallas_call(
        _ada_kernel,
        grid=(DEPTH, 3 * D_MODEL // tn),
        in_specs=[
            pl.BlockSpec((COND_ROWS, D_MODEL), lambda l, n: (0, 0)),
            pl.BlockSpec((None, D_MODEL, tn), lambda l, n: (l, 0, n)),
            pl.BlockSpec((DEPTH, tn), lambda l, n: (0, n)),
        ],
        out_specs=pl.BlockSpec((None, COND_ROWS, tn), lambda l, n: (l, 0, n)),
        out_shape=jax.ShapeDtypeStruct((DEPTH, COND_ROWS, 3 * D_MODEL), F32),
        compiler_params=_params(("parallel", "parallel")),
        name="ada",
    )(cond8, w_ada, b_ada)


def _mod_rmsnorm(x, nw, mod):
    ms = jnp.mean(x * x, axis=-1, keepdims=True)
    y = x * lax.rsqrt(ms + EPS) * nw
    return y * (1.0 + mod[:, D_MODEL:2 * D_MODEL]) + mod[:, 0:D_MODEL]


def _pack_transposed(w_ref, wb_s, src, tiles, dst, keep):
    lane = lax.broadcasted_iota(jnp.int32, (D_MODEL, LANE), 1)
    for t in range(tiles):
        blk = w_ref[src + t * LANE:src + (t + 1) * LANE, :].T
        if keep is not None:
            blk = jnp.where(lane < keep, blk, 0.0)
        wb_s[:, dst + t * LANE:dst + (t + 1) * LANE] = blk.astype(BF16)


def _in_proj_kernel(xp_ref, xs_ref, mod_ref, nw_ref, w_ref, zg_ref, zm_ref, zs_ref, wb_s, *, p_blocks, layer,
                    mod_idx):
    i = pl.program_id(0)

    @pl.when(i == 0)
    def _():
        for src, tiles, dst, keep in IN_PIECES:
            _pack_transposed(w_ref, wb_s, src, tiles, dst, keep)

    x = jnp.where(i < p_blocks, xp_ref[...], xs_ref[...])
    h = _mod_rmsnorm(x, nw_ref[layer:layer + 1, :], mod_ref[pl.ds(mod_idx(i), 1), :]).astype(BF16)
    z = jnp.dot(h, wb_s[...], preferred_element_type=F32)
    zg_ref[...] = z[:, ZG_BASE:ZG_BASE + ZG_W].astype(BF16)
    zm_ref[...] = z[:, ZM_BASE:ZM_BASE + ZM_W].astype(BF16)
    zs_ref[...] = z[:, ZS_BASE:ZS_BASE + ZS_W].astype(BF16)


def _two_group_rows(width, p_blocks):
    tm = ROW_TILE
    return (pl.BlockSpec((tm, width), lambda i: (jnp.minimum(i, p_blocks - 1), 0)),
            pl.BlockSpec((tm, width), lambda i: (jnp.maximum(i - p_blocks, 0), 0)))


def _in_proj(xp, xs, mod, mod_idx, nw, w_in_t, layer):
    tm = ROW_TILE
    p_blocks = xp.shape[0] // tm
    n = xp.shape[0] + xs.shape[0]
    return pl.pallas_call(
        functools.partial(_in_proj_kernel, p_blocks=p_blocks, layer=layer, mod_idx=mod_idx),
        grid=(n // tm,),
        in_specs=[
            *_two_group_rows(D_MODEL, p_blocks),
            _layer_spec((COND_ROWS, 3 * D_MODEL), layer),
            _vector_spec(D_MODEL),
            pl.BlockSpec((None, MERGE_COL, D_MODEL), lambda i: (layer, 0, 0), pipeline_mode=pl.Buffered(1)),
        ],
        out_specs=[
            pl.BlockSpec((tm, ZG_W), lambda i: (i, 0)),
            pl.BlockSpec((tm, ZM_W), lambda i: (i, 0)),
            pl.BlockSpec((tm, ZS_W), lambda i: (i, 0)),
        ],
        out_shape=[
            jax.ShapeDtypeStruct((n, ZG_W), BF16),
            jax.ShapeDtypeStruct((n, ZM_W), BF16),
            jax.ShapeDtypeStruct((n, ZS_W), BF16),
        ],
        scratch_shapes=[pltpu.VMEM((D_MODEL, PACK_W), BF16)],
        compiler_params=_params(("arbitrary",)),
        name="in_proj",
    )(xp, xs, mod, nw, w_in_t)


def _gla_kernel(*refs, nsteps, seq, nseq, has_ctx, layer):
    it = iter(refs)
    zg_ref = next(it)
    s0_ref = next(it) if has_ctx else None
    waf_ref, wab_ref, ba_ref, onorm_ref, o_ref, sfin_ref, la_s, o_s, st_s = (next(it) for _ in range(9))
    chains = [(g, d) for g in range(nseq) for d in (0, 1)]
    inv_norm = 1.0 / GLA_GATE_NORM
    zero_blk = jnp.zeros((GLA_DK, GLA_DV), F32)
    for ch, (g, d) in enumerate(chains):
        if d == 0:
            a_blk = zg_ref[g * seq:(g + 1) * seq, ZG_A:ZG_A + LANE]
        wa_ref = waf_ref if d == 0 else wab_ref
        a_low = _dot(a_blk, wa_ref[...]) + ba_ref[d:d + 1, :]
        la_s[ch] = (jnp.minimum(a_low, 0.0) - jnp.log(1.0 + jnp.exp(-jnp.abs(a_low)))) * inv_norm
        if has_ctx:
            s0 = s0_ref[g, d]
            rows_bd = []
            for h in range(GLA_HEADS):
                sh = s0[h * GLA_DK:(h + 1) * GLA_DK, :]
                rows_bd.append(jnp.concatenate([sh if h2 == h else zero_blk for h2 in range(GLA_HEADS)], axis=1))
            st_s[ch] = jnp.concatenate(rows_bd, axis=0).T
        else:
            st_s[ch] = jnp.zeros((GLA_WIDTH, GLA_QK), F32)

    def iota(shape, axis, shift):
        return lax.shift_right_logical(lax.broadcasted_iota(jnp.int32, shape, axis), shift)

    log_chunk, log_dv = CHUNK.bit_length() - 1, GLA_DV.bit_length() - 1
    row = lax.broadcasted_iota(jnp.int32, (GLA_STEP, GLA_STEP), 0)
    col = lax.broadcasted_iota(jnp.int32, (GLA_STEP, GLA_STEP), 1)
    same_chunk = iota((GLA_STEP, GLA_STEP), 0, log_chunk) == iota((GLA_STEP, GLA_STEP), 1, log_chunk)
    masks = (same_chunk & (row >= col), same_chunk & (row <= col))
    lane_head = iota((GLA_STEP, GLA_QK), 1, log_chunk)
    row_chunk = iota((GLA_STEP, GLA_QK), 0, log_chunk)
    state_blk = iota((GLA_WIDTH, GLA_QK), 0, log_dv) == iota((GLA_WIDTH, GLA_QK), 1, log_chunk)
    qscale = GLA_DK ** -0.5
    nch = GLA_STEP // CHUNK

    def chain_phases(i, ch, g, d):
        r0 = pl.multiple_of((i if d == 0 else nsteps - 1 - i) * GLA_STEP, GLA_STEP)
        zrows, rows = pl.ds(g * seq + r0, GLA_STEP), pl.ds(r0, GLA_STEP)
        a_hi, a_lo = _split_bf16(la_s[ch, rows, :], 2)
        tri = masks[d].astype(BF16)
        cum = (jnp.dot(tri, a_hi, preferred_element_type=F32)
               + jnp.dot(tri, a_lo, preferred_element_type=F32))
        yield
        edge = CHUNK - 1 if d == 0 else 0
        blast = [cum[c * CHUNK + edge:c * CHUNK + edge + 1, :] for c in range(nch)]
        bl = jnp.concatenate([jnp.broadcast_to(b, (CHUNK, GLA_QK)) for b in blast], axis=0)
        q = zg_ref[zrows, ZG_Q:ZG_Q + GLA_QK].astype(F32) * qscale
        k = zg_ref[zrows, ZG_K:ZG_K + GLA_QK].astype(F32)
        v = zg_ref[zrows, ZG_V:ZG_V + GLA_WIDTH]
        v_t = v.astype(F32).T.astype(BF16)
        qd = q * jnp.exp(cum)
        kd = (k * jnp.exp(-cum)).astype(BF16)
        kr = k * jnp.exp(bl - cum)
        yield
        outs = []
        for h in range(GLA_HEADS):
            qh = jnp.where(lane_head == h, qd, 0.0)
            att = _dot_nt(qh, kd)
            yield
            att = jnp.where(masks[d], att, 0.0)
            outs.append(_dot(att, v[:, h * GLA_DV:(h + 1) * GLA_DV]))
            yield
        s = st_s[ch]
        inter = [None] * nch
        for c in (range(nch) if d == 0 else reversed(range(nch))):
            inter[c] = _dot_nt(qd[c * CHUNK:(c + 1) * CHUNK, :], s)
            kv_t = _dot(v_t, jnp.where(row_chunk == c, kr, 0.0))
            yield
            s = s * jnp.exp(blast[c]) + jnp.where(state_blk, kv_t, 0.0)
            yield
        st_s[ch] = s
        o_s[ch, rows, :] = jnp.concatenate(outs, axis=1) + jnp.concatenate(inter, axis=0)

    def step(i, carry):
        _emit_skewed([chain_phases(i, ch, g, d) for ch, (g, d) in enumerate(chains)], GLA_SKEW)
        return carry

    lax.fori_loop(0, nsteps, step, 0)
    onorm = onorm_ref[layer:layer + 1, :]
    for ch, (g, d) in enumerate(chains):
        s_fin = st_s[ch].T
        for h in range(GLA_HEADS):
            sfin_ref[g, d, h * GLA_DK:(h + 1) * GLA_DK, :] = (
                s_fin[h * GLA_DK:(h + 1) * GLA_DK, h * GLA_DV:(h + 1) * GLA_DV])
    for g in range(nseq):
        srows = slice(g * seq, (g + 1) * seq)
        o = o_s[2 * g] + o_s[2 * g + 1]
        gate = zg_ref[srows, ZG_GATE:ZG_GATE + GLA_WIDTH].astype(F32)
        for h in range(GLA_HEADS):
            vs = slice(h * GLA_DV, (h + 1) * GLA_DV)
            oh = o[:, vs]
            ms = jnp.mean(oh * oh, axis=-1, keepdims=True)
            o_ref[srows, vs] = oh * lax.rsqrt(ms + EPS) * onorm * jax.nn.silu(gate[:, vs])


def _gla(zg, row0, ctx, layer, waf, wab, ba, onorm, bsz, seq):
    nseq = max(GLA_SEQS_PER_STEP, GLA_ROWS_PER_STEP // seq)
    blk0 = row0 // (nseq * seq)
    in_specs = [pl.BlockSpec((nseq * seq, ZG_W), lambda b: (b + blk0, 0))]
    args = [zg]
    if ctx is not None:
        in_specs.append(pl.BlockSpec((nseq, None, 2, GLA_QK, GLA_DV), lambda b: (b, layer, 0, 0, 0)))
        args.append(ctx)
    in_specs += [
        _layer_spec((LANE, GLA_QK), layer),
        _layer_spec((LANE, GLA_QK), layer),
        _layer_spec((2, GLA_QK), layer),
        _vector_spec(GLA_DV),
    ]
    return pl.pallas_call(
        functools.partial(_gla_kernel, nsteps=seq // GLA_STEP, seq=seq, nseq=nseq, has_ctx=ctx is not None,
                          layer=layer),
        grid=(bsz // nseq,),
        in_specs=in_specs,
        out_specs=[
            pl.BlockSpec((nseq * seq, GLA_WIDTH), lambda b: (b, 0)),
            pl.BlockSpec((nseq, 2, GLA_QK, GLA_DV), lambda b: (b, 0, 0, 0)),
        ],
        out_shape=[
            jax.ShapeDtypeStruct((bsz * seq, GLA_WIDTH), F32),
            jax.ShapeDtypeStruct((bsz, 2, GLA_QK, GLA_DV), F32),
        ],
        scratch_shapes=[
            pltpu.VMEM((2 * nseq, seq, GLA_QK), F32),
            pltpu.VMEM((2 * nseq, seq, GLA_WIDTH), F32),
            pltpu.VMEM((2 * nseq, GLA_WIDTH, GLA_QK), F32),
        ],
        compiler_params=_params(("parallel",)),
        name="gla",
    )(*args, waf, wab, ba, onorm)


def _rms(x, w):
    ms = jnp.mean(x * x, axis=-1, keepdims=True)
    return x * lax.rsqrt(ms + EPS) * w


def _head_sums_mxu(x):
    width = x.shape[-1]
    shift = HEAD_PAD.bit_length() - 1
    gi = lax.shift_right_logical(lax.broadcasted_iota(jnp.int32, (width, width), 0), shift)
    gj = lax.shift_right_logical(lax.broadcasted_iota(jnp.int32, (width, width), 1), shift)
    return _dot(x * x, jnp.where(gi == gj, 1.0, 0.0))


def _head_norm(x, w, rope, on_mxu):
    sums = _head_sums_mxu(x) if on_mxu else None
    outs = []
    for h in range(MLA_HEADS):
        hs = slice(h * HEAD_PAD, (h + 1) * HEAD_PAD)
        xh = x[:, hs]
        ss = sums[:, hs] if on_mxu else jnp.sum(xh * xh, axis=-1, keepdims=True)
        yh = xh * lax.rsqrt(ss * (1.0 / MLA_QK) + EPS) * w
        if rope is not None:
            c, s = rope
            yh = yh * c + pltpu.roll(yh, ROPE_SHIFT, 1) * s
        outs.append(yh)
    return outs


def _place_rope_key(kr, e):
    return sum(jnp.dot(p, e, preferred_element_type=F32) for p in _split_bf16(kr, 3))


def _mla_kernel(*refs, seq, nseq, n_ctx, use_rope, layer):
    it = iter(refs)
    zm_ref = next(it)
    if n_ctx:
        cckv_ref, ckr_ref = next(it), next(it)
    qn_ref, wuq_ref, kvn_ref, wuk_ref, wuv_ref, qhn_ref, khn_ref, e_ref = (next(it) for _ in range(8))
    rope_ref = next(it) if use_rope else None
    o_ref, ckv_ref = next(it), next(it)
    q_s, k_s, v_s = next(it), next(it), next(it)
    qn, kvn, qhn, khn = (r[layer:layer + 1, :] for r in (qn_ref, kvn_ref, qhn_ref, khn_ref))

    qscale = MLA_QK ** -0.5
    heads = [slice(h * HEAD_PAD, (h + 1) * HEAD_PAD) for h in range(MLA_HEADS)]

    def keys_values(g, ckv, k_rope_placed, rope, k_rows):
        k_raw = _dot(ckv, wuk_ref[...]) + k_rope_placed
        yield
        kh = _head_norm(k_raw, khn, rope, False)
        for h, hs in enumerate(heads):
            k_s[g, k_rows, hs] = kh[h].astype(BF16)
        yield
        v_s[g, k_rows, :] = _dot(ckv, wuv_ref[...]).astype(BF16)
        yield

    def latent_phases(i, g):
        r0 = pl.multiple_of(i * PROJ_TILE, PROJ_TILE)
        tile, rows = pl.ds(r0, PROJ_TILE), pl.ds(g * seq + r0, PROJ_TILE)
        rope = (rope_ref[0, tile, :], rope_ref[1, tile, :]) if use_rope else None
        ckv = _rms(zm_ref[rows, ZM_KV:ZM_KV + MLA_KV_LORA].astype(F32), kvn)
        ckv_ref[rows, :] = ckv
        k_pe = jnp.dot(zm_ref[rows, ZM_KR:ZM_KR + LANE], e_ref[...], preferred_element_type=F32)
        yield from keys_values(g, ckv, k_pe, rope, pl.ds(n_ctx + r0, PROJ_TILE))
        cq = _rms(zm_ref[rows, ZM_Q:ZM_Q + MLA_Q_LORA].astype(F32), qn)
        q_raw = _dot(cq, wuq_ref[...])
        yield
        qh = _head_norm(q_raw, qhn, rope, True)
        for h, hs in enumerate(heads):
            q_s[rows, hs] = (qh[h] * qscale).astype(BF16)

    def latent_tile(i, carry):
        _emit_skewed([latent_phases(i, g) for g in range(nseq)], 0)
        return carry

    lax.fori_loop(0, seq // PROJ_TILE, latent_tile, 0)

    def context_tile(i, carry):
        rows = pl.ds(pl.multiple_of(i * PROJ_TILE, PROJ_TILE), PROJ_TILE)
        _emit_skewed([keys_values(g, cckv_ref[g, rows, :], _place_rope_key(ckr_ref[g, rows, :], e_ref[...]),
                                  None, rows) for g in range(nseq)], 0)
        return carry

    if n_ctx:
        lax.fori_loop(0, n_ctx // PROJ_TILE, context_tile, 0)

    q_tile = min(seq, Q_TILE)

    def head_phases(g, h, hs, rows, gate):
        s = lax.dot_general(q_s[rows, hs], k_s[g, :, hs], (((1,), (1,)), ((), ())),
                            preferred_element_type=F32)
        yield
        e = jnp.exp(s - jnp.max(s, axis=-1, keepdims=True))
        l = jnp.sum(e, axis=-1, keepdims=True)
        p = e.astype(BF16)
        yield
        o = jnp.dot(p, v_s[g, :, hs], preferred_element_type=F32) / l
        o_ref[rows, hs] = o * jax.nn.silu(gate[:, hs])

    def q_block(i, carry):
        chains = []
        for g in range(nseq):
            rows = pl.ds(g * seq + pl.multiple_of(i * q_tile, q_tile), q_tile)
            gate = zm_ref[rows, ZM_GATE:ZM_GATE + MLA_WIDTH].astype(F32)
            chains += [head_phases(g, h, hs, rows, gate) for h, hs in enumerate(heads)]
        _emit_skewed(chains, 0)
        return carry

    lax.fori_loop(0, seq // q_tile, q_block, 0)


def _mla(zm, row0, ctx, layer, w, rope_tab, bsz, seq):
    n_ctx = 0 if ctx is None else ctx[0].shape[-2]
    nseq = max(1, MLA_ROWS_PER_STEP // seq)
    blk0 = row0 // (nseq * seq)
    in_specs = [pl.BlockSpec((nseq * seq, ZM_W), lambda b: (b + blk0, 0))]
    args = [zm]
    if ctx is not None:
        cckv, ckr = ctx
        in_specs += [
            pl.BlockSpec((nseq, None, n_ctx, MLA_KV_LORA), lambda b: (b, layer, 0, 0)),
            pl.BlockSpec((nseq, None, n_ctx, LANE), lambda b: (b, layer, 0, 0)),
        ]
        args += [cckv, ckr]
    in_specs += [
        _vector_spec(MLA_Q_LORA),
        _layer_spec((MLA_Q_LORA, MLA_HEADS * HEAD_PAD), layer),
        _vector_spec(MLA_KV_LORA),
        _layer_spec((MLA_KV_LORA, MLA_HEADS * HEAD_PAD), layer),
        _layer_spec((MLA_KV_LORA, MLA_WIDTH), layer),
        _vector_spec(HEAD_PAD),
        _vector_spec(HEAD_PAD),
        pl.BlockSpec((LANE, MLA_HEADS * HEAD_PAD), lambda b: (0, 0)),
    ]
    args += list(w)
    if rope_tab is not None:
        in_specs.append(pl.BlockSpec((2, seq, HEAD_PAD), lambda b: (0, 0, 0)))
        args.append(rope_tab)
    return pl.pallas_call(
        functools.partial(_mla_kernel, seq=seq, nseq=nseq, n_ctx=n_ctx, use_rope=rope_tab is not None,
                          layer=layer),
        grid=(bsz // nseq,),
        in_specs=in_specs,
        out_specs=[
            pl.BlockSpec((nseq * seq, MLA_WIDTH), lambda b: (b, 0)),
            pl.BlockSpec((nseq * seq, MLA_KV_LORA), lambda b: (b, 0)),
        ],
        out_shape=[
            jax.ShapeDtypeStruct((bsz * seq, MLA_WIDTH), F32),
            jax.ShapeDtypeStruct((bsz * seq, MLA_KV_LORA), F32),
        ],
        scratch_shapes=[
            pltpu.VMEM((nseq * seq, MLA_HEADS * HEAD_PAD), BF16),
            pltpu.VMEM((nseq, n_ctx + seq, MLA_HEADS * HEAD_PAD), BF16),
            pltpu.VMEM((nseq, n_ctx + seq, MLA_WIDTH), BF16),
        ],
        compiler_params=_params(("parallel",)),
        name="mla",
    )(*args)


S5_T = 8
S5_R = CHUNK // S5_T
S5_SUB_CH = 64
S5_SUBS = LANE // S5_SUB_CH
S5_SUB_STATE = S5_TILE_STATE // S5_SUBS
S5_ROW = S5_T * S5_SUB_CH
S5_W = 2 * S5_SUB_STATE
W_M, W_SF, W_SB, W_CF, W_CB = range(5)


def _cmul(ar, ai, br, bi):
    return ar * br - ai * bi, ar * bi + ai * br


def _s5_prep_kernel(are_ref, aim_ref, ldt_ref, bre_ref, bim_ref, cre_ref, cim_ref, d_ref,
                    w_ref, tab8_ref, tab1_ref):
    gr = lax.shift_right_logical(lax.broadcasted_iota(jnp.int32, (S5_SUB_CH, S5_SUB_STATE), 0),
                                 S5_GROUP.bit_length() - 1)
    gc = lax.shift_right_logical(lax.broadcasted_iota(jnp.int32, (S5_SUB_CH, S5_SUB_STATE), 1),
                                 S5_STATE.bit_length() - 1)

    def spread(ref, h):
        x = ref[h * S5_SUB_CH:(h + 1) * S5_SUB_CH, :]
        return jnp.where(gr == gc, jnp.concatenate([x] * (S5_SUB_CH // S5_GROUP), axis=1), 0.0)

    row = lax.broadcasted_iota(jnp.int32, (S5_SUB_CH, S5_SUB_CH), 0)
    col = lax.broadcasted_iota(jnp.int32, (S5_SUB_CH, S5_SUB_CH), 1)
    taps = [[[], []] for _ in range(S5_SUBS)]
    for d in (0, 1):
        prow = pl.ds(2 * pl.program_id(0) + d, 1)
        a_re, a_im = are_ref[prow, :], aim_ref[prow, :]
        dt = jnp.exp(ldt_ref[prow, :])
        lam = a_re * dt
        th = a_im * dt
        mag = jnp.exp(lam)
        ab_re = mag * jnp.cos(th)
        ab_im = mag * jnp.sin(th)
        den = a_re * a_re + a_im * a_im
        n_re = ab_re - 1.0
        cf_re = (n_re * a_re + ab_im * a_im) / den
        cf_im = (ab_im * a_re - n_re * a_im) / den
        k = lax.broadcasted_iota(jnp.int32, (2 * S5_T, S5_TILE_STATE), 0).astype(F32)
        pmag = jnp.exp(k * lam)
        pw_re = pmag * jnp.cos(k * th)
        pw_im = pmag * jnp.sin(k * th)
        for h in range(S5_SUBS):
            ss = slice(h * S5_SUB_STATE, (h + 1) * S5_SUB_STATE)
            c_re, c_im = spread(cre_ref, h), spread(cim_ref, h)
            c_cat = jnp.concatenate([c_re, c_im], axis=1).astype(BF16)
            bp_re, bp_im = _cmul(spread(bre_ref, h), spread(bim_ref, h), cf_re[:, ss], cf_im[:, ss])
            for p in range(S5_T + 1):
                ar, ai = pw_re[p:p + 1, ss], pw_im[p:p + 1, ss]
                t_in = S5_T - 1 - p if d == 0 else p
                t_out = p - 1 if d == 0 else S5_T - p
                if p < S5_T:
                    l_re, l_im = _cmul(bp_re, bp_im, ar, ai)
                    w_ref[h, W_SF + d, t_in * S5_SUB_CH:(t_in + 1) * S5_SUB_CH, :] = (
                        jnp.concatenate([l_re, l_im], axis=1).astype(BF16))
                    taps[h][d].append(_dot_nt(jnp.concatenate([l_re, -l_im], axis=1), c_cat))
                if p > 0:
                    v_re, v_im = _cmul(c_re, c_im, ar, ai)
                    w_ref[h, W_CF + d, t_out * S5_SUB_CH:(t_out + 1) * S5_SUB_CH, :] = (
                        jnp.concatenate([v_re, -v_im], axis=1).astype(BF16))
        r = lax.broadcasted_iota(jnp.int32, (S5_R, S5_TILE_STATE), 0).astype(F32) * float(S5_T)
        r1 = float(S5_T * (S5_R - 1)) - r
        pm = jnp.exp(r * lam)
        qm = jnp.exp(r1 * lam)
        tab8_ref[d, 0] = pm * jnp.cos(r * th)
        tab8_ref[d, 1] = pm * jnp.sin(r * th)
        tab8_ref[d, 2] = qm * jnp.cos(r1 * th)
        tab8_ref[d, 3] = qm * jnp.sin(r1 * th)
        mc = jnp.exp(float(CHUNK) * lam)
        tab1_ref[d, 0:1, :] = mc * jnp.cos(float(CHUNK) * th)
        tab1_ref[d, 1:2, :] = mc * jnp.sin(float(CHUNK) * th)
    for h in range(S5_SUBS):
        skip = jnp.where(row == col, d_ref[:, h * S5_SUB_CH:(h + 1) * S5_SUB_CH], 0.0)
        for t in range(S5_T):
            blocks = []
            for t2 in range(S5_T):
                if t < t2:
                    blocks.append(taps[h][0][t2 - t])
                elif t > t2:
                    blocks.append(taps[h][1][t - t2])
                else:
                    blocks.append(taps[h][0][0] + taps[h][1][0] + skip)
            w_ref[h, W_M, t * S5_SUB_CH:(t + 1) * S5_SUB_CH, :] = jnp.concatenate(blocks, axis=1).astype(BF16)


def _s5_prep(a_re, a_im, ldt, b_re, b_im, c_re, c_im, dsk):
    vec = pl.BlockSpec((2 * DEPTH, S5_TILE_STATE), lambda l, j: (0, j))
    blk = pl.BlockSpec((None, None, LANE, S5_STATE), lambda l, j: (l, j, 0, 0))
    return pl.pallas_call(
        _s5_prep_kernel,
        grid=(DEPTH, S5_TILES),
        in_specs=[vec, vec, vec, blk, blk, blk, blk,
                  pl.BlockSpec((None, 1, LANE), lambda l, j: (l, 0, j))],
        out_specs=[
            pl.BlockSpec((None, None, S5_SUBS, 5, S5_ROW, S5_W), lambda l, j: (l, j, 0, 0, 0, 0)),
            pl.BlockSpec((None, None, 2, 4, S5_R, S5_TILE_STATE), lambda l, j: (l, j, 0, 0, 0, 0)),
            pl.BlockSpec((None, None, 2, 2, S5_TILE_STATE), lambda l, j: (l, j, 0, 0, 0)),
        ],
        out_shape=[
            jax.ShapeDtypeStruct((DEPTH, S5_TILES, S5_SUBS, 5, S5_ROW, S5_W), BF16),
            jax.ShapeDtypeStruct((DEPTH, S5_TILES, 2, 4, S5_R, S5_TILE_STATE), F32),
            jax.ShapeDtypeStruct((DEPTH, S5_TILES, 2, 2, S5_TILE_STATE), F32),
        ],
        compiler_params=_params(("parallel", "parallel")),
        name="s5_prep",
    )(a_re, a_im, ldt, b_re, b_im, c_re, c_im, dsk)


def _s5_scan_kernel(u_ref, x0_ref, w_ref, tab8_ref, tab1_ref, y_ref, fs_ref, u_s, *, nseq, nb):
    groups = nseq * nb
    nrow = groups * S5_R
    ts = S5_SUB_STATE
    u_s[...] = u_ref[...].astype(F32)
    tokens = [u_s[pl.ds(t, nrow, stride=S5_T), :] for t in range(S5_T)]
    rowi = lax.broadcasted_iota(jnp.int32, (groups, S5_R, ts), 1)

    def shift(x, n, down):
        if down:
            return jnp.where(rowi >= n, pltpu.roll(x, n, 1), 0.0)
        return jnp.where(rowi < S5_R - n, pltpu.roll(x, S5_R - n, 1), 0.0)

    def block_scan(e_re, e_im, p_re, p_im, down):
        row8 = lax.broadcasted_iota(jnp.int32, (S5_R, ts), 0)
        for n in (1, 2, 4):
            keep = row8 >= n if down else row8 < S5_R - n
            c_re, c_im = jnp.where(keep, p_re[n:n + 1, :], 0.0), jnp.where(keep, p_im[n:n + 1, :], 0.0)
            turn = n if down else S5_R - n
            s_re, s_im = _cmul(c_re, c_im, pltpu.roll(e_re, turn, 1), pltpu.roll(e_im, turn, 1))
            e_re, e_im = e_re + s_re, e_im + s_im
        return e_re, e_im

    y_sub = []
    for h in range(S5_SUBS):
        ch = slice(h * S5_SUB_CH, (h + 1) * S5_SUB_CH)
        ss = slice(h * ts, (h + 1) * ts)
        u8 = jnp.concatenate([tok[:, ch] for tok in tokens], axis=1).astype(BF16)
        ef = jnp.dot(u8, w_ref[h, W_SF], preferred_element_type=F32).reshape(groups, S5_R, S5_W)
        eb = jnp.dot(u8, w_ref[h, W_SB], preferred_element_type=F32).reshape(groups, S5_R, S5_W)

        p_re, p_im, q_re, q_im = (tab8_ref[0, i, :, ss] for i in range(4))
        a_re, a_im = tab1_ref[0, 0:1, ss], tab1_ref[0, 1:2, ss]
        cs_re, cs_im = block_scan(ef[:, :, :ts], ef[:, :, ts:], p_re, p_im, True)
        st_re, st_im = [], []
        for s in range(nseq):
            x_re, x_im = x0_ref[s, 0, 0:1, ss], x0_ref[s, 0, 1:2, ss]
            for b in range(nb):
                g = s * nb + b
                st_re.append(x_re)
                st_im.append(x_im)
                k_re, k_im = _cmul(a_re, a_im, x_re, x_im)
                x_re = k_re + cs_re[g, S5_R - 1:S5_R, :]
                x_im = k_im + cs_im[g, S5_R - 1:S5_R, :]
            fs_ref[s, 0, 0:1, ss] = x_re
            fs_ref[s, 0, 1:2, ss] = x_im
        k_re, k_im = _cmul(p_re, p_im, jnp.stack(st_re), jnp.stack(st_im))
        xin = jnp.concatenate([shift(cs_re, 1, True) + k_re, shift(cs_im, 1, True) + k_im],
                              axis=2).reshape(nrow, S5_W)

        p_re, p_im, q_re, q_im = (tab8_ref[1, i, :, ss] for i in range(4))
        a_re, a_im = tab1_ref[1, 0:1, ss], tab1_ref[1, 1:2, ss]
        sf_re, sf_im = block_scan(eb[:, :, :ts], eb[:, :, ts:], p_re, p_im, False)
        z_re, z_im = [None] * groups, [None] * groups
        for s in range(nseq):
            x_re, x_im = x0_ref[s, 1, 0:1, ss], x0_ref[s, 1, 1:2, ss]
            for b in reversed(range(nb)):
                g = s * nb + b
                z_re[g], z_im[g] = x_re, x_im
                k_re, k_im = _cmul(a_re, a_im, x_re, x_im)
                x_re = sf_re[g, 0:1, :] + k_re
                x_im = sf_im[g, 0:1, :] + k_im
            fs_ref[s, 1, 0:1, ss] = x_re
            fs_ref[s, 1, 1:2, ss] = x_im
        k_re, k_im = _cmul(q_re, q_im, jnp.stack(z_re), jnp.stack(z_im))
        xnx = jnp.concatenate([shift(sf_re, 1, False) + k_re, shift(sf_im, 1, False) + k_im],
                              axis=2).reshape(nrow, S5_W)

        y_sub.append(jnp.dot(u8, w_ref[h, W_M], preferred_element_type=F32)
                     + _dot_nt(xin, w_ref[h, W_CF]) + _dot_nt(xnx, w_ref[h, W_CB]))
    for t in range(S5_T):
        tc = slice(t * S5_SUB_CH, (t + 1) * S5_SUB_CH)
        y_ref[pl.ds(t, nrow, stride=S5_T), :] = jnp.concatenate([y[:, tc] for y in y_sub], axis=1)


def _s5_scan(zs, row0, x0, x0_block, x0_idx, layer, wmat, tab8, tab1, nseq, seq):
    n = nseq * seq
    rblk = row0 // n
    return pl.pallas_call(
        functools.partial(_s5_scan_kernel, nseq=nseq, nb=seq // CHUNK),
        grid=(S5_TILES,),
        in_specs=[
            pl.BlockSpec((n, LANE), lambda j: (rblk, ZS_U // LANE + j)),
            pl.BlockSpec(x0_block, x0_idx),
            pl.BlockSpec((None, None, S5_SUBS, 5, S5_ROW, S5_W), lambda j: (layer, j, 0, 0, 0, 0)),
            pl.BlockSpec((None, None, 2, 4, S5_R, S5_TILE_STATE), lambda j: (layer, j, 0, 0, 0, 0)),
            pl.BlockSpec((None, None, 2, 2, S5_TILE_STATE), lambda j: (layer, j, 0, 0, 0)),
        ],
        out_specs=[
            pl.BlockSpec((n, LANE), lambda j: (0, j)),
            pl.BlockSpec((nseq, 2, 2, S5_TILE_STATE), lambda j: (0, 0, 0, j)),
        ],
        out_shape=[
            jax.ShapeDtypeStruct((n, S5_WIDTH), F32),
            jax.ShapeDtypeStruct((nseq, 2, 2, S5_NSTATE), F32),
        ],
        scratch_shapes=[pltpu.VMEM((n, LANE), F32)],
        compiler_params=_params(("parallel",)),
        name="s5_scan",
    )(zs, x0, wmat, tab8, tab1)


def _merge_kernel(x_ref, mod_ref, nw_ref, oa_ref, ob_ref, ys_ref, sg_ref, wglu_ref, bglu_ref, wmg_ref,
                  wa_ref, wb_ref, wc_ref, wout_ref, y_ref, wmg_s, *, layer, mod_idx):
    @pl.when(pl.program_id(0) == 0)
    def _():
        _pack_transposed(wmg_ref.at[0], wmg_s, 0, 3 * D_MODEL // LANE, 0, None)

    x = x_ref[...]
    mod = mod_ref[pl.ds(mod_idx(pl.program_id(0)), 1), :]
    h = _mod_rmsnorm(x, nw_ref[layer:layer + 1, :], mod).astype(BF16)
    zg = _dot(jax.nn.gelu(ys_ref[...]), wglu_ref[...]) + bglu_ref[layer:layer + 1, :]
    oc = (zg[:, :S5_WIDTH] * jax.nn.sigmoid(zg[:, S5_WIDTH:])
          * jax.nn.silu(sg_ref[...].astype(F32)))
    mixed = None
    for br, (o_br, w_ref) in enumerate(((oa_ref[...], wa_ref), (ob_ref[...], wb_ref), (oc, wc_ref))):
        gate = jax.nn.sigmoid(jnp.dot(h, wmg_s[:, br * D_MODEL:(br + 1) * D_MODEL], preferred_element_type=F32))
        term = gate * _dot(o_br, w_ref[...])
        mixed = term if mixed is None else mixed + term
    y_ref[...] = x + mod[:, 2 * D_MODEL:] * _dot(mixed, wout_ref[...])


def _merge(x2, row0, mod, mod_idx, nw, oa, ob, ys, zs, layer, wglu, bglu, w_in_t, wa, wb, wc, wout):
    n = x2.shape[0]
    tm = ROW_TILE
    blk0 = row0 // tm
    once = pl.Buffered(1)
    rows = lambda w: pl.BlockSpec((tm, w), lambda i: (i, 0))
    return pl.pallas_call(
        functools.partial(_merge_kernel, layer=layer, mod_idx=mod_idx),
        grid=(n // tm,),
        in_specs=[
            rows(D_MODEL),
            _layer_spec((COND_ROWS, 3 * D_MODEL), layer),
            _vector_spec(D_MODEL),
            rows(GLA_WIDTH), rows(MLA_WIDTH), rows(S5_WIDTH),
            pl.BlockSpec((tm, S5_WIDTH), lambda i: (i + blk0, ZS_GATE // S5_WIDTH)),
            pl.BlockSpec((None, S5_WIDTH, 2 * S5_WIDTH), lambda i: (layer, 0, 0), pipeline_mode=once),
            _vector_spec(2 * S5_WIDTH),
            pl.BlockSpec((pl.Element(1), pl.Element(3 * D_MODEL), pl.Element(D_MODEL)),
                         lambda i: (layer, MERGE_COL, 0), pipeline_mode=once),
            _layer_spec((GLA_WIDTH, D_MODEL), layer, once),
            _layer_spec((MLA_WIDTH, D_MODEL), layer, once),
            _layer_spec((S5_WIDTH, D_MODEL), layer, once),
            _layer_spec((D_MODEL, D_MODEL), layer, once),
        ],
        out_specs=rows(D_MODEL),
        out_shape=jax.ShapeDtypeStruct((n, D_MODEL), F32),
        scratch_shapes=[pltpu.VMEM((D_MODEL, 3 * D_MODEL), BF16)],
        compiler_params=_params(("arbitrary",)),
        name="merge",
    )(x2, mod, nw, oa, ob, ys, zs, wglu, bglu, w_in_t, wa, wb, wc, wout)


def _mla_lane_of_dim():
    half = MLA_ROPE // 2
    first_gap = ROPE_SHIFT - half
    lane = np.zeros(MLA_QK, np.int32)
    for j in range(MLA_NOPE):
        lane[j] = half + j if j < first_gap else 2 * half + j
    for r in range(half):
        lane[MLA_NOPE + r] = r
        lane[MLA_NOPE + half + r] = ROPE_SHIFT + r
    return lane


MLA_LANE_OF_DIM = _mla_lane_of_dim()


def _place_heads(w, heads, lane_of_dim):
    width = len(lane_of_dim)
    order = np.argsort(lane_of_dim)
    zeros = lambda n: jnp.zeros(w.shape[:-1] + (n,), w.dtype)
    pieces = []
    for h in range(heads):
        lane, i = 0, 0
        while i < width:
            j = i
            while (j + 1 < width and order[j + 1] == order[j] + 1
                   and lane_of_dim[order[j + 1]] == lane_of_dim[order[j]] + 1):
                j += 1
            dst = int(lane_of_dim[order[i]])
            if dst > lane:
                pieces.append(zeros(dst - lane))
            pieces.append(w[..., h * width + int(order[i]):h * width + int(order[j]) + 1])
            lane, i = dst + (j - i + 1), j + 1
        if lane < HEAD_PAD:
            pieces.append(zeros(HEAD_PAD - lane))
    return jnp.concatenate(pieces, axis=-1)


def _place_heads_bf16(w, heads, lane_of_dim):
    width = len(lane_of_dim)
    place = np.zeros((heads * width, heads * HEAD_PAD), np.float32)
    for h in range(heads):
        place[h * width + np.arange(width), h * HEAD_PAD + lane_of_dim] = 1.0
    return jnp.dot(w.astype(BF16), jnp.asarray(place, BF16), preferred_element_type=BF16)


def _rope_tables(n_tok):
    rows = n_tok // GRID_W
    r = jnp.repeat(jnp.arange(rows, dtype=F32), GRID_W)
    col = jnp.tile(jnp.arange(GRID_W, dtype=F32), rows)
    n_freq = MLA_ROPE // 4
    inv = ROPE_THETA ** (-jnp.arange(n_freq, dtype=F32) / n_freq)
    ang = jnp.concatenate([r[:, None] * inv, col[:, None] * inv], axis=-1)
    cos, sin = jnp.cos(ang), jnp.sin(ang)
    ones = jnp.ones((n_tok, MLA_NOPE), F32)
    c = _place_heads(jnp.concatenate([ones, cos, cos], axis=1), 1, MLA_LANE_OF_DIM)
    s = _place_heads(jnp.concatenate([0.0 * ones, -sin, sin], axis=1), 1, MLA_LANE_OF_DIM)
    return jnp.stack([c, s])


def kernel(x_prompt, x_sample, c, c_ctx, cache_mla_ckv, cache_mla_krope, state_gla, state_s5,
           norm_w, w_ada, b_ada, w_in, gla_w_a2, gla_b_a, gla_o_norm,
           mla_q_norm, mla_w_uq, mla_kv_norm, mla_w_uk, mla_w_uv, mla_qh_norm, mla_kh_norm,
           s5_a_re, s5_a_im, s5_log_dt, s5_b_re, s5_b_im, s5_c_re, s5_c_im, s5_d, s5_w_glu, s5_b_glu,
           w_bo_gla, w_bo_mla, w_bo_s5, w_out):
    bsz, seq, _ = x_prompt.shape
    dbsz, dseq, _ = x_sample.shape
    ctx_row = COND_ROWS - 1
    assert dbsz <= ctx_row and (bsz * seq) % ROW_TILE == 0 and dseq % ROW_TILE == 0

    cond = jnp.zeros((COND_ROWS, D_MODEL), F32).at[0:dbsz].set(c).at[ctx_row].set(c_ctx)
    ada = _ada(cond, w_ada, b_ada)

    vec = lambda a: a.reshape(2 * DEPTH, S5_NSTATE)
    ldt = jnp.repeat(s5_log_dt[..., None], S5_STATE, axis=-1)
    rows_gp = lambda t: t.reshape(DEPTH, S5_TILES, LANE, S5_STATE)
    bt = lambda b: rows_gp(b.transpose(0, 1, 3, 2))
    wmat, tab8, tab1 = _s5_prep(vec(s5_a_re), vec(s5_a_im), vec(ldt), bt(s5_b_re), bt(s5_b_im),
                                rows_gp(s5_c_re), rows_gp(s5_c_im), s5_d.reshape(DEPTH, 1, S5_WIDTH))

    wuq = _place_heads_bf16(mla_w_uq, MLA_HEADS, MLA_LANE_OF_DIM)
    wuk = _place_heads_bf16(mla_w_uk, MLA_HEADS, MLA_LANE_OF_DIM[:MLA_NOPE])
    wuv = mla_w_uv.astype(BF16)
    qhn = _place_heads(mla_qh_norm, 1, MLA_LANE_OF_DIM)
    khn = _place_heads(mla_kh_norm, 1, MLA_LANE_OF_DIM)
    e_np = np.zeros((LANE, MLA_HEADS * HEAD_PAD), np.float32)
    for h in range(MLA_HEADS):
        for i in range(MLA_ROPE):
            e_np[i, h * HEAD_PAD + MLA_LANE_OF_DIM[MLA_NOPE + i]] = 1.0
    e_place = jnp.asarray(e_np, BF16)
    rope_tab = _rope_tables(dseq)
    ckr_pad = jnp.pad(cache_mla_krope, ((0, 0), (0, 0), (0, 0), (0, LANE - MLA_ROPE)))

    zrow = lambda n: jnp.zeros((DEPTH, n, GLA_QK), F32)
    waf = jnp.concatenate([gla_w_a2[:, 0], zrow(LANE - GLA_RANK)], axis=1).astype(BF16)
    wab = jnp.concatenate([zrow(GLA_RANK), gla_w_a2[:, 1], zrow(LANE - 2 * GLA_RANK)], axis=1).astype(BF16)
    sgla = state_gla.reshape(dbsz, DEPTH, 2, GLA_QK, GLA_DV)
    ss5 = state_s5.reshape(dbsz, DEPTH, 2, 2, S5_NSTATE)
    zero_s5 = jnp.zeros((bsz, 2, 2, S5_NSTATE), F32)

    hp = x_prompt.reshape(bsz * seq, D_MODEL)
    hs = x_sample.reshape(dbsz * dseq, D_MODEL)
    ckv_l, krope_l, gla_l, s5_l = [], [], [], []
    w_in_t = jnp.swapaxes(w_in, 1, 2)
    mod = ada
    mla_w = (mla_q_norm, wuq, mla_kv_norm, wuk, wuv, qhn, khn, e_place)
    wbo = (w_bo_gla, w_bo_mla, w_bo_s5)
    for l in range(DEPTH):
        p_rows, p_blocks, blocks_per_seq = bsz * seq, bsz * seq // ROW_TILE, dseq // ROW_TILE
        mod_idx = lambda i: jnp.where(i < p_blocks, ctx_row, (i - p_blocks) // blocks_per_seq)
        zg, zm, zs = _in_proj(hp, hs, mod, mod_idx, norm_w, w_in_t, l)

        def mixers(x2, row0, nb, n, ctx):
            if ctx:
                gctx = sgla
                x0, x0_blk = ss5, (nb, None, 2, 2, S5_TILE_STATE)
                x0_idx = lambda j: (0, l, 0, 0, j)
                mctx, rt = (cache_mla_ckv, ckr_pad), rope_tab
            else:
                gctx = None
                x0, x0_blk = zero_s5, (nb, 2, 2, S5_TILE_STATE)
                x0_idx = lambda j: (0, 0, 0, j)
                mctx, rt = None, None
            oa, st_gla = _gla(zg, row0, gctx, l, waf, wab, gla_b_a, gla_o_norm, nb, n)
            ob, ckv = _mla(zm, row0, mctx, l, mla_w, rt, nb, n)
            y_ssm, st_s5 = _s5_scan(zs, row0, x0, x0_blk, x0_idx, l, wmat, tab8, tab1, nb, n)
            grp_mod_idx = lambda i: mod_idx(i + row0 // ROW_TILE)
            y = _merge(x2, row0, mod, grp_mod_idx, norm_w, oa, ob, y_ssm, zs, l, s5_w_glu, s5_b_glu, w_in_t, *wbo, w_out)
            return y, ckv, st_gla, st_s5

        hp_next, ckv_p, st_gla_p, st_s5_p = mixers(hp, 0, bsz, seq, False)
        hs = mixers(hs, p_rows, dbsz, dseq, True)[0]
        hp = hp_next
        ckv_l.append(ckv_p.reshape(bsz, seq, MLA_KV_LORA))
        krope_l.append(zm[:p_rows, ZM_KR:ZM_KR + MLA_ROPE].astype(F32).reshape(bsz, seq, MLA_ROPE))
        gla_l.append(st_gla_p.reshape(bsz, 2, GLA_HEADS, GLA_DK, GLA_DV))
        s5_l.append(st_s5_p.reshape(bsz, 2, 2, S5_GROUPS, S5_STATE))

    return (hp.reshape(bsz, seq, D_MODEL), hs.reshape(dbsz, dseq, D_MODEL),
            jnp.stack(ckv_l, axis=1), jnp.stack(krope_l, axis=1),
            jnp.stack(gla_l, axis=1), jnp.stack(s5_l, axis=1))
```

```python
import functools

import jax
import jax.numpy as jnp
import numpy as np
from jax import lax
from jax.experimental import pallas as pl
from jax.experimental.pallas import tpu as pltpu

F32 = jnp.float32
BF16 = jnp.bfloat16

EPS = 1e-6
D_MODEL = 1024
DEPTH = 2
GRID_W = 64
ROPE_THETA = 10000.0
GLA_HEADS = 4
GLA_DK = 64
GLA_DV = 128
GLA_RANK = 16
GLA_GATE_NORM = 16.0
GLA_QK = GLA_HEADS * GLA_DK
GLA_WIDTH = GLA_HEADS * GLA_DV
MLA_HEADS = 4
MLA_Q_LORA = 384
MLA_KV_LORA = 256
MLA_NOPE = 64
MLA_ROPE = 32
MLA_QK = MLA_NOPE + MLA_ROPE
MLA_DV = 128
MLA_WIDTH = MLA_HEADS * MLA_DV
S5_WIDTH = 512
S5_GROUP = 16
S5_GROUPS = 32
S5_STATE = 64
S5_NSTATE = S5_GROUPS * S5_STATE

LANE = 128
SUBLANE = 8
COND_ROWS = SUBLANE
HEAD_PAD = LANE
ROPE_SHIFT = LANE // 2
CHUNK = 64
GLA_STEP = 256
GLA_SEQS_PER_STEP = 2
GLA_ROWS_PER_STEP = 1024
GLA_SKEW = 1
S5_TILES = S5_WIDTH // LANE
S5_TILE_STATE = S5_NSTATE // S5_TILES
ROW_TILE = 512
Q_TILE = 512
PROJ_TILE = 256
MLA_ROWS_PER_STEP = 1024
VMEM_LIMIT = 56 * 1024 * 1024

IN_SPLITS = (GLA_QK, GLA_QK, GLA_WIDTH, GLA_RANK, GLA_RANK, GLA_WIDTH,
             MLA_Q_LORA, MLA_KV_LORA, MLA_ROPE, MLA_WIDTH,
             S5_WIDTH, S5_WIDTH, 3 * D_MODEL)
(IN_GQ, IN_GK, IN_GV, IN_GA, IN_GAB, IN_GG, IN_MQ, IN_MKV, IN_MKR, IN_MG, IN_SU, IN_SG,
 MERGE_COL, D_IN) = (int(c) for c in np.cumsum((0,) + IN_SPLITS))
ZG_Q, ZG_K, ZG_V, ZG_A = 0, GLA_QK, 2 * GLA_QK, 2 * GLA_QK + GLA_WIDTH
ZG_GATE = ZG_A + LANE
ZG_W = ZG_GATE + GLA_WIDTH
ZM_Q, ZM_KV, ZM_KR = 0, MLA_Q_LORA, MLA_Q_LORA + MLA_KV_LORA
ZM_GATE = ZM_KR + LANE
ZM_W = ZM_GATE + MLA_WIDTH
ZS_U, ZS_GATE, ZS_W = 0, S5_WIDTH, 2 * S5_WIDTH
ZG_BASE, ZM_BASE, ZS_BASE, PACK_W = 0, ZG_W, ZG_W + ZM_W, ZG_W + ZM_W + ZS_W
IN_PIECES = (
    (IN_GQ, (IN_GA - IN_GQ) // LANE, ZG_BASE + ZG_Q, None),
    (IN_GA, 1, ZG_BASE + ZG_A, 2 * GLA_RANK),
    (IN_GG, GLA_WIDTH // LANE, ZG_BASE + ZG_GATE, None),
    (IN_MQ, (IN_MKR - IN_MQ) // LANE, ZM_BASE + ZM_Q, None),
    (IN_MKR, 1, ZM_BASE + ZM_KR, MLA_ROPE),
    (IN_MG, MLA_WIDTH // LANE, ZM_BASE + ZM_GATE, None),
    (IN_SU, (MERGE_COL - IN_SU) // LANE, ZS_BASE + ZS_U, None),
)


def _dot(a, b):
    return jnp.dot(a.astype(BF16), b.astype(BF16), preferred_element_type=F32)


def _dot_nt(a, b):
    return lax.dot_general(a.astype(BF16), b.astype(BF16), (((1,), (1,)), ((), ())),
                           preferred_element_type=F32)


def _split_bf16(x, parts):
    out = []
    r = x
    for _ in range(parts):
        p = r.astype(BF16)
        out.append(p)
        r = r - p.astype(F32)
    return out


def _emit_skewed(chains, skew):
    pending, active, tick = list(chains), [], 0
    while pending or active:
        while pending and (skew == 0 or tick % skew == 0):
            active.append(pending.pop(0))
            if skew:
                break
        for gen in list(active):
            if next(gen, "done") == "done":
                active.remove(gen)
        tick += 1


def _vector_spec(width):
    return pl.BlockSpec((DEPTH, width), lambda *_: (0, 0))


def _layer_spec(shape, layer, pipeline_mode=None):
    kwargs = {} if pipeline_mode is None else {"pipeline_mode": pipeline_mode}
    return pl.BlockSpec((None,) + tuple(shape), lambda *_: (layer,) + (0,) * len(shape), **kwargs)


def _params(sem):
    return pltpu.CompilerParams(dimension_semantics=sem, vmem_limit_bytes=VMEM_LIMIT)


def _ada_kernel(c_ref, w_ref, b_ref, o_ref):
    s = jax.nn.silu(c_ref[...])
    o_ref[...] = _dot(s, w_ref[...]) + b_ref[pl.ds(pl.program_id(0), 1), :]


def _ada(cond8, w_ada, b_ada):
    tn = 3 * D_MODEL // 2
    return pl.pallas_call(
        _ada_kernel,
        grid=(DEPTH, 3 * D_MODEL // tn),
        in_specs=[
            pl.BlockSpec((COND_ROWS, D_MODEL), lambda l, n: (0, 0)),
            pl.BlockSpec((None, D_MODEL, tn), lambda l, n: (l, 0, n)),
            pl.BlockSpec((DEPTH, tn), lambda l, n: (0, n)),
        ],
        out_specs=pl.BlockSpec((None, COND_ROWS, tn), lambda l, n: (l, 0, n)),
        out_shape=jax.ShapeDtypeStruct((DEPTH, COND_ROWS, 3 * D_MODEL), F32),
        compiler_params=_params(("parallel", "parallel")),
        name="ada",
    )(cond8, w_ada, b_ada)


def _mod_rmsnorm(x, nw, mod):
    ms = jnp.mean(x * x, axis=-1, keepdims=True)
    y = x * lax.rsqrt(ms + EPS) * nw
    return y * (1.0 + mod[:, D_MODEL:2 * D_MODEL]) + mod[:, 0:D_MODEL]


def _pack_transposed(w_ref, wb_s, src, tiles, dst, keep):
    lane = lax.broadcasted_iota(jnp.int32, (D_MODEL, LANE), 1)
    for t in range(tiles):
        blk = w_ref[src + t * LANE:src + (t + 1) * LANE, :].T
        if keep is not None:
            blk = jnp.where(lane < keep, blk, 0.0)
        wb_s[:, dst + t * LANE:dst + (t + 1) * LANE] = blk.astype(BF16)


def _in_proj_kernel(xp_ref, xs_ref, mod_ref, nw_ref, w_ref, zg_ref, zm_ref, zs_ref, wb_s, *, p_blocks, layer,
                    mod_idx):
    i = pl.program_id(0)

    @pl.when(i == 0)
    def _():
        for src, tiles, dst, keep in IN_PIECES:
            _pack_transposed(w_ref, wb_s, src, tiles, dst, keep)

    x = jnp.where(i < p_blocks, xp_ref[...], xs_ref[...])
    h = _mod_rmsnorm(x, nw_ref[layer:layer + 1, :], mod_ref[pl.ds(mod_idx(i), 1), :]).astype(BF16)
    z = jnp.dot(h, wb_s[...], preferred_element_type=F32)
    zg_ref[...] = z[:, ZG_BASE:ZG_BASE + ZG_W].astype(BF16)
    zm_ref[...] = z[:, ZM_BASE:ZM_BASE + ZM_W].astype(BF16)
    zs_ref[...] = z[:, ZS_BASE:ZS_BASE + ZS_W].astype(BF16)


def _two_group_rows(width, p_blocks):
    tm = ROW_TILE
    return (pl.BlockSpec((tm, width), lambda i: (jnp.minimum(i, p_blocks - 1), 0)),
            pl.BlockSpec((tm, width), lambda i: (jnp.maximum(i - p_blocks, 0), 0)))


def _in_proj(xp, xs, mod, mod_idx, nw, w_in_t, layer):
    tm = ROW_TILE
    p_blocks = xp.shape[0] // tm
    n = xp.shape[0] + xs.shape[0]
    return pl.pallas_call(
        functools.partial(_in_proj_kernel, p_blocks=p_blocks, layer=layer, mod_idx=mod_idx),
        grid=(n // tm,),
        in_specs=[
            *_two_group_rows(D_MODEL, p_blocks),
            _layer_spec((COND_ROWS, 3 * D_MODEL), layer),
            _vector_spec(D_MODEL),
            pl.BlockSpec((None, MERGE_COL, D_MODEL), lambda i: (layer, 0, 0), pipeline_mode=pl.Buffered(1)),
        ],
        out_specs=[
            pl.BlockSpec((tm, ZG_W), lambda i: (i, 0)),
            pl.BlockSpec((tm, ZM_W), lambda i: (i, 0)),
            pl.BlockSpec((tm, ZS_W), lambda i: (i, 0)),
        ],
        out_shape=[
            jax.ShapeDtypeStruct((n, ZG_W), BF16),
            jax.ShapeDtypeStruct((n, ZM_W), BF16),
            jax.ShapeDtypeStruct((n, ZS_W), BF16),
        ],
        scratch_shapes=[pltpu.VMEM((D_MODEL, PACK_W), BF16)],
        compiler_params=_params(("arbitrary",)),
        name="in_proj",
    )(xp, xs, mod, nw, w_in_t)


def _gla_kernel(*refs, nsteps, seq, nseq, has_ctx, layer):
    it = iter(refs)
    zg_ref = next(it)
    s0_ref = next(it) if has_ctx else None
    waf_ref, wab_ref, ba_ref, onorm_ref, o_ref, sfin_ref, la_s, o_s, st_s = (next(it) for _ in range(9))
    chains = [(g, d) for g in range(nseq) for d in (0, 1)]
    inv_norm = 1.0 / GLA_GATE_NORM
    zero_blk = jnp.zeros((GLA_DK, GLA_DV), F32)
    for ch, (g, d) in enumerate(chains):
        if d == 0:
            a_blk = zg_ref[g * seq:(g + 1) * seq, ZG_A:ZG_A + LANE]
        wa_ref = waf_ref if d == 0 else wab_ref
        a_low = _dot(a_blk, wa_ref[...]) + ba_ref[d:d + 1, :]
        la_s[ch] = (jnp.minimum(a_low, 0.0) - jnp.log(1.0 + jnp.exp(-jnp.abs(a_low)))) * inv_norm
        if has_ctx:
            s0 = s0_ref[g, d]
            rows_bd = []
            for h in range(GLA_HEADS):
                sh = s0[h * GLA_DK:(h + 1) * GLA_DK, :]
                rows_bd.append(jnp.concatenate([sh if h2 == h else zero_blk for h2 in range(GLA_HEADS)], axis=1))
            st_s[ch] = jnp.concatenate(rows_bd, axis=0).T
        else:
            st_s[ch] = jnp.zeros((GLA_WIDTH, GLA_QK), F32)

    def iota(shape, axis, shift):
        return lax.shift_right_logical(lax.broadcasted_iota(jnp.int32, shape, axis), shift)

    log_chunk, log_dv = CHUNK.bit_length() - 1, GLA_DV.bit_length() - 1
    row = lax.broadcasted_iota(jnp.int32, (GLA_STEP, GLA_STEP), 0)
    col = lax.broadcasted_iota(jnp.int32, (GLA_STEP, GLA_STEP), 1)
    same_chunk = iota((GLA_STEP, GLA_STEP), 0, log_chunk) == iota((GLA_STEP, GLA_STEP), 1, log_chunk)
    masks = (same_chunk & (row >= col), same_chunk & (row <= col))
    lane_head = iota((GLA_STEP, GLA_QK), 1, log_chunk)
    row_chunk = iota((GLA_STEP, GLA_QK), 0, log_chunk)
    state_blk = iota((GLA_WIDTH, GLA_QK), 0, log_dv) == iota((GLA_WIDTH, GLA_QK), 1, log_chunk)
    qscale = GLA_DK ** -0.5
    nch = GLA_STEP // CHUNK

    def chain_phases(i, ch, g, d):
        r0 = pl.multiple_of((i if d == 0 else nsteps - 1 - i) * GLA_STEP, GLA_STEP)
        zrows, rows = pl.ds(g * seq + r0, GLA_STEP), pl.ds(r0, GLA_STEP)
        a_hi, a_lo = _split_bf16(la_s[ch, rows, :], 2)
        tri = masks[d].astype(BF16)
        cum = (jnp.dot(tri, a_hi, preferred_element_type=F32)
               + jnp.dot(tri, a_lo, preferred_element_type=F32))
        yield
        edge = CHUNK - 1 if d == 0 else 0
        blast = [cum[c * CHUNK + edge:c * CHUNK + edge + 1, :] for c in range(nch)]
        bl = jnp.concatenate([jnp.broadcast_to(b, (CHUNK, GLA_QK)) for b in blast], axis=0)
        q = zg_ref[zrows, ZG_Q:ZG_Q + GLA_QK].astype(F32) * qscale
        k = zg_ref[zrows, ZG_K:ZG_K + GLA_QK].astype(F32)
        v = zg_ref[zrows, ZG_V:ZG_V + GLA_WIDTH]
        v_t = v.astype(F32).T.astype(BF16)
        qd = q * jnp.exp(cum)
        kd = (k * jnp.exp(-cum)).astype(BF16)
        kr = k * jnp.exp(bl - cum)
        yield
        outs = []
        for h in range(GLA_HEADS):
            qh = jnp.where(lane_head == h, qd, 0.0)
            att = _dot_nt(qh, kd)
            yield
            att = jnp.where(masks[d], att, 0.0)
            outs.append(_dot(att, v[:, h * GLA_DV:(h + 1) * GLA_DV]))
            yield
        s = st_s[ch]
        inter = [None] * nch
        for c in (range(nch) if d == 0 else reversed(range(nch))):
            inter[c] = _dot_nt(qd[c * CHUNK:(c + 1) * CHUNK, :], s)
            kv_t = _dot(v_t, jnp.where(row_chunk == c, kr, 0.0))
            yield
            s = s * jnp.exp(blast[c]) + jnp.where(state_blk, kv_t, 0.0)
            yield
        st_s[ch] = s
        o_s[ch, rows, :] = jnp.concatenate(outs, axis=1) + jnp.concatenate(inter, axis=0)

    def step(i, carry):
        _emit_skewed([chain_phases(i, ch, g, d) for ch, (g, d) in enumerate(chains)], GLA_SKEW)
        return carry

    lax.fori_loop(0, nsteps, step, 0)
    onorm = onorm_ref[layer:layer + 1, :]
    for ch, (g, d) in enumerate(chains):
        s_fin = st_s[ch].T
        for h in range(GLA_HEADS):
            sfin_ref[g, d, h * GLA_DK:(h + 1) * GLA_DK, :] = (
                s_fin[h * GLA_DK:(h + 1) * GLA_DK, h * GLA_DV:(h + 1) * GLA_DV])
    for g in range(nseq):
        srows = slice(g * seq, (g + 1) * seq)
        o = o_s[2 * g] + o_s[2 * g + 1]
        gate = zg_ref[srows, ZG_GATE:ZG_GATE + GLA_WIDTH].astype(F32)
        for h in range(GLA_HEADS):
            vs = slice(h * GLA_DV, (h + 1) * GLA_DV)
            oh = o[:, vs]
            ms = jnp.mean(oh * oh, axis=-1, keepdims=True)
            o_ref[srows, vs] = oh * lax.rsqrt(ms + EPS) * onorm * jax.nn.silu(gate[:, vs])


def _gla(zg, row0, ctx, layer, waf, wab, ba, onorm, bsz, seq):
    nseq = max(GLA_SEQS_PER_STEP, GLA_ROWS_PER_STEP // seq)
    blk0 = row0 // (nseq * seq)
    in_specs = [pl.BlockSpec((nseq * seq, ZG_W), lambda b: (b + blk0, 0))]
    args = [zg]
    if ctx is not None:
        in_specs.append(pl.BlockSpec((nseq, None, 2, GLA_QK, GLA_DV), lambda b: (b, layer, 0, 0, 0)))
        args.append(ctx)
    in_specs += [
        _layer_spec((LANE, GLA_QK), layer),
        _layer_spec((LANE, GLA_QK), layer),
        _layer_spec((2, GLA_QK), layer),
        _vector_spec(GLA_DV),
    ]
    return pl.pallas_call(
        functools.partial(_gla_kernel, nsteps=seq // GLA_STEP, seq=seq, nseq=nseq, has_ctx=ctx is not None,
                          layer=layer),
        grid=(bsz // nseq,),
        in_specs=in_specs,
        out_specs=[
            pl.BlockSpec((nseq * seq, GLA_WIDTH), lambda b: (b, 0)),
            pl.BlockSpec((nseq, 2, GLA_QK, GLA_DV), lambda b: (b, 0, 0, 0)),
        ],
        out_shape=[
            jax.ShapeDtypeStruct((bsz * seq, GLA_WIDTH), F32),
            jax.ShapeDtypeStruct((bsz, 2, GLA_QK, GLA_DV), F32),
        ],
        scratch_shapes=[
            pltpu.VMEM((2 * nseq, seq, GLA_QK), F32),
            pltpu.VMEM((2 * nseq, seq, GLA_WIDTH), F32),
            pltpu.VMEM((2 * nseq, GLA_WIDTH, GLA_QK), F32),
        ],
        compiler_params=_params(("parallel",)),
        name="gla",
    )(*args, waf, wab, ba, onorm)


def _rms(x, w):
    ms = jnp.mean(x * x, axis=-1, keepdims=True)
    return x * lax.rsqrt(ms + EPS) * w


def _head_sums_mxu(x):
    width = x.shape[-1]
    shift = HEAD_PAD.bit_length() - 1
    gi = lax.shift_right_logical(lax.broadcasted_iota(jnp.int32, (width, width), 0), shift)
    gj = lax.shift_right_logical(lax.broadcasted_iota(jnp.int32, (width, width), 1), shift)
    return _dot(x * x, jnp.where(gi == gj, 1.0, 0.0))


def _head_norm(x, w, rope, on_mxu):
    sums = _head_sums_mxu(x) if on_mxu else None
    outs = []
    for h in range(MLA_HEADS):
        hs = slice(h * HEAD_PAD, (h + 1) * HEAD_PAD)
        xh = x[:, hs]
        ss = sums[:, hs] if on_mxu else jnp.sum(xh * xh, axis=-1, keepdims=True)
        yh = xh * lax.rsqrt(ss * (1.0 / MLA_QK) + EPS) * w
        if rope is not None:
            c, s = rope
            yh = yh * c + pltpu.roll(yh, ROPE_SHIFT, 1) * s
        outs.append(yh)
    return outs


def _place_rope_key(kr, e):
    return sum(jnp.dot(p, e, preferred_element_type=F32) for p in _split_bf16(kr, 3))


def _mla_kernel(*refs, seq, nseq, n_ctx, use_rope, layer):
    it = iter(refs)
    zm_ref = next(it)
    if n_ctx:
        cckv_ref, ckr_ref = next(it), next(it)
    qn_ref, wuq_ref, kvn_ref, wuk_ref, wuv_ref, qhn_ref, khn_ref, e_ref = (next(it) for _ in range(8))
    rope_ref = next(it) if use_rope else None
    o_ref, ckv_ref = next(it), next(it)
    q_s, k_s, v_s = next(it), next(it), next(it)
    qn, kvn, qhn, khn = (r[layer:layer + 1, :] for r in (qn_ref, kvn_ref, qhn_ref, khn_ref))

    qscale = MLA_QK ** -0.5
    heads = [slice(h * HEAD_PAD, (h + 1) * HEAD_PAD) for h in range(MLA_HEADS)]

    def keys_values(g, ckv, k_rope_placed, rope, k_rows):
        k_raw = _dot(ckv, wuk_ref[...]) + k_rope_placed
        yield
        kh = _head_norm(k_raw, khn, rope, False)
        for h, hs in enumerate(heads):
            k_s[g, k_rows, hs] = kh[h].astype(BF16)
        yield
        v_s[g, k_rows, :] = _dot(ckv, wuv_ref[...]).astype(BF16)
        yield

    def latent_phases(i, g):
        r0 = pl.multiple_of(i * PROJ_TILE, PROJ_TILE)
        tile, rows = pl.ds(r0, PROJ_TILE), pl.ds(g * seq + r0, PROJ_TILE)
        rope = (rope_ref[0, tile, :], rope_ref[1, tile, :]) if use_rope else None
        ckv = _rms(zm_ref[rows, ZM_KV:ZM_KV + MLA_KV_LORA].astype(F32), kvn)
        ckv_ref[rows, :] = ckv
        k_pe = jnp.dot(zm_ref[rows, ZM_KR:ZM_KR + LANE], e_ref[...], preferred_element_type=F32)
        yield from keys_values(g, ckv, k_pe, rope, pl.ds(n_ctx + r0, PROJ_TILE))
        cq = _rms(zm_ref[rows, ZM_Q:ZM_Q + MLA_Q_LORA].astype(F32), qn)
        q_raw = _dot(cq, wuq_ref[...])
        yield
        qh = _head_norm(q_raw, qhn, rope, True)
        for h, hs in enumerate(heads):
            q_s[rows, hs] = (qh[h] * qscale).astype(BF16)

    def latent_tile(i, carry):
        _emit_skewed([latent_phases(i, g) for g in range(nseq)], 0)
        return carry

    lax.fori_loop(0, seq // PROJ_TILE, latent_tile, 0)

    def context_tile(i, carry):
        rows = pl.ds(pl.multiple_of(i * PROJ_TILE, PROJ_TILE), PROJ_TILE)
        _emit_skewed([keys_values(g, cckv_ref[g, rows, :], _place_rope_key(ckr_ref[g, rows, :], e_ref[...]),
                                  None, rows) for g in range(nseq)], 0)
        return carry

    if n_ctx:
        lax.fori_loop(0, n_ctx // PROJ_TILE, context_tile, 0)

    q_tile = min(seq, Q_TILE)

    def head_phases(g, h, hs, rows, gate):
        s = lax.dot_general(q_s[rows, hs], k_s[g, :, hs], (((1,), (1,)), ((), ())),
                            preferred_element_type=F32)
        yield
        e = jnp.exp(s - jnp.max(s, axis=-1, keepdims=True))
        l = jnp.sum(e, axis=-1, keepdims=True)
        p = e.astype(BF16)
        yield
        o = jnp.dot(p, v_s[g, :, hs], preferred_element_type=F32) / l
        o_ref[rows, hs] = o * jax.nn.silu(gate[:, hs])

    def q_block(i, carry):
        chains = []
        for g in range(nseq):
            rows = pl.ds(g * seq + pl.multiple_of(i * q_tile, q_tile), q_tile)
            gate = zm_ref[rows, ZM_GATE:ZM_GATE + MLA_WIDTH].astype(F32)
            chains += [head_phases(g, h, hs, rows, gate) for h, hs in enumerate(heads)]
        _emit_skewed(chains, 0)
        return carry

    lax.fori_loop(0, seq // q_tile, q_block, 0)


def _mla(zm, row0, ctx, layer, w, rope_tab, bsz, seq):
    n_ctx = 0 if ctx is None else ctx[0].shape[-2]
    nseq = max(1, MLA_ROWS_PER_STEP // seq)
    blk0 = row0 // (nseq * seq)
    in_specs = [pl.BlockSpec((nseq * seq, ZM_W), lambda b: (b + blk0, 0))]
    args = [zm]
    if ctx is not None:
        cckv, ckr = ctx
        in_specs += [
            pl.BlockSpec((nseq, None, n_ctx, MLA_KV_LORA), lambda b: (b, layer, 0, 0)),
            pl.BlockSpec((nseq, None, n_ctx, LANE), lambda b: (b, layer, 0, 0)),
        ]
        args += [cckv, ckr]
    in_specs += [
        _vector_spec(MLA_Q_LORA),
        _layer_spec((MLA_Q_LORA, MLA_HEADS * HEAD_PAD), layer),
        _vector_spec(MLA_KV_LORA),
        _layer_spec((MLA_KV_LORA, MLA_HEADS * HEAD_PAD), layer),
        _layer_spec((MLA_KV_LORA, MLA_WIDTH), layer),
        _vector_spec(HEAD_PAD),
        _vector_spec(HEAD_PAD),
        pl.BlockSpec((LANE, MLA_HEADS * HEAD_PAD), lambda b: (0, 0)),
    ]
    args += list(w)
    if rope_tab is not None:
        in_specs.append(pl.BlockSpec((2, seq, HEAD_PAD), lambda b: (0, 0, 0)))
        args.append(rope_tab)
    return pl.pallas_call(
        functools.partial(_mla_kernel, seq=seq, nseq=nseq, n_ctx=n_ctx, use_rope=rope_tab is not None,
                          layer=layer),
        grid=(bsz // nseq,),
        in_specs=in_specs,
        out_specs=[
            pl.BlockSpec((nseq * seq, MLA_WIDTH), lambda b: (b, 0)),
            pl.BlockSpec((nseq * seq, MLA_KV_LORA), lambda b: (b, 0)),
        ],
        out_shape=[
            jax.ShapeDtypeStruct((bsz * seq, MLA_WIDTH), F32),
            jax.ShapeDtypeStruct((bsz * seq, MLA_KV_LORA), F32),
        ],
        scratch_shapes=[
            pltpu.VMEM((nseq * seq, MLA_HEADS * HEAD_PAD), BF16),
            pltpu.VMEM((nseq, n_ctx + seq, MLA_HEADS * HEAD_PAD), BF16),
            pltpu.VMEM((nseq, n_ctx + seq, MLA_WIDTH), BF16),
        ],
        compiler_params=_params(("parallel",)),
        name="mla",
    )(*args)


S5_T = 8
S5_R = CHUNK // S5_T
S5_SUB_CH = 64
S5_SUBS = LANE // S5_SUB_CH
S5_SUB_STATE = S5_TILE_STATE // S5_SUBS
S5_ROW = S5_T * S5_SUB_CH
S5_W = 2 * S5_SUB_STATE
W_M, W_SF, W_SB, W_CF, W_CB = range(5)


def _cmul(ar, ai, br, bi):
    return ar * br - ai * bi, ar * bi + ai * br


def _s5_prep_kernel(are_ref, aim_ref, ldt_ref, bre_ref, bim_ref, cre_ref, cim_ref, d_ref,
                    w_ref, tab8_ref, tab1_ref):
    gr = lax.shift_right_logical(lax.broadcasted_iota(jnp.int32, (S5_SUB_CH, S5_SUB_STATE), 0),
                                 S5_GROUP.bit_length() - 1)
    gc = lax.shift_right_logical(lax.broadcasted_iota(jnp.int32, (S5_SUB_CH, S5_SUB_STATE), 1),
                                 S5_STATE.bit_length() - 1)

    def spread(ref, h):
        x = ref[h * S5_SUB_CH:(h + 1) * S5_SUB_CH, :]
        return jnp.where(gr == gc, jnp.concatenate([x] * (S5_SUB_CH // S5_GROUP), axis=1), 0.0)

    row = lax.broadcasted_iota(jnp.int32, (S5_SUB_CH, S5_SUB_CH), 0)
    col = lax.broadcasted_iota(jnp.int32, (S5_SUB_CH, S5_SUB_CH), 1)
    taps = [[[], []] for _ in range(S5_SUBS)]
    for d in (0, 1):
        prow = pl.ds(2 * pl.program_id(0) + d, 1)
        a_re, a_im = are_ref[prow, :], aim_ref[prow, :]
        dt = jnp.exp(ldt_ref[prow, :])
        lam = a_re * dt
        th = a_im * dt
        mag = jnp.exp(lam)
        ab_re = mag * jnp.cos(th)
        ab_im = mag * jnp.sin(th)
        den = a_re * a_re + a_im * a_im
        n_re = ab_re - 1.0
        cf_re = (n_re * a_re + ab_im * a_im) / den
        cf_im = (ab_im * a_re - n_re * a_im) / den
        k = lax.broadcasted_iota(jnp.int32, (2 * S5_T, S5_TILE_STATE), 0).astype(F32)
        pmag = jnp.exp(k * lam)
        pw_re = pmag * jnp.cos(k * th)
        pw_im = pmag * jnp.sin(k * th)
        for h in range(S5_SUBS):
            ss = slice(h * S5_SUB_STATE, (h + 1) * S5_SUB_STATE)
            c_re, c_im = spread(cre_ref, h), spread(cim_ref, h)
            c_cat = jnp.concatenate([c_re, c_im], axis=1).astype(BF16)
            bp_re, bp_im = _cmul(spread(bre_ref, h), spread(bim_ref, h), cf_re[:, ss], cf_im[:, ss])
            for p in range(S5_T + 1):
                ar, ai = pw_re[p:p + 1, ss], pw_im[p:p + 1, ss]
                t_in = S5_T - 1 - p if d == 0 else p
                t_out = p - 1 if d == 0 else S5_T - p
                if p < S5_T:
                    l_re, l_im = _cmul(bp_re, bp_im, ar, ai)
                    w_ref[h, W_SF + d, t_in * S5_SUB_CH:(t_in + 1) * S5_SUB_CH, :] = (
                        jnp.concatenate([l_re, l_im], axis=1).astype(BF16))
                    taps[h][d].append(_dot_nt(jnp.concatenate([l_re, -l_im], axis=1), c_cat))
                if p > 0:
                    v_re, v_im = _cmul(c_re, c_im, ar, ai)
                    w_ref[h, W_CF + d, t_out * S5_SUB_CH:(t_out + 1) * S5_SUB_CH, :] = (
                        jnp.concatenate([v_re, -v_im], axis=1).astype(BF16))
        r = lax.broadcasted_iota(jnp.int32, (S5_R, S5_TILE_STATE), 0).astype(F32) * float(S5_T)
        r1 = float(S5_T * (S5_R - 1)) - r
        pm = jnp.exp(r * lam)
        qm = jnp.exp(r1 * lam)
        tab8_ref[d, 0] = pm * jnp.cos(r * th)
        tab8_ref[d, 1] = pm * jnp.sin(r * th)
        tab8_ref[d, 2] = qm * jnp.cos(r1 * th)
        tab8_ref[d, 3] = qm * jnp.sin(r1 * th)
        mc = jnp.exp(float(CHUNK) * lam)
        tab1_ref[d, 0:1, :] = mc * jnp.cos(float(CHUNK) * th)
        tab1_ref[d, 1:2, :] = mc * jnp.sin(float(CHUNK) * th)
    for h in range(S5_SUBS):
        skip = jnp.where(row == col, d_ref[:, h * S5_SUB_CH:(h + 1) * S5_SUB_CH], 0.0)
        for t in range(S5_T):
            blocks = []
            for t2 in range(S5_T):
                if t < t2:
                    blocks.append(taps[h][0][t2 - t])
                elif t > t2:
                    blocks.append(taps[h][1][t - t2])
                else:
                    blocks.append(taps[h][0][0] + taps[h][1][0] + skip)
            w_ref[h, W_M, t * S5_SUB_CH:(t + 1) * S5_SUB_CH, :] = jnp.concatenate(blocks, axis=1).astype(BF16)


def _s5_prep(a_re, a_im, ldt, b_re, b_im, c_re, c_im, dsk):
    vec = pl.BlockSpec((2 * DEPTH, S5_TILE_STATE), lambda l, j: (0, j))
    blk = pl.BlockSpec((None, None, LANE, S5_STATE), lambda l, j: (l, j, 0, 0))
    return pl.pallas_call(
        _s5_prep_kernel,
        grid=(DEPTH, S5_TILES),
        in_specs=[vec, vec, vec, blk, blk, blk, blk,
                  pl.BlockSpec((None, 1, LANE), lambda l, j: (l, 0, j))],
        out_specs=[
            pl.BlockSpec((None, None, S5_SUBS, 5, S5_ROW, S5_W), lambda l, j: (l, j, 0, 0, 0, 0)),
            pl.BlockSpec((None, None, 2, 4, S5_R, S5_TILE_STATE), lambda l, j: (l, j, 0, 0, 0, 0)),
            pl.BlockSpec((None, None, 2, 2, S5_TILE_STATE), lambda l, j: (l, j, 0, 0, 0)),
        ],
        out_shape=[
            jax.ShapeDtypeStruct((DEPTH, S5_TILES, S5_SUBS, 5, S5_ROW, S5_W), BF16),
            jax.ShapeDtypeStruct((DEPTH, S5_TILES, 2, 4, S5_R, S5_TILE_STATE), F32),
            jax.ShapeDtypeStruct((DEPTH, S5_TILES, 2, 2, S5_TILE_STATE), F32),
        ],
        compiler_params=_params(("parallel", "parallel")),
        name="s5_prep",
    )(a_re, a_im, ldt, b_re, b_im, c_re, c_im, dsk)


def _s5_scan_kernel(u_ref, x0_ref, w_ref, tab8_ref, tab1_ref, y_ref, fs_ref, u_s, *, nseq, nb):
    groups = nseq * nb
    nrow = groups * S5_R
    ts = S5_SUB_STATE
    u_s[...] = u_ref[...].astype(F32)
    tokens = [u_s[pl.ds(t, nrow, stride=S5_T), :] for t in range(S5_T)]
    rowi = lax.broadcasted_iota(jnp.int32, (groups, S5_R, ts), 1)

    def shift(x, n, down):
        if down:
            return jnp.where(rowi >= n, pltpu.roll(x, n, 1), 0.0)
        return jnp.where(rowi < S5_R - n, pltpu.roll(x, S5_R - n, 1), 0.0)

    def block_scan(e_re, e_im, p_re, p_im, down):
        row8 = lax.broadcasted_iota(jnp.int32, (S5_R, ts), 0)
        for n in (1, 2, 4):
            keep = row8 >= n if down else row8 < S5_R - n
            c_re, c_im = jnp.where(keep, p_re[n:n + 1, :], 0.0), jnp.where(keep, p_im[n:n + 1, :], 0.0)
            turn = n if down else S5_R - n
            s_re, s_im = _cmul(c_re, c_im, pltpu.roll(e_re, turn, 1), pltpu.roll(e_im, turn, 1))
            e_re, e_im = e_re + s_re, e_im + s_im
        return e_re, e_im

    y_sub = []
    for h in range(S5_SUBS):
        ch = slice(h * S5_SUB_CH, (h + 1) * S5_SUB_CH)
        ss = slice(h * ts, (h + 1) * ts)
        u8 = jnp.concatenate([tok[:, ch] for tok in tokens], axis=1).astype(BF16)
        ef = jnp.dot(u8, w_ref[h, W_SF], preferred_element_type=F32).reshape(groups, S5_R, S5_W)
        eb = jnp.dot(u8, w_ref[h, W_SB], preferred_element_type=F32).reshape(groups, S5_R, S5_W)

        p_re, p_im, q_re, q_im = (tab8_ref[0, i, :, ss] for i in range(4))
        a_re, a_im = tab1_ref[0, 0:1, ss], tab1_ref[0, 1:2, ss]
        cs_re, cs_im = block_scan(ef[:, :, :ts], ef[:, :, ts:], p_re, p_im, True)
        st_re, st_im = [], []
        for s in range(nseq):
            x_re, x_im = x0_ref[s, 0, 0:1, ss], x0_ref[s, 0, 1:2, ss]
            for b in range(nb):
                g = s * nb + b
                st_re.append(x_re)
                st_im.append(x_im)
                k_re, k_im = _cmul(a_re, a_im, x_re, x_im)
                x_re = k_re + cs_re[g, S5_R - 1:S5_R, :]
                x_im = k_im + cs_im[g, S5_R - 1:S5_R, :]
            fs_ref[s, 0, 0:1, ss] = x_re
            fs_ref[s, 0, 1:2, ss] = x_im
        k_re, k_im = _cmul(p_re, p_im, jnp.stack(st_re), jnp.stack(st_im))
        xin = jnp.concatenate([shift(cs_re, 1, True) + k_re, shift(cs_im, 1, True) + k_im],
                              axis=2).reshape(nrow, S5_W)

        p_re, p_im, q_re, q_im = (tab8_ref[1, i, :, ss] for i in range(4))
        a_re, a_im = tab1_ref[1, 0:1, ss], tab1_ref[1, 1:2, ss]
        sf_re, sf_im = block_scan(eb[:, :, :ts], eb[:, :, ts:], p_re, p_im, False)
        z_re, z_im = [None] * groups, [None] * groups
        for s in range(nseq):
            x_re, x_im = x0_ref[s, 1, 0:1, ss], x0_ref[s, 1, 1:2, ss]
            for b in reversed(range(nb)):
                g = s * nb + b
                z_re[g], z_im[g] = x_re, x_im
                k_re, k_im = _cmul(a_re, a_im, x_re, x_im)
                x_re = sf_re[g, 0:1, :] + k_re
                x_im = sf_im[g, 0:1, :] + k_im
            fs_ref[s, 1, 0:1, ss] = x_re
            fs_ref[s, 1, 1:2, ss] = x_im
        k_re, k_im = _cmul(q_re, q_im, jnp.stack(z_re), jnp.stack(z_im))
        xnx = jnp.concatenate([shift(sf_re, 1, False) + k_re, shift(sf_im, 1, False) + k_im],
                              axis=2).reshape(nrow, S5_W)

        y_sub.append(jnp.dot(u8, w_ref[h, W_M], preferred_element_type=F32)
                     + _dot_nt(xin, w_ref[h, W_CF]) + _dot_nt(xnx, w_ref[h, W_CB]))
    for t in range(S5_T):
        tc = slice(t * S5_SUB_CH, (t + 1) * S5_SUB_CH)
        y_ref[pl.ds(t, nrow, stride=S5_T), :] = jnp.concatenate([y[:, tc] for y in y_sub], axis=1)


def _s5_scan(zs, row0, x0, x0_block, x0_idx, layer, wmat, tab8, tab1, nseq, seq):
    n = nseq * seq
    rblk = row0 // n
    return pl.pallas_call(
        functools.partial(_s5_scan_kernel, nseq=nseq, nb=seq // CHUNK),
        grid=(S5_TILES,),
        in_specs=[
            pl.BlockSpec((n, LANE), lambda j: (rblk, ZS_U // LANE + j)),
            pl.BlockSpec(x0_block, x0_idx),
            pl.BlockSpec((None, None, S5_SUBS, 5, S5_ROW, S5_W), lambda j: (layer, j, 0, 0, 0, 0)),
            pl.BlockSpec((None, None, 2, 4, S5_R, S5_TILE_STATE), lambda j: (layer, j, 0, 0, 0, 0)),
            pl.BlockSpec((None, None, 2, 2, S5_TILE_STATE), lambda j: (layer, j, 0, 0, 0)),
        ],
        out_specs=[
            pl.BlockSpec((n, LANE), lambda j: (0, j)),
            pl.BlockSpec((nseq, 2, 2, S5_TILE_STATE), lambda j: (0, 0, 0, j)),
        ],
        out_shape=[
            jax.ShapeDtypeStruct((n, S5_WIDTH), F32),
            jax.ShapeDtypeStruct((nseq, 2, 2, S5_NSTATE), F32),
        ],
        scratch_shapes=[pltpu.VMEM((n, LANE), F32)],
        compiler_params=_params(("parallel",)),
        name="s5_scan",
    )(zs, x0, wmat, tab8, tab1)


def _merge_kernel(x_ref, mod_ref, nw_ref, oa_ref, ob_ref, ys_ref, sg_ref, wglu_hbm, bglu_ref, wmg_hbm,
                  wa_hbm, wb_hbm, wc_hbm, wout_hbm, y_ref, wmg_s, wmg_v, wglu_ref, wa_ref, wb_ref, wc_ref,
                  wout_ref, sem, *, layer, mod_idx):
    @pl.when(pl.program_id(0) == 0)
    def _():
        copies = [pltpu.make_async_copy(wmg_hbm.at[layer, pl.ds(MERGE_COL + br * D_MODEL, D_MODEL), :],
                                        wmg_v.at[pl.ds(br * D_MODEL, D_MODEL), :], sem.at[br])
                  for br in range(3)]
        copies += [pltpu.make_async_copy(src.at[layer], dst, sem.at[3 + k])
                   for k, (src, dst) in enumerate(((wglu_hbm, wglu_ref), (wa_hbm, wa_ref), (wb_hbm, wb_ref),
                                                   (wc_hbm, wc_ref), (wout_hbm, wout_ref)))]
        for cp in copies:
            cp.start()
        for br in range(3):
            copies[br].wait()
            _pack_transposed(wmg_v, wmg_s, br * D_MODEL, D_MODEL // LANE, br * D_MODEL, None)
        for cp in copies[3:]:
            cp.wait()

    x = x_ref[...]
    mod = mod_ref[pl.ds(mod_idx(pl.program_id(0)), 1), :]
    h = _mod_rmsnorm(x, nw_ref[layer:layer + 1, :], mod).astype(BF16)
    zg = _dot(jax.nn.gelu(ys_ref[...]), wglu_ref[...]) + bglu_ref[layer:layer + 1, :]
    oc = (zg[:, :S5_WIDTH] * jax.nn.sigmoid(zg[:, S5_WIDTH:])
          * jax.nn.silu(sg_ref[...].astype(F32)))
    mixed = None
    for br, (o_br, w_ref) in enumerate(((oa_ref[...], wa_ref), (ob_ref[...], wb_ref), (oc, wc_ref))):
        gate = jax.nn.sigmoid(jnp.dot(h, wmg_s[:, br * D_MODEL:(br + 1) * D_MODEL], preferred_element_type=F32))
        term = gate * _dot(o_br, w_ref[...])
        mixed = term if mixed is None else mixed + term
    y_ref[...] = x + mod[:, 2 * D_MODEL:] * _dot(mixed, wout_ref[...])


def _merge(x2, row0, mod, mod_idx, nw, oa, ob, ys, zs, layer, wglu, bglu, w_in_t, wa, wb, wc, wout):
    n = x2.shape[0]
    tm = ROW_TILE
    blk0 = row0 // tm
    in_hbm = pl.BlockSpec(memory_space=pl.ANY)
    rows = lambda w: pl.BlockSpec((tm, w), lambda i: (i, 0))
    return pl.pallas_call(
        functools.partial(_merge_kernel, layer=layer, mod_idx=mod_idx),
        grid=(n // tm,),
        in_specs=[
            rows(D_MODEL),
            _layer_spec((COND_ROWS, 3 * D_MODEL), layer),
            _vector_spec(D_MODEL),
            rows(GLA_WIDTH), rows(MLA_WIDTH), rows(S5_WIDTH),
            pl.BlockSpec((tm, S5_WIDTH), lambda i: (i + blk0, ZS_GATE // S5_WIDTH)),
            in_hbm,
            _vector_spec(2 * S5_WIDTH),
            in_hbm, in_hbm, in_hbm, in_hbm, in_hbm,
        ],
        out_specs=rows(D_MODEL),
        out_shape=jax.ShapeDtypeStruct((n, D_MODEL), F32),
        scratch_shapes=[pltpu.VMEM((D_MODEL, 3 * D_MODEL), BF16),
                        pltpu.VMEM((3 * D_MODEL, D_MODEL), F32),
                        pltpu.VMEM((S5_WIDTH, 2 * S5_WIDTH), F32),
                        pltpu.VMEM((GLA_WIDTH, D_MODEL), F32),
                        pltpu.VMEM((MLA_WIDTH, D_MODEL), F32),
                        pltpu.VMEM((S5_WIDTH, D_MODEL), F32),
                        pltpu.VMEM((D_MODEL, D_MODEL), F32),
                        pltpu.SemaphoreType.DMA((8,))],
        compiler_params=_params(("arbitrary",)),
        name="merge",
    )(x2, mod, nw, oa, ob, ys, zs, wglu, bglu, w_in_t, wa, wb, wc, wout)


def _mla_lane_of_dim():
    half = MLA_ROPE // 2
    first_gap = ROPE_SHIFT - half
    lane = np.zeros(MLA_QK, np.int32)
    for j in range(MLA_NOPE):
        lane[j] = half + j if j < first_gap else 2 * half + j
    for r in range(half):
        lane[MLA_NOPE + r] = r
        lane[MLA_NOPE + half + r] = ROPE_SHIFT + r
    return lane


MLA_LANE_OF_DIM = _mla_lane_of_dim()


def _place_heads(w, heads, lane_of_dim):
    width = len(lane_of_dim)
    order = np.argsort(lane_of_dim)
    zeros = lambda n: jnp.zeros(w.shape[:-1] + (n,), w.dtype)
    pieces = []
    for h in range(heads):
        lane, i = 0, 0
        while i < width:
            j = i
            while (j + 1 < width and order[j + 1] == order[j] + 1
                   and lane_of_dim[order[j + 1]] == lane_of_dim[order[j]] + 1):
                j += 1
            dst = int(lane_of_dim[order[i]])
            if dst > lane:
                pieces.append(zeros(dst - lane))
            pieces.append(w[..., h * width + int(order[i]):h * width + int(order[j]) + 1])
            lane, i = dst + (j - i + 1), j + 1
        if lane < HEAD_PAD:
            pieces.append(zeros(HEAD_PAD - lane))
    return jnp.concatenate(pieces, axis=-1)


def _place_heads_bf16(w, heads, lane_of_dim):
    width = len(lane_of_dim)
    place = np.zeros((heads * width, heads * HEAD_PAD), np.float32)
    for h in range(heads):
        place[h * width + np.arange(width), h * HEAD_PAD + lane_of_dim] = 1.0
    return jnp.dot(w.astype(BF16), jnp.asarray(place, BF16), preferred_element_type=BF16)


def _rope_tables(n_tok):
    rows = n_tok // GRID_W
    r = jnp.repeat(jnp.arange(rows, dtype=F32), GRID_W)
    col = jnp.tile(jnp.arange(GRID_W, dtype=F32), rows)
    n_freq = MLA_ROPE // 4
    inv = ROPE_THETA ** (-jnp.arange(n_freq, dtype=F32) / n_freq)
    ang = jnp.concatenate([r[:, None] * inv, col[:, None] * inv], axis=-1)
    cos, sin = jnp.cos(ang), jnp.sin(ang)
    ones = jnp.ones((n_tok, MLA_NOPE), F32)
    c = _place_heads(jnp.concatenate([ones, cos, cos], axis=1), 1, MLA_LANE_OF_DIM)
    s = _place_heads(jnp.concatenate([0.0 * ones, -sin, sin], axis=1), 1, MLA_LANE_OF_DIM)
    return jnp.stack([c, s])


def kernel(x_prompt, x_sample, c, c_ctx, cache_mla_ckv, cache_mla_krope, state_gla, state_s5,
           norm_w, w_ada, b_ada, w_in, gla_w_a2, gla_b_a, gla_o_norm,
           mla_q_norm, mla_w_uq, mla_kv_norm, mla_w_uk, mla_w_uv, mla_qh_norm, mla_kh_norm,
           s5_a_re, s5_a_im, s5_log_dt, s5_b_re, s5_b_im, s5_c_re, s5_c_im, s5_d, s5_w_glu, s5_b_glu,
           w_bo_gla, w_bo_mla, w_bo_s5, w_out):
    bsz, seq, _ = x_prompt.shape
    dbsz, dseq, _ = x_sample.shape
    ctx_row = COND_ROWS - 1
    assert dbsz <= ctx_row and (bsz * seq) % ROW_TILE == 0 and dseq % ROW_TILE == 0

    cond = jnp.zeros((COND_ROWS, D_MODEL), F32).at[0:dbsz].set(c).at[ctx_row].set(c_ctx)
    ada = _ada(cond, w_ada, b_ada)

    vec = lambda a: a.reshape(2 * DEPTH, S5_NSTATE)
    ldt = jnp.repeat(s5_log_dt[..., None], S5_STATE, axis=-1)
    rows_gp = lambda t: t.reshape(DEPTH, S5_TILES, LANE, S5_STATE)
    bt = lambda b: rows_gp(b.transpose(0, 1, 3, 2))
    wmat, tab8, tab1 = _s5_prep(vec(s5_a_re), vec(s5_a_im), vec(ldt), bt(s5_b_re), bt(s5_b_im),
                                rows_gp(s5_c_re), rows_gp(s5_c_im), s5_d.reshape(DEPTH, 1, S5_WIDTH))

    wuq = _place_heads_bf16(mla_w_uq, MLA_HEADS, MLA_LANE_OF_DIM)
    wuk = _place_heads_bf16(mla_w_uk, MLA_HEADS, MLA_LANE_OF_DIM[:MLA_NOPE])
    wuv = mla_w_uv.astype(BF16)
    qhn = _place_heads(mla_qh_norm, 1, MLA_LANE_OF_DIM)
    khn = _place_heads(mla_kh_norm, 1, MLA_LANE_OF_DIM)
    e_np = np.zeros((LANE, MLA_HEADS * HEAD_PAD), np.float32)
    for h in range(MLA_HEADS):
        for i in range(MLA_ROPE):
            e_np[i, h * HEAD_PAD + MLA_LANE_OF_DIM[MLA_NOPE + i]] = 1.0
    e_place = jnp.asarray(e_np, BF16)
    rope_tab = _rope_tables(dseq)
    ckr_pad = jnp.pad(cache_mla_krope, ((0, 0), (0, 0), (0, 0), (0, LANE - MLA_ROPE)))

    zrow = lambda n: jnp.zeros((DEPTH, n, GLA_QK), F32)
    waf = jnp.concatenate([gla_w_a2[:, 0], zrow(LANE - GLA_RANK)], axis=1).astype(BF16)
    wab = jnp.concatenate([zrow(GLA_RANK), gla_w_a2[:, 1], zrow(LANE - 2 * GLA_RANK)], axis=1).astype(BF16)
    sgla = state_gla.reshape(dbsz, DEPTH, 2, GLA_QK, GLA_DV)
    ss5 = state_s5.reshape(dbsz, DEPTH, 2, 2, S5_NSTATE)
    zero_s5 = jnp.zeros((bsz, 2, 2, S5_NSTATE), F32)

    hp = x_prompt.reshape(bsz * seq, D_MODEL)
    hs = x_sample.reshape(dbsz * dseq, D_MODEL)
    ckv_l, krope_l, gla_l, s5_l = [], [], [], []
    w_in_t = jnp.swapaxes(w_in, 1, 2)
    mod = ada
    mla_w = (mla_q_norm, wuq, mla_kv_norm, wuk, wuv, qhn, khn, e_place)
    wbo = (w_bo_gla, w_bo_mla, w_bo_s5)
    for l in range(DEPTH):
        p_rows, p_blocks, blocks_per_seq = bsz * seq, bsz * seq // ROW_TILE, dseq // ROW_TILE
        mod_idx = lambda i: jnp.where(i < p_blocks, ctx_row, (i - p_blocks) // blocks_per_seq)
        zg, zm, zs = _in_proj(hp, hs, mod, mod_idx, norm_w, w_in_t, l)

        def mixers(x2, row0, nb, n, ctx):
            if ctx:
                gctx = sgla
                x0, x0_blk = ss5, (nb, None, 2, 2, S5_TILE_STATE)
                x0_idx = lambda j: (0, l, 0, 0, j)
                mctx, rt = (cache_mla_ckv, ckr_pad), rope_tab
            else:
                gctx = None
                x0, x0_blk = zero_s5, (nb, 2, 2, S5_TILE_STATE)
                x0_idx = lambda j: (0, 0, 0, j)
                mctx, rt = None, None
            oa, st_gla = _gla(zg, row0, gctx, l, waf, wab, gla_b_a, gla_o_norm, nb, n)
            ob, ckv = _mla(zm, row0, mctx, l, mla_w, rt, nb, n)
            y_ssm, st_s5 = _s5_scan(zs, row0, x0, x0_blk, x0_idx, l, wmat, tab8, tab1, nb, n)
            grp_mod_idx = lambda i: mod_idx(i + row0 // ROW_TILE)
            y = _merge(x2, row0, mod, grp_mod_idx, norm_w, oa, ob, y_ssm, zs, l, s5_w_glu, s5_b_glu, w_in_t, *wbo, w_out)
            return y, ckv, st_gla, st_s5

        hp_next, ckv_p, st_gla_p, st_s5_p = mixers(hp, 0, bsz, seq, False)
        hs = mixers(hs, p_rows, dbsz, dseq, True)[0]
        hp = hp_next
        ckv_l.append(ckv_p.reshape(bsz, seq, MLA_KV_LORA))
        krope_l.append(zm[:p_rows, ZM_KR:ZM_KR + MLA_ROPE].astype(F32).reshape(bsz, seq, MLA_ROPE))
        gla_l.append(st_gla_p.reshape(bsz, 2, GLA_HEADS, GLA_DK, GLA_DV))
        s5_l.append(st_s5_p.reshape(bsz, 2, 2, S5_GROUPS, S5_STATE))

    return (hp.reshape(bsz, seq, D_MODEL), hs.reshape(dbsz, dseq, D_MODEL),
            jnp.stack(ckv_l, axis=1), jnp.stack(krope_l, axis=1),
            jnp.stack(gla_l, axis=1), jnp.stack(s5_l, axis=1))
```
